```python
import jax, jax.numpy as jnp
from jax import lax
import numpy as np

D_MODEL = 1024
BATCH = 8
SEQ = 16384
DEPTH = 4

N_MIXERS = 2
N_ATTN_LAYERS = (DEPTH + 1) // 2
N_GDN_LAYERS = DEPTH // 2
D_FF = 2816
NORM_EPS = 1e-6

ATTN_Q_HEADS = 16
ATTN_KV_HEADS = 4
ATTN_HEAD_DIM = 64
ATTN_GROUP = ATTN_Q_HEADS // ATTN_KV_HEADS
WINDOW = 128
ATTN_BLOCK = 128
ROPE_DIM = ATTN_HEAD_DIM // 4
ROPE_THETA = 500000.0
ATTN_Q_W = ATTN_Q_HEADS * ATTN_HEAD_DIM
ATTN_KV_W = ATTN_KV_HEADS * ATTN_HEAD_DIM
ATTN_IN = ATTN_Q_W + 2 * ATTN_KV_W

GDN_HEADS = 8
GDN_DK = 128
GDN_DV = 128
GDN_CONV = 4
GDN_CHUNK = 64
GDN_QK_W = GDN_HEADS * GDN_DK
GDN_V_W = GDN_HEADS * GDN_DV
GDN_CONV_W = 2 * GDN_QK_W + GDN_V_W
GDN_IN = GDN_CONV_W + GDN_V_W + 2 * GDN_HEADS

kernel_name = "hybrid_swa_sink_gdn_macaron"


def rms_norm(x, w):
    xf = x.astype(jnp.float32)
    y = xf * lax.rsqrt(jnp.mean(xf * xf, axis=-1, keepdims=True) + NORM_EPS)
    return (y * w.astype(jnp.float32)).astype(x.dtype)


def swiglu_ffn(h, w_gate_up, w_down):
    gate, up = jnp.split(h @ w_gate_up, 2, axis=-1)
    return (jax.nn.silu(gate) * up) @ w_down


def partial_rope(t, cos, sin):
    tf = t.astype(jnp.float32)
    half = ROPE_DIM // 2
    r1, r2, rest = tf[..., :half], tf[..., half:ROPE_DIM], tf[..., ROPE_DIM:]
    out = jnp.concatenate([r1 * cos - r2 * sin, r2 * cos + r1 * sin, rest], axis=-1)
    return out.astype(t.dtype)


def swa_sink_attention(h, cos, sin, w_in, b_in, sinks, w_out, b_out):
    B, S, _ = h.shape
    nb = S // ATTN_BLOCK
    qkv = h @ w_in + b_in
    q, k, v = jnp.split(qkv, [ATTN_Q_W, ATTN_Q_W + ATTN_KV_W], axis=-1)
    q = partial_rope(q.reshape(B, S, ATTN_Q_HEADS, ATTN_HEAD_DIM), cos, sin)
    k = partial_rope(k.reshape(B, S, ATTN_KV_HEADS, ATTN_HEAD_DIM), cos, sin)
    v = v.reshape(B, S, ATTN_KV_HEADS, ATTN_HEAD_DIM)
    qb = q.reshape(B, nb, ATTN_BLOCK, ATTN_KV_HEADS, ATTN_GROUP, ATTN_HEAD_DIM)

    def band(t):
        cur = t.reshape(B, nb, ATTN_BLOCK, ATTN_KV_HEADS, ATTN_HEAD_DIM)
        prev = jnp.pad(cur, ((0, 0), (1, 0), (0, 0), (0, 0), (0, 0)))[:, :-1]
        return jnp.concatenate([prev, cur], axis=2)

    kb, vb = band(k), band(v)
    scores = jnp.einsum('bnqhgd,bnkhd->bnhgqk', qb, kb,
                        preferred_element_type=jnp.float32) * (ATTN_HEAD_DIM ** -0.5)
    qi = jnp.arange(ATTN_BLOCK)[:, None]
    kj = jnp.arange(2 * ATTN_BLOCK)[None, :]
    rel = qi + ATTN_BLOCK - kj
    in_window = (rel >= 0) & (rel < WINDOW)
    blk = jnp.arange(nb)[:, None, None]
    valid = in_window[None] & ((blk - 1) * ATTN_BLOCK + kj[None] >= 0)
    scores = jnp.where(valid[None, :, None, None], scores, -jnp.inf)
    sink = sinks.astype(jnp.float32).reshape(ATTN_KV_HEADS, ATTN_GROUP)[None, None, :, :, None, None]
    m = jnp.maximum(jnp.max(scores, axis=-1, keepdims=True), sink)
    p = jnp.exp(scores - m)
    probs = (p / (jnp.sum(p, axis=-1, keepdims=True) + jnp.exp(sink - m))).astype(v.dtype)
    o = jnp.einsum('bnhgqk,bnkhd->bnqhgd', probs, vb).reshape(B, S, ATTN_Q_W)
    return o @ w_out + b_out


def l2_normalize(t):
    return t * lax.rsqrt(jnp.sum(t * t, axis=-1, keepdims=True) + NORM_EPS)


def chunk_gated_delta_rule(q, k, v, g, beta):
    B, S, H, DK = q.shape
    DV = v.shape[-1]
    C = GDN_CHUNK
    N = S // C

    def to_chunks(t):
        return jnp.moveaxis(t.reshape(B, N, C, H, *t.shape[3:]), 3, 1)

    q, k, v, g, beta = (to_chunks(t) for t in (q, k, v, g, beta))
    decay = jnp.cumsum(g, axis=-1)
    causal = jnp.tril(jnp.ones((C, C), dtype=bool))
    strict = jnp.tril(jnp.ones((C, C), dtype=bool), -1)
    L = jnp.exp(jnp.where(causal, decay[..., :, None] - decay[..., None, :], -jnp.inf))
    k_beta = k * beta[..., None]
    A = jnp.where(strict, jnp.einsum('bhncd,bhnsd->bhncs', k_beta, k) * L, 0.0)
    eye = jnp.eye(C, dtype=jnp.float32)
    rhs = jnp.concatenate([v * beta[..., None], k_beta * jnp.exp(decay)[..., None]], axis=-1)
    sol = lax.linalg.triangular_solve(eye + A, rhs, left_side=True, lower=True)
    U, W = sol[..., :DV], sol[..., DV:]
    Aqk = jnp.where(causal, jnp.einsum('bhncd,bhnsd->bhncs', q, k) * L, 0.0)
    decay_last = decay[..., -1:]
    q_dec = q * jnp.exp(decay)[..., None]
    k_dec = k * jnp.exp(decay_last - decay)[..., None]
    chunk_decay = jnp.exp(decay_last[..., 0])
    xs = tuple(jnp.moveaxis(t, 2, 0) for t in (q_dec, k_dec, U, W, Aqk, chunk_decay))

    def step(state, inp):
        qd, kd, u, w, aqk, cd = inp
        v_new = u - jnp.einsum('bhcd,bhde->bhce', w, state)
        o = jnp.einsum('bhcd,bhde->bhce', qd, state) + jnp.einsum('bhcs,bhse->bhce', aqk, v_new)
        state = state * cd[..., None, None] + jnp.einsum('bhcd,bhce->bhde', kd, v_new)
        return state, o

    _, o = lax.scan(step, jnp.zeros((B, H, DK, DV), jnp.float32), xs)
    return jnp.transpose(o, (1, 0, 3, 2, 4)).reshape(B, S, H, DV)


def gated_deltanet(h, w_in, conv_w, A_log, dt_bias, norm_w, w_out):
    B, S, _ = h.shape
    proj = h @ w_in
    qkv, z, b, a = jnp.split(proj, [GDN_CONV_W, GDN_CONV_W + GDN_V_W, GDN_CONV_W + GDN_V_W + GDN_HEADS], axis=-1)
    qkv = lax.conv_general_dilated(qkv, conv_w[:, None, :].astype(qkv.dtype), window_strides=(1,),
                                   padding=[(GDN_CONV - 1, 0)], dimension_numbers=('NWC', 'WIO', 'NWC'),
                                   feature_group_count=GDN_CONV_W)
    qkv = jax.nn.silu(qkv).astype(jnp.float32)
    q, k, v = jnp.split(qkv, [GDN_QK_W, 2 * GDN_QK_W], axis=-1)
    q = l2_normalize(q.reshape(B, S, GDN_HEADS, GDN_DK)) * (GDN_DK ** -0.5)
    k = l2_normalize(k.reshape(B, S, GDN_HEADS, GDN_DK))
    v = v.reshape(B, S, GDN_HEADS, GDN_DV)
    beta = jax.nn.sigmoid(b.astype(jnp.float32))
    g = -jnp.exp(A_log.astype(jnp.float32)) * jax.nn.softplus(a.astype(jnp.float32) + dt_bias.astype(jnp.float32))
    o = chunk_gated_delta_rule(q, k, v, g, beta)
    zf = z.reshape(B, S, GDN_HEADS, GDN_DV).astype(jnp.float32)
    o = o * lax.rsqrt(jnp.mean(o * o, axis=-1, keepdims=True) + NORM_EPS) * norm_w.astype(jnp.float32) * jax.nn.silu(zf)
    return o.astype(h.dtype).reshape(B, S, GDN_V_W) @ w_out


def _fwd_setup_inputs(seed: int = 0) -> dict:
    key = jax.random.key(seed)
    ks = jax.random.split(key, 24)
    f32 = jnp.float32

    def nrm(k, shape, fan_in):
        return jax.random.normal(k, shape, f32) * (fan_in ** -0.5)

    def gain(k, shape):
        return 1.0 + 0.05 * jax.random.normal(k, shape, f32)

    x = jax.random.normal(ks[0], (BATCH, SEQ, D_MODEL), f32)
    positions = (jnp.arange(SEQ, dtype=jnp.int32)[None, :]
                 + jax.random.randint(ks[1], (BATCH, 1), 0, 4096, dtype=jnp.int32))
    dt = jnp.exp(jax.random.uniform(ks[17], (N_GDN_LAYERS, GDN_HEADS), f32, np.log(1e-3), np.log(1e-1)))
    return {
        "x": x,
        "positions": positions,
        "ffn1_norm": gain(ks[2], (DEPTH, D_MODEL)),
        "ffn1_w_gate_up": nrm(ks[3], (DEPTH, D_MODEL, 2 * D_FF), D_MODEL),
        "ffn1_w_down": nrm(ks[4], (DEPTH, D_FF, D_MODEL), D_FF),
        "mix_norm": gain(ks[5], (DEPTH, D_MODEL)),
        "ffn2_norm": gain(ks[6], (DEPTH, D_MODEL)),
        "ffn2_w_gate_up": nrm(ks[7], (DEPTH, D_MODEL, 2 * D_FF), D_MODEL),
        "ffn2_w_down": nrm(ks[8], (DEPTH, D_FF, D_MODEL), D_FF),
        "attn_w_in": nrm(ks[9], (N_ATTN_LAYERS, D_MODEL, ATTN_IN), D_MODEL),
        "attn_b_in": 0.02 * jax.random.normal(ks[10], (N_ATTN_LAYERS, ATTN_IN), f32),
        "attn_sinks": jax.random.normal(ks[11], (N_ATTN_LAYERS, ATTN_Q_HEADS), f32),
        "attn_w_out": nrm(ks[12], (N_ATTN_LAYERS, ATTN_Q_W, D_MODEL), ATTN_Q_W),
        "attn_b_out": 0.02 * jax.random.normal(ks[13], (N_ATTN_LAYERS, D_MODEL), f32),
        "gdn_w_in": nrm(ks[14], (N_GDN_LAYERS, D_MODEL, GDN_IN), D_MODEL),
        "gdn_conv_w": nrm(ks[15], (N_GDN_LAYERS, GDN_CONV, GDN_CONV_W), GDN_CONV),
        "gdn_A_log": jnp.log(jax.random.uniform(ks[16], (N_GDN_LAYERS, GDN_HEADS), f32, 1.0, 16.0)),
        "gdn_dt_bias": dt + jnp.log(-jnp.expm1(-dt)),
        "gdn_norm_w": gain(ks[18], (N_GDN_LAYERS, GDN_DV)),
        "gdn_w_out": nrm(ks[19], (N_GDN_LAYERS, GDN_V_W, D_MODEL), GDN_V_W),
        "final_norm": gain(ks[20], (D_MODEL,)),
    }


def _fwd_reference(x, positions, ffn1_norm, ffn1_w_gate_up, ffn1_w_down, mix_norm, ffn2_norm,
              ffn2_w_gate_up, ffn2_w_down, attn_w_in, attn_b_in, attn_sinks, attn_w_out,
              attn_b_out, gdn_w_in, gdn_conv_w, gdn_A_log, gdn_dt_bias, gdn_norm_w, gdn_w_out,
              final_norm):
    inv_freq = ROPE_THETA ** (-jnp.arange(0, ROPE_DIM, 2, dtype=jnp.float32) / ROPE_DIM)
    ang = positions.astype(jnp.float32)[..., None] * inv_freq
    cos, sin = jnp.cos(ang)[:, :, None, :], jnp.sin(ang)[:, :, None, :]
    h = x
    for layer in range(DEPTH):
        h = h + 0.5 * swiglu_ffn(rms_norm(h, ffn1_norm[layer]), ffn1_w_gate_up[layer], ffn1_w_down[layer])
        hn = rms_norm(h, mix_norm[layer])
        j = layer // N_MIXERS
        if layer % N_MIXERS == 0:
            h = h + swa_sink_attention(hn, cos, sin, attn_w_in[j], attn_b_in[j], attn_sinks[j],
                                       attn_w_out[j], attn_b_out[j])
        else:
            h = h + gated_deltanet(hn, gdn_w_in[j], gdn_conv_w[j], gdn_A_log[j], gdn_dt_bias[j],
                                   gdn_norm_w[j], gdn_w_out[j])
        h = h + 0.5 * swiglu_ffn(rms_norm(h, ffn2_norm[layer]), ffn2_w_gate_up[layer], ffn2_w_down[layer])
    return rms_norm(h, final_norm)


import jax as _jax
import jax.numpy as _jnp

TWIN_FORMAT = 'train_step'
FWD_PARAMS = ['x', 'positions', 'ffn1_norm', 'ffn1_w_gate_up', 'ffn1_w_down', 'mix_norm', 'ffn2_norm', 'ffn2_w_gate_up', 'ffn2_w_down', 'attn_w_in', 'attn_b_in', 'attn_sinks', 'attn_w_out', 'attn_b_out', 'gdn_w_in', 'gdn_conv_w', 'gdn_A_log', 'gdn_dt_bias', 'gdn_norm_w', 'gdn_w_out', 'final_norm']
TWIN_WEIGHTS = ['ffn1_norm', 'ffn1_w_gate_up', 'ffn1_w_down', 'mix_norm', 'ffn2_norm', 'ffn2_w_gate_up', 'ffn2_w_down', 'attn_w_in', 'attn_b_in', 'attn_sinks', 'attn_w_out', 'attn_b_out', 'gdn_w_in', 'gdn_conv_w', 'gdn_A_log', 'gdn_dt_bias', 'gdn_norm_w', 'gdn_w_out', 'final_norm']
TWIN_DIFF_INPUT = 'x'
TWIN_INPUTS = ['x', 'positions', 'ffn1_norm', 'ffn1_w_gate_up', 'ffn1_w_down', 'mix_norm', 'ffn2_norm', 'ffn2_w_gate_up', 'ffn2_w_down', 'attn_w_in', 'attn_b_in', 'attn_sinks', 'attn_w_out', 'attn_b_out', 'gdn_w_in', 'gdn_conv_w', 'gdn_A_log', 'gdn_dt_bias', 'gdn_norm_w', 'gdn_w_out', 'final_norm', 'loss_target', 'm_ffn1_norm', 'm_ffn1_w_gate_up', 'm_ffn1_w_down', 'm_mix_norm', 'm_ffn2_norm', 'm_ffn2_w_gate_up', 'm_ffn2_w_down', 'm_attn_w_in', 'm_attn_b_in', 'm_attn_sinks', 'm_attn_w_out', 'm_attn_b_out', 'm_gdn_w_in', 'm_gdn_conv_w', 'm_gdn_A_log', 'm_gdn_dt_bias', 'm_gdn_norm_w', 'm_gdn_w_out', 'm_final_norm', 'v_ffn1_norm', 'v_ffn1_w_gate_up', 'v_ffn1_w_down', 'v_mix_norm', 'v_ffn2_norm', 'v_ffn2_w_gate_up', 'v_ffn2_w_down', 'v_attn_w_in', 'v_attn_b_in', 'v_attn_sinks', 'v_attn_w_out', 'v_attn_b_out', 'v_gdn_w_in', 'v_gdn_conv_w', 'v_gdn_A_log', 'v_gdn_dt_bias', 'v_gdn_norm_w', 'v_gdn_w_out', 'v_final_norm']
TWIN_OUTPUTS = ['loss', 'grad_x', 'grad_ffn1_norm', 'grad_ffn1_w_gate_up', 'grad_ffn1_w_down', 'grad_mix_norm', 'grad_ffn2_norm', 'grad_ffn2_w_gate_up', 'grad_ffn2_w_down', 'grad_attn_w_in', 'grad_attn_b_in', 'grad_attn_sinks', 'grad_attn_w_out', 'grad_attn_b_out', 'grad_gdn_w_in', 'grad_gdn_conv_w', 'grad_gdn_A_log', 'grad_gdn_dt_bias', 'grad_gdn_norm_w', 'grad_gdn_w_out', 'grad_final_norm', 'delta_ffn1_norm', 'delta_ffn1_w_gate_up', 'delta_ffn1_w_down', 'delta_mix_norm', 'delta_ffn2_norm', 'delta_ffn2_w_gate_up', 'delta_ffn2_w_down', 'delta_attn_w_in', 'delta_attn_b_in', 'delta_attn_sinks', 'delta_attn_w_out', 'delta_attn_b_out', 'delta_gdn_w_in', 'delta_gdn_conv_w', 'delta_gdn_A_log', 'delta_gdn_dt_bias', 'delta_gdn_norm_w', 'delta_gdn_w_out', 'delta_final_norm', 'new_m_ffn1_norm', 'new_m_ffn1_w_gate_up', 'new_m_ffn1_w_down', 'new_m_mix_norm', 'new_m_ffn2_norm', 'new_m_ffn2_w_gate_up', 'new_m_ffn2_w_down', 'new_m_attn_w_in', 'new_m_attn_b_in', 'new_m_attn_sinks', 'new_m_attn_w_out', 'new_m_attn_b_out', 'new_m_gdn_w_in', 'new_m_gdn_conv_w', 'new_m_gdn_A_log', 'new_m_gdn_dt_bias', 'new_m_gdn_norm_w', 'new_m_gdn_w_out', 'new_m_final_norm', 'new_v_ffn1_norm', 'new_v_ffn1_w_gate_up', 'new_v_ffn1_w_down', 'new_v_mix_norm', 'new_v_ffn2_norm', 'new_v_ffn2_w_gate_up', 'new_v_ffn2_w_down', 'new_v_attn_w_in', 'new_v_attn_b_in', 'new_v_attn_sinks', 'new_v_attn_w_out', 'new_v_attn_b_out', 'new_v_gdn_w_in', 'new_v_gdn_conv_w', 'new_v_gdn_A_log', 'new_v_gdn_dt_bias', 'new_v_gdn_norm_w', 'new_v_gdn_w_out', 'new_v_final_norm']
TWIN_LEAF_KINDS = {'loss': 'loss', 'grad_x': 'grad_x', 'grad_ffn1_norm': 'grad_w', 'grad_ffn1_w_gate_up': 'grad_w', 'grad_ffn1_w_down': 'grad_w', 'grad_mix_norm': 'grad_w', 'grad_ffn2_norm': 'grad_w', 'grad_ffn2_w_gate_up': 'grad_w', 'grad_ffn2_w_down': 'grad_w', 'grad_attn_w_in': 'grad_w', 'grad_attn_b_in': 'grad_w', 'grad_attn_sinks': 'grad_w', 'grad_attn_w_out': 'grad_w', 'grad_attn_b_out': 'grad_w', 'grad_gdn_w_in': 'grad_w', 'grad_gdn_conv_w': 'grad_w', 'grad_gdn_A_log': 'grad_w', 'grad_gdn_dt_bias': 'grad_w', 'grad_gdn_norm_w': 'grad_w', 'grad_gdn_w_out': 'grad_w', 'grad_final_norm': 'grad_w', 'delta_ffn1_norm': 'delta_w', 'delta_ffn1_w_gate_up': 'delta_w', 'delta_ffn1_w_down': 'delta_w', 'delta_mix_norm': 'delta_w', 'delta_ffn2_norm': 'delta_w', 'delta_ffn2_w_gate_up': 'delta_w', 'delta_ffn2_w_down': 'delta_w', 'delta_attn_w_in': 'delta_w', 'delta_attn_b_in': 'delta_w', 'delta_attn_sinks': 'delta_w', 'delta_attn_w_out': 'delta_w', 'delta_attn_b_out': 'delta_w', 'delta_gdn_w_in': 'delta_w', 'delta_gdn_conv_w': 'delta_w', 'delta_gdn_A_log': 'delta_w', 'delta_gdn_dt_bias': 'delta_w', 'delta_gdn_norm_w': 'delta_w', 'delta_gdn_w_out': 'delta_w', 'delta_final_norm': 'delta_w', 'new_m_ffn1_norm': 'new_m', 'new_m_ffn1_w_gate_up': 'new_m', 'new_m_ffn1_w_down': 'new_m', 'new_m_mix_norm': 'new_m', 'new_m_ffn2_norm': 'new_m', 'new_m_ffn2_w_gate_up': 'new_m', 'new_m_ffn2_w_down': 'new_m', 'new_m_attn_w_in': 'new_m', 'new_m_attn_b_in': 'new_m', 'new_m_attn_sinks': 'new_m', 'new_m_attn_w_out': 'new_m', 'new_m_attn_b_out': 'new_m', 'new_m_gdn_w_in': 'new_m', 'new_m_gdn_conv_w': 'new_m', 'new_m_gdn_A_log': 'new_m', 'new_m_gdn_dt_bias': 'new_m', 'new_m_gdn_norm_w': 'new_m', 'new_m_gdn_w_out': 'new_m', 'new_m_final_norm': 'new_m', 'new_v_ffn1_norm': 'new_v', 'new_v_ffn1_w_gate_up': 'new_v', 'new_v_ffn1_w_down': 'new_v', 'new_v_mix_norm': 'new_v', 'new_v_ffn2_norm': 'new_v', 'new_v_ffn2_w_gate_up': 'new_v', 'new_v_ffn2_w_down': 'new_v', 'new_v_attn_w_in': 'new_v', 'new_v_attn_b_in': 'new_v', 'new_v_attn_sinks': 'new_v', 'new_v_attn_w_out': 'new_v', 'new_v_attn_b_out': 'new_v', 'new_v_gdn_w_in': 'new_v', 'new_v_gdn_conv_w': 'new_v', 'new_v_gdn_A_log': 'new_v', 'new_v_gdn_dt_bias': 'new_v', 'new_v_gdn_norm_w': 'new_v', 'new_v_gdn_w_out': 'new_v', 'new_v_final_norm': 'new_v'}


def _forward(args):
    return _fwd_reference(*[args[k] for k in FWD_PARAMS])


def _output_shape():
    def fwd():
        inp = _fwd_setup_inputs(0)
        return _fwd_reference(*[inp[k] for k in FWD_PARAMS])
    out = _jax.eval_shape(fwd)
    return out.shape, out.dtype

N_MICROBATCH = 1
ADAM_LR = 0.001
ADAM_B1 = 0.9
ADAM_B2 = 0.999
ADAM_EPS = 1e-08
ADAM_WD = 0.01
ADAM_STEP = 10
PER_EXAMPLE_BATCH_AXIS = {'x': 0, 'positions': 0, 'loss_target': 0}
SHARED_INPUTS = []
_WEIGHT_DTYPES = {'ffn1_norm': _jnp.float32, 'ffn1_w_gate_up': _jnp.float32, 'ffn1_w_down': _jnp.float32, 'mix_norm': _jnp.float32, 'ffn2_norm': _jnp.float32, 'ffn2_w_gate_up': _jnp.float32, 'ffn2_w_down': _jnp.float32, 'attn_w_in': _jnp.float32, 'attn_b_in': _jnp.float32, 'attn_sinks': _jnp.float32, 'attn_w_out': _jnp.float32, 'attn_b_out': _jnp.float32, 'gdn_w_in': _jnp.float32, 'gdn_conv_w': _jnp.float32, 'gdn_A_log': _jnp.float32, 'gdn_dt_bias': _jnp.float32, 'gdn_norm_w': _jnp.float32, 'gdn_w_out': _jnp.float32, 'final_norm': _jnp.float32}
MOMENT_SCALE = {'ffn1_norm': 1.853037e-01, 'ffn1_w_gate_up': 7.867997e-02, 'ffn1_w_down': 1.286935e-01, 'mix_norm': 2.357823e-01, 'ffn2_norm': 1.545039e-01, 'ffn2_w_gate_up': 6.692569e-02, 'ffn2_w_down': 1.096750e-01, 'attn_w_in': 1.241596e-01, 'attn_b_in': 6.293410e-01, 'attn_sinks': 9.067582e-02, 'attn_w_out': 1.029578e-01, 'attn_b_out': 7.876097e-01, 'gdn_w_in': 1.508567e-01, 'gdn_conv_w': 1.419955e-01, 'gdn_A_log': 1.746778e+00, 'gdn_dt_bias': 1.618057e+00, 'gdn_norm_w': 6.005083e-01, 'gdn_w_out': 1.950868e-01, 'final_norm': 1.284647e+02}


def _to_microbatches(a, axis):
    t = _jnp.moveaxis(a, axis, 0)
    t = t.reshape((N_MICROBATCH, t.shape[0] // N_MICROBATCH) + t.shape[1:])
    return _jnp.moveaxis(t, 1, axis + 1)


def setup_inputs(seed: int = 0) -> dict:
    inp = _fwd_setup_inputs(seed)
    key = _jax.random.fold_in(_jax.random.key(seed), 7919)
    shape, _ = _output_shape()
    out = dict(inp)
    out["loss_target"] = _jax.random.normal(_jax.random.fold_in(key, 0), shape, _jnp.float32)
    for i, name in enumerate(TWIN_WEIGHTS):
        w = inp[name].astype(_jnp.float32)
        if MOMENT_SCALE is None:
            s = _jnp.sqrt(_jnp.mean(_jnp.square(w)) + 1e-30)
        else:
            s = MOMENT_SCALE[name]
        km, kv = _jax.random.split(_jax.random.fold_in(key, i + 1))
        out[name] = w
        out["m_" + name] = s * _jax.random.normal(km, w.shape, _jnp.float32)
        out["v_" + name] = (s * s) * _jax.random.uniform(kv, w.shape, _jnp.float32, 0.5, 1.5)
    if N_MICROBATCH > 1:
        for name, axis in PER_EXAMPLE_BATCH_AXIS.items():
            out[name] = _to_microbatches(out[name], axis)
    return {'x': out['x'], 'positions': out['positions'], 'ffn1_norm': out['ffn1_norm'], 'ffn1_w_gate_up': out['ffn1_w_gate_up'], 'ffn1_w_down': out['ffn1_w_down'], 'mix_norm': out['mix_norm'], 'ffn2_norm': out['ffn2_norm'], 'ffn2_w_gate_up': out['ffn2_w_gate_up'], 'ffn2_w_down': out['ffn2_w_down'], 'attn_w_in': out['attn_w_in'], 'attn_b_in': out['attn_b_in'], 'attn_sinks': out['attn_sinks'], 'attn_w_out': out['attn_w_out'], 'attn_b_out': out['attn_b_out'], 'gdn_w_in': out['gdn_w_in'], 'gdn_conv_w': out['gdn_conv_w'], 'gdn_A_log': out['gdn_A_log'], 'gdn_dt_bias': out['gdn_dt_bias'], 'gdn_norm_w': out['gdn_norm_w'], 'gdn_w_out': out['gdn_w_out'], 'final_norm': out['final_norm'], 'loss_target': out['loss_target'], 'm_ffn1_norm': out['m_ffn1_norm'], 'm_ffn1_w_gate_up': out['m_ffn1_w_gate_up'], 'm_ffn1_w_down': out['m_ffn1_w_down'], 'm_mix_norm': out['m_mix_norm'], 'm_ffn2_norm': out['m_ffn2_norm'], 'm_ffn2_w_gate_up': out['m_ffn2_w_gate_up'], 'm_ffn2_w_down': out['m_ffn2_w_down'], 'm_attn_w_in': out['m_attn_w_in'], 'm_attn_b_in': out['m_attn_b_in'], 'm_attn_sinks': out['m_attn_sinks'], 'm_attn_w_out': out['m_attn_w_out'], 'm_attn_b_out': out['m_attn_b_out'], 'm_gdn_w_in': out['m_gdn_w_in'], 'm_gdn_conv_w': out['m_gdn_conv_w'], 'm_gdn_A_log': out['m_gdn_A_log'], 'm_gdn_dt_bias': out['m_gdn_dt_bias'], 'm_gdn_norm_w': out['m_gdn_norm_w'], 'm_gdn_w_out': out['m_gdn_w_out'], 'm_final_norm': out['m_final_norm'], 'v_ffn1_norm': out['v_ffn1_norm'], 'v_ffn1_w_gate_up': out['v_ffn1_w_gate_up'], 'v_ffn1_w_down': out['v_ffn1_w_down'], 'v_mix_norm': out['v_mix_norm'], 'v_ffn2_norm': out['v_ffn2_norm'], 'v_ffn2_w_gate_up': out['v_ffn2_w_gate_up'], 'v_ffn2_w_down': out['v_ffn2_w_down'], 'v_attn_w_in': out['v_attn_w_in'], 'v_attn_b_in': out['v_attn_b_in'], 'v_attn_sinks': out['v_attn_sinks'], 'v_attn_w_out': out['v_attn_w_out'], 'v_attn_b_out': out['v_attn_b_out'], 'v_gdn_w_in': out['v_gdn_w_in'], 'v_gdn_conv_w': out['v_gdn_conv_w'], 'v_gdn_A_log': out['v_gdn_A_log'], 'v_gdn_dt_bias': out['v_gdn_dt_bias'], 'v_gdn_norm_w': out['v_gdn_norm_w'], 'v_gdn_w_out': out['v_gdn_w_out'], 'v_final_norm': out['v_final_norm']}


def _loss(weights, diff, rest, loss_target):
    with _jax.named_scope("forward"):
        args = {**rest, TWIN_DIFF_INPUT: diff, **{k: w.astype(_WEIGHT_DTYPES[k]) for k, w in weights.items()}}
        y = _forward(args)
    with _jax.named_scope("loss_head"):
        err = _jnp.square(y.astype(_jnp.float32) - loss_target)
        return 0.5 * _jnp.sum(_jnp.mean(err, axis=-1)) if err.ndim else 0.5 * err


def _adamw(w, g, m, v):
    m = ADAM_B1 * m + (1.0 - ADAM_B1) * g
    v = ADAM_B2 * v + (1.0 - ADAM_B2) * _jnp.square(g)
    m_hat = m / (1.0 - ADAM_B1 ** ADAM_STEP)
    v_hat = v / (1.0 - ADAM_B2 ** ADAM_STEP)
    delta = -ADAM_LR * (m_hat / (_jnp.sqrt(v_hat) + ADAM_EPS) + ADAM_WD * w)
    return delta, m, v


def reference(x, positions, ffn1_norm, ffn1_w_gate_up, ffn1_w_down, mix_norm, ffn2_norm, ffn2_w_gate_up, ffn2_w_down, attn_w_in, attn_b_in, attn_sinks, attn_w_out, attn_b_out, gdn_w_in, gdn_conv_w, gdn_A_log, gdn_dt_bias, gdn_norm_w, gdn_w_out, final_norm, loss_target, m_ffn1_norm, m_ffn1_w_gate_up, m_ffn1_w_down, m_mix_norm, m_ffn2_norm, m_ffn2_w_gate_up, m_ffn2_w_down, m_attn_w_in, m_attn_b_in, m_attn_sinks, m_attn_w_out, m_attn_b_out, m_gdn_w_in, m_gdn_conv_w, m_gdn_A_log, m_gdn_dt_bias, m_gdn_norm_w, m_gdn_w_out, m_final_norm, v_ffn1_norm, v_ffn1_w_gate_up, v_ffn1_w_down, v_mix_norm, v_ffn2_norm, v_ffn2_w_gate_up, v_ffn2_w_down, v_attn_w_in, v_attn_b_in, v_attn_sinks, v_attn_w_out, v_attn_b_out, v_gdn_w_in, v_gdn_conv_w, v_gdn_A_log, v_gdn_dt_bias, v_gdn_norm_w, v_gdn_w_out, v_final_norm):
    given = dict(x=x, positions=positions, ffn1_norm=ffn1_norm, ffn1_w_gate_up=ffn1_w_gate_up, ffn1_w_down=ffn1_w_down, mix_norm=mix_norm, ffn2_norm=ffn2_norm, ffn2_w_gate_up=ffn2_w_gate_up, ffn2_w_down=ffn2_w_down, attn_w_in=attn_w_in, attn_b_in=attn_b_in, attn_sinks=attn_sinks, attn_w_out=attn_w_out, attn_b_out=attn_b_out, gdn_w_in=gdn_w_in, gdn_conv_w=gdn_conv_w, gdn_A_log=gdn_A_log, gdn_dt_bias=gdn_dt_bias, gdn_norm_w=gdn_norm_w, gdn_w_out=gdn_w_out, final_norm=final_norm, loss_target=loss_target, m_ffn1_norm=m_ffn1_norm, m_ffn1_w_gate_up=m_ffn1_w_gate_up, m_ffn1_w_down=m_ffn1_w_down, m_mix_norm=m_mix_norm, m_ffn2_norm=m_ffn2_norm, m_ffn2_w_gate_up=m_ffn2_w_gate_up, m_ffn2_w_down=m_ffn2_w_down, m_attn_w_in=m_attn_w_in, m_attn_b_in=m_attn_b_in, m_attn_sinks=m_attn_sinks, m_attn_w_out=m_attn_w_out, m_attn_b_out=m_attn_b_out, m_gdn_w_in=m_gdn_w_in, m_gdn_conv_w=m_gdn_conv_w, m_gdn_A_log=m_gdn_A_log, m_gdn_dt_bias=m_gdn_dt_bias, m_gdn_norm_w=m_gdn_norm_w, m_gdn_w_out=m_gdn_w_out, m_final_norm=m_final_norm, v_ffn1_norm=v_ffn1_norm, v_ffn1_w_gate_up=v_ffn1_w_gate_up, v_ffn1_w_down=v_ffn1_w_down, v_mix_norm=v_mix_norm, v_ffn2_norm=v_ffn2_norm, v_ffn2_w_gate_up=v_ffn2_w_gate_up, v_ffn2_w_down=v_ffn2_w_down, v_attn_w_in=v_attn_w_in, v_attn_b_in=v_attn_b_in, v_attn_sinks=v_attn_sinks, v_attn_w_out=v_attn_w_out, v_attn_b_out=v_attn_b_out, v_gdn_w_in=v_gdn_w_in, v_gdn_conv_w=v_gdn_conv_w, v_gdn_A_log=v_gdn_A_log, v_gdn_dt_bias=v_gdn_dt_bias, v_gdn_norm_w=v_gdn_norm_w, v_gdn_w_out=v_gdn_w_out, v_final_norm=v_final_norm)
    weights = {n: given[n] for n in TWIN_WEIGHTS}
    shared = {n: given[n] for n in SHARED_INPUTS}
    per_example = {n: given[n] for n in ['x', 'positions']}
    grad_fn = _jax.value_and_grad(_loss, argnums=(0, 1))

    def one_microbatch(ex, loss_target):
        ex = dict(ex)
        diff = ex.pop(TWIN_DIFF_INPUT)
        return grad_fn(weights, diff, {**shared, **ex}, loss_target)

    if N_MICROBATCH == 1:
        loss, (grad_w, grad_x) = one_microbatch(per_example, given["loss_target"])
    else:
        def body(carry, xs):
            loss_sum, grad_sum = carry
            l_k, (gw_k, gx_k) = one_microbatch(xs[0], xs[1])
            with _jax.named_scope("update"):
                return (loss_sum + l_k, _jax.tree.map(_jnp.add, grad_sum, gw_k)), gx_k

        init = (_jnp.zeros((), _jnp.float32), _jax.tree.map(_jnp.zeros_like, weights))
        (loss, grad_w), grad_x = _jax.lax.scan(body, init, (per_example, given["loss_target"]))
    with _jax.named_scope("update"):
        delta_w, new_m, new_v = {}, {}, {}
        for n in TWIN_WEIGHTS:
            delta_w[n], new_m[n], new_v[n] = _adamw(weights[n], grad_w[n], given["m_" + n], given["v_" + n])
    return (loss, grad_x, *[grad_w[n] for n in TWIN_WEIGHTS], *[delta_w[n] for n in TWIN_WEIGHTS],
            *[new_m[n] for n in TWIN_WEIGHTS], *[new_v[n] for n in TWIN_WEIGHTS])
```

```python
import functools

import jax
import jax.numpy as jnp
from jax import lax
from jax.experimental import pallas as pl
from jax.experimental.pallas import tpu as pltpu

F32 = jnp.float32
BF16 = jnp.bfloat16

D_MODEL = 1024
DEPTH = 4
D_FF = 2816
NORM_EPS = 1e-6
N_DEV = 8

ATTN_Q_HEADS = 16
ATTN_KV_HEADS = 4
ATTN_HEAD_DIM = 64
ATTN_GROUP = 4
ATTN_BLOCK = 128
ROPE_DIM = 16
ROPE_THETA = 500000.0
ATTN_Q_W = 1024
ATTN_KV_W = 256
ATTN_IN = 1536
ATTN_SCALE = ATTN_HEAD_DIM ** -0.5

GDN_HEADS = 8
GDN_DK = 128
GDN_CONV = 4
GDN_CHUNK = 64
GDN_CONV_W = 3072
GDN_IN = 4112
GDN_QSCALE = GDN_DK ** -0.5
GDN_ROWS = 256
GDN_SUB_SHIFT = 4

ADAM_LR = 0.001
ADAM_B1 = 0.9
ADAM_B2 = 0.999
ADAM_EPS = 1e-08
ADAM_WD = 0.01
ADAM_STEP = 10

LANES = 128
NEG_BIG = -1e30
VMEM_LIMIT_BYTES = 56 * 1024 * 1024
HIGH = lax.Precision.HIGH
HIGHEST = lax.Precision.HIGHEST
MESH_ID = pl.DeviceIdType.MESH


def _params(n_axes):
    return pltpu.CompilerParams(dimension_semantics=("arbitrary",) * n_axes,
                                vmem_limit_bytes=VMEM_LIMIT_BYTES)


def _nn(a, b):
    return jnp.dot(a, b, preferred_element_type=F32)


def _nt(a, b):
    return lax.dot_general(a, b, (((1,), (1,)), ((), ())), preferred_element_type=F32)


def _tn(a, b):
    return lax.dot_general(a, b, (((0,), (0,)), ((), ())), preferred_element_type=F32)


def _bnn(a, b, precision=None):
    return lax.dot_general(a, b, (((2,), (1,)), ((0,), (0,))), precision=precision,
                           preferred_element_type=F32)


def _bnt(a, b, precision=None):
    return lax.dot_general(a, b, (((2,), (2,)), ((0,), (0,))), precision=precision,
                           preferred_element_type=F32)


def _sigmoid(x):
    return 1.0 / (1.0 + jnp.exp(-x))


def _rms_stats(x):
    r = lax.rsqrt(jnp.mean(x * x, axis=-1, keepdims=True) + NORM_EPS)
    return r, x * r


def _norm_bwd(x, nw, dhn):
    r, xhat = _rms_stats(x)
    dxh = dhn * nw
    dx = r * (dxh - xhat * jnp.mean(dxh * xhat, axis=-1, keepdims=True))
    dnw = jnp.sum(dhn * xhat, axis=0, keepdims=True)
    return dx, dnw


def _row_tile(t, pref):
    return min(t, pref)


FFN_TM = 512
FFN_BWD_TM = 256
FFN_TF = 1408


def _ffn_fwd(h, nw, wg, wu, wd):
    t = h.shape[0]
    tm, tf = _row_tile(t, FFN_TM), FFN_TF
    nj = D_FF // tf

    def body(h_ref, nw_ref, wg_ref, wu_ref, wd_ref, out_ref, g_ref, u_ref, hn_ref, acc_ref):
        j = pl.program_id(1)

        @pl.when(j == 0)
        def _():
            _, xhat = _rms_stats(h_ref[...])
            hn_ref[...] = (xhat * nw_ref[...]).astype(BF16)
            acc_ref[...] = jnp.zeros_like(acc_ref)

        hn = hn_ref[...]
        g = _nn(hn, wg_ref[...])
        u = _nn(hn, wu_ref[...])
        g_ref[...] = g.astype(BF16)
        u_ref[...] = u.astype(BF16)
        a = (g * _sigmoid(g) * u).astype(BF16)
        acc_ref[...] += _nn(a, wd_ref[...])

        @pl.when(j == nj - 1)
        def _():
            out_ref[...] = h_ref[...] + 0.5 * acc_ref[...]

    return pl.pallas_call(
        body, name="ffn_fwd", grid=(t // tm, nj),
        in_specs=[pl.BlockSpec((tm, D_MODEL), lambda i, j: (i, 0)),
                  pl.BlockSpec((1, D_MODEL), lambda i, j: (0, 0)),
                  pl.BlockSpec((D_MODEL, tf), lambda i, j: (0, j)),
                  pl.BlockSpec((D_MODEL, tf), lambda i, j: (0, j)),
                  pl.BlockSpec((tf, D_MODEL), lambda i, j: (j, 0))],
        out_specs=[pl.BlockSpec((tm, D_MODEL), lambda i, j: (i, 0)),
                   pl.BlockSpec((tm, tf), lambda i, j: (i, j)),
                   pl.BlockSpec((tm, tf), lambda i, j: (i, j)),
                   pl.BlockSpec((tm, D_MODEL), lambda i, j: (i, 0))],
        out_shape=[jax.ShapeDtypeStruct((t, D_MODEL), F32),
                   jax.ShapeDtypeStruct((t, D_FF), BF16),
                   jax.ShapeDtypeStruct((t, D_FF), BF16),
                   jax.ShapeDtypeStruct((t, D_MODEL), BF16)],
        scratch_shapes=[pltpu.VMEM((tm, D_MODEL), F32)],
        compiler_params=_params(2),
    )(h, nw, wg, wu, wd)


def _ffn_bwd(dy, h, nw, g, u, wg, wu, wd):
    t = h.shape[0]
    tm, tf = _row_tile(t, FFN_BWD_TM), FFN_TF
    nj = D_FF // tf

    def body(dy_ref, h_ref, nw_ref, g_ref, u_ref, wg_ref, wu_ref, wd_ref,
             dh_ref, dg_ref, du_ref, a_ref, dz_ref, dnw_ref, acc_ref):
        i, j = pl.program_id(0), pl.program_id(1)

        @pl.when(j == 0)
        def _():
            dz_ref[...] = (0.5 * dy_ref[...]).astype(BF16)
            acc_ref[...] = jnp.zeros_like(acc_ref)

        @pl.when(jnp.logical_and(i == 0, j == 0))
        def _():
            dnw_ref[...] = jnp.zeros_like(dnw_ref)

        da = _nt(dz_ref[...], wd_ref[...])
        gv = g_ref[...].astype(F32)
        uv = u_ref[...].astype(F32)
        sig = _sigmoid(gv)
        silu = gv * sig
        dg = (da * uv * (sig * (1.0 + gv * (1.0 - sig)))).astype(BF16)
        du = (da * silu).astype(BF16)
        dg_ref[...] = dg
        du_ref[...] = du
        a_ref[...] = (silu * uv).astype(BF16)
        acc_ref[...] += _nt(dg, wg_ref[...]) + _nt(du, wu_ref[...])

        @pl.when(j == nj - 1)
        def _():
            dx, dnw = _norm_bwd(h_ref[...], nw_ref[...], acc_ref[...])
            dh_ref[...] = dy_ref[...] + dx
            dnw_ref[...] += dnw

    return pl.pallas_call(
        body, name="ffn_bwd", grid=(t // tm, nj),
        in_specs=[pl.BlockSpec((tm, D_MODEL), lambda i, j: (i, 0)),
                  pl.BlockSpec((tm, D_MODEL), lambda i, j: (i, 0)),
                  pl.BlockSpec((1, D_MODEL), lambda i, j: (0, 0)),
                  pl.BlockSpec((tm, tf), lambda i, j: (i, j)),
                  pl.BlockSpec((tm, tf), lambda i, j: (i, j)),
                  pl.BlockSpec((D_MODEL, tf), lambda i, j: (0, j)),
                  pl.BlockSpec((D_MODEL, tf), lambda i, j: (0, j)),
                  pl.BlockSpec((tf, D_MODEL), lambda i, j: (j, 0))],
        out_specs=[pl.BlockSpec((tm, D_MODEL), lambda i, j: (i, 0)),
                   pl.BlockSpec((tm, tf), lambda i, j: (i, j)),
                   pl.BlockSpec((tm, tf), lambda i, j: (i, j)),
                   pl.BlockSpec((tm, tf), lambda i, j: (i, j)),
                   pl.BlockSpec((tm, D_MODEL), lambda i, j: (i, 0)),
                   pl.BlockSpec((1, D_MODEL), lambda i, j: (0, 0))],
        out_shape=[jax.ShapeDtypeStruct((t, D_MODEL), F32),
                   jax.ShapeDtypeStruct((t, D_FF), BF16),
                   jax.ShapeDtypeStruct((t, D_FF), BF16),
                   jax.ShapeDtypeStruct((t, D_FF), BF16),
                   jax.ShapeDtypeStruct((t, D_MODEL), BF16),
                   jax.ShapeDtypeStruct((1, D_MODEL), F32)],
        scratch_shapes=[pltpu.VMEM((tm, D_MODEL), F32)],
        compiler_params=_params(2),
    )(dy, h, nw, g, u, wg, wu, wd)


def _matmul_tn(a, b, tn, name):
    k, m = a.shape
    n = b.shape[1]
    tm = min(m, 1408 if m % 1408 == 0 else 1024)
    tk = min(k, 1024)

    def body(a_ref, b_ref, o_ref):
        @pl.when(pl.program_id(2) == 0)
        def _():
            o_ref[...] = jnp.zeros_like(o_ref)

        o_ref[...] += _tn(a_ref[...], b_ref[...])

    return pl.pallas_call(
        body, name=name, grid=(m // tm, n // tn, k // tk),
        in_specs=[pl.BlockSpec((tk, tm), lambda i, j, kk: (kk, i)),
                  pl.BlockSpec((tk, tn), lambda i, j, kk: (kk, j))],
        out_specs=pl.BlockSpec((tm, tn), lambda i, j, kk: (i, j)),
        out_shape=jax.ShapeDtypeStruct((m, n), F32),
        compiler_params=_params(3),
    )(a, b)


ATTN_TM = 512


def _rope(t, c, s1, s2):
    return t * c + pltpu.roll(t, LANES - ROPE_DIM // 2, 1) * s1 + pltpu.roll(t, ROPE_DIM // 2, 1) * s2


def _rope_bwd(d, c, s1, s2):
    return d * c + pltpu.roll(d * s1, ROPE_DIM // 2, 1) + pltpu.roll(d * s2, LANES - ROPE_DIM // 2, 1)


def _attn_qkv_fwd(h, nw, w_in, b_in, rc, rs1, rs2):
    t = h.shape[0]
    tm = _row_tile(t, ATTN_TM)

    def body(h_ref, nw_ref, w_ref, b_ref, c_ref, s1_ref, s2_ref, q_ref, k_ref, v_ref, hn_ref):
        _, xhat = _rms_stats(h_ref[...])
        hn = (xhat * nw_ref[...]).astype(BF16)
        hn_ref[...] = hn
        qkv = _nn(hn, w_ref[...]) + b_ref[...]
        c, s1, s2 = c_ref[...], s1_ref[...], s2_ref[...]
        for s in range(ATTN_Q_W // LANES):
            q_ref[:, s * LANES:(s + 1) * LANES] = _rope(qkv[:, s * LANES:(s + 1) * LANES], c, s1, s2).astype(BF16)
        for s in range(ATTN_KV_W // LANES):
            lo = ATTN_Q_W + s * LANES
            k_ref[:, s * LANES:(s + 1) * LANES] = _rope(qkv[:, lo:lo + LANES], c, s1, s2).astype(BF16)
        v_ref[...] = qkv[:, ATTN_Q_W + ATTN_KV_W:].astype(BF16)

    row = lambda w: pl.BlockSpec((tm, w), lambda i: (i, 0))
    full = lambda a: pl.BlockSpec(a.shape, lambda i: (0, 0))
    return pl.pallas_call(
        body, name="attn_qkv_fwd", grid=(t // tm,),
        in_specs=[row(D_MODEL), full(nw), full(w_in), full(b_in), row(LANES), row(LANES), row(LANES)],
        out_specs=[row(ATTN_Q_W), row(ATTN_KV_W), row(ATTN_KV_W), row(D_MODEL)],
        out_shape=[jax.ShapeDtypeStruct((t, ATTN_Q_W), BF16),
                   jax.ShapeDtypeStruct((t, ATTN_KV_W), BF16),
                   jax.ShapeDtypeStruct((t, ATTN_KV_W), BF16),
                   jax.ShapeDtypeStruct((t, D_MODEL), BF16)],
        compiler_params=_params(1),
    )(h, nw, w_in, b_in, rc, rs1, rs2)


def _attn_group(q_ref, kc_ref, kp_ref, vc_ref, vp_ref, sinks_ref, hk, n):
    hd = ATTN_HEAD_DIM
    cols = slice(hk * hd, (hk + 1) * hd)
    kcat = jnp.concatenate([kp_ref[:, cols], kc_ref[:, cols]], axis=0)
    vcat = jnp.concatenate([vp_ref[:, cols], vc_ref[:, cols]], axis=0)
    heads = [hk * ATTN_GROUP + g for g in range(ATTN_GROUP)]
    qs = jnp.concatenate([q_ref[:, hq * hd:(hq + 1) * hd] for hq in heads], axis=0)
    s = _nt(qs, kcat) * ATTN_SCALE
    rows = ATTN_GROUP * ATTN_BLOCK
    ri = lax.broadcasted_iota(jnp.int32, (rows, 2 * ATTN_BLOCK), 0) & (ATTN_BLOCK - 1)
    cj = lax.broadcasted_iota(jnp.int32, (rows, 2 * ATTN_BLOCK), 1)
    first = jnp.where(n > 0, 0, 2 * ATTN_BLOCK)
    valid = jnp.logical_or(jnp.logical_and(cj < ATTN_BLOCK, cj > ri + first),
                           jnp.logical_and(cj >= ATTN_BLOCK, cj - ATTN_BLOCK <= ri))
    s = jnp.where(valid, s, NEG_BIG)
    sink = jnp.concatenate(
        [jnp.broadcast_to(sinks_ref[:, hq:hq + 1], (ATTN_BLOCK, 1)) for hq in heads], axis=0)
    m = jnp.maximum(jnp.max(s, axis=-1, keepdims=True), sink)
    p = jnp.exp(s - m)
    esink = jnp.exp(sink - m)
    den = jnp.sum(p, axis=-1, keepdims=True) + esink
    probs = p / den
    return heads, qs, kcat, vcat, probs, esink / den


def _attn_core_fwd(q, k, v, sinks):
    t = q.shape[0]
    nb = t // ATTN_BLOCK

    def body(q_ref, kc_ref, kp_ref, vc_ref, vp_ref, sinks_ref, o_ref):
        n = pl.program_id(0)
        for hk in range(ATTN_KV_HEADS):
            heads, _, _, vcat, probs, _ = _attn_group(q_ref, kc_ref, kp_ref, vc_ref, vp_ref, sinks_ref, hk, n)
            o = _nn(probs.astype(BF16), vcat)
            for g, hq in enumerate(heads):
                o_ref[:, hq * ATTN_HEAD_DIM:(hq + 1) * ATTN_HEAD_DIM] = (
                    o[g * ATTN_BLOCK:(g + 1) * ATTN_BLOCK].astype(BF16))

    cur = lambda w: pl.BlockSpec((ATTN_BLOCK, w), lambda n: (n, 0))
    prev = lambda w: pl.BlockSpec((ATTN_BLOCK, w), lambda n: (jnp.maximum(n - 1, 0), 0))
    return pl.pallas_call(
        body, name="attn_core_fwd", grid=(nb,),
        in_specs=[cur(ATTN_Q_W), cur(ATTN_KV_W), prev(ATTN_KV_W), cur(ATTN_KV_W), prev(ATTN_KV_W),
                  pl.BlockSpec(sinks.shape, lambda n: (0, 0))],
        out_specs=cur(ATTN_Q_W),
        out_shape=jax.ShapeDtypeStruct((t, ATTN_Q_W), BF16),
        compiler_params=_params(1),
    )(q, k, k, v, v, sinks)


def _attn_core_bwd(q, k, v, do, sinks):
    t = q.shape[0]
    nb = t // ATTN_BLOCK
    hd = ATTN_HEAD_DIM

    def body(q_ref, kc_ref, kp_ref, vc_ref, vp_ref, do_ref, sinks_ref,
             dq_ref, dk_ref, dv_ref, dsink_ref, ck_ref, cv_ref):
        n = pl.program_id(0)

        @pl.when(n == 0)
        def _():
            dsink_ref[...] = jnp.zeros_like(dsink_ref)
            ck_ref[...] = jnp.zeros_like(ck_ref)
            cv_ref[...] = jnp.zeros_like(cv_ref)

        @pl.when(n == nb)
        def _():
            dk_ref[...] = ck_ref[...]
            dv_ref[...] = cv_ref[...]

        @pl.when(n < nb)
        def _():
            for hk in range(ATTN_KV_HEADS):
                heads, qs, kcat, vcat, probs, psink = _attn_group(
                    q_ref, kc_ref, kp_ref, vc_ref, vp_ref, sinks_ref, hk, n)
                dos = jnp.concatenate([do_ref[:, hq * hd:(hq + 1) * hd] for hq in heads], axis=0)
                dp = _nt(dos, vcat)
                delta = jnp.sum(probs * dp, axis=-1, keepdims=True)
                ds = (probs * (dp - delta)).astype(BF16)
                dsk = -(psink * delta)
                dqs = _nn(ds, kcat) * ATTN_SCALE
                dkc = _tn(ds, qs) * ATTN_SCALE
                dvc = _tn(probs.astype(BF16), dos)
                for g, hq in enumerate(heads):
                    blk = slice(g * ATTN_BLOCK, (g + 1) * ATTN_BLOCK)
                    dq_ref[:, hq * hd:(hq + 1) * hd] = dqs[blk]
                    dsink_ref[hq:hq + 1, :] += jnp.broadcast_to(
                        jnp.sum(dsk[blk], axis=0, keepdims=True), (1, LANES))
                cols = slice(hk * hd, (hk + 1) * hd)
                dk_ref[:, cols] = ck_ref[:, cols] + dkc[:ATTN_BLOCK]
                dv_ref[:, cols] = cv_ref[:, cols] + dvc[:ATTN_BLOCK]
                ck_ref[:, cols] = dkc[ATTN_BLOCK:]
                cv_ref[:, cols] = dvc[ATTN_BLOCK:]

    cur = lambda w: pl.BlockSpec((ATTN_BLOCK, w), lambda n: (jnp.minimum(n, nb - 1), 0))
    prev = lambda w: pl.BlockSpec((ATTN_BLOCK, w), lambda n: (jnp.clip(n - 1, 0, nb - 1), 0))
    return pl.pallas_call(
        body, name="attn_core_bwd", grid=(nb + 1,),
        in_specs=[cur(ATTN_Q_W), cur(ATTN_KV_W), prev(ATTN_KV_W), cur(ATTN_KV_W), prev(ATTN_KV_W),
                  cur(ATTN_Q_W), pl.BlockSpec(sinks.shape, lambda n: (0, 0))],
        out_specs=[cur(ATTN_Q_W), prev(ATTN_KV_W), prev(ATTN_KV_W),
                   pl.BlockSpec((ATTN_Q_HEADS, LANES), lambda n: (0, 0))],
        out_shape=[jax.ShapeDtypeStruct((t, ATTN_Q_W), F32),
                   jax.ShapeDtypeStruct((t, ATTN_KV_W), F32),
                   jax.ShapeDtypeStruct((t, ATTN_KV_W), F32),
                   jax.ShapeDtypeStruct((ATTN_Q_HEADS, LANES), F32)],
        scratch_shapes=[pltpu.VMEM((ATTN_BLOCK, ATTN_KV_W), F32),
                        pltpu.VMEM((ATTN_BLOCK, ATTN_KV_W), F32)],
        compiler_params=_params(1),
    )(q, k, k, v, v, do, sinks)


def _proj_out_fwd(x, w, b, res):
    t = x.shape[0]
    tm = _row_tile(t, 512)

    def body(x_ref, w_ref, b_ref, r_ref, o_ref):
        o_ref[...] = r_ref[...] + _nn(x_ref[...], w_ref[...]) + b_ref[...]

    row = pl.BlockSpec((tm, D_MODEL), lambda i: (i, 0))
    return pl.pallas_call(
        body, name="proj_out_fwd", grid=(t // tm,),
        in_specs=[row, pl.BlockSpec(w.shape, lambda i: (0, 0)), pl.BlockSpec(b.shape, lambda i: (0, 0)), row],
        out_specs=row,
        out_shape=jax.ShapeDtypeStruct((t, D_MODEL), F32),
        compiler_params=_params(1),
    )(x, w, b, res)


def _proj_out_bwd(dy, w):
    t = dy.shape[0]
    tm = _row_tile(t, 512)

    def body(dy_ref, w_ref, dx_ref, dyb_ref, db_ref):
        @pl.when(pl.program_id(0) == 0)
        def _():
            db_ref[...] = jnp.zeros_like(db_ref)

        dy_v = dy_ref[...]
        dyb = dy_v.astype(BF16)
        dyb_ref[...] = dyb
        dx_ref[...] = _nt(dyb, w_ref[...]).astype(BF16)
        db_ref[...] += jnp.sum(dy_v, axis=0, keepdims=True)

    row = pl.BlockSpec((tm, D_MODEL), lambda i: (i, 0))
    return pl.pallas_call(
        body, name="proj_out_bwd", grid=(t // tm,),
        in_specs=[row, pl.BlockSpec(w.shape, lambda i: (0, 0))],
        out_specs=[row, row, pl.BlockSpec((1, D_MODEL), lambda i: (0, 0))],
        out_shape=[jax.ShapeDtypeStruct((t, D_MODEL), BF16),
                   jax.ShapeDtypeStruct((t, D_MODEL), BF16),
                   jax.ShapeDtypeStruct((1, D_MODEL), F32)],
        compiler_params=_params(1),
    )(dy, w)


def _attn_qkv_bwd(dy, h, nw, w_in, dq, dk, dv, rc, rs1, rs2):
    t = h.shape[0]
    tm = _row_tile(t, ATTN_TM)

    def body(dy_ref, h_ref, nw_ref, w_ref, dq_ref, dk_ref, dv_ref, c_ref, s1_ref, s2_ref,
             dh_ref, dqkv_ref, db_ref, dnw_ref, tmp_ref):
        @pl.when(pl.program_id(0) == 0)
        def _():
            db_ref[...] = jnp.zeros_like(db_ref)
            dnw_ref[...] = jnp.zeros_like(dnw_ref)

        c, s1, s2 = c_ref[...], s1_ref[...], s2_ref[...]
        for s in range(ATTN_Q_W // LANES):
            tmp_ref[:, s * LANES:(s + 1) * LANES] = _rope_bwd(dq_ref[:, s * LANES:(s + 1) * LANES], c, s1, s2)
        for s in range(ATTN_KV_W // LANES):
            lo = ATTN_Q_W + s * LANES
            tmp_ref[:, lo:lo + LANES] = _rope_bwd(dk_ref[:, s * LANES:(s + 1) * LANES], c, s1, s2)
        tmp_ref[:, ATTN_Q_W + ATTN_KV_W:] = dv_ref[...]
        dqkv = tmp_ref[...]
        db_ref[...] += jnp.sum(dqkv, axis=0, keepdims=True)
        dqkv_b = dqkv.astype(BF16)
        dqkv_ref[...] = dqkv_b
        dx, dnw = _norm_bwd(h_ref[...], nw_ref[...], _nt(dqkv_b, w_ref[...]))
        dh_ref[...] = dy_ref[...] + dx
        dnw_ref[...] += dnw

    row = lambda w: pl.BlockSpec((tm, w), lambda i: (i, 0))
    full = lambda a: pl.BlockSpec(a.shape, lambda i: (0, 0))
    return pl.pallas_call(
        body, name="attn_qkv_bwd", grid=(t // tm,),
        in_specs=[row(D_MODEL), row(D_MODEL), full(nw), full(w_in), row(ATTN_Q_W), row(ATTN_KV_W),
                  row(ATTN_KV_W), row(LANES), row(LANES), row(LANES)],
        out_specs=[row(D_MODEL), row(ATTN_IN), pl.BlockSpec((1, ATTN_IN), lambda i: (0, 0)),
                   pl.BlockSpec((1, D_MODEL), lambda i: (0, 0))],
        out_shape=[jax.ShapeDtypeStruct((t, D_MODEL), F32),
                   jax.ShapeDtypeStruct((t, ATTN_IN), BF16),
                   jax.ShapeDtypeStruct((1, ATTN_IN), F32),
                   jax.ShapeDtypeStruct((1, D_MODEL), F32)],
        scratch_shapes=[pltpu.VMEM((tm, ATTN_IN), F32)],
        compiler_params=_params(1),
    )(dy, h, nw, w_in, dq, dk, dv, rc, rs1, rs2)


GDN_TM = 256
GDN_CONV_TM = 128


def _gdn_proj_fwd(h, nw, w_qkv, w_z, w_ba):
    t = h.shape[0]
    tm = _row_tile(t, GDN_TM)

    def body(h_ref, nw_ref, wq_ref, wz_ref, wb_ref, x_ref, z_ref, ba_ref, hn_ref):
        _, xhat = _rms_stats(h_ref[...])
        hn = (xhat * nw_ref[...]).astype(BF16)
        hn_ref[...] = hn
        x_ref[...] = _nn(hn, wq_ref[...])
        z_ref[...] = _nn(hn, wz_ref[...])
        ba_ref[...] = _nn(hn, wb_ref[...])

    row = lambda w: pl.BlockSpec((tm, w), lambda i: (i, 0))
    full = lambda a: pl.BlockSpec(a.shape, lambda i: (0, 0))
    return pl.pallas_call(
        body, name="gdn_proj_fwd", grid=(t // tm,),
        in_specs=[row(D_MODEL), full(nw), full(w_qkv), full(w_z), full(w_ba)],
        out_specs=[row(GDN_CONV_W), row(D_MODEL), row(LANES), row(D_MODEL)],
        out_shape=[jax.ShapeDtypeStruct((t, GDN_CONV_W), F32),
                   jax.ShapeDtypeStruct((t, D_MODEL), F32),
                   jax.ShapeDtypeStruct((t, LANES), F32),
                   jax.ShapeDtypeStruct((t, D_MODEL), BF16)],
        compiler_params=_params(1),
    )(h, nw, w_qkv, w_z, w_ba)


def _softplus(x):
    return jnp.maximum(x, 0.0) + jnp.log(1.0 + jnp.exp(-jnp.abs(x)))


def _conv_taps(x, halo, cw):
    tm = x.shape[0]
    xx = jnp.concatenate([halo, x], axis=0)
    taps = [xx[8 - (GDN_CONV - 1) + i: 8 - (GDN_CONV - 1) + i + tm] for i in range(GDN_CONV)]
    c = taps[0] * cw[0:1]
    for i in range(1, GDN_CONV):
        c = c + taps[i] * cw[i:i + 1]
    return c, taps


def _gates(ba, gp):
    lane = lax.broadcasted_iota(jnp.int32, ba.shape, 1)
    beta = _sigmoid(ba)
    pre = ba + gp[1:2]
    g = -jnp.exp(gp[0:1]) * _softplus(pre)
    gates = jnp.where(lane < GDN_HEADS, beta, jnp.where(lane < 2 * GDN_HEADS, g, 0.0))
    return lane, beta, pre, g, gates


def _gdn_conv_fwd(x, cw, ba, gp):
    t = x.shape[0]
    tm = _row_tile(t, GDN_CONV_TM)

    def body(x_ref, halo_ref, cw_ref, ba_ref, gp_ref, q_ref, k_ref, v_ref, gates_ref):
        halo = jnp.where(pl.program_id(0) > 0, halo_ref[...], 0.0)
        c, _ = _conv_taps(x_ref[...], halo, cw_ref[...])
        s = c * _sigmoid(c)
        for hh in range(GDN_HEADS):
            sq = s[:, hh * LANES:(hh + 1) * LANES]
            q_ref[:, hh * LANES:(hh + 1) * LANES] = (
                sq * lax.rsqrt(jnp.sum(sq * sq, axis=-1, keepdims=True) + NORM_EPS) * GDN_QSCALE)
            sk = s[:, D_MODEL + hh * LANES:D_MODEL + (hh + 1) * LANES]
            k_ref[:, hh * LANES:(hh + 1) * LANES] = (
                sk * lax.rsqrt(jnp.sum(sk * sk, axis=-1, keepdims=True) + NORM_EPS))
        v_ref[...] = s[:, 2 * D_MODEL:]
        gates_ref[...] = _gates(ba_ref[...], gp_ref[...])[4]

    row = lambda w: pl.BlockSpec((tm, w), lambda i: (i, 0))
    full = lambda a: pl.BlockSpec(a.shape, lambda i: (0, 0))
    halo = pl.BlockSpec((8, GDN_CONV_W), lambda i: (jnp.maximum(i * (tm // 8) - 1, 0), 0))
    return pl.pallas_call(
        body, name="gdn_conv_fwd", grid=(t // tm,),
        in_specs=[row(GDN_CONV_W), halo, full(cw), row(LANES), full(gp)],
        out_specs=[row(D_MODEL), row(D_MODEL), row(D_MODEL), row(LANES)],
        out_shape=[jax.ShapeDtypeStruct((t, D_MODEL), F32)] * 3 + [jax.ShapeDtypeStruct((t, LANES), F32)],
        compiler_params=_params(1),
    )(x, x, cw, ba, gp)


def _unit_inverse(nmat):
    nb, c, _ = nmat.shape
    ri = lax.broadcasted_iota(jnp.int32, (nb, c, c), 1)
    ci = lax.broadcasted_iota(jnp.int32, (nb, c, c), 2)
    eye = jnp.where(ri == ci, 1.0, 0.0).astype(F32)
    same = (ri >> GDN_SUB_SHIFT) == (ci >> GDN_SUB_SHIFT)
    nd = jnp.where(same, nmat, 0.0)
    no = nmat - nd
    mm = functools.partial(_bnn, precision=HIGH)
    n2 = mm(nd, nd)
    n4 = mm(n2, n2)
    n8 = mm(n4, n4)
    td = mm(mm(mm(eye - nd, eye + n2), eye + n4), eye + n8)
    bm = mm(td, no)
    b2 = mm(bm, bm)
    return mm(mm(eye - bm, eye + b2), td)


def _chunk_terms(q_ref, k_ref, gates_ref, h, transposed):
    rows = k_ref.shape[0]
    nb = rows // GDN_CHUNK
    c = GDN_CHUNK
    lane = lax.broadcasted_iota(jnp.int32, (rows, LANES), 1)
    gt = gates_ref[...]
    beta = jnp.sum(jnp.where(lane == h, gt, 0.0), axis=-1, keepdims=True).reshape(nb, c, 1)
    g = jnp.sum(jnp.where(lane == GDN_HEADS + h, gt, 0.0), axis=-1, keepdims=True).reshape(nb, c, 1)
    ri = lax.broadcasted_iota(jnp.int32, (nb, c, c), 1)
    ci = lax.broadcasted_iota(jnp.int32, (nb, c, c), 2)
    tril = jnp.where(ri >= ci, 1.0, 0.0).astype(F32)
    gamma = _bnn(tril, jnp.broadcast_to(g, (nb, c, LANES)), precision=HIGHEST)
    l3 = lax.broadcasted_iota(jnp.int32, (nb, c, LANES), 2)
    pmat = jnp.where(l3 == 0, gamma, jnp.where(l3 == 1, 1.0, 0.0))
    qmat = jnp.where(l3 == 0, 1.0, jnp.where(l3 == 1, -gamma, 0.0))
    k = k_ref[...].reshape(nb, c, LANES)
    q = q_ref[...].reshape(nb, c, LANES)
    kb = k * beta
    kbf, kbb, qb = k.astype(BF16), kb.astype(BF16), q.astype(BF16)
    out = dict(beta=beta, g=g, gamma=gamma, k=k, q=q, kb=kb, kbf=kbf, kbb=kbb, qb=qb, ri=ri, ci=ci)
    diff = _bnt(pmat, qmat, precision=HIGHEST)
    lmat = jnp.exp(jnp.where(ri >= ci, diff, NEG_BIG))
    out["L"] = lmat
    out["A"] = jnp.where(ri > ci, _bnt(kbb, kbf) * lmat, 0.0)
    out["Aqk"] = jnp.where(ri >= ci, _bnt(qb, kbf) * lmat, 0.0)
    if transposed:
        difft = _bnt(qmat, pmat, precision=HIGHEST)
        lt = jnp.exp(jnp.where(ci >= ri, difft, NEG_BIG))
        out["LT"] = lt
        out["AT"] = jnp.where(ci > ri, _bnt(kbf, kbb) * lt, 0.0)
        out["AqkT"] = jnp.where(ci >= ri, _bnt(kbf, qb) * lt, 0.0)
    return out


def _gdn_intra_fwd(q, k, v, gates):
    t = q.shape[0]
    rows = _row_tile(t, GDN_ROWS)
    nb = rows // GDN_CHUNK
    nchunks = t // GDN_CHUNK

    def body(q_ref, k_ref, v_ref, gates_ref, u_ref, w_ref, qd_ref, kd_ref, aqk_ref, cd_ref):
        h = pl.program_id(1)
        tm_ = _chunk_terms(q_ref, k_ref, gates_ref, h, False)
        gamma, beta = tm_["gamma"], tm_["beta"]
        eg = jnp.exp(gamma)
        tinv = _unit_inverse(tm_["A"])
        v3 = v_ref[...].reshape(nb, GDN_CHUNK, LANES)
        rhs = jnp.concatenate([v3 * beta, tm_["kb"] * eg], axis=-1)
        sol = _bnn(tinv, rhs, precision=HIGH)
        u_ref[...] = sol[..., :LANES].reshape(rows, LANES)
        w_ref[...] = sol[..., LANES:].reshape(rows, LANES).astype(BF16)
        gl = gamma[:, GDN_CHUNK - 1:GDN_CHUNK, :]
        qd_ref[...] = (tm_["q"] * eg).reshape(rows, LANES).astype(BF16)
        kd_ref[...] = (tm_["k"] * jnp.exp(gl - gamma)).reshape(rows, LANES).astype(BF16)
        aqk_ref[0] = tm_["Aqk"].reshape(rows, GDN_CHUNK).astype(BF16)
        cd_ref[0] = jnp.broadcast_to(jnp.exp(gl), (nb, 8, LANES)).reshape(nb * 8, LANES)

    blk = pl.BlockSpec((rows, LANES), lambda i, h: (i, h))
    return pl.pallas_call(
        body, name="gdn_intra_fwd", grid=(t // rows, GDN_HEADS),
        in_specs=[blk, blk, blk, pl.BlockSpec((rows, LANES), lambda i, h: (i, 0))],
        out_specs=[blk, blk, blk, blk,
                   pl.BlockSpec((1, rows, GDN_CHUNK), lambda i, h: (h, i, 0)),
                   pl.BlockSpec((1, nb * 8, LANES), lambda i, h: (h, i, 0))],
        out_shape=[jax.ShapeDtypeStruct((t, D_MODEL), F32),
                   jax.ShapeDtypeStruct((t, D_MODEL), BF16),
                   jax.ShapeDtypeStruct((t, D_MODEL), BF16),
                   jax.ShapeDtypeStruct((t, D_MODEL), BF16),
                   jax.ShapeDtypeStruct((GDN_HEADS, t, GDN_CHUNK), BF16),
                   jax.ShapeDtypeStruct((GDN_HEADS, nchunks * 8, LANES), F32)],
        compiler_params=_params(2),
    )(q, k, v, gates)


def _gdn_scan_fwd(u, w, qd, kd, aqk, cd):
    t = u.shape[0]
    rows = _row_tile(t, GDN_ROWS)
    nb = rows // GDN_CHUNK
    nchunks = t // GDN_CHUNK

    def body(u_ref, w_ref, qd_ref, kd_ref, aqk_ref, cd_ref, o_ref, st_ref, s_ref):
        @pl.when(pl.program_id(1) == 0)
        def _():
            s_ref[...] = jnp.zeros_like(s_ref)

        for c in range(nb):
            r = slice(c * GDN_CHUNK, (c + 1) * GDN_CHUNK)
            s = s_ref[...]
            st_ref[0, c * LANES:(c + 1) * LANES, :] = s
            sb = s.astype(BF16)
            vb = (u_ref[r, :] - _nn(w_ref[r, :], sb)).astype(BF16)
            o_ref[r, :] = _nn(qd_ref[r, :], sb) + _nn(aqk_ref[0, r, :], vb)
            s_ref[...] = s * cd_ref[0, c * 8:c * 8 + 1, :] + _tn(kd_ref[r, :], vb)

    blk = pl.BlockSpec((rows, LANES), lambda h, i: (i, h))
    return pl.pallas_call(
        body, name="gdn_scan_fwd", grid=(GDN_HEADS, t // rows),
        in_specs=[blk, blk, blk, blk,
                  pl.BlockSpec((1, rows, GDN_CHUNK), lambda h, i: (h, i, 0)),
                  pl.BlockSpec((1, nb * 8, LANES), lambda h, i: (h, i, 0))],
        out_specs=[blk, pl.BlockSpec((1, nb * LANES, LANES), lambda h, i: (h, i, 0))],
        out_shape=[jax.ShapeDtypeStruct((t, D_MODEL), F32),
                   jax.ShapeDtypeStruct((GDN_HEADS, nchunks * LANES, LANES), F32)],
        scratch_shapes=[pltpu.VMEM((LANES, LANES), F32)],
        compiler_params=_params(2),
    )(u, w, qd, kd, aqk, cd)


def _gdn_out_fwd(o, z, nw, w_out, res):
    t = o.shape[0]
    tm = _row_tile(t, GDN_TM)

    def body(o_ref, z_ref, nw_ref, w_ref, r_ref, out_ref, gated_ref):
        nwv = nw_ref[...]
        for hh in range(GDN_HEADS):
            sl = slice(hh * LANES, (hh + 1) * LANES)
            _, on = _rms_stats(o_ref[:, sl])
            zv = z_ref[:, sl]
            gated_ref[:, sl] = (on * nwv * (zv * _sigmoid(zv))).astype(BF16)
        out_ref[...] = r_ref[...] + _nn(gated_ref[...], w_ref[...])

    row = pl.BlockSpec((tm, D_MODEL), lambda i: (i, 0))
    full = lambda a: pl.BlockSpec(a.shape, lambda i: (0, 0))
    return pl.pallas_call(
        body, name="gdn_out_fwd", grid=(t // tm,),
        in_specs=[row, row, full(nw), full(w_out), row],
        out_specs=[row, row],
        out_shape=[jax.ShapeDtypeStruct((t, D_MODEL), F32), jax.ShapeDtypeStruct((t, D_MODEL), BF16)],
        compiler_params=_params(1),
    )(o, z, nw, w_out, res)


def _gdn_out_bwd(dy, w_out, o, z, nw):
    t = o.shape[0]
    tm = _row_tile(t, GDN_TM)

    def body(dy_ref, w_ref, o_ref, z_ref, nw_ref, do_ref, dz_ref, dyb_ref, dnw_ref, dgt_ref):
        @pl.when(pl.program_id(0) == 0)
        def _():
            dnw_ref[...] = jnp.zeros_like(dnw_ref)

        dyb = dy_ref[...].astype(BF16)
        dyb_ref[...] = dyb
        dgt_ref[...] = _nt(dyb, w_ref[...])
        nwv = nw_ref[...]
        for hh in range(GDN_HEADS):
            sl = slice(hh * LANES, (hh + 1) * LANES)
            r, on = _rms_stats(o_ref[:, sl])
            zv = z_ref[:, sl]
            sig = _sigmoid(zv)
            sz = zv * sig
            dgt = dgt_ref[:, sl]
            d_on = dgt * nwv * sz
            dz_ref[:, sl] = (dgt * on * nwv * (sig * (1.0 + zv * (1.0 - sig)))).astype(BF16)
            dnw_ref[...] += jnp.sum(dgt * on * sz, axis=0, keepdims=True)
            do_ref[:, sl] = (r * (d_on - on * jnp.mean(d_on * on, axis=-1, keepdims=True))).astype(BF16)

    row = pl.BlockSpec((tm, D_MODEL), lambda i: (i, 0))
    full = lambda a: pl.BlockSpec(a.shape, lambda i: (0, 0))
    return pl.pallas_call(
        body, name="gdn_out_bwd", grid=(t // tm,),
        in_specs=[row, full(w_out), row, row, full(nw)],
        out_specs=[row, row, row, pl.BlockSpec((1, LANES), lambda i: (0, 0))],
        out_shape=[jax.ShapeDtypeStruct((t, D_MODEL), BF16)] * 3 + [jax.ShapeDtypeStruct((1, LANES), F32)],
        scratch_shapes=[pltpu.VMEM((tm, D_MODEL), F32)],
        compiler_params=_params(1),
    )(dy, w_out, o, z, nw)


def _gdn_scan_bwd(u, w, qd, kd, aqk, cd, st, do):
    t = u.shape[0]
    rows = _row_tile(t, GDN_ROWS)
    nb = rows // GDN_CHUNK
    nchunks = t // GDN_CHUNK
    nsteps = t // rows
    cc = GDN_CHUNK

    def body(u_ref, w_ref, qd_ref, kd_ref, aqk_ref, cd_ref, st_ref, do_ref,
             du_ref, dw_ref, dqd_ref, dkd_ref, daqk_ref, daqkt_ref, dcd_ref, ds_ref):
        @pl.when(pl.program_id(1) == 0)
        def _():
            ds_ref[...] = jnp.zeros_like(ds_ref)

        ri = lax.broadcasted_iota(jnp.int32, (cc, cc), 0)
        ci = lax.broadcasted_iota(jnp.int32, (cc, cc), 1)
        for c in reversed(range(nb)):
            r = slice(c * cc, (c + 1) * cc)
            s = st_ref[0, c * LANES:(c + 1) * LANES, :]
            sb = s.astype(BF16)
            dsn = ds_ref[...]
            dsb = dsn.astype(BF16)
            wv, kdv, qdv, aq, dov = w_ref[r, :], kd_ref[r, :], qd_ref[r, :], aqk_ref[0, r, :], do_ref[r, :]
            vb = (u_ref[r, :] - _nn(wv, sb)).astype(BF16)
            dv = _tn(aq, dov) + _nn(kdv, dsb)
            dvb = dv.astype(BF16)
            daqk_ref[0, r, :] = jnp.where(ri >= ci, _nt(dov, vb), 0.0)
            daqkt_ref[0, r, :] = jnp.where(ci >= ri, _nt(vb, dov), 0.0)
            dqd_ref[r, :] = _nt(dov, sb)
            dkd_ref[r, :] = _nt(vb, dsb)
            dcd_ref[0, c * 8:(c + 1) * 8, :] = jnp.broadcast_to(jnp.sum(s * dsn), (8, LANES))
            du_ref[r, :] = dv
            dw_ref[r, :] = -_nt(dvb, sb)
            ds_ref[...] = _tn(qdv, dov) + dsn * cd_ref[0, c * 8:c * 8 + 1, :] - _tn(wv, dvb)

    rev = lambda i: nsteps - 1 - i
    blk = pl.BlockSpec((rows, LANES), lambda h, i: (rev(i), h))
    sq = pl.BlockSpec((1, rows, cc), lambda h, i: (h, rev(i), 0))
    cdb = pl.BlockSpec((1, nb * 8, LANES), lambda h, i: (h, rev(i), 0))
    return pl.pallas_call(
        body, name="gdn_scan_bwd", grid=(GDN_HEADS, nsteps),
        in_specs=[blk, blk, blk, blk, sq, cdb,
                  pl.BlockSpec((1, nb * LANES, LANES), lambda h, i: (h, rev(i), 0)), blk],
        out_specs=[blk, blk, blk, blk, sq, sq, cdb],
        out_shape=[jax.ShapeDtypeStruct((t, D_MODEL), F32)] * 4
        + [jax.ShapeDtypeStruct((GDN_HEADS, t, cc), F32)] * 2
        + [jax.ShapeDtypeStruct((GDN_HEADS, nchunks * 8, LANES), F32)],
        scratch_shapes=[pltpu.VMEM((LANES, LANES), F32)],
        compiler_params=_params(2),
    )(u, w, qd, kd, aqk, cd, st, do)


def _gdn_intra_bwd(q, k, v, gates, u, w, du, dw, dqd, dkd, daqk, daqkt, dcd):
    t = q.shape[0]
    rows = _row_tile(t, GDN_ROWS)
    nb = rows // GDN_CHUNK
    cc = GDN_CHUNK

    def body(q_ref, k_ref, v_ref, gates_ref, u_ref, w_ref, du_ref, dw_ref, dqd_ref, dkd_ref,
             daqk_ref, daqkt_ref, dcd_ref, dq_ref, dk_ref, dv_ref, dgates_ref):
        h = pl.program_id(1)

        @pl.when(h == 0)
        def _():
            dgates_ref[...] = jnp.zeros_like(dgates_ref)

        tm_ = _chunk_terms(q_ref, k_ref, gates_ref, h, True)
        gamma, beta, kk, qq, kb = tm_["gamma"], tm_["beta"], tm_["k"], tm_["q"], tm_["kb"]
        kbf, kbb, qb = tm_["kbf"], tm_["kbb"], tm_["qb"]
        lmat, lt = tm_["L"], tm_["LT"]
        ri, ci = tm_["ri"], tm_["ci"]
        r3 = lambda ref: ref[...].reshape(nb, cc, LANES)
        eg = jnp.exp(gamma)
        gl = gamma[:, cc - 1:cc, :]
        ekd = jnp.exp(gl - gamma)
        v3 = r3(v_ref)
        tt = _unit_inverse(tm_["AT"])
        dsol = jnp.concatenate([r3(du_ref), r3(dw_ref)], axis=-1)
        sol = jnp.concatenate([r3(u_ref), r3(w_ref).astype(F32)], axis=-1)
        dx = _bnn(tt, dsol, precision=HIGH)
        da = jnp.where(ri > ci, -_bnt(dx, sol, precision=HIGH), 0.0)
        dat = jnp.where(ci > ri, -_bnt(sol, dx, precision=HIGH), 0.0)
        dxu, dxw = dx[..., :LANES], dx[..., LANES:]
        dv_ref[...] = (dxu * beta).reshape(rows, LANES)
        dbeta = jnp.sum(dxu * v3, axis=-1, keepdims=True)
        dkb = dxw * eg
        dgam = jnp.sum(dxw * kb * eg, axis=-1, keepdims=True)
        dkb = dkb + _bnn((da * lmat).astype(BF16), kbf)
        dk = _bnn((dat * lt).astype(BF16), kbb)
        dgam = dgam + jnp.sum(da * tm_["A"], axis=-1, keepdims=True) - jnp.sum(dat * tm_["AT"], axis=-1, keepdims=True)
        daq = daqk_ref[0].reshape(nb, cc, cc)
        daqt = daqkt_ref[0].reshape(nb, cc, cc)
        dq = _bnn((daq * lmat).astype(BF16), kbf)
        dk = dk + _bnn((daqt * lt).astype(BF16), qb)
        dgam = dgam + jnp.sum(daq * tm_["Aqk"], axis=-1, keepdims=True) - jnp.sum(daqt * tm_["AqkT"], axis=-1, keepdims=True)
        dqd3, dkd3 = r3(dqd_ref), r3(dkd_ref)
        dq = dq + dqd3 * eg
        dgam = dgam + jnp.sum(dqd3 * qq * eg, axis=-1, keepdims=True)
        dk = dk + dkd3 * ekd
        tk = jnp.sum(dkd3 * kk * ekd, axis=-1, keepdims=True)
        dgam = dgam - tk
        dcdv = dcd_ref[0].reshape(nb, 8, LANES)[:, 0:1, 0:1]
        dglast = jnp.sum(tk, axis=1, keepdims=True) + dcdv * jnp.exp(gl[:, :, 0:1])
        rowi = lax.broadcasted_iota(jnp.int32, (nb, cc, 1), 1)
        dgam = dgam + jnp.where(rowi == cc - 1, dglast, 0.0)
        dk = dk + dkb * beta
        dbeta = dbeta + jnp.sum(dkb * kk, axis=-1, keepdims=True)
        triu = jnp.where(ci >= ri, 1.0, 0.0).astype(F32)
        dg = _bnn(triu, jnp.broadcast_to(dgam, (nb, cc, LANES)), precision=HIGHEST)
        dq_ref[...] = dq.reshape(rows, LANES)
        dk_ref[...] = dk.reshape(rows, LANES)
        lane = lax.broadcasted_iota(jnp.int32, (rows, LANES), 1)
        dgates_ref[...] += (jnp.where(lane == h, dbeta.reshape(rows, 1), 0.0)
                            + jnp.where(lane == GDN_HEADS + h, dg.reshape(rows, LANES), 0.0))

    blk = pl.BlockSpec((rows, LANES), lambda i, h: (i, h))
    shared = pl.BlockSpec((rows, LANES), lambda i, h: (i, 0))
    sq = pl.BlockSpec((1, rows, cc), lambda i, h: (h, i, 0))
    return pl.pallas_call(
        body, name="gdn_intra_bwd", grid=(t // rows, GDN_HEADS),
        in_specs=[blk, blk, blk, shared, blk, blk, blk, blk, blk, blk, sq, sq,
                  pl.BlockSpec((1, nb * 8, LANES), lambda i, h: (h, i, 0))],
        out_specs=[blk, blk, blk, shared],
        out_shape=[jax.ShapeDtypeStruct((t, D_MODEL), F32)] * 3 + [jax.ShapeDtypeStruct((t, LANES), F32)],
        compiler_params=_params(2),
    )(q, k, v, gates, u, w, du, dw, dqd, dkd, daqk, daqkt, dcd)


def _gdn_conv_bwd_a(x, cw, ba, gp, dq, dk, dv, dgates):
    t = x.shape[0]
    tm = _row_tile(t, GDN_CONV_TM)

    def body(x_ref, halo_ref, cw_ref, ba_ref, gp_ref, dq_ref, dk_ref, dv_ref, dgates_ref,
             dc_ref, dba_ref, dcw_ref, dgp_ref, ds_ref):
        @pl.when(pl.program_id(0) == 0)
        def _():
            dcw_ref[...] = jnp.zeros_like(dcw_ref)
            dgp_ref[...] = jnp.zeros_like(dgp_ref)

        halo = jnp.where(pl.program_id(0) > 0, halo_ref[...], 0.0)
        c, taps = _conv_taps(x_ref[...], halo, cw_ref[...])
        sig = _sigmoid(c)
        s = c * sig
        for hh in range(GDN_HEADS):
            sl = slice(hh * LANES, (hh + 1) * LANES)
            sq = s[:, sl]
            rq = lax.rsqrt(jnp.sum(sq * sq, axis=-1, keepdims=True) + NORM_EPS)
            qh = sq * rq
            dqv = dq_ref[:, sl]
            ds_ref[:, sl] = GDN_QSCALE * rq * (dqv - qh * jnp.sum(dqv * qh, axis=-1, keepdims=True))
            sl2 = slice(D_MODEL + hh * LANES, D_MODEL + (hh + 1) * LANES)
            sk = s[:, sl2]
            rk = lax.rsqrt(jnp.sum(sk * sk, axis=-1, keepdims=True) + NORM_EPS)
            kh = sk * rk
            dkv = dk_ref[:, sl]
            ds_ref[:, sl2] = rk * (dkv - kh * jnp.sum(dkv * kh, axis=-1, keepdims=True))
        ds_ref[:, 2 * D_MODEL:] = dv_ref[...]
        dc = ds_ref[...] * (sig * (1.0 + c * (1.0 - sig)))
        dc_ref[...] = dc
        for i in range(GDN_CONV):
            dcw_ref[i:i + 1, :] += jnp.sum(dc * taps[i], axis=0, keepdims=True)
        lane, beta, pre, g, _ = _gates(ba_ref[...], gp_ref[...])
        dgt = dgates_ref[...]
        db = dgt * beta * (1.0 - beta)
        dpre = dgt * (-jnp.exp(gp_ref[0:1, :])) * _sigmoid(pre)
        isa = jnp.logical_and(lane >= GDN_HEADS, lane < 2 * GDN_HEADS)
        dba_ref[...] = jnp.where(lane < GDN_HEADS, db, jnp.where(isa, dpre, 0.0)).astype(BF16)
        dgp_ref[0:1, :] += jnp.sum(jnp.where(isa, dgt * g, 0.0), axis=0, keepdims=True)
        dgp_ref[1:2, :] += jnp.sum(jnp.where(isa, dpre, 0.0), axis=0, keepdims=True)

    row = lambda w: pl.BlockSpec((tm, w), lambda i: (i, 0))
    full = lambda a: pl.BlockSpec(a.shape, lambda i: (0, 0))
    halo = pl.BlockSpec((8, GDN_CONV_W), lambda i: (jnp.maximum(i * (tm // 8) - 1, 0), 0))
    return pl.pallas_call(
        body, name="gdn_conv_bwd_a", grid=(t // tm,),
        in_specs=[row(GDN_CONV_W), halo, full(cw), row(LANES), full(gp), row(D_MODEL), row(D_MODEL),
                  row(D_MODEL), row(LANES)],
        out_specs=[row(GDN_CONV_W), row(LANES), pl.BlockSpec((8, GDN_CONV_W), lambda i: (0, 0)),
                   pl.BlockSpec((8, LANES), lambda i: (0, 0))],
        out_shape=[jax.ShapeDtypeStruct((t, GDN_CONV_W), F32),
                   jax.ShapeDtypeStruct((t, LANES), BF16),
                   jax.ShapeDtypeStruct((8, GDN_CONV_W), F32),
                   jax.ShapeDtypeStruct((8, LANES), F32)],
        scratch_shapes=[pltpu.VMEM((tm, GDN_CONV_W), F32)],
        compiler_params=_params(1),
    )(x, x, cw, ba, gp, dq, dk, dv, dgates)


def _gdn_conv_bwd_b(dc, cw):
    t = dc.shape[0]
    tm = _row_tile(t, GDN_CONV_TM)
    nsteps = t // tm

    def body(dc_ref, halo_ref, cw_ref, dx_ref):
        halo = jnp.where(pl.program_id(0) < nsteps - 1, halo_ref[...], 0.0)
        dd = jnp.concatenate([dc_ref[...], halo], axis=0)
        cw_v = cw_ref[...]
        acc = dd[GDN_CONV - 1:GDN_CONV - 1 + tm] * cw_v[0:1]
        for i in range(1, GDN_CONV):
            acc = acc + dd[GDN_CONV - 1 - i:GDN_CONV - 1 - i + tm] * cw_v[i:i + 1]
        dx_ref[...] = acc.astype(BF16)

    row = pl.BlockSpec((tm, GDN_CONV_W), lambda i: (i, 0))
    halo = pl.BlockSpec((8, GDN_CONV_W), lambda i: (jnp.minimum((i + 1) * (tm // 8), t // 8 - 1), 0))
    return pl.pallas_call(
        body, name="gdn_conv_bwd_b", grid=(nsteps,),
        in_specs=[row, halo, pl.BlockSpec(cw.shape, lambda i: (0, 0))],
        out_specs=row,
        out_shape=jax.ShapeDtypeStruct((t, GDN_CONV_W), BF16),
        compiler_params=_params(1),
    )(dc, dc, cw)


def _gdn_proj_bwd(dy, h, nw, w_qkv, w_z, w_ba, dx, dz, dba):
    t = h.shape[0]
    tm = _row_tile(t, GDN_TM)

    def body(dy_ref, h_ref, nw_ref, wq_ref, wz_ref, wb_ref, dx_ref, dz_ref, dba_ref, dh_ref, dnw_ref):
        @pl.when(pl.program_id(0) == 0)
        def _():
            dnw_ref[...] = jnp.zeros_like(dnw_ref)

        dhn = _nt(dx_ref[...], wq_ref[...]) + _nt(dz_ref[...], wz_ref[...]) + _nt(dba_ref[...], wb_ref[...])
        dxx, dnw = _norm_bwd(h_ref[...], nw_ref[...], dhn)
        dh_ref[...] = dy_ref[...] + dxx
        dnw_ref[...] += dnw

    row = lambda w: pl.BlockSpec((tm, w), lambda i: (i, 0))
    full = lambda a: pl.BlockSpec(a.shape, lambda i: (0, 0))
    return pl.pallas_call(
        body, name="gdn_proj_bwd", grid=(t // tm,),
        in_specs=[row(D_MODEL), row(D_MODEL), full(nw), full(w_qkv), full(w_z), full(w_ba),
                  row(GDN_CONV_W), row(D_MODEL), row(LANES)],
        out_specs=[row(D_MODEL), pl.BlockSpec((1, D_MODEL), lambda i: (0, 0))],
        out_shape=[jax.ShapeDtypeStruct((t, D_MODEL), F32), jax.ShapeDtypeStruct((1, D_MODEL), F32)],
        compiler_params=_params(1),
    )(dy, h, nw, w_qkv, w_z, w_ba, dx, dz, dba)


def _loss_head(h, nw, target):
    t = h.shape[0]
    tm = _row_tile(t, 512)

    def body(h_ref, nw_ref, t_ref, loss_ref, dh_ref, dnw_ref):
        @pl.when(pl.program_id(0) == 0)
        def _():
            loss_ref[...] = jnp.zeros_like(loss_ref)
            dnw_ref[...] = jnp.zeros_like(dnw_ref)

        x = h_ref[...]
        nwv = nw_ref[...]
        _, xhat = _rms_stats(x)
        err = xhat * nwv - t_ref[...]
        loss_ref[...] += 0.5 * jnp.sum(jnp.mean(err * err, axis=-1, keepdims=True))
        dx, dnw = _norm_bwd(x, nwv, err * (1.0 / D_MODEL))
        dh_ref[...] = dx
        dnw_ref[...] += dnw

    row = pl.BlockSpec((tm, D_MODEL), lambda i: (i, 0))
    return pl.pallas_call(
        body, name="loss_head", grid=(t // tm,),
        in_specs=[row, pl.BlockSpec((1, D_MODEL), lambda i: (0, 0)), row],
        out_specs=[pl.BlockSpec((8, LANES), lambda i: (0, 0)), row, pl.BlockSpec((1, D_MODEL), lambda i: (0, 0))],
        out_shape=[jax.ShapeDtypeStruct((8, LANES), F32),
                   jax.ShapeDtypeStruct((t, D_MODEL), F32),
                   jax.ShapeDtypeStruct((1, D_MODEL), F32)],
        compiler_params=_params(1),
    )(h, nw, target)


_PEER_FLIPS = [(dx, dy, dc) for dx in (0, 1) for dy in (0, 1) for dc in (0, 1)][1:]


def _exchange(arrs, scatter, name):
    n = len(arrs)
    npeer = len(_PEER_FLIPS)

    def body(*refs):
        ins, outs = refs[:n], refs[n:2 * n]
        send_sems, recv_sems, local_sems = refs[2 * n:]
        x, y, c = lax.axis_index("x"), lax.axis_index("y"), lax.axis_index("c")
        me = 4 * x + 2 * y + c
        copies = []
        for a in range(n):
            src = ins[a].at[me] if scatter else ins[a]
            local = pltpu.make_async_copy(src, outs[a].at[me], local_sems.at[a])
            local.start()
            copies.append(local)
        for k, (fx, fy, fc) in enumerate(_PEER_FLIPS):
            px, py, pc = lax.rem(x + fx, 2), lax.rem(y + fy, 2), lax.rem(c + fc, 2)
            peer = 4 * px + 2 * py + pc
            for a in range(n):
                cp = pltpu.make_async_remote_copy(
                    src_ref=ins[a].at[peer] if scatter else ins[a],
                    dst_ref=outs[a].at[me],
                    send_sem=send_sems.at[a, k], recv_sem=recv_sems.at[a, k],
                    device_id=(px, py, pc), device_id_type=MESH_ID)
                cp.start()
                copies.append(cp)
        for cp in copies:
            cp.wait()

    anyspec = pl.BlockSpec(memory_space=pl.ANY)
    return pl.pallas_call(
        body, name=name,
        in_specs=[anyspec] * n, out_specs=[anyspec] * n,
        out_shape=[jax.ShapeDtypeStruct((N_DEV,) + (a.shape[1:] if scatter else a.shape), a.dtype) for a in arrs],
        scratch_shapes=[pltpu.SemaphoreType.DMA((n, npeer)), pltpu.SemaphoreType.DMA((n, npeer)),
                        pltpu.SemaphoreType.DMA((n,))],
    )(*arrs)


def _adamw(parts, w, m, v, name):
    r, c = w.shape
    tr = r
    for cand in (512, 256, 128, 64, 32, 16, 8):
        if r % cand == 0:
            tr = cand
            break
    c1 = 1.0 - ADAM_B1 ** ADAM_STEP
    c2 = 1.0 - ADAM_B2 ** ADAM_STEP

    def body(p_ref, w_ref, m_ref, v_ref, g_ref, d_ref, nm_ref, nv_ref):
        g = p_ref[0].astype(F32)
        for s in range(1, N_DEV):
            g = g + p_ref[s].astype(F32)
        nm = ADAM_B1 * m_ref[...] + (1.0 - ADAM_B1) * g
        nv = ADAM_B2 * v_ref[...] + (1.0 - ADAM_B2) * (g * g)
        g_ref[...] = g
        nm_ref[...] = nm
        nv_ref[...] = nv
        d_ref[...] = -ADAM_LR * ((nm / c1) / (jnp.sqrt(nv / c2) + ADAM_EPS) + ADAM_WD * w_ref[...])

    blk = pl.BlockSpec((tr, c), lambda i: (i, 0))
    return pl.pallas_call(
        body, name=name, grid=(r // tr,),
        in_specs=[pl.BlockSpec((N_DEV, tr, c), lambda i: (0, i, 0)), blk, blk, blk],
        out_specs=[blk] * 4,
        out_shape=[jax.ShapeDtypeStruct((r, c), F32)] * 4,
        compiler_params=_params(1),
    )(parts, w, m, v)


def _rope_tables(positions):
    half = ROPE_DIM // 2
    inv_freq = ROPE_THETA ** (-jnp.arange(0, ROPE_DIM, 2, dtype=F32) / ROPE_DIM)
    ang = positions.astype(F32)[:, None] * inv_freq
    cos, sin = jnp.cos(ang), jnp.sin(ang)
    t = positions.shape[0]
    zeros = lambda w: jnp.zeros((t, w), F32)
    c = jnp.concatenate([cos, cos, jnp.ones((t, ATTN_HEAD_DIM - ROPE_DIM), F32)], axis=1)
    s1 = jnp.concatenate([-sin, zeros(ATTN_HEAD_DIM - half)], axis=1)
    s2 = jnp.concatenate([zeros(half), sin, zeros(ATTN_HEAD_DIM - ROPE_DIM)], axis=1)
    return tuple(jnp.tile(a, (1, LANES // ATTN_HEAD_DIM)) for a in (c, s1, s2))


def _ffn_layer_fwd(h, nw, w):
    out, g, u, hn = _ffn_fwd(h, nw, w["wg"], w["wu"], w["wd"])
    return out, (h, g, u, hn)


def _ffn_layer_bwd(dy, saved, nw, w):
    h, g, u, hn = saved
    dh, dg, du, act, dz, dnw = _ffn_bwd(dy, h, nw, g, u, w["wg"], w["wu"], w["wd"])
    dwg = _matmul_tn(hn, dg, FFN_TF, "ffn_dwg")
    dwu = _matmul_tn(hn, du, FFN_TF, "ffn_dwu")
    dwd = _matmul_tn(act, dz, D_MODEL, "ffn_dwd")
    return dh, dnw, dwg, dwu, dwd


def _attn_layer_fwd(h, nw, w, ropes):
    q, k, v, hn = _attn_qkv_fwd(h, nw, w["w_in"], w["b_in"], *ropes)
    o = _attn_core_fwd(q, k, v, w["sinks"])
    out = _proj_out_fwd(o, w["w_out"], w["b_out"], h)
    return out, (h, hn, q, k, v, o)


def _attn_layer_bwd(dy, saved, nw, w, ropes):
    h, hn, q, k, v, o = saved
    do, dyb, db_out = _proj_out_bwd(dy, w["w_out"])
    dw_out = _matmul_tn(o, dyb, D_MODEL, "attn_dw_out")
    dq, dk, dv, dsink = _attn_core_bwd(q, k, v, do, w["sinks"])
    dh, dqkv, db_in, dnw = _attn_qkv_bwd(dy, h, nw, w["w_in"], dq, dk, dv, *ropes)
    dw_in = _matmul_tn(hn, dqkv, ATTN_IN, "attn_dw_in")
    return dh, dnw, dict(w_in=dw_in, b_in=db_in, sinks=dsink[:, 0], w_out=dw_out, b_out=db_out)


def _gdn_layer_fwd(h, nw, w):
    x, z, ba, hn = _gdn_proj_fwd(h, nw, w["w_qkv"], w["w_z"], w["w_ba"])
    q, k, v, gates = _gdn_conv_fwd(x, w["conv_w"], ba, w["gp"])
    u, ww, qd, kd, aqk, cd = _gdn_intra_fwd(q, k, v, gates)
    o, st = _gdn_scan_fwd(u, ww, qd, kd, aqk, cd)
    out, gated = _gdn_out_fwd(o, z, w["norm_w"], w["w_out"], h)
    return out, (h, hn, x, z, ba, q, k, v, gates, u, ww, qd, kd, aqk, cd, st, o, gated)


def _gdn_layer_bwd(dy, saved, nw, w):
    h, hn, x, z, ba, q, k, v, gates, u, ww, qd, kd, aqk, cd, st, o, gated = saved
    do, dz, dyb, dnorm_w = _gdn_out_bwd(dy, w["w_out"], o, z, w["norm_w"])
    dw_out = _matmul_tn(gated, dyb, D_MODEL, "gdn_dw_out")
    du, dw, dqd, dkd, daqk, daqkt, dcd = _gdn_scan_bwd(u, ww, qd, kd, aqk, cd, st, do)
    dq, dk, dv, dgates = _gdn_intra_bwd(q, k, v, gates, u, ww, du, dw, dqd, dkd, daqk, daqkt, dcd)
    dc, dba, dcw, dgp = _gdn_conv_bwd_a(x, w["conv_w"], ba, w["gp"], dq, dk, dv, dgates)
    dx = _gdn_conv_bwd_b(dc, w["conv_w"])
    dh, dnw = _gdn_proj_bwd(dy, h, nw, w["w_qkv"], w["w_z"], w["w_ba"], dx, dz, dba)
    dw_qkv = _matmul_tn(hn, dx, GDN_CONV_W // 2, "gdn_dw_qkv")
    dw_z = _matmul_tn(hn, dz, D_MODEL, "gdn_dw_z")
    dw_ba = _matmul_tn(hn, dba, LANES, "gdn_dw_ba")
    dw_in = jnp.concatenate([dw_qkv, dw_z, dw_ba[:, :2 * GDN_HEADS]], axis=1)
    lo, hi = GDN_HEADS, 2 * GDN_HEADS
    return dh, dnw, dict(w_in=dw_in, conv_w=dcw[:GDN_CONV], A_log=dgp[0, lo:hi], dt_bias=dgp[1, lo:hi],
                         norm_w=dnorm_w[0], w_out=dw_out)


def _local_step(x, positions, target, norms, ffn_w, attn_w, gdn_w, final_norm):
    ropes = _rope_tables(positions)
    h = x
    saved = []
    for layer in range(DEPTH):
        h, s1 = _ffn_layer_fwd(h, norms["ffn1"][layer], ffn_w["ffn1"][layer])
        j = layer // 2
        if layer % 2 == 0:
            h, s2 = _attn_layer_fwd(h, norms["mix"][layer], attn_w[j], ropes)
        else:
            h, s2 = _gdn_layer_fwd(h, norms["mix"][layer], gdn_w[j])
        h, s3 = _ffn_layer_fwd(h, norms["ffn2"][layer], ffn_w["ffn2"][layer])
        saved.append((s1, s2, s3))
    loss, dh, dfinal = _loss_head(h, final_norm, target)

    g_norm = {k: [None] * DEPTH for k in ("ffn1", "mix", "ffn2")}
    g_ffn = {k: [None] * DEPTH for k in ("ffn1", "ffn2")}
    g_attn, g_gdn = [None] * (DEPTH // 2), [None] * (DEPTH // 2)
    for layer in reversed(range(DEPTH)):
        s1, s2, s3 = saved[layer]
        j = layer // 2
        dh, g_norm["ffn2"][layer], *g_ffn["ffn2"][layer] = _ffn_layer_bwd(
            dh, s3, norms["ffn2"][layer], ffn_w["ffn2"][layer])
        if layer % 2 == 0:
            dh, g_norm["mix"][layer], g_attn[j] = _attn_layer_bwd(dh, s2, norms["mix"][layer], attn_w[j], ropes)
        else:
            dh, g_norm["mix"][layer], g_gdn[j] = _gdn_layer_bwd(dh, s2, norms["mix"][layer], gdn_w[j])
        dh, g_norm["ffn1"][layer], *g_ffn["ffn1"][layer] = _ffn_layer_bwd(
            dh, s1, norms["ffn1"][layer], ffn_w["ffn1"][layer])
    return loss, dh, dfinal, g_norm, g_ffn, g_attn, g_gdn


def _cols_full(gathered, layer):
    g = gathered[:, layer]
    return jnp.transpose(g, (1, 0, 2)).reshape(g.shape[1], -1)


def _rows_full(gathered, layer):
    g = gathered[:, layer]
    return g.reshape(-1, g.shape[-1])


def _cols_shards(full, n=N_DEV):
    r = full.shape[0]
    return jnp.transpose(full.reshape(r, n, -1), (1, 0, 2))


def _rows_shards(full):
    return full.reshape(N_DEV, -1, full.shape[-1])


_SMALL = ("ffn1_norm", "mix_norm", "ffn2_norm", "attn_b_in", "attn_sinks", "attn_b_out",
          "gdn_A_log", "gdn_dt_bias", "gdn_norm_w", "final_norm")
_BIG = ("ffn1_w_gate_up", "ffn1_w_down", "ffn2_w_gate_up", "ffn2_w_down", "attn_w_in", "attn_w_out",
        "gdn_w_in", "gdn_conv_w", "gdn_w_out")
_WEIGHTS = ("ffn1_norm", "ffn1_w_gate_up", "ffn1_w_down", "mix_norm", "ffn2_norm", "ffn2_w_gate_up",
            "ffn2_w_down", "attn_w_in", "attn_b_in", "attn_sinks", "attn_w_out", "attn_b_out", "gdn_w_in",
            "gdn_conv_w", "gdn_A_log", "gdn_dt_bias", "gdn_norm_w", "gdn_w_out", "final_norm")


def _pack_small(vals):
    flat = jnp.concatenate([v.reshape(-1).astype(F32) for v in vals])
    pad = (-flat.shape[0]) % (8 * LANES)
    return jnp.pad(flat, (0, pad)).reshape(-1, LANES)


def _unpack_small(packed, shapes):
    flat = packed.reshape(-1)
    out, off = [], 0
    for s in shapes:
        size = 1
        for d in s:
            size *= d
        out.append(flat[off:off + size].reshape(s))
        off += size
    return out


def kernel(x, positions, ffn1_norm, ffn1_w_gate_up, ffn1_w_down, mix_norm, ffn2_norm, ffn2_w_gate_up, ffn2_w_down, attn_w_in, attn_b_in, attn_sinks, attn_w_out, attn_b_out, gdn_w_in, gdn_conv_w, gdn_A_log, gdn_dt_bias, gdn_norm_w, gdn_w_out, final_norm, loss_target, m_ffn1_norm, m_ffn1_w_gate_up, m_ffn1_w_down, m_mix_norm, m_ffn2_norm, m_ffn2_w_gate_up, m_ffn2_w_down, m_attn_w_in, m_attn_b_in, m_attn_sinks, m_attn_w_out, m_attn_b_out, m_gdn_w_in, m_gdn_conv_w, m_gdn_A_log, m_gdn_dt_bias, m_gdn_norm_w, m_gdn_w_out, m_final_norm, v_ffn1_norm, v_ffn1_w_gate_up, v_ffn1_w_down, v_mix_norm, v_ffn2_norm, v_ffn2_w_gate_up, v_ffn2_w_down, v_attn_w_in, v_attn_b_in, v_attn_sinks, v_attn_w_out, v_attn_b_out, v_gdn_w_in, v_gdn_conv_w, v_gdn_A_log, v_gdn_dt_bias, v_gdn_norm_w, v_gdn_w_out, v_final_norm):
    args = dict(locals())
    wts = {n: args[n] for n in _WEIGHTS}
    moms = {n: args["m_" + n] for n in _WEIGHTS}
    vels = {n: args["v_" + n] for n in _WEIGHTS}

    order = [n for n in _BIG if n != "gdn_conv_w"]
    gathered = dict(zip(order, _exchange([wts[n].astype(BF16) for n in order], False, "gather_weights")))
    conv_all = _exchange([gdn_conv_w], False, "gather_conv")[0]

    ffn_w = {}
    for kind in ("ffn1", "ffn2"):
        gu, dn = gathered[kind + "_w_gate_up"], gathered[kind + "_w_down"]
        ffn_w[kind] = [dict(wg=_cols_full(gu[:4], l), wu=_cols_full(gu[4:], l), wd=_rows_full(dn, l))
                       for l in range(DEPTH)]
    attn_w, gdn_w = [], []
    for j in range(DEPTH // 2):
        attn_w.append(dict(w_in=_cols_full(gathered["attn_w_in"], j), b_in=attn_b_in[j][None],
                           sinks=attn_sinks[j][None], w_out=_rows_full(gathered["attn_w_out"], j),
                           b_out=attn_b_out[j][None]))
        w_in = _cols_full(gathered["gdn_w_in"], j)
        w_ba = jnp.pad(w_in[:, GDN_CONV_W + D_MODEL:], ((0, 0), (0, LANES - 2 * GDN_HEADS)))
        gp = jnp.pad(jnp.stack([gdn_A_log[j], gdn_dt_bias[j]]), ((0, 6), (GDN_HEADS, LANES - 2 * GDN_HEADS)))
        gdn_w.append(dict(w_qkv=w_in[:, :GDN_CONV_W], w_z=w_in[:, GDN_CONV_W:GDN_CONV_W + D_MODEL], w_ba=w_ba,
                          conv_w=_cols_full(conv_all, j), gp=gp, norm_w=gdn_norm_w[j][None],
                          w_out=_rows_full(gathered["gdn_w_out"], j)))
    norms = dict(ffn1=ffn1_norm[:, None, :], mix=mix_norm[:, None, :], ffn2=ffn2_norm[:, None, :])

    loss, grad_x, dfinal, g_norm, g_ffn, g_attn, g_gdn = _local_step(
        x[0], positions[0], loss_target[0], norms, ffn_w, attn_w, gdn_w, final_norm[None])

    parts = {}
    for kind in ("ffn1", "ffn2"):
        gu = [jnp.concatenate([_cols_shards(g_ffn[kind][l][0], 4), _cols_shards(g_ffn[kind][l][1], 4)], axis=0)
              for l in range(DEPTH)]
        parts[kind + "_w_gate_up"] = jnp.stack(gu, axis=1)
        parts[kind + "_w_down"] = jnp.stack([_rows_shards(g_ffn[kind][l][2]) for l in range(DEPTH)], axis=1)
    parts["attn_w_in"] = jnp.stack([_cols_shards(g["w_in"]) for g in g_attn], axis=1)
    parts["attn_w_out"] = jnp.stack([_rows_shards(g["w_out"]) for g in g_attn], axis=1)
    parts["gdn_w_in"] = jnp.stack([_cols_shards(g["w_in"]) for g in g_gdn], axis=1)
    parts["gdn_conv_w"] = jnp.stack([_cols_shards(g["conv_w"]) for g in g_gdn], axis=1)
    parts["gdn_w_out"] = jnp.stack([_rows_shards(g["w_out"]) for g in g_gdn], axis=1)
    sent = [parts[n] if n == "gdn_conv_w" else parts[n].astype(BF16) for n in _BIG]
    received = dict(zip(_BIG, _exchange(sent, True, "scatter_grads")))

    small_g = dict(
        ffn1_norm=jnp.concatenate(g_norm["ffn1"], axis=0), mix_norm=jnp.concatenate(g_norm["mix"], axis=0),
        ffn2_norm=jnp.concatenate(g_norm["ffn2"], axis=0),
        attn_b_in=jnp.concatenate([g["b_in"] for g in g_attn], axis=0),
        attn_sinks=jnp.stack([g["sinks"] for g in g_attn]),
        attn_b_out=jnp.concatenate([g["b_out"] for g in g_attn], axis=0),
        gdn_A_log=jnp.stack([g["A_log"] for g in g_gdn]), gdn_dt_bias=jnp.stack([g["dt_bias"] for g in g_gdn]),
        gdn_norm_w=jnp.stack([g["norm_w"] for g in g_gdn]), final_norm=dfinal[0])
    small_parts = _exchange([_pack_small([small_g[n] for n in _SMALL] + [loss[0, :1]])], False, "gather_small")[0]
    pad1 = jnp.zeros((1,), F32)
    sw = _pack_small([wts[n] for n in _SMALL] + [pad1])
    sm = _pack_small([moms[n] for n in _SMALL] + [pad1])
    sv = _pack_small([vels[n] for n in _SMALL] + [pad1])
    shapes = [wts[n].shape for n in _SMALL] + [(1,)]
    small_out = [_unpack_small(o, shapes) for o in _adamw(small_parts, sw, sm, sv, "adamw_small")]
    results = {n: tuple(o[i] for o in small_out) for i, n in enumerate(_SMALL)}
    loss_total = small_out[0][-1][0]

    for n in _BIG:
        shape = wts[n].shape
        two_d = lambda a: a.reshape(-1, shape[-1])
        p = received[n].reshape(N_DEV, -1, shape[-1])
        outs = _adamw(p, two_d(wts[n]), two_d(moms[n]), two_d(vels[n]), "adamw_" + n)
        results[n] = tuple(o.reshape(shape) for o in outs)

    return (loss_total, grad_x[None],
            *[results[n][0] for n in _WEIGHTS], *[results[n][1] for n in _WEIGHTS],
            *[results[n][2] for n in _WEIGHTS], *[results[n][3] for n in _WEIGHTS])
```

```python
import jax
import jax.numpy as jnp
from jax import lax
from jax.experimental import pallas as pl
from jax.experimental.pallas import tpu as pltpu

F32 = jnp.float32
BF16 = jnp.bfloat16

D_MODEL = 1024
DEPTH = 4
D_FF = 2816
NORM_EPS = 1e-6
N_DEV = 8

ATTN_Q_HEADS = 16
ATTN_KV_HEADS = 4
ATTN_HEAD_DIM = 64
ATTN_GROUP = 4
ATTN_BLOCK = 128
ROPE_DIM = 16
ROPE_THETA = 500000.0
ATTN_Q_W = 1024
ATTN_KV_W = 256
ATTN_IN = 1536
ATTN_SCALE = ATTN_HEAD_DIM ** -0.5

GDN_HEADS = 8
GDN_DK = 128
GDN_CONV = 4
GDN_CHUNK = 64
GDN_CONV_W = 3072
GDN_IN = 4112
GDN_QSCALE = GDN_DK ** -0.5
GDN_ROWS = 256
GDN_SUB_SHIFT = 4

ADAM_LR = 0.001
ADAM_B1 = 0.9
ADAM_B2 = 0.999
ADAM_EPS = 1e-08
ADAM_WD = 0.01
ADAM_STEP = 10

LANES = 128
NEG_BIG = -1e30
VMEM_LIMIT_BYTES = 56 * 1024 * 1024
MESH_ID = pl.DeviceIdType.MESH


def _params(n_axes):
    return pltpu.CompilerParams(dimension_semantics=("arbitrary",) * n_axes,
                                vmem_limit_bytes=VMEM_LIMIT_BYTES)


def _nn(a, b):
    return jnp.dot(a, b, preferred_element_type=F32)


def _nt(a, b):
    return lax.dot_general(a, b, (((1,), (1,)), ((), ())), preferred_element_type=F32)


def _tn(a, b):
    return lax.dot_general(a, b, (((0,), (0,)), ((), ())), preferred_element_type=F32)


def _bnn(a, b, precision=None):
    return lax.dot_general(a, b, (((2,), (1,)), ((0,), (0,))), precision=precision,
                           preferred_element_type=F32)


def _bnt(a, b, precision=None):
    return lax.dot_general(a, b, (((2,), (2,)), ((0,), (0,))), precision=precision,
                           preferred_element_type=F32)


def _sigmoid(x):
    return 1.0 / (1.0 + jnp.exp(-x))


def _rms_stats(x):
    r = lax.rsqrt(jnp.mean(x * x, axis=-1, keepdims=True) + NORM_EPS)
    return r, x * r


def _norm_bwd(x, nw, dhn):
    r, xhat = _rms_stats(x)
    dxh = dhn * nw
    dx = r * (dxh - xhat * jnp.mean(dxh * xhat, axis=-1, keepdims=True))
    dnw = jnp.sum(dhn * xhat, axis=0, keepdims=True)
    return dx, dnw


def _row_tile(t, pref):
    return min(t, pref)


FFN_TM = 512
FFN_BWD_TM = 256
FFN_TF = 1408


def _carried(body, n_in, n_out, carry, scatter, last_step):
    nc = len(carry)
    if not nc:
        return body

    def wrapped(*refs):
        ins, cin = refs[:n_in], refs[n_in:n_in + nc]
        outs = refs[n_in + nc:n_in + nc + n_out]
        cout = refs[n_in + nc + n_out:n_in + 2 * nc + n_out]
        scratch = refs[n_in + 2 * nc + n_out:]
        sems = scratch[len(scratch) - 3:]
        i, j = pl.program_id(0), pl.program_id(1)

        @pl.when(jnp.logical_and(i == 0, j == 0))
        def _():
            for cp in _exchange_copies(cin, cout, *sems, scatter):
                cp.start()

        body(*ins, *outs, *scratch[:len(scratch) - 3])

        @pl.when(jnp.logical_and(i == last_step[0], j == last_step[1]))
        def _():
            for cp in _exchange_copies(cin, cout, *sems, scatter):
                cp.wait()

    return wrapped


def _ffn_fwd(h, nw, wg, wu, wd, carry=(), scatter=False):
    t = h.shape[0]
    tm, tf = _row_tile(t, FFN_TM), FFN_TF
    nj = D_FF // tf
    nc = len(carry)

    def body(h_ref, nw_ref, wg_ref, wu_ref, wd_ref, out_ref, g_ref, u_ref, hn_ref, acc_ref):
        j = pl.program_id(1)

        @pl.when(j == 0)
        def _():
            _, xhat = _rms_stats(h_ref[...])
            hn_ref[...] = (xhat * nw_ref[...]).astype(BF16)
            acc_ref[...] = jnp.zeros_like(acc_ref)

        hn = hn_ref[...]
        g = _nn(hn, wg_ref[...])
        u = _nn(hn, wu_ref[...])
        g_ref[...] = g.astype(BF16)
        u_ref[...] = u.astype(BF16)
        a = (g * _sigmoid(g) * u).astype(BF16)
        acc_ref[...] += _nn(a, wd_ref[...])

        @pl.when(j == nj - 1)
        def _():
            out_ref[...] = h_ref[...] + 0.5 * acc_ref[...]

    outs = pl.pallas_call(
        _carried(body, 5, 4, carry, scatter, (t // tm - 1, nj - 1)),
        name="ffn_fwd_x" if nc else "ffn_fwd", grid=(t // tm, nj),
        in_specs=[pl.BlockSpec((tm, D_MODEL), lambda i, j: (i, 0)),
                  pl.BlockSpec((1, D_MODEL), lambda i, j: (0, 0)),
                  pl.BlockSpec((D_MODEL, tf), lambda i, j: (0, j)),
                  pl.BlockSpec((D_MODEL, tf), lambda i, j: (0, j)),
                  pl.BlockSpec((tf, D_MODEL), lambda i, j: (j, 0))] + [_ANY] * nc,
        out_specs=[pl.BlockSpec((tm, D_MODEL), lambda i, j: (i, 0)),
                   pl.BlockSpec((tm, tf), lambda i, j: (i, j)),
                   pl.BlockSpec((tm, tf), lambda i, j: (i, j)),
                   pl.BlockSpec((tm, D_MODEL), lambda i, j: (i, 0))] + [_ANY] * nc,
        out_shape=[jax.ShapeDtypeStruct((t, D_MODEL), F32),
                   jax.ShapeDtypeStruct((t, D_FF), BF16),
                   jax.ShapeDtypeStruct((t, D_FF), BF16),
                   jax.ShapeDtypeStruct((t, D_MODEL), BF16)] + _exchange_shapes(carry, scatter),
        scratch_shapes=[pltpu.VMEM((tm, D_MODEL), F32)] + (_exchange_sems(nc) if nc else []),
        compiler_params=_params(2),
    )(h, nw, wg, wu, wd, *carry)
    return outs[:4], outs[4:]


def _ffn_bwd(dy, h, nw, g, u, wg, wu, wd, carry=(), scatter=True):
    t = h.shape[0]
    tm, tf = _row_tile(t, FFN_BWD_TM), FFN_TF
    nj = D_FF // tf
    nc = len(carry)

    def body(dy_ref, h_ref, nw_ref, g_ref, u_ref, wg_ref, wu_ref, wd_ref,
             dh_ref, dg_ref, du_ref, a_ref, dz_ref, dnw_ref, acc_ref):
        i, j = pl.program_id(0), pl.program_id(1)

        @pl.when(j == 0)
        def _():
            dz_ref[...] = (0.5 * dy_ref[...]).astype(BF16)
            acc_ref[...] = jnp.zeros_like(acc_ref)

        @pl.when(jnp.logical_and(i == 0, j == 0))
        def _():
            dnw_ref[...] = jnp.zeros_like(dnw_ref)

        da = _nt(dz_ref[...], wd_ref[...])
        gv = g_ref[...].astype(F32)
        uv = u_ref[...].astype(F32)
        sig = _sigmoid(gv)
        silu = gv * sig
        dg = (da * uv * (sig * (1.0 + gv * (1.0 - sig)))).astype(BF16)
        du = (da * silu).astype(BF16)
        dg_ref[...] = dg
        du_ref[...] = du
        a_ref[...] = (silu * uv).astype(BF16)
        acc_ref[...] += _nt(dg, wg_ref[...]) + _nt(du, wu_ref[...])

        @pl.when(j == nj - 1)
        def _():
            dx, dnw = _norm_bwd(h_ref[...], nw_ref[...], acc_ref[...])
            dh_ref[...] = dy_ref[...] + dx
            dnw_ref[...] += dnw

    outs = pl.pallas_call(
        _carried(body, 8, 6, carry, scatter, (t // tm - 1, nj - 1)),
        name="ffn_bwd_x" if nc else "ffn_bwd", grid=(t // tm, nj),
        in_specs=[pl.BlockSpec((tm, D_MODEL), lambda i, j: (i, 0)),
                  pl.BlockSpec((tm, D_MODEL), lambda i, j: (i, 0)),
                  pl.BlockSpec((1, D_MODEL), lambda i, j: (0, 0)),
                  pl.BlockSpec((tm, tf), lambda i, j: (i, j)),
                  pl.BlockSpec((tm, tf), lambda i, j: (i, j)),
                  pl.BlockSpec((D_MODEL, tf), lambda i, j: (0, j)),
                  pl.BlockSpec((D_MODEL, tf), lambda i, j: (0, j)),
                  pl.BlockSpec((tf, D_MODEL), lambda i, j: (j, 0))] + [_ANY] * nc,
        out_specs=[pl.BlockSpec((tm, D_MODEL), lambda i, j: (i, 0)),
                   pl.BlockSpec((tm, tf), lambda i, j: (i, j)),
                   pl.BlockSpec((tm, tf), lambda i, j: (i, j)),
                   pl.BlockSpec((tm, tf), lambda i, j: (i, j)),
                   pl.BlockSpec((tm, D_MODEL), lambda i, j: (i, 0)),
                   pl.BlockSpec((1, D_MODEL), lambda i, j: (0, 0))] + [_ANY] * nc,
        out_shape=[jax.ShapeDtypeStruct((t, D_MODEL), F32),
                   jax.ShapeDtypeStruct((t, D_FF), BF16),
                   jax.ShapeDtypeStruct((t, D_FF), BF16),
                   jax.ShapeDtypeStruct((t, D_FF), BF16),
                   jax.ShapeDtypeStruct((t, D_MODEL), BF16),
                   jax.ShapeDtypeStruct((1, D_MODEL), F32)] + _exchange_shapes(carry, scatter),
        scratch_shapes=[pltpu.VMEM((tm, D_MODEL), F32)] + (_exchange_sems(nc) if nc else []),
        compiler_params=_params(2),
    )(dy, h, nw, g, u, wg, wu, wd, *carry)
    return outs[:6], outs[6:]


def _matmul_tn(a, b, tn, name):
    k, m = a.shape
    n = b.shape[1]
    tm = min(m, 1408 if m % 1408 == 0 else 1024)
    tk = min(k, 1024)

    def body(a_ref, b_ref, o_ref):
        @pl.when(pl.program_id(2) == 0)
        def _():
            o_ref[...] = jnp.zeros_like(o_ref)

        o_ref[...] += _tn(a_ref[...], b_ref[...])

    return pl.pallas_call(
        body, name=name, grid=(m // tm, n // tn, k // tk),
        in_specs=[pl.BlockSpec((tk, tm), lambda i, j, kk: (kk, i)),
                  pl.BlockSpec((tk, tn), lambda i, j, kk: (kk, j))],
        out_specs=pl.BlockSpec((tm, tn), lambda i, j, kk: (i, j)),
        out_shape=jax.ShapeDtypeStruct((m, n), F32),
        compiler_params=_params(3),
    )(a, b)


ATTN_TM = 512


def _rope(t, c, s1, s2):
    return t * c + pltpu.roll(t, LANES - ROPE_DIM // 2, 1) * s1 + pltpu.roll(t, ROPE_DIM // 2, 1) * s2


def _rope_bwd(d, c, s1, s2):
    return d * c + pltpu.roll(d * s1, ROPE_DIM // 2, 1) + pltpu.roll(d * s2, LANES - ROPE_DIM // 2, 1)


def _attn_qkv_fwd(h, nw, w_in, b_in, rc, rs1, rs2):
    t = h.shape[0]
    tm = _row_tile(t, ATTN_TM)

    def body(h_ref, nw_ref, w_ref, b_ref, c_ref, s1_ref, s2_ref, q_ref, k_ref, v_ref, hn_ref):
        _, xhat = _rms_stats(h_ref[...])
        hn = (xhat * nw_ref[...]).astype(BF16)
        hn_ref[...] = hn
        qkv = _nn(hn, w_ref[...]) + b_ref[...]
        c, s1, s2 = c_ref[...], s1_ref[...], s2_ref[...]
        for s in range(ATTN_Q_W // LANES):
            q_ref[:, s * LANES:(s + 1) * LANES] = _rope(qkv[:, s * LANES:(s + 1) * LANES], c, s1, s2).astype(BF16)
        for s in range(ATTN_KV_W // LANES):
            lo = ATTN_Q_W + s * LANES
            k_ref[:, s * LANES:(s + 1) * LANES] = _rope(qkv[:, lo:lo + LANES], c, s1, s2).astype(BF16)
        v_ref[...] = qkv[:, ATTN_Q_W + ATTN_KV_W:].astype(BF16)

    row = lambda w: pl.BlockSpec((tm, w), lambda i: (i, 0))
    full = lambda a: pl.BlockSpec(a.shape, lambda i: (0, 0))
    return pl.pallas_call(
        body, name="attn_qkv_fwd", grid=(t // tm,),
        in_specs=[row(D_MODEL), full(nw), full(w_in), full(b_in), row(LANES), row(LANES), row(LANES)],
        out_specs=[row(ATTN_Q_W), row(ATTN_KV_W), row(ATTN_KV_W), row(D_MODEL)],
        out_shape=[jax.ShapeDtypeStruct((t, ATTN_Q_W), BF16),
                   jax.ShapeDtypeStruct((t, ATTN_KV_W), BF16),
                   jax.ShapeDtypeStruct((t, ATTN_KV_W), BF16),
                   jax.ShapeDtypeStruct((t, D_MODEL), BF16)],
        compiler_params=_params(1),
    )(h, nw, w_in, b_in, rc, rs1, rs2)


def _attn_group(q_ref, kc_ref, kp_ref, vc_ref, vp_ref, sinks_ref, hk, n):
    hd = ATTN_HEAD_DIM
    cols = slice(hk * hd, (hk + 1) * hd)
    kcat = jnp.concatenate([kp_ref[:, cols], kc_ref[:, cols]], axis=0)
    vcat = jnp.concatenate([vp_ref[:, cols], vc_ref[:, cols]], axis=0)
    heads = [hk * ATTN_GROUP + g for g in range(ATTN_GROUP)]
    qs = jnp.concatenate([q_ref[:, hq * hd:(hq + 1) * hd] for hq in heads], axis=0)
    s = _nt(qs, kcat) * ATTN_SCALE
    rows = ATTN_GROUP * ATTN_BLOCK
    ri = lax.broadcasted_iota(jnp.int32, (rows, 2 * ATTN_BLOCK), 0) & (ATTN_BLOCK - 1)
    cj = lax.broadcasted_iota(jnp.int32, (rows, 2 * ATTN_BLOCK), 1)
    first = jnp.where(n > 0, 0, 2 * ATTN_BLOCK)
    valid = jnp.logical_or(jnp.logical_and(cj < ATTN_BLOCK, cj > ri + first),
                           jnp.logical_and(cj >= ATTN_BLOCK, cj - ATTN_BLOCK <= ri))
    s = jnp.where(valid, s, NEG_BIG)
    sink = jnp.concatenate(
        [jnp.broadcast_to(sinks_ref[:, hq:hq + 1], (ATTN_BLOCK, 1)) for hq in heads], axis=0)
    m = jnp.maximum(jnp.max(s, axis=-1, keepdims=True), sink)
    p = jnp.exp(s - m)
    esink = jnp.exp(sink - m)
    den = jnp.sum(p, axis=-1, keepdims=True) + esink
    probs = p / den
    return heads, qs, kcat, vcat, probs, esink / den


def _attn_core_fwd(q, k, v, sinks):
    t = q.shape[0]
    nb = t // ATTN_BLOCK

    def body(q_ref, kc_ref, kp_ref, vc_ref, vp_ref, sinks_ref, o_ref):
        n = pl.program_id(0)
        for hk in range(ATTN_KV_HEADS):
            heads, _, _, vcat, probs, _ = _attn_group(q_ref, kc_ref, kp_ref, vc_ref, vp_ref, sinks_ref, hk, n)
            o = _nn(probs.astype(BF16), vcat)
            for g, hq in enumerate(heads):
                o_ref[:, hq * ATTN_HEAD_DIM:(hq + 1) * ATTN_HEAD_DIM] = (
                    o[g * ATTN_BLOCK:(g + 1) * ATTN_BLOCK].astype(BF16))

    cur = lambda w: pl.BlockSpec((ATTN_BLOCK, w), lambda n: (n, 0))
    prev = lambda w: pl.BlockSpec((ATTN_BLOCK, w), lambda n: (jnp.maximum(n - 1, 0), 0))
    return pl.pallas_call(
        body, name="attn_core_fwd", grid=(nb,),
        in_specs=[cur(ATTN_Q_W), cur(ATTN_KV_W), prev(ATTN_KV_W), cur(ATTN_KV_W), prev(ATTN_KV_W),
                  pl.BlockSpec(sinks.shape, lambda n: (0, 0))],
        out_specs=cur(ATTN_Q_W),
        out_shape=jax.ShapeDtypeStruct((t, ATTN_Q_W), BF16),
        compiler_params=_params(1),
    )(q, k, k, v, v, sinks)


def _attn_core_bwd(q, k, v, do, sinks):
    t = q.shape[0]
    nb = t // ATTN_BLOCK
    hd = ATTN_HEAD_DIM

    def body(q_ref, kc_ref, kp_ref, vc_ref, vp_ref, do_ref, sinks_ref,
             dq_ref, dk_ref, dv_ref, dsink_ref, ck_ref, cv_ref):
        n = pl.program_id(0)

        @pl.when(n == 0)
        def _():
            dsink_ref[...] = jnp.zeros_like(dsink_ref)
            ck_ref[...] = jnp.zeros_like(ck_ref)
            cv_ref[...] = jnp.zeros_like(cv_ref)

        @pl.when(n == nb)
        def _():
            dk_ref[...] = ck_ref[...]
            dv_ref[...] = cv_ref[...]

        @pl.when(n < nb)
        def _():
            for hk in range(ATTN_KV_HEADS):
                heads, qs, kcat, vcat, probs, psink = _attn_group(
                    q_ref, kc_ref, kp_ref, vc_ref, vp_ref, sinks_ref, hk, n)
                dos = jnp.concatenate([do_ref[:, hq * hd:(hq + 1) * hd] for hq in heads], axis=0)
                dp = _nt(dos, vcat)
                delta = jnp.sum(probs * dp, axis=-1, keepdims=True)
                ds = (probs * (dp - delta)).astype(BF16)
                dsk = -(psink * delta)
                dqs = _nn(ds, kcat) * ATTN_SCALE
                dkc = _tn(ds, qs) * ATTN_SCALE
                dvc = _tn(probs.astype(BF16), dos)
                for g, hq in enumerate(heads):
                    blk = slice(g * ATTN_BLOCK, (g + 1) * ATTN_BLOCK)
                    dq_ref[:, hq * hd:(hq + 1) * hd] = dqs[blk]
                    dsink_ref[hq:hq + 1, :] += jnp.broadcast_to(
                        jnp.sum(dsk[blk], axis=0, keepdims=True), (1, LANES))
                cols = slice(hk * hd, (hk + 1) * hd)
                dk_ref[:, cols] = ck_ref[:, cols] + dkc[:ATTN_BLOCK]
                dv_ref[:, cols] = cv_ref[:, cols] + dvc[:ATTN_BLOCK]
                ck_ref[:, cols] = dkc[ATTN_BLOCK:]
                cv_ref[:, cols] = dvc[ATTN_BLOCK:]

    cur = lambda w: pl.BlockSpec((ATTN_BLOCK, w), lambda n: (jnp.minimum(n, nb - 1), 0))
    prev = lambda w: pl.BlockSpec((ATTN_BLOCK, w), lambda n: (jnp.clip(n - 1, 0, nb - 1), 0))
    return pl.pallas_call(
        body, name="attn_core_bwd", grid=(nb + 1,),
        in_specs=[cur(ATTN_Q_W), cur(ATTN_KV_W), prev(ATTN_KV_W), cur(ATTN_KV_W), prev(ATTN_KV_W),
                  cur(ATTN_Q_W), pl.BlockSpec(sinks.shape, lambda n: (0, 0))],
        out_specs=[cur(ATTN_Q_W), prev(ATTN_KV_W), prev(ATTN_KV_W),
                   pl.BlockSpec((ATTN_Q_HEADS, LANES), lambda n: (0, 0))],
        out_shape=[jax.ShapeDtypeStruct((t, ATTN_Q_W), F32),
                   jax.ShapeDtypeStruct((t, ATTN_KV_W), F32),
                   jax.ShapeDtypeStruct((t, ATTN_KV_W), F32),
                   jax.ShapeDtypeStruct((ATTN_Q_HEADS, LANES), F32)],
        scratch_shapes=[pltpu.VMEM((ATTN_BLOCK, ATTN_KV_W), F32),
                        pltpu.VMEM((ATTN_BLOCK, ATTN_KV_W), F32)],
        compiler_params=_params(1),
    )(q, k, k, v, v, do, sinks)


def _proj_out_fwd(x, w, b, res):
    t = x.shape[0]
    tm = _row_tile(t, 512)

    def body(x_ref, w_ref, b_ref, r_ref, o_ref):
        o_ref[...] = r_ref[...] + _nn(x_ref[...], w_ref[...]) + b_ref[...]

    row = pl.BlockSpec((tm, D_MODEL), lambda i: (i, 0))
    return pl.pallas_call(
        body, name="proj_out_fwd", grid=(t // tm,),
        in_specs=[row, pl.BlockSpec(w.shape, lambda i: (0, 0)), pl.BlockSpec(b.shape, lambda i: (0, 0)), row],
        out_specs=row,
        out_shape=jax.ShapeDtypeStruct((t, D_MODEL), F32),
        compiler_params=_params(1),
    )(x, w, b, res)


def _proj_out_bwd(dy, w):
    t = dy.shape[0]
    tm = _row_tile(t, 512)

    def body(dy_ref, w_ref, dx_ref, dyb_ref, db_ref):
        @pl.when(pl.program_id(0) == 0)
        def _():
            db_ref[...] = jnp.zeros_like(db_ref)

        dy_v = dy_ref[...]
        dyb = dy_v.astype(BF16)
        dyb_ref[...] = dyb
        dx_ref[...] = _nt(dyb, w_ref[...]).astype(BF16)
        db_ref[...] += jnp.sum(dy_v, axis=0, keepdims=True)

    row = pl.BlockSpec((tm, D_MODEL), lambda i: (i, 0))
    return pl.pallas_call(
        body, name="proj_out_bwd", grid=(t // tm,),
        in_specs=[row, pl.BlockSpec(w.shape, lambda i: (0, 0))],
        out_specs=[row, row, pl.BlockSpec((1, D_MODEL), lambda i: (0, 0))],
        out_shape=[jax.ShapeDtypeStruct((t, D_MODEL), BF16),
                   jax.ShapeDtypeStruct((t, D_MODEL), BF16),
                   jax.ShapeDtypeStruct((1, D_MODEL), F32)],
        compiler_params=_params(1),
    )(dy, w)


def _attn_qkv_bwd(dy, h, nw, w_in, dq, dk, dv, rc, rs1, rs2):
    t = h.shape[0]
    tm = _row_tile(t, ATTN_TM)

    def body(dy_ref, h_ref, nw_ref, w_ref, dq_ref, dk_ref, dv_ref, c_ref, s1_ref, s2_ref,
             dh_ref, dqkv_ref, db_ref, dnw_ref, tmp_ref):
        @pl.when(pl.program_id(0) == 0)
        def _():
            db_ref[...] = jnp.zeros_like(db_ref)
            dnw_ref[...] = jnp.zeros_like(dnw_ref)

        c, s1, s2 = c_ref[...], s1_ref[...], s2_ref[...]
        for s in range(ATTN_Q_W // LANES):
            tmp_ref[:, s * LANES:(s + 1) * LANES] = _rope_bwd(dq_ref[:, s * LANES:(s + 1) * LANES], c, s1, s2)
        for s in range(ATTN_KV_W // LANES):
            lo = ATTN_Q_W + s * LANES
            tmp_ref[:, lo:lo + LANES] = _rope_bwd(dk_ref[:, s * LANES:(s + 1) * LANES], c, s1, s2)
        tmp_ref[:, ATTN_Q_W + ATTN_KV_W:] = dv_ref[...]
        dqkv = tmp_ref[...]
        db_ref[...] += jnp.sum(dqkv, axis=0, keepdims=True)
        dqkv_b = dqkv.astype(BF16)
        dqkv_ref[...] = dqkv_b
        dx, dnw = _norm_bwd(h_ref[...], nw_ref[...], _nt(dqkv_b, w_ref[...]))
        dh_ref[...] = dy_ref[...] + dx
        dnw_ref[...] += dnw

    row = lambda w: pl.BlockSpec((tm, w), lambda i: (i, 0))
    full = lambda a: pl.BlockSpec(a.shape, lambda i: (0, 0))
    return pl.pallas_call(
        body, name="attn_qkv_bwd", grid=(t // tm,),
        in_specs=[row(D_MODEL), row(D_MODEL), full(nw), full(w_in), row(ATTN_Q_W), row(ATTN_KV_W),
                  row(ATTN_KV_W), row(LANES), row(LANES), row(LANES)],
        out_specs=[row(D_MODEL), row(ATTN_IN), pl.BlockSpec((1, ATTN_IN), lambda i: (0, 0)),
                   pl.BlockSpec((1, D_MODEL), lambda i: (0, 0))],
        out_shape=[jax.ShapeDtypeStruct((t, D_MODEL), F32),
                   jax.ShapeDtypeStruct((t, ATTN_IN), BF16),
                   jax.ShapeDtypeStruct((1, ATTN_IN), F32),
                   jax.ShapeDtypeStruct((1, D_MODEL), F32)],
        scratch_shapes=[pltpu.VMEM((tm, ATTN_IN), F32)],
        compiler_params=_params(1),
    )(dy, h, nw, w_in, dq, dk, dv, rc, rs1, rs2)


GDN_TM = 256
GDN_CONV_TM = 128


def _gdn_proj_fwd(h, nw, w_qkv, w_z, w_ba):
    t = h.shape[0]
    tm = _row_tile(t, GDN_TM)

    def body(h_ref, nw_ref, wq_ref, wz_ref, wb_ref, x_ref, z_ref, ba_ref, hn_ref):
        _, xhat = _rms_stats(h_ref[...])
        hn = (xhat * nw_ref[...]).astype(BF16)
        hn_ref[...] = hn
        x_ref[...] = _nn(hn, wq_ref[...])
        z_ref[...] = _nn(hn, wz_ref[...])
        ba_ref[...] = _nn(hn, wb_ref[...])

    row = lambda w: pl.BlockSpec((tm, w), lambda i: (i, 0))
    full = lambda a: pl.BlockSpec(a.shape, lambda i: (0, 0))
    return pl.pallas_call(
        body, name="gdn_proj_fwd", grid=(t // tm,),
        in_specs=[row(D_MODEL), full(nw), full(w_qkv), full(w_z), full(w_ba)],
        out_specs=[row(GDN_CONV_W), row(D_MODEL), row(LANES), row(D_MODEL)],
        out_shape=[jax.ShapeDtypeStruct((t, GDN_CONV_W), F32),
                   jax.ShapeDtypeStruct((t, D_MODEL), F32),
                   jax.ShapeDtypeStruct((t, LANES), F32),
                   jax.ShapeDtypeStruct((t, D_MODEL), BF16)],
        compiler_params=_params(1),
    )(h, nw, w_qkv, w_z, w_ba)


def _softplus(x):
    return jnp.maximum(x, 0.0) + jnp.log(1.0 + jnp.exp(-jnp.abs(x)))


def _conv_taps(x, halo, cw):
    tm = x.shape[0]
    xx = jnp.concatenate([halo, x], axis=0)
    taps = [xx[8 - (GDN_CONV - 1) + i: 8 - (GDN_CONV - 1) + i + tm] for i in range(GDN_CONV)]
    c = taps[0] * cw[0:1]
    for i in range(1, GDN_CONV):
        c = c + taps[i] * cw[i:i + 1]
    return c, taps


def _gates(ba, gp):
    lane = lax.broadcasted_iota(jnp.int32, ba.shape, 1)
    beta = _sigmoid(ba)
    pre = ba + gp[1:2]
    g = -jnp.exp(gp[0:1]) * _softplus(pre)
    gates = jnp.where(lane < GDN_HEADS, beta, jnp.where(lane < 2 * GDN_HEADS, g, 0.0))
    return lane, beta, pre, g, gates


def _gdn_conv_fwd(x, cw, ba, gp):
    t = x.shape[0]
    tm = _row_tile(t, GDN_CONV_TM)

    def body(x_ref, halo_ref, cw_ref, ba_ref, gp_ref, q_ref, k_ref, v_ref, gates_ref):
        halo = jnp.where(pl.program_id(0) > 0, halo_ref[...], 0.0)
        c, _ = _conv_taps(x_ref[...], halo, cw_ref[...])
        s = c * _sigmoid(c)
        for hh in range(GDN_HEADS):
            sq = s[:, hh * LANES:(hh + 1) * LANES]
            q_ref[:, hh * LANES:(hh + 1) * LANES] = (
                sq * lax.rsqrt(jnp.sum(sq * sq, axis=-1, keepdims=True) + NORM_EPS) * GDN_QSCALE)
            sk = s[:, D_MODEL + hh * LANES:D_MODEL + (hh + 1) * LANES]
            k_ref[:, hh * LANES:(hh + 1) * LANES] = (
                sk * lax.rsqrt(jnp.sum(sk * sk, axis=-1, keepdims=True) + NORM_EPS))
        v_ref[...] = s[:, 2 * D_MODEL:]
        gates_ref[...] = _gates(ba_ref[...], gp_ref[...])[4]

    row = lambda w: pl.BlockSpec((tm, w), lambda i: (i, 0))
    full = lambda a: pl.BlockSpec(a.shape, lambda i: (0, 0))
    halo = pl.BlockSpec((8, GDN_CONV_W), lambda i: (jnp.maximum(i * (tm // 8) - 1, 0), 0))
    return pl.pallas_call(
        body, name="gdn_conv_fwd", grid=(t // tm,),
        in_specs=[row(GDN_CONV_W), halo, full(cw), row(LANES), full(gp)],
        out_specs=[row(D_MODEL), row(D_MODEL), row(D_MODEL), row(LANES)],
        out_shape=[jax.ShapeDtypeStruct((t, D_MODEL), F32)] * 3 + [jax.ShapeDtypeStruct((t, LANES), F32)],
        compiler_params=_params(1),
    )(x, x, cw, ba, gp)


def _split3(x):
    hi = x.astype(BF16).astype(F32)
    r1 = x - hi
    mid = r1.astype(BF16).astype(F32)
    lo = (r1 - mid).astype(BF16).astype(F32)
    return hi, mid, lo


def _chunk_cumsum(col, keep):
    nb, c, _ = col.shape
    l3 = lax.broadcasted_iota(jnp.int32, (nb, c, LANES), 2)
    hi, mid, lo = _split3(col)
    pieces = jnp.where(l3 == 0, hi, jnp.where(l3 == 1, mid, jnp.where(l3 == 2, lo, 0.0))).astype(BF16)
    s = _bnn(jnp.where(keep, 1.0, 0.0).astype(BF16), pieces)
    return s[..., 0:1] + s[..., 1:2] + s[..., 2:3]


def _unit_inverse(nmat):
    nb, c, _ = nmat.shape
    ri = lax.broadcasted_iota(jnp.int32, (nb, c, c), 1)
    ci = lax.broadcasted_iota(jnp.int32, (nb, c, c), 2)
    eye = jnp.where(ri == ci, 1.0, 0.0).astype(F32)
    same = (ri >> GDN_SUB_SHIFT) == (ci >> GDN_SUB_SHIFT)
    nd = jnp.where(same, nmat, 0.0)
    no = nmat - nd
    mm = lambda a, b: _bnn(a.astype(BF16), b.astype(BF16))
    n2 = mm(nd, nd)
    n4 = mm(n2, n2)
    n8 = mm(n4, n4)
    td = mm(mm(mm(eye - nd, eye + n2), eye + n4), eye + n8)
    bm = mm(td, no)
    b2 = mm(bm, bm)
    return mm(mm(eye - bm, eye + b2), td)


def _chunk_terms(q_ref, k_ref, gates_ref, h, transposed):
    rows = k_ref.shape[0]
    nb = rows // GDN_CHUNK
    c = GDN_CHUNK
    lane = lax.broadcasted_iota(jnp.int32, (rows, LANES), 1)
    gt = gates_ref[...]
    beta = jnp.sum(jnp.where(lane == h, gt, 0.0), axis=-1, keepdims=True).reshape(nb, c, 1)
    g = jnp.sum(jnp.where(lane == GDN_HEADS + h, gt, 0.0), axis=-1, keepdims=True).reshape(nb, c, 1)
    ri = lax.broadcasted_iota(jnp.int32, (nb, c, c), 1)
    ci = lax.broadcasted_iota(jnp.int32, (nb, c, c), 2)
    gcol = _chunk_cumsum(g, ri >= ci)
    gamma = jnp.broadcast_to(gcol, (nb, c, LANES))
    l3 = lax.broadcasted_iota(jnp.int32, (nb, c, LANES), 2)
    gh, gm, gl = _split3(gcol)
    pmat = jnp.where(l3 == 0, gh, jnp.where(l3 == 1, gm, jnp.where(l3 == 2, gl, jnp.where(l3 < 6, 1.0, 0.0))))
    qmat = jnp.where(l3 < 3, 1.0, jnp.where(l3 == 3, -gh, jnp.where(l3 == 4, -gm, jnp.where(l3 == 5, -gl, 0.0))))
    pmat, qmat = pmat.astype(BF16), qmat.astype(BF16)
    k = k_ref[...].reshape(nb, c, LANES)
    q = q_ref[...].reshape(nb, c, LANES)
    kb = k * beta
    kbf, kbb, qb = k.astype(BF16), kb.astype(BF16), q.astype(BF16)
    out = dict(beta=beta, g=g, gamma=gamma, k=k, q=q, kb=kb, kbf=kbf, kbb=kbb, qb=qb, ri=ri, ci=ci)
    diff = _bnt(pmat, qmat)
    lmat = jnp.exp(jnp.where(ri >= ci, diff, NEG_BIG))
    out["L"] = lmat
    out["A"] = jnp.where(ri > ci, _bnt(kbb, kbf) * lmat, 0.0)
    out["Aqk"] = jnp.where(ri >= ci, _bnt(qb, kbf) * lmat, 0.0)
    if transposed:
        difft = _bnt(qmat, pmat)
        lt = jnp.exp(jnp.where(ci >= ri, difft, NEG_BIG))
        out["LT"] = lt
        out["AT"] = jnp.where(ci > ri, _bnt(kbf, kbb) * lt, 0.0)
        out["AqkT"] = jnp.where(ci >= ri, _bnt(kbf, qb) * lt, 0.0)
    return out


def _gdn_intra_fwd(q, k, v, gates):
    t = q.shape[0]
    rows = _row_tile(t, GDN_ROWS)
    nb = rows // GDN_CHUNK
    nchunks = t // GDN_CHUNK

    def body(q_ref, k_ref, v_ref, gates_ref, u_ref, w_ref, qd_ref, kd_ref, aqk_ref, cd_ref):
        h = pl.program_id(1)
        tm_ = _chunk_terms(q_ref, k_ref, gates_ref, h, False)
        gamma, beta = tm_["gamma"], tm_["beta"]
        eg = jnp.exp(gamma)
        tinv = _unit_inverse(tm_["A"])
        v3 = v_ref[...].reshape(nb, GDN_CHUNK, LANES)
        rhs = jnp.concatenate([v3 * beta, tm_["kb"] * eg], axis=-1)
        sol = _bnn(tinv.astype(BF16), rhs.astype(BF16))
        u_ref[...] = sol[..., :LANES].reshape(rows, LANES)
        w_ref[...] = sol[..., LANES:].reshape(rows, LANES).astype(BF16)
        gl = gamma[:, GDN_CHUNK - 1:GDN_CHUNK, :]
        qd_ref[...] = (tm_["q"] * eg).reshape(rows, LANES).astype(BF16)
        kd_ref[...] = (tm_["k"] * jnp.exp(gl - gamma)).reshape(rows, LANES).astype(BF16)
        aqk_ref[0] = tm_["Aqk"].reshape(rows, GDN_CHUNK).astype(BF16)
        cd_ref[0] = jnp.broadcast_to(jnp.exp(gl), (nb, 8, LANES)).reshape(nb * 8, LANES)

    blk = pl.BlockSpec((rows, LANES), lambda i, h: (i, h))
    return pl.pallas_call(
        body, name="gdn_intra_fwd", grid=(t // rows, GDN_HEADS),
        in_specs=[blk, blk, blk, pl.BlockSpec((rows, LANES), lambda i, h: (i, 0))],
        out_specs=[blk, blk, blk, blk,
                   pl.BlockSpec((1, rows, GDN_CHUNK), lambda i, h: (h, i, 0)),
                   pl.BlockSpec((1, nb * 8, LANES), lambda i, h: (h, i, 0))],
        out_shape=[jax.ShapeDtypeStruct((t, D_MODEL), F32),
                   jax.ShapeDtypeStruct((t, D_MODEL), BF16),
                   jax.ShapeDtypeStruct((t, D_MODEL), BF16),
                   jax.ShapeDtypeStruct((t, D_MODEL), BF16),
                   jax.ShapeDtypeStruct((GDN_HEADS, t, GDN_CHUNK), BF16),
                   jax.ShapeDtypeStruct((GDN_HEADS, nchunks * 8, LANES), F32)],
        compiler_params=_params(2),
    )(q, k, v, gates)


def _gdn_scan_fwd(u, w, qd, kd, aqk, cd):
    t = u.shape[0]
    rows = _row_tile(t, GDN_ROWS)
    nb = rows // GDN_CHUNK
    nchunks = t // GDN_CHUNK

    def body(u_ref, w_ref, qd_ref, kd_ref, aqk_ref, cd_ref, o_ref, st_ref, s_ref):
        @pl.when(pl.program_id(1) == 0)
        def _():
            s_ref[...] = jnp.zeros_like(s_ref)

        for c in range(nb):
            r = slice(c * GDN_CHUNK, (c + 1) * GDN_CHUNK)
            s = s_ref[...]
            st_ref[0, c * LANES:(c + 1) * LANES, :] = s
            sb = s.astype(BF16)
            vb = (u_ref[r, :] - _nn(w_ref[r, :], sb)).astype(BF16)
            o_ref[r, :] = _nn(qd_ref[r, :], sb) + _nn(aqk_ref[0, r, :], vb)
            s_ref[...] = s * cd_ref[0, c * 8:c * 8 + 1, :] + _tn(kd_ref[r, :], vb)

    blk = pl.BlockSpec((rows, LANES), lambda h, i: (i, h))
    return pl.pallas_call(
        body, name="gdn_scan_fwd", grid=(GDN_HEADS, t // rows),
        in_specs=[blk, blk, blk, blk,
                  pl.BlockSpec((1, rows, GDN_CHUNK), lambda h, i: (h, i, 0)),
                  pl.BlockSpec((1, nb * 8, LANES), lambda h, i: (h, i, 0))],
        out_specs=[blk, pl.BlockSpec((1, nb * LANES, LANES), lambda h, i: (h, i, 0))],
        out_shape=[jax.ShapeDtypeStruct((t, D_MODEL), F32),
                   jax.ShapeDtypeStruct((GDN_HEADS, nchunks * LANES, LANES), F32)],
        scratch_shapes=[pltpu.VMEM((LANES, LANES), F32)],
        compiler_params=_params(2),
    )(u, w, qd, kd, aqk, cd)


def _gdn_out_fwd(o, z, nw, w_out, res):
    t = o.shape[0]
    tm = _row_tile(t, GDN_TM)

    def body(o_ref, z_ref, nw_ref, w_ref, r_ref, out_ref, gated_ref):
        nwv = nw_ref[...]
        for hh in range(GDN_HEADS):
            sl = slice(hh * LANES, (hh + 1) * LANES)
            _, on = _rms_stats(o_ref[:, sl])
            zv = z_ref[:, sl]
            gated_ref[:, sl] = (on * nwv * (zv * _sigmoid(zv))).astype(BF16)
        out_ref[...] = r_ref[...] + _nn(gated_ref[...], w_ref[...])

    row = pl.BlockSpec((tm, D_MODEL), lambda i: (i, 0))
    full = lambda a: pl.BlockSpec(a.shape, lambda i: (0, 0))
    return pl.pallas_call(
        body, name="gdn_out_fwd", grid=(t // tm,),
        in_specs=[row, row, full(nw), full(w_out), row],
        out_specs=[row, row],
        out_shape=[jax.ShapeDtypeStruct((t, D_MODEL), F32), jax.ShapeDtypeStruct((t, D_MODEL), BF16)],
        compiler_params=_params(1),
    )(o, z, nw, w_out, res)


def _gdn_out_bwd(dy, w_out, o, z, nw):
    t = o.shape[0]
    tm = _row_tile(t, GDN_TM)

    def body(dy_ref, w_ref, o_ref, z_ref, nw_ref, do_ref, dz_ref, dyb_ref, dnw_ref, dgt_ref):
        @pl.when(pl.program_id(0) == 0)
        def _():
            dnw_ref[...] = jnp.zeros_like(dnw_ref)

        dyb = dy_ref[...].astype(BF16)
        dyb_ref[...] = dyb
        dgt_ref[...] = _nt(dyb, w_ref[...])
        nwv = nw_ref[...]
        for hh in range(GDN_HEADS):
            sl = slice(hh * LANES, (hh + 1) * LANES)
            r, on = _rms_stats(o_ref[:, sl])
            zv = z_ref[:, sl]
            sig = _sigmoid(zv)
            sz = zv * sig
            dgt = dgt_ref[:, sl]
            d_on = dgt * nwv * sz
            dz_ref[:, sl] = (dgt * on * nwv * (sig * (1.0 + zv * (1.0 - sig)))).astype(BF16)
            dnw_ref[...] += jnp.sum(dgt * on * sz, axis=0, keepdims=True)
            do_ref[:, sl] = (r * (d_on - on * jnp.mean(d_on * on, axis=-1, keepdims=True))).astype(BF16)

    row = pl.BlockSpec((tm, D_MODEL), lambda i: (i, 0))
    full = lambda a: pl.BlockSpec(a.shape, lambda i: (0, 0))
    return pl.pallas_call(
        body, name="gdn_out_bwd", grid=(t // tm,),
        in_specs=[row, full(w_out), row, row, full(nw)],
        out_specs=[row, row, row, pl.BlockSpec((1, LANES), lambda i: (0, 0))],
        out_shape=[jax.ShapeDtypeStruct((t, D_MODEL), BF16)] * 3 + [jax.ShapeDtypeStruct((1, LANES), F32)],
        scratch_shapes=[pltpu.VMEM((tm, D_MODEL), F32)],
        compiler_params=_params(1),
    )(dy, w_out, o, z, nw)


def _gdn_scan_bwd(u, w, qd, kd, aqk, cd, st, do):
    t = u.shape[0]
    rows = _row_tile(t, GDN_ROWS)
    nb = rows // GDN_CHUNK
    nchunks = t // GDN_CHUNK
    nsteps = t // rows
    cc = GDN_CHUNK

    def body(u_ref, w_ref, qd_ref, kd_ref, aqk_ref, cd_ref, st_ref, do_ref,
             du_ref, dw_ref, dqd_ref, dkd_ref, daqk_ref, daqkt_ref, dcd_ref, ds_ref):
        @pl.when(pl.program_id(1) == 0)
        def _():
            ds_ref[...] = jnp.zeros_like(ds_ref)

        ri = lax.broadcasted_iota(jnp.int32, (cc, cc), 0)
        ci = lax.broadcasted_iota(jnp.int32, (cc, cc), 1)
        for c in reversed(range(nb)):
            r = slice(c * cc, (c + 1) * cc)
            s = st_ref[0, c * LANES:(c + 1) * LANES, :]
            sb = s.astype(BF16)
            dsn = ds_ref[...]
            dsb = dsn.astype(BF16)
            wv, kdv, qdv, aq, dov = w_ref[r, :], kd_ref[r, :], qd_ref[r, :], aqk_ref[0, r, :], do_ref[r, :]
            vb = (u_ref[r, :] - _nn(wv, sb)).astype(BF16)
            dv = _tn(aq, dov) + _nn(kdv, dsb)
            dvb = dv.astype(BF16)
            daqk_ref[0, r, :] = jnp.where(ri >= ci, _nt(dov, vb), 0.0)
            daqkt_ref[0, r, :] = jnp.where(ci >= ri, _nt(vb, dov), 0.0)
            dqd_ref[r, :] = _nt(dov, sb)
            dkd_ref[r, :] = _nt(vb, dsb)
            dcd_ref[0, c * 8:(c + 1) * 8, :] = jnp.broadcast_to(jnp.sum(s * dsn), (8, LANES))
            du_ref[r, :] = dv
            dw_ref[r, :] = -_nt(dvb, sb)
            ds_ref[...] = _tn(qdv, dov) + dsn * cd_ref[0, c * 8:c * 8 + 1, :] - _tn(wv, dvb)

    rev = lambda i: nsteps - 1 - i
    blk = pl.BlockSpec((rows, LANES), lambda h, i: (rev(i), h))
    sq = pl.BlockSpec((1, rows, cc), lambda h, i: (h, rev(i), 0))
    cdb = pl.BlockSpec((1, nb * 8, LANES), lambda h, i: (h, rev(i), 0))
    return pl.pallas_call(
        body, name="gdn_scan_bwd", grid=(GDN_HEADS, nsteps),
        in_specs=[blk, blk, blk, blk, sq, cdb,
                  pl.BlockSpec((1, nb * LANES, LANES), lambda h, i: (h, rev(i), 0)), blk],
        out_specs=[blk, blk, blk, blk, sq, sq, cdb],
        out_shape=[jax.ShapeDtypeStruct((t, D_MODEL), F32)] * 4
        + [jax.ShapeDtypeStruct((GDN_HEADS, t, cc), F32)] * 2
        + [jax.ShapeDtypeStruct((GDN_HEADS, nchunks * 8, LANES), F32)],
        scratch_shapes=[pltpu.VMEM((LANES, LANES), F32)],
        compiler_params=_params(2),
    )(u, w, qd, kd, aqk, cd, st, do)


def _gdn_intra_bwd(q, k, v, gates, u, w, du, dw, dqd, dkd, daqk, daqkt, dcd):
    t = q.shape[0]
    rows = _row_tile(t, GDN_ROWS)
    nb = rows // GDN_CHUNK
    cc = GDN_CHUNK

    def body(q_ref, k_ref, v_ref, gates_ref, u_ref, w_ref, du_ref, dw_ref, dqd_ref, dkd_ref,
             daqk_ref, daqkt_ref, dcd_ref, dq_ref, dk_ref, dv_ref, dgates_ref):
        h = pl.program_id(1)

        @pl.when(h == 0)
        def _():
            dgates_ref[...] = jnp.zeros_like(dgates_ref)

        tm_ = _chunk_terms(q_ref, k_ref, gates_ref, h, True)
        gamma, beta, kk, qq, kb = tm_["gamma"], tm_["beta"], tm_["k"], tm_["q"], tm_["kb"]
        kbf, kbb, qb = tm_["kbf"], tm_["kbb"], tm_["qb"]
        lmat, lt = tm_["L"], tm_["LT"]
        ri, ci = tm_["ri"], tm_["ci"]
        r3 = lambda ref: ref[...].reshape(nb, cc, LANES)
        eg = jnp.exp(gamma)
        gl = gamma[:, cc - 1:cc, :]
        ekd = jnp.exp(gl - gamma)
        v3 = r3(v_ref)
        tt = _unit_inverse(tm_["AT"])
        dsol = jnp.concatenate([r3(du_ref), r3(dw_ref)], axis=-1)
        sol = jnp.concatenate([r3(u_ref), r3(w_ref).astype(F32)], axis=-1)
        dx = _bnn(tt.astype(BF16), dsol.astype(BF16))
        dxb, solb = dx.astype(BF16), sol.astype(BF16)
        da = jnp.where(ri > ci, -_bnt(dxb, solb), 0.0)
        dat = jnp.where(ci > ri, -_bnt(solb, dxb), 0.0)
        dxu, dxw = dx[..., :LANES], dx[..., LANES:]
        dv_ref[...] = (dxu * beta).reshape(rows, LANES)
        dbeta = jnp.sum(dxu * v3, axis=-1, keepdims=True)
        dkb = dxw * eg
        dgam = jnp.sum(dxw * kb * eg, axis=-1, keepdims=True)
        dkb = dkb + _bnn((da * lmat).astype(BF16), kbf)
        dk = _bnn((dat * lt).astype(BF16), kbb)
        dgam = dgam + jnp.sum(da * tm_["A"], axis=-1, keepdims=True) - jnp.sum(dat * tm_["AT"], axis=-1, keepdims=True)
        daq = daqk_ref[0].reshape(nb, cc, cc)
        daqt = daqkt_ref[0].reshape(nb, cc, cc)
        dq = _bnn((daq * lmat).astype(BF16), kbf)
        dk = dk + _bnn((daqt * lt).astype(BF16), qb)
        dgam = dgam + jnp.sum(daq * tm_["Aqk"], axis=-1, keepdims=True) - jnp.sum(daqt * tm_["AqkT"], axis=-1, keepdims=True)
        dqd3, dkd3 = r3(dqd_ref), r3(dkd_ref)
        dq = dq + dqd3 * eg
        dgam = dgam + jnp.sum(dqd3 * qq * eg, axis=-1, keepdims=True)
        dk = dk + dkd3 * ekd
        tk = jnp.sum(dkd3 * kk * ekd, axis=-1, keepdims=True)
        dgam = dgam - tk
        dcdv = dcd_ref[0].reshape(nb, 8, LANES)[:, 0:1, 0:1]
        dglast = jnp.sum(tk, axis=1, keepdims=True) + dcdv * jnp.exp(gl[:, :, 0:1])
        rowi = lax.broadcasted_iota(jnp.int32, (nb, cc, 1), 1)
        dgam = dgam + jnp.where(rowi == cc - 1, dglast, 0.0)
        dk = dk + dkb * beta
        dbeta = dbeta + jnp.sum(dkb * kk, axis=-1, keepdims=True)
        dg = _chunk_cumsum(dgam, ci >= ri)
        dq_ref[...] = dq.reshape(rows, LANES)
        dk_ref[...] = dk.reshape(rows, LANES)
        lane = lax.broadcasted_iota(jnp.int32, (rows, LANES), 1)
        dgates_ref[...] += (jnp.where(lane == h, dbeta.reshape(rows, 1), 0.0)
                            + jnp.where(lane == GDN_HEADS + h, dg.reshape(rows, 1), 0.0))

    blk = pl.BlockSpec((rows, LANES), lambda i, h: (i, h))
    shared = pl.BlockSpec((rows, LANES), lambda i, h: (i, 0))
    sq = pl.BlockSpec((1, rows, cc), lambda i, h: (h, i, 0))
    return pl.pallas_call(
        body, name="gdn_intra_bwd", grid=(t // rows, GDN_HEADS),
        in_specs=[blk, blk, blk, shared, blk, blk, blk, blk, blk, blk, sq, sq,
                  pl.BlockSpec((1, nb * 8, LANES), lambda i, h: (h, i, 0))],
        out_specs=[blk, blk, blk, shared],
        out_shape=[jax.ShapeDtypeStruct((t, D_MODEL), F32)] * 3 + [jax.ShapeDtypeStruct((t, LANES), F32)],
        compiler_params=_params(2),
    )(q, k, v, gates, u, w, du, dw, dqd, dkd, daqk, daqkt, dcd)


def _gdn_conv_bwd_a(x, cw, ba, gp, dq, dk, dv, dgates):
    t = x.shape[0]
    tm = _row_tile(t, GDN_CONV_TM)

    def body(x_ref, halo_ref, cw_ref, ba_ref, gp_ref, dq_ref, dk_ref, dv_ref, dgates_ref,
             dc_ref, dba_ref, dcw_ref, dgp_ref, ds_ref):
        @pl.when(pl.program_id(0) == 0)
        def _():
            dcw_ref[...] = jnp.zeros_like(dcw_ref)
            dgp_ref[...] = jnp.zeros_like(dgp_ref)

        halo = jnp.where(pl.program_id(0) > 0, halo_ref[...], 0.0)
        c, taps = _conv_taps(x_ref[...], halo, cw_ref[...])
        sig = _sigmoid(c)
        s = c * sig
        for hh in range(GDN_HEADS):
            sl = slice(hh * LANES, (hh + 1) * LANES)
            sq = s[:, sl]
            rq = lax.rsqrt(jnp.sum(sq * sq, axis=-1, keepdims=True) + NORM_EPS)
            qh = sq * rq
            dqv = dq_ref[:, sl]
            ds_ref[:, sl] = GDN_QSCALE * rq * (dqv - qh * jnp.sum(dqv * qh, axis=-1, keepdims=True))
            sl2 = slice(D_MODEL + hh * LANES, D_MODEL + (hh + 1) * LANES)
            sk = s[:, sl2]
            rk = lax.rsqrt(jnp.sum(sk * sk, axis=-1, keepdims=True) + NORM_EPS)
            kh = sk * rk
            dkv = dk_ref[:, sl]
            ds_ref[:, sl2] = rk * (dkv - kh * jnp.sum(dkv * kh, axis=-1, keepdims=True))
        ds_ref[:, 2 * D_MODEL:] = dv_ref[...]
        dc = ds_ref[...] * (sig * (1.0 + c * (1.0 - sig)))
        dc_ref[...] = dc
        for i in range(GDN_CONV):
            dcw_ref[i:i + 1, :] += jnp.sum(dc * taps[i], axis=0, keepdims=True)
        lane, beta, pre, g, _ = _gates(ba_ref[...], gp_ref[...])
        dgt = dgates_ref[...]
        db = dgt * beta * (1.0 - beta)
        dpre = dgt * (-jnp.exp(gp_ref[0:1, :])) * _sigmoid(pre)
        isa = jnp.logical_and(lane >= GDN_HEADS, lane < 2 * GDN_HEADS)
        dba_ref[...] = jnp.where(lane < GDN_HEADS, db, jnp.where(isa, dpre, 0.0)).astype(BF16)
        dgp_ref[0:1, :] += jnp.sum(jnp.where(isa, dgt * g, 0.0), axis=0, keepdims=True)
        dgp_ref[1:2, :] += jnp.sum(jnp.where(isa, dpre, 0.0), axis=0, keepdims=True)

    row = lambda w: pl.BlockSpec((tm, w), lambda i: (i, 0))
    full = lambda a: pl.BlockSpec(a.shape, lambda i: (0, 0))
    halo = pl.BlockSpec((8, GDN_CONV_W), lambda i: (jnp.maximum(i * (tm // 8) - 1, 0), 0))
    return pl.pallas_call(
        body, name="gdn_conv_bwd_a", grid=(t // tm,),
        in_specs=[row(GDN_CONV_W), halo, full(cw), row(LANES), full(gp), row(D_MODEL), row(D_MODEL),
                  row(D_MODEL), row(LANES)],
        out_specs=[row(GDN_CONV_W), row(LANES), pl.BlockSpec((8, GDN_CONV_W), lambda i: (0, 0)),
                   pl.BlockSpec((8, LANES), lambda i: (0, 0))],
        out_shape=[jax.ShapeDtypeStruct((t, GDN_CONV_W), F32),
                   jax.ShapeDtypeStruct((t, LANES), BF16),
                   jax.ShapeDtypeStruct((8, GDN_CONV_W), F32),
                   jax.ShapeDtypeStruct((8, LANES), F32)],
        scratch_shapes=[pltpu.VMEM((tm, GDN_CONV_W), F32)],
        compiler_params=_params(1),
    )(x, x, cw, ba, gp, dq, dk, dv, dgates)


def _gdn_conv_bwd_b(dc, cw):
    t = dc.shape[0]
    tm = _row_tile(t, GDN_CONV_TM)
    nsteps = t // tm

    def body(dc_ref, halo_ref, cw_ref, dx_ref):
        halo = jnp.where(pl.program_id(0) < nsteps - 1, halo_ref[...], 0.0)
        dd = jnp.concatenate([dc_ref[...], halo], axis=0)
        cw_v = cw_ref[...]
        acc = dd[GDN_CONV - 1:GDN_CONV - 1 + tm] * cw_v[0:1]
        for i in range(1, GDN_CONV):
            acc = acc + dd[GDN_CONV - 1 - i:GDN_CONV - 1 - i + tm] * cw_v[i:i + 1]
        dx_ref[...] = acc.astype(BF16)

    row = pl.BlockSpec((tm, GDN_CONV_W), lambda i: (i, 0))
    halo = pl.BlockSpec((8, GDN_CONV_W), lambda i: (jnp.minimum((i + 1) * (tm // 8), t // 8 - 1), 0))
    return pl.pallas_call(
        body, name="gdn_conv_bwd_b", grid=(nsteps,),
        in_specs=[row, halo, pl.BlockSpec(cw.shape, lambda i: (0, 0))],
        out_specs=row,
        out_shape=jax.ShapeDtypeStruct((t, GDN_CONV_W), BF16),
        compiler_params=_params(1),
    )(dc, dc, cw)


def _gdn_proj_bwd(dy, h, nw, w_qkv, w_z, w_ba, dx, dz, dba):
    t = h.shape[0]
    tm = _row_tile(t, GDN_TM)

    def body(dy_ref, h_ref, nw_ref, wq_ref, wz_ref, wb_ref, dx_ref, dz_ref, dba_ref, dh_ref, dnw_ref):
        @pl.when(pl.program_id(0) == 0)
        def _():
            dnw_ref[...] = jnp.zeros_like(dnw_ref)

        dhn = _nt(dx_ref[...], wq_ref[...]) + _nt(dz_ref[...], wz_ref[...]) + _nt(dba_ref[...], wb_ref[...])
        dxx, dnw = _norm_bwd(h_ref[...], nw_ref[...], dhn)
        dh_ref[...] = dy_ref[...] + dxx
        dnw_ref[...] += dnw

    row = lambda w: pl.BlockSpec((tm, w), lambda i: (i, 0))
    full = lambda a: pl.BlockSpec(a.shape, lambda i: (0, 0))
    return pl.pallas_call(
        body, name="gdn_proj_bwd", grid=(t // tm,),
        in_specs=[row(D_MODEL), row(D_MODEL), full(nw), full(w_qkv), full(w_z), full(w_ba),
                  row(GDN_CONV_W), row(D_MODEL), row(LANES)],
        out_specs=[row(D_MODEL), pl.BlockSpec((1, D_MODEL), lambda i: (0, 0))],
        out_shape=[jax.ShapeDtypeStruct((t, D_MODEL), F32), jax.ShapeDtypeStruct((1, D_MODEL), F32)],
        compiler_params=_params(1),
    )(dy, h, nw, w_qkv, w_z, w_ba, dx, dz, dba)


def _loss_head(h, nw, target):
    t = h.shape[0]
    tm = _row_tile(t, 512)

    def body(h_ref, nw_ref, t_ref, loss_ref, dh_ref, dnw_ref):
        @pl.when(pl.program_id(0) == 0)
        def _():
            loss_ref[...] = jnp.zeros_like(loss_ref)
            dnw_ref[...] = jnp.zeros_like(dnw_ref)

        x = h_ref[...]
        nwv = nw_ref[...]
        _, xhat = _rms_stats(x)
        err = xhat * nwv - t_ref[...]
        loss_ref[...] += 0.5 * jnp.sum(jnp.mean(err * err, axis=-1, keepdims=True))
        dx, dnw = _norm_bwd(x, nwv, err * (1.0 / D_MODEL))
        dh_ref[...] = dx
        dnw_ref[...] += dnw

    row = pl.BlockSpec((tm, D_MODEL), lambda i: (i, 0))
    return pl.pallas_call(
        body, name="loss_head", grid=(t // tm,),
        in_specs=[row, pl.BlockSpec((1, D_MODEL), lambda i: (0, 0)), row],
        out_specs=[pl.BlockSpec((8, LANES), lambda i: (0, 0)), row, pl.BlockSpec((1, D_MODEL), lambda i: (0, 0))],
        out_shape=[jax.ShapeDtypeStruct((8, LANES), F32),
                   jax.ShapeDtypeStruct((t, D_MODEL), F32),
                   jax.ShapeDtypeStruct((1, D_MODEL), F32)],
        compiler_params=_params(1),
    )(h, nw, target)


_PEER_FLIPS = [(dx, dy, dc) for dx in (0, 1) for dy in (0, 1) for dc in (0, 1)][1:]


_ANY = pl.BlockSpec(memory_space=pl.ANY)


def _exchange_copies(ins, outs, send_sems, recv_sems, local_sems, scatter):
    x, y, c = lax.axis_index("x"), lax.axis_index("y"), lax.axis_index("c")
    me = 4 * x + 2 * y + c
    copies = []
    for a in range(len(ins)):
        src = ins[a].at[me] if scatter else ins[a]
        copies.append(pltpu.make_async_copy(src, outs[a].at[me], local_sems.at[a]))
    for k, (fx, fy, fc) in enumerate(_PEER_FLIPS):
        px, py, pc = lax.rem(x + fx, 2), lax.rem(y + fy, 2), lax.rem(c + fc, 2)
        peer = 4 * px + 2 * py + pc
        for a in range(len(ins)):
            copies.append(pltpu.make_async_remote_copy(
                src_ref=ins[a].at[peer] if scatter else ins[a],
                dst_ref=outs[a].at[me],
                send_sem=send_sems.at[a, k], recv_sem=recv_sems.at[a, k],
                device_id=(px, py, pc), device_id_type=MESH_ID))
    return copies


def _exchange_shapes(arrs, scatter):
    return [jax.ShapeDtypeStruct((N_DEV,) + (a.shape[1:] if scatter else a.shape), a.dtype) for a in arrs]


def _exchange_sems(n):
    npeer = len(_PEER_FLIPS)
    return [pltpu.SemaphoreType.DMA((n, npeer)), pltpu.SemaphoreType.DMA((n, npeer)),
            pltpu.SemaphoreType.DMA((n,))]


def _exchange(arrs, scatter, name):
    n = len(arrs)

    def body(*refs):
        copies = _exchange_copies(refs[:n], refs[n:2 * n], *refs[2 * n:], scatter)
        for cp in copies:
            cp.start()
        for cp in copies:
            cp.wait()

    return pl.pallas_call(
        body, name=name, in_specs=[_ANY] * n, out_specs=[_ANY] * n,
        out_shape=_exchange_shapes(arrs, scatter), scratch_shapes=_exchange_sems(n),
    )(*arrs)


def _adamw(parts, w, m, v, name):
    r, c = w.shape
    tr = r
    for cand in (512, 256, 128, 64, 32, 16, 8):
        if r % cand == 0:
            tr = cand
            break
    c1 = 1.0 - ADAM_B1 ** ADAM_STEP
    c2 = 1.0 - ADAM_B2 ** ADAM_STEP

    def body(p_ref, w_ref, m_ref, v_ref, g_ref, d_ref, nm_ref, nv_ref):
        g = p_ref[0].astype(F32)
        for s in range(1, N_DEV):
            g = g + p_ref[s].astype(F32)
        nm = ADAM_B1 * m_ref[...] + (1.0 - ADAM_B1) * g
        nv = ADAM_B2 * v_ref[...] + (1.0 - ADAM_B2) * (g * g)
        g_ref[...] = g
        nm_ref[...] = nm
        nv_ref[...] = nv
        d_ref[...] = -ADAM_LR * ((nm / c1) / (jnp.sqrt(nv / c2) + ADAM_EPS) + ADAM_WD * w_ref[...])

    blk = pl.BlockSpec((tr, c), lambda i: (i, 0))
    return pl.pallas_call(
        body, name=name, grid=(r // tr,),
        in_specs=[pl.BlockSpec((N_DEV, tr, c), lambda i: (0, i, 0)), blk, blk, blk],
        out_specs=[blk] * 4,
        out_shape=[jax.ShapeDtypeStruct((r, c), F32)] * 4,
        compiler_params=_params(1),
    )(parts, w, m, v)


def _rope_tables(positions):
    half = ROPE_DIM // 2
    inv_freq = ROPE_THETA ** (-jnp.arange(0, ROPE_DIM, 2, dtype=F32) / ROPE_DIM)
    ang = positions.astype(F32)[:, None] * inv_freq
    cos, sin = jnp.cos(ang), jnp.sin(ang)
    t = positions.shape[0]
    zeros = lambda w: jnp.zeros((t, w), F32)
    c = jnp.concatenate([cos, cos, jnp.ones((t, ATTN_HEAD_DIM - ROPE_DIM), F32)], axis=1)
    s1 = jnp.concatenate([-sin, zeros(ATTN_HEAD_DIM - half)], axis=1)
    s2 = jnp.concatenate([zeros(half), sin, zeros(ATTN_HEAD_DIM - ROPE_DIM)], axis=1)
    return tuple(jnp.tile(a, (1, LANES // ATTN_HEAD_DIM)) for a in (c, s1, s2))


def _ffn_layer_fwd(h, nw, w, carry):
    (out, g, u, hn), landed = _ffn_fwd(h, nw, w["wg"], w["wu"], w["wd"], carry, False)
    return out, (h, g, u, hn), landed


def _ffn_layer_bwd(dy, saved, nw, w, carry):
    h, g, u, hn = saved
    (dh, dg, du, act, dz, dnw), landed = _ffn_bwd(dy, h, nw, g, u, w["wg"], w["wu"], w["wd"], carry, True)
    dwg = _matmul_tn(hn, dg, FFN_TF, "ffn_dwg")
    dwu = _matmul_tn(hn, du, FFN_TF, "ffn_dwu")
    dwd = _matmul_tn(act, dz, D_MODEL, "ffn_dwd")
    return dh, dnw, (dwg, dwu, dwd), landed


def _attn_layer_fwd(h, nw, w, ropes):
    q, k, v, hn = _attn_qkv_fwd(h, nw, w["w_in"], w["b_in"], *ropes)
    o = _attn_core_fwd(q, k, v, w["sinks"])
    out = _proj_out_fwd(o, w["w_out"], w["b_out"], h)
    return out, (h, hn, q, k, v, o)


def _attn_layer_bwd(dy, saved, nw, w, ropes):
    h, hn, q, k, v, o = saved
    do, dyb, db_out = _proj_out_bwd(dy, w["w_out"])
    dw_out = _matmul_tn(o, dyb, D_MODEL, "attn_dw_out")
    dq, dk, dv, dsink = _attn_core_bwd(q, k, v, do, w["sinks"])
    dh, dqkv, db_in, dnw = _attn_qkv_bwd(dy, h, nw, w["w_in"], dq, dk, dv, *ropes)
    dw_in = _matmul_tn(hn, dqkv, ATTN_IN, "attn_dw_in")
    return dh, dnw, dict(w_in=dw_in, b_in=db_in, sinks=dsink[:, 0], w_out=dw_out, b_out=db_out)


def _gdn_layer_fwd(h, nw, w):
    x, z, ba, hn = _gdn_proj_fwd(h, nw, w["w_qkv"], w["w_z"], w["w_ba"])
    q, k, v, gates = _gdn_conv_fwd(x, w["conv_w"], ba, w["gp"])
    u, ww, qd, kd, aqk, cd = _gdn_intra_fwd(q, k, v, gates)
    o, st = _gdn_scan_fwd(u, ww, qd, kd, aqk, cd)
    out, gated = _gdn_out_fwd(o, z, w["norm_w"], w["w_out"], h)
    return out, (h, hn, x, z, ba, q, k, v, gates, u, ww, qd, kd, aqk, cd, st, o, gated)


def _gdn_layer_bwd(dy, saved, nw, w):
    h, hn, x, z, ba, q, k, v, gates, u, ww, qd, kd, aqk, cd, st, o, gated = saved
    do, dz, dyb, dnorm_w = _gdn_out_bwd(dy, w["w_out"], o, z, w["norm_w"])
    dw_out = _matmul_tn(gated, dyb, D_MODEL, "gdn_dw_out")
    du, dw, dqd, dkd, daqk, daqkt, dcd = _gdn_scan_bwd(u, ww, qd, kd, aqk, cd, st, do)
    dq, dk, dv, dgates = _gdn_intra_bwd(q, k, v, gates, u, ww, du, dw, dqd, dkd, daqk, daqkt, dcd)
    dc, dba, dcw, dgp = _gdn_conv_bwd_a(x, w["conv_w"], ba, w["gp"], dq, dk, dv, dgates)
    dx = _gdn_conv_bwd_b(dc, w["conv_w"])
    dh, dnw = _gdn_proj_bwd(dy, h, nw, w["w_qkv"], w["w_z"], w["w_ba"], dx, dz, dba)
    dw_qkv = _matmul_tn(hn, dx, GDN_CONV_W // 2, "gdn_dw_qkv")
    dw_z = _matmul_tn(hn, dz, D_MODEL, "gdn_dw_z")
    dw_ba = _matmul_tn(hn, dba, LANES, "gdn_dw_ba")
    dw_in = jnp.concatenate([dw_qkv, dw_z, dw_ba[:, :2 * GDN_HEADS]], axis=1)
    lo, hi = GDN_HEADS, 2 * GDN_HEADS
    return dh, dnw, dict(w_in=dw_in, conv_w=dcw[:GDN_CONV], A_log=dgp[0, lo:hi], dt_bias=dgp[1, lo:hi],
                         norm_w=dnorm_w[0], w_out=dw_out)


def _local_step(x, positions, target, norms, final_norm, plan):
    ropes = _rope_tables(positions)
    h = x
    saved = []
    for layer in range(DEPTH):
        h, s1, landed = _ffn_layer_fwd(h, norms["ffn1"][layer], plan.weights("ffn1", layer),
                                       plan.fwd_carry("ffn1", layer))
        plan.fwd_landed(landed)
        if layer % 2 == 0:
            h, s2 = _attn_layer_fwd(h, norms["mix"][layer], plan.weights("mix", layer), ropes)
        else:
            h, s2 = _gdn_layer_fwd(h, norms["mix"][layer], plan.weights("mix", layer))
        h, s3, landed = _ffn_layer_fwd(h, norms["ffn2"][layer], plan.weights("ffn2", layer),
                                       plan.fwd_carry("ffn2", layer))
        plan.fwd_landed(landed)
        saved.append((s1, s2, s3))
    loss, dh, dfinal = _loss_head(h, final_norm, target)

    g_norm = {k: [None] * DEPTH for k in ("ffn1", "mix", "ffn2")}
    for layer in reversed(range(DEPTH)):
        s1, s2, s3 = saved[layer]
        dh, g_norm["ffn2"][layer], gw, landed = _ffn_layer_bwd(
            dh, s3, norms["ffn2"][layer], plan.weights("ffn2", layer), plan.bwd_carry("ffn2", layer))
        plan.bwd_landed(landed)
        plan.grads("ffn2", layer, gw)
        if layer % 2 == 0:
            dh, g_norm["mix"][layer], gw = _attn_layer_bwd(dh, s2, norms["mix"][layer], plan.weights("mix", layer),
                                                           ropes)
        else:
            dh, g_norm["mix"][layer], gw = _gdn_layer_bwd(dh, s2, norms["mix"][layer], plan.weights("mix", layer))
        plan.grads("mix", layer, gw)
        dh, g_norm["ffn1"][layer], gw, landed = _ffn_layer_bwd(
            dh, s1, norms["ffn1"][layer], plan.weights("ffn1", layer), plan.bwd_carry("ffn1", layer))
        plan.bwd_landed(landed)
        plan.grads("ffn1", layer, gw)
    return loss, dh, dfinal, g_norm


def _cols_full(g):
    return jnp.transpose(g, (1, 0, 2)).reshape(g.shape[1], -1)


def _rows_full(g):
    return g.reshape(-1, g.shape[-1])


class _ShardedWeights:
    def __init__(self, shards, small):
        self.shards, self.small = shards, small
        self.whole, self.pending, self.received, self.small_grads = {}, {}, {}, {}
        self.in_flight = []
        first = [("ffn1", 0), ("mix", 0)]
        self.in_flight = first
        self.fwd_landed(_exchange(self._shards_of(first), False, "gather_first"))

    def _group(self, key):
        kind, layer = key
        j = layer // 2
        if kind != "mix":
            return [self.shards[kind + "_w_gate_up"][layer], self.shards[kind + "_w_down"][layer]]
        if layer % 2 == 0:
            return [self.shards["attn_w_in"][j], self.shards["attn_w_out"][j]]
        return [self.shards["gdn_w_in"][j], self.shards["gdn_w_out"][j], self.shards["gdn_conv_w"][j]]

    def _shards_of(self, keys):
        return [a for key in keys for a in self._group(key)]

    def weights(self, kind, layer):
        return self.whole[(kind, layer)]

    def fwd_carry(self, kind, layer):
        if kind == "ffn1":
            keys = [("ffn2", layer)]
        else:
            keys = [("ffn1", layer + 1), ("mix", layer + 1)] if layer + 1 < DEPTH else []
        self.in_flight = keys
        return self._shards_of(keys)

    def fwd_landed(self, landed):
        landed = list(landed)
        for key in self.in_flight:
            kind, layer = key
            j = layer // 2
            sm = self.small
            if kind != "mix":
                gu, dn = landed[:2]
                w = dict(wg=_cols_full(gu[:4]), wu=_cols_full(gu[4:]), wd=_rows_full(dn))
                landed = landed[2:]
            elif layer % 2 == 0:
                w = dict(w_in=_cols_full(landed[0]), w_out=_rows_full(landed[1]), b_in=sm["attn_b_in"][j][None],
                         sinks=sm["attn_sinks"][j][None], b_out=sm["attn_b_out"][j][None])
                landed = landed[2:]
            else:
                w_in = _cols_full(landed[0])
                w_ba = jnp.pad(w_in[:, GDN_CONV_W + D_MODEL:], ((0, 0), (0, LANES - 2 * GDN_HEADS)))
                gp = jnp.pad(jnp.stack([sm["gdn_A_log"][j], sm["gdn_dt_bias"][j]]),
                             ((0, 6), (GDN_HEADS, LANES - 2 * GDN_HEADS)))
                w = dict(w_qkv=w_in[:, :GDN_CONV_W], w_z=w_in[:, GDN_CONV_W:GDN_CONV_W + D_MODEL], w_ba=w_ba,
                         conv_w=_cols_full(landed[2]), gp=gp, norm_w=sm["gdn_norm_w"][j][None],
                         w_out=_rows_full(landed[1]))
                landed = landed[3:]
            self.whole[key] = w
        self.in_flight = []

    def grads(self, kind, layer, g):
        if kind != "mix":
            dwg, dwu, dwd = g
            parts = [jnp.concatenate([_cols_shards(dwg, 4), _cols_shards(dwu, 4)], axis=0).astype(BF16),
                     _rows_shards(dwd).astype(BF16)]
        else:
            parts = [_cols_shards(g["w_in"]).astype(BF16), _rows_shards(g["w_out"]).astype(BF16)]
            if layer % 2 == 1:
                parts.append(_cols_shards(g["conv_w"]))
            self.small_grads[layer] = g
        self.pending[(kind, layer)] = parts

    def bwd_carry(self, kind, layer):
        if kind == "ffn1":
            keys = [("ffn2", layer), ("mix", layer)]
        else:
            keys = [("ffn1", layer + 1)] if layer + 1 < DEPTH else []
        self.in_flight = keys
        return [a for key in keys for a in self.pending[key]]

    def bwd_landed(self, landed):
        landed = list(landed)
        for key in self.in_flight:
            n = len(self.pending.pop(key))
            self.received[key], landed = landed[:n], landed[n:]
        self.in_flight = []

    def finish(self):
        self.in_flight = list(self.pending)
        self.bwd_landed(_exchange([a for key in self.in_flight for a in self.pending[key]], True, "scatter_last"))


def _cols_shards(full, n=N_DEV):
    r = full.shape[0]
    return jnp.transpose(full.reshape(r, n, -1), (1, 0, 2))


def _rows_shards(full):
    return full.reshape(N_DEV, -1, full.shape[-1])


_SMALL = ("ffn1_norm", "mix_norm", "ffn2_norm", "attn_b_in", "attn_sinks", "attn_b_out",
          "gdn_A_log", "gdn_dt_bias", "gdn_norm_w", "final_norm")
_BIG = ("ffn1_w_gate_up", "ffn1_w_down", "ffn2_w_gate_up", "ffn2_w_down", "attn_w_in", "attn_w_out",
        "gdn_w_in", "gdn_conv_w", "gdn_w_out")
_WEIGHTS = ("ffn1_norm", "ffn1_w_gate_up", "ffn1_w_down", "mix_norm", "ffn2_norm", "ffn2_w_gate_up",
            "ffn2_w_down", "attn_w_in", "attn_b_in", "attn_sinks", "attn_w_out", "attn_b_out", "gdn_w_in",
            "gdn_conv_w", "gdn_A_log", "gdn_dt_bias", "gdn_norm_w", "gdn_w_out", "final_norm")


def _pack_small(vals):
    flat = jnp.concatenate([v.reshape(-1).astype(F32) for v in vals])
    pad = (-flat.shape[0]) % (8 * LANES)
    return jnp.pad(flat, (0, pad)).reshape(-1, LANES)


def _unpack_small(packed, shapes):
    flat = packed.reshape(-1)
    out, off = [], 0
    for s in shapes:
        size = 1
        for d in s:
            size *= d
        out.append(flat[off:off + size].reshape(s))
        off += size
    return out


def kernel(x, positions, ffn1_norm, ffn1_w_gate_up, ffn1_w_down, mix_norm, ffn2_norm, ffn2_w_gate_up, ffn2_w_down, attn_w_in, attn_b_in, attn_sinks, attn_w_out, attn_b_out, gdn_w_in, gdn_conv_w, gdn_A_log, gdn_dt_bias, gdn_norm_w, gdn_w_out, final_norm, loss_target, m_ffn1_norm, m_ffn1_w_gate_up, m_ffn1_w_down, m_mix_norm, m_ffn2_norm, m_ffn2_w_gate_up, m_ffn2_w_down, m_attn_w_in, m_attn_b_in, m_attn_sinks, m_attn_w_out, m_attn_b_out, m_gdn_w_in, m_gdn_conv_w, m_gdn_A_log, m_gdn_dt_bias, m_gdn_norm_w, m_gdn_w_out, m_final_norm, v_ffn1_norm, v_ffn1_w_gate_up, v_ffn1_w_down, v_mix_norm, v_ffn2_norm, v_ffn2_w_gate_up, v_ffn2_w_down, v_attn_w_in, v_attn_b_in, v_attn_sinks, v_attn_w_out, v_attn_b_out, v_gdn_w_in, v_gdn_conv_w, v_gdn_A_log, v_gdn_dt_bias, v_gdn_norm_w, v_gdn_w_out, v_final_norm):
    args = dict(locals())
    wts = {n: args[n] for n in _WEIGHTS}
    moms = {n: args["m_" + n] for n in _WEIGHTS}
    vels = {n: args["v_" + n] for n in _WEIGHTS}

    shards = {n: wts[n] if n == "gdn_conv_w" else wts[n].astype(BF16) for n in _BIG}
    plan = _ShardedWeights(shards, wts)
    norms = dict(ffn1=ffn1_norm[:, None, :], mix=mix_norm[:, None, :], ffn2=ffn2_norm[:, None, :])
    loss, grad_x, dfinal, g_norm = _local_step(x[0], positions[0], loss_target[0], norms, final_norm[None], plan)
    plan.finish()

    layers = range(DEPTH)
    got = lambda kind, ls, i: jnp.stack([plan.received[(kind, l)][i] for l in ls], axis=1)
    received = {"attn_w_in": got("mix", layers[0::2], 0), "attn_w_out": got("mix", layers[0::2], 1),
                "gdn_w_in": got("mix", layers[1::2], 0), "gdn_w_out": got("mix", layers[1::2], 1),
                "gdn_conv_w": got("mix", layers[1::2], 2)}
    for kind in ("ffn1", "ffn2"):
        received[kind + "_w_gate_up"] = got(kind, layers, 0)
        received[kind + "_w_down"] = got(kind, layers, 1)
    g_attn = [plan.small_grads[l] for l in layers[0::2]]
    g_gdn = [plan.small_grads[l] for l in layers[1::2]]


    small_g = dict(
        ffn1_norm=jnp.concatenate(g_norm["ffn1"], axis=0), mix_norm=jnp.concatenate(g_norm["mix"], axis=0),
        ffn2_norm=jnp.concatenate(g_norm["ffn2"], axis=0),
        attn_b_in=jnp.concatenate([g["b_in"] for g in g_attn], axis=0),
        attn_sinks=jnp.stack([g["sinks"] for g in g_attn]),
        attn_b_out=jnp.concatenate([g["b_out"] for g in g_attn], axis=0),
        gdn_A_log=jnp.stack([g["A_log"] for g in g_gdn]), gdn_dt_bias=jnp.stack([g["dt_bias"] for g in g_gdn]),
        gdn_norm_w=jnp.stack([g["norm_w"] for g in g_gdn]), final_norm=dfinal[0])
    small_parts = _exchange([_pack_small([small_g[n] for n in _SMALL] + [loss[0, :1]])], False, "gather_small")[0]
    pad1 = jnp.zeros((1,), F32)
    sw = _pack_small([wts[n] for n in _SMALL] + [pad1])
    sm = _pack_small([moms[n] for n in _SMALL] + [pad1])
    sv = _pack_small([vels[n] for n in _SMALL] + [pad1])
    shapes = [wts[n].shape for n in _SMALL] + [(1,)]
    small_out = [_unpack_small(o, shapes) for o in _adamw(small_parts, sw, sm, sv, "adamw_small")]
    results = {n: tuple(o[i] for o in small_out) for i, n in enumerate(_SMALL)}
    loss_total = small_out[0][-1][0]

    for n in _BIG:
        shape = wts[n].shape
        two_d = lambda a: a.reshape(-1, shape[-1])
        p = received[n].reshape(N_DEV, -1, shape[-1])
        outs = _adamw(p, two_d(wts[n]), two_d(moms[n]), two_d(vels[n]), "adamw_" + n)
        results[n] = tuple(o.reshape(shape) for o in outs)

    return (loss_total, grad_x[None],
            *[results[n][0] for n in _WEIGHTS], *[results[n][1] for n in _WEIGHTS],
            *[results[n][2] for n in _WEIGHTS], *[results[n][3] for n in _WEIGHTS])
```

```python
import jax
import jax.numpy as jnp
from jax import lax
from jax.experimental import pallas as pl
from jax.experimental.pallas import tpu as pltpu

F32 = jnp.float32
BF16 = jnp.bfloat16

D_MODEL = 1024
DEPTH = 4
D_FF = 2816
NORM_EPS = 1e-6
N_DEV = 8

ATTN_Q_HEADS = 16
ATTN_KV_HEADS = 4
ATTN_HEAD_DIM = 64
ATTN_GROUP = 4
ATTN_BLOCK = 128
ROPE_DIM = 16
ROPE_THETA = 500000.0
ATTN_Q_W = 1024
ATTN_KV_W = 256
ATTN_IN = 1536
ATTN_SCALE = ATTN_HEAD_DIM ** -0.5

GDN_HEADS = 8
GDN_DK = 128
GDN_CONV = 4
GDN_CHUNK = 64
GDN_CONV_W = 3072
GDN_IN = 4112
GDN_QSCALE = GDN_DK ** -0.5
GDN_ROWS = 512
GDN_SUB_SHIFT = 4

ADAM_LR = 0.001
ADAM_B1 = 0.9
ADAM_B2 = 0.999
ADAM_EPS = 1e-08
ADAM_WD = 0.01
ADAM_STEP = 10

LANES = 128
NEG_BIG = -1e30
VMEM_LIMIT_BYTES = 56 * 1024 * 1024
MESH_ID = pl.DeviceIdType.MESH


def _params(n_axes):
    return pltpu.CompilerParams(dimension_semantics=("arbitrary",) * n_axes,
                                vmem_limit_bytes=VMEM_LIMIT_BYTES)


def _nn(a, b):
    return jnp.dot(a, b, preferred_element_type=F32)


def _nt(a, b):
    return lax.dot_general(a, b, (((1,), (1,)), ((), ())), preferred_element_type=F32)


def _tn(a, b):
    return lax.dot_general(a, b, (((0,), (0,)), ((), ())), preferred_element_type=F32)


def _bnn(a, b, precision=None):
    return lax.dot_general(a, b, (((2,), (1,)), ((0,), (0,))), precision=precision,
                           preferred_element_type=F32)


def _bnt(a, b, precision=None):
    return lax.dot_general(a, b, (((2,), (2,)), ((0,), (0,))), precision=precision,
                           preferred_element_type=F32)


def _sigmoid(x):
    return 1.0 / (1.0 + jnp.exp(-x))


def _rms_stats(x):
    r = lax.rsqrt(jnp.mean(x * x, axis=-1, keepdims=True) + NORM_EPS)
    return r, x * r


def _norm_bwd(x, nw, dhn):
    r, xhat = _rms_stats(x)
    dxh = dhn * nw
    dx = r * (dxh - xhat * jnp.mean(dxh * xhat, axis=-1, keepdims=True))
    dnw = jnp.sum(dhn * xhat, axis=0, keepdims=True)
    return dx, dnw


def _row_tile(t, pref):
    return min(t, pref)


FFN_TM = 512
FFN_BWD_TM = 256
FFN_TF = 1408


def _carried(body, n_in, n_out, carry, scatter, last_step):
    nc = len(carry)
    if not nc:
        return body

    def wrapped(*refs):
        ins, cin = refs[:n_in], refs[n_in:n_in + nc]
        outs = refs[n_in + nc:n_in + nc + n_out]
        cout = refs[n_in + nc + n_out:n_in + 2 * nc + n_out]
        scratch = refs[n_in + 2 * nc + n_out:]
        sems = scratch[len(scratch) - 3:]
        i, j = pl.program_id(0), pl.program_id(1)

        @pl.when(jnp.logical_and(i == 0, j == 0))
        def _():
            for cp in _exchange_copies(cin, cout, *sems, scatter):
                cp.start()

        body(*ins, *outs, *scratch[:len(scratch) - 3])

        @pl.when(jnp.logical_and(i == last_step[0], j == last_step[1]))
        def _():
            for cp in _exchange_copies(cin, cout, *sems, scatter):
                cp.wait()

    return wrapped


def _ffn_fwd(h, nw, wg, wu, wd, carry=(), scatter=False):
    t = h.shape[0]
    tm, tf = _row_tile(t, FFN_TM), FFN_TF
    nj = D_FF // tf
    nc = len(carry)

    def body(h_ref, nw_ref, wg_ref, wu_ref, wd_ref, out_ref, g_ref, u_ref, hn_ref, acc_ref):
        j = pl.program_id(1)

        @pl.when(j == 0)
        def _():
            _, xhat = _rms_stats(h_ref[...])
            hn_ref[...] = (xhat * nw_ref[...]).astype(BF16)
            acc_ref[...] = jnp.zeros_like(acc_ref)

        hn = hn_ref[...]
        g = _nn(hn, wg_ref[...])
        u = _nn(hn, wu_ref[...])
        g_ref[...] = g.astype(BF16)
        u_ref[...] = u.astype(BF16)
        a = (g * _sigmoid(g) * u).astype(BF16)
        acc_ref[...] += _nn(a, wd_ref[...])

        @pl.when(j == nj - 1)
        def _():
            out_ref[...] = h_ref[...] + 0.5 * acc_ref[...]

    outs = pl.pallas_call(
        _carried(body, 5, 4, carry, scatter, (t // tm - 1, nj - 1)),
        name="ffn_fwd_x" if nc else "ffn_fwd", grid=(t // tm, nj),
        in_specs=[pl.BlockSpec((tm, D_MODEL), lambda i, j: (i, 0)),
                  pl.BlockSpec((1, D_MODEL), lambda i, j: (0, 0)),
                  pl.BlockSpec((D_MODEL, tf), lambda i, j: (0, j)),
                  pl.BlockSpec((D_MODEL, tf), lambda i, j: (0, j)),
                  pl.BlockSpec((tf, D_MODEL), lambda i, j: (j, 0))] + [_ANY] * nc,
        out_specs=[pl.BlockSpec((tm, D_MODEL), lambda i, j: (i, 0)),
                   pl.BlockSpec((tm, tf), lambda i, j: (i, j)),
                   pl.BlockSpec((tm, tf), lambda i, j: (i, j)),
                   pl.BlockSpec((tm, D_MODEL), lambda i, j: (i, 0))] + [_ANY] * nc,
        out_shape=[jax.ShapeDtypeStruct((t, D_MODEL), F32),
                   jax.ShapeDtypeStruct((t, D_FF), BF16),
                   jax.ShapeDtypeStruct((t, D_FF), BF16),
                   jax.ShapeDtypeStruct((t, D_MODEL), BF16)] + _exchange_shapes(carry, scatter),
        scratch_shapes=[pltpu.VMEM((tm, D_MODEL), F32)] + (_exchange_sems(nc) if nc else []),
        compiler_params=_params(2),
    )(h, nw, wg, wu, wd, *carry)
    return outs[:4], outs[4:]


def _ffn_bwd(dy, h, nw, g, u, wg, wu, wd, carry=(), scatter=True):
    t = h.shape[0]
    tm, tf = _row_tile(t, FFN_BWD_TM), FFN_TF
    nj = D_FF // tf
    nc = len(carry)

    def body(dy_ref, h_ref, nw_ref, g_ref, u_ref, wg_ref, wu_ref, wd_ref,
             dh_ref, dg_ref, du_ref, a_ref, dz_ref, dnw_ref, acc_ref):
        i, j = pl.program_id(0), pl.program_id(1)

        @pl.when(j == 0)
        def _():
            dz_ref[...] = (0.5 * dy_ref[...]).astype(BF16)
            acc_ref[...] = jnp.zeros_like(acc_ref)

        @pl.when(jnp.logical_and(i == 0, j == 0))
        def _():
            dnw_ref[...] = jnp.zeros_like(dnw_ref)

        da = _nt(dz_ref[...], wd_ref[...])
        gv = g_ref[...].astype(F32)
        uv = u_ref[...].astype(F32)
        sig = _sigmoid(gv)
        silu = gv * sig
        dg = (da * uv * (sig * (1.0 + gv * (1.0 - sig)))).astype(BF16)
        du = (da * silu).astype(BF16)
        dg_ref[...] = dg
        du_ref[...] = du
        a_ref[...] = (silu * uv).astype(BF16)
        acc_ref[...] += _nt(dg, wg_ref[...]) + _nt(du, wu_ref[...])

        @pl.when(j == nj - 1)
        def _():
            dx, dnw = _norm_bwd(h_ref[...], nw_ref[...], acc_ref[...])
            dh_ref[...] = dy_ref[...] + dx
            dnw_ref[...] += dnw

    outs = pl.pallas_call(
        _carried(body, 8, 6, carry, scatter, (t // tm - 1, nj - 1)),
        name="ffn_bwd_x" if nc else "ffn_bwd", grid=(t // tm, nj),
        in_specs=[pl.BlockSpec((tm, D_MODEL), lambda i, j: (i, 0)),
                  pl.BlockSpec((tm, D_MODEL), lambda i, j: (i, 0)),
                  pl.BlockSpec((1, D_MODEL), lambda i, j: (0, 0)),
                  pl.BlockSpec((tm, tf), lambda i, j: (i, j)),
                  pl.BlockSpec((tm, tf), lambda i, j: (i, j)),
                  pl.BlockSpec((D_MODEL, tf), lambda i, j: (0, j)),
                  pl.BlockSpec((D_MODEL, tf), lambda i, j: (0, j)),
                  pl.BlockSpec((tf, D_MODEL), lambda i, j: (j, 0))] + [_ANY] * nc,
        out_specs=[pl.BlockSpec((tm, D_MODEL), lambda i, j: (i, 0)),
                   pl.BlockSpec((tm, tf), lambda i, j: (i, j)),
                   pl.BlockSpec((tm, tf), lambda i, j: (i, j)),
                   pl.BlockSpec((tm, tf), lambda i, j: (i, j)),
                   pl.BlockSpec((tm, D_MODEL), lambda i, j: (i, 0)),
                   pl.BlockSpec((1, D_MODEL), lambda i, j: (0, 0))] + [_ANY] * nc,
        out_shape=[jax.ShapeDtypeStruct((t, D_MODEL), F32),
                   jax.ShapeDtypeStruct((t, D_FF), BF16),
                   jax.ShapeDtypeStruct((t, D_FF), BF16),
                   jax.ShapeDtypeStruct((t, D_FF), BF16),
                   jax.ShapeDtypeStruct((t, D_MODEL), BF16),
                   jax.ShapeDtypeStruct((1, D_MODEL), F32)] + _exchange_shapes(carry, scatter),
        scratch_shapes=[pltpu.VMEM((tm, D_MODEL), F32)] + (_exchange_sems(nc) if nc else []),
        compiler_params=_params(2),
    )(dy, h, nw, g, u, wg, wu, wd, *carry)
    return outs[:6], outs[6:]


def _matmul_tn(a, b, tn, name):
    k, m = a.shape
    n = b.shape[1]
    tm = min(m, 1408 if m % 1408 == 0 else 1024)
    tk = min(k, 1024)

    def body(a_ref, b_ref, o_ref):
        @pl.when(pl.program_id(2) == 0)
        def _():
            o_ref[...] = jnp.zeros_like(o_ref)

        o_ref[...] += _tn(a_ref[...], b_ref[...])

    return pl.pallas_call(
        body, name=name, grid=(m // tm, n // tn, k // tk),
        in_specs=[pl.BlockSpec((tk, tm), lambda i, j, kk: (kk, i)),
                  pl.BlockSpec((tk, tn), lambda i, j, kk: (kk, j))],
        out_specs=pl.BlockSpec((tm, tn), lambda i, j, kk: (i, j)),
        out_shape=jax.ShapeDtypeStruct((m, n), F32),
        compiler_params=_params(3),
    )(a, b)


ATTN_TM = 512


def _rope(t, c, s1, s2):
    return t * c + pltpu.roll(t, LANES - ROPE_DIM // 2, 1) * s1 + pltpu.roll(t, ROPE_DIM // 2, 1) * s2


def _rope_bwd(d, c, s1, s2):
    return d * c + pltpu.roll(d * s1, ROPE_DIM // 2, 1) + pltpu.roll(d * s2, LANES - ROPE_DIM // 2, 1)


def _attn_qkv_fwd(h, nw, w_in, b_in, rc, rs1, rs2):
    t = h.shape[0]
    tm = _row_tile(t, ATTN_TM)

    def body(h_ref, nw_ref, w_ref, b_ref, c_ref, s1_ref, s2_ref, q_ref, k_ref, v_ref, hn_ref):
        _, xhat = _rms_stats(h_ref[...])
        hn = (xhat * nw_ref[...]).astype(BF16)
        hn_ref[...] = hn
        qkv = _nn(hn, w_ref[...]) + b_ref[...]
        c, s1, s2 = c_ref[...], s1_ref[...], s2_ref[...]
        for s in range(ATTN_Q_W // LANES):
            q_ref[:, s * LANES:(s + 1) * LANES] = _rope(qkv[:, s * LANES:(s + 1) * LANES], c, s1, s2).astype(BF16)
        for s in range(ATTN_KV_W // LANES):
            lo = ATTN_Q_W + s * LANES
            k_ref[:, s * LANES:(s + 1) * LANES] = _rope(qkv[:, lo:lo + LANES], c, s1, s2).astype(BF16)
        v_ref[...] = qkv[:, ATTN_Q_W + ATTN_KV_W:].astype(BF16)

    row = lambda w: pl.BlockSpec((tm, w), lambda i: (i, 0))
    full = lambda a: pl.BlockSpec(a.shape, lambda i: (0, 0))
    return pl.pallas_call(
        body, name="attn_qkv_fwd", grid=(t // tm,),
        in_specs=[row(D_MODEL), full(nw), full(w_in), full(b_in), row(LANES), row(LANES), row(LANES)],
        out_specs=[row(ATTN_Q_W), row(ATTN_KV_W), row(ATTN_KV_W), row(D_MODEL)],
        out_shape=[jax.ShapeDtypeStruct((t, ATTN_Q_W), BF16),
                   jax.ShapeDtypeStruct((t, ATTN_KV_W), BF16),
                   jax.ShapeDtypeStruct((t, ATTN_KV_W), BF16),
                   jax.ShapeDtypeStruct((t, D_MODEL), BF16)],
        compiler_params=_params(1),
    )(h, nw, w_in, b_in, rc, rs1, rs2)


def _attn_group(q_ref, kc_ref, kp_ref, vc_ref, vp_ref, sinks_ref, hk, n):
    hd = ATTN_HEAD_DIM
    cols = slice(hk * hd, (hk + 1) * hd)
    krow = lax.broadcasted_iota(jnp.int32, (2 * ATTN_BLOCK, hd), 0)
    kcat = jnp.concatenate([kp_ref[:, cols], kc_ref[:, cols]], axis=0)
    vcat = jnp.concatenate([vp_ref[:, cols], vc_ref[:, cols]], axis=0)
    kcat = jnp.where(krow == 0, jnp.zeros_like(kcat), kcat)
    vcat = jnp.where(krow == 0, jnp.zeros_like(vcat), vcat)
    heads = [hk * ATTN_GROUP + g for g in range(ATTN_GROUP)]
    qs = jnp.concatenate([q_ref[:, hq * hd:(hq + 1) * hd] for hq in heads], axis=0)
    s = _nt(qs, kcat) * ATTN_SCALE
    rows = ATTN_GROUP * ATTN_BLOCK
    ri = lax.broadcasted_iota(jnp.int32, (rows, 2 * ATTN_BLOCK), 0) & (ATTN_BLOCK - 1)
    cj = lax.broadcasted_iota(jnp.int32, (rows, 2 * ATTN_BLOCK), 1)
    first = jnp.where(n > 0, 0, 2 * ATTN_BLOCK)
    valid = jnp.logical_or(jnp.logical_and(cj < ATTN_BLOCK, cj > ri + first),
                           jnp.logical_and(cj >= ATTN_BLOCK, cj - ATTN_BLOCK <= ri))
    sink = jnp.concatenate([jnp.full((ATTN_BLOCK, 2 * ATTN_BLOCK), sinks_ref[hq], F32) for hq in heads], axis=0)
    s = jnp.where(valid, s, jnp.where(cj == 0, sink, NEG_BIG))
    p = jnp.exp(s - jnp.max(s, axis=-1, keepdims=True))
    probs = p / jnp.sum(p, axis=-1, keepdims=True)
    return heads, qs, kcat, vcat, probs, krow


def _attn_core_fwd(q, k, v, sinks):
    t = q.shape[0]
    nb = t // ATTN_BLOCK

    def body(q_ref, kc_ref, kp_ref, vc_ref, vp_ref, sinks_ref, o_ref):
        n = pl.program_id(0)
        for hk in range(ATTN_KV_HEADS):
            heads, _, _, vcat, probs, _ = _attn_group(q_ref, kc_ref, kp_ref, vc_ref, vp_ref, sinks_ref, hk, n)
            o = _nn(probs.astype(BF16), vcat)
            for g, hq in enumerate(heads):
                o_ref[:, hq * ATTN_HEAD_DIM:(hq + 1) * ATTN_HEAD_DIM] = (
                    o[g * ATTN_BLOCK:(g + 1) * ATTN_BLOCK].astype(BF16))

    cur = lambda w: pl.BlockSpec((ATTN_BLOCK, w), lambda n: (n, 0))
    prev = lambda w: pl.BlockSpec((ATTN_BLOCK, w), lambda n: (jnp.maximum(n - 1, 0), 0))
    return pl.pallas_call(
        body, name="attn_core_fwd", grid=(nb,),
        in_specs=[cur(ATTN_Q_W), cur(ATTN_KV_W), prev(ATTN_KV_W), cur(ATTN_KV_W), prev(ATTN_KV_W),
                  pl.BlockSpec(memory_space=pltpu.SMEM)],
        out_specs=cur(ATTN_Q_W),
        out_shape=jax.ShapeDtypeStruct((t, ATTN_Q_W), BF16),
        compiler_params=_params(1),
    )(q, k, k, v, v, sinks.reshape(-1))


def _attn_core_bwd(q, k, v, do, sinks):
    t = q.shape[0]
    nb = t // ATTN_BLOCK
    hd = ATTN_HEAD_DIM

    def body(q_ref, kc_ref, kp_ref, vc_ref, vp_ref, do_ref, sinks_ref,
             dq_ref, dk_ref, dv_ref, dsink_ref, ck_ref, cv_ref):
        n = pl.program_id(0)

        @pl.when(n == 0)
        def _():
            dsink_ref[...] = jnp.zeros_like(dsink_ref)
            ck_ref[...] = jnp.zeros_like(ck_ref)
            cv_ref[...] = jnp.zeros_like(cv_ref)

        @pl.when(n == nb)
        def _():
            dk_ref[...] = ck_ref[...]
            dv_ref[...] = cv_ref[...]

        @pl.when(n < nb)
        def _():
            for hk in range(ATTN_KV_HEADS):
                heads, qs, kcat, vcat, probs, krow = _attn_group(
                    q_ref, kc_ref, kp_ref, vc_ref, vp_ref, sinks_ref, hk, n)
                dos = jnp.concatenate([do_ref[:, hq * hd:(hq + 1) * hd] for hq in heads], axis=0)
                dp = _nt(dos, vcat)
                delta = jnp.sum(probs * dp, axis=-1, keepdims=True)
                dsf = probs * (dp - delta)
                ds = dsf.astype(BF16)
                dqs = _nn(ds, kcat) * ATTN_SCALE
                dkc = jnp.where(krow == 0, 0.0, _tn(ds, qs) * ATTN_SCALE)
                dvc = jnp.where(krow == 0, 0.0, _tn(probs.astype(BF16), dos))
                for g, hq in enumerate(heads):
                    blk = slice(g * ATTN_BLOCK, (g + 1) * ATTN_BLOCK)
                    dq_ref[:, hq * hd:(hq + 1) * hd] = dqs[blk]
                    dsink_ref[hq:hq + 1, :] += jnp.broadcast_to(
                        jnp.sum(dsf[blk, 0:1], axis=0, keepdims=True), (1, LANES))
                cols = slice(hk * hd, (hk + 1) * hd)
                dk_ref[:, cols] = ck_ref[:, cols] + dkc[:ATTN_BLOCK]
                dv_ref[:, cols] = cv_ref[:, cols] + dvc[:ATTN_BLOCK]
                ck_ref[:, cols] = dkc[ATTN_BLOCK:]
                cv_ref[:, cols] = dvc[ATTN_BLOCK:]

    cur = lambda w: pl.BlockSpec((ATTN_BLOCK, w), lambda n: (jnp.minimum(n, nb - 1), 0))
    prev = lambda w: pl.BlockSpec((ATTN_BLOCK, w), lambda n: (jnp.clip(n - 1, 0, nb - 1), 0))
    return pl.pallas_call(
        body, name="attn_core_bwd", grid=(nb + 1,),
        in_specs=[cur(ATTN_Q_W), cur(ATTN_KV_W), prev(ATTN_KV_W), cur(ATTN_KV_W), prev(ATTN_KV_W),
                  cur(ATTN_Q_W), pl.BlockSpec(memory_space=pltpu.SMEM)],
        out_specs=[cur(ATTN_Q_W), prev(ATTN_KV_W), prev(ATTN_KV_W),
                   pl.BlockSpec((ATTN_Q_HEADS, LANES), lambda n: (0, 0))],
        out_shape=[jax.ShapeDtypeStruct((t, ATTN_Q_W), F32),
                   jax.ShapeDtypeStruct((t, ATTN_KV_W), F32),
                   jax.ShapeDtypeStruct((t, ATTN_KV_W), F32),
                   jax.ShapeDtypeStruct((ATTN_Q_HEADS, LANES), F32)],
        scratch_shapes=[pltpu.VMEM((ATTN_BLOCK, ATTN_KV_W), F32),
                        pltpu.VMEM((ATTN_BLOCK, ATTN_KV_W), F32)],
        compiler_params=_params(1),
    )(q, k, k, v, v, do, sinks.reshape(-1))


def _proj_out_fwd(x, w, b, res):
    t = x.shape[0]
    tm = _row_tile(t, 512)

    def body(x_ref, w_ref, b_ref, r_ref, o_ref):
        o_ref[...] = r_ref[...] + _nn(x_ref[...], w_ref[...]) + b_ref[...]

    row = pl.BlockSpec((tm, D_MODEL), lambda i: (i, 0))
    return pl.pallas_call(
        body, name="proj_out_fwd", grid=(t // tm,),
        in_specs=[row, pl.BlockSpec(w.shape, lambda i: (0, 0)), pl.BlockSpec(b.shape, lambda i: (0, 0)), row],
        out_specs=row,
        out_shape=jax.ShapeDtypeStruct((t, D_MODEL), F32),
        compiler_params=_params(1),
    )(x, w, b, res)


def _proj_out_bwd(dy, w):
    t = dy.shape[0]
    tm = _row_tile(t, 512)

    def body(dy_ref, w_ref, dx_ref, dyb_ref, db_ref):
        @pl.when(pl.program_id(0) == 0)
        def _():
            db_ref[...] = jnp.zeros_like(db_ref)

        dy_v = dy_ref[...]
        dyb = dy_v.astype(BF16)
        dyb_ref[...] = dyb
        dx_ref[...] = _nt(dyb, w_ref[...]).astype(BF16)
        db_ref[...] += jnp.sum(dy_v, axis=0, keepdims=True)

    row = pl.BlockSpec((tm, D_MODEL), lambda i: (i, 0))
    return pl.pallas_call(
        body, name="proj_out_bwd", grid=(t // tm,),
        in_specs=[row, pl.BlockSpec(w.shape, lambda i: (0, 0))],
        out_specs=[row, row, pl.BlockSpec((1, D_MODEL), lambda i: (0, 0))],
        out_shape=[jax.ShapeDtypeStruct((t, D_MODEL), BF16),
                   jax.ShapeDtypeStruct((t, D_MODEL), BF16),
                   jax.ShapeDtypeStruct((1, D_MODEL), F32)],
        compiler_params=_params(1),
    )(dy, w)


def _attn_qkv_bwd(dy, h, nw, w_in, dq, dk, dv, rc, rs1, rs2):
    t = h.shape[0]
    tm = _row_tile(t, ATTN_TM)

    def body(dy_ref, h_ref, nw_ref, w_ref, dq_ref, dk_ref, dv_ref, c_ref, s1_ref, s2_ref,
             dh_ref, dqkv_ref, db_ref, dnw_ref, tmp_ref):
        @pl.when(pl.program_id(0) == 0)
        def _():
            db_ref[...] = jnp.zeros_like(db_ref)
            dnw_ref[...] = jnp.zeros_like(dnw_ref)

        c, s1, s2 = c_ref[...], s1_ref[...], s2_ref[...]
        for s in range(ATTN_Q_W // LANES):
            tmp_ref[:, s * LANES:(s + 1) * LANES] = _rope_bwd(dq_ref[:, s * LANES:(s + 1) * LANES], c, s1, s2)
        for s in range(ATTN_KV_W // LANES):
            lo = ATTN_Q_W + s * LANES
            tmp_ref[:, lo:lo + LANES] = _rope_bwd(dk_ref[:, s * LANES:(s + 1) * LANES], c, s1, s2)
        tmp_ref[:, ATTN_Q_W + ATTN_KV_W:] = dv_ref[...]
        dqkv = tmp_ref[...]
        db_ref[...] += jnp.sum(dqkv, axis=0, keepdims=True)
        dqkv_b = dqkv.astype(BF16)
        dqkv_ref[...] = dqkv_b
        dx, dnw = _norm_bwd(h_ref[...], nw_ref[...], _nt(dqkv_b, w_ref[...]))
        dh_ref[...] = dy_ref[...] + dx
        dnw_ref[...] += dnw

    row = lambda w: pl.BlockSpec((tm, w), lambda i: (i, 0))
    full = lambda a: pl.BlockSpec(a.shape, lambda i: (0, 0))
    return pl.pallas_call(
        body, name="attn_qkv_bwd", grid=(t // tm,),
        in_specs=[row(D_MODEL), row(D_MODEL), full(nw), full(w_in), row(ATTN_Q_W), row(ATTN_KV_W),
                  row(ATTN_KV_W), row(LANES), row(LANES), row(LANES)],
        out_specs=[row(D_MODEL), row(ATTN_IN), pl.BlockSpec((1, ATTN_IN), lambda i: (0, 0)),
                   pl.BlockSpec((1, D_MODEL), lambda i: (0, 0))],
        out_shape=[jax.ShapeDtypeStruct((t, D_MODEL), F32),
                   jax.ShapeDtypeStruct((t, ATTN_IN), BF16),
                   jax.ShapeDtypeStruct((1, ATTN_IN), F32),
                   jax.ShapeDtypeStruct((1, D_MODEL), F32)],
        scratch_shapes=[pltpu.VMEM((tm, ATTN_IN), F32)],
        compiler_params=_params(1),
    )(dy, h, nw, w_in, dq, dk, dv, rc, rs1, rs2)


GDN_TM = 256
GDN_CONV_TM = 128


def _gdn_proj_fwd(h, nw, w_qkv, w_z, w_ba):
    t = h.shape[0]
    tm = _row_tile(t, GDN_TM)

    def body(h_ref, nw_ref, wq_ref, wz_ref, wb_ref, x_ref, z_ref, ba_ref, hn_ref):
        _, xhat = _rms_stats(h_ref[...])
        hn = (xhat * nw_ref[...]).astype(BF16)
        hn_ref[...] = hn
        x_ref[...] = _nn(hn, wq_ref[...])
        z_ref[...] = _nn(hn, wz_ref[...])
        ba_ref[...] = _nn(hn, wb_ref[...])

    row = lambda w: pl.BlockSpec((tm, w), lambda i: (i, 0))
    full = lambda a: pl.BlockSpec(a.shape, lambda i: (0, 0))
    return pl.pallas_call(
        body, name="gdn_proj_fwd", grid=(t // tm,),
        in_specs=[row(D_MODEL), full(nw), full(w_qkv), full(w_z), full(w_ba)],
        out_specs=[row(GDN_CONV_W), row(D_MODEL), row(LANES), row(D_MODEL)],
        out_shape=[jax.ShapeDtypeStruct((t, GDN_CONV_W), F32),
                   jax.ShapeDtypeStruct((t, D_MODEL), F32),
                   jax.ShapeDtypeStruct((t, LANES), F32),
                   jax.ShapeDtypeStruct((t, D_MODEL), BF16)],
        compiler_params=_params(1),
    )(h, nw, w_qkv, w_z, w_ba)


def _softplus(x):
    return jnp.maximum(x, 0.0) + jnp.log(1.0 + jnp.exp(-jnp.abs(x)))


def _conv_taps(x, halo, cw):
    tm = x.shape[0]
    xx = jnp.concatenate([halo, x], axis=0)
    taps = [xx[8 - (GDN_CONV - 1) + i: 8 - (GDN_CONV - 1) + i + tm] for i in range(GDN_CONV)]
    c = taps[0] * cw[0:1]
    for i in range(1, GDN_CONV):
        c = c + taps[i] * cw[i:i + 1]
    return c, taps


def _gates(ba, gp):
    lane = lax.broadcasted_iota(jnp.int32, ba.shape, 1)
    beta = _sigmoid(ba)
    pre = ba + gp[1:2]
    g = -jnp.exp(gp[0:1]) * _softplus(pre)
    gates = jnp.where(lane < GDN_HEADS, beta, jnp.where(lane < 2 * GDN_HEADS, g, 0.0))
    return lane, beta, pre, g, gates


def _gdn_conv_fwd(x, cw, ba, gp):
    t = x.shape[0]
    tm = _row_tile(t, GDN_CONV_TM)

    def body(x_ref, halo_ref, cw_ref, ba_ref, gp_ref, q_ref, k_ref, v_ref, gates_ref):
        halo = jnp.where(pl.program_id(0) > 0, halo_ref[...], 0.0)
        c, _ = _conv_taps(x_ref[...], halo, cw_ref[...])
        s = c * _sigmoid(c)
        for hh in range(GDN_HEADS):
            sq = s[:, hh * LANES:(hh + 1) * LANES]
            q_ref[:, hh * LANES:(hh + 1) * LANES] = (
                sq * lax.rsqrt(jnp.sum(sq * sq, axis=-1, keepdims=True) + NORM_EPS) * GDN_QSCALE)
            sk = s[:, D_MODEL + hh * LANES:D_MODEL + (hh + 1) * LANES]
            k_ref[:, hh * LANES:(hh + 1) * LANES] = (
                sk * lax.rsqrt(jnp.sum(sk * sk, axis=-1, keepdims=True) + NORM_EPS))
        v_ref[...] = s[:, 2 * D_MODEL:]
        gates_ref[...] = _gates(ba_ref[...], gp_ref[...])[4]

    row = lambda w: pl.BlockSpec((tm, w), lambda i: (i, 0))
    full = lambda a: pl.BlockSpec(a.shape, lambda i: (0, 0))
    halo = pl.BlockSpec((8, GDN_CONV_W), lambda i: (jnp.maximum(i * (tm // 8) - 1, 0), 0))
    return pl.pallas_call(
        body, name="gdn_conv_fwd", grid=(t // tm,),
        in_specs=[row(GDN_CONV_W), halo, full(cw), row(LANES), full(gp)],
        out_specs=[row(D_MODEL), row(D_MODEL), row(D_MODEL), row(LANES)],
        out_shape=[jax.ShapeDtypeStruct((t, D_MODEL), F32)] * 3 + [jax.ShapeDtypeStruct((t, LANES), F32)],
        compiler_params=_params(1),
    )(x, x, cw, ba, gp)


def _split3(x):
    hi = x.astype(BF16).astype(F32)
    r1 = x - hi
    mid = r1.astype(BF16).astype(F32)
    lo = (r1 - mid).astype(BF16).astype(F32)
    return hi, mid, lo


def _chunk_cumsum(col, keep):
    nb, c, _ = col.shape
    l3 = lax.broadcasted_iota(jnp.int32, (nb, c, LANES), 2)
    hi, mid, lo = _split3(col)
    pieces = jnp.where(l3 == 0, hi, jnp.where(l3 == 1, mid, jnp.where(l3 == 2, lo, 0.0))).astype(BF16)
    s = _bnn(jnp.where(keep, 1.0, 0.0).astype(BF16), pieces)
    return s[..., 0:1] + s[..., 1:2] + s[..., 2:3]


def _unit_inverse(nmat):
    nb, c, _ = nmat.shape
    ri = lax.broadcasted_iota(jnp.int32, (nb, c, c), 1)
    ci = lax.broadcasted_iota(jnp.int32, (nb, c, c), 2)
    eye = jnp.where(ri == ci, 1.0, 0.0).astype(F32)
    same = (ri >> GDN_SUB_SHIFT) == (ci >> GDN_SUB_SHIFT)
    nd = jnp.where(same, nmat, 0.0)
    no = nmat - nd
    mm = lambda a, b: _bnn(a.astype(BF16), b.astype(BF16))
    n2 = mm(nd, nd)
    n4 = mm(n2, n2)
    n8 = mm(n4, n4)
    td = mm(mm(mm(eye - nd, eye + n2), eye + n4), eye + n8)
    bm = mm(td, no)
    b2 = mm(bm, bm)
    return mm(mm(eye - bm, eye + b2), td)


def _chunk_terms(q_ref, k_ref, gates_ref, h, transposed):
    rows = k_ref.shape[0]
    nb = rows // GDN_CHUNK
    c = GDN_CHUNK
    lane = lax.broadcasted_iota(jnp.int32, (rows, LANES), 1)
    gt = gates_ref[...]
    beta = jnp.sum(jnp.where(lane == h, gt, 0.0), axis=-1, keepdims=True).reshape(nb, c, 1)
    g = jnp.sum(jnp.where(lane == GDN_HEADS + h, gt, 0.0), axis=-1, keepdims=True).reshape(nb, c, 1)
    ri = lax.broadcasted_iota(jnp.int32, (nb, c, c), 1)
    ci = lax.broadcasted_iota(jnp.int32, (nb, c, c), 2)
    gcol = _chunk_cumsum(g, ri >= ci)
    gamma = jnp.broadcast_to(gcol, (nb, c, LANES))
    l3 = lax.broadcasted_iota(jnp.int32, (nb, c, LANES), 2)
    gh, gm, gl = _split3(gcol)
    pmat = jnp.where(l3 == 0, gh, jnp.where(l3 == 1, gm, jnp.where(l3 == 2, gl, jnp.where(l3 < 6, 1.0, 0.0))))
    qmat = jnp.where(l3 < 3, 1.0, jnp.where(l3 == 3, -gh, jnp.where(l3 == 4, -gm, jnp.where(l3 == 5, -gl, 0.0))))
    pmat, qmat = pmat.astype(BF16), qmat.astype(BF16)
    k = k_ref[...].reshape(nb, c, LANES)
    q = q_ref[...].reshape(nb, c, LANES)
    kb = k * beta
    kbf, kbb, qb = k.astype(BF16), kb.astype(BF16), q.astype(BF16)
    out = dict(beta=beta, g=g, gamma=gamma, k=k, q=q, kb=kb, kbf=kbf, kbb=kbb, qb=qb, ri=ri, ci=ci)
    diff = _bnt(pmat, qmat)
    lmat = jnp.exp(jnp.where(ri >= ci, diff, NEG_BIG))
    out["L"] = lmat
    out["A"] = jnp.where(ri > ci, _bnt(kbb, kbf) * lmat, 0.0)
    out["Aqk"] = jnp.where(ri >= ci, _bnt(qb, kbf) * lmat, 0.0)
    if transposed:
        difft = _bnt(qmat, pmat)
        lt = jnp.exp(jnp.where(ci >= ri, difft, NEG_BIG))
        out["LT"] = lt
        out["AT"] = jnp.where(ci > ri, _bnt(kbf, kbb) * lt, 0.0)
        out["AqkT"] = jnp.where(ci >= ri, _bnt(kbf, qb) * lt, 0.0)
    return out


def _gdn_intra_fwd(q, k, v, gates):
    t = q.shape[0]
    rows = _row_tile(t, GDN_ROWS)
    nb = rows // GDN_CHUNK
    nchunks = t // GDN_CHUNK

    def body(q_ref, k_ref, v_ref, gates_ref, u_ref, w_ref, qd_ref, kd_ref, aqk_ref, cd_ref):
        h = pl.program_id(1)
        tm_ = _chunk_terms(q_ref, k_ref, gates_ref, h, False)
        gamma, beta = tm_["gamma"], tm_["beta"]
        eg = jnp.exp(gamma)
        tinv = _unit_inverse(tm_["A"])
        v3 = v_ref[...].reshape(nb, GDN_CHUNK, LANES)
        rhs = jnp.concatenate([v3 * beta, tm_["kb"] * eg], axis=-1)
        sol = _bnn(tinv.astype(BF16), rhs.astype(BF16))
        u_ref[...] = sol[..., :LANES].reshape(rows, LANES)
        w_ref[...] = sol[..., LANES:].reshape(rows, LANES).astype(BF16)
        gl = gamma[:, GDN_CHUNK - 1:GDN_CHUNK, :]
        qd_ref[...] = (tm_["q"] * eg).reshape(rows, LANES).astype(BF16)
        kd_ref[...] = (tm_["k"] * jnp.exp(gl - gamma)).reshape(rows, LANES).astype(BF16)
        aqk_ref[0] = tm_["Aqk"].reshape(rows, GDN_CHUNK).astype(BF16)
        cd_ref[0] = jnp.broadcast_to(jnp.exp(gl), (nb, 8, LANES)).reshape(nb * 8, LANES)

    blk = pl.BlockSpec((rows, LANES), lambda i, h: (i, h))
    return pl.pallas_call(
        body, name="gdn_intra_fwd", grid=(t // rows, GDN_HEADS),
        in_specs=[blk, blk, blk, pl.BlockSpec((rows, LANES), lambda i, h: (i, 0))],
        out_specs=[blk, blk, blk, blk,
                   pl.BlockSpec((1, rows, GDN_CHUNK), lambda i, h: (h, i, 0)),
                   pl.BlockSpec((1, nb * 8, LANES), lambda i, h: (h, i, 0))],
        out_shape=[jax.ShapeDtypeStruct((t, D_MODEL), F32),
                   jax.ShapeDtypeStruct((t, D_MODEL), BF16),
                   jax.ShapeDtypeStruct((t, D_MODEL), BF16),
                   jax.ShapeDtypeStruct((t, D_MODEL), BF16),
                   jax.ShapeDtypeStruct((GDN_HEADS, t, GDN_CHUNK), BF16),
                   jax.ShapeDtypeStruct((GDN_HEADS, nchunks * 8, LANES), F32)],
        compiler_params=_params(2),
    )(q, k, v, gates)


def _gdn_scan_fwd(u, w, qd, kd, aqk, cd):
    t = u.shape[0]
    rows = _row_tile(t, GDN_ROWS)
    nb = rows // GDN_CHUNK
    nchunks = t // GDN_CHUNK

    def body(u_ref, w_ref, qd_ref, kd_ref, aqk_ref, cd_ref, o_ref, st_ref, s_ref):
        @pl.when(pl.program_id(1) == 0)
        def _():
            s_ref[...] = jnp.zeros_like(s_ref)

        for c in range(nb):
            r = slice(c * GDN_CHUNK, (c + 1) * GDN_CHUNK)
            s = s_ref[...]
            st_ref[0, c * LANES:(c + 1) * LANES, :] = s
            sb = s.astype(BF16)
            vb = (u_ref[r, :] - _nn(w_ref[r, :], sb)).astype(BF16)
            o_ref[r, :] = _nn(qd_ref[r, :], sb) + _nn(aqk_ref[0, r, :], vb)
            s_ref[...] = s * cd_ref[0, c * 8:c * 8 + 1, :] + _tn(kd_ref[r, :], vb)

    blk = pl.BlockSpec((rows, LANES), lambda h, i: (i, h))
    return pl.pallas_call(
        body, name="gdn_scan_fwd", grid=(GDN_HEADS, t // rows),
        in_specs=[blk, blk, blk, blk,
                  pl.BlockSpec((1, rows, GDN_CHUNK), lambda h, i: (h, i, 0)),
                  pl.BlockSpec((1, nb * 8, LANES), lambda h, i: (h, i, 0))],
        out_specs=[blk, pl.BlockSpec((1, nb * LANES, LANES), lambda h, i: (h, i, 0))],
        out_shape=[jax.ShapeDtypeStruct((t, D_MODEL), F32),
                   jax.ShapeDtypeStruct((GDN_HEADS, nchunks * LANES, LANES), F32)],
        scratch_shapes=[pltpu.VMEM((LANES, LANES), F32)],
        compiler_params=_params(2),
    )(u, w, qd, kd, aqk, cd)


def _gdn_out_fwd(o, z, nw, w_out, res):
    t = o.shape[0]
    tm = _row_tile(t, GDN_TM)

    def body(o_ref, z_ref, nw_ref, w_ref, r_ref, out_ref, gated_ref):
        nwv = nw_ref[...]
        for hh in range(GDN_HEADS):
            sl = slice(hh * LANES, (hh + 1) * LANES)
            _, on = _rms_stats(o_ref[:, sl])
            zv = z_ref[:, sl]
            gated_ref[:, sl] = (on * nwv * (zv * _sigmoid(zv))).astype(BF16)
        out_ref[...] = r_ref[...] + _nn(gated_ref[...], w_ref[...])

    row = pl.BlockSpec((tm, D_MODEL), lambda i: (i, 0))
    full = lambda a: pl.BlockSpec(a.shape, lambda i: (0, 0))
    return pl.pallas_call(
        body, name="gdn_out_fwd", grid=(t // tm,),
        in_specs=[row, row, full(nw), full(w_out), row],
        out_specs=[row, row],
        out_shape=[jax.ShapeDtypeStruct((t, D_MODEL), F32), jax.ShapeDtypeStruct((t, D_MODEL), BF16)],
        compiler_params=_params(1),
    )(o, z, nw, w_out, res)


def _gdn_out_bwd(dy, w_out, o, z, nw):
    t = o.shape[0]
    tm = _row_tile(t, GDN_TM)

    def body(dy_ref, w_ref, o_ref, z_ref, nw_ref, do_ref, dz_ref, dyb_ref, dnw_ref, dgt_ref):
        @pl.when(pl.program_id(0) == 0)
        def _():
            dnw_ref[...] = jnp.zeros_like(dnw_ref)

        dyb = dy_ref[...].astype(BF16)
        dyb_ref[...] = dyb
        dgt_ref[...] = _nt(dyb, w_ref[...])
        nwv = nw_ref[...]
        for hh in range(GDN_HEADS):
            sl = slice(hh * LANES, (hh + 1) * LANES)
            r, on = _rms_stats(o_ref[:, sl])
            zv = z_ref[:, sl]
            sig = _sigmoid(zv)
            sz = zv * sig
            dgt = dgt_ref[:, sl]
            d_on = dgt * nwv * sz
            dz_ref[:, sl] = (dgt * on * nwv * (sig * (1.0 + zv * (1.0 - sig)))).astype(BF16)
            dnw_ref[...] += jnp.sum(dgt * on * sz, axis=0, keepdims=True)
            do_ref[:, sl] = (r * (d_on - on * jnp.mean(d_on * on, axis=-1, keepdims=True))).astype(BF16)

    row = pl.BlockSpec((tm, D_MODEL), lambda i: (i, 0))
    full = lambda a: pl.BlockSpec(a.shape, lambda i: (0, 0))
    return pl.pallas_call(
        body, name="gdn_out_bwd", grid=(t // tm,),
        in_specs=[row, full(w_out), row, row, full(nw)],
        out_specs=[row, row, row, pl.BlockSpec((1, LANES), lambda i: (0, 0))],
        out_shape=[jax.ShapeDtypeStruct((t, D_MODEL), BF16)] * 3 + [jax.ShapeDtypeStruct((1, LANES), F32)],
        scratch_shapes=[pltpu.VMEM((tm, D_MODEL), F32)],
        compiler_params=_params(1),
    )(dy, w_out, o, z, nw)


def _gdn_scan_bwd(u, w, qd, kd, aqk, cd, st, do):
    t = u.shape[0]
    rows = _row_tile(t, GDN_ROWS)
    nb = rows // GDN_CHUNK
    nchunks = t // GDN_CHUNK
    nsteps = t // rows
    cc = GDN_CHUNK

    def body(u_ref, w_ref, qd_ref, kd_ref, aqk_ref, cd_ref, st_ref, do_ref,
             du_ref, dw_ref, dqd_ref, dkd_ref, daqk_ref, daqkt_ref, dcd_ref, ds_ref):
        @pl.when(pl.program_id(1) == 0)
        def _():
            ds_ref[...] = jnp.zeros_like(ds_ref)

        ri = lax.broadcasted_iota(jnp.int32, (cc, cc), 0)
        ci = lax.broadcasted_iota(jnp.int32, (cc, cc), 1)
        for c in reversed(range(nb)):
            r = slice(c * cc, (c + 1) * cc)
            s = st_ref[0, c * LANES:(c + 1) * LANES, :]
            sb = s.astype(BF16)
            dsn = ds_ref[...]
            dsb = dsn.astype(BF16)
            wv, kdv, qdv, aq, dov = w_ref[r, :], kd_ref[r, :], qd_ref[r, :], aqk_ref[0, r, :], do_ref[r, :]
            vb = (u_ref[r, :] - _nn(wv, sb)).astype(BF16)
            dv = _tn(aq, dov) + _nn(kdv, dsb)
            dvb = dv.astype(BF16)
            daqk_ref[0, r, :] = jnp.where(ri >= ci, _nt(dov, vb), 0.0)
            daqkt_ref[0, r, :] = jnp.where(ci >= ri, _nt(vb, dov), 0.0)
            dqd_ref[r, :] = _nt(dov, sb)
            dkd_ref[r, :] = _nt(vb, dsb)
            dcd_ref[0, c * 8:(c + 1) * 8, :] = jnp.broadcast_to(jnp.sum(s * dsn), (8, LANES))
            du_ref[r, :] = dv
            dw_ref[r, :] = -_nt(dvb, sb)
            ds_ref[...] = _tn(qdv, dov) + dsn * cd_ref[0, c * 8:c * 8 + 1, :] - _tn(wv, dvb)

    rev = lambda i: nsteps - 1 - i
    blk = pl.BlockSpec((rows, LANES), lambda h, i: (rev(i), h))
    sq = pl.BlockSpec((1, rows, cc), lambda h, i: (h, rev(i), 0))
    cdb = pl.BlockSpec((1, nb * 8, LANES), lambda h, i: (h, rev(i), 0))
    return pl.pallas_call(
        body, name="gdn_scan_bwd", grid=(GDN_HEADS, nsteps),
        in_specs=[blk, blk, blk, blk, sq, cdb,
                  pl.BlockSpec((1, nb * LANES, LANES), lambda h, i: (h, rev(i), 0)), blk],
        out_specs=[blk, blk, blk, blk, sq, sq, cdb],
        out_shape=[jax.ShapeDtypeStruct((t, D_MODEL), F32)] * 4
        + [jax.ShapeDtypeStruct((GDN_HEADS, t, cc), F32)] * 2
        + [jax.ShapeDtypeStruct((GDN_HEADS, nchunks * 8, LANES), F32)],
        scratch_shapes=[pltpu.VMEM((LANES, LANES), F32)],
        compiler_params=_params(2),
    )(u, w, qd, kd, aqk, cd, st, do)


def _gdn_intra_bwd(q, k, v, gates, u, w, du, dw, dqd, dkd, daqk, daqkt, dcd):
    t = q.shape[0]
    rows = _row_tile(t, GDN_ROWS)
    nb = rows // GDN_CHUNK
    cc = GDN_CHUNK

    def body(q_ref, k_ref, v_ref, gates_ref, u_ref, w_ref, du_ref, dw_ref, dqd_ref, dkd_ref,
             daqk_ref, daqkt_ref, dcd_ref, dq_ref, dk_ref, dv_ref, dgates_ref):
        h = pl.program_id(1)

        @pl.when(h == 0)
        def _():
            dgates_ref[...] = jnp.zeros_like(dgates_ref)

        tm_ = _chunk_terms(q_ref, k_ref, gates_ref, h, True)
        gamma, beta, kk, qq, kb = tm_["gamma"], tm_["beta"], tm_["k"], tm_["q"], tm_["kb"]
        kbf, kbb, qb = tm_["kbf"], tm_["kbb"], tm_["qb"]
        lmat, lt = tm_["L"], tm_["LT"]
        ri, ci = tm_["ri"], tm_["ci"]
        r3 = lambda ref: ref[...].reshape(nb, cc, LANES)
        eg = jnp.exp(gamma)
        gl = gamma[:, cc - 1:cc, :]
        ekd = jnp.exp(gl - gamma)
        v3 = r3(v_ref)
        tt = _unit_inverse(tm_["AT"])
        dsol = jnp.concatenate([r3(du_ref), r3(dw_ref)], axis=-1)
        sol = jnp.concatenate([r3(u_ref), r3(w_ref).astype(F32)], axis=-1)
        dx = _bnn(tt.astype(BF16), dsol.astype(BF16))
        dxb, solb = dx.astype(BF16), sol.astype(BF16)
        da = jnp.where(ri > ci, -_bnt(dxb, solb), 0.0)
        dat = jnp.where(ci > ri, -_bnt(solb, dxb), 0.0)
        dxu, dxw = dx[..., :LANES], dx[..., LANES:]
        dv_ref[...] = (dxu * beta).reshape(rows, LANES)
        dbeta = jnp.sum(dxu * v3, axis=-1, keepdims=True)
        dkb = dxw * eg
        dgam = jnp.sum(dxw * kb * eg, axis=-1, keepdims=True)
        dkb = dkb + _bnn((da * lmat).astype(BF16), kbf)
        dk = _bnn((dat * lt).astype(BF16), kbb)
        dgam = dgam + jnp.sum(da * tm_["A"], axis=-1, keepdims=True) - jnp.sum(dat * tm_["AT"], axis=-1, keepdims=True)
        daq = daqk_ref[0].reshape(nb, cc, cc)
        daqt = daqkt_ref[0].reshape(nb, cc, cc)
        dq = _bnn((daq * lmat).astype(BF16), kbf)
        dk = dk + _bnn((daqt * lt).astype(BF16), qb)
        dgam = dgam + jnp.sum(daq * tm_["Aqk"], axis=-1, keepdims=True) - jnp.sum(daqt * tm_["AqkT"], axis=-1, keepdims=True)
        dqd3, dkd3 = r3(dqd_ref), r3(dkd_ref)
        dq = dq + dqd3 * eg
        dgam = dgam + jnp.sum(dqd3 * qq * eg, axis=-1, keepdims=True)
        dk = dk + dkd3 * ekd
        tk = jnp.sum(dkd3 * kk * ekd, axis=-1, keepdims=True)
        dgam = dgam - tk
        dcdv = dcd_ref[0].reshape(nb, 8, LANES)[:, 0:1, 0:1]
        dglast = jnp.sum(tk, axis=1, keepdims=True) + dcdv * jnp.exp(gl[:, :, 0:1])
        rowi = lax.broadcasted_iota(jnp.int32, (nb, cc, 1), 1)
        dgam = dgam + jnp.where(rowi == cc - 1, dglast, 0.0)
        dk = dk + dkb * beta
        dbeta = dbeta + jnp.sum(dkb * kk, axis=-1, keepdims=True)
        dg = _chunk_cumsum(dgam, ci >= ri)
        dq_ref[...] = dq.reshape(rows, LANES)
        dk_ref[...] = dk.reshape(rows, LANES)
        lane = lax.broadcasted_iota(jnp.int32, (rows, LANES), 1)
        dgates_ref[...] += (jnp.where(lane == h, dbeta.reshape(rows, 1), 0.0)
                            + jnp.where(lane == GDN_HEADS + h, dg.reshape(rows, 1), 0.0))

    blk = pl.BlockSpec((rows, LANES), lambda i, h: (i, h))
    shared = pl.BlockSpec((rows, LANES), lambda i, h: (i, 0))
    sq = pl.BlockSpec((1, rows, cc), lambda i, h: (h, i, 0))
    return pl.pallas_call(
        body, name="gdn_intra_bwd", grid=(t // rows, GDN_HEADS),
        in_specs=[blk, blk, blk, shared, blk, blk, blk, blk, blk, blk, sq, sq,
                  pl.BlockSpec((1, nb * 8, LANES), lambda i, h: (h, i, 0))],
        out_specs=[blk, blk, blk, shared],
        out_shape=[jax.ShapeDtypeStruct((t, D_MODEL), F32)] * 3 + [jax.ShapeDtypeStruct((t, LANES), F32)],
        compiler_params=_params(2),
    )(q, k, v, gates, u, w, du, dw, dqd, dkd, daqk, daqkt, dcd)


def _gdn_conv_bwd_a(x, cw, ba, gp, dq, dk, dv, dgates):
    t = x.shape[0]
    tm = _row_tile(t, GDN_CONV_TM)

    def body(x_ref, halo_ref, cw_ref, ba_ref, gp_ref, dq_ref, dk_ref, dv_ref, dgates_ref,
             dc_ref, dba_ref, dcw_ref, dgp_ref, ds_ref):
        @pl.when(pl.program_id(0) == 0)
        def _():
            dcw_ref[...] = jnp.zeros_like(dcw_ref)
            dgp_ref[...] = jnp.zeros_like(dgp_ref)

        halo = jnp.where(pl.program_id(0) > 0, halo_ref[...], 0.0)
        c, taps = _conv_taps(x_ref[...], halo, cw_ref[...])
        sig = _sigmoid(c)
        s = c * sig
        for hh in range(GDN_HEADS):
            sl = slice(hh * LANES, (hh + 1) * LANES)
            sq = s[:, sl]
            rq = lax.rsqrt(jnp.sum(sq * sq, axis=-1, keepdims=True) + NORM_EPS)
            qh = sq * rq
            dqv = dq_ref[:, sl]
            ds_ref[:, sl] = GDN_QSCALE * rq * (dqv - qh * jnp.sum(dqv * qh, axis=-1, keepdims=True))
            sl2 = slice(D_MODEL + hh * LANES, D_MODEL + (hh + 1) * LANES)
            sk = s[:, sl2]
            rk = lax.rsqrt(jnp.sum(sk * sk, axis=-1, keepdims=True) + NORM_EPS)
            kh = sk * rk
            dkv = dk_ref[:, sl]
            ds_ref[:, sl2] = rk * (dkv - kh * jnp.sum(dkv * kh, axis=-1, keepdims=True))
        ds_ref[:, 2 * D_MODEL:] = dv_ref[...]
        dc = ds_ref[...] * (sig * (1.0 + c * (1.0 - sig)))
        dc_ref[...] = dc
        for i in range(GDN_CONV):
            dcw_ref[i:i + 1, :] += jnp.sum(dc * taps[i], axis=0, keepdims=True)
        lane, beta, pre, g, _ = _gates(ba_ref[...], gp_ref[...])
        dgt = dgates_ref[...]
        db = dgt * beta * (1.0 - beta)
        dpre = dgt * (-jnp.exp(gp_ref[0:1, :])) * _sigmoid(pre)
        isa = jnp.logical_and(lane >= GDN_HEADS, lane < 2 * GDN_HEADS)
        dba_ref[...] = jnp.where(lane < GDN_HEADS, db, jnp.where(isa, dpre, 0.0)).astype(BF16)
        dgp_ref[0:1, :] += jnp.sum(jnp.where(isa, dgt * g, 0.0), axis=0, keepdims=True)
        dgp_ref[1:2, :] += jnp.sum(jnp.where(isa, dpre, 0.0), axis=0, keepdims=True)

    row = lambda w: pl.BlockSpec((tm, w), lambda i: (i, 0))
    full = lambda a: pl.BlockSpec(a.shape, lambda i: (0, 0))
    halo = pl.BlockSpec((8, GDN_CONV_W), lambda i: (jnp.maximum(i * (tm // 8) - 1, 0), 0))
    return pl.pallas_call(
        body, name="gdn_conv_bwd_a", grid=(t // tm,),
        in_specs=[row(GDN_CONV_W), halo, full(cw), row(LANES), full(gp), row(D_MODEL), row(D_MODEL),
                  row(D_MODEL), row(LANES)],
        out_specs=[row(GDN_CONV_W), row(LANES), pl.BlockSpec((8, GDN_CONV_W), lambda i: (0, 0)),
                   pl.BlockSpec((8, LANES), lambda i: (0, 0))],
        out_shape=[jax.ShapeDtypeStruct((t, GDN_CONV_W), F32),
                   jax.ShapeDtypeStruct((t, LANES), BF16),
                   jax.ShapeDtypeStruct((8, GDN_CONV_W), F32),
                   jax.ShapeDtypeStruct((8, LANES), F32)],
        scratch_shapes=[pltpu.VMEM((tm, GDN_CONV_W), F32)],
        compiler_params=_params(1),
    )(x, x, cw, ba, gp, dq, dk, dv, dgates)


def _gdn_conv_bwd_b(dc, cw):
    t = dc.shape[0]
    tm = _row_tile(t, GDN_CONV_TM)
    nsteps = t // tm

    def body(dc_ref, halo_ref, cw_ref, dx_ref):
        halo = jnp.where(pl.program_id(0) < nsteps - 1, halo_ref[...], 0.0)
        dd = jnp.concatenate([dc_ref[...], halo], axis=0)
        cw_v = cw_ref[...]
        acc = dd[GDN_CONV - 1:GDN_CONV - 1 + tm] * cw_v[0:1]
        for i in range(1, GDN_CONV):
            acc = acc + dd[GDN_CONV - 1 - i:GDN_CONV - 1 - i + tm] * cw_v[i:i + 1]
        dx_ref[...] = acc.astype(BF16)

    row = pl.BlockSpec((tm, GDN_CONV_W), lambda i: (i, 0))
    halo = pl.BlockSpec((8, GDN_CONV_W), lambda i: (jnp.minimum((i + 1) * (tm // 8), t // 8 - 1), 0))
    return pl.pallas_call(
        body, name="gdn_conv_bwd_b", grid=(nsteps,),
        in_specs=[row, halo, pl.BlockSpec(cw.shape, lambda i: (0, 0))],
        out_specs=row,
        out_shape=jax.ShapeDtypeStruct((t, GDN_CONV_W), BF16),
        compiler_params=_params(1),
    )(dc, dc, cw)


def _gdn_proj_bwd(dy, h, nw, w_qkv, w_z, w_ba, dx, dz, dba):
    t = h.shape[0]
    tm = _row_tile(t, GDN_TM)

    def body(dy_ref, h_ref, nw_ref, wq_ref, wz_ref, wb_ref, dx_ref, dz_ref, dba_ref, dh_ref, dnw_ref):
        @pl.when(pl.program_id(0) == 0)
        def _():
            dnw_ref[...] = jnp.zeros_like(dnw_ref)

        dhn = _nt(dx_ref[...], wq_ref[...]) + _nt(dz_ref[...], wz_ref[...]) + _nt(dba_ref[...], wb_ref[...])
        dxx, dnw = _norm_bwd(h_ref[...], nw_ref[...], dhn)
        dh_ref[...] = dy_ref[...] + dxx
        dnw_ref[...] += dnw

    row = lambda w: pl.BlockSpec((tm, w), lambda i: (i, 0))
    full = lambda a: pl.BlockSpec(a.shape, lambda i: (0, 0))
    return pl.pallas_call(
        body, name="gdn_proj_bwd", grid=(t // tm,),
        in_specs=[row(D_MODEL), row(D_MODEL), full(nw), full(w_qkv), full(w_z), full(w_ba),
                  row(GDN_CONV_W), row(D_MODEL), row(LANES)],
        out_specs=[row(D_MODEL), pl.BlockSpec((1, D_MODEL), lambda i: (0, 0))],
        out_shape=[jax.ShapeDtypeStruct((t, D_MODEL), F32), jax.ShapeDtypeStruct((1, D_MODEL), F32)],
        compiler_params=_params(1),
    )(dy, h, nw, w_qkv, w_z, w_ba, dx, dz, dba)


def _loss_head(h, nw, target):
    t = h.shape[0]
    tm = _row_tile(t, 512)

    def body(h_ref, nw_ref, t_ref, loss_ref, dh_ref, dnw_ref):
        @pl.when(pl.program_id(0) == 0)
        def _():
            loss_ref[...] = jnp.zeros_like(loss_ref)
            dnw_ref[...] = jnp.zeros_like(dnw_ref)

        x = h_ref[...]
        nwv = nw_ref[...]
        _, xhat = _rms_stats(x)
        err = xhat * nwv - t_ref[...]
        loss_ref[...] += 0.5 * jnp.sum(jnp.mean(err * err, axis=-1, keepdims=True))
        dx, dnw = _norm_bwd(x, nwv, err * (1.0 / D_MODEL))
        dh_ref[...] = dx
        dnw_ref[...] += dnw

    row = pl.BlockSpec((tm, D_MODEL), lambda i: (i, 0))
    return pl.pallas_call(
        body, name="loss_head", grid=(t // tm,),
        in_specs=[row, pl.BlockSpec((1, D_MODEL), lambda i: (0, 0)), row],
        out_specs=[pl.BlockSpec((8, LANES), lambda i: (0, 0)), row, pl.BlockSpec((1, D_MODEL), lambda i: (0, 0))],
        out_shape=[jax.ShapeDtypeStruct((8, LANES), F32),
                   jax.ShapeDtypeStruct((t, D_MODEL), F32),
                   jax.ShapeDtypeStruct((1, D_MODEL), F32)],
        compiler_params=_params(1),
    )(h, nw, target)


_PEER_FLIPS = [(dx, dy, dc) for dx in (0, 1) for dy in (0, 1) for dc in (0, 1)][1:]


_ANY = pl.BlockSpec(memory_space=pl.ANY)


def _exchange_copies(ins, outs, send_sems, recv_sems, local_sems, scatter):
    x, y, c = lax.axis_index("x"), lax.axis_index("y"), lax.axis_index("c")
    me = 4 * x + 2 * y + c
    copies = []
    for a in range(len(ins)):
        src = ins[a].at[me] if scatter else ins[a]
        copies.append(pltpu.make_async_copy(src, outs[a].at[me], local_sems.at[a]))
    for k, (fx, fy, fc) in enumerate(_PEER_FLIPS):
        px, py, pc = lax.rem(x + fx, 2), lax.rem(y + fy, 2), lax.rem(c + fc, 2)
        peer = 4 * px + 2 * py + pc
        for a in range(len(ins)):
            copies.append(pltpu.make_async_remote_copy(
                src_ref=ins[a].at[peer] if scatter else ins[a],
                dst_ref=outs[a].at[me],
                send_sem=send_sems.at[a, k], recv_sem=recv_sems.at[a, k],
                device_id=(px, py, pc), device_id_type=MESH_ID))
    return copies


def _exchange_shapes(arrs, scatter):
    return [jax.ShapeDtypeStruct((N_DEV,) + (a.shape[1:] if scatter else a.shape), a.dtype) for a in arrs]


def _exchange_sems(n):
    npeer = len(_PEER_FLIPS)
    return [pltpu.SemaphoreType.DMA((n, npeer)), pltpu.SemaphoreType.DMA((n, npeer)),
            pltpu.SemaphoreType.DMA((n,))]


def _exchange(arrs, scatter, name):
    n = len(arrs)

    def body(*refs):
        copies = _exchange_copies(refs[:n], refs[n:2 * n], *refs[2 * n:], scatter)
        for cp in copies:
            cp.start()
        for cp in copies:
            cp.wait()

    return pl.pallas_call(
        body, name=name, in_specs=[_ANY] * n, out_specs=[_ANY] * n,
        out_shape=_exchange_shapes(arrs, scatter), scratch_shapes=_exchange_sems(n),
    )(*arrs)


def _adamw(parts, w, m, v, name):
    r, c = w.shape
    tr = r
    for cand in (512, 256, 128, 64, 32, 16, 8):
        if r % cand == 0:
            tr = cand
            break
    c1 = 1.0 - ADAM_B1 ** ADAM_STEP
    c2 = 1.0 - ADAM_B2 ** ADAM_STEP

    def body(p_ref, w_ref, m_ref, v_ref, g_ref, d_ref, nm_ref, nv_ref):
        g = p_ref[0].astype(F32)
        for s in range(1, N_DEV):
            g = g + p_ref[s].astype(F32)
        nm = ADAM_B1 * m_ref[...] + (1.0 - ADAM_B1) * g
        nv = ADAM_B2 * v_ref[...] + (1.0 - ADAM_B2) * (g * g)
        g_ref[...] = g
        nm_ref[...] = nm
        nv_ref[...] = nv
        d_ref[...] = -ADAM_LR * ((nm / c1) / (jnp.sqrt(nv / c2) + ADAM_EPS) + ADAM_WD * w_ref[...])

    blk = pl.BlockSpec((tr, c), lambda i: (i, 0))
    return pl.pallas_call(
        body, name=name, grid=(r // tr,),
        in_specs=[pl.BlockSpec((N_DEV, tr, c), lambda i: (0, i, 0)), blk, blk, blk],
        out_specs=[blk] * 4,
        out_shape=[jax.ShapeDtypeStruct((r, c), F32)] * 4,
        compiler_params=_params(1),
    )(parts, w, m, v)


def _rope_tables(positions):
    half = ROPE_DIM // 2
    inv_freq = ROPE_THETA ** (-jnp.arange(0, ROPE_DIM, 2, dtype=F32) / ROPE_DIM)
    ang = positions.astype(F32)[:, None] * inv_freq
    cos, sin = jnp.cos(ang), jnp.sin(ang)
    t = positions.shape[0]
    zeros = lambda w: jnp.zeros((t, w), F32)
    c = jnp.concatenate([cos, cos, jnp.ones((t, ATTN_HEAD_DIM - ROPE_DIM), F32)], axis=1)
    s1 = jnp.concatenate([-sin, zeros(ATTN_HEAD_DIM - half)], axis=1)
    s2 = jnp.concatenate([zeros(half), sin, zeros(ATTN_HEAD_DIM - ROPE_DIM)], axis=1)
    return tuple(jnp.tile(a, (1, LANES // ATTN_HEAD_DIM)) for a in (c, s1, s2))


def _ffn_layer_fwd(h, nw, w, carry):
    (out, g, u, hn), landed = _ffn_fwd(h, nw, w["wg"], w["wu"], w["wd"], carry, False)
    return out, (h, g, u, hn), landed


def _ffn_layer_bwd(dy, saved, nw, w, carry):
    h, g, u, hn = saved
    (dh, dg, du, act, dz, dnw), landed = _ffn_bwd(dy, h, nw, g, u, w["wg"], w["wu"], w["wd"], carry, True)
    dwg = _matmul_tn(hn, dg, FFN_TF, "ffn_dwg")
    dwu = _matmul_tn(hn, du, FFN_TF, "ffn_dwu")
    dwd = _matmul_tn(act, dz, D_MODEL, "ffn_dwd")
    return dh, dnw, (dwg, dwu, dwd), landed


def _attn_layer_fwd(h, nw, w, ropes):
    q, k, v, hn = _attn_qkv_fwd(h, nw, w["w_in"], w["b_in"], *ropes)
    o = _attn_core_fwd(q, k, v, w["sinks"])
    out = _proj_out_fwd(o, w["w_out"], w["b_out"], h)
    return out, (h, hn, q, k, v, o)


def _attn_layer_bwd(dy, saved, nw, w, ropes):
    h, hn, q, k, v, o = saved
    do, dyb, db_out = _proj_out_bwd(dy, w["w_out"])
    dw_out = _matmul_tn(o, dyb, D_MODEL, "attn_dw_out")
    dq, dk, dv, dsink = _attn_core_bwd(q, k, v, do, w["sinks"])
    dh, dqkv, db_in, dnw = _attn_qkv_bwd(dy, h, nw, w["w_in"], dq, dk, dv, *ropes)
    dw_in = _matmul_tn(hn, dqkv, ATTN_IN, "attn_dw_in")
    return dh, dnw, dict(w_in=dw_in, b_in=db_in, sinks=dsink[:, 0], w_out=dw_out, b_out=db_out)


def _gdn_layer_fwd(h, nw, w):
    x, z, ba, hn = _gdn_proj_fwd(h, nw, w["w_qkv"], w["w_z"], w["w_ba"])
    q, k, v, gates = _gdn_conv_fwd(x, w["conv_w"], ba, w["gp"])
    u, ww, qd, kd, aqk, cd = _gdn_intra_fwd(q, k, v, gates)
    o, st = _gdn_scan_fwd(u, ww, qd, kd, aqk, cd)
    out, gated = _gdn_out_fwd(o, z, w["norm_w"], w["w_out"], h)
    return out, (h, hn, x, z, ba, q, k, v, gates, u, ww, qd, kd, aqk, cd, st, o, gated)


def _gdn_layer_bwd(dy, saved, nw, w):
    h, hn, x, z, ba, q, k, v, gates, u, ww, qd, kd, aqk, cd, st, o, gated = saved
    do, dz, dyb, dnorm_w = _gdn_out_bwd(dy, w["w_out"], o, z, w["norm_w"])
    dw_out = _matmul_tn(gated, dyb, D_MODEL, "gdn_dw_out")
    du, dw, dqd, dkd, daqk, daqkt, dcd = _gdn_scan_bwd(u, ww, qd, kd, aqk, cd, st, do)
    dq, dk, dv, dgates = _gdn_intra_bwd(q, k, v, gates, u, ww, du, dw, dqd, dkd, daqk, daqkt, dcd)
    dc, dba, dcw, dgp = _gdn_conv_bwd_a(x, w["conv_w"], ba, w["gp"], dq, dk, dv, dgates)
    dx = _gdn_conv_bwd_b(dc, w["conv_w"])
    dh, dnw = _gdn_proj_bwd(dy, h, nw, w["w_qkv"], w["w_z"], w["w_ba"], dx, dz, dba)
    dw_qkv = _matmul_tn(hn, dx, GDN_CONV_W // 2, "gdn_dw_qkv")
    dw_z = _matmul_tn(hn, dz, D_MODEL, "gdn_dw_z")
    dw_ba = _matmul_tn(hn, dba, LANES, "gdn_dw_ba")
    dw_in = jnp.concatenate([dw_qkv, dw_z, dw_ba[:, :2 * GDN_HEADS]], axis=1)
    lo, hi = GDN_HEADS, 2 * GDN_HEADS
    return dh, dnw, dict(w_in=dw_in, conv_w=dcw[:GDN_CONV], A_log=dgp[0, lo:hi], dt_bias=dgp[1, lo:hi],
                         norm_w=dnorm_w[0], w_out=dw_out)


def _local_step(x, positions, target, norms, final_norm, plan):
    ropes = _rope_tables(positions)
    h = x
    saved = []
    for layer in range(DEPTH):
        h, s1, landed = _ffn_layer_fwd(h, norms["ffn1"][layer], plan.weights("ffn1", layer),
                                       plan.fwd_carry("ffn1", layer))
        plan.fwd_landed(landed)
        if layer % 2 == 0:
            h, s2 = _attn_layer_fwd(h, norms["mix"][layer], plan.weights("mix", layer), ropes)
        else:
            h, s2 = _gdn_layer_fwd(h, norms["mix"][layer], plan.weights("mix", layer))
        h, s3, landed = _ffn_layer_fwd(h, norms["ffn2"][layer], plan.weights("ffn2", layer),
                                       plan.fwd_carry("ffn2", layer))
        plan.fwd_landed(landed)
        saved.append((s1, s2, s3))
    loss, dh, dfinal = _loss_head(h, final_norm, target)

    g_norm = {k: [None] * DEPTH for k in ("ffn1", "mix", "ffn2")}
    for layer in reversed(range(DEPTH)):
        s1, s2, s3 = saved[layer]
        dh, g_norm["ffn2"][layer], gw, landed = _ffn_layer_bwd(
            dh, s3, norms["ffn2"][layer], plan.weights("ffn2", layer), plan.bwd_carry("ffn2", layer))
        plan.bwd_landed(landed)
        plan.grads("ffn2", layer, gw)
        if layer % 2 == 0:
            dh, g_norm["mix"][layer], gw = _attn_layer_bwd(dh, s2, norms["mix"][layer], plan.weights("mix", layer),
                                                           ropes)
        else:
            dh, g_norm["mix"][layer], gw = _gdn_layer_bwd(dh, s2, norms["mix"][layer], plan.weights("mix", layer))
        plan.grads("mix", layer, gw)
        dh, g_norm["ffn1"][layer], gw, landed = _ffn_layer_bwd(
            dh, s1, norms["ffn1"][layer], plan.weights("ffn1", layer), plan.bwd_carry("ffn1", layer))
        plan.bwd_landed(landed)
        plan.grads("ffn1", layer, gw)
    return loss, dh, dfinal, g_norm


def _cols_full(g):
    return jnp.transpose(g, (1, 0, 2)).reshape(g.shape[1], -1)


def _rows_full(g):
    return g.reshape(-1, g.shape[-1])


class _ShardedWeights:
    def __init__(self, shards, small):
        self.shards, self.small = shards, small
        self.whole, self.pending, self.received, self.small_grads = {}, {}, {}, {}
        self.in_flight = []
        first = [("ffn1", 0), ("mix", 0)]
        self.in_flight = first
        self.fwd_landed(_exchange(self._shards_of(first), False, "gather_first"))

    def _group(self, key):
        kind, layer = key
        j = layer // 2
        if kind != "mix":
            return [self.shards[kind + "_w_gate_up"][layer], self.shards[kind + "_w_down"][layer]]
        if layer % 2 == 0:
            return [self.shards["attn_w_in"][j], self.shards["attn_w_out"][j]]
        return [self.shards["gdn_w_in"][j], self.shards["gdn_w_out"][j], self.shards["gdn_conv_w"][j]]

    def _shards_of(self, keys):
        return [a for key in keys for a in self._group(key)]

    def weights(self, kind, layer):
        return self.whole[(kind, layer)]

    def fwd_carry(self, kind, layer):
        if kind == "ffn1":
            keys = [("ffn2", layer)]
        else:
            keys = [("ffn1", layer + 1), ("mix", layer + 1)] if layer + 1 < DEPTH else []
        self.in_flight = keys
        return self._shards_of(keys)

    def fwd_landed(self, landed):
        landed = list(landed)
        for key in self.in_flight:
            kind, layer = key
            j = layer // 2
            sm = self.small
            if kind != "mix":
                gu, dn = landed[:2]
                w = dict(wg=_cols_full(gu[:4]), wu=_cols_full(gu[4:]), wd=_rows_full(dn))
                landed = landed[2:]
            elif layer % 2 == 0:
                w = dict(w_in=_cols_full(landed[0]), w_out=_rows_full(landed[1]), b_in=sm["attn_b_in"][j][None],
                         sinks=sm["attn_sinks"][j][None], b_out=sm["attn_b_out"][j][None])
                landed = landed[2:]
            else:
                w_in = _cols_full(landed[0])
                w_ba = jnp.pad(w_in[:, GDN_CONV_W + D_MODEL:], ((0, 0), (0, LANES - 2 * GDN_HEADS)))
                gp = jnp.pad(jnp.stack([sm["gdn_A_log"][j], sm["gdn_dt_bias"][j]]),
                             ((0, 6), (GDN_HEADS, LANES - 2 * GDN_HEADS)))
                w = dict(w_qkv=w_in[:, :GDN_CONV_W], w_z=w_in[:, GDN_CONV_W:GDN_CONV_W + D_MODEL], w_ba=w_ba,
                         conv_w=_cols_full(landed[2]), gp=gp, norm_w=sm["gdn_norm_w"][j][None],
                         w_out=_rows_full(landed[1]))
                landed = landed[3:]
            self.whole[key] = w
        self.in_flight = []

    def grads(self, kind, layer, g):
        if kind != "mix":
            dwg, dwu, dwd = g
            parts = [jnp.concatenate([_cols_shards(dwg, 4), _cols_shards(dwu, 4)], axis=0).astype(BF16),
                     _rows_shards(dwd).astype(BF16)]
        else:
            parts = [_cols_shards(g["w_in"]).astype(BF16), _rows_shards(g["w_out"]).astype(BF16)]
            if layer % 2 == 1:
                parts.append(_cols_shards(g["conv_w"]))
            self.small_grads[layer] = g
        self.pending[(kind, layer)] = parts

    def bwd_carry(self, kind, layer):
        if kind == "ffn1":
            keys = [("ffn2", layer), ("mix", layer)]
        else:
            keys = [("ffn1", layer + 1)] if layer + 1 < DEPTH else []
        self.in_flight = keys
        return [a for key in keys for a in self.pending[key]]

    def bwd_landed(self, landed):
        landed = list(landed)
        for key in self.in_flight:
            n = len(self.pending.pop(key))
            self.received[key], landed = landed[:n], landed[n:]
        self.in_flight = []

    def finish(self):
        self.in_flight = list(self.pending)
        self.bwd_landed(_exchange([a for key in self.in_flight for a in self.pending[key]], True, "scatter_last"))


def _cols_shards(full, n=N_DEV):
    r = full.shape[0]
    return jnp.transpose(full.reshape(r, n, -1), (1, 0, 2))


def _rows_shards(full):
    return full.reshape(N_DEV, -1, full.shape[-1])


_SMALL = ("ffn1_norm", "mix_norm", "ffn2_norm", "attn_b_in", "attn_sinks", "attn_b_out",
          "gdn_A_log", "gdn_dt_bias", "gdn_norm_w", "final_norm")
_BIG = ("ffn1_w_gate_up", "ffn1_w_down", "ffn2_w_gate_up", "ffn2_w_down", "attn_w_in", "attn_w_out",
        "gdn_w_in", "gdn_conv_w", "gdn_w_out")
_WEIGHTS = ("ffn1_norm", "ffn1_w_gate_up", "ffn1_w_down", "mix_norm", "ffn2_norm", "ffn2_w_gate_up",
            "ffn2_w_down", "attn_w_in", "attn_b_in", "attn_sinks", "attn_w_out", "attn_b_out", "gdn_w_in",
            "gdn_conv_w", "gdn_A_log", "gdn_dt_bias", "gdn_norm_w", "gdn_w_out", "final_norm")


def _pack_small(vals):
    flat = jnp.concatenate([v.reshape(-1).astype(F32) for v in vals])
    pad = (-flat.shape[0]) % (8 * LANES)
    return jnp.pad(flat, (0, pad)).reshape(-1, LANES)


def _unpack_small(packed, shapes):
    flat = packed.reshape(-1)
    out, off = [], 0
    for s in shapes:
        size = 1
        for d in s:
            size *= d
        out.append(flat[off:off + size].reshape(s))
        off += size
    return out


def kernel(x, positions, ffn1_norm, ffn1_w_gate_up, ffn1_w_down, mix_norm, ffn2_norm, ffn2_w_gate_up, ffn2_w_down, attn_w_in, attn_b_in, attn_sinks, attn_w_out, attn_b_out, gdn_w_in, gdn_conv_w, gdn_A_log, gdn_dt_bias, gdn_norm_w, gdn_w_out, final_norm, loss_target, m_ffn1_norm, m_ffn1_w_gate_up, m_ffn1_w_down, m_mix_norm, m_ffn2_norm, m_ffn2_w_gate_up, m_ffn2_w_down, m_attn_w_in, m_attn_b_in, m_attn_sinks, m_attn_w_out, m_attn_b_out, m_gdn_w_in, m_gdn_conv_w, m_gdn_A_log, m_gdn_dt_bias, m_gdn_norm_w, m_gdn_w_out, m_final_norm, v_ffn1_norm, v_ffn1_w_gate_up, v_ffn1_w_down, v_mix_norm, v_ffn2_norm, v_ffn2_w_gate_up, v_ffn2_w_down, v_attn_w_in, v_attn_b_in, v_attn_sinks, v_attn_w_out, v_attn_b_out, v_gdn_w_in, v_gdn_conv_w, v_gdn_A_log, v_gdn_dt_bias, v_gdn_norm_w, v_gdn_w_out, v_final_norm):
    args = dict(locals())
    wts = {n: args[n] for n in _WEIGHTS}
    moms = {n: args["m_" + n] for n in _WEIGHTS}
    vels = {n: args["v_" + n] for n in _WEIGHTS}

    shards = {n: wts[n] if n == "gdn_conv_w" else wts[n].astype(BF16) for n in _BIG}
    plan = _ShardedWeights(shards, wts)
    norms = dict(ffn1=ffn1_norm[:, None, :], mix=mix_norm[:, None, :], ffn2=ffn2_norm[:, None, :])
    loss, grad_x, dfinal, g_norm = _local_step(x[0], positions[0], loss_target[0], norms, final_norm[None], plan)
    plan.finish()

    layers = range(DEPTH)
    got = lambda kind, ls, i: jnp.stack([plan.received[(kind, l)][i] for l in ls], axis=1)
    received = {"attn_w_in": got("mix", layers[0::2], 0), "attn_w_out": got("mix", layers[0::2], 1),
                "gdn_w_in": got("mix", layers[1::2], 0), "gdn_w_out": got("mix", layers[1::2], 1),
                "gdn_conv_w": got("mix", layers[1::2], 2)}
    for kind in ("ffn1", "ffn2"):
        received[kind + "_w_gate_up"] = got(kind, layers, 0)
        received[kind + "_w_down"] = got(kind, layers, 1)
    g_attn = [plan.small_grads[l] for l in layers[0::2]]
    g_gdn = [plan.small_grads[l] for l in layers[1::2]]


    small_g = dict(
        ffn1_norm=jnp.concatenate(g_norm["ffn1"], axis=0), mix_norm=jnp.concatenate(g_norm["mix"], axis=0),
        ffn2_norm=jnp.concatenate(g_norm["ffn2"], axis=0),
        attn_b_in=jnp.concatenate([g["b_in"] for g in g_attn], axis=0),
        attn_sinks=jnp.stack([g["sinks"] for g in g_attn]),
        attn_b_out=jnp.concatenate([g["b_out"] for g in g_attn], axis=0),
        gdn_A_log=jnp.stack([g["A_log"] for g in g_gdn]), gdn_dt_bias=jnp.stack([g["dt_bias"] for g in g_gdn]),
        gdn_norm_w=jnp.stack([g["norm_w"] for g in g_gdn]), final_norm=dfinal[0])
    small_parts = _exchange([_pack_small([small_g[n] for n in _SMALL] + [loss[0, :1]])], False, "gather_small")[0]
    pad1 = jnp.zeros((1,), F32)
    sw = _pack_small([wts[n] for n in _SMALL] + [pad1])
    sm = _pack_small([moms[n] for n in _SMALL] + [pad1])
    sv = _pack_small([vels[n] for n in _SMALL] + [pad1])
    shapes = [wts[n].shape for n in _SMALL] + [(1,)]
    small_out = [_unpack_small(o, shapes) for o in _adamw(small_parts, sw, sm, sv, "adamw_small")]
    results = {n: tuple(o[i] for o in small_out) for i, n in enumerate(_SMALL)}
    loss_total = small_out[0][-1][0]

    for n in _BIG:
        shape = wts[n].shape
        two_d = lambda a: a.reshape(-1, shape[-1])
        p = received[n].reshape(N_DEV, -1, shape[-1])
        outs = _adamw(p, two_d(wts[n]), two_d(moms[n]), two_d(vels[n]), "adamw_" + n)
        results[n] = tuple(o.reshape(shape) for o in outs)

    return (loss_total, grad_x[None],
            *[results[n][0] for n in _WEIGHTS], *[results[n][1] for n in _WEIGHTS],
            *[results[n][2] for n in _WEIGHTS], *[results[n][3] for n in _WEIGHTS])
```

```python
import jax
import jax.numpy as jnp
from jax import lax
from jax.experimental import pallas as pl
from jax.experimental.pallas import tpu as pltpu

F32 = jnp.float32
BF16 = jnp.bfloat16

D_MODEL = 1024
DEPTH = 4
D_FF = 2816
NORM_EPS = 1e-6
N_DEV = 8

ATTN_Q_HEADS = 16
ATTN_KV_HEADS = 4
ATTN_HEAD_DIM = 64
ATTN_GROUP = 4
ATTN_BLOCK = 128
ROPE_DIM = 16
ROPE_THETA = 500000.0
ATTN_Q_W = 1024
ATTN_KV_W = 256
ATTN_IN = 1536
ATTN_SCALE = ATTN_HEAD_DIM ** -0.5

GDN_HEADS = 8
GDN_DK = 128
GDN_CONV = 4
GDN_CHUNK = 64
GDN_CONV_W = 3072
GDN_IN = 4112
GDN_QSCALE = GDN_DK ** -0.5
GDN_ROWS = 512
GDN_SCAN_HEADS = 2
GDN_SUB_SHIFT = 4

ADAM_LR = 0.001
ADAM_B1 = 0.9
ADAM_B2 = 0.999
ADAM_EPS = 1e-08
ADAM_WD = 0.01
ADAM_STEP = 10

LANES = 128
NEG_BIG = -1e30
VMEM_LIMIT_BYTES = 56 * 1024 * 1024
MESH_ID = pl.DeviceIdType.MESH


def _params(n_axes, vmem_limit_bytes=VMEM_LIMIT_BYTES):
    return pltpu.CompilerParams(dimension_semantics=("arbitrary",) * n_axes,
                                vmem_limit_bytes=vmem_limit_bytes)


def _nn(a, b):
    return jnp.dot(a, b, preferred_element_type=F32)


def _nt(a, b):
    return lax.dot_general(a, b, (((1,), (1,)), ((), ())), preferred_element_type=F32)


def _tn(a, b):
    return lax.dot_general(a, b, (((0,), (0,)), ((), ())), preferred_element_type=F32)


def _bnn(a, b, precision=None):
    return lax.dot_general(a, b, (((2,), (1,)), ((0,), (0,))), precision=precision,
                           preferred_element_type=F32)


def _bnt(a, b, precision=None):
    return lax.dot_general(a, b, (((2,), (2,)), ((0,), (0,))), precision=precision,
                           preferred_element_type=F32)


def _sigmoid(x):
    return 1.0 / (1.0 + jnp.exp(-x))


def _rms_stats(x):
    r = lax.rsqrt(jnp.mean(x * x, axis=-1, keepdims=True) + NORM_EPS)
    return r, x * r


def _norm_bwd(x, nw, dhn):
    r, xhat = _rms_stats(x)
    dxh = dhn * nw
    dx = r * (dxh - xhat * jnp.mean(dxh * xhat, axis=-1, keepdims=True))
    dnw = jnp.sum(dhn * xhat, axis=0, keepdims=True)
    return dx, dnw


def _row_tile(t, pref):
    return min(t, pref)


FFN_TM = 512
FFN_BWD_TM = 512
FFN_BWD_TF = 1408
FFN_BWD_VMEM_LIMIT_BYTES = 61 * 1024 * 1024
FFN_TF = 1408


def _carried(body, n_in, n_out, carry, scatter, last_step):
    nc = len(carry)
    if not nc:
        return body

    def wrapped(*refs):
        ins, cin = refs[:n_in], refs[n_in:n_in + nc]
        outs = refs[n_in + nc:n_in + nc + n_out]
        cout = refs[n_in + nc + n_out:n_in + 2 * nc + n_out]
        scratch = refs[n_in + 2 * nc + n_out:]
        sems = scratch[len(scratch) - 3:]
        i, j = pl.program_id(0), pl.program_id(1)

        @pl.when(jnp.logical_and(i == 0, j == 0))
        def _():
            for cp in _exchange_copies(cin, cout, *sems, scatter):
                cp.start()

        body(*ins, *outs, *scratch[:len(scratch) - 3])

        @pl.when(jnp.logical_and(i == last_step[0], j == last_step[1]))
        def _():
            for cp in _exchange_copies(cin, cout, *sems, scatter):
                cp.wait()

    return wrapped


def _ffn_fwd(h, nw, wg, wu, wd, carry=(), scatter=False):
    t = h.shape[0]
    tm, tf = _row_tile(t, FFN_TM), FFN_TF
    nj = D_FF // tf
    nc = len(carry)

    def body(h_ref, nw_ref, wg_ref, wu_ref, wd_ref, out_ref, g_ref, u_ref, hn_ref, acc_ref):
        j = pl.program_id(1)

        @pl.when(j == 0)
        def _():
            _, xhat = _rms_stats(h_ref[...])
            hn_ref[...] = (xhat * nw_ref[...]).astype(BF16)
            acc_ref[...] = jnp.zeros_like(acc_ref)

        hn = hn_ref[...]
        g = _nn(hn, wg_ref[...])
        u = _nn(hn, wu_ref[...])
        g_ref[...] = g.astype(BF16)
        u_ref[...] = u.astype(BF16)
        a = (g * _sigmoid(g) * u).astype(BF16)
        acc_ref[...] += _nn(a, wd_ref[...])

        @pl.when(j == nj - 1)
        def _():
            out_ref[...] = h_ref[...] + 0.5 * acc_ref[...]

    outs = pl.pallas_call(
        _carried(body, 5, 4, carry, scatter, (t // tm - 1, nj - 1)),
        name="ffn_fwd_x" if nc else "ffn_fwd", grid=(t // tm, nj),
        in_specs=[pl.BlockSpec((tm, D_MODEL), lambda i, j: (i, 0)),
                  pl.BlockSpec((1, D_MODEL), lambda i, j: (0, 0)),
                  pl.BlockSpec((D_MODEL, tf), lambda i, j: (0, j)),
                  pl.BlockSpec((D_MODEL, tf), lambda i, j: (0, j)),
                  pl.BlockSpec((tf, D_MODEL), lambda i, j: (j, 0))] + [_ANY] * nc,
        out_specs=[pl.BlockSpec((tm, D_MODEL), lambda i, j: (i, 0)),
                   pl.BlockSpec((tm, tf), lambda i, j: (i, j)),
                   pl.BlockSpec((tm, tf), lambda i, j: (i, j)),
                   pl.BlockSpec((tm, D_MODEL), lambda i, j: (i, 0))] + [_ANY] * nc,
        out_shape=[jax.ShapeDtypeStruct((t, D_MODEL), F32),
                   jax.ShapeDtypeStruct((t, D_FF), BF16),
                   jax.ShapeDtypeStruct((t, D_FF), BF16),
                   jax.ShapeDtypeStruct((t, D_MODEL), BF16)] + _exchange_shapes(carry, scatter),
        scratch_shapes=[pltpu.VMEM((tm, D_MODEL), F32)] + (_exchange_sems(nc) if nc else []),
        compiler_params=_params(2),
    )(h, nw, wg, wu, wd, *carry)
    return outs[:4], outs[4:]


def _ffn_bwd(dy, h, nw, g, u, wg, wu, wd, carry=(), scatter=True):
    t = h.shape[0]
    tm, tf = _row_tile(t, FFN_BWD_TM), FFN_BWD_TF
    nj = D_FF // tf
    nc = len(carry)

    def body(dy_ref, h_ref, nw_ref, g_ref, u_ref, wg_ref, wu_ref, wd_ref,
             dh_ref, dg_ref, du_ref, a_ref, dz_ref, dnw_ref, acc_ref):
        i, j = pl.program_id(0), pl.program_id(1)

        @pl.when(j == 0)
        def _():
            dz_ref[...] = (0.5 * dy_ref[...]).astype(BF16)
            acc_ref[...] = jnp.zeros_like(acc_ref)

        @pl.when(jnp.logical_and(i == 0, j == 0))
        def _():
            dnw_ref[...] = jnp.zeros_like(dnw_ref)

        da = _nt(dz_ref[...], wd_ref[...])
        gv = g_ref[...].astype(F32)
        uv = u_ref[...].astype(F32)
        sig = _sigmoid(gv)
        silu = gv * sig
        dg = (da * uv * (sig * (1.0 + gv * (1.0 - sig)))).astype(BF16)
        du = (da * silu).astype(BF16)
        dg_ref[...] = dg
        du_ref[...] = du
        a_ref[...] = (silu * uv).astype(BF16)
        acc_ref[...] += _nt(dg, wg_ref[...]) + _nt(du, wu_ref[...])

        @pl.when(j == nj - 1)
        def _():
            dx, dnw = _norm_bwd(h_ref[...], nw_ref[...], acc_ref[...])
            dh_ref[...] = dy_ref[...] + dx
            dnw_ref[...] += dnw

    outs = pl.pallas_call(
        _carried(body, 8, 6, carry, scatter, (t // tm - 1, nj - 1)),
        name="ffn_bwd_x" if nc else "ffn_bwd", grid=(t // tm, nj),
        in_specs=[pl.BlockSpec((tm, D_MODEL), lambda i, j: (i, 0)),
                  pl.BlockSpec((tm, D_MODEL), lambda i, j: (i, 0)),
                  pl.BlockSpec((1, D_MODEL), lambda i, j: (0, 0)),
                  pl.BlockSpec((tm, tf), lambda i, j: (i, j)),
                  pl.BlockSpec((tm, tf), lambda i, j: (i, j)),
                  pl.BlockSpec((D_MODEL, tf), lambda i, j: (0, j)),
                  pl.BlockSpec((D_MODEL, tf), lambda i, j: (0, j)),
                  pl.BlockSpec((tf, D_MODEL), lambda i, j: (j, 0))] + [_ANY] * nc,
        out_specs=[pl.BlockSpec((tm, D_MODEL), lambda i, j: (i, 0)),
                   pl.BlockSpec((tm, tf), lambda i, j: (i, j)),
                   pl.BlockSpec((tm, tf), lambda i, j: (i, j)),
                   pl.BlockSpec((tm, tf), lambda i, j: (i, j)),
                   pl.BlockSpec((tm, D_MODEL), lambda i, j: (i, 0)),
                   pl.BlockSpec((1, D_MODEL), lambda i, j: (0, 0))] + [_ANY] * nc,
        out_shape=[jax.ShapeDtypeStruct((t, D_MODEL), F32),
                   jax.ShapeDtypeStruct((t, D_FF), BF16),
                   jax.ShapeDtypeStruct((t, D_FF), BF16),
                   jax.ShapeDtypeStruct((t, D_FF), BF16),
                   jax.ShapeDtypeStruct((t, D_MODEL), BF16),
                   jax.ShapeDtypeStruct((1, D_MODEL), F32)] + _exchange_shapes(carry, scatter),
        scratch_shapes=[pltpu.VMEM((tm, D_MODEL), F32)] + (_exchange_sems(nc) if nc else []),
        compiler_params=_params(2, FFN_BWD_VMEM_LIMIT_BYTES),
    )(dy, h, nw, g, u, wg, wu, wd, *carry)
    return outs[:6], outs[6:]


def _matmul_tn(a, b, tn, name):
    k, m = a.shape
    n = b.shape[1]
    tm = min(m, 1408 if m % 1408 == 0 else 1024)
    tk = min(k, 1024)

    def body(a_ref, b_ref, o_ref):
        @pl.when(pl.program_id(2) == 0)
        def _():
            o_ref[...] = jnp.zeros_like(o_ref)

        o_ref[...] += _tn(a_ref[...], b_ref[...])

    return pl.pallas_call(
        body, name=name, grid=(m // tm, n // tn, k // tk),
        in_specs=[pl.BlockSpec((tk, tm), lambda i, j, kk: (kk, i)),
                  pl.BlockSpec((tk, tn), lambda i, j, kk: (kk, j))],
        out_specs=pl.BlockSpec((tm, tn), lambda i, j, kk: (i, j)),
        out_shape=jax.ShapeDtypeStruct((m, n), F32),
        compiler_params=_params(3),
    )(a, b)


ATTN_TM = 512


def _rope(t, c, s1, s2):
    return t * c + pltpu.roll(t, LANES - ROPE_DIM // 2, 1) * s1 + pltpu.roll(t, ROPE_DIM // 2, 1) * s2


def _rope_bwd(d, c, s1, s2):
    return d * c + pltpu.roll(d * s1, ROPE_DIM // 2, 1) + pltpu.roll(d * s2, LANES - ROPE_DIM // 2, 1)


def _attn_qkv_fwd(h, nw, w_in, b_in, rc, rs1, rs2):
    t = h.shape[0]
    tm = _row_tile(t, ATTN_TM)

    def body(h_ref, nw_ref, w_ref, b_ref, c_ref, s1_ref, s2_ref, q_ref, k_ref, v_ref, hn_ref):
        _, xhat = _rms_stats(h_ref[...])
        hn = (xhat * nw_ref[...]).astype(BF16)
        hn_ref[...] = hn
        qkv = _nn(hn, w_ref[...]) + b_ref[...]
        c, s1, s2 = c_ref[...], s1_ref[...], s2_ref[...]
        for s in range(ATTN_Q_W // LANES):
            q_ref[:, s * LANES:(s + 1) * LANES] = _rope(qkv[:, s * LANES:(s + 1) * LANES], c, s1, s2).astype(BF16)
        for s in range(ATTN_KV_W // LANES):
            lo = ATTN_Q_W + s * LANES
            k_ref[:, s * LANES:(s + 1) * LANES] = _rope(qkv[:, lo:lo + LANES], c, s1, s2).astype(BF16)
        v_ref[...] = qkv[:, ATTN_Q_W + ATTN_KV_W:].astype(BF16)

    row = lambda w: pl.BlockSpec((tm, w), lambda i: (i, 0))
    full = lambda a: pl.BlockSpec(a.shape, lambda i: (0, 0))
    return pl.pallas_call(
        body, name="attn_qkv_fwd", grid=(t // tm,),
        in_specs=[row(D_MODEL), full(nw), full(w_in), full(b_in), row(LANES), row(LANES), row(LANES)],
        out_specs=[row(ATTN_Q_W), row(ATTN_KV_W), row(ATTN_KV_W), row(D_MODEL)],
        out_shape=[jax.ShapeDtypeStruct((t, ATTN_Q_W), BF16),
                   jax.ShapeDtypeStruct((t, ATTN_KV_W), BF16),
                   jax.ShapeDtypeStruct((t, ATTN_KV_W), BF16),
                   jax.ShapeDtypeStruct((t, D_MODEL), BF16)],
        compiler_params=_params(1),
    )(h, nw, w_in, b_in, rc, rs1, rs2)


def _attn_group(q_ref, kc_ref, kp_ref, vc_ref, vp_ref, sinks_ref, hk, n):
    hd = ATTN_HEAD_DIM
    cols = slice(hk * hd, (hk + 1) * hd)
    krow = lax.broadcasted_iota(jnp.int32, (2 * ATTN_BLOCK, hd), 0)
    kcat = jnp.concatenate([kp_ref[:, cols], kc_ref[:, cols]], axis=0)
    vcat = jnp.concatenate([vp_ref[:, cols], vc_ref[:, cols]], axis=0)
    kcat = jnp.where(krow == 0, jnp.zeros_like(kcat), kcat)
    vcat = jnp.where(krow == 0, jnp.zeros_like(vcat), vcat)
    heads = [hk * ATTN_GROUP + g for g in range(ATTN_GROUP)]
    qs = jnp.concatenate([q_ref[:, hq * hd:(hq + 1) * hd] for hq in heads], axis=0)
    s = _nt(qs, kcat) * ATTN_SCALE
    rows = ATTN_GROUP * ATTN_BLOCK
    ri = lax.broadcasted_iota(jnp.int32, (rows, 2 * ATTN_BLOCK), 0) & (ATTN_BLOCK - 1)
    cj = lax.broadcasted_iota(jnp.int32, (rows, 2 * ATTN_BLOCK), 1)
    first = jnp.where(n > 0, 0, 2 * ATTN_BLOCK)
    valid = jnp.logical_or(jnp.logical_and(cj < ATTN_BLOCK, cj > ri + first),
                           jnp.logical_and(cj >= ATTN_BLOCK, cj - ATTN_BLOCK <= ri))
    sink = jnp.concatenate([jnp.full((ATTN_BLOCK, 2 * ATTN_BLOCK), sinks_ref[hq], F32) for hq in heads], axis=0)
    s = jnp.where(valid, s, jnp.where(cj == 0, sink, NEG_BIG))
    p = jnp.exp(s - jnp.max(s, axis=-1, keepdims=True))
    probs = p / jnp.sum(p, axis=-1, keepdims=True)
    return heads, qs, kcat, vcat, probs, krow


def _attn_core_fwd(q, k, v, sinks):
    t = q.shape[0]
    nb = t // ATTN_BLOCK

    def body(q_ref, kc_ref, kp_ref, vc_ref, vp_ref, sinks_ref, o_ref):
        n = pl.program_id(0)
        for hk in range(ATTN_KV_HEADS):
            heads, _, _, vcat, probs, _ = _attn_group(q_ref, kc_ref, kp_ref, vc_ref, vp_ref, sinks_ref, hk, n)
            o = _nn(probs.astype(BF16), vcat)
            for g, hq in enumerate(heads):
                o_ref[:, hq * ATTN_HEAD_DIM:(hq + 1) * ATTN_HEAD_DIM] = (
                    o[g * ATTN_BLOCK:(g + 1) * ATTN_BLOCK].astype(BF16))

    cur = lambda w: pl.BlockSpec((ATTN_BLOCK, w), lambda n: (n, 0))
    prev = lambda w: pl.BlockSpec((ATTN_BLOCK, w), lambda n: (jnp.maximum(n - 1, 0), 0))
    return pl.pallas_call(
        body, name="attn_core_fwd", grid=(nb,),
        in_specs=[cur(ATTN_Q_W), cur(ATTN_KV_W), prev(ATTN_KV_W), cur(ATTN_KV_W), prev(ATTN_KV_W),
                  pl.BlockSpec(memory_space=pltpu.SMEM)],
        out_specs=cur(ATTN_Q_W),
        out_shape=jax.ShapeDtypeStruct((t, ATTN_Q_W), BF16),
        compiler_params=_params(1),
    )(q, k, k, v, v, sinks.reshape(-1))


def _attn_core_bwd(q, k, v, do, sinks):
    t = q.shape[0]
    nb = t // ATTN_BLOCK
    hd = ATTN_HEAD_DIM

    def body(q_ref, kc_ref, kp_ref, vc_ref, vp_ref, do_ref, sinks_ref,
             dq_ref, dk_ref, dv_ref, dsink_ref, ck_ref, cv_ref):
        n = pl.program_id(0)

        @pl.when(n == 0)
        def _():
            dsink_ref[...] = jnp.zeros_like(dsink_ref)
            ck_ref[...] = jnp.zeros_like(ck_ref)
            cv_ref[...] = jnp.zeros_like(cv_ref)

        @pl.when(n == nb)
        def _():
            dk_ref[...] = ck_ref[...]
            dv_ref[...] = cv_ref[...]

        @pl.when(n < nb)
        def _():
            for hk in range(ATTN_KV_HEADS):
                heads, qs, kcat, vcat, probs, krow = _attn_group(
                    q_ref, kc_ref, kp_ref, vc_ref, vp_ref, sinks_ref, hk, n)
                dos = jnp.concatenate([do_ref[:, hq * hd:(hq + 1) * hd] for hq in heads], axis=0)
                dp = _nt(dos, vcat)
                delta = jnp.sum(probs * dp, axis=-1, keepdims=True)
                dsf = probs * (dp - delta)
                ds = dsf.astype(BF16)
                dqs = _nn(ds, kcat) * ATTN_SCALE
                dkc = jnp.where(krow == 0, 0.0, _tn(ds, qs) * ATTN_SCALE)
                dvc = jnp.where(krow == 0, 0.0, _tn(probs.astype(BF16), dos))
                for g, hq in enumerate(heads):
                    blk = slice(g * ATTN_BLOCK, (g + 1) * ATTN_BLOCK)
                    dq_ref[:, hq * hd:(hq + 1) * hd] = dqs[blk]
                    dsink_ref[hq:hq + 1, :] += jnp.broadcast_to(
                        jnp.sum(dsf[blk, 0:1], axis=0, keepdims=True), (1, LANES))
                cols = slice(hk * hd, (hk + 1) * hd)
                dk_ref[:, cols] = ck_ref[:, cols] + dkc[:ATTN_BLOCK]
                dv_ref[:, cols] = cv_ref[:, cols] + dvc[:ATTN_BLOCK]
                ck_ref[:, cols] = dkc[ATTN_BLOCK:]
                cv_ref[:, cols] = dvc[ATTN_BLOCK:]

    cur = lambda w: pl.BlockSpec((ATTN_BLOCK, w), lambda n: (jnp.minimum(n, nb - 1), 0))
    prev = lambda w: pl.BlockSpec((ATTN_BLOCK, w), lambda n: (jnp.clip(n - 1, 0, nb - 1), 0))
    return pl.pallas_call(
        body, name="attn_core_bwd", grid=(nb + 1,),
        in_specs=[cur(ATTN_Q_W), cur(ATTN_KV_W), prev(ATTN_KV_W), cur(ATTN_KV_W), prev(ATTN_KV_W),
                  cur(ATTN_Q_W), pl.BlockSpec(memory_space=pltpu.SMEM)],
        out_specs=[cur(ATTN_Q_W), prev(ATTN_KV_W), prev(ATTN_KV_W),
                   pl.BlockSpec((ATTN_Q_HEADS, LANES), lambda n: (0, 0))],
        out_shape=[jax.ShapeDtypeStruct((t, ATTN_Q_W), F32),
                   jax.ShapeDtypeStruct((t, ATTN_KV_W), F32),
                   jax.ShapeDtypeStruct((t, ATTN_KV_W), F32),
                   jax.ShapeDtypeStruct((ATTN_Q_HEADS, LANES), F32)],
        scratch_shapes=[pltpu.VMEM((ATTN_BLOCK, ATTN_KV_W), F32),
                        pltpu.VMEM((ATTN_BLOCK, ATTN_KV_W), F32)],
        compiler_params=_params(1),
    )(q, k, k, v, v, do, sinks.reshape(-1))


def _proj_out_fwd(x, w, b, res):
    t = x.shape[0]
    tm = _row_tile(t, 512)

    def body(x_ref, w_ref, b_ref, r_ref, o_ref):
        o_ref[...] = r_ref[...] + _nn(x_ref[...], w_ref[...]) + b_ref[...]

    row = pl.BlockSpec((tm, D_MODEL), lambda i: (i, 0))
    return pl.pallas_call(
        body, name="proj_out_fwd", grid=(t // tm,),
        in_specs=[row, pl.BlockSpec(w.shape, lambda i: (0, 0)), pl.BlockSpec(b.shape, lambda i: (0, 0)), row],
        out_specs=row,
        out_shape=jax.ShapeDtypeStruct((t, D_MODEL), F32),
        compiler_params=_params(1),
    )(x, w, b, res)


def _proj_out_bwd(dy, w):
    t = dy.shape[0]
    tm = _row_tile(t, 512)

    def body(dy_ref, w_ref, dx_ref, dyb_ref, db_ref):
        @pl.when(pl.program_id(0) == 0)
        def _():
            db_ref[...] = jnp.zeros_like(db_ref)

        dy_v = dy_ref[...]
        dyb = dy_v.astype(BF16)
        dyb_ref[...] = dyb
        dx_ref[...] = _nt(dyb, w_ref[...]).astype(BF16)
        db_ref[...] += jnp.sum(dy_v, axis=0, keepdims=True)

    row = pl.BlockSpec((tm, D_MODEL), lambda i: (i, 0))
    return pl.pallas_call(
        body, name="proj_out_bwd", grid=(t // tm,),
        in_specs=[row, pl.BlockSpec(w.shape, lambda i: (0, 0))],
        out_specs=[row, row, pl.BlockSpec((1, D_MODEL), lambda i: (0, 0))],
        out_shape=[jax.ShapeDtypeStruct((t, D_MODEL), BF16),
                   jax.ShapeDtypeStruct((t, D_MODEL), BF16),
                   jax.ShapeDtypeStruct((1, D_MODEL), F32)],
        compiler_params=_params(1),
    )(dy, w)


def _attn_qkv_bwd(dy, h, nw, w_in, dq, dk, dv, rc, rs1, rs2):
    t = h.shape[0]
    tm = _row_tile(t, ATTN_TM)

    def body(dy_ref, h_ref, nw_ref, w_ref, dq_ref, dk_ref, dv_ref, c_ref, s1_ref, s2_ref,
             dh_ref, dqkv_ref, db_ref, dnw_ref, tmp_ref):
        @pl.when(pl.program_id(0) == 0)
        def _():
            db_ref[...] = jnp.zeros_like(db_ref)
            dnw_ref[...] = jnp.zeros_like(dnw_ref)

        c, s1, s2 = c_ref[...], s1_ref[...], s2_ref[...]
        for s in range(ATTN_Q_W // LANES):
            tmp_ref[:, s * LANES:(s + 1) * LANES] = _rope_bwd(dq_ref[:, s * LANES:(s + 1) * LANES], c, s1, s2)
        for s in range(ATTN_KV_W // LANES):
            lo = ATTN_Q_W + s * LANES
            tmp_ref[:, lo:lo + LANES] = _rope_bwd(dk_ref[:, s * LANES:(s + 1) * LANES], c, s1, s2)
        tmp_ref[:, ATTN_Q_W + ATTN_KV_W:] = dv_ref[...]
        dqkv = tmp_ref[...]
        db_ref[...] += jnp.sum(dqkv, axis=0, keepdims=True)
        dqkv_b = dqkv.astype(BF16)
        dqkv_ref[...] = dqkv_b
        dx, dnw = _norm_bwd(h_ref[...], nw_ref[...], _nt(dqkv_b, w_ref[...]))
        dh_ref[...] = dy_ref[...] + dx
        dnw_ref[...] += dnw

    row = lambda w: pl.BlockSpec((tm, w), lambda i: (i, 0))
    full = lambda a: pl.BlockSpec(a.shape, lambda i: (0, 0))
    return pl.pallas_call(
        body, name="attn_qkv_bwd", grid=(t // tm,),
        in_specs=[row(D_MODEL), row(D_MODEL), full(nw), full(w_in), row(ATTN_Q_W), row(ATTN_KV_W),
                  row(ATTN_KV_W), row(LANES), row(LANES), row(LANES)],
        out_specs=[row(D_MODEL), row(ATTN_IN), pl.BlockSpec((1, ATTN_IN), lambda i: (0, 0)),
                   pl.BlockSpec((1, D_MODEL), lambda i: (0, 0))],
        out_shape=[jax.ShapeDtypeStruct((t, D_MODEL), F32),
                   jax.ShapeDtypeStruct((t, ATTN_IN), BF16),
                   jax.ShapeDtypeStruct((1, ATTN_IN), F32),
                   jax.ShapeDtypeStruct((1, D_MODEL), F32)],
        scratch_shapes=[pltpu.VMEM((tm, ATTN_IN), F32)],
        compiler_params=_params(1),
    )(dy, h, nw, w_in, dq, dk, dv, rc, rs1, rs2)


GDN_TM = 256
GDN_CONV_TM = 128


def _gdn_proj_fwd(h, nw, w_qkv, w_z, w_ba):
    t = h.shape[0]
    tm = _row_tile(t, GDN_TM)

    def body(h_ref, nw_ref, wq_ref, wz_ref, wb_ref, x_ref, z_ref, ba_ref, hn_ref):
        _, xhat = _rms_stats(h_ref[...])
        hn = (xhat * nw_ref[...]).astype(BF16)
        hn_ref[...] = hn
        x_ref[...] = _nn(hn, wq_ref[...])
        z_ref[...] = _nn(hn, wz_ref[...])
        ba_ref[...] = _nn(hn, wb_ref[...])

    row = lambda w: pl.BlockSpec((tm, w), lambda i: (i, 0))
    full = lambda a: pl.BlockSpec(a.shape, lambda i: (0, 0))
    return pl.pallas_call(
        body, name="gdn_proj_fwd", grid=(t // tm,),
        in_specs=[row(D_MODEL), full(nw), full(w_qkv), full(w_z), full(w_ba)],
        out_specs=[row(GDN_CONV_W), row(D_MODEL), row(LANES), row(D_MODEL)],
        out_shape=[jax.ShapeDtypeStruct((t, GDN_CONV_W), F32),
                   jax.ShapeDtypeStruct((t, D_MODEL), F32),
                   jax.ShapeDtypeStruct((t, LANES), F32),
                   jax.ShapeDtypeStruct((t, D_MODEL), BF16)],
        compiler_params=_params(1),
    )(h, nw, w_qkv, w_z, w_ba)


def _softplus(x):
    return jnp.maximum(x, 0.0) + jnp.log(1.0 + jnp.exp(-jnp.abs(x)))


def _conv_taps(x, halo, cw):
    tm = x.shape[0]
    xx = jnp.concatenate([halo, x], axis=0)
    taps = [xx[8 - (GDN_CONV - 1) + i: 8 - (GDN_CONV - 1) + i + tm] for i in range(GDN_CONV)]
    c = taps[0] * cw[0:1]
    for i in range(1, GDN_CONV):
        c = c + taps[i] * cw[i:i + 1]
    return c, taps


def _gates(ba, gp):
    lane = lax.broadcasted_iota(jnp.int32, ba.shape, 1)
    beta = _sigmoid(ba)
    pre = ba + gp[1:2]
    g = -jnp.exp(gp[0:1]) * _softplus(pre)
    gates = jnp.where(lane < GDN_HEADS, beta, jnp.where(lane < 2 * GDN_HEADS, g, 0.0))
    return lane, beta, pre, g, gates


def _gdn_conv_fwd(x, cw, ba, gp):
    t = x.shape[0]
    tm = _row_tile(t, GDN_CONV_TM)

    def body(x_ref, halo_ref, cw_ref, ba_ref, gp_ref, q_ref, k_ref, v_ref, gates_ref):
        halo = jnp.where(pl.program_id(0) > 0, halo_ref[...], 0.0)
        c, _ = _conv_taps(x_ref[...], halo, cw_ref[...])
        s = c * _sigmoid(c)
        for hh in range(GDN_HEADS):
            sq = s[:, hh * LANES:(hh + 1) * LANES]
            q_ref[:, hh * LANES:(hh + 1) * LANES] = (
                sq * lax.rsqrt(jnp.sum(sq * sq, axis=-1, keepdims=True) + NORM_EPS) * GDN_QSCALE)
            sk = s[:, D_MODEL + hh * LANES:D_MODEL + (hh + 1) * LANES]
            k_ref[:, hh * LANES:(hh + 1) * LANES] = (
                sk * lax.rsqrt(jnp.sum(sk * sk, axis=-1, keepdims=True) + NORM_EPS))
        v_ref[...] = s[:, 2 * D_MODEL:]
        gates_ref[...] = _gates(ba_ref[...], gp_ref[...])[4]

    row = lambda w: pl.BlockSpec((tm, w), lambda i: (i, 0))
    full = lambda a: pl.BlockSpec(a.shape, lambda i: (0, 0))
    halo = pl.BlockSpec((8, GDN_CONV_W), lambda i: (jnp.maximum(i * (tm // 8) - 1, 0), 0))
    return pl.pallas_call(
        body, name="gdn_conv_fwd", grid=(t // tm,),
        in_specs=[row(GDN_CONV_W), halo, full(cw), row(LANES), full(gp)],
        out_specs=[row(D_MODEL), row(D_MODEL), row(D_MODEL), row(LANES)],
        out_shape=[jax.ShapeDtypeStruct((t, D_MODEL), F32)] * 3 + [jax.ShapeDtypeStruct((t, LANES), F32)],
        compiler_params=_params(1),
    )(x, x, cw, ba, gp)


def _split3(x):
    hi = x.astype(BF16).astype(F32)
    r1 = x - hi
    mid = r1.astype(BF16).astype(F32)
    lo = (r1 - mid).astype(BF16).astype(F32)
    return hi, mid, lo


def _chunk_cumsum(col, keep):
    nb, c, _ = col.shape
    l3 = lax.broadcasted_iota(jnp.int32, (nb, c, LANES), 2)
    hi, mid, lo = _split3(col)
    pieces = jnp.where(l3 == 0, hi, jnp.where(l3 == 1, mid, jnp.where(l3 == 2, lo, 0.0))).astype(BF16)
    s = _bnn(jnp.where(keep, 1.0, 0.0).astype(BF16), pieces)
    return s[..., 0:1] + s[..., 1:2] + s[..., 2:3]


def _unit_inverse(nmat):
    nb, c, _ = nmat.shape
    ri = lax.broadcasted_iota(jnp.int32, (nb, c, c), 1)
    ci = lax.broadcasted_iota(jnp.int32, (nb, c, c), 2)
    eye = jnp.where(ri == ci, 1.0, 0.0).astype(F32)
    same = (ri >> GDN_SUB_SHIFT) == (ci >> GDN_SUB_SHIFT)
    nd = jnp.where(same, nmat, 0.0)
    no = nmat - nd
    mm = lambda a, b: _bnn(a.astype(BF16), b.astype(BF16))
    n2 = mm(nd, nd)
    n4 = mm(n2, n2)
    n8 = mm(n4, n4)
    td = mm(mm(mm(eye - nd, eye + n2), eye + n4), eye + n8)
    bm = mm(td, no)
    b2 = mm(bm, bm)
    return mm(mm(eye - bm, eye + b2), td)


def _chunk_terms(q_ref, k_ref, gates_ref, h, transposed):
    rows = k_ref.shape[0]
    nb = rows // GDN_CHUNK
    c = GDN_CHUNK
    lane = lax.broadcasted_iota(jnp.int32, (rows, LANES), 1)
    gt = gates_ref[...]
    beta = jnp.sum(jnp.where(lane == h, gt, 0.0), axis=-1, keepdims=True).reshape(nb, c, 1)
    g = jnp.sum(jnp.where(lane == GDN_HEADS + h, gt, 0.0), axis=-1, keepdims=True).reshape(nb, c, 1)
    ri = lax.broadcasted_iota(jnp.int32, (nb, c, c), 1)
    ci = lax.broadcasted_iota(jnp.int32, (nb, c, c), 2)
    gcol = _chunk_cumsum(g, ri >= ci)
    gamma = jnp.broadcast_to(gcol, (nb, c, LANES))
    l3 = lax.broadcasted_iota(jnp.int32, (nb, c, LANES), 2)
    gh, gm, gl = _split3(gcol)
    pmat = jnp.where(l3 == 0, gh, jnp.where(l3 == 1, gm, jnp.where(l3 == 2, gl, jnp.where(l3 < 6, 1.0, 0.0))))
    qmat = jnp.where(l3 < 3, 1.0, jnp.where(l3 == 3, -gh, jnp.where(l3 == 4, -gm, jnp.where(l3 == 5, -gl, 0.0))))
    pmat, qmat = pmat.astype(BF16), qmat.astype(BF16)
    k = k_ref[...].reshape(nb, c, LANES)
    q = q_ref[...].reshape(nb, c, LANES)
    kb = k * beta
    kbf, kbb, qb = k.astype(BF16), kb.astype(BF16), q.astype(BF16)
    out = dict(beta=beta, g=g, gamma=gamma, k=k, q=q, kb=kb, kbf=kbf, kbb=kbb, qb=qb, ri=ri, ci=ci)
    diff = _bnt(pmat, qmat)
    lmat = jnp.exp(jnp.where(ri >= ci, diff, NEG_BIG))
    out["L"] = lmat
    out["A"] = jnp.where(ri > ci, _bnt(kbb, kbf) * lmat, 0.0)
    out["Aqk"] = jnp.where(ri >= ci, _bnt(qb, kbf) * lmat, 0.0)
    if transposed:
        difft = _bnt(qmat, pmat)
        lt = jnp.exp(jnp.where(ci >= ri, difft, NEG_BIG))
        out["LT"] = lt
        out["AT"] = jnp.where(ci > ri, _bnt(kbf, kbb) * lt, 0.0)
        out["AqkT"] = jnp.where(ci >= ri, _bnt(kbf, qb) * lt, 0.0)
    return out


def _gdn_intra_fwd(q, k, v, gates):
    t = q.shape[0]
    rows = _row_tile(t, GDN_ROWS)
    nb = rows // GDN_CHUNK
    nchunks = t // GDN_CHUNK

    def body(q_ref, k_ref, v_ref, gates_ref, u_ref, w_ref, qd_ref, kd_ref, aqk_ref, cd_ref):
        h = pl.program_id(1)
        tm_ = _chunk_terms(q_ref, k_ref, gates_ref, h, False)
        gamma, beta = tm_["gamma"], tm_["beta"]
        eg = jnp.exp(gamma)
        tinv = _unit_inverse(tm_["A"])
        v3 = v_ref[...].reshape(nb, GDN_CHUNK, LANES)
        rhs = jnp.concatenate([v3 * beta, tm_["kb"] * eg], axis=-1)
        sol = _bnn(tinv.astype(BF16), rhs.astype(BF16))
        u_ref[...] = sol[..., :LANES].reshape(rows, LANES)
        w_ref[...] = sol[..., LANES:].reshape(rows, LANES).astype(BF16)
        gl = gamma[:, GDN_CHUNK - 1:GDN_CHUNK, :]
        qd_ref[...] = (tm_["q"] * eg).reshape(rows, LANES).astype(BF16)
        kd_ref[...] = (tm_["k"] * jnp.exp(gl - gamma)).reshape(rows, LANES).astype(BF16)
        aqk_ref[0] = tm_["Aqk"].reshape(rows, GDN_CHUNK).astype(BF16)
        cd_ref[0] = jnp.broadcast_to(jnp.exp(gl), (nb, 8, LANES)).reshape(nb * 8, LANES)

    blk = pl.BlockSpec((rows, LANES), lambda i, h: (i, h))
    return pl.pallas_call(
        body, name="gdn_intra_fwd", grid=(t // rows, GDN_HEADS),
        in_specs=[blk, blk, blk, pl.BlockSpec((rows, LANES), lambda i, h: (i, 0))],
        out_specs=[blk, blk, blk, blk,
                   pl.BlockSpec((1, rows, GDN_CHUNK), lambda i, h: (h, i, 0)),
                   pl.BlockSpec((1, nb * 8, LANES), lambda i, h: (h, i, 0))],
        out_shape=[jax.ShapeDtypeStruct((t, D_MODEL), F32),
                   jax.ShapeDtypeStruct((t, D_MODEL), BF16),
                   jax.ShapeDtypeStruct((t, D_MODEL), BF16),
                   jax.ShapeDtypeStruct((t, D_MODEL), BF16),
                   jax.ShapeDtypeStruct((GDN_HEADS, t, GDN_CHUNK), BF16),
                   jax.ShapeDtypeStruct((GDN_HEADS, nchunks * 8, LANES), F32)],
        compiler_params=_params(2),
    )(q, k, v, gates)


def _gdn_scan_fwd(u, w, qd, kd, aqk, cd):
    t = u.shape[0]
    rows = _row_tile(t, GDN_ROWS)
    nb = rows // GDN_CHUNK
    nchunks = t // GDN_CHUNK
    hs = GDN_SCAN_HEADS

    def body(u_ref, w_ref, qd_ref, kd_ref, aqk_ref, cd_ref, o_ref, st_ref, s_ref):
        @pl.when(pl.program_id(1) == 0)
        def _():
            s_ref[...] = jnp.zeros_like(s_ref)

        states = [s_ref[hh] for hh in range(hs)]
        for c in range(nb):
            r = slice(c * GDN_CHUNK, (c + 1) * GDN_CHUNK)
            for hh in range(hs):
                ln = slice(hh * LANES, (hh + 1) * LANES)
                s = states[hh]
                st_ref[hh, c * LANES:(c + 1) * LANES, :] = s
                sb = s.astype(BF16)
                vb = (u_ref[r, ln] - _nn(w_ref[r, ln], sb)).astype(BF16)
                o_ref[r, ln] = _nn(qd_ref[r, ln], sb) + _nn(aqk_ref[hh, r, :], vb)
                states[hh] = s * cd_ref[hh, c * 8:c * 8 + 1, :] + _tn(kd_ref[r, ln], vb)
        for hh in range(hs):
            s_ref[hh] = states[hh]

    blk = pl.BlockSpec((rows, hs * LANES), lambda h, i: (i, h))
    return pl.pallas_call(
        body, name="gdn_scan_fwd", grid=(GDN_HEADS // hs, t // rows),
        in_specs=[blk, blk, blk, blk,
                  pl.BlockSpec((hs, rows, GDN_CHUNK), lambda h, i: (h, i, 0)),
                  pl.BlockSpec((hs, nb * 8, LANES), lambda h, i: (h, i, 0))],
        out_specs=[blk, pl.BlockSpec((hs, nb * LANES, LANES), lambda h, i: (h, i, 0))],
        out_shape=[jax.ShapeDtypeStruct((t, D_MODEL), F32),
                   jax.ShapeDtypeStruct((GDN_HEADS, nchunks * LANES, LANES), F32)],
        scratch_shapes=[pltpu.VMEM((hs, LANES, LANES), F32)],
        compiler_params=_params(2),
    )(u, w, qd, kd, aqk, cd)


def _gdn_out_fwd(o, z, nw, w_out, res):
    t = o.shape[0]
    tm = _row_tile(t, GDN_TM)

    def body(o_ref, z_ref, nw_ref, w_ref, r_ref, out_ref, gated_ref):
        nwv = nw_ref[...]
        for hh in range(GDN_HEADS):
            sl = slice(hh * LANES, (hh + 1) * LANES)
            _, on = _rms_stats(o_ref[:, sl])
            zv = z_ref[:, sl]
            gated_ref[:, sl] = (on * nwv * (zv * _sigmoid(zv))).astype(BF16)
        out_ref[...] = r_ref[...] + _nn(gated_ref[...], w_ref[...])

    row = pl.BlockSpec((tm, D_MODEL), lambda i: (i, 0))
    full = lambda a: pl.BlockSpec(a.shape, lambda i: (0, 0))
    return pl.pallas_call(
        body, name="gdn_out_fwd", grid=(t // tm,),
        in_specs=[row, row, full(nw), full(w_out), row],
        out_specs=[row, row],
        out_shape=[jax.ShapeDtypeStruct((t, D_MODEL), F32), jax.ShapeDtypeStruct((t, D_MODEL), BF16)],
        compiler_params=_params(1),
    )(o, z, nw, w_out, res)


def _gdn_out_bwd(dy, w_out, o, z, nw):
    t = o.shape[0]
    tm = _row_tile(t, GDN_TM)

    def body(dy_ref, w_ref, o_ref, z_ref, nw_ref, do_ref, dz_ref, dyb_ref, dnw_ref, dgt_ref):
        @pl.when(pl.program_id(0) == 0)
        def _():
            dnw_ref[...] = jnp.zeros_like(dnw_ref)

        dyb = dy_ref[...].astype(BF16)
        dyb_ref[...] = dyb
        dgt_ref[...] = _nt(dyb, w_ref[...])
        nwv = nw_ref[...]
        for hh in range(GDN_HEADS):
            sl = slice(hh * LANES, (hh + 1) * LANES)
            r, on = _rms_stats(o_ref[:, sl])
            zv = z_ref[:, sl]
            sig = _sigmoid(zv)
            sz = zv * sig
            dgt = dgt_ref[:, sl]
            d_on = dgt * nwv * sz
            dz_ref[:, sl] = (dgt * on * nwv * (sig * (1.0 + zv * (1.0 - sig)))).astype(BF16)
            dnw_ref[...] += jnp.sum(dgt * on * sz, axis=0, keepdims=True)
            do_ref[:, sl] = (r * (d_on - on * jnp.mean(d_on * on, axis=-1, keepdims=True))).astype(BF16)

    row = pl.BlockSpec((tm, D_MODEL), lambda i: (i, 0))
    full = lambda a: pl.BlockSpec(a.shape, lambda i: (0, 0))
    return pl.pallas_call(
        body, name="gdn_out_bwd", grid=(t // tm,),
        in_specs=[row, full(w_out), row, row, full(nw)],
        out_specs=[row, row, row, pl.BlockSpec((1, LANES), lambda i: (0, 0))],
        out_shape=[jax.ShapeDtypeStruct((t, D_MODEL), BF16)] * 3 + [jax.ShapeDtypeStruct((1, LANES), F32)],
        scratch_shapes=[pltpu.VMEM((tm, D_MODEL), F32)],
        compiler_params=_params(1),
    )(dy, w_out, o, z, nw)


def _gdn_scan_bwd(u, w, qd, kd, aqk, cd, st, do):
    t = u.shape[0]
    rows = _row_tile(t, GDN_ROWS)
    nb = rows // GDN_CHUNK
    nchunks = t // GDN_CHUNK
    nsteps = t // rows
    cc = GDN_CHUNK
    hs = GDN_SCAN_HEADS

    def body(u_ref, w_ref, qd_ref, kd_ref, aqk_ref, cd_ref, st_ref, do_ref,
             du_ref, dw_ref, dqd_ref, dkd_ref, daqk_ref, daqkt_ref, dcd_ref, ds_ref):
        @pl.when(pl.program_id(1) == 0)
        def _():
            ds_ref[...] = jnp.zeros_like(ds_ref)

        ri = lax.broadcasted_iota(jnp.int32, (cc, cc), 0)
        ci = lax.broadcasted_iota(jnp.int32, (cc, cc), 1)
        dstates = [ds_ref[hh] for hh in range(hs)]
        for c in reversed(range(nb)):
            r = slice(c * cc, (c + 1) * cc)
            for hh in range(hs):
                ln = slice(hh * LANES, (hh + 1) * LANES)
                s = st_ref[hh, c * LANES:(c + 1) * LANES, :]
                sb = s.astype(BF16)
                dsn = dstates[hh]
                dsb = dsn.astype(BF16)
                wv, kdv, qdv, aq, dov = w_ref[r, ln], kd_ref[r, ln], qd_ref[r, ln], aqk_ref[hh, r, :], do_ref[r, ln]
                vb = (u_ref[r, ln] - _nn(wv, sb)).astype(BF16)
                dv = _tn(aq, dov) + _nn(kdv, dsb)
                dvb = dv.astype(BF16)
                daqk_ref[hh, r, :] = jnp.where(ri >= ci, _nt(dov, vb), 0.0)
                daqkt_ref[hh, r, :] = jnp.where(ci >= ri, _nt(vb, dov), 0.0)
                dqd_ref[r, ln] = _nt(dov, sb)
                dkd_ref[r, ln] = _nt(vb, dsb)
                dcd_ref[hh, c * 8:(c + 1) * 8, :] = jnp.broadcast_to(jnp.sum(s * dsn), (8, LANES))
                du_ref[r, ln] = dv
                dw_ref[r, ln] = -_nt(dvb, sb)
                dstates[hh] = _tn(qdv, dov) + dsn * cd_ref[hh, c * 8:c * 8 + 1, :] - _tn(wv, dvb)
        for hh in range(hs):
            ds_ref[hh] = dstates[hh]

    rev = lambda i: nsteps - 1 - i
    blk = pl.BlockSpec((rows, hs * LANES), lambda h, i: (rev(i), h))
    sq = pl.BlockSpec((hs, rows, cc), lambda h, i: (h, rev(i), 0))
    cdb = pl.BlockSpec((hs, nb * 8, LANES), lambda h, i: (h, rev(i), 0))
    return pl.pallas_call(
        body, name="gdn_scan_bwd", grid=(GDN_HEADS // hs, nsteps),
        in_specs=[blk, blk, blk, blk, sq, cdb,
                  pl.BlockSpec((hs, nb * LANES, LANES), lambda h, i: (h, rev(i), 0)), blk],
        out_specs=[blk, blk, blk, blk, sq, sq, cdb],
        out_shape=[jax.ShapeDtypeStruct((t, D_MODEL), F32)] * 4
        + [jax.ShapeDtypeStruct((GDN_HEADS, t, cc), F32)] * 2
        + [jax.ShapeDtypeStruct((GDN_HEADS, nchunks * 8, LANES), F32)],
        scratch_shapes=[pltpu.VMEM((hs, LANES, LANES), F32)],
        compiler_params=_params(2),
    )(u, w, qd, kd, aqk, cd, st, do)


def _gdn_intra_bwd(q, k, v, gates, u, w, du, dw, dqd, dkd, daqk, daqkt, dcd):
    t = q.shape[0]
    rows = _row_tile(t, GDN_ROWS)
    nb = rows // GDN_CHUNK
    cc = GDN_CHUNK

    def body(q_ref, k_ref, v_ref, gates_ref, u_ref, w_ref, du_ref, dw_ref, dqd_ref, dkd_ref,
             daqk_ref, daqkt_ref, dcd_ref, dq_ref, dk_ref, dv_ref, dgates_ref):
        h = pl.program_id(1)

        @pl.when(h == 0)
        def _():
            dgates_ref[...] = jnp.zeros_like(dgates_ref)

        tm_ = _chunk_terms(q_ref, k_ref, gates_ref, h, True)
        gamma, beta, kk, qq, kb = tm_["gamma"], tm_["beta"], tm_["k"], tm_["q"], tm_["kb"]
        kbf, kbb, qb = tm_["kbf"], tm_["kbb"], tm_["qb"]
        lmat, lt = tm_["L"], tm_["LT"]
        ri, ci = tm_["ri"], tm_["ci"]
        r3 = lambda ref: ref[...].reshape(nb, cc, LANES)
        eg = jnp.exp(gamma)
        gl = gamma[:, cc - 1:cc, :]
        ekd = jnp.exp(gl - gamma)
        v3 = r3(v_ref)
        tt = _unit_inverse(tm_["AT"])
        dsol = jnp.concatenate([r3(du_ref), r3(dw_ref)], axis=-1)
        sol = jnp.concatenate([r3(u_ref), r3(w_ref).astype(F32)], axis=-1)
        dx = _bnn(tt.astype(BF16), dsol.astype(BF16))
        dxb, solb = dx.astype(BF16), sol.astype(BF16)
        da = jnp.where(ri > ci, -_bnt(dxb, solb), 0.0)
        dat = jnp.where(ci > ri, -_bnt(solb, dxb), 0.0)
        dxu, dxw = dx[..., :LANES], dx[..., LANES:]
        dv_ref[...] = (dxu * beta).reshape(rows, LANES)
        dbeta = jnp.sum(dxu * v3, axis=-1, keepdims=True)
        dkb = dxw * eg
        dgam = jnp.sum(dxw * kb * eg, axis=-1, keepdims=True)
        dkb = dkb + _bnn((da * lmat).astype(BF16), kbf)
        dk = _bnn((dat * lt).astype(BF16), kbb)
        dgam = dgam + jnp.sum(da * tm_["A"], axis=-1, keepdims=True) - jnp.sum(dat * tm_["AT"], axis=-1, keepdims=True)
        daq = daqk_ref[0].reshape(nb, cc, cc)
        daqt = daqkt_ref[0].reshape(nb, cc, cc)
        dq = _bnn((daq * lmat).astype(BF16), kbf)
        dk = dk + _bnn((daqt * lt).astype(BF16), qb)
        dgam = dgam + jnp.sum(daq * tm_["Aqk"], axis=-1, keepdims=True) - jnp.sum(daqt * tm_["AqkT"], axis=-1, keepdims=True)
        dqd3, dkd3 = r3(dqd_ref), r3(dkd_ref)
        dq = dq + dqd3 * eg
        dgam = dgam + jnp.sum(dqd3 * qq * eg, axis=-1, keepdims=True)
        dk = dk + dkd3 * ekd
        tk = jnp.sum(dkd3 * kk * ekd, axis=-1, keepdims=True)
        dgam = dgam - tk
        dcdv = dcd_ref[0].reshape(nb, 8, LANES)[:, 0:1, 0:1]
        dglast = jnp.sum(tk, axis=1, keepdims=True) + dcdv * jnp.exp(gl[:, :, 0:1])
        rowi = lax.broadcasted_iota(jnp.int32, (nb, cc, 1), 1)
        dgam = dgam + jnp.where(rowi == cc - 1, dglast, 0.0)
        dk = dk + dkb * beta
        dbeta = dbeta + jnp.sum(dkb * kk, axis=-1, keepdims=True)
        dg = _chunk_cumsum(dgam, ci >= ri)
        dq_ref[...] = dq.reshape(rows, LANES)
        dk_ref[...] = dk.reshape(rows, LANES)
        lane = lax.broadcasted_iota(jnp.int32, (rows, LANES), 1)
        dgates_ref[...] += (jnp.where(lane == h, dbeta.reshape(rows, 1), 0.0)
                            + jnp.where(lane == GDN_HEADS + h, dg.reshape(rows, 1), 0.0))

    blk = pl.BlockSpec((rows, LANES), lambda i, h: (i, h))
    shared = pl.BlockSpec((rows, LANES), lambda i, h: (i, 0))
    sq = pl.BlockSpec((1, rows, cc), lambda i, h: (h, i, 0))
    return pl.pallas_call(
        body, name="gdn_intra_bwd", grid=(t // rows, GDN_HEADS),
        in_specs=[blk, blk, blk, shared, blk, blk, blk, blk, blk, blk, sq, sq,
                  pl.BlockSpec((1, nb * 8, LANES), lambda i, h: (h, i, 0))],
        out_specs=[blk, blk, blk, shared],
        out_shape=[jax.ShapeDtypeStruct((t, D_MODEL), F32)] * 3 + [jax.ShapeDtypeStruct((t, LANES), F32)],
        compiler_params=_params(2),
    )(q, k, v, gates, u, w, du, dw, dqd, dkd, daqk, daqkt, dcd)


def _gdn_conv_bwd_a(x, cw, ba, gp, dq, dk, dv, dgates):
    t = x.shape[0]
    tm = _row_tile(t, GDN_CONV_TM)

    def body(x_ref, halo_ref, cw_ref, ba_ref, gp_ref, dq_ref, dk_ref, dv_ref, dgates_ref,
             dc_ref, dba_ref, dcw_ref, dgp_ref, ds_ref):
        @pl.when(pl.program_id(0) == 0)
        def _():
            dcw_ref[...] = jnp.zeros_like(dcw_ref)
            dgp_ref[...] = jnp.zeros_like(dgp_ref)

        halo = jnp.where(pl.program_id(0) > 0, halo_ref[...], 0.0)
        c, taps = _conv_taps(x_ref[...], halo, cw_ref[...])
        sig = _sigmoid(c)
        s = c * sig
        for hh in range(GDN_HEADS):
            sl = slice(hh * LANES, (hh + 1) * LANES)
            sq = s[:, sl]
            rq = lax.rsqrt(jnp.sum(sq * sq, axis=-1, keepdims=True) + NORM_EPS)
            qh = sq * rq
            dqv = dq_ref[:, sl]
            ds_ref[:, sl] = GDN_QSCALE * rq * (dqv - qh * jnp.sum(dqv * qh, axis=-1, keepdims=True))
            sl2 = slice(D_MODEL + hh * LANES, D_MODEL + (hh + 1) * LANES)
            sk = s[:, sl2]
            rk = lax.rsqrt(jnp.sum(sk * sk, axis=-1, keepdims=True) + NORM_EPS)
            kh = sk * rk
            dkv = dk_ref[:, sl]
            ds_ref[:, sl2] = rk * (dkv - kh * jnp.sum(dkv * kh, axis=-1, keepdims=True))
        ds_ref[:, 2 * D_MODEL:] = dv_ref[...]
        dc = ds_ref[...] * (sig * (1.0 + c * (1.0 - sig)))
        dc_ref[...] = dc
        for i in range(GDN_CONV):
            dcw_ref[i:i + 1, :] += jnp.sum(dc * taps[i], axis=0, keepdims=True)
        lane, beta, pre, g, _ = _gates(ba_ref[...], gp_ref[...])
        dgt = dgates_ref[...]
        db = dgt * beta * (1.0 - beta)
        dpre = dgt * (-jnp.exp(gp_ref[0:1, :])) * _sigmoid(pre)
        isa = jnp.logical_and(lane >= GDN_HEADS, lane < 2 * GDN_HEADS)
        dba_ref[...] = jnp.where(lane < GDN_HEADS, db, jnp.where(isa, dpre, 0.0)).astype(BF16)
        dgp_ref[0:1, :] += jnp.sum(jnp.where(isa, dgt * g, 0.0), axis=0, keepdims=True)
        dgp_ref[1:2, :] += jnp.sum(jnp.where(isa, dpre, 0.0), axis=0, keepdims=True)

    row = lambda w: pl.BlockSpec((tm, w), lambda i: (i, 0))
    full = lambda a: pl.BlockSpec(a.shape, lambda i: (0, 0))
    halo = pl.BlockSpec((8, GDN_CONV_W), lambda i: (jnp.maximum(i * (tm // 8) - 1, 0), 0))
    return pl.pallas_call(
        body, name="gdn_conv_bwd_a", grid=(t // tm,),
        in_specs=[row(GDN_CONV_W), halo, full(cw), row(LANES), full(gp), row(D_MODEL), row(D_MODEL),
                  row(D_MODEL), row(LANES)],
        out_specs=[row(GDN_CONV_W), row(LANES), pl.BlockSpec((8, GDN_CONV_W), lambda i: (0, 0)),
                   pl.BlockSpec((8, LANES), lambda i: (0, 0))],
        out_shape=[jax.ShapeDtypeStruct((t, GDN_CONV_W), F32),
                   jax.ShapeDtypeStruct((t, LANES), BF16),
                   jax.ShapeDtypeStruct((8, GDN_CONV_W), F32),
                   jax.ShapeDtypeStruct((8, LANES), F32)],
        scratch_shapes=[pltpu.VMEM((tm, GDN_CONV_W), F32)],
        compiler_params=_params(1),
    )(x, x, cw, ba, gp, dq, dk, dv, dgates)


def _gdn_conv_bwd_b(dc, cw):
    t = dc.shape[0]
    tm = _row_tile(t, GDN_CONV_TM)
    nsteps = t // tm

    def body(dc_ref, halo_ref, cw_ref, dx_ref):
        halo = jnp.where(pl.program_id(0) < nsteps - 1, halo_ref[...], 0.0)
        dd = jnp.concatenate([dc_ref[...], halo], axis=0)
        cw_v = cw_ref[...]
        acc = dd[GDN_CONV - 1:GDN_CONV - 1 + tm] * cw_v[0:1]
        for i in range(1, GDN_CONV):
            acc = acc + dd[GDN_CONV - 1 - i:GDN_CONV - 1 - i + tm] * cw_v[i:i + 1]
        dx_ref[...] = acc.astype(BF16)

    row = pl.BlockSpec((tm, GDN_CONV_W), lambda i: (i, 0))
    halo = pl.BlockSpec((8, GDN_CONV_W), lambda i: (jnp.minimum((i + 1) * (tm // 8), t // 8 - 1), 0))
    return pl.pallas_call(
        body, name="gdn_conv_bwd_b", grid=(nsteps,),
        in_specs=[row, halo, pl.BlockSpec(cw.shape, lambda i: (0, 0))],
        out_specs=row,
        out_shape=jax.ShapeDtypeStruct((t, GDN_CONV_W), BF16),
        compiler_params=_params(1),
    )(dc, dc, cw)


def _gdn_proj_bwd(dy, h, nw, w_qkv, w_z, w_ba, dx, dz, dba):
    t = h.shape[0]
    tm = _row_tile(t, GDN_TM)

    def body(dy_ref, h_ref, nw_ref, wq_ref, wz_ref, wb_ref, dx_ref, dz_ref, dba_ref, dh_ref, dnw_ref):
        @pl.when(pl.program_id(0) == 0)
        def _():
            dnw_ref[...] = jnp.zeros_like(dnw_ref)

        dhn = _nt(dx_ref[...], wq_ref[...]) + _nt(dz_ref[...], wz_ref[...]) + _nt(dba_ref[...], wb_ref[...])
        dxx, dnw = _norm_bwd(h_ref[...], nw_ref[...], dhn)
        dh_ref[...] = dy_ref[...] + dxx
        dnw_ref[...] += dnw

    row = lambda w: pl.BlockSpec((tm, w), lambda i: (i, 0))
    full = lambda a: pl.BlockSpec(a.shape, lambda i: (0, 0))
    return pl.pallas_call(
        body, name="gdn_proj_bwd", grid=(t // tm,),
        in_specs=[row(D_MODEL), row(D_MODEL), full(nw), full(w_qkv), full(w_z), full(w_ba),
                  row(GDN_CONV_W), row(D_MODEL), row(LANES)],
        out_specs=[row(D_MODEL), pl.BlockSpec((1, D_MODEL), lambda i: (0, 0))],
        out_shape=[jax.ShapeDtypeStruct((t, D_MODEL), F32), jax.ShapeDtypeStruct((1, D_MODEL), F32)],
        compiler_params=_params(1),
    )(dy, h, nw, w_qkv, w_z, w_ba, dx, dz, dba)


def _loss_head(h, nw, target):
    t = h.shape[0]
    tm = _row_tile(t, 512)

    def body(h_ref, nw_ref, t_ref, loss_ref, dh_ref, dnw_ref):
        @pl.when(pl.program_id(0) == 0)
        def _():
            loss_ref[...] = jnp.zeros_like(loss_ref)
            dnw_ref[...] = jnp.zeros_like(dnw_ref)

        x = h_ref[...]
        nwv = nw_ref[...]
        _, xhat = _rms_stats(x)
        err = xhat * nwv - t_ref[...]
        loss_ref[...] += 0.5 * jnp.sum(jnp.mean(err * err, axis=-1, keepdims=True))
        dx, dnw = _norm_bwd(x, nwv, err * (1.0 / D_MODEL))
        dh_ref[...] = dx
        dnw_ref[...] += dnw

    row = pl.BlockSpec((tm, D_MODEL), lambda i: (i, 0))
    return pl.pallas_call(
        body, name="loss_head", grid=(t // tm,),
        in_specs=[row, pl.BlockSpec((1, D_MODEL), lambda i: (0, 0)), row],
        out_specs=[pl.BlockSpec((8, LANES), lambda i: (0, 0)), row, pl.BlockSpec((1, D_MODEL), lambda i: (0, 0))],
        out_shape=[jax.ShapeDtypeStruct((8, LANES), F32),
                   jax.ShapeDtypeStruct((t, D_MODEL), F32),
                   jax.ShapeDtypeStruct((1, D_MODEL), F32)],
        compiler_params=_params(1),
    )(h, nw, target)


_PEER_FLIPS = [(dx, dy, dc) for dx in (0, 1) for dy in (0, 1) for dc in (0, 1)][1:]


_ANY = pl.BlockSpec(memory_space=pl.ANY)


def _exchange_copies(ins, outs, send_sems, recv_sems, local_sems, scatter):
    x, y, c = lax.axis_index("x"), lax.axis_index("y"), lax.axis_index("c")
    me = 4 * x + 2 * y + c
    copies = []
    for a in range(len(ins)):
        src = ins[a].at[me] if scatter else ins[a]
        copies.append(pltpu.make_async_copy(src, outs[a].at[me], local_sems.at[a]))
    for k, (fx, fy, fc) in enumerate(_PEER_FLIPS):
        px, py, pc = lax.rem(x + fx, 2), lax.rem(y + fy, 2), lax.rem(c + fc, 2)
        peer = 4 * px + 2 * py + pc
        for a in range(len(ins)):
            copies.append(pltpu.make_async_remote_copy(
                src_ref=ins[a].at[peer] if scatter else ins[a],
                dst_ref=outs[a].at[me],
                send_sem=send_sems.at[a, k], recv_sem=recv_sems.at[a, k],
                device_id=(px, py, pc), device_id_type=MESH_ID))
    return copies


def _exchange_shapes(arrs, scatter):
    return [jax.ShapeDtypeStruct((N_DEV,) + (a.shape[1:] if scatter else a.shape), a.dtype) for a in arrs]


def _exchange_sems(n):
    npeer = len(_PEER_FLIPS)
    return [pltpu.SemaphoreType.DMA((n, npeer)), pltpu.SemaphoreType.DMA((n, npeer)),
            pltpu.SemaphoreType.DMA((n,))]


def _exchange(arrs, scatter, name):
    n = len(arrs)

    def body(*refs):
        copies = _exchange_copies(refs[:n], refs[n:2 * n], *refs[2 * n:], scatter)
        for cp in copies:
            cp.start()
        for cp in copies:
            cp.wait()

    return pl.pallas_call(
        body, name=name, in_specs=[_ANY] * n, out_specs=[_ANY] * n,
        out_shape=_exchange_shapes(arrs, scatter), scratch_shapes=_exchange_sems(n),
    )(*arrs)


def _adamw(parts, w, m, v, name):
    r, c = w.shape
    tr = r
    for cand in (512, 256, 128, 64, 32, 16, 8):
        if r % cand == 0:
            tr = cand
            break
    c1 = 1.0 - ADAM_B1 ** ADAM_STEP
    c2 = 1.0 - ADAM_B2 ** ADAM_STEP

    def body(p_ref, w_ref, m_ref, v_ref, g_ref, d_ref, nm_ref, nv_ref):
        g = p_ref[0].astype(F32)
        for s in range(1, N_DEV):
            g = g + p_ref[s].astype(F32)
        nm = ADAM_B1 * m_ref[...] + (1.0 - ADAM_B1) * g
        nv = ADAM_B2 * v_ref[...] + (1.0 - ADAM_B2) * (g * g)
        g_ref[...] = g
        nm_ref[...] = nm
        nv_ref[...] = nv
        d_ref[...] = -ADAM_LR * ((nm / c1) / (jnp.sqrt(nv / c2) + ADAM_EPS) + ADAM_WD * w_ref[...])

    blk = pl.BlockSpec((tr, c), lambda i: (i, 0))
    return pl.pallas_call(
        body, name=name, grid=(r // tr,),
        in_specs=[pl.BlockSpec((N_DEV, tr, c), lambda i: (0, i, 0)), blk, blk, blk],
        out_specs=[blk] * 4,
        out_shape=[jax.ShapeDtypeStruct((r, c), F32)] * 4,
        compiler_params=_params(1),
    )(parts, w, m, v)


def _rope_tables(positions):
    half = ROPE_DIM // 2
    inv_freq = ROPE_THETA ** (-jnp.arange(0, ROPE_DIM, 2, dtype=F32) / ROPE_DIM)
    ang = positions.astype(F32)[:, None] * inv_freq
    cos, sin = jnp.cos(ang), jnp.sin(ang)
    t = positions.shape[0]
    zeros = lambda w: jnp.zeros((t, w), F32)
    c = jnp.concatenate([cos, cos, jnp.ones((t, ATTN_HEAD_DIM - ROPE_DIM), F32)], axis=1)
    s1 = jnp.concatenate([-sin, zeros(ATTN_HEAD_DIM - half)], axis=1)
    s2 = jnp.concatenate([zeros(half), sin, zeros(ATTN_HEAD_DIM - ROPE_DIM)], axis=1)
    return tuple(jnp.tile(a, (1, LANES // ATTN_HEAD_DIM)) for a in (c, s1, s2))


def _ffn_layer_fwd(h, nw, w, carry):
    (out, g, u, hn), landed = _ffn_fwd(h, nw, w["wg"], w["wu"], w["wd"], carry, False)
    return out, (h, g, u, hn), landed


def _ffn_layer_bwd(dy, saved, nw, w, carry):
    h, g, u, hn = saved
    (dh, dg, du, act, dz, dnw), landed = _ffn_bwd(dy, h, nw, g, u, w["wg"], w["wu"], w["wd"], carry, True)
    dwg = _matmul_tn(hn, dg, FFN_TF, "ffn_dwg")
    dwu = _matmul_tn(hn, du, FFN_TF, "ffn_dwu")
    dwd = _matmul_tn(act, dz, D_MODEL, "ffn_dwd")
    return dh, dnw, (dwg, dwu, dwd), landed


def _attn_layer_fwd(h, nw, w, ropes):
    q, k, v, hn = _attn_qkv_fwd(h, nw, w["w_in"], w["b_in"], *ropes)
    o = _attn_core_fwd(q, k, v, w["sinks"])
    out = _proj_out_fwd(o, w["w_out"], w["b_out"], h)
    return out, (h, hn, q, k, v, o)


def _attn_layer_bwd(dy, saved, nw, w, ropes):
    h, hn, q, k, v, o = saved
    do, dyb, db_out = _proj_out_bwd(dy, w["w_out"])
    dw_out = _matmul_tn(o, dyb, D_MODEL, "attn_dw_out")
    dq, dk, dv, dsink = _attn_core_bwd(q, k, v, do, w["sinks"])
    dh, dqkv, db_in, dnw = _attn_qkv_bwd(dy, h, nw, w["w_in"], dq, dk, dv, *ropes)
    dw_in = _matmul_tn(hn, dqkv, ATTN_IN, "attn_dw_in")
    return dh, dnw, dict(w_in=dw_in, b_in=db_in, sinks=dsink[:, 0], w_out=dw_out, b_out=db_out)


def _gdn_layer_fwd(h, nw, w):
    x, z, ba, hn = _gdn_proj_fwd(h, nw, w["w_qkv"], w["w_z"], w["w_ba"])
    q, k, v, gates = _gdn_conv_fwd(x, w["conv_w"], ba, w["gp"])
    u, ww, qd, kd, aqk, cd = _gdn_intra_fwd(q, k, v, gates)
    o, st = _gdn_scan_fwd(u, ww, qd, kd, aqk, cd)
    out, gated = _gdn_out_fwd(o, z, w["norm_w"], w["w_out"], h)
    return out, (h, hn, x, z, ba, q, k, v, gates, u, ww, qd, kd, aqk, cd, st, o, gated)


def _gdn_layer_bwd(dy, saved, nw, w):
    h, hn, x, z, ba, q, k, v, gates, u, ww, qd, kd, aqk, cd, st, o, gated = saved
    do, dz, dyb, dnorm_w = _gdn_out_bwd(dy, w["w_out"], o, z, w["norm_w"])
    dw_out = _matmul_tn(gated, dyb, D_MODEL, "gdn_dw_out")
    du, dw, dqd, dkd, daqk, daqkt, dcd = _gdn_scan_bwd(u, ww, qd, kd, aqk, cd, st, do)
    dq, dk, dv, dgates = _gdn_intra_bwd(q, k, v, gates, u, ww, du, dw, dqd, dkd, daqk, daqkt, dcd)
    dc, dba, dcw, dgp = _gdn_conv_bwd_a(x, w["conv_w"], ba, w["gp"], dq, dk, dv, dgates)
    dx = _gdn_conv_bwd_b(dc, w["conv_w"])
    dh, dnw = _gdn_proj_bwd(dy, h, nw, w["w_qkv"], w["w_z"], w["w_ba"], dx, dz, dba)
    dw_qkv = _matmul_tn(hn, dx, GDN_CONV_W // 2, "gdn_dw_qkv")
    dw_z = _matmul_tn(hn, dz, D_MODEL, "gdn_dw_z")
    dw_ba = _matmul_tn(hn, dba, LANES, "gdn_dw_ba")
    dw_in = jnp.concatenate([dw_qkv, dw_z, dw_ba[:, :2 * GDN_HEADS]], axis=1)
    lo, hi = GDN_HEADS, 2 * GDN_HEADS
    return dh, dnw, dict(w_in=dw_in, conv_w=dcw[:GDN_CONV], A_log=dgp[0, lo:hi], dt_bias=dgp[1, lo:hi],
                         norm_w=dnorm_w[0], w_out=dw_out)


def _local_step(x, positions, target, norms, final_norm, plan):
    ropes = _rope_tables(positions)
    h = x
    saved = []
    for layer in range(DEPTH):
        h, s1, landed = _ffn_layer_fwd(h, norms["ffn1"][layer], plan.weights("ffn1", layer),
                                       plan.fwd_carry("ffn1", layer))
        plan.fwd_landed(landed)
        if layer % 2 == 0:
            h, s2 = _attn_layer_fwd(h, norms["mix"][layer], plan.weights("mix", layer), ropes)
        else:
            h, s2 = _gdn_layer_fwd(h, norms["mix"][layer], plan.weights("mix", layer))
        h, s3, landed = _ffn_layer_fwd(h, norms["ffn2"][layer], plan.weights("ffn2", layer),
                                       plan.fwd_carry("ffn2", layer))
        plan.fwd_landed(landed)
        saved.append((s1, s2, s3))
    loss, dh, dfinal = _loss_head(h, final_norm, target)

    g_norm = {k: [None] * DEPTH for k in ("ffn1", "mix", "ffn2")}
    for layer in reversed(range(DEPTH)):
        s1, s2, s3 = saved[layer]
        dh, g_norm["ffn2"][layer], gw, landed = _ffn_layer_bwd(
            dh, s3, norms["ffn2"][layer], plan.weights("ffn2", layer), plan.bwd_carry("ffn2", layer))
        plan.bwd_landed(landed)
        plan.grads("ffn2", layer, gw)
        if layer % 2 == 0:
            dh, g_norm["mix"][layer], gw = _attn_layer_bwd(dh, s2, norms["mix"][layer], plan.weights("mix", layer),
                                                           ropes)
        else:
            dh, g_norm["mix"][layer], gw = _gdn_layer_bwd(dh, s2, norms["mix"][layer], plan.weights("mix", layer))
        plan.grads("mix", layer, gw)
        dh, g_norm["ffn1"][layer], gw, landed = _ffn_layer_bwd(
            dh, s1, norms["ffn1"][layer], plan.weights("ffn1", layer), plan.bwd_carry("ffn1", layer))
        plan.bwd_landed(landed)
        plan.grads("ffn1", layer, gw)
    return loss, dh, dfinal, g_norm


def _cols_full(g):
    return jnp.transpose(g, (1, 0, 2)).reshape(g.shape[1], -1)


def _rows_full(g):
    return g.reshape(-1, g.shape[-1])


class _ShardedWeights:
    def __init__(self, shards, small):
        self.shards, self.small = shards, small
        self.whole, self.pending, self.received, self.small_grads = {}, {}, {}, {}
        self.in_flight = []
        first = [("ffn1", 0), ("mix", 0)]
        self.in_flight = first
        self.fwd_landed(_exchange(self._shards_of(first), False, "gather_first"))

    def _group(self, key):
        kind, layer = key
        j = layer // 2
        if kind != "mix":
            return [self.shards[kind + "_w_gate_up"][layer], self.shards[kind + "_w_down"][layer]]
        if layer % 2 == 0:
            return [self.shards["attn_w_in"][j], self.shards["attn_w_out"][j]]
        return [self.shards["gdn_w_in"][j], self.shards["gdn_w_out"][j], self.shards["gdn_conv_w"][j]]

    def _shards_of(self, keys):
        return [a for key in keys for a in self._group(key)]

    def weights(self, kind, layer):
        return self.whole[(kind, layer)]

    def fwd_carry(self, kind, layer):
        if kind == "ffn1":
            keys = [("ffn2", layer)]
        else:
            keys = [("ffn1", layer + 1), ("mix", layer + 1)] if layer + 1 < DEPTH else []
        self.in_flight = keys
        return self._shards_of(keys)

    def fwd_landed(self, landed):
        landed = list(landed)
        for key in self.in_flight:
            kind, layer = key
            j = layer // 2
            sm = self.small
            if kind != "mix":
                gu, dn = landed[:2]
                w = dict(wg=_cols_full(gu[:4]), wu=_cols_full(gu[4:]), wd=_rows_full(dn))
                landed = landed[2:]
            elif layer % 2 == 0:
                w = dict(w_in=_cols_full(landed[0]), w_out=_rows_full(landed[1]), b_in=sm["attn_b_in"][j][None],
                         sinks=sm["attn_sinks"][j][None], b_out=sm["attn_b_out"][j][None])
                landed = landed[2:]
            else:
                w_in = _cols_full(landed[0])
                w_ba = jnp.pad(w_in[:, GDN_CONV_W + D_MODEL:], ((0, 0), (0, LANES - 2 * GDN_HEADS)))
                gp = jnp.pad(jnp.stack([sm["gdn_A_log"][j], sm["gdn_dt_bias"][j]]),
                             ((0, 6), (GDN_HEADS, LANES - 2 * GDN_HEADS)))
                w = dict(w_qkv=w_in[:, :GDN_CONV_W], w_z=w_in[:, GDN_CONV_W:GDN_CONV_W + D_MODEL], w_ba=w_ba,
                         conv_w=_cols_full(landed[2]), gp=gp, norm_w=sm["gdn_norm_w"][j][None],
                         w_out=_rows_full(landed[1]))
                landed = landed[3:]
            self.whole[key] = w
        self.in_flight = []

    def grads(self, kind, layer, g):
        if kind != "mix":
            dwg, dwu, dwd = g
            parts = [jnp.concatenate([_cols_shards(dwg, 4), _cols_shards(dwu, 4)], axis=0).astype(BF16),
                     _rows_shards(dwd).astype(BF16)]
        else:
            parts = [_cols_shards(g["w_in"]).astype(BF16), _rows_shards(g["w_out"]).astype(BF16)]
            if layer % 2 == 1:
                parts.append(_cols_shards(g["conv_w"]))
            self.small_grads[layer] = g
        self.pending[(kind, layer)] = parts

    def bwd_carry(self, kind, layer):
        if kind == "ffn1":
            keys = [("ffn2", layer), ("mix", layer)]
        else:
            keys = [("ffn1", layer + 1)] if layer + 1 < DEPTH else []
        self.in_flight = keys
        return [a for key in keys for a in self.pending[key]]

    def bwd_landed(self, landed):
        landed = list(landed)
        for key in self.in_flight:
            n = len(self.pending.pop(key))
            self.received[key], landed = landed[:n], landed[n:]
        self.in_flight = []

    def finish(self):
        self.in_flight = list(self.pending)
        self.bwd_landed(_exchange([a for key in self.in_flight for a in self.pending[key]], True, "scatter_last"))


def _cols_shards(full, n=N_DEV):
    r = full.shape[0]
    return jnp.transpose(full.reshape(r, n, -1), (1, 0, 2))


def _rows_shards(full):
    return full.reshape(N_DEV, -1, full.shape[-1])


_SMALL = ("ffn1_norm", "mix_norm", "ffn2_norm", "attn_b_in", "attn_sinks", "attn_b_out",
          "gdn_A_log", "gdn_dt_bias", "gdn_norm_w", "final_norm")
_BIG = ("ffn1_w_gate_up", "ffn1_w_down", "ffn2_w_gate_up", "ffn2_w_down", "attn_w_in", "attn_w_out",
        "gdn_w_in", "gdn_conv_w", "gdn_w_out")
_WEIGHTS = ("ffn1_norm", "ffn1_w_gate_up", "ffn1_w_down", "mix_norm", "ffn2_norm", "ffn2_w_gate_up",
            "ffn2_w_down", "attn_w_in", "attn_b_in", "attn_sinks", "attn_w_out", "attn_b_out", "gdn_w_in",
            "gdn_conv_w", "gdn_A_log", "gdn_dt_bias", "gdn_norm_w", "gdn_w_out", "final_norm")


def _pack_small(vals):
    flat = jnp.concatenate([v.reshape(-1).astype(F32) for v in vals])
    pad = (-flat.shape[0]) % (8 * LANES)
    return jnp.pad(flat, (0, pad)).reshape(-1, LANES)


def _unpack_small(packed, shapes):
    flat = packed.reshape(-1)
    out, off = [], 0
    for s in shapes:
        size = 1
        for d in s:
            size *= d
        out.append(flat[off:off + size].reshape(s))
        off += size
    return out


def kernel(x, positions, ffn1_norm, ffn1_w_gate_up, ffn1_w_down, mix_norm, ffn2_norm, ffn2_w_gate_up, ffn2_w_down, attn_w_in, attn_b_in, attn_sinks, attn_w_out, attn_b_out, gdn_w_in, gdn_conv_w, gdn_A_log, gdn_dt_bias, gdn_norm_w, gdn_w_out, final_norm, loss_target, m_ffn1_norm, m_ffn1_w_gate_up, m_ffn1_w_down, m_mix_norm, m_ffn2_norm, m_ffn2_w_gate_up, m_ffn2_w_down, m_attn_w_in, m_attn_b_in, m_attn_sinks, m_attn_w_out, m_attn_b_out, m_gdn_w_in, m_gdn_conv_w, m_gdn_A_log, m_gdn_dt_bias, m_gdn_norm_w, m_gdn_w_out, m_final_norm, v_ffn1_norm, v_ffn1_w_gate_up, v_ffn1_w_down, v_mix_norm, v_ffn2_norm, v_ffn2_w_gate_up, v_ffn2_w_down, v_attn_w_in, v_attn_b_in, v_attn_sinks, v_attn_w_out, v_attn_b_out, v_gdn_w_in, v_gdn_conv_w, v_gdn_A_log, v_gdn_dt_bias, v_gdn_norm_w, v_gdn_w_out, v_final_norm):
    args = dict(locals())
    wts = {n: args[n] for n in _WEIGHTS}
    moms = {n: args["m_" + n] for n in _WEIGHTS}
    vels = {n: args["v_" + n] for n in _WEIGHTS}

    shards = {n: wts[n] if n == "gdn_conv_w" else wts[n].astype(BF16) for n in _BIG}
    plan = _ShardedWeights(shards, wts)
    norms = dict(ffn1=ffn1_norm[:, None, :], mix=mix_norm[:, None, :], ffn2=ffn2_norm[:, None, :])
    loss, grad_x, dfinal, g_norm = _local_step(x[0], positions[0], loss_target[0], norms, final_norm[None], plan)
    plan.finish()

    layers = range(DEPTH)
    got = lambda kind, ls, i: jnp.stack([plan.received[(kind, l)][i] for l in ls], axis=1)
    received = {"attn_w_in": got("mix", layers[0::2], 0), "attn_w_out": got("mix", layers[0::2], 1),
                "gdn_w_in": got("mix", layers[1::2], 0), "gdn_w_out": got("mix", layers[1::2], 1),
                "gdn_conv_w": got("mix", layers[1::2], 2)}
    for kind in ("ffn1", "ffn2"):
        received[kind + "_w_gate_up"] = got(kind, layers, 0)
        received[kind + "_w_down"] = got(kind, layers, 1)
    g_attn = [plan.small_grads[l] for l in layers[0::2]]
    g_gdn = [plan.small_grads[l] for l in layers[1::2]]


    small_g = dict(
        ffn1_norm=jnp.concatenate(g_norm["ffn1"], axis=0), mix_norm=jnp.concatenate(g_norm["mix"], axis=0),
        ffn2_norm=jnp.concatenate(g_norm["ffn2"], axis=0),
        attn_b_in=jnp.concatenate([g["b_in"] for g in g_attn], axis=0),
        attn_sinks=jnp.stack([g["sinks"] for g in g_attn]),
        attn_b_out=jnp.concatenate([g["b_out"] for g in g_attn], axis=0),
        gdn_A_log=jnp.stack([g["A_log"] for g in g_gdn]), gdn_dt_bias=jnp.stack([g["dt_bias"] for g in g_gdn]),
        gdn_norm_w=jnp.stack([g["norm_w"] for g in g_gdn]), final_norm=dfinal[0])
    small_parts = _exchange([_pack_small([small_g[n] for n in _SMALL] + [loss[0, :1]])], False, "gather_small")[0]
    pad1 = jnp.zeros((1,), F32)
    sw = _pack_small([wts[n] for n in _SMALL] + [pad1])
    sm = _pack_small([moms[n] for n in _SMALL] + [pad1])
    sv = _pack_small([vels[n] for n in _SMALL] + [pad1])
    shapes = [wts[n].shape for n in _SMALL] + [(1,)]
    small_out = [_unpack_small(o, shapes) for o in _adamw(small_parts, sw, sm, sv, "adamw_small")]
    results = {n: tuple(o[i] for o in small_out) for i, n in enumerate(_SMALL)}
    loss_total = small_out[0][-1][0]

    for n in _BIG:
        shape = wts[n].shape
        two_d = lambda a: a.reshape(-1, shape[-1])
        p = received[n].reshape(N_DEV, -1, shape[-1])
        outs = _adamw(p, two_d(wts[n]), two_d(moms[n]), two_d(vels[n]), "adamw_" + n)
        results[n] = tuple(o.reshape(shape) for o in outs)

    return (loss_total, grad_x[None],
            *[results[n][0] for n in _WEIGHTS], *[results[n][1] for n in _WEIGHTS],
            *[results[n][2] for n in _WEIGHTS], *[results[n][3] for n in _WEIGHTS])
```

```python
import jax
import jax.numpy as jnp
from jax import lax
from jax.experimental import pallas as pl
from jax.experimental.pallas import tpu as pltpu

F32 = jnp.float32
BF16 = jnp.bfloat16

D_MODEL = 1024
DEPTH = 4
D_FF = 2816
NORM_EPS = 1e-6
N_DEV = 8

ATTN_Q_HEADS = 16
ATTN_KV_HEADS = 4
ATTN_HEAD_DIM = 64
ATTN_GROUP = 4
ATTN_BLOCK = 128
ROPE_DIM = 16
ROPE_THETA = 500000.0
ATTN_Q_W = 1024
ATTN_KV_W = 256
ATTN_IN = 1536
ATTN_SCALE = ATTN_HEAD_DIM ** -0.5

GDN_HEADS = 8
GDN_DK = 128
GDN_CONV = 4
GDN_CHUNK = 64
GDN_CONV_W = 3072
GDN_IN = 4112
GDN_QSCALE = GDN_DK ** -0.5
GDN_ROWS = 512
GDN_SCAN_HEADS = 2
GDN_SUB_SHIFT = 4

ADAM_LR = 0.001
ADAM_B1 = 0.9
ADAM_B2 = 0.999
ADAM_EPS = 1e-08
ADAM_WD = 0.01
ADAM_STEP = 10

LANES = 128
NEG_BIG = -1e30
VMEM_LIMIT_BYTES = 56 * 1024 * 1024
MESH_ID = pl.DeviceIdType.MESH


def _params(n_axes, vmem_limit_bytes=VMEM_LIMIT_BYTES):
    return pltpu.CompilerParams(dimension_semantics=("arbitrary",) * n_axes,
                                vmem_limit_bytes=vmem_limit_bytes)


def _nn(a, b):
    return jnp.dot(a, b, preferred_element_type=F32)


def _nt(a, b):
    return lax.dot_general(a, b, (((1,), (1,)), ((), ())), preferred_element_type=F32)


def _tn(a, b):
    return lax.dot_general(a, b, (((0,), (0,)), ((), ())), preferred_element_type=F32)


def _bnn(a, b, precision=None):
    return lax.dot_general(a, b, (((2,), (1,)), ((0,), (0,))), precision=precision,
                           preferred_element_type=F32)


def _bnt(a, b, precision=None):
    return lax.dot_general(a, b, (((2,), (2,)), ((0,), (0,))), precision=precision,
                           preferred_element_type=F32)


def _sigmoid(x):
    return 1.0 / (1.0 + jnp.exp(-x))


def _rms_stats(x):
    r = lax.rsqrt(jnp.mean(x * x, axis=-1, keepdims=True) + NORM_EPS)
    return r, x * r


def _norm_bwd(x, nw, dhn):
    r, xhat = _rms_stats(x)
    dxh = dhn * nw
    dx = r * (dxh - xhat * jnp.mean(dxh * xhat, axis=-1, keepdims=True))
    dnw = jnp.sum(dhn * xhat, axis=0, keepdims=True)
    return dx, dnw


def _row_tile(t, pref):
    return min(t, pref)


FFN_TM = 512
FFN_BWD_TM = 512
FFN_BWD_TF = 1408
FFN_BWD_VMEM_LIMIT_BYTES = 61 * 1024 * 1024
FFN_TF = 1408


def _carried(body, n_in, n_out, carry, scatter, last_step):
    nc = len(carry)
    if not nc:
        return body

    def wrapped(*refs):
        ins, cin = refs[:n_in], refs[n_in:n_in + nc]
        outs = refs[n_in + nc:n_in + nc + n_out]
        cout = refs[n_in + nc + n_out:n_in + 2 * nc + n_out]
        scratch = refs[n_in + 2 * nc + n_out:]
        sems = scratch[len(scratch) - 3:]
        i, j = pl.program_id(0), pl.program_id(1)

        @pl.when(jnp.logical_and(i == 0, j == 0))
        def _():
            for cp in _exchange_copies(cin, cout, *sems, scatter):
                cp.start()

        body(*ins, *outs, *scratch[:len(scratch) - 3])

        @pl.when(jnp.logical_and(i == last_step[0], j == last_step[1]))
        def _():
            for cp in _exchange_copies(cin, cout, *sems, scatter):
                cp.wait()

    return wrapped


def _ffn_fwd(h, nw, wg, wu, wd, carry=(), scatter=False):
    t = h.shape[0]
    tm, tf = _row_tile(t, FFN_TM), FFN_TF
    nj = D_FF // tf
    nc = len(carry)

    def body(h_ref, nw_ref, wg_ref, wu_ref, wd_ref, out_ref, g_ref, u_ref, hn_ref, acc_ref):
        j = pl.program_id(1)

        @pl.when(j == 0)
        def _():
            _, xhat = _rms_stats(h_ref[...])
            hn_ref[...] = (xhat * nw_ref[...]).astype(BF16)
            acc_ref[...] = jnp.zeros_like(acc_ref)

        hn = hn_ref[...]
        g = _nn(hn, wg_ref[...])
        u = _nn(hn, wu_ref[...])
        g_ref[...] = g.astype(BF16)
        u_ref[...] = u.astype(BF16)
        a = (g * _sigmoid(g) * u).astype(BF16)
        acc_ref[...] += _nn(a, wd_ref[...])

        @pl.when(j == nj - 1)
        def _():
            out_ref[...] = h_ref[...] + 0.5 * acc_ref[...]

    outs = pl.pallas_call(
        _carried(body, 5, 4, carry, scatter, (t // tm - 1, nj - 1)),
        name="ffn_fwd_x" if nc else "ffn_fwd", grid=(t // tm, nj),
        in_specs=[pl.BlockSpec((tm, D_MODEL), lambda i, j: (i, 0)),
                  pl.BlockSpec((1, D_MODEL), lambda i, j: (0, 0)),
                  pl.BlockSpec((D_MODEL, tf), lambda i, j: (0, j)),
                  pl.BlockSpec((D_MODEL, tf), lambda i, j: (0, j)),
                  pl.BlockSpec((tf, D_MODEL), lambda i, j: (j, 0))] + [_ANY] * nc,
        out_specs=[pl.BlockSpec((tm, D_MODEL), lambda i, j: (i, 0)),
                   pl.BlockSpec((tm, tf), lambda i, j: (i, j)),
                   pl.BlockSpec((tm, tf), lambda i, j: (i, j)),
                   pl.BlockSpec((tm, D_MODEL), lambda i, j: (i, 0))] + [_ANY] * nc,
        out_shape=[jax.ShapeDtypeStruct((t, D_MODEL), F32),
                   jax.ShapeDtypeStruct((t, D_FF), BF16),
                   jax.ShapeDtypeStruct((t, D_FF), BF16),
                   jax.ShapeDtypeStruct((t, D_MODEL), BF16)] + _exchange_shapes(carry, scatter),
        scratch_shapes=[pltpu.VMEM((tm, D_MODEL), F32)] + (_exchange_sems(nc) if nc else []),
        compiler_params=_params(2),
    )(h, nw, wg, wu, wd, *carry)
    return outs[:4], outs[4:]


def _ffn_bwd(dy, h, nw, g, u, wg, wu, wd, carry=(), scatter=True):
    t = h.shape[0]
    tm, tf = _row_tile(t, FFN_BWD_TM), FFN_BWD_TF
    nj = D_FF // tf
    nc = len(carry)

    def body(dy_ref, h_ref, nw_ref, g_ref, u_ref, wg_ref, wu_ref, wd_ref,
             dh_ref, dg_ref, du_ref, a_ref, dz_ref, dnw_ref, acc_ref):
        i, j = pl.program_id(0), pl.program_id(1)

        @pl.when(j == 0)
        def _():
            dz_ref[...] = (0.5 * dy_ref[...]).astype(BF16)
            acc_ref[...] = jnp.zeros_like(acc_ref)

        @pl.when(jnp.logical_and(i == 0, j == 0))
        def _():
            dnw_ref[...] = jnp.zeros_like(dnw_ref)

        da = _nt(dz_ref[...], wd_ref[...])
        gv = g_ref[...].astype(F32)
        uv = u_ref[...].astype(F32)
        sig = _sigmoid(gv)
        silu = gv * sig
        dg = (da * uv * (sig * (1.0 + gv * (1.0 - sig)))).astype(BF16)
        du = (da * silu).astype(BF16)
        dg_ref[...] = dg
        du_ref[...] = du
        a_ref[...] = (silu * uv).astype(BF16)
        acc_ref[...] += _nt(dg, wg_ref[...]) + _nt(du, wu_ref[...])

        @pl.when(j == nj - 1)
        def _():
            dx, dnw = _norm_bwd(h_ref[...], nw_ref[...], acc_ref[...])
            dh_ref[...] = dy_ref[...] + dx
            dnw_ref[...] += dnw

    outs = pl.pallas_call(
        _carried(body, 8, 6, carry, scatter, (t // tm - 1, nj - 1)),
        name="ffn_bwd_x" if nc else "ffn_bwd", grid=(t // tm, nj),
        in_specs=[pl.BlockSpec((tm, D_MODEL), lambda i, j: (i, 0)),
                  pl.BlockSpec((tm, D_MODEL), lambda i, j: (i, 0)),
                  pl.BlockSpec((1, D_MODEL), lambda i, j: (0, 0)),
                  pl.BlockSpec((tm, tf), lambda i, j: (i, j)),
                  pl.BlockSpec((tm, tf), lambda i, j: (i, j)),
                  pl.BlockSpec((D_MODEL, tf), lambda i, j: (0, j)),
                  pl.BlockSpec((D_MODEL, tf), lambda i, j: (0, j)),
                  pl.BlockSpec((tf, D_MODEL), lambda i, j: (j, 0))] + [_ANY] * nc,
        out_specs=[pl.BlockSpec((tm, D_MODEL), lambda i, j: (i, 0)),
                   pl.BlockSpec((tm, tf), lambda i, j: (i, j)),
                   pl.BlockSpec((tm, tf), lambda i, j: (i, j)),
                   pl.BlockSpec((tm, tf), lambda i, j: (i, j)),
                   pl.BlockSpec((tm, D_MODEL), lambda i, j: (i, 0)),
                   pl.BlockSpec((1, D_MODEL), lambda i, j: (0, 0))] + [_ANY] * nc,
        out_shape=[jax.ShapeDtypeStruct((t, D_MODEL), F32),
                   jax.ShapeDtypeStruct((t, D_FF), BF16),
                   jax.ShapeDtypeStruct((t, D_FF), BF16),
                   jax.ShapeDtypeStruct((t, D_FF), BF16),
                   jax.ShapeDtypeStruct((t, D_MODEL), BF16),
                   jax.ShapeDtypeStruct((1, D_MODEL), F32)] + _exchange_shapes(carry, scatter),
        scratch_shapes=[pltpu.VMEM((tm, D_MODEL), F32)] + (_exchange_sems(nc) if nc else []),
        compiler_params=_params(2, FFN_BWD_VMEM_LIMIT_BYTES),
    )(dy, h, nw, g, u, wg, wu, wd, *carry)
    return outs[:6], outs[6:]


def _matmul_tn(a, b, tn, name):
    k, m = a.shape
    n = b.shape[1]
    tm = min(m, 1408 if m % 1408 == 0 else 1024)
    tk = min(k, 2048)

    def body(a_ref, b_ref, o_ref):
        @pl.when(pl.program_id(2) == 0)
        def _():
            o_ref[...] = jnp.zeros_like(o_ref)

        o_ref[...] += _tn(a_ref[...], b_ref[...])

    return pl.pallas_call(
        body, name=name, grid=(m // tm, n // tn, k // tk),
        in_specs=[pl.BlockSpec((tk, tm), lambda i, j, kk: (kk, i)),
                  pl.BlockSpec((tk, tn), lambda i, j, kk: (kk, j))],
        out_specs=pl.BlockSpec((tm, tn), lambda i, j, kk: (i, j)),
        out_shape=jax.ShapeDtypeStruct((m, n), F32),
        compiler_params=_params(3),
    )(a, b)


ATTN_TM = 512


def _rope(t, c, s1, s2):
    return t * c + pltpu.roll(t, LANES - ROPE_DIM // 2, 1) * s1 + pltpu.roll(t, ROPE_DIM // 2, 1) * s2


def _rope_bwd(d, c, s1, s2):
    return d * c + pltpu.roll(d * s1, ROPE_DIM // 2, 1) + pltpu.roll(d * s2, LANES - ROPE_DIM // 2, 1)


def _attn_qkv_fwd(h, nw, w_in, b_in, rc, rs1, rs2):
    t = h.shape[0]
    tm = _row_tile(t, ATTN_TM)

    def body(h_ref, nw_ref, w_ref, b_ref, c_ref, s1_ref, s2_ref, q_ref, k_ref, v_ref, hn_ref):
        _, xhat = _rms_stats(h_ref[...])
        hn = (xhat * nw_ref[...]).astype(BF16)
        hn_ref[...] = hn
        qkv = _nn(hn, w_ref[...]) + b_ref[...]
        c, s1, s2 = c_ref[...], s1_ref[...], s2_ref[...]
        for s in range(ATTN_Q_W // LANES):
            q_ref[:, s * LANES:(s + 1) * LANES] = _rope(qkv[:, s * LANES:(s + 1) * LANES], c, s1, s2).astype(BF16)
        for s in range(ATTN_KV_W // LANES):
            lo = ATTN_Q_W + s * LANES
            k_ref[:, s * LANES:(s + 1) * LANES] = _rope(qkv[:, lo:lo + LANES], c, s1, s2).astype(BF16)
        v_ref[...] = qkv[:, ATTN_Q_W + ATTN_KV_W:].astype(BF16)

    row = lambda w: pl.BlockSpec((tm, w), lambda i: (i, 0))
    full = lambda a: pl.BlockSpec(a.shape, lambda i: (0, 0))
    return pl.pallas_call(
        body, name="attn_qkv_fwd", grid=(t // tm,),
        in_specs=[row(D_MODEL), full(nw), full(w_in), full(b_in), row(LANES), row(LANES), row(LANES)],
        out_specs=[row(ATTN_Q_W), row(ATTN_KV_W), row(ATTN_KV_W), row(D_MODEL)],
        out_shape=[jax.ShapeDtypeStruct((t, ATTN_Q_W), BF16),
                   jax.ShapeDtypeStruct((t, ATTN_KV_W), BF16),
                   jax.ShapeDtypeStruct((t, ATTN_KV_W), BF16),
                   jax.ShapeDtypeStruct((t, D_MODEL), BF16)],
        compiler_params=_params(1),
    )(h, nw, w_in, b_in, rc, rs1, rs2)


def _attn_group(q_ref, kc_ref, kp_ref, vc_ref, vp_ref, sinks_ref, hk, n):
    hd = ATTN_HEAD_DIM
    cols = slice(hk * hd, (hk + 1) * hd)
    krow = lax.broadcasted_iota(jnp.int32, (2 * ATTN_BLOCK, hd), 0)
    kcat = jnp.concatenate([kp_ref[:, cols], kc_ref[:, cols]], axis=0)
    vcat = jnp.concatenate([vp_ref[:, cols], vc_ref[:, cols]], axis=0)
    kcat = jnp.where(krow == 0, jnp.zeros_like(kcat), kcat)
    vcat = jnp.where(krow == 0, jnp.zeros_like(vcat), vcat)
    heads = [hk * ATTN_GROUP + g for g in range(ATTN_GROUP)]
    qs = jnp.concatenate([q_ref[:, hq * hd:(hq + 1) * hd] for hq in heads], axis=0)
    s = _nt(qs, kcat) * ATTN_SCALE
    rows = ATTN_GROUP * ATTN_BLOCK
    ri = lax.broadcasted_iota(jnp.int32, (rows, 2 * ATTN_BLOCK), 0) & (ATTN_BLOCK - 1)
    cj = lax.broadcasted_iota(jnp.int32, (rows, 2 * ATTN_BLOCK), 1)
    first = jnp.where(n > 0, 0, 2 * ATTN_BLOCK)
    valid = jnp.logical_or(jnp.logical_and(cj < ATTN_BLOCK, cj > ri + first),
                           jnp.logical_and(cj >= ATTN_BLOCK, cj - ATTN_BLOCK <= ri))
    sink = jnp.concatenate([jnp.full((ATTN_BLOCK, 2 * ATTN_BLOCK), sinks_ref[hq], F32) for hq in heads], axis=0)
    s = jnp.where(valid, s, jnp.where(cj == 0, sink, NEG_BIG))
    p = jnp.exp(s - jnp.max(s, axis=-1, keepdims=True))
    probs = p / jnp.sum(p, axis=-1, keepdims=True)
    return heads, qs, kcat, vcat, probs, krow


def _attn_core_fwd(q, k, v, sinks):
    t = q.shape[0]
    nb = t // ATTN_BLOCK

    def body(q_ref, kc_ref, kp_ref, vc_ref, vp_ref, sinks_ref, o_ref):
        n = pl.program_id(0)
        for hk in range(ATTN_KV_HEADS):
            heads, _, _, vcat, probs, _ = _attn_group(q_ref, kc_ref, kp_ref, vc_ref, vp_ref, sinks_ref, hk, n)
            o = _nn(probs.astype(BF16), vcat)
            for g, hq in enumerate(heads):
                o_ref[:, hq * ATTN_HEAD_DIM:(hq + 1) * ATTN_HEAD_DIM] = (
                    o[g * ATTN_BLOCK:(g + 1) * ATTN_BLOCK].astype(BF16))

    cur = lambda w: pl.BlockSpec((ATTN_BLOCK, w), lambda n: (n, 0))
    prev = lambda w: pl.BlockSpec((ATTN_BLOCK, w), lambda n: (jnp.maximum(n - 1, 0), 0))
    return pl.pallas_call(
        body, name="attn_core_fwd", grid=(nb,),
        in_specs=[cur(ATTN_Q_W), cur(ATTN_KV_W), prev(ATTN_KV_W), cur(ATTN_KV_W), prev(ATTN_KV_W),
                  pl.BlockSpec(memory_space=pltpu.SMEM)],
        out_specs=cur(ATTN_Q_W),
        out_shape=jax.ShapeDtypeStruct((t, ATTN_Q_W), BF16),
        compiler_params=_params(1),
    )(q, k, k, v, v, sinks.reshape(-1))


def _attn_core_bwd(q, k, v, do, sinks):
    t = q.shape[0]
    nb = t // ATTN_BLOCK
    hd = ATTN_HEAD_DIM

    def body(q_ref, kc_ref, kp_ref, vc_ref, vp_ref, do_ref, sinks_ref,
             dq_ref, dk_ref, dv_ref, dsink_ref, ck_ref, cv_ref):
        n = pl.program_id(0)

        @pl.when(n == 0)
        def _():
            dsink_ref[...] = jnp.zeros_like(dsink_ref)
            ck_ref[...] = jnp.zeros_like(ck_ref)
            cv_ref[...] = jnp.zeros_like(cv_ref)

        @pl.when(n == nb)
        def _():
            dk_ref[...] = ck_ref[...]
            dv_ref[...] = cv_ref[...]

        @pl.when(n < nb)
        def _():
            for hk in range(ATTN_KV_HEADS):
                heads, qs, kcat, vcat, probs, krow = _attn_group(
                    q_ref, kc_ref, kp_ref, vc_ref, vp_ref, sinks_ref, hk, n)
                dos = jnp.concatenate([do_ref[:, hq * hd:(hq + 1) * hd] for hq in heads], axis=0)
                dp = _nt(dos, vcat)
                delta = jnp.sum(probs * dp, axis=-1, keepdims=True)
                dsf = probs * (dp - delta)
                ds = dsf.astype(BF16)
                dqs = _nn(ds, kcat) * ATTN_SCALE
                dkc = jnp.where(krow == 0, 0.0, _tn(ds, qs) * ATTN_SCALE)
                dvc = jnp.where(krow == 0, 0.0, _tn(probs.astype(BF16), dos))
                for g, hq in enumerate(heads):
                    blk = slice(g * ATTN_BLOCK, (g + 1) * ATTN_BLOCK)
                    dq_ref[:, hq * hd:(hq + 1) * hd] = dqs[blk]
                    dsink_ref[hq:hq + 1, :] += jnp.broadcast_to(
                        jnp.sum(dsf[blk, 0:1], axis=0, keepdims=True), (1, LANES))
                cols = slice(hk * hd, (hk + 1) * hd)
                dk_ref[:, cols] = ck_ref[:, cols] + dkc[:ATTN_BLOCK]
                dv_ref[:, cols] = cv_ref[:, cols] + dvc[:ATTN_BLOCK]
                ck_ref[:, cols] = dkc[ATTN_BLOCK:]
                cv_ref[:, cols] = dvc[ATTN_BLOCK:]

    cur = lambda w: pl.BlockSpec((ATTN_BLOCK, w), lambda n: (jnp.minimum(n, nb - 1), 0))
    prev = lambda w: pl.BlockSpec((ATTN_BLOCK, w), lambda n: (jnp.clip(n - 1, 0, nb - 1), 0))
    return pl.pallas_call(
        body, name="attn_core_bwd", grid=(nb + 1,),
        in_specs=[cur(ATTN_Q_W), cur(ATTN_KV_W), prev(ATTN_KV_W), cur(ATTN_KV_W), prev(ATTN_KV_W),
                  cur(ATTN_Q_W), pl.BlockSpec(memory_space=pltpu.SMEM)],
        out_specs=[cur(ATTN_Q_W), prev(ATTN_KV_W), prev(ATTN_KV_W),
                   pl.BlockSpec((ATTN_Q_HEADS, LANES), lambda n: (0, 0))],
        out_shape=[jax.ShapeDtypeStruct((t, ATTN_Q_W), F32),
                   jax.ShapeDtypeStruct((t, ATTN_KV_W), F32),
                   jax.ShapeDtypeStruct((t, ATTN_KV_W), F32),
                   jax.ShapeDtypeStruct((ATTN_Q_HEADS, LANES), F32)],
        scratch_shapes=[pltpu.VMEM((ATTN_BLOCK, ATTN_KV_W), F32),
                        pltpu.VMEM((ATTN_BLOCK, ATTN_KV_W), F32)],
        compiler_params=_params(1),
    )(q, k, k, v, v, do, sinks.reshape(-1))


def _proj_out_fwd(x, w, b, res):
    t = x.shape[0]
    tm = _row_tile(t, 512)

    def body(x_ref, w_ref, b_ref, r_ref, o_ref):
        o_ref[...] = r_ref[...] + _nn(x_ref[...], w_ref[...]) + b_ref[...]

    row = pl.BlockSpec((tm, D_MODEL), lambda i: (i, 0))
    return pl.pallas_call(
        body, name="proj_out_fwd", grid=(t // tm,),
        in_specs=[row, pl.BlockSpec(w.shape, lambda i: (0, 0)), pl.BlockSpec(b.shape, lambda i: (0, 0)), row],
        out_specs=row,
        out_shape=jax.ShapeDtypeStruct((t, D_MODEL), F32),
        compiler_params=_params(1),
    )(x, w, b, res)


def _proj_out_bwd(dy, w):
    t = dy.shape[0]
    tm = _row_tile(t, 512)

    def body(dy_ref, w_ref, dx_ref, dyb_ref, db_ref):
        @pl.when(pl.program_id(0) == 0)
        def _():
            db_ref[...] = jnp.zeros_like(db_ref)

        dy_v = dy_ref[...]
        dyb = dy_v.astype(BF16)
        dyb_ref[...] = dyb
        dx_ref[...] = _nt(dyb, w_ref[...]).astype(BF16)
        db_ref[...] += jnp.sum(dy_v, axis=0, keepdims=True)

    row = pl.BlockSpec((tm, D_MODEL), lambda i: (i, 0))
    return pl.pallas_call(
        body, name="proj_out_bwd", grid=(t // tm,),
        in_specs=[row, pl.BlockSpec(w.shape, lambda i: (0, 0))],
        out_specs=[row, row, pl.BlockSpec((1, D_MODEL), lambda i: (0, 0))],
        out_shape=[jax.ShapeDtypeStruct((t, D_MODEL), BF16),
                   jax.ShapeDtypeStruct((t, D_MODEL), BF16),
                   jax.ShapeDtypeStruct((1, D_MODEL), F32)],
        compiler_params=_params(1),
    )(dy, w)


def _attn_qkv_bwd(dy, h, nw, w_in, dq, dk, dv, rc, rs1, rs2):
    t = h.shape[0]
    tm = _row_tile(t, ATTN_TM)

    def body(dy_ref, h_ref, nw_ref, w_ref, dq_ref, dk_ref, dv_ref, c_ref, s1_ref, s2_ref,
             dh_ref, dqkv_ref, db_ref, dnw_ref, tmp_ref):
        @pl.when(pl.program_id(0) == 0)
        def _():
            db_ref[...] = jnp.zeros_like(db_ref)
            dnw_ref[...] = jnp.zeros_like(dnw_ref)

        c, s1, s2 = c_ref[...], s1_ref[...], s2_ref[...]
        for s in range(ATTN_Q_W // LANES):
            tmp_ref[:, s * LANES:(s + 1) * LANES] = _rope_bwd(dq_ref[:, s * LANES:(s + 1) * LANES], c, s1, s2)
        for s in range(ATTN_KV_W // LANES):
            lo = ATTN_Q_W + s * LANES
            tmp_ref[:, lo:lo + LANES] = _rope_bwd(dk_ref[:, s * LANES:(s + 1) * LANES], c, s1, s2)
        tmp_ref[:, ATTN_Q_W + ATTN_KV_W:] = dv_ref[...]
        dqkv = tmp_ref[...]
        db_ref[...] += jnp.sum(dqkv, axis=0, keepdims=True)
        dqkv_b = dqkv.astype(BF16)
        dqkv_ref[...] = dqkv_b
        dx, dnw = _norm_bwd(h_ref[...], nw_ref[...], _nt(dqkv_b, w_ref[...]))
        dh_ref[...] = dy_ref[...] + dx
        dnw_ref[...] += dnw

    row = lambda w: pl.BlockSpec((tm, w), lambda i: (i, 0))
    full = lambda a: pl.BlockSpec(a.shape, lambda i: (0, 0))
    return pl.pallas_call(
        body, name="attn_qkv_bwd", grid=(t // tm,),
        in_specs=[row(D_MODEL), row(D_MODEL), full(nw), full(w_in), row(ATTN_Q_W), row(ATTN_KV_W),
                  row(ATTN_KV_W), row(LANES), row(LANES), row(LANES)],
        out_specs=[row(D_MODEL), row(ATTN_IN), pl.BlockSpec((1, ATTN_IN), lambda i: (0, 0)),
                   pl.BlockSpec((1, D_MODEL), lambda i: (0, 0))],
        out_shape=[jax.ShapeDtypeStruct((t, D_MODEL), F32),
                   jax.ShapeDtypeStruct((t, ATTN_IN), BF16),
                   jax.ShapeDtypeStruct((1, ATTN_IN), F32),
                   jax.ShapeDtypeStruct((1, D_MODEL), F32)],
        scratch_shapes=[pltpu.VMEM((tm, ATTN_IN), F32)],
        compiler_params=_params(1),
    )(dy, h, nw, w_in, dq, dk, dv, rc, rs1, rs2)


GDN_TM = 256
GDN_CONV_TM = 256


def _gdn_proj_fwd(h, nw, w_qkv, w_z, w_ba):
    t = h.shape[0]
    tm = _row_tile(t, GDN_TM)

    def body(h_ref, nw_ref, wq_ref, wz_ref, wb_ref, x_ref, z_ref, ba_ref, hn_ref):
        _, xhat = _rms_stats(h_ref[...])
        hn = (xhat * nw_ref[...]).astype(BF16)
        hn_ref[...] = hn
        x_ref[...] = _nn(hn, wq_ref[...])
        z_ref[...] = _nn(hn, wz_ref[...])
        ba_ref[...] = _nn(hn, wb_ref[...])

    row = lambda w: pl.BlockSpec((tm, w), lambda i: (i, 0))
    full = lambda a: pl.BlockSpec(a.shape, lambda i: (0, 0))
    return pl.pallas_call(
        body, name="gdn_proj_fwd", grid=(t // tm,),
        in_specs=[row(D_MODEL), full(nw), full(w_qkv), full(w_z), full(w_ba)],
        out_specs=[row(GDN_CONV_W), row(D_MODEL), row(LANES), row(D_MODEL)],
        out_shape=[jax.ShapeDtypeStruct((t, GDN_CONV_W), F32),
                   jax.ShapeDtypeStruct((t, D_MODEL), F32),
                   jax.ShapeDtypeStruct((t, LANES), F32),
                   jax.ShapeDtypeStruct((t, D_MODEL), BF16)],
        compiler_params=_params(1),
    )(h, nw, w_qkv, w_z, w_ba)


def _softplus(x):
    return jnp.maximum(x, 0.0) + jnp.log(1.0 + jnp.exp(-jnp.abs(x)))


def _conv_taps(x, halo, cw):
    tm = x.shape[0]
    xx = jnp.concatenate([halo, x], axis=0)
    taps = [xx[8 - (GDN_CONV - 1) + i: 8 - (GDN_CONV - 1) + i + tm] for i in range(GDN_CONV)]
    c = taps[0] * cw[0:1]
    for i in range(1, GDN_CONV):
        c = c + taps[i] * cw[i:i + 1]
    return c, taps


def _gates(ba, gp):
    lane = lax.broadcasted_iota(jnp.int32, ba.shape, 1)
    beta = _sigmoid(ba)
    pre = ba + gp[1:2]
    g = -jnp.exp(gp[0:1]) * _softplus(pre)
    gates = jnp.where(lane < GDN_HEADS, beta, jnp.where(lane < 2 * GDN_HEADS, g, 0.0))
    return lane, beta, pre, g, gates


def _gdn_conv_fwd(x, cw, ba, gp):
    t = x.shape[0]
    tm = _row_tile(t, GDN_CONV_TM)

    def body(x_ref, halo_ref, cw_ref, ba_ref, gp_ref, q_ref, k_ref, v_ref, gates_ref):
        halo = jnp.where(pl.program_id(0) > 0, halo_ref[...], 0.0)
        c, _ = _conv_taps(x_ref[...], halo, cw_ref[...])
        s = c * _sigmoid(c)
        for hh in range(GDN_HEADS):
            sq = s[:, hh * LANES:(hh + 1) * LANES]
            q_ref[:, hh * LANES:(hh + 1) * LANES] = (
                sq * lax.rsqrt(jnp.sum(sq * sq, axis=-1, keepdims=True) + NORM_EPS) * GDN_QSCALE)
            sk = s[:, D_MODEL + hh * LANES:D_MODEL + (hh + 1) * LANES]
            k_ref[:, hh * LANES:(hh + 1) * LANES] = (
                sk * lax.rsqrt(jnp.sum(sk * sk, axis=-1, keepdims=True) + NORM_EPS))
        v_ref[...] = s[:, 2 * D_MODEL:]
        gates_ref[...] = _gates(ba_ref[...], gp_ref[...])[4]

    row = lambda w: pl.BlockSpec((tm, w), lambda i: (i, 0))
    full = lambda a: pl.BlockSpec(a.shape, lambda i: (0, 0))
    halo = pl.BlockSpec((8, GDN_CONV_W), lambda i: (jnp.maximum(i * (tm // 8) - 1, 0), 0))
    return pl.pallas_call(
        body, name="gdn_conv_fwd", grid=(t // tm,),
        in_specs=[row(GDN_CONV_W), halo, full(cw), row(LANES), full(gp)],
        out_specs=[row(D_MODEL), row(D_MODEL), row(D_MODEL), row(LANES)],
        out_shape=[jax.ShapeDtypeStruct((t, D_MODEL), F32)] * 3 + [jax.ShapeDtypeStruct((t, LANES), F32)],
        compiler_params=_params(1),
    )(x, x, cw, ba, gp)


def _split3(x):
    hi = x.astype(BF16).astype(F32)
    r1 = x - hi
    mid = r1.astype(BF16).astype(F32)
    lo = (r1 - mid).astype(BF16).astype(F32)
    return hi, mid, lo


def _chunk_cumsum(col, keep):
    nb, c, _ = col.shape
    l3 = lax.broadcasted_iota(jnp.int32, (nb, c, LANES), 2)
    hi, mid, lo = _split3(col)
    pieces = jnp.where(l3 == 0, hi, jnp.where(l3 == 1, mid, jnp.where(l3 == 2, lo, 0.0))).astype(BF16)
    s = _bnn(jnp.where(keep, 1.0, 0.0).astype(BF16), pieces)
    return s[..., 0:1] + s[..., 1:2] + s[..., 2:3]


def _unit_inverse(nmat):
    nb, c, _ = nmat.shape
    ri = lax.broadcasted_iota(jnp.int32, (nb, c, c), 1)
    ci = lax.broadcasted_iota(jnp.int32, (nb, c, c), 2)
    eye = jnp.where(ri == ci, 1.0, 0.0).astype(F32)
    same = (ri >> GDN_SUB_SHIFT) == (ci >> GDN_SUB_SHIFT)
    nd = jnp.where(same, nmat, 0.0)
    no = nmat - nd
    mm = lambda a, b: _bnn(a.astype(BF16), b.astype(BF16))
    n2 = mm(nd, nd)
    n4 = mm(n2, n2)
    n8 = mm(n4, n4)
    td = mm(mm(mm(eye - nd, eye + n2), eye + n4), eye + n8)
    bm = mm(td, no)
    b2 = mm(bm, bm)
    return mm(mm(eye - bm, eye + b2), td)


def _chunk_terms(q_ref, k_ref, gates_ref, h, transposed):
    rows = k_ref.shape[0]
    nb = rows // GDN_CHUNK
    c = GDN_CHUNK
    lane = lax.broadcasted_iota(jnp.int32, (rows, LANES), 1)
    gt = gates_ref[...]
    beta = jnp.sum(jnp.where(lane == h, gt, 0.0), axis=-1, keepdims=True).reshape(nb, c, 1)
    g = jnp.sum(jnp.where(lane == GDN_HEADS + h, gt, 0.0), axis=-1, keepdims=True).reshape(nb, c, 1)
    ri = lax.broadcasted_iota(jnp.int32, (nb, c, c), 1)
    ci = lax.broadcasted_iota(jnp.int32, (nb, c, c), 2)
    gcol = _chunk_cumsum(g, ri >= ci)
    gamma = jnp.broadcast_to(gcol, (nb, c, LANES))
    l3 = lax.broadcasted_iota(jnp.int32, (nb, c, LANES), 2)
    gh, gm, gl = _split3(gcol)
    pmat = jnp.where(l3 == 0, gh, jnp.where(l3 == 1, gm, jnp.where(l3 == 2, gl, jnp.where(l3 < 6, 1.0, 0.0))))
    qmat = jnp.where(l3 < 3, 1.0, jnp.where(l3 == 3, -gh, jnp.where(l3 == 4, -gm, jnp.where(l3 == 5, -gl, 0.0))))
    pmat, qmat = pmat.astype(BF16), qmat.astype(BF16)
    k = k_ref[...].reshape(nb, c, LANES)
    q = q_ref[...].reshape(nb, c, LANES)
    kb = k * beta
    kbf, kbb, qb = k.astype(BF16), kb.astype(BF16), q.astype(BF16)
    out = dict(beta=beta, g=g, gamma=gamma, k=k, q=q, kb=kb, kbf=kbf, kbb=kbb, qb=qb, ri=ri, ci=ci)
    diff = _bnt(pmat, qmat)
    lmat = jnp.exp(jnp.where(ri >= ci, diff, NEG_BIG))
    out["L"] = lmat
    out["A"] = jnp.where(ri > ci, _bnt(kbb, kbf) * lmat, 0.0)
    out["Aqk"] = jnp.where(ri >= ci, _bnt(qb, kbf) * lmat, 0.0)
    if transposed:
        difft = _bnt(qmat, pmat)
        lt = jnp.exp(jnp.where(ci >= ri, difft, NEG_BIG))
        out["LT"] = lt
        out["AT"] = jnp.where(ci > ri, _bnt(kbf, kbb) * lt, 0.0)
        out["AqkT"] = jnp.where(ci >= ri, _bnt(kbf, qb) * lt, 0.0)
    return out


def _gdn_intra_fwd(q, k, v, gates):
    t = q.shape[0]
    rows = _row_tile(t, GDN_ROWS)
    nb = rows // GDN_CHUNK
    nchunks = t // GDN_CHUNK

    def body(q_ref, k_ref, v_ref, gates_ref, u_ref, w_ref, qd_ref, kd_ref, aqk_ref, cd_ref):
        h = pl.program_id(1)
        tm_ = _chunk_terms(q_ref, k_ref, gates_ref, h, False)
        gamma, beta = tm_["gamma"], tm_["beta"]
        eg = jnp.exp(gamma)
        tinv = _unit_inverse(tm_["A"])
        v3 = v_ref[...].reshape(nb, GDN_CHUNK, LANES)
        rhs = jnp.concatenate([v3 * beta, tm_["kb"] * eg], axis=-1)
        sol = _bnn(tinv.astype(BF16), rhs.astype(BF16))
        u_ref[...] = sol[..., :LANES].reshape(rows, LANES)
        w_ref[...] = sol[..., LANES:].reshape(rows, LANES).astype(BF16)
        gl = gamma[:, GDN_CHUNK - 1:GDN_CHUNK, :]
        qd_ref[...] = (tm_["q"] * eg).reshape(rows, LANES).astype(BF16)
        kd_ref[...] = (tm_["k"] * jnp.exp(gl - gamma)).reshape(rows, LANES).astype(BF16)
        aqk_ref[0] = tm_["Aqk"].reshape(rows, GDN_CHUNK).astype(BF16)
        cd_ref[0] = jnp.broadcast_to(jnp.exp(gl), (nb, 8, LANES)).reshape(nb * 8, LANES)

    blk = pl.BlockSpec((rows, LANES), lambda i, h: (i, h))
    return pl.pallas_call(
        body, name="gdn_intra_fwd", grid=(t // rows, GDN_HEADS),
        in_specs=[blk, blk, blk, pl.BlockSpec((rows, LANES), lambda i, h: (i, 0))],
        out_specs=[blk, blk, blk, blk,
                   pl.BlockSpec((1, rows, GDN_CHUNK), lambda i, h: (h, i, 0)),
                   pl.BlockSpec((1, nb * 8, LANES), lambda i, h: (h, i, 0))],
        out_shape=[jax.ShapeDtypeStruct((t, D_MODEL), F32),
                   jax.ShapeDtypeStruct((t, D_MODEL), BF16),
                   jax.ShapeDtypeStruct((t, D_MODEL), BF16),
                   jax.ShapeDtypeStruct((t, D_MODEL), BF16),
                   jax.ShapeDtypeStruct((GDN_HEADS, t, GDN_CHUNK), BF16),
                   jax.ShapeDtypeStruct((GDN_HEADS, nchunks * 8, LANES), F32)],
        compiler_params=_params(2),
    )(q, k, v, gates)


def _gdn_scan_fwd(u, w, qd, kd, aqk, cd):
    t = u.shape[0]
    rows = _row_tile(t, GDN_ROWS)
    nb = rows // GDN_CHUNK
    nchunks = t // GDN_CHUNK
    hs = GDN_SCAN_HEADS

    def body(u_ref, w_ref, qd_ref, kd_ref, aqk_ref, cd_ref, o_ref, st_ref, s_ref):
        @pl.when(pl.program_id(1) == 0)
        def _():
            s_ref[...] = jnp.zeros_like(s_ref)

        states = [s_ref[hh] for hh in range(hs)]
        for c in range(nb):
            r = slice(c * GDN_CHUNK, (c + 1) * GDN_CHUNK)
            for hh in range(hs):
                ln = slice(hh * LANES, (hh + 1) * LANES)
                s = states[hh]
                st_ref[hh, c * LANES:(c + 1) * LANES, :] = s
                sb = s.astype(BF16)
                vb = (u_ref[r, ln] - _nn(w_ref[r, ln], sb)).astype(BF16)
                o_ref[r, ln] = _nn(qd_ref[r, ln], sb) + _nn(aqk_ref[hh, r, :], vb)
                states[hh] = s * cd_ref[hh, c * 8:c * 8 + 1, :] + _tn(kd_ref[r, ln], vb)
        for hh in range(hs):
            s_ref[hh] = states[hh]

    blk = pl.BlockSpec((rows, hs * LANES), lambda h, i: (i, h))
    return pl.pallas_call(
        body, name="gdn_scan_fwd", grid=(GDN_HEADS // hs, t // rows),
        in_specs=[blk, blk, blk, blk,
                  pl.BlockSpec((hs, rows, GDN_CHUNK), lambda h, i: (h, i, 0)),
                  pl.BlockSpec((hs, nb * 8, LANES), lambda h, i: (h, i, 0))],
        out_specs=[blk, pl.BlockSpec((hs, nb * LANES, LANES), lambda h, i: (h, i, 0))],
        out_shape=[jax.ShapeDtypeStruct((t, D_MODEL), F32),
                   jax.ShapeDtypeStruct((GDN_HEADS, nchunks * LANES, LANES), F32)],
        scratch_shapes=[pltpu.VMEM((hs, LANES, LANES), F32)],
        compiler_params=_params(2),
    )(u, w, qd, kd, aqk, cd)


def _gdn_out_fwd(o, z, nw, w_out, res):
    t = o.shape[0]
    tm = _row_tile(t, GDN_TM)

    def body(o_ref, z_ref, nw_ref, w_ref, r_ref, out_ref, gated_ref):
        nwv = nw_ref[...]
        for hh in range(GDN_HEADS):
            sl = slice(hh * LANES, (hh + 1) * LANES)
            _, on = _rms_stats(o_ref[:, sl])
            zv = z_ref[:, sl]
            gated_ref[:, sl] = (on * nwv * (zv * _sigmoid(zv))).astype(BF16)
        out_ref[...] = r_ref[...] + _nn(gated_ref[...], w_ref[...])

    row = pl.BlockSpec((tm, D_MODEL), lambda i: (i, 0))
    full = lambda a: pl.BlockSpec(a.shape, lambda i: (0, 0))
    return pl.pallas_call(
        body, name="gdn_out_fwd", grid=(t // tm,),
        in_specs=[row, row, full(nw), full(w_out), row],
        out_specs=[row, row],
        out_shape=[jax.ShapeDtypeStruct((t, D_MODEL), F32), jax.ShapeDtypeStruct((t, D_MODEL), BF16)],
        compiler_params=_params(1),
    )(o, z, nw, w_out, res)


def _gdn_out_bwd(dy, w_out, o, z, nw):
    t = o.shape[0]
    tm = _row_tile(t, GDN_TM)

    def body(dy_ref, w_ref, o_ref, z_ref, nw_ref, do_ref, dz_ref, dyb_ref, dnw_ref, dgt_ref):
        @pl.when(pl.program_id(0) == 0)
        def _():
            dnw_ref[...] = jnp.zeros_like(dnw_ref)

        dyb = dy_ref[...].astype(BF16)
        dyb_ref[...] = dyb
        dgt_ref[...] = _nt(dyb, w_ref[...])
        nwv = nw_ref[...]
        for hh in range(GDN_HEADS):
            sl = slice(hh * LANES, (hh + 1) * LANES)
            r, on = _rms_stats(o_ref[:, sl])
            zv = z_ref[:, sl]
            sig = _sigmoid(zv)
            sz = zv * sig
            dgt = dgt_ref[:, sl]
            d_on = dgt * nwv * sz
            dz_ref[:, sl] = (dgt * on * nwv * (sig * (1.0 + zv * (1.0 - sig)))).astype(BF16)
            dnw_ref[...] += jnp.sum(dgt * on * sz, axis=0, keepdims=True)
            do_ref[:, sl] = (r * (d_on - on * jnp.mean(d_on * on, axis=-1, keepdims=True))).astype(BF16)

    row = pl.BlockSpec((tm, D_MODEL), lambda i: (i, 0))
    full = lambda a: pl.BlockSpec(a.shape, lambda i: (0, 0))
    return pl.pallas_call(
        body, name="gdn_out_bwd", grid=(t // tm,),
        in_specs=[row, full(w_out), row, row, full(nw)],
        out_specs=[row, row, row, pl.BlockSpec((1, LANES), lambda i: (0, 0))],
        out_shape=[jax.ShapeDtypeStruct((t, D_MODEL), BF16)] * 3 + [jax.ShapeDtypeStruct((1, LANES), F32)],
        scratch_shapes=[pltpu.VMEM((tm, D_MODEL), F32)],
        compiler_params=_params(1),
    )(dy, w_out, o, z, nw)


def _gdn_scan_bwd(u, w, qd, kd, aqk, cd, st, do):
    t = u.shape[0]
    rows = _row_tile(t, GDN_ROWS)
    nb = rows // GDN_CHUNK
    nchunks = t // GDN_CHUNK
    nsteps = t // rows
    cc = GDN_CHUNK
    hs = GDN_SCAN_HEADS

    def body(u_ref, w_ref, qd_ref, kd_ref, aqk_ref, cd_ref, st_ref, do_ref,
             du_ref, dw_ref, dqd_ref, dkd_ref, daqk_ref, daqkt_ref, dcd_ref, ds_ref):
        @pl.when(pl.program_id(1) == 0)
        def _():
            ds_ref[...] = jnp.zeros_like(ds_ref)

        ri = lax.broadcasted_iota(jnp.int32, (cc, cc), 0)
        ci = lax.broadcasted_iota(jnp.int32, (cc, cc), 1)
        dstates = [ds_ref[hh] for hh in range(hs)]
        for c in reversed(range(nb)):
            r = slice(c * cc, (c + 1) * cc)
            for hh in range(hs):
                ln = slice(hh * LANES, (hh + 1) * LANES)
                s = st_ref[hh, c * LANES:(c + 1) * LANES, :]
                sb = s.astype(BF16)
                dsn = dstates[hh]
                dsb = dsn.astype(BF16)
                wv, kdv, qdv, aq, dov = w_ref[r, ln], kd_ref[r, ln], qd_ref[r, ln], aqk_ref[hh, r, :], do_ref[r, ln]
                vb = (u_ref[r, ln] - _nn(wv, sb)).astype(BF16)
                dv = _tn(aq, dov) + _nn(kdv, dsb)
                dvb = dv.astype(BF16)
                daqk_ref[hh, r, :] = jnp.where(ri >= ci, _nt(dov, vb), 0.0)
                daqkt_ref[hh, r, :] = jnp.where(ci >= ri, _nt(vb, dov), 0.0)
                dqd_ref[r, ln] = _nt(dov, sb)
                dkd_ref[r, ln] = _nt(vb, dsb)
                dcd_ref[hh, c * 8:(c + 1) * 8, :] = jnp.broadcast_to(jnp.sum(s * dsn), (8, LANES))
                du_ref[r, ln] = dv
                dw_ref[r, ln] = -_nt(dvb, sb)
                dstates[hh] = _tn(qdv, dov) + dsn * cd_ref[hh, c * 8:c * 8 + 1, :] - _tn(wv, dvb)
        for hh in range(hs):
            ds_ref[hh] = dstates[hh]

    rev = lambda i: nsteps - 1 - i
    blk = pl.BlockSpec((rows, hs * LANES), lambda h, i: (rev(i), h))
    sq = pl.BlockSpec((hs, rows, cc), lambda h, i: (h, rev(i), 0))
    cdb = pl.BlockSpec((hs, nb * 8, LANES), lambda h, i: (h, rev(i), 0))
    return pl.pallas_call(
        body, name="gdn_scan_bwd", grid=(GDN_HEADS // hs, nsteps),
        in_specs=[blk, blk, blk, blk, sq, cdb,
                  pl.BlockSpec((hs, nb * LANES, LANES), lambda h, i: (h, rev(i), 0)), blk],
        out_specs=[blk, blk, blk, blk, sq, sq, cdb],
        out_shape=[jax.ShapeDtypeStruct((t, D_MODEL), F32)] * 4
        + [jax.ShapeDtypeStruct((GDN_HEADS, t, cc), F32)] * 2
        + [jax.ShapeDtypeStruct((GDN_HEADS, nchunks * 8, LANES), F32)],
        scratch_shapes=[pltpu.VMEM((hs, LANES, LANES), F32)],
        compiler_params=_params(2),
    )(u, w, qd, kd, aqk, cd, st, do)


def _gdn_intra_bwd(q, k, v, gates, u, w, du, dw, dqd, dkd, daqk, daqkt, dcd):
    t = q.shape[0]
    rows = _row_tile(t, GDN_ROWS)
    nb = rows // GDN_CHUNK
    cc = GDN_CHUNK

    def body(q_ref, k_ref, v_ref, gates_ref, u_ref, w_ref, du_ref, dw_ref, dqd_ref, dkd_ref,
             daqk_ref, daqkt_ref, dcd_ref, dq_ref, dk_ref, dv_ref, dgates_ref):
        h = pl.program_id(1)

        @pl.when(h == 0)
        def _():
            dgates_ref[...] = jnp.zeros_like(dgates_ref)

        tm_ = _chunk_terms(q_ref, k_ref, gates_ref, h, True)
        gamma, beta, kk, qq, kb = tm_["gamma"], tm_["beta"], tm_["k"], tm_["q"], tm_["kb"]
        kbf, kbb, qb = tm_["kbf"], tm_["kbb"], tm_["qb"]
        lmat, lt = tm_["L"], tm_["LT"]
        ri, ci = tm_["ri"], tm_["ci"]
        r3 = lambda ref: ref[...].reshape(nb, cc, LANES)
        eg = jnp.exp(gamma)
        gl = gamma[:, cc - 1:cc, :]
        ekd = jnp.exp(gl - gamma)
        v3 = r3(v_ref)
        tt = _unit_inverse(tm_["AT"])
        dsol = jnp.concatenate([r3(du_ref), r3(dw_ref)], axis=-1)
        sol = jnp.concatenate([r3(u_ref), r3(w_ref).astype(F32)], axis=-1)
        dx = _bnn(tt.astype(BF16), dsol.astype(BF16))
        dxb, solb = dx.astype(BF16), sol.astype(BF16)
        da = jnp.where(ri > ci, -_bnt(dxb, solb), 0.0)
        dat = jnp.where(ci > ri, -_bnt(solb, dxb), 0.0)
        dxu, dxw = dx[..., :LANES], dx[..., LANES:]
        dv_ref[...] = (dxu * beta).reshape(rows, LANES)
        dbeta = jnp.sum(dxu * v3, axis=-1, keepdims=True)
        dkb = dxw * eg
        dgam = jnp.sum(dxw * kb * eg, axis=-1, keepdims=True)
        dkb = dkb + _bnn((da * lmat).astype(BF16), kbf)
        dk = _bnn((dat * lt).astype(BF16), kbb)
        dgam = dgam + jnp.sum(da * tm_["A"], axis=-1, keepdims=True) - jnp.sum(dat * tm_["AT"], axis=-1, keepdims=True)
        daq = daqk_ref[0].reshape(nb, cc, cc)
        daqt = daqkt_ref[0].reshape(nb, cc, cc)
        dq = _bnn((daq * lmat).astype(BF16), kbf)
        dk = dk + _bnn((daqt * lt).astype(BF16), qb)
        dgam = dgam + jnp.sum(daq * tm_["Aqk"], axis=-1, keepdims=True) - jnp.sum(daqt * tm_["AqkT"], axis=-1, keepdims=True)
        dqd3, dkd3 = r3(dqd_ref), r3(dkd_ref)
        dq = dq + dqd3 * eg
        dgam = dgam + jnp.sum(dqd3 * qq * eg, axis=-1, keepdims=True)
        dk = dk + dkd3 * ekd
        tk = jnp.sum(dkd3 * kk * ekd, axis=-1, keepdims=True)
        dgam = dgam - tk
        dcdv = dcd_ref[0].reshape(nb, 8, LANES)[:, 0:1, 0:1]
        dglast = jnp.sum(tk, axis=1, keepdims=True) + dcdv * jnp.exp(gl[:, :, 0:1])
        rowi = lax.broadcasted_iota(jnp.int32, (nb, cc, 1), 1)
        dgam = dgam + jnp.where(rowi == cc - 1, dglast, 0.0)
        dk = dk + dkb * beta
        dbeta = dbeta + jnp.sum(dkb * kk, axis=-1, keepdims=True)
        dg = _chunk_cumsum(dgam, ci >= ri)
        dq_ref[...] = dq.reshape(rows, LANES)
        dk_ref[...] = dk.reshape(rows, LANES)
        lane = lax.broadcasted_iota(jnp.int32, (rows, LANES), 1)
        dgates_ref[...] += (jnp.where(lane == h, dbeta.reshape(rows, 1), 0.0)
                            + jnp.where(lane == GDN_HEADS + h, dg.reshape(rows, 1), 0.0))

    blk = pl.BlockSpec((rows, LANES), lambda i, h: (i, h))
    shared = pl.BlockSpec((rows, LANES), lambda i, h: (i, 0))
    sq = pl.BlockSpec((1, rows, cc), lambda i, h: (h, i, 0))
    return pl.pallas_call(
        body, name="gdn_intra_bwd", grid=(t // rows, GDN_HEADS),
        in_specs=[blk, blk, blk, shared, blk, blk, blk, blk, blk, blk, sq, sq,
                  pl.BlockSpec((1, nb * 8, LANES), lambda i, h: (h, i, 0))],
        out_specs=[blk, blk, blk, shared],
        out_shape=[jax.ShapeDtypeStruct((t, D_MODEL), F32)] * 3 + [jax.ShapeDtypeStruct((t, LANES), F32)],
        compiler_params=_params(2),
    )(q, k, v, gates, u, w, du, dw, dqd, dkd, daqk, daqkt, dcd)


def _gdn_conv_bwd_a(x, cw, ba, gp, dq, dk, dv, dgates):
    t = x.shape[0]
    tm = _row_tile(t, GDN_CONV_TM)

    def body(x_ref, halo_ref, cw_ref, ba_ref, gp_ref, dq_ref, dk_ref, dv_ref, dgates_ref,
             dc_ref, dba_ref, dcw_ref, dgp_ref, ds_ref):
        @pl.when(pl.program_id(0) == 0)
        def _():
            dcw_ref[...] = jnp.zeros_like(dcw_ref)
            dgp_ref[...] = jnp.zeros_like(dgp_ref)

        halo = jnp.where(pl.program_id(0) > 0, halo_ref[...], 0.0)
        c, taps = _conv_taps(x_ref[...], halo, cw_ref[...])
        sig = _sigmoid(c)
        s = c * sig
        for hh in range(GDN_HEADS):
            sl = slice(hh * LANES, (hh + 1) * LANES)
            sq = s[:, sl]
            rq = lax.rsqrt(jnp.sum(sq * sq, axis=-1, keepdims=True) + NORM_EPS)
            qh = sq * rq
            dqv = dq_ref[:, sl]
            ds_ref[:, sl] = GDN_QSCALE * rq * (dqv - qh * jnp.sum(dqv * qh, axis=-1, keepdims=True))
            sl2 = slice(D_MODEL + hh * LANES, D_MODEL + (hh + 1) * LANES)
            sk = s[:, sl2]
            rk = lax.rsqrt(jnp.sum(sk * sk, axis=-1, keepdims=True) + NORM_EPS)
            kh = sk * rk
            dkv = dk_ref[:, sl]
            ds_ref[:, sl2] = rk * (dkv - kh * jnp.sum(dkv * kh, axis=-1, keepdims=True))
        ds_ref[:, 2 * D_MODEL:] = dv_ref[...]
        dc = ds_ref[...] * (sig * (1.0 + c * (1.0 - sig)))
        dc_ref[...] = dc
        for i in range(GDN_CONV):
            dcw_ref[i:i + 1, :] += jnp.sum(dc * taps[i], axis=0, keepdims=True)
        lane, beta, pre, g, _ = _gates(ba_ref[...], gp_ref[...])
        dgt = dgates_ref[...]
        db = dgt * beta * (1.0 - beta)
        dpre = dgt * (-jnp.exp(gp_ref[0:1, :])) * _sigmoid(pre)
        isa = jnp.logical_and(lane >= GDN_HEADS, lane < 2 * GDN_HEADS)
        dba_ref[...] = jnp.where(lane < GDN_HEADS, db, jnp.where(isa, dpre, 0.0)).astype(BF16)
        dgp_ref[0:1, :] += jnp.sum(jnp.where(isa, dgt * g, 0.0), axis=0, keepdims=True)
        dgp_ref[1:2, :] += jnp.sum(jnp.where(isa, dpre, 0.0), axis=0, keepdims=True)

    row = lambda w: pl.BlockSpec((tm, w), lambda i: (i, 0))
    full = lambda a: pl.BlockSpec(a.shape, lambda i: (0, 0))
    halo = pl.BlockSpec((8, GDN_CONV_W), lambda i: (jnp.maximum(i * (tm // 8) - 1, 0), 0))
    return pl.pallas_call(
        body, name="gdn_conv_bwd_a", grid=(t // tm,),
        in_specs=[row(GDN_CONV_W), halo, full(cw), row(LANES), full(gp), row(D_MODEL), row(D_MODEL),
                  row(D_MODEL), row(LANES)],
        out_specs=[row(GDN_CONV_W), row(LANES), pl.BlockSpec((8, GDN_CONV_W), lambda i: (0, 0)),
                   pl.BlockSpec((8, LANES), lambda i: (0, 0))],
        out_shape=[jax.ShapeDtypeStruct((t, GDN_CONV_W), F32),
                   jax.ShapeDtypeStruct((t, LANES), BF16),
                   jax.ShapeDtypeStruct((8, GDN_CONV_W), F32),
                   jax.ShapeDtypeStruct((8, LANES), F32)],
        scratch_shapes=[pltpu.VMEM((tm, GDN_CONV_W), F32)],
        compiler_params=_params(1),
    )(x, x, cw, ba, gp, dq, dk, dv, dgates)


def _gdn_conv_bwd_b(dc, cw):
    t = dc.shape[0]
    tm = _row_tile(t, GDN_CONV_TM)
    nsteps = t // tm

    def body(dc_ref, halo_ref, cw_ref, dx_ref):
        halo = jnp.where(pl.program_id(0) < nsteps - 1, halo_ref[...], 0.0)
        dd = jnp.concatenate([dc_ref[...], halo], axis=0)
        cw_v = cw_ref[...]
        acc = dd[GDN_CONV - 1:GDN_CONV - 1 + tm] * cw_v[0:1]
        for i in range(1, GDN_CONV):
            acc = acc + dd[GDN_CONV - 1 - i:GDN_CONV - 1 - i + tm] * cw_v[i:i + 1]
        dx_ref[...] = acc.astype(BF16)

    row = pl.BlockSpec((tm, GDN_CONV_W), lambda i: (i, 0))
    halo = pl.BlockSpec((8, GDN_CONV_W), lambda i: (jnp.minimum((i + 1) * (tm // 8), t // 8 - 1), 0))
    return pl.pallas_call(
        body, name="gdn_conv_bwd_b", grid=(nsteps,),
        in_specs=[row, halo, pl.BlockSpec(cw.shape, lambda i: (0, 0))],
        out_specs=row,
        out_shape=jax.ShapeDtypeStruct((t, GDN_CONV_W), BF16),
        compiler_params=_params(1),
    )(dc, dc, cw)


def _gdn_proj_bwd(dy, h, nw, w_qkv, w_z, w_ba, dx, dz, dba):
    t = h.shape[0]
    tm = _row_tile(t, GDN_TM)

    def body(dy_ref, h_ref, nw_ref, wq_ref, wz_ref, wb_ref, dx_ref, dz_ref, dba_ref, dh_ref, dnw_ref):
        @pl.when(pl.program_id(0) == 0)
        def _():
            dnw_ref[...] = jnp.zeros_like(dnw_ref)

        dhn = _nt(dx_ref[...], wq_ref[...]) + _nt(dz_ref[...], wz_ref[...]) + _nt(dba_ref[...], wb_ref[...])
        dxx, dnw = _norm_bwd(h_ref[...], nw_ref[...], dhn)
        dh_ref[...] = dy_ref[...] + dxx
        dnw_ref[...] += dnw

    row = lambda w: pl.BlockSpec((tm, w), lambda i: (i, 0))
    full = lambda a: pl.BlockSpec(a.shape, lambda i: (0, 0))
    return pl.pallas_call(
        body, name="gdn_proj_bwd", grid=(t // tm,),
        in_specs=[row(D_MODEL), row(D_MODEL), full(nw), full(w_qkv), full(w_z), full(w_ba),
                  row(GDN_CONV_W), row(D_MODEL), row(LANES)],
        out_specs=[row(D_MODEL), pl.BlockSpec((1, D_MODEL), lambda i: (0, 0))],
        out_shape=[jax.ShapeDtypeStruct((t, D_MODEL), F32), jax.ShapeDtypeStruct((1, D_MODEL), F32)],
        compiler_params=_params(1),
    )(dy, h, nw, w_qkv, w_z, w_ba, dx, dz, dba)


def _loss_head(h, nw, target):
    t = h.shape[0]
    tm = _row_tile(t, 512)

    def body(h_ref, nw_ref, t_ref, loss_ref, dh_ref, dnw_ref):
        @pl.when(pl.program_id(0) == 0)
        def _():
            loss_ref[...] = jnp.zeros_like(loss_ref)
            dnw_ref[...] = jnp.zeros_like(dnw_ref)

        x = h_ref[...]
        nwv = nw_ref[...]
        _, xhat = _rms_stats(x)
        err = xhat * nwv - t_ref[...]
        loss_ref[...] += 0.5 * jnp.sum(jnp.mean(err * err, axis=-1, keepdims=True))
        dx, dnw = _norm_bwd(x, nwv, err * (1.0 / D_MODEL))
        dh_ref[...] = dx
        dnw_ref[...] += dnw

    row = pl.BlockSpec((tm, D_MODEL), lambda i: (i, 0))
    return pl.pallas_call(
        body, name="loss_head", grid=(t // tm,),
        in_specs=[row, pl.BlockSpec((1, D_MODEL), lambda i: (0, 0)), row],
        out_specs=[pl.BlockSpec((8, LANES), lambda i: (0, 0)), row, pl.BlockSpec((1, D_MODEL), lambda i: (0, 0))],
        out_shape=[jax.ShapeDtypeStruct((8, LANES), F32),
                   jax.ShapeDtypeStruct((t, D_MODEL), F32),
                   jax.ShapeDtypeStruct((1, D_MODEL), F32)],
        compiler_params=_params(1),
    )(h, nw, target)


_PEER_FLIPS = [(dx, dy, dc) for dx in (0, 1) for dy in (0, 1) for dc in (0, 1)][1:]


_ANY = pl.BlockSpec(memory_space=pl.ANY)


def _exchange_copies(ins, outs, send_sems, recv_sems, local_sems, scatter):
    x, y, c = lax.axis_index("x"), lax.axis_index("y"), lax.axis_index("c")
    me = 4 * x + 2 * y + c
    copies = []
    for a in range(len(ins)):
        src = ins[a].at[me] if scatter else ins[a]
        copies.append(pltpu.make_async_copy(src, outs[a].at[me], local_sems.at[a]))
    for k, (fx, fy, fc) in enumerate(_PEER_FLIPS):
        px, py, pc = lax.rem(x + fx, 2), lax.rem(y + fy, 2), lax.rem(c + fc, 2)
        peer = 4 * px + 2 * py + pc
        for a in range(len(ins)):
            copies.append(pltpu.make_async_remote_copy(
                src_ref=ins[a].at[peer] if scatter else ins[a],
                dst_ref=outs[a].at[me],
                send_sem=send_sems.at[a, k], recv_sem=recv_sems.at[a, k],
                device_id=(px, py, pc), device_id_type=MESH_ID))
    return copies


def _exchange_shapes(arrs, scatter):
    return [jax.ShapeDtypeStruct((N_DEV,) + (a.shape[1:] if scatter else a.shape), a.dtype) for a in arrs]


def _exchange_sems(n):
    npeer = len(_PEER_FLIPS)
    return [pltpu.SemaphoreType.DMA((n, npeer)), pltpu.SemaphoreType.DMA((n, npeer)),
            pltpu.SemaphoreType.DMA((n,))]


def _exchange(arrs, scatter, name):
    n = len(arrs)

    def body(*refs):
        copies = _exchange_copies(refs[:n], refs[n:2 * n], *refs[2 * n:], scatter)
        for cp in copies:
            cp.start()
        for cp in copies:
            cp.wait()

    return pl.pallas_call(
        body, name=name, in_specs=[_ANY] * n, out_specs=[_ANY] * n,
        out_shape=_exchange_shapes(arrs, scatter), scratch_shapes=_exchange_sems(n),
    )(*arrs)


def _adamw(parts, w, m, v, name):
    r, c = w.shape
    tr = r
    for cand in (512, 256, 128, 64, 32, 16, 8):
        if r % cand == 0:
            tr = cand
            break
    c1 = 1.0 - ADAM_B1 ** ADAM_STEP
    c2 = 1.0 - ADAM_B2 ** ADAM_STEP

    def body(p_ref, w_ref, m_ref, v_ref, g_ref, d_ref, nm_ref, nv_ref):
        g = p_ref[0].astype(F32)
        for s in range(1, N_DEV):
            g = g + p_ref[s].astype(F32)
        nm = ADAM_B1 * m_ref[...] + (1.0 - ADAM_B1) * g
        nv = ADAM_B2 * v_ref[...] + (1.0 - ADAM_B2) * (g * g)
        g_ref[...] = g
        nm_ref[...] = nm
        nv_ref[...] = nv
        d_ref[...] = -ADAM_LR * ((nm / c1) / (jnp.sqrt(nv / c2) + ADAM_EPS) + ADAM_WD * w_ref[...])

    blk = pl.BlockSpec((tr, c), lambda i: (i, 0))
    return pl.pallas_call(
        body, name=name, grid=(r // tr,),
        in_specs=[pl.BlockSpec((N_DEV, tr, c), lambda i: (0, i, 0)), blk, blk, blk],
        out_specs=[blk] * 4,
        out_shape=[jax.ShapeDtypeStruct((r, c), F32)] * 4,
        compiler_params=_params(1),
    )(parts, w, m, v)


def _rope_tables(positions):
    half = ROPE_DIM // 2
    inv_freq = ROPE_THETA ** (-jnp.arange(0, ROPE_DIM, 2, dtype=F32) / ROPE_DIM)
    ang = positions.astype(F32)[:, None] * inv_freq
    cos, sin = jnp.cos(ang), jnp.sin(ang)
    t = positions.shape[0]
    zeros = lambda w: jnp.zeros((t, w), F32)
    c = jnp.concatenate([cos, cos, jnp.ones((t, ATTN_HEAD_DIM - ROPE_DIM), F32)], axis=1)
    s1 = jnp.concatenate([-sin, zeros(ATTN_HEAD_DIM - half)], axis=1)
    s2 = jnp.concatenate([zeros(half), sin, zeros(ATTN_HEAD_DIM - ROPE_DIM)], axis=1)
    return tuple(jnp.tile(a, (1, LANES // ATTN_HEAD_DIM)) for a in (c, s1, s2))


def _ffn_layer_fwd(h, nw, w, carry):
    (out, g, u, hn), landed = _ffn_fwd(h, nw, w["wg"], w["wu"], w["wd"], carry, False)
    return out, (h, g, u, hn), landed


def _ffn_layer_bwd(dy, saved, nw, w, carry):
    h, g, u, hn = saved
    (dh, dg, du, act, dz, dnw), landed = _ffn_bwd(dy, h, nw, g, u, w["wg"], w["wu"], w["wd"], carry, True)
    dwg = _matmul_tn(hn, dg, FFN_TF, "ffn_dwg")
    dwu = _matmul_tn(hn, du, FFN_TF, "ffn_dwu")
    dwd = _matmul_tn(act, dz, D_MODEL, "ffn_dwd")
    return dh, dnw, (dwg, dwu, dwd), landed


def _attn_layer_fwd(h, nw, w, ropes):
    q, k, v, hn = _attn_qkv_fwd(h, nw, w["w_in"], w["b_in"], *ropes)
    o = _attn_core_fwd(q, k, v, w["sinks"])
    out = _proj_out_fwd(o, w["w_out"], w["b_out"], h)
    return out, (h, hn, q, k, v, o)


def _attn_layer_bwd(dy, saved, nw, w, ropes):
    h, hn, q, k, v, o = saved
    do, dyb, db_out = _proj_out_bwd(dy, w["w_out"])
    dw_out = _matmul_tn(o, dyb, D_MODEL, "attn_dw_out")
    dq, dk, dv, dsink = _attn_core_bwd(q, k, v, do, w["sinks"])
    dh, dqkv, db_in, dnw = _attn_qkv_bwd(dy, h, nw, w["w_in"], dq, dk, dv, *ropes)
    dw_in = _matmul_tn(hn, dqkv, ATTN_IN, "attn_dw_in")
    return dh, dnw, dict(w_in=dw_in, b_in=db_in, sinks=dsink[:, 0], w_out=dw_out, b_out=db_out)


def _gdn_layer_fwd(h, nw, w):
    x, z, ba, hn = _gdn_proj_fwd(h, nw, w["w_qkv"], w["w_z"], w["w_ba"])
    q, k, v, gates = _gdn_conv_fwd(x, w["conv_w"], ba, w["gp"])
    u, ww, qd, kd, aqk, cd = _gdn_intra_fwd(q, k, v, gates)
    o, st = _gdn_scan_fwd(u, ww, qd, kd, aqk, cd)
    out, gated = _gdn_out_fwd(o, z, w["norm_w"], w["w_out"], h)
    return out, (h, hn, x, z, ba, q, k, v, gates, u, ww, qd, kd, aqk, cd, st, o, gated)


def _gdn_layer_bwd(dy, saved, nw, w):
    h, hn, x, z, ba, q, k, v, gates, u, ww, qd, kd, aqk, cd, st, o, gated = saved
    do, dz, dyb, dnorm_w = _gdn_out_bwd(dy, w["w_out"], o, z, w["norm_w"])
    dw_out = _matmul_tn(gated, dyb, D_MODEL, "gdn_dw_out")
    du, dw, dqd, dkd, daqk, daqkt, dcd = _gdn_scan_bwd(u, ww, qd, kd, aqk, cd, st, do)
    dq, dk, dv, dgates = _gdn_intra_bwd(q, k, v, gates, u, ww, du, dw, dqd, dkd, daqk, daqkt, dcd)
    dc, dba, dcw, dgp = _gdn_conv_bwd_a(x, w["conv_w"], ba, w["gp"], dq, dk, dv, dgates)
    dx = _gdn_conv_bwd_b(dc, w["conv_w"])
    dh, dnw = _gdn_proj_bwd(dy, h, nw, w["w_qkv"], w["w_z"], w["w_ba"], dx, dz, dba)
    dw_qkv = _matmul_tn(hn, dx, GDN_CONV_W // 2, "gdn_dw_qkv")
    dw_z = _matmul_tn(hn, dz, D_MODEL, "gdn_dw_z")
    dw_ba = _matmul_tn(hn, dba, LANES, "gdn_dw_ba")
    dw_in = jnp.concatenate([dw_qkv, dw_z, dw_ba[:, :2 * GDN_HEADS]], axis=1)
    lo, hi = GDN_HEADS, 2 * GDN_HEADS
    return dh, dnw, dict(w_in=dw_in, conv_w=dcw[:GDN_CONV], A_log=dgp[0, lo:hi], dt_bias=dgp[1, lo:hi],
                         norm_w=dnorm_w[0], w_out=dw_out)


def _local_step(x, positions, target, norms, final_norm, plan):
    ropes = _rope_tables(positions)
    h = x
    saved = []
    for layer in range(DEPTH):
        h, s1, landed = _ffn_layer_fwd(h, norms["ffn1"][layer], plan.weights("ffn1", layer),
                                       plan.fwd_carry("ffn1", layer))
        plan.fwd_landed(landed)
        if layer % 2 == 0:
            h, s2 = _attn_layer_fwd(h, norms["mix"][layer], plan.weights("mix", layer), ropes)
        else:
            h, s2 = _gdn_layer_fwd(h, norms["mix"][layer], plan.weights("mix", layer))
        h, s3, landed = _ffn_layer_fwd(h, norms["ffn2"][layer], plan.weights("ffn2", layer),
                                       plan.fwd_carry("ffn2", layer))
        plan.fwd_landed(landed)
        saved.append((s1, s2, s3))
    loss, dh, dfinal = _loss_head(h, final_norm, target)

    g_norm = {k: [None] * DEPTH for k in ("ffn1", "mix", "ffn2")}
    for layer in reversed(range(DEPTH)):
        s1, s2, s3 = saved[layer]
        dh, g_norm["ffn2"][layer], gw, landed = _ffn_layer_bwd(
            dh, s3, norms["ffn2"][layer], plan.weights("ffn2", layer), plan.bwd_carry("ffn2", layer))
        plan.bwd_landed(landed)
        plan.grads("ffn2", layer, gw)
        if layer % 2 == 0:
            dh, g_norm["mix"][layer], gw = _attn_layer_bwd(dh, s2, norms["mix"][layer], plan.weights("mix", layer),
                                                           ropes)
        else:
            dh, g_norm["mix"][layer], gw = _gdn_layer_bwd(dh, s2, norms["mix"][layer], plan.weights("mix", layer))
        plan.grads("mix", layer, gw)
        dh, g_norm["ffn1"][layer], gw, landed = _ffn_layer_bwd(
            dh, s1, norms["ffn1"][layer], plan.weights("ffn1", layer), plan.bwd_carry("ffn1", layer))
        plan.bwd_landed(landed)
        plan.grads("ffn1", layer, gw)
    return loss, dh, dfinal, g_norm


def _cols_full(g):
    return jnp.transpose(g, (1, 0, 2)).reshape(g.shape[1], -1)


def _rows_full(g):
    return g.reshape(-1, g.shape[-1])


class _ShardedWeights:
    def __init__(self, shards, small):
        self.shards, self.small = shards, small
        self.whole, self.pending, self.received, self.small_grads = {}, {}, {}, {}
        self.in_flight = []
        first = [("ffn1", 0)]
        self.in_flight = first
        self.fwd_landed(_exchange(self._shards_of(first), False, "gather_first"))

    def _group(self, key):
        kind, layer = key
        j = layer // 2
        if kind != "mix":
            return [self.shards[kind + "_w_gate_up"][layer], self.shards[kind + "_w_down"][layer]]
        if layer % 2 == 0:
            return [self.shards["attn_w_in"][j], self.shards["attn_w_out"][j]]
        return [self.shards["gdn_w_in"][j], self.shards["gdn_w_out"][j], self.shards["gdn_conv_w"][j]]

    def _shards_of(self, keys):
        return [a for key in keys for a in self._group(key)]

    def weights(self, kind, layer):
        return self.whole[(kind, layer)]

    def fwd_carry(self, kind, layer):
        if kind == "ffn1":
            keys = [("mix", 0), ("ffn2", 0)] if layer == 0 else [("ffn2", layer)]
        else:
            keys = [("ffn1", layer + 1), ("mix", layer + 1)] if layer + 1 < DEPTH else []
        self.in_flight = keys
        return self._shards_of(keys)

    def fwd_landed(self, landed):
        landed = list(landed)
        for key in self.in_flight:
            kind, layer = key
            j = layer // 2
            sm = self.small
            if kind != "mix":
                gu, dn = landed[:2]
                w = dict(wg=_cols_full(gu[:4]), wu=_cols_full(gu[4:]), wd=_rows_full(dn))
                landed = landed[2:]
            elif layer % 2 == 0:
                w = dict(w_in=_cols_full(landed[0]), w_out=_rows_full(landed[1]), b_in=sm["attn_b_in"][j][None],
                         sinks=sm["attn_sinks"][j][None], b_out=sm["attn_b_out"][j][None])
                landed = landed[2:]
            else:
                w_in = _cols_full(landed[0])
                w_ba = jnp.pad(w_in[:, GDN_CONV_W + D_MODEL:], ((0, 0), (0, LANES - 2 * GDN_HEADS)))
                gp = jnp.pad(jnp.stack([sm["gdn_A_log"][j], sm["gdn_dt_bias"][j]]),
                             ((0, 6), (GDN_HEADS, LANES - 2 * GDN_HEADS)))
                w = dict(w_qkv=w_in[:, :GDN_CONV_W], w_z=w_in[:, GDN_CONV_W:GDN_CONV_W + D_MODEL], w_ba=w_ba,
                         conv_w=_cols_full(landed[2]), gp=gp, norm_w=sm["gdn_norm_w"][j][None],
                         w_out=_rows_full(landed[1]))
                landed = landed[3:]
            self.whole[key] = w
        self.in_flight = []

    def grads(self, kind, layer, g):
        if kind != "mix":
            dwg, dwu, dwd = g
            parts = [jnp.concatenate([_cols_shards(dwg, 4), _cols_shards(dwu, 4)], axis=0).astype(BF16),
                     _rows_shards(dwd).astype(BF16)]
        else:
            parts = [_cols_shards(g["w_in"]).astype(BF16), _rows_shards(g["w_out"]).astype(BF16)]
            if layer % 2 == 1:
                parts.append(_cols_shards(g["conv_w"]))
            self.small_grads[layer] = g
        self.pending[(kind, layer)] = parts

    def bwd_carry(self, kind, layer):
        if kind == "ffn1":
            keys = [("ffn2", layer), ("mix", layer)]
        else:
            keys = [("ffn1", layer + 1)] if layer + 1 < DEPTH else []
        self.in_flight = keys
        return [a for key in keys for a in self.pending[key]]

    def bwd_landed(self, landed):
        landed = list(landed)
        for key in self.in_flight:
            n = len(self.pending.pop(key))
            self.received[key], landed = landed[:n], landed[n:]
        self.in_flight = []

    def finish(self):
        self.in_flight = list(self.pending)
        self.bwd_landed(_exchange([a for key in self.in_flight for a in self.pending[key]], True, "scatter_last"))


def _cols_shards(full, n=N_DEV):
    r = full.shape[0]
    return jnp.transpose(full.reshape(r, n, -1), (1, 0, 2))


def _rows_shards(full):
    return full.reshape(N_DEV, -1, full.shape[-1])


_SMALL = ("ffn1_norm", "mix_norm", "ffn2_norm", "attn_b_in", "attn_sinks", "attn_b_out",
          "gdn_A_log", "gdn_dt_bias", "gdn_norm_w", "final_norm")
_BIG = ("ffn1_w_gate_up", "ffn1_w_down", "ffn2_w_gate_up", "ffn2_w_down", "attn_w_in", "attn_w_out",
        "gdn_w_in", "gdn_conv_w", "gdn_w_out")
_WEIGHTS = ("ffn1_norm", "ffn1_w_gate_up", "ffn1_w_down", "mix_norm", "ffn2_norm", "ffn2_w_gate_up",
            "ffn2_w_down", "attn_w_in", "attn_b_in", "attn_sinks", "attn_w_out", "attn_b_out", "gdn_w_in",
            "gdn_conv_w", "gdn_A_log", "gdn_dt_bias", "gdn_norm_w", "gdn_w_out", "final_norm")


def _pack_small(vals):
    flat = jnp.concatenate([v.reshape(-1).astype(F32) for v in vals])
    pad = (-flat.shape[0]) % (8 * LANES)
    return jnp.pad(flat, (0, pad)).reshape(-1, LANES)


def _unpack_small(packed, shapes):
    flat = packed.reshape(-1)
    out, off = [], 0
    for s in shapes:
        size = 1
        for d in s:
            size *= d
        out.append(flat[off:off + size].reshape(s))
        off += size
    return out


def kernel(x, positions, ffn1_norm, ffn1_w_gate_up, ffn1_w_down, mix_norm, ffn2_norm, ffn2_w_gate_up, ffn2_w_down, attn_w_in, attn_b_in, attn_sinks, attn_w_out, attn_b_out, gdn_w_in, gdn_conv_w, gdn_A_log, gdn_dt_bias, gdn_norm_w, gdn_w_out, final_norm, loss_target, m_ffn1_norm, m_ffn1_w_gate_up, m_ffn1_w_down, m_mix_norm, m_ffn2_norm, m_ffn2_w_gate_up, m_ffn2_w_down, m_attn_w_in, m_attn_b_in, m_attn_sinks, m_attn_w_out, m_attn_b_out, m_gdn_w_in, m_gdn_conv_w, m_gdn_A_log, m_gdn_dt_bias, m_gdn_norm_w, m_gdn_w_out, m_final_norm, v_ffn1_norm, v_ffn1_w_gate_up, v_ffn1_w_down, v_mix_norm, v_ffn2_norm, v_ffn2_w_gate_up, v_ffn2_w_down, v_attn_w_in, v_attn_b_in, v_attn_sinks, v_attn_w_out, v_attn_b_out, v_gdn_w_in, v_gdn_conv_w, v_gdn_A_log, v_gdn_dt_bias, v_gdn_norm_w, v_gdn_w_out, v_final_norm):
    args = dict(locals())
    wts = {n: args[n] for n in _WEIGHTS}
    moms = {n: args["m_" + n] for n in _WEIGHTS}
    vels = {n: args["v_" + n] for n in _WEIGHTS}

    shards = {n: wts[n] if n == "gdn_conv_w" else wts[n].astype(BF16) for n in _BIG}
    plan = _ShardedWeights(shards, wts)
    norms = dict(ffn1=ffn1_norm[:, None, :], mix=mix_norm[:, None, :], ffn2=ffn2_norm[:, None, :])
    loss, grad_x, dfinal, g_norm = _local_step(x[0], positions[0], loss_target[0], norms, final_norm[None], plan)
    plan.finish()

    layers = range(DEPTH)
    got = lambda kind, ls, i: jnp.stack([plan.received[(kind, l)][i] for l in ls], axis=1)
    received = {"attn_w_in": got("mix", layers[0::2], 0), "attn_w_out": got("mix", layers[0::2], 1),
                "gdn_w_in": got("mix", layers[1::2], 0), "gdn_w_out": got("mix", layers[1::2], 1),
                "gdn_conv_w": got("mix", layers[1::2], 2)}
    for kind in ("ffn1", "ffn2"):
        received[kind + "_w_gate_up"] = got(kind, layers, 0)
        received[kind + "_w_down"] = got(kind, layers, 1)
    g_attn = [plan.small_grads[l] for l in layers[0::2]]
    g_gdn = [plan.small_grads[l] for l in layers[1::2]]


    small_g = dict(
        ffn1_norm=jnp.concatenate(g_norm["ffn1"], axis=0), mix_norm=jnp.concatenate(g_norm["mix"], axis=0),
        ffn2_norm=jnp.concatenate(g_norm["ffn2"], axis=0),
        attn_b_in=jnp.concatenate([g["b_in"] for g in g_attn], axis=0),
        attn_sinks=jnp.stack([g["sinks"] for g in g_attn]),
        attn_b_out=jnp.concatenate([g["b_out"] for g in g_attn], axis=0),
        gdn_A_log=jnp.stack([g["A_log"] for g in g_gdn]), gdn_dt_bias=jnp.stack([g["dt_bias"] for g in g_gdn]),
        gdn_norm_w=jnp.stack([g["norm_w"] for g in g_gdn]), final_norm=dfinal[0])
    small_parts = _exchange([_pack_small([small_g[n] for n in _SMALL] + [loss[0, :1]])], False, "gather_small")[0]
    pad1 = jnp.zeros((1,), F32)
    sw = _pack_small([wts[n] for n in _SMALL] + [pad1])
    sm = _pack_small([moms[n] for n in _SMALL] + [pad1])
    sv = _pack_small([vels[n] for n in _SMALL] + [pad1])
    shapes = [wts[n].shape for n in _SMALL] + [(1,)]
    small_out = [_unpack_small(o, shapes) for o in _adamw(small_parts, sw, sm, sv, "adamw_small")]
    results = {n: tuple(o[i] for o in small_out) for i, n in enumerate(_SMALL)}
    loss_total = small_out[0][-1][0]

    for n in _BIG:
        shape = wts[n].shape
        two_d = lambda a: a.reshape(-1, shape[-1])
        p = received[n].reshape(N_DEV, -1, shape[-1])
        outs = _adamw(p, two_d(wts[n]), two_d(moms[n]), two_d(vels[n]), "adamw_" + n)
        results[n] = tuple(o.reshape(shape) for o in outs)

    return (loss_total, grad_x[None],
            *[results[n][0] for n in _WEIGHTS], *[results[n][1] for n in _WEIGHTS],
            *[results[n][2] for n in _WEIGHTS], *[results[n][3] for n in _WEIGHTS])
```

```python
import jax
import jax.numpy as jnp
from jax import lax
from jax.experimental import pallas as pl
from jax.experimental.pallas import tpu as pltpu

F32 = jnp.float32
BF16 = jnp.bfloat16

D_MODEL = 1024
DEPTH = 4
D_FF = 2816
NORM_EPS = 1e-6
N_DEV = 8

ATTN_Q_HEADS = 16
ATTN_KV_HEADS = 4
ATTN_HEAD_DIM = 64
ATTN_GROUP = 4
ATTN_BLOCK = 128
ROPE_DIM = 16
ROPE_THETA = 500000.0
ATTN_Q_W = 1024
ATTN_KV_W = 256
ATTN_IN = 1536
ATTN_SCALE = ATTN_HEAD_DIM ** -0.5

GDN_HEADS = 8
GDN_DK = 128
GDN_CONV = 4
GDN_CHUNK = 64
GDN_CONV_W = 3072
GDN_IN = 4112
GDN_QSCALE = GDN_DK ** -0.5
GDN_ROWS = 512
GDN_INTRA_HEADS = 2
GDN_SCAN_HEADS = 2
GDN_SUB_SHIFT = 4

ADAM_LR = 0.001
ADAM_B1 = 0.9
ADAM_B2 = 0.999
ADAM_EPS = 1e-08
ADAM_WD = 0.01
ADAM_STEP = 10

LANES = 128
NEG_BIG = -1e30
VMEM_LIMIT_BYTES = 56 * 1024 * 1024
MESH_ID = pl.DeviceIdType.MESH


def _params(n_axes, vmem_limit_bytes=VMEM_LIMIT_BYTES):
    return pltpu.CompilerParams(dimension_semantics=("arbitrary",) * n_axes,
                                vmem_limit_bytes=vmem_limit_bytes)


def _nn(a, b):
    return jnp.dot(a, b, preferred_element_type=F32)


def _nt(a, b):
    return lax.dot_general(a, b, (((1,), (1,)), ((), ())), preferred_element_type=F32)


def _tn(a, b):
    return lax.dot_general(a, b, (((0,), (0,)), ((), ())), preferred_element_type=F32)


def _bnn(a, b, precision=None):
    return lax.dot_general(a, b, (((2,), (1,)), ((0,), (0,))), precision=precision,
                           preferred_element_type=F32)


def _bnt(a, b, precision=None):
    return lax.dot_general(a, b, (((2,), (2,)), ((0,), (0,))), precision=precision,
                           preferred_element_type=F32)


def _sigmoid(x):
    return 1.0 / (1.0 + jnp.exp(-x))


def _rms_stats(x):
    r = lax.rsqrt(jnp.mean(x * x, axis=-1, keepdims=True) + NORM_EPS)
    return r, x * r


def _norm_bwd(x, nw, dhn):
    r, xhat = _rms_stats(x)
    dxh = dhn * nw
    dx = r * (dxh - xhat * jnp.mean(dxh * xhat, axis=-1, keepdims=True))
    dnw = jnp.sum(dhn * xhat, axis=0, keepdims=True)
    return dx, dnw


def _row_tile(t, pref):
    return min(t, pref)


FFN_TM = 512
FFN_BWD_TM = 512
FFN_BWD_TF = 1408
FFN_BWD_VMEM_LIMIT_BYTES = 61 * 1024 * 1024
FFN_TF = 1408


def _carried(body, n_in, n_out, carry, scatter, last_step):
    nc = len(carry)
    if not nc:
        return body

    def wrapped(*refs):
        ins, cin = refs[:n_in], refs[n_in:n_in + nc]
        outs = refs[n_in + nc:n_in + nc + n_out]
        cout = refs[n_in + nc + n_out:n_in + 2 * nc + n_out]
        scratch = refs[n_in + 2 * nc + n_out:]
        sems = scratch[len(scratch) - 3:]
        ids = [pl.program_id(ax) for ax in range(len(last_step))]
        first, last = ids[0] == 0, ids[0] == last_step[0]
        for ax in range(1, len(last_step)):
            first = jnp.logical_and(first, ids[ax] == 0)
            last = jnp.logical_and(last, ids[ax] == last_step[ax])

        @pl.when(first)
        def _():
            for cp in _exchange_copies(cin, cout, *sems, scatter):
                cp.start()

        body(*ins, *outs, *scratch[:len(scratch) - 3])

        @pl.when(last)
        def _():
            for cp in _exchange_copies(cin, cout, *sems, scatter):
                cp.wait()

    return wrapped


def _ffn_fwd(h, nw, wg, wu, wd, carry=(), scatter=False):
    t = h.shape[0]
    tm, tf = _row_tile(t, FFN_TM), FFN_TF
    nj = D_FF // tf
    nc = len(carry)

    def body(h_ref, nw_ref, wg_ref, wu_ref, wd_ref, out_ref, g_ref, u_ref, hn_ref, acc_ref):
        j = pl.program_id(1)

        @pl.when(j == 0)
        def _():
            _, xhat = _rms_stats(h_ref[...])
            hn_ref[...] = (xhat * nw_ref[...]).astype(BF16)
            acc_ref[...] = jnp.zeros_like(acc_ref)

        hn = hn_ref[...]
        g = _nn(hn, wg_ref[...])
        u = _nn(hn, wu_ref[...])
        g_ref[...] = g.astype(BF16)
        u_ref[...] = u.astype(BF16)
        a = (g * _sigmoid(g) * u).astype(BF16)
        acc_ref[...] += _nn(a, wd_ref[...])

        @pl.when(j == nj - 1)
        def _():
            out_ref[...] = h_ref[...] + 0.5 * acc_ref[...]

    outs = pl.pallas_call(
        _carried(body, 5, 4, carry, scatter, (t // tm - 1, nj - 1)),
        name="ffn_fwd_x" if nc else "ffn_fwd", grid=(t // tm, nj),
        in_specs=[pl.BlockSpec((tm, D_MODEL), lambda i, j: (i, 0)),
                  pl.BlockSpec((1, D_MODEL), lambda i, j: (0, 0)),
                  pl.BlockSpec((D_MODEL, tf), lambda i, j: (0, j)),
                  pl.BlockSpec((D_MODEL, tf), lambda i, j: (0, j)),
                  pl.BlockSpec((tf, D_MODEL), lambda i, j: (j, 0))] + [_ANY] * nc,
        out_specs=[pl.BlockSpec((tm, D_MODEL), lambda i, j: (i, 0)),
                   pl.BlockSpec((tm, tf), lambda i, j: (i, j)),
                   pl.BlockSpec((tm, tf), lambda i, j: (i, j)),
                   pl.BlockSpec((tm, D_MODEL), lambda i, j: (i, 0))] + [_ANY] * nc,
        out_shape=[jax.ShapeDtypeStruct((t, D_MODEL), F32),
                   jax.ShapeDtypeStruct((t, D_FF), BF16),
                   jax.ShapeDtypeStruct((t, D_FF), BF16),
                   jax.ShapeDtypeStruct((t, D_MODEL), BF16)] + _exchange_shapes(carry, scatter),
        scratch_shapes=[pltpu.VMEM((tm, D_MODEL), F32)] + (_exchange_sems(nc) if nc else []),
        compiler_params=_params(2),
    )(h, nw, wg, wu, wd, *carry)
    return outs[:4], outs[4:]


def _ffn_bwd(dy, h, nw, g, u, wg, wu, wd, carry=(), scatter=True):
    t = h.shape[0]
    tm, tf = _row_tile(t, FFN_BWD_TM), FFN_BWD_TF
    nj = D_FF // tf
    nc = len(carry)

    def body(dy_ref, h_ref, nw_ref, g_ref, u_ref, wg_ref, wu_ref, wd_ref,
             dh_ref, dg_ref, du_ref, a_ref, dz_ref, dnw_ref, acc_ref):
        i, j = pl.program_id(0), pl.program_id(1)

        @pl.when(j == 0)
        def _():
            dz_ref[...] = (0.5 * dy_ref[...]).astype(BF16)
            acc_ref[...] = jnp.zeros_like(acc_ref)

        @pl.when(jnp.logical_and(i == 0, j == 0))
        def _():
            dnw_ref[...] = jnp.zeros_like(dnw_ref)

        da = _nt(dz_ref[...], wd_ref[...])
        gv = g_ref[...].astype(F32)
        uv = u_ref[...].astype(F32)
        sig = _sigmoid(gv)
        silu = gv * sig
        dg = (da * uv * (sig * (1.0 + gv * (1.0 - sig)))).astype(BF16)
        du = (da * silu).astype(BF16)
        dg_ref[...] = dg
        du_ref[...] = du
        a_ref[...] = (silu * uv).astype(BF16)
        acc_ref[...] += _nt(dg, wg_ref[...]) + _nt(du, wu_ref[...])

        @pl.when(j == nj - 1)
        def _():
            dx, dnw = _norm_bwd(h_ref[...], nw_ref[...], acc_ref[...])
            dh_ref[...] = dy_ref[...] + dx
            dnw_ref[...] += dnw

    outs = pl.pallas_call(
        _carried(body, 8, 6, carry, scatter, (t // tm - 1, nj - 1)),
        name="ffn_bwd_x" if nc else "ffn_bwd", grid=(t // tm, nj),
        in_specs=[pl.BlockSpec((tm, D_MODEL), lambda i, j: (i, 0)),
                  pl.BlockSpec((tm, D_MODEL), lambda i, j: (i, 0)),
                  pl.BlockSpec((1, D_MODEL), lambda i, j: (0, 0)),
                  pl.BlockSpec((tm, tf), lambda i, j: (i, j)),
                  pl.BlockSpec((tm, tf), lambda i, j: (i, j)),
                  pl.BlockSpec((D_MODEL, tf), lambda i, j: (0, j)),
                  pl.BlockSpec((D_MODEL, tf), lambda i, j: (0, j)),
                  pl.BlockSpec((tf, D_MODEL), lambda i, j: (j, 0))] + [_ANY] * nc,
        out_specs=[pl.BlockSpec((tm, D_MODEL), lambda i, j: (i, 0)),
                   pl.BlockSpec((tm, tf), lambda i, j: (i, j)),
                   pl.BlockSpec((tm, tf), lambda i, j: (i, j)),
                   pl.BlockSpec((tm, tf), lambda i, j: (i, j)),
                   pl.BlockSpec((tm, D_MODEL), lambda i, j: (i, 0)),
                   pl.BlockSpec((1, D_MODEL), lambda i, j: (0, 0))] + [_ANY] * nc,
        out_shape=[jax.ShapeDtypeStruct((t, D_MODEL), F32),
                   jax.ShapeDtypeStruct((t, D_FF), BF16),
                   jax.ShapeDtypeStruct((t, D_FF), BF16),
                   jax.ShapeDtypeStruct((t, D_FF), BF16),
                   jax.ShapeDtypeStruct((t, D_MODEL), BF16),
                   jax.ShapeDtypeStruct((1, D_MODEL), F32)] + _exchange_shapes(carry, scatter),
        scratch_shapes=[pltpu.VMEM((tm, D_MODEL), F32)] + (_exchange_sems(nc) if nc else []),
        compiler_params=_params(2, FFN_BWD_VMEM_LIMIT_BYTES),
    )(dy, h, nw, g, u, wg, wu, wd, *carry)
    return outs[:6], outs[6:]


def _matmul_tn(a, b, tn, name, carry=()):
    k, m = a.shape
    n = b.shape[1]
    tm = min(m, 1408 if m % 1408 == 0 else 1024)
    tk = min(k, 2048)
    nc = len(carry)

    def body(a_ref, b_ref, o_ref):
        @pl.when(pl.program_id(2) == 0)
        def _():
            o_ref[...] = jnp.zeros_like(o_ref)

        o_ref[...] += _tn(a_ref[...], b_ref[...])

    grid = (m // tm, n // tn, k // tk)
    outs = pl.pallas_call(
        _carried(body, 2, 1, carry, True, tuple(g - 1 for g in grid)),
        name=name + "_x" if nc else name, grid=grid,
        in_specs=[pl.BlockSpec((tk, tm), lambda i, j, kk: (kk, i)),
                  pl.BlockSpec((tk, tn), lambda i, j, kk: (kk, j))] + [_ANY] * nc,
        out_specs=[pl.BlockSpec((tm, tn), lambda i, j, kk: (i, j))] + [_ANY] * nc,
        out_shape=[jax.ShapeDtypeStruct((m, n), F32)] + _exchange_shapes(carry, True),
        scratch_shapes=_exchange_sems(nc) if nc else [],
        compiler_params=_params(3),
    )(a, b, *carry)
    return (outs[0], outs[1:]) if nc else outs[0]


ATTN_TM = 512


def _rope(t, c, s1, s2):
    return t * c + pltpu.roll(t, LANES - ROPE_DIM // 2, 1) * s1 + pltpu.roll(t, ROPE_DIM // 2, 1) * s2


def _rope_bwd(d, c, s1, s2):
    return d * c + pltpu.roll(d * s1, ROPE_DIM // 2, 1) + pltpu.roll(d * s2, LANES - ROPE_DIM // 2, 1)


def _attn_qkv_fwd(h, nw, w_in, b_in, rc, rs1, rs2):
    t = h.shape[0]
    tm = _row_tile(t, ATTN_TM)

    def body(h_ref, nw_ref, w_ref, b_ref, c_ref, s1_ref, s2_ref, q_ref, k_ref, v_ref, hn_ref):
        _, xhat = _rms_stats(h_ref[...])
        hn = (xhat * nw_ref[...]).astype(BF16)
        hn_ref[...] = hn
        qkv = _nn(hn, w_ref[...]) + b_ref[...]
        c, s1, s2 = c_ref[...], s1_ref[...], s2_ref[...]
        for s in range(ATTN_Q_W // LANES):
            q_ref[:, s * LANES:(s + 1) * LANES] = _rope(qkv[:, s * LANES:(s + 1) * LANES], c, s1, s2).astype(BF16)
        for s in range(ATTN_KV_W // LANES):
            lo = ATTN_Q_W + s * LANES
            k_ref[:, s * LANES:(s + 1) * LANES] = _rope(qkv[:, lo:lo + LANES], c, s1, s2).astype(BF16)
        v_ref[...] = qkv[:, ATTN_Q_W + ATTN_KV_W:].astype(BF16)

    row = lambda w: pl.BlockSpec((tm, w), lambda i: (i, 0))
    full = lambda a: pl.BlockSpec(a.shape, lambda i: (0, 0))
    return pl.pallas_call(
        body, name="attn_qkv_fwd", grid=(t // tm,),
        in_specs=[row(D_MODEL), full(nw), full(w_in), full(b_in), row(LANES), row(LANES), row(LANES)],
        out_specs=[row(ATTN_Q_W), row(ATTN_KV_W), row(ATTN_KV_W), row(D_MODEL)],
        out_shape=[jax.ShapeDtypeStruct((t, ATTN_Q_W), BF16),
                   jax.ShapeDtypeStruct((t, ATTN_KV_W), BF16),
                   jax.ShapeDtypeStruct((t, ATTN_KV_W), BF16),
                   jax.ShapeDtypeStruct((t, D_MODEL), BF16)],
        compiler_params=_params(1),
    )(h, nw, w_in, b_in, rc, rs1, rs2)


def _attn_group(q_ref, kc_ref, kp_ref, vc_ref, vp_ref, sinks_ref, hk, n):
    hd = ATTN_HEAD_DIM
    cols = slice(hk * hd, (hk + 1) * hd)
    krow = lax.broadcasted_iota(jnp.int32, (2 * ATTN_BLOCK, hd), 0)
    kcat = jnp.concatenate([kp_ref[:, cols], kc_ref[:, cols]], axis=0)
    vcat = jnp.concatenate([vp_ref[:, cols], vc_ref[:, cols]], axis=0)
    kcat = jnp.where(krow == 0, jnp.zeros_like(kcat), kcat)
    vcat = jnp.where(krow == 0, jnp.zeros_like(vcat), vcat)
    heads = [hk * ATTN_GROUP + g for g in range(ATTN_GROUP)]
    qs = jnp.concatenate([q_ref[:, hq * hd:(hq + 1) * hd] for hq in heads], axis=0)
    s = _nt(qs, kcat) * ATTN_SCALE
    rows = ATTN_GROUP * ATTN_BLOCK
    ri = lax.broadcasted_iota(jnp.int32, (rows, 2 * ATTN_BLOCK), 0) & (ATTN_BLOCK - 1)
    cj = lax.broadcasted_iota(jnp.int32, (rows, 2 * ATTN_BLOCK), 1)
    first = jnp.where(n > 0, 0, 2 * ATTN_BLOCK)
    valid = jnp.logical_or(jnp.logical_and(cj < ATTN_BLOCK, cj > ri + first),
                           jnp.logical_and(cj >= ATTN_BLOCK, cj - ATTN_BLOCK <= ri))
    sink = jnp.concatenate([jnp.full((ATTN_BLOCK, 2 * ATTN_BLOCK), sinks_ref[hq], F32) for hq in heads], axis=0)
    s = jnp.where(valid, s, jnp.where(cj == 0, sink, NEG_BIG))
    p = jnp.exp(s - jnp.max(s, axis=-1, keepdims=True))
    probs = p / jnp.sum(p, axis=-1, keepdims=True)
    return heads, qs, kcat, vcat, probs, krow


def _attn_core_fwd(q, k, v, sinks):
    t = q.shape[0]
    nb = t // ATTN_BLOCK

    def body(q_ref, kc_ref, kp_ref, vc_ref, vp_ref, sinks_ref, o_ref):
        n = pl.program_id(0)
        for hk in range(ATTN_KV_HEADS):
            heads, _, _, vcat, probs, _ = _attn_group(q_ref, kc_ref, kp_ref, vc_ref, vp_ref, sinks_ref, hk, n)
            o = _nn(probs.astype(BF16), vcat)
            for g, hq in enumerate(heads):
                o_ref[:, hq * ATTN_HEAD_DIM:(hq + 1) * ATTN_HEAD_DIM] = (
                    o[g * ATTN_BLOCK:(g + 1) * ATTN_BLOCK].astype(BF16))

    cur = lambda w: pl.BlockSpec((ATTN_BLOCK, w), lambda n: (n, 0))
    prev = lambda w: pl.BlockSpec((ATTN_BLOCK, w), lambda n: (jnp.maximum(n - 1, 0), 0))
    return pl.pallas_call(
        body, name="attn_core_fwd", grid=(nb,),
        in_specs=[cur(ATTN_Q_W), cur(ATTN_KV_W), prev(ATTN_KV_W), cur(ATTN_KV_W), prev(ATTN_KV_W),
                  pl.BlockSpec(memory_space=pltpu.SMEM)],
        out_specs=cur(ATTN_Q_W),
        out_shape=jax.ShapeDtypeStruct((t, ATTN_Q_W), BF16),
        compiler_params=_params(1),
    )(q, k, k, v, v, sinks.reshape(-1))


def _attn_core_bwd(q, k, v, do, sinks):
    t = q.shape[0]
    nb = t // ATTN_BLOCK
    hd = ATTN_HEAD_DIM

    def body(q_ref, kc_ref, kp_ref, vc_ref, vp_ref, do_ref, sinks_ref,
             dq_ref, dk_ref, dv_ref, dsink_ref, ck_ref, cv_ref):
        n = pl.program_id(0)

        @pl.when(n == 0)
        def _():
            dsink_ref[...] = jnp.zeros_like(dsink_ref)
            ck_ref[...] = jnp.zeros_like(ck_ref)
            cv_ref[...] = jnp.zeros_like(cv_ref)

        @pl.when(n == nb)
        def _():
            dk_ref[...] = ck_ref[...]
            dv_ref[...] = cv_ref[...]

        @pl.when(n < nb)
        def _():
            for hk in range(ATTN_KV_HEADS):
                heads, qs, kcat, vcat, probs, krow = _attn_group(
                    q_ref, kc_ref, kp_ref, vc_ref, vp_ref, sinks_ref, hk, n)
                dos = jnp.concatenate([do_ref[:, hq * hd:(hq + 1) * hd] for hq in heads], axis=0)
                dp = _nt(dos, vcat)
                delta = jnp.sum(probs * dp, axis=-1, keepdims=True)
                dsf = probs * (dp - delta)
                ds = dsf.astype(BF16)
                dqs = _nn(ds, kcat) * ATTN_SCALE
                dkc = jnp.where(krow == 0, 0.0, _tn(ds, qs) * ATTN_SCALE)
                dvc = jnp.where(krow == 0, 0.0, _tn(probs.astype(BF16), dos))
                for g, hq in enumerate(heads):
                    blk = slice(g * ATTN_BLOCK, (g + 1) * ATTN_BLOCK)
                    dq_ref[:, hq * hd:(hq + 1) * hd] = dqs[blk]
                    dsink_ref[hq:hq + 1, :] += jnp.broadcast_to(
                        jnp.sum(dsf[blk, 0:1], axis=0, keepdims=True), (1, LANES))
                cols = slice(hk * hd, (hk + 1) * hd)
                dk_ref[:, cols] = ck_ref[:, cols] + dkc[:ATTN_BLOCK]
                dv_ref[:, cols] = cv_ref[:, cols] + dvc[:ATTN_BLOCK]
                ck_ref[:, cols] = dkc[ATTN_BLOCK:]
                cv_ref[:, cols] = dvc[ATTN_BLOCK:]

    cur = lambda w: pl.BlockSpec((ATTN_BLOCK, w), lambda n: (jnp.minimum(n, nb - 1), 0))
    prev = lambda w: pl.BlockSpec((ATTN_BLOCK, w), lambda n: (jnp.clip(n - 1, 0, nb - 1), 0))
    return pl.pallas_call(
        body, name="attn_core_bwd", grid=(nb + 1,),
        in_specs=[cur(ATTN_Q_W), cur(ATTN_KV_W), prev(ATTN_KV_W), cur(ATTN_KV_W), prev(ATTN_KV_W),
                  cur(ATTN_Q_W), pl.BlockSpec(memory_space=pltpu.SMEM)],
        out_specs=[cur(ATTN_Q_W), prev(ATTN_KV_W), prev(ATTN_KV_W),
                   pl.BlockSpec((ATTN_Q_HEADS, LANES), lambda n: (0, 0))],
        out_shape=[jax.ShapeDtypeStruct((t, ATTN_Q_W), F32),
                   jax.ShapeDtypeStruct((t, ATTN_KV_W), F32),
                   jax.ShapeDtypeStruct((t, ATTN_KV_W), F32),
                   jax.ShapeDtypeStruct((ATTN_Q_HEADS, LANES), F32)],
        scratch_shapes=[pltpu.VMEM((ATTN_BLOCK, ATTN_KV_W), F32),
                        pltpu.VMEM((ATTN_BLOCK, ATTN_KV_W), F32)],
        compiler_params=_params(1),
    )(q, k, k, v, v, do, sinks.reshape(-1))


def _proj_out_fwd(x, w, b, res):
    t = x.shape[0]
    tm = _row_tile(t, 512)

    def body(x_ref, w_ref, b_ref, r_ref, o_ref):
        o_ref[...] = r_ref[...] + _nn(x_ref[...], w_ref[...]) + b_ref[...]

    row = pl.BlockSpec((tm, D_MODEL), lambda i: (i, 0))
    return pl.pallas_call(
        body, name="proj_out_fwd", grid=(t // tm,),
        in_specs=[row, pl.BlockSpec(w.shape, lambda i: (0, 0)), pl.BlockSpec(b.shape, lambda i: (0, 0)), row],
        out_specs=row,
        out_shape=jax.ShapeDtypeStruct((t, D_MODEL), F32),
        compiler_params=_params(1),
    )(x, w, b, res)


def _proj_out_bwd(dy, w):
    t = dy.shape[0]
    tm = _row_tile(t, 512)

    def body(dy_ref, w_ref, dx_ref, dyb_ref, db_ref):
        @pl.when(pl.program_id(0) == 0)
        def _():
            db_ref[...] = jnp.zeros_like(db_ref)

        dy_v = dy_ref[...]
        dyb = dy_v.astype(BF16)
        dyb_ref[...] = dyb
        dx_ref[...] = _nt(dyb, w_ref[...]).astype(BF16)
        db_ref[...] += jnp.sum(dy_v, axis=0, keepdims=True)

    row = pl.BlockSpec((tm, D_MODEL), lambda i: (i, 0))
    return pl.pallas_call(
        body, name="proj_out_bwd", grid=(t // tm,),
        in_specs=[row, pl.BlockSpec(w.shape, lambda i: (0, 0))],
        out_specs=[row, row, pl.BlockSpec((1, D_MODEL), lambda i: (0, 0))],
        out_shape=[jax.ShapeDtypeStruct((t, D_MODEL), BF16),
                   jax.ShapeDtypeStruct((t, D_MODEL), BF16),
                   jax.ShapeDtypeStruct((1, D_MODEL), F32)],
        compiler_params=_params(1),
    )(dy, w)


def _attn_qkv_bwd(dy, h, nw, w_in, dq, dk, dv, rc, rs1, rs2):
    t = h.shape[0]
    tm = _row_tile(t, ATTN_TM)

    def body(dy_ref, h_ref, nw_ref, w_ref, dq_ref, dk_ref, dv_ref, c_ref, s1_ref, s2_ref,
             dh_ref, dqkv_ref, db_ref, dnw_ref, tmp_ref):
        @pl.when(pl.program_id(0) == 0)
        def _():
            db_ref[...] = jnp.zeros_like(db_ref)
            dnw_ref[...] = jnp.zeros_like(dnw_ref)

        c, s1, s2 = c_ref[...], s1_ref[...], s2_ref[...]
        for s in range(ATTN_Q_W // LANES):
            tmp_ref[:, s * LANES:(s + 1) * LANES] = _rope_bwd(dq_ref[:, s * LANES:(s + 1) * LANES], c, s1, s2)
        for s in range(ATTN_KV_W // LANES):
            lo = ATTN_Q_W + s * LANES
            tmp_ref[:, lo:lo + LANES] = _rope_bwd(dk_ref[:, s * LANES:(s + 1) * LANES], c, s1, s2)
        tmp_ref[:, ATTN_Q_W + ATTN_KV_W:] = dv_ref[...]
        dqkv = tmp_ref[...]
        db_ref[...] += jnp.sum(dqkv, axis=0, keepdims=True)
        dqkv_b = dqkv.astype(BF16)
        dqkv_ref[...] = dqkv_b
        dx, dnw = _norm_bwd(h_ref[...], nw_ref[...], _nt(dqkv_b, w_ref[...]))
        dh_ref[...] = dy_ref[...] + dx
        dnw_ref[...] += dnw

    row = lambda w: pl.BlockSpec((tm, w), lambda i: (i, 0))
    full = lambda a: pl.BlockSpec(a.shape, lambda i: (0, 0))
    return pl.pallas_call(
        body, name="attn_qkv_bwd", grid=(t // tm,),
        in_specs=[row(D_MODEL), row(D_MODEL), full(nw), full(w_in), row(ATTN_Q_W), row(ATTN_KV_W),
                  row(ATTN_KV_W), row(LANES), row(LANES), row(LANES)],
        out_specs=[row(D_MODEL), row(ATTN_IN), pl.BlockSpec((1, ATTN_IN), lambda i: (0, 0)),
                   pl.BlockSpec((1, D_MODEL), lambda i: (0, 0))],
        out_shape=[jax.ShapeDtypeStruct((t, D_MODEL), F32),
                   jax.ShapeDtypeStruct((t, ATTN_IN), BF16),
                   jax.ShapeDtypeStruct((1, ATTN_IN), F32),
                   jax.ShapeDtypeStruct((1, D_MODEL), F32)],
        scratch_shapes=[pltpu.VMEM((tm, ATTN_IN), F32)],
        compiler_params=_params(1),
    )(dy, h, nw, w_in, dq, dk, dv, rc, rs1, rs2)


GDN_TM = 512
GDN_CONV_TM = 256


def _gdn_proj_fwd(h, nw, w_qkv, w_z, w_ba):
    t = h.shape[0]
    tm = _row_tile(t, GDN_TM)

    def body(h_ref, nw_ref, wq_ref, wz_ref, wb_ref, x_ref, z_ref, ba_ref, hn_ref):
        _, xhat = _rms_stats(h_ref[...])
        hn = (xhat * nw_ref[...]).astype(BF16)
        hn_ref[...] = hn
        x_ref[...] = _nn(hn, wq_ref[...])
        z_ref[...] = _nn(hn, wz_ref[...])
        ba_ref[...] = _nn(hn, wb_ref[...])

    row = lambda w: pl.BlockSpec((tm, w), lambda i: (i, 0))
    full = lambda a: pl.BlockSpec(a.shape, lambda i: (0, 0))
    return pl.pallas_call(
        body, name="gdn_proj_fwd", grid=(t // tm,),
        in_specs=[row(D_MODEL), full(nw), full(w_qkv), full(w_z), full(w_ba)],
        out_specs=[row(GDN_CONV_W), row(D_MODEL), row(LANES), row(D_MODEL)],
        out_shape=[jax.ShapeDtypeStruct((t, GDN_CONV_W), F32),
                   jax.ShapeDtypeStruct((t, D_MODEL), F32),
                   jax.ShapeDtypeStruct((t, LANES), F32),
                   jax.ShapeDtypeStruct((t, D_MODEL), BF16)],
        compiler_params=_params(1),
    )(h, nw, w_qkv, w_z, w_ba)


def _softplus(x):
    return jnp.maximum(x, 0.0) + jnp.log(1.0 + jnp.exp(-jnp.abs(x)))


def _conv_taps(x, halo, cw):
    tm = x.shape[0]
    xx = jnp.concatenate([halo, x], axis=0)
    taps = [xx[8 - (GDN_CONV - 1) + i: 8 - (GDN_CONV - 1) + i + tm] for i in range(GDN_CONV)]
    c = taps[0] * cw[0:1]
    for i in range(1, GDN_CONV):
        c = c + taps[i] * cw[i:i + 1]
    return c, taps


def _gates(ba, gp):
    lane = lax.broadcasted_iota(jnp.int32, ba.shape, 1)
    beta = _sigmoid(ba)
    pre = ba + gp[1:2]
    g = -jnp.exp(gp[0:1]) * _softplus(pre)
    gates = jnp.where(lane < GDN_HEADS, beta, jnp.where(lane < 2 * GDN_HEADS, g, 0.0))
    return lane, beta, pre, g, gates


def _gdn_conv_fwd(x, cw, ba, gp):
    t = x.shape[0]
    tm = _row_tile(t, GDN_CONV_TM)

    def body(x_ref, halo_ref, cw_ref, ba_ref, gp_ref, q_ref, k_ref, v_ref, gates_ref):
        halo = jnp.where(pl.program_id(0) > 0, halo_ref[...], 0.0)
        c, _ = _conv_taps(x_ref[...], halo, cw_ref[...])
        s = c * _sigmoid(c)
        for hh in range(GDN_HEADS):
            sq = s[:, hh * LANES:(hh + 1) * LANES]
            q_ref[:, hh * LANES:(hh + 1) * LANES] = (
                sq * lax.rsqrt(jnp.sum(sq * sq, axis=-1, keepdims=True) + NORM_EPS) * GDN_QSCALE)
            sk = s[:, D_MODEL + hh * LANES:D_MODEL + (hh + 1) * LANES]
            k_ref[:, hh * LANES:(hh + 1) * LANES] = (
                sk * lax.rsqrt(jnp.sum(sk * sk, axis=-1, keepdims=True) + NORM_EPS))
        v_ref[...] = s[:, 2 * D_MODEL:]
        gates_ref[...] = _gates(ba_ref[...], gp_ref[...])[4]

    row = lambda w: pl.BlockSpec((tm, w), lambda i: (i, 0))
    full = lambda a: pl.BlockSpec(a.shape, lambda i: (0, 0))
    halo = pl.BlockSpec((8, GDN_CONV_W), lambda i: (jnp.maximum(i * (tm // 8) - 1, 0), 0))
    return pl.pallas_call(
        body, name="gdn_conv_fwd", grid=(t // tm,),
        in_specs=[row(GDN_CONV_W), halo, full(cw), row(LANES), full(gp)],
        out_specs=[row(D_MODEL), row(D_MODEL), row(D_MODEL), row(LANES)],
        out_shape=[jax.ShapeDtypeStruct((t, D_MODEL), F32)] * 3 + [jax.ShapeDtypeStruct((t, LANES), F32)],
        compiler_params=_params(1),
    )(x, x, cw, ba, gp)


def _split3(x):
    hi = x.astype(BF16).astype(F32)
    r1 = x - hi
    mid = r1.astype(BF16).astype(F32)
    lo = (r1 - mid).astype(BF16).astype(F32)
    return hi, mid, lo


def _chunk_cumsum(col, keep):
    nb, c, _ = col.shape
    l3 = lax.broadcasted_iota(jnp.int32, (nb, c, LANES), 2)
    hi, mid, lo = _split3(col)
    pieces = jnp.where(l3 == 0, hi, jnp.where(l3 == 1, mid, jnp.where(l3 == 2, lo, 0.0))).astype(BF16)
    s = _bnn(jnp.where(keep, 1.0, 0.0).astype(BF16), pieces)
    return s[..., 0:1] + s[..., 1:2] + s[..., 2:3]


def _unit_inverse(nmat):
    nb, c, _ = nmat.shape
    ri = lax.broadcasted_iota(jnp.int32, (nb, c, c), 1)
    ci = lax.broadcasted_iota(jnp.int32, (nb, c, c), 2)
    eye = jnp.where(ri == ci, 1.0, 0.0).astype(F32)
    same = (ri >> GDN_SUB_SHIFT) == (ci >> GDN_SUB_SHIFT)
    nd = jnp.where(same, nmat, 0.0)
    no = nmat - nd
    mm = lambda a, b: _bnn(a.astype(BF16), b.astype(BF16))
    n2 = mm(nd, nd)
    n4 = mm(n2, n2)
    n8 = mm(n4, n4)
    td = mm(mm(mm(eye - nd, eye + n2), eye + n4), eye + n8)
    bm = mm(td, no)
    b2 = mm(bm, bm)
    return mm(mm(eye - bm, eye + b2), td)


class _LaneWindow:
    def __init__(self, ref, j):
        self.ref, self.lanes, self.shape = ref, slice(j * LANES, (j + 1) * LANES), (ref.shape[0], LANES)

    def __getitem__(self, idx):
        return self.ref[:, self.lanes]

    def __setitem__(self, idx, val):
        self.ref[:, self.lanes] = val


def _chunk_terms(q_ref, k_ref, gates_ref, h, transposed):
    rows = k_ref.shape[0]
    nb = rows // GDN_CHUNK
    c = GDN_CHUNK
    lane = lax.broadcasted_iota(jnp.int32, (rows, LANES), 1)
    gt = gates_ref[...]
    beta = jnp.sum(jnp.where(lane == h, gt, 0.0), axis=-1, keepdims=True).reshape(nb, c, 1)
    g = jnp.sum(jnp.where(lane == GDN_HEADS + h, gt, 0.0), axis=-1, keepdims=True).reshape(nb, c, 1)
    ri = lax.broadcasted_iota(jnp.int32, (nb, c, c), 1)
    ci = lax.broadcasted_iota(jnp.int32, (nb, c, c), 2)
    gcol = _chunk_cumsum(g, ri >= ci)
    gamma = jnp.broadcast_to(gcol, (nb, c, LANES))
    l3 = lax.broadcasted_iota(jnp.int32, (nb, c, LANES), 2)
    gh, gm, gl = _split3(gcol)
    pmat = jnp.where(l3 == 0, gh, jnp.where(l3 == 1, gm, jnp.where(l3 == 2, gl, jnp.where(l3 < 6, 1.0, 0.0))))
    qmat = jnp.where(l3 < 3, 1.0, jnp.where(l3 == 3, -gh, jnp.where(l3 == 4, -gm, jnp.where(l3 == 5, -gl, 0.0))))
    pmat, qmat = pmat.astype(BF16), qmat.astype(BF16)
    k = k_ref[...].reshape(nb, c, LANES)
    q = q_ref[...].reshape(nb, c, LANES)
    kb = k * beta
    kbf, kbb, qb = k.astype(BF16), kb.astype(BF16), q.astype(BF16)
    out = dict(beta=beta, g=g, gamma=gamma, k=k, q=q, kb=kb, kbf=kbf, kbb=kbb, qb=qb, ri=ri, ci=ci)
    diff = _bnt(pmat, qmat)
    lmat = jnp.exp(jnp.where(ri >= ci, diff, NEG_BIG))
    out["L"] = lmat
    out["A"] = jnp.where(ri > ci, _bnt(kbb, kbf) * lmat, 0.0)
    out["Aqk"] = jnp.where(ri >= ci, _bnt(qb, kbf) * lmat, 0.0)
    if transposed:
        difft = _bnt(qmat, pmat)
        lt = jnp.exp(jnp.where(ci >= ri, difft, NEG_BIG))
        out["LT"] = lt
        out["AT"] = jnp.where(ci > ri, _bnt(kbf, kbb) * lt, 0.0)
        out["AqkT"] = jnp.where(ci >= ri, _bnt(kbf, qb) * lt, 0.0)
    return out


def _gdn_intra_fwd(q, k, v, gates):
    t = q.shape[0]
    rows = _row_tile(t, GDN_ROWS)
    nb = rows // GDN_CHUNK
    nchunks = t // GDN_CHUNK
    hs = GDN_INTRA_HEADS

    def head(h, hh, q_ref, k_ref, v_ref, gates_ref, u_ref, w_ref, qd_ref, kd_ref, aqk_ref, cd_ref):
        tm_ = _chunk_terms(q_ref, k_ref, gates_ref, h, False)
        gamma, beta = tm_["gamma"], tm_["beta"]
        eg = jnp.exp(gamma)
        tinv = _unit_inverse(tm_["A"])
        v3 = v_ref[...].reshape(nb, GDN_CHUNK, LANES)
        rhs = jnp.concatenate([v3 * beta, tm_["kb"] * eg], axis=-1)
        sol = _bnn(tinv.astype(BF16), rhs.astype(BF16))
        u_ref[...] = sol[..., :LANES].reshape(rows, LANES)
        w_ref[...] = sol[..., LANES:].reshape(rows, LANES).astype(BF16)
        gl = gamma[:, GDN_CHUNK - 1:GDN_CHUNK, :]
        qd_ref[...] = (tm_["q"] * eg).reshape(rows, LANES).astype(BF16)
        kd_ref[...] = (tm_["k"] * jnp.exp(gl - gamma)).reshape(rows, LANES).astype(BF16)
        aqk_ref[hh] = tm_["Aqk"].reshape(rows, GDN_CHUNK).astype(BF16)
        cd_ref[hh] = jnp.broadcast_to(jnp.exp(gl), (nb, 8, LANES)).reshape(nb * 8, LANES)

    def body(q_ref, k_ref, v_ref, gates_ref, u_ref, w_ref, qd_ref, kd_ref, aqk_ref, cd_ref):
        for hh in range(hs):
            win = lambda ref: _LaneWindow(ref, hh)
            head(pl.program_id(1) * hs + hh, hh, win(q_ref), win(k_ref), win(v_ref), gates_ref,
                 win(u_ref), win(w_ref), win(qd_ref), win(kd_ref), aqk_ref, cd_ref)

    blk = pl.BlockSpec((rows, hs * LANES), lambda i, h: (i, h))
    return pl.pallas_call(
        body, name="gdn_intra_fwd", grid=(t // rows, GDN_HEADS // hs),
        in_specs=[blk, blk, blk, pl.BlockSpec((rows, LANES), lambda i, h: (i, 0))],
        out_specs=[blk, blk, blk, blk,
                   pl.BlockSpec((hs, rows, GDN_CHUNK), lambda i, h: (h, i, 0)),
                   pl.BlockSpec((hs, nb * 8, LANES), lambda i, h: (h, i, 0))],
        out_shape=[jax.ShapeDtypeStruct((t, D_MODEL), F32),
                   jax.ShapeDtypeStruct((t, D_MODEL), BF16),
                   jax.ShapeDtypeStruct((t, D_MODEL), BF16),
                   jax.ShapeDtypeStruct((t, D_MODEL), BF16),
                   jax.ShapeDtypeStruct((GDN_HEADS, t, GDN_CHUNK), BF16),
                   jax.ShapeDtypeStruct((GDN_HEADS, nchunks * 8, LANES), F32)],
        compiler_params=_params(2),
    )(q, k, v, gates)


def _gdn_scan_fwd(u, w, qd, kd, aqk, cd):
    t = u.shape[0]
    rows = _row_tile(t, GDN_ROWS)
    nb = rows // GDN_CHUNK
    nchunks = t // GDN_CHUNK
    hs = GDN_SCAN_HEADS

    def body(u_ref, w_ref, qd_ref, kd_ref, aqk_ref, cd_ref, o_ref, st_ref, s_ref):
        @pl.when(pl.program_id(1) == 0)
        def _():
            s_ref[...] = jnp.zeros_like(s_ref)

        states = [s_ref[hh] for hh in range(hs)]
        for c in range(nb):
            r = slice(c * GDN_CHUNK, (c + 1) * GDN_CHUNK)
            for hh in range(hs):
                ln = slice(hh * LANES, (hh + 1) * LANES)
                s = states[hh]
                st_ref[hh, c * LANES:(c + 1) * LANES, :] = s
                sb = s.astype(BF16)
                vb = (u_ref[r, ln] - _nn(w_ref[r, ln], sb)).astype(BF16)
                o_ref[r, ln] = _nn(qd_ref[r, ln], sb) + _nn(aqk_ref[hh, r, :], vb)
                states[hh] = s * cd_ref[hh, c * 8:c * 8 + 1, :] + _tn(kd_ref[r, ln], vb)
        for hh in range(hs):
            s_ref[hh] = states[hh]

    blk = pl.BlockSpec((rows, hs * LANES), lambda h, i: (i, h))
    return pl.pallas_call(
        body, name="gdn_scan_fwd", grid=(GDN_HEADS // hs, t // rows),
        in_specs=[blk, blk, blk, blk,
                  pl.BlockSpec((hs, rows, GDN_CHUNK), lambda h, i: (h, i, 0)),
                  pl.BlockSpec((hs, nb * 8, LANES), lambda h, i: (h, i, 0))],
        out_specs=[blk, pl.BlockSpec((hs, nb * LANES, LANES), lambda h, i: (h, i, 0))],
        out_shape=[jax.ShapeDtypeStruct((t, D_MODEL), F32),
                   jax.ShapeDtypeStruct((GDN_HEADS, nchunks * LANES, LANES), F32)],
        scratch_shapes=[pltpu.VMEM((hs, LANES, LANES), F32)],
        compiler_params=_params(2),
    )(u, w, qd, kd, aqk, cd)


def _gdn_out_fwd(o, z, nw, w_out, res):
    t = o.shape[0]
    tm = _row_tile(t, GDN_TM)

    def body(o_ref, z_ref, nw_ref, w_ref, r_ref, out_ref, gated_ref):
        nwv = nw_ref[...]
        for hh in range(GDN_HEADS):
            sl = slice(hh * LANES, (hh + 1) * LANES)
            _, on = _rms_stats(o_ref[:, sl])
            zv = z_ref[:, sl]
            gated_ref[:, sl] = (on * nwv * (zv * _sigmoid(zv))).astype(BF16)
        out_ref[...] = r_ref[...] + _nn(gated_ref[...], w_ref[...])

    row = pl.BlockSpec((tm, D_MODEL), lambda i: (i, 0))
    full = lambda a: pl.BlockSpec(a.shape, lambda i: (0, 0))
    return pl.pallas_call(
        body, name="gdn_out_fwd", grid=(t // tm,),
        in_specs=[row, row, full(nw), full(w_out), row],
        out_specs=[row, row],
        out_shape=[jax.ShapeDtypeStruct((t, D_MODEL), F32), jax.ShapeDtypeStruct((t, D_MODEL), BF16)],
        compiler_params=_params(1),
    )(o, z, nw, w_out, res)


def _gdn_out_bwd(dy, w_out, o, z, nw):
    t = o.shape[0]
    tm = _row_tile(t, GDN_TM)

    def body(dy_ref, w_ref, o_ref, z_ref, nw_ref, do_ref, dz_ref, dyb_ref, dnw_ref, dgt_ref):
        @pl.when(pl.program_id(0) == 0)
        def _():
            dnw_ref[...] = jnp.zeros_like(dnw_ref)

        dyb = dy_ref[...].astype(BF16)
        dyb_ref[...] = dyb
        dgt_ref[...] = _nt(dyb, w_ref[...])
        nwv = nw_ref[...]
        for hh in range(GDN_HEADS):
            sl = slice(hh * LANES, (hh + 1) * LANES)
            r, on = _rms_stats(o_ref[:, sl])
            zv = z_ref[:, sl]
            sig = _sigmoid(zv)
            sz = zv * sig
            dgt = dgt_ref[:, sl]
            d_on = dgt * nwv * sz
            dz_ref[:, sl] = (dgt * on * nwv * (sig * (1.0 + zv * (1.0 - sig)))).astype(BF16)
            dnw_ref[...] += jnp.sum(dgt * on * sz, axis=0, keepdims=True)
            do_ref[:, sl] = (r * (d_on - on * jnp.mean(d_on * on, axis=-1, keepdims=True))).astype(BF16)

    row = pl.BlockSpec((tm, D_MODEL), lambda i: (i, 0))
    full = lambda a: pl.BlockSpec(a.shape, lambda i: (0, 0))
    return pl.pallas_call(
        body, name="gdn_out_bwd", grid=(t // tm,),
        in_specs=[row, full(w_out), row, row, full(nw)],
        out_specs=[row, row, row, pl.BlockSpec((1, LANES), lambda i: (0, 0))],
        out_shape=[jax.ShapeDtypeStruct((t, D_MODEL), BF16)] * 3 + [jax.ShapeDtypeStruct((1, LANES), F32)],
        scratch_shapes=[pltpu.VMEM((tm, D_MODEL), F32)],
        compiler_params=_params(1),
    )(dy, w_out, o, z, nw)


def _gdn_scan_bwd(u, w, qd, kd, aqk, cd, st, do):
    t = u.shape[0]
    rows = _row_tile(t, GDN_ROWS)
    nb = rows // GDN_CHUNK
    nchunks = t // GDN_CHUNK
    nsteps = t // rows
    cc = GDN_CHUNK
    hs = GDN_SCAN_HEADS

    def body(u_ref, w_ref, qd_ref, kd_ref, aqk_ref, cd_ref, st_ref, do_ref,
             du_ref, dw_ref, dqd_ref, dkd_ref, daqk_ref, daqkt_ref, dcd_ref, ds_ref):
        @pl.when(pl.program_id(1) == 0)
        def _():
            ds_ref[...] = jnp.zeros_like(ds_ref)

        ri = lax.broadcasted_iota(jnp.int32, (cc, cc), 0)
        ci = lax.broadcasted_iota(jnp.int32, (cc, cc), 1)
        dstates = [ds_ref[hh] for hh in range(hs)]
        for c in reversed(range(nb)):
            r = slice(c * cc, (c + 1) * cc)
            for hh in range(hs):
                ln = slice(hh * LANES, (hh + 1) * LANES)
                s = st_ref[hh, c * LANES:(c + 1) * LANES, :]
                sb = s.astype(BF16)
                dsn = dstates[hh]
                dsb = dsn.astype(BF16)
                wv, kdv, qdv, aq, dov = w_ref[r, ln], kd_ref[r, ln], qd_ref[r, ln], aqk_ref[hh, r, :], do_ref[r, ln]
                vb = (u_ref[r, ln] - _nn(wv, sb)).astype(BF16)
                dv = _tn(aq, dov) + _nn(kdv, dsb)
                dvb = dv.astype(BF16)
                daqk_ref[hh, r, :] = jnp.where(ri >= ci, _nt(dov, vb), 0.0)
                daqkt_ref[hh, r, :] = jnp.where(ci >= ri, _nt(vb, dov), 0.0)
                dqd_ref[r, ln] = _nt(dov, sb)
                dkd_ref[r, ln] = _nt(vb, dsb)
                dcd_ref[hh, c * 8:(c + 1) * 8, :] = jnp.broadcast_to(jnp.sum(s * dsn), (8, LANES))
                du_ref[r, ln] = dv
                dw_ref[r, ln] = -_nt(dvb, sb)
                dstates[hh] = _tn(qdv, dov) + dsn * cd_ref[hh, c * 8:c * 8 + 1, :] - _tn(wv, dvb)
        for hh in range(hs):
            ds_ref[hh] = dstates[hh]

    rev = lambda i: nsteps - 1 - i
    blk = pl.BlockSpec((rows, hs * LANES), lambda h, i: (rev(i), h))
    sq = pl.BlockSpec((hs, rows, cc), lambda h, i: (h, rev(i), 0))
    cdb = pl.BlockSpec((hs, nb * 8, LANES), lambda h, i: (h, rev(i), 0))
    return pl.pallas_call(
        body, name="gdn_scan_bwd", grid=(GDN_HEADS // hs, nsteps),
        in_specs=[blk, blk, blk, blk, sq, cdb,
                  pl.BlockSpec((hs, nb * LANES, LANES), lambda h, i: (h, rev(i), 0)), blk],
        out_specs=[blk, blk, blk, blk, sq, sq, cdb],
        out_shape=[jax.ShapeDtypeStruct((t, D_MODEL), F32)] * 4
        + [jax.ShapeDtypeStruct((GDN_HEADS, t, cc), F32)] * 2
        + [jax.ShapeDtypeStruct((GDN_HEADS, nchunks * 8, LANES), F32)],
        scratch_shapes=[pltpu.VMEM((hs, LANES, LANES), F32)],
        compiler_params=_params(2),
    )(u, w, qd, kd, aqk, cd, st, do)


def _gdn_intra_bwd(q, k, v, gates, u, w, du, dw, dqd, dkd, daqk, daqkt, dcd):
    t = q.shape[0]
    rows = _row_tile(t, GDN_ROWS)
    nb = rows // GDN_CHUNK
    cc = GDN_CHUNK
    hs = GDN_INTRA_HEADS

    def head(h, hh, q_ref, k_ref, v_ref, gates_ref, u_ref, w_ref, du_ref, dw_ref, dqd_ref, dkd_ref,
             daqk_ref, daqkt_ref, dcd_ref, dq_ref, dk_ref, dv_ref, dgates_ref):
        tm_ = _chunk_terms(q_ref, k_ref, gates_ref, h, True)
        gamma, beta, kk, qq, kb = tm_["gamma"], tm_["beta"], tm_["k"], tm_["q"], tm_["kb"]
        kbf, kbb, qb = tm_["kbf"], tm_["kbb"], tm_["qb"]
        lmat, lt = tm_["L"], tm_["LT"]
        ri, ci = tm_["ri"], tm_["ci"]
        r3 = lambda ref: ref[...].reshape(nb, cc, LANES)
        eg = jnp.exp(gamma)
        gl = gamma[:, cc - 1:cc, :]
        ekd = jnp.exp(gl - gamma)
        v3 = r3(v_ref)
        tt = _unit_inverse(tm_["AT"])
        dsol = jnp.concatenate([r3(du_ref), r3(dw_ref)], axis=-1)
        sol = jnp.concatenate([r3(u_ref), r3(w_ref).astype(F32)], axis=-1)
        dx = _bnn(tt.astype(BF16), dsol.astype(BF16))
        dxb, solb = dx.astype(BF16), sol.astype(BF16)
        da = jnp.where(ri > ci, -_bnt(dxb, solb), 0.0)
        dat = jnp.where(ci > ri, -_bnt(solb, dxb), 0.0)
        dxu, dxw = dx[..., :LANES], dx[..., LANES:]
        dv_ref[...] = (dxu * beta).reshape(rows, LANES)
        dbeta = jnp.sum(dxu * v3, axis=-1, keepdims=True)
        dkb = dxw * eg
        dgam = jnp.sum(dxw * kb * eg, axis=-1, keepdims=True)
        dkb = dkb + _bnn((da * lmat).astype(BF16), kbf)
        dk = _bnn((dat * lt).astype(BF16), kbb)
        dgam = dgam + jnp.sum(da * tm_["A"], axis=-1, keepdims=True) - jnp.sum(dat * tm_["AT"], axis=-1, keepdims=True)
        daq = daqk_ref[hh].reshape(nb, cc, cc)
        daqt = daqkt_ref[hh].reshape(nb, cc, cc)
        dq = _bnn((daq * lmat).astype(BF16), kbf)
        dk = dk + _bnn((daqt * lt).astype(BF16), qb)
        dgam = dgam + jnp.sum(daq * tm_["Aqk"], axis=-1, keepdims=True) - jnp.sum(daqt * tm_["AqkT"], axis=-1, keepdims=True)
        dqd3, dkd3 = r3(dqd_ref), r3(dkd_ref)
        dq = dq + dqd3 * eg
        dgam = dgam + jnp.sum(dqd3 * qq * eg, axis=-1, keepdims=True)
        dk = dk + dkd3 * ekd
        tk = jnp.sum(dkd3 * kk * ekd, axis=-1, keepdims=True)
        dgam = dgam - tk
        dcdv = dcd_ref[hh].reshape(nb, 8, LANES)[:, 0:1, 0:1]
        dglast = jnp.sum(tk, axis=1, keepdims=True) + dcdv * jnp.exp(gl[:, :, 0:1])
        rowi = lax.broadcasted_iota(jnp.int32, (nb, cc, 1), 1)
        dgam = dgam + jnp.where(rowi == cc - 1, dglast, 0.0)
        dk = dk + dkb * beta
        dbeta = dbeta + jnp.sum(dkb * kk, axis=-1, keepdims=True)
        dg = _chunk_cumsum(dgam, ci >= ri)
        dq_ref[...] = dq.reshape(rows, LANES)
        dk_ref[...] = dk.reshape(rows, LANES)
        lane = lax.broadcasted_iota(jnp.int32, (rows, LANES), 1)
        dgates_ref[...] += (jnp.where(lane == h, dbeta.reshape(rows, 1), 0.0)
                            + jnp.where(lane == GDN_HEADS + h, dg.reshape(rows, 1), 0.0))

    def body(q_ref, k_ref, v_ref, gates_ref, u_ref, w_ref, du_ref, dw_ref, dqd_ref, dkd_ref,
             daqk_ref, daqkt_ref, dcd_ref, dq_ref, dk_ref, dv_ref, dgates_ref):
        @pl.when(pl.program_id(1) == 0)
        def _():
            dgates_ref[...] = jnp.zeros_like(dgates_ref)

        for hh in range(hs):
            win = lambda ref: _LaneWindow(ref, hh)
            head(pl.program_id(1) * hs + hh, hh, win(q_ref), win(k_ref), win(v_ref), gates_ref, win(u_ref),
                 win(w_ref), win(du_ref), win(dw_ref), win(dqd_ref), win(dkd_ref), daqk_ref, daqkt_ref, dcd_ref,
                 win(dq_ref), win(dk_ref), win(dv_ref), dgates_ref)

    blk = pl.BlockSpec((rows, hs * LANES), lambda i, h: (i, h))
    shared = pl.BlockSpec((rows, LANES), lambda i, h: (i, 0))
    sq = pl.BlockSpec((hs, rows, cc), lambda i, h: (h, i, 0))
    return pl.pallas_call(
        body, name="gdn_intra_bwd", grid=(t // rows, GDN_HEADS // hs),
        in_specs=[blk, blk, blk, shared, blk, blk, blk, blk, blk, blk, sq, sq,
                  pl.BlockSpec((hs, nb * 8, LANES), lambda i, h: (h, i, 0))],
        out_specs=[blk, blk, blk, shared],
        out_shape=[jax.ShapeDtypeStruct((t, D_MODEL), F32)] * 3 + [jax.ShapeDtypeStruct((t, LANES), F32)],
        compiler_params=_params(2),
    )(q, k, v, gates, u, w, du, dw, dqd, dkd, daqk, daqkt, dcd)


def _gdn_conv_bwd_a(x, cw, ba, gp, dq, dk, dv, dgates):
    t = x.shape[0]
    tm = _row_tile(t, GDN_CONV_TM)

    def body(x_ref, halo_ref, cw_ref, ba_ref, gp_ref, dq_ref, dk_ref, dv_ref, dgates_ref,
             dc_ref, dba_ref, dcw_ref, dgp_ref, ds_ref):
        @pl.when(pl.program_id(0) == 0)
        def _():
            dcw_ref[...] = jnp.zeros_like(dcw_ref)
            dgp_ref[...] = jnp.zeros_like(dgp_ref)

        halo = jnp.where(pl.program_id(0) > 0, halo_ref[...], 0.0)
        c, taps = _conv_taps(x_ref[...], halo, cw_ref[...])
        sig = _sigmoid(c)
        s = c * sig
        for hh in range(GDN_HEADS):
            sl = slice(hh * LANES, (hh + 1) * LANES)
            sq = s[:, sl]
            rq = lax.rsqrt(jnp.sum(sq * sq, axis=-1, keepdims=True) + NORM_EPS)
            qh = sq * rq
            dqv = dq_ref[:, sl]
            ds_ref[:, sl] = GDN_QSCALE * rq * (dqv - qh * jnp.sum(dqv * qh, axis=-1, keepdims=True))
            sl2 = slice(D_MODEL + hh * LANES, D_MODEL + (hh + 1) * LANES)
            sk = s[:, sl2]
            rk = lax.rsqrt(jnp.sum(sk * sk, axis=-1, keepdims=True) + NORM_EPS)
            kh = sk * rk
            dkv = dk_ref[:, sl]
            ds_ref[:, sl2] = rk * (dkv - kh * jnp.sum(dkv * kh, axis=-1, keepdims=True))
        ds_ref[:, 2 * D_MODEL:] = dv_ref[...]
        dc = ds_ref[...] * (sig * (1.0 + c * (1.0 - sig)))
        dc_ref[...] = dc
        for i in range(GDN_CONV):
            dcw_ref[i:i + 1, :] += jnp.sum(dc * taps[i], axis=0, keepdims=True)
        lane, beta, pre, g, _ = _gates(ba_ref[...], gp_ref[...])
        dgt = dgates_ref[...]
        db = dgt * beta * (1.0 - beta)
        dpre = dgt * (-jnp.exp(gp_ref[0:1, :])) * _sigmoid(pre)
        isa = jnp.logical_and(lane >= GDN_HEADS, lane < 2 * GDN_HEADS)
        dba_ref[...] = jnp.where(lane < GDN_HEADS, db, jnp.where(isa, dpre, 0.0)).astype(BF16)
        dgp_ref[0:1, :] += jnp.sum(jnp.where(isa, dgt * g, 0.0), axis=0, keepdims=True)
        dgp_ref[1:2, :] += jnp.sum(jnp.where(isa, dpre, 0.0), axis=0, keepdims=True)

    row = lambda w: pl.BlockSpec((tm, w), lambda i: (i, 0))
    full = lambda a: pl.BlockSpec(a.shape, lambda i: (0, 0))
    halo = pl.BlockSpec((8, GDN_CONV_W), lambda i: (jnp.maximum(i * (tm // 8) - 1, 0), 0))
    return pl.pallas_call(
        body, name="gdn_conv_bwd_a", grid=(t // tm,),
        in_specs=[row(GDN_CONV_W), halo, full(cw), row(LANES), full(gp), row(D_MODEL), row(D_MODEL),
                  row(D_MODEL), row(LANES)],
        out_specs=[row(GDN_CONV_W), row(LANES), pl.BlockSpec((8, GDN_CONV_W), lambda i: (0, 0)),
                   pl.BlockSpec((8, LANES), lambda i: (0, 0))],
        out_shape=[jax.ShapeDtypeStruct((t, GDN_CONV_W), F32),
                   jax.ShapeDtypeStruct((t, LANES), BF16),
                   jax.ShapeDtypeStruct((8, GDN_CONV_W), F32),
                   jax.ShapeDtypeStruct((8, LANES), F32)],
        scratch_shapes=[pltpu.VMEM((tm, GDN_CONV_W), F32)],
        compiler_params=_params(1),
    )(x, x, cw, ba, gp, dq, dk, dv, dgates)


def _gdn_conv_bwd_b(dc, cw):
    t = dc.shape[0]
    tm = _row_tile(t, GDN_CONV_TM)
    nsteps = t // tm

    def body(dc_ref, halo_ref, cw_ref, dx_ref):
        halo = jnp.where(pl.program_id(0) < nsteps - 1, halo_ref[...], 0.0)
        dd = jnp.concatenate([dc_ref[...], halo], axis=0)
        cw_v = cw_ref[...]
        acc = dd[GDN_CONV - 1:GDN_CONV - 1 + tm] * cw_v[0:1]
        for i in range(1, GDN_CONV):
            acc = acc + dd[GDN_CONV - 1 - i:GDN_CONV - 1 - i + tm] * cw_v[i:i + 1]
        dx_ref[...] = acc.astype(BF16)

    row = pl.BlockSpec((tm, GDN_CONV_W), lambda i: (i, 0))
    halo = pl.BlockSpec((8, GDN_CONV_W), lambda i: (jnp.minimum((i + 1) * (tm // 8), t // 8 - 1), 0))
    return pl.pallas_call(
        body, name="gdn_conv_bwd_b", grid=(nsteps,),
        in_specs=[row, halo, pl.BlockSpec(cw.shape, lambda i: (0, 0))],
        out_specs=row,
        out_shape=jax.ShapeDtypeStruct((t, GDN_CONV_W), BF16),
        compiler_params=_params(1),
    )(dc, dc, cw)


def _gdn_proj_bwd(dy, h, nw, w_qkv, w_z, w_ba, dx, dz, dba):
    t = h.shape[0]
    tm = _row_tile(t, GDN_TM)

    def body(dy_ref, h_ref, nw_ref, wq_ref, wz_ref, wb_ref, dx_ref, dz_ref, dba_ref, dh_ref, dnw_ref):
        @pl.when(pl.program_id(0) == 0)
        def _():
            dnw_ref[...] = jnp.zeros_like(dnw_ref)

        dhn = _nt(dx_ref[...], wq_ref[...]) + _nt(dz_ref[...], wz_ref[...]) + _nt(dba_ref[...], wb_ref[...])
        dxx, dnw = _norm_bwd(h_ref[...], nw_ref[...], dhn)
        dh_ref[...] = dy_ref[...] + dxx
        dnw_ref[...] += dnw

    row = lambda w: pl.BlockSpec((tm, w), lambda i: (i, 0))
    full = lambda a: pl.BlockSpec(a.shape, lambda i: (0, 0))
    return pl.pallas_call(
        body, name="gdn_proj_bwd", grid=(t // tm,),
        in_specs=[row(D_MODEL), row(D_MODEL), full(nw), full(w_qkv), full(w_z), full(w_ba),
                  row(GDN_CONV_W), row(D_MODEL), row(LANES)],
        out_specs=[row(D_MODEL), pl.BlockSpec((1, D_MODEL), lambda i: (0, 0))],
        out_shape=[jax.ShapeDtypeStruct((t, D_MODEL), F32), jax.ShapeDtypeStruct((1, D_MODEL), F32)],
        compiler_params=_params(1),
    )(dy, h, nw, w_qkv, w_z, w_ba, dx, dz, dba)


def _loss_head(h, nw, target):
    t = h.shape[0]
    tm = _row_tile(t, 512)

    def body(h_ref, nw_ref, t_ref, loss_ref, dh_ref, dnw_ref):
        @pl.when(pl.program_id(0) == 0)
        def _():
            loss_ref[...] = jnp.zeros_like(loss_ref)
            dnw_ref[...] = jnp.zeros_like(dnw_ref)

        x = h_ref[...]
        nwv = nw_ref[...]
        _, xhat = _rms_stats(x)
        err = xhat * nwv - t_ref[...]
        loss_ref[...] += 0.5 * jnp.sum(jnp.mean(err * err, axis=-1, keepdims=True))
        dx, dnw = _norm_bwd(x, nwv, err * (1.0 / D_MODEL))
        dh_ref[...] = dx
        dnw_ref[...] += dnw

    row = pl.BlockSpec((tm, D_MODEL), lambda i: (i, 0))
    return pl.pallas_call(
        body, name="loss_head", grid=(t // tm,),
        in_specs=[row, pl.BlockSpec((1, D_MODEL), lambda i: (0, 0)), row],
        out_specs=[pl.BlockSpec((8, LANES), lambda i: (0, 0)), row, pl.BlockSpec((1, D_MODEL), lambda i: (0, 0))],
        out_shape=[jax.ShapeDtypeStruct((8, LANES), F32),
                   jax.ShapeDtypeStruct((t, D_MODEL), F32),
                   jax.ShapeDtypeStruct((1, D_MODEL), F32)],
        compiler_params=_params(1),
    )(h, nw, target)


_PEER_FLIPS = [(dx, dy, dc) for dx in (0, 1) for dy in (0, 1) for dc in (0, 1)][1:]


_ANY = pl.BlockSpec(memory_space=pl.ANY)


def _exchange_copies(ins, outs, send_sems, recv_sems, local_sems, scatter):
    x, y, c = lax.axis_index("x"), lax.axis_index("y"), lax.axis_index("c")
    me = 4 * x + 2 * y + c
    copies = []
    for a in range(len(ins)):
        src = ins[a].at[me] if scatter else ins[a]
        copies.append(pltpu.make_async_copy(src, outs[a].at[me], local_sems.at[a]))
    for k, (fx, fy, fc) in enumerate(_PEER_FLIPS):
        px, py, pc = lax.rem(x + fx, 2), lax.rem(y + fy, 2), lax.rem(c + fc, 2)
        peer = 4 * px + 2 * py + pc
        for a in range(len(ins)):
            copies.append(pltpu.make_async_remote_copy(
                src_ref=ins[a].at[peer] if scatter else ins[a],
                dst_ref=outs[a].at[me],
                send_sem=send_sems.at[a, k], recv_sem=recv_sems.at[a, k],
                device_id=(px, py, pc), device_id_type=MESH_ID))
    return copies


def _exchange_shapes(arrs, scatter):
    return [jax.ShapeDtypeStruct((N_DEV,) + (a.shape[1:] if scatter else a.shape), a.dtype) for a in arrs]


def _exchange_sems(n):
    npeer = len(_PEER_FLIPS)
    return [pltpu.SemaphoreType.DMA((n, npeer)), pltpu.SemaphoreType.DMA((n, npeer)),
            pltpu.SemaphoreType.DMA((n,))]


def _exchange(arrs, scatter, name):
    n = len(arrs)

    def body(*refs):
        copies = _exchange_copies(refs[:n], refs[n:2 * n], *refs[2 * n:], scatter)
        for cp in copies:
            cp.start()
        for cp in copies:
            cp.wait()

    return pl.pallas_call(
        body, name=name, in_specs=[_ANY] * n, out_specs=[_ANY] * n,
        out_shape=_exchange_shapes(arrs, scatter), scratch_shapes=_exchange_sems(n),
    )(*arrs)


def _adamw(parts, w, m, v, name):
    r, c = w.shape
    tr = r
    for cand in (512, 256, 128, 64, 32, 16, 8):
        if r % cand == 0:
            tr = cand
            break
    c1 = 1.0 - ADAM_B1 ** ADAM_STEP
    c2 = 1.0 - ADAM_B2 ** ADAM_STEP

    def body(p_ref, w_ref, m_ref, v_ref, g_ref, d_ref, nm_ref, nv_ref):
        g = p_ref[0].astype(F32)
        for s in range(1, N_DEV):
            g = g + p_ref[s].astype(F32)
        nm = ADAM_B1 * m_ref[...] + (1.0 - ADAM_B1) * g
        nv = ADAM_B2 * v_ref[...] + (1.0 - ADAM_B2) * (g * g)
        g_ref[...] = g
        nm_ref[...] = nm
        nv_ref[...] = nv
        d_ref[...] = -ADAM_LR * ((nm / c1) / (jnp.sqrt(nv / c2) + ADAM_EPS) + ADAM_WD * w_ref[...])

    blk = pl.BlockSpec((tr, c), lambda i: (i, 0))
    return pl.pallas_call(
        body, name=name, grid=(r // tr,),
        in_specs=[pl.BlockSpec((N_DEV, tr, c), lambda i: (0, i, 0)), blk, blk, blk],
        out_specs=[blk] * 4,
        out_shape=[jax.ShapeDtypeStruct((r, c), F32)] * 4,
        compiler_params=_params(1),
    )(parts, w, m, v)


def _rope_tables(positions):
    half = ROPE_DIM // 2
    inv_freq = ROPE_THETA ** (-jnp.arange(0, ROPE_DIM, 2, dtype=F32) / ROPE_DIM)
    ang = positions.astype(F32)[:, None] * inv_freq
    cos, sin = jnp.cos(ang), jnp.sin(ang)
    t = positions.shape[0]
    zeros = lambda w: jnp.zeros((t, w), F32)
    c = jnp.concatenate([cos, cos, jnp.ones((t, ATTN_HEAD_DIM - ROPE_DIM), F32)], axis=1)
    s1 = jnp.concatenate([-sin, zeros(ATTN_HEAD_DIM - half)], axis=1)
    s2 = jnp.concatenate([zeros(half), sin, zeros(ATTN_HEAD_DIM - ROPE_DIM)], axis=1)
    return tuple(jnp.tile(a, (1, LANES // ATTN_HEAD_DIM)) for a in (c, s1, s2))


def _ffn_layer_fwd(h, nw, w, carry):
    (out, g, u, hn), landed = _ffn_fwd(h, nw, w["wg"], w["wu"], w["wd"], carry, False)
    return out, (h, g, u, hn), landed


def _ffn_layer_bwd(dy, saved, nw, w, carry, early=None):
    h, g, u, hn = saved
    (dh, dg, du, act, dz, dnw), landed = _ffn_bwd(dy, h, nw, g, u, w["wg"], w["wu"], w["wd"], carry, True)
    dwg = _matmul_tn(hn, dg, FFN_TF, "ffn_dwg")
    dwu = _matmul_tn(hn, du, FFN_TF, "ffn_dwu")
    if early is None:
        dwd, early_landed = _matmul_tn(act, dz, D_MODEL, "ffn_dwd"), ()
    else:
        dwd, early_landed = _matmul_tn(act, dz, D_MODEL, "ffn_dwd", early(dwg, dwu))
    return dh, dnw, (dwg, dwu, dwd), landed, early_landed


def _attn_layer_fwd(h, nw, w, ropes):
    q, k, v, hn = _attn_qkv_fwd(h, nw, w["w_in"], w["b_in"], *ropes)
    o = _attn_core_fwd(q, k, v, w["sinks"])
    out = _proj_out_fwd(o, w["w_out"], w["b_out"], h)
    return out, (h, hn, q, k, v, o)


def _attn_layer_bwd(dy, saved, nw, w, ropes):
    h, hn, q, k, v, o = saved
    do, dyb, db_out = _proj_out_bwd(dy, w["w_out"])
    dw_out = _matmul_tn(o, dyb, D_MODEL, "attn_dw_out")
    dq, dk, dv, dsink = _attn_core_bwd(q, k, v, do, w["sinks"])
    dh, dqkv, db_in, dnw = _attn_qkv_bwd(dy, h, nw, w["w_in"], dq, dk, dv, *ropes)
    dw_in = _matmul_tn(hn, dqkv, ATTN_IN, "attn_dw_in")
    return dh, dnw, dict(w_in=dw_in, b_in=db_in, sinks=dsink[:, 0], w_out=dw_out, b_out=db_out)


def _gdn_layer_fwd(h, nw, w):
    x, z, ba, hn = _gdn_proj_fwd(h, nw, w["w_qkv"], w["w_z"], w["w_ba"])
    q, k, v, gates = _gdn_conv_fwd(x, w["conv_w"], ba, w["gp"])
    u, ww, qd, kd, aqk, cd = _gdn_intra_fwd(q, k, v, gates)
    o, st = _gdn_scan_fwd(u, ww, qd, kd, aqk, cd)
    out, gated = _gdn_out_fwd(o, z, w["norm_w"], w["w_out"], h)
    return out, (h, hn, x, z, ba, q, k, v, gates, u, ww, qd, kd, aqk, cd, st, o, gated)


def _gdn_layer_bwd(dy, saved, nw, w):
    h, hn, x, z, ba, q, k, v, gates, u, ww, qd, kd, aqk, cd, st, o, gated = saved
    do, dz, dyb, dnorm_w = _gdn_out_bwd(dy, w["w_out"], o, z, w["norm_w"])
    dw_out = _matmul_tn(gated, dyb, D_MODEL, "gdn_dw_out")
    du, dw, dqd, dkd, daqk, daqkt, dcd = _gdn_scan_bwd(u, ww, qd, kd, aqk, cd, st, do)
    dq, dk, dv, dgates = _gdn_intra_bwd(q, k, v, gates, u, ww, du, dw, dqd, dkd, daqk, daqkt, dcd)
    dc, dba, dcw, dgp = _gdn_conv_bwd_a(x, w["conv_w"], ba, w["gp"], dq, dk, dv, dgates)
    dx = _gdn_conv_bwd_b(dc, w["conv_w"])
    dh, dnw = _gdn_proj_bwd(dy, h, nw, w["w_qkv"], w["w_z"], w["w_ba"], dx, dz, dba)
    dw_qkv = _matmul_tn(hn, dx, GDN_CONV_W // 2, "gdn_dw_qkv")
    dw_z = _matmul_tn(hn, dz, D_MODEL, "gdn_dw_z")
    dw_ba = _matmul_tn(hn, dba, LANES, "gdn_dw_ba")
    dw_in = jnp.concatenate([dw_qkv, dw_z, dw_ba[:, :2 * GDN_HEADS]], axis=1)
    lo, hi = GDN_HEADS, 2 * GDN_HEADS
    return dh, dnw, dict(w_in=dw_in, conv_w=dcw[:GDN_CONV], A_log=dgp[0, lo:hi], dt_bias=dgp[1, lo:hi],
                         norm_w=dnorm_w[0], w_out=dw_out)


def _local_step(x, positions, target, norms, final_norm, plan):
    ropes = _rope_tables(positions)
    h = x
    saved = []
    for layer in range(DEPTH):
        h, s1, landed = _ffn_layer_fwd(h, norms["ffn1"][layer], plan.weights("ffn1", layer),
                                       plan.fwd_carry("ffn1", layer))
        plan.fwd_landed(landed)
        if layer % 2 == 0:
            h, s2 = _attn_layer_fwd(h, norms["mix"][layer], plan.weights("mix", layer), ropes)
        else:
            h, s2 = _gdn_layer_fwd(h, norms["mix"][layer], plan.weights("mix", layer))
        h, s3, landed = _ffn_layer_fwd(h, norms["ffn2"][layer], plan.weights("ffn2", layer),
                                       plan.fwd_carry("ffn2", layer))
        plan.fwd_landed(landed)
        saved.append((s1, s2, s3))
    loss, dh, dfinal = _loss_head(h, final_norm, target)

    g_norm = {k: [None] * DEPTH for k in ("ffn1", "mix", "ffn2")}
    for layer in reversed(range(DEPTH)):
        s1, s2, s3 = saved[layer]
        dh, g_norm["ffn2"][layer], gw, landed, _ = _ffn_layer_bwd(
            dh, s3, norms["ffn2"][layer], plan.weights("ffn2", layer), plan.bwd_carry("ffn2", layer))
        plan.bwd_landed(landed)
        plan.grads("ffn2", layer, gw)
        if layer % 2 == 0:
            dh, g_norm["mix"][layer], gw = _attn_layer_bwd(dh, s2, norms["mix"][layer], plan.weights("mix", layer),
                                                           ropes)
        else:
            dh, g_norm["mix"][layer], gw = _gdn_layer_bwd(dh, s2, norms["mix"][layer], plan.weights("mix", layer))
        plan.grads("mix", layer, gw)
        dh, g_norm["ffn1"][layer], gw, landed, early_landed = _ffn_layer_bwd(
            dh, s1, norms["ffn1"][layer], plan.weights("ffn1", layer), plan.bwd_carry("ffn1", layer),
            plan.early_parts("ffn1", layer))
        plan.bwd_landed(landed)
        plan.early_landed("ffn1", layer, early_landed)
        plan.grads("ffn1", layer, gw)
    return loss, dh, dfinal, g_norm


def _cols_full(g):
    return jnp.transpose(g, (1, 0, 2)).reshape(g.shape[1], -1)


def _rows_full(g):
    return g.reshape(-1, g.shape[-1])


class _ShardedWeights:
    def __init__(self, shards, small):
        self.shards, self.small = shards, small
        self.whole, self.pending, self.received, self.small_grads, self.early = {}, {}, {}, {}, {}
        self.in_flight = []
        first = [("ffn1", 0)]
        self.in_flight = first
        self.fwd_landed(_exchange(self._shards_of(first), False, "gather_first"))

    def _group(self, key):
        kind, layer = key
        j = layer // 2
        if kind != "mix":
            return [self.shards[kind + "_w_gate_up"][layer], self.shards[kind + "_w_down"][layer]]
        if layer % 2 == 0:
            return [self.shards["attn_w_in"][j], self.shards["attn_w_out"][j]]
        return [self.shards["gdn_w_in"][j], self.shards["gdn_w_out"][j], self.shards["gdn_conv_w"][j]]

    def _shards_of(self, keys):
        return [a for key in keys for a in self._group(key)]

    def weights(self, kind, layer):
        return self.whole[(kind, layer)]

    def fwd_carry(self, kind, layer):
        if kind == "ffn1":
            keys = [("mix", 0), ("ffn2", 0)] if layer == 0 else [("ffn2", layer)]
        else:
            keys = [("ffn1", layer + 1), ("mix", layer + 1)] if layer + 1 < DEPTH else []
        self.in_flight = keys
        return self._shards_of(keys)

    def fwd_landed(self, landed):
        landed = list(landed)
        for key in self.in_flight:
            kind, layer = key
            j = layer // 2
            sm = self.small
            if kind != "mix":
                gu, dn = landed[:2]
                w = dict(wg=_cols_full(gu[:4]), wu=_cols_full(gu[4:]), wd=_rows_full(dn))
                landed = landed[2:]
            elif layer % 2 == 0:
                w = dict(w_in=_cols_full(landed[0]), w_out=_rows_full(landed[1]), b_in=sm["attn_b_in"][j][None],
                         sinks=sm["attn_sinks"][j][None], b_out=sm["attn_b_out"][j][None])
                landed = landed[2:]
            else:
                w_in = _cols_full(landed[0])
                w_ba = jnp.pad(w_in[:, GDN_CONV_W + D_MODEL:], ((0, 0), (0, LANES - 2 * GDN_HEADS)))
                gp = jnp.pad(jnp.stack([sm["gdn_A_log"][j], sm["gdn_dt_bias"][j]]),
                             ((0, 6), (GDN_HEADS, LANES - 2 * GDN_HEADS)))
                w = dict(w_qkv=w_in[:, :GDN_CONV_W], w_z=w_in[:, GDN_CONV_W:GDN_CONV_W + D_MODEL], w_ba=w_ba,
                         conv_w=_cols_full(landed[2]), gp=gp, norm_w=sm["gdn_norm_w"][j][None],
                         w_out=_rows_full(landed[1]))
                landed = landed[3:]
            self.whole[key] = w
        self.in_flight = []

    @staticmethod
    def _gate_up_part(dwg, dwu):
        return jnp.concatenate([_cols_shards(dwg, 4), _cols_shards(dwu, 4)], axis=0).astype(BF16)

    def early_parts(self, kind, layer):
        if (kind, layer) != ("ffn1", 0):
            return None
        return lambda dwg, dwu: [self._gate_up_part(dwg, dwu)]

    def early_landed(self, kind, layer, landed):
        if len(landed):
            self.early[(kind, layer)] = list(landed)

    def grads(self, kind, layer, g):
        if kind != "mix":
            dwg, dwu, dwd = g
            parts = [_rows_shards(dwd).astype(BF16)]
            if (kind, layer) not in self.early:
                parts = [self._gate_up_part(dwg, dwu)] + parts
        else:
            parts = [_cols_shards(g["w_in"]).astype(BF16), _rows_shards(g["w_out"]).astype(BF16)]
            if layer % 2 == 1:
                parts.append(_cols_shards(g["conv_w"]))
            self.small_grads[layer] = g
        self.pending[(kind, layer)] = parts

    def bwd_carry(self, kind, layer):
        if kind == "ffn1":
            keys = [("ffn2", layer), ("mix", layer)]
        else:
            keys = [("ffn1", layer + 1)] if layer + 1 < DEPTH else []
        self.in_flight = keys
        return [a for key in keys for a in self.pending[key]]

    def bwd_landed(self, landed):
        landed = list(landed)
        for key in self.in_flight:
            n = len(self.pending.pop(key))
            self.received[key], landed = landed[:n], landed[n:]
        self.in_flight = []

    def finish(self):
        self.in_flight = list(self.pending)
        self.bwd_landed(_exchange([a for key in self.in_flight for a in self.pending[key]], True, "scatter_last"))
        for key, landed in self.early.items():
            self.received[key] = landed + self.received[key]


def _cols_shards(full, n=N_DEV):
    r = full.shape[0]
    return jnp.transpose(full.reshape(r, n, -1), (1, 0, 2))


def _rows_shards(full):
    return full.reshape(N_DEV, -1, full.shape[-1])


_SMALL = ("ffn1_norm", "mix_norm", "ffn2_norm", "attn_b_in", "attn_sinks", "attn_b_out",
          "gdn_A_log", "gdn_dt_bias", "gdn_norm_w", "final_norm")
_BIG = ("ffn1_w_gate_up", "ffn1_w_down", "ffn2_w_gate_up", "ffn2_w_down", "attn_w_in", "attn_w_out",
        "gdn_w_in", "gdn_conv_w", "gdn_w_out")
_WEIGHTS = ("ffn1_norm", "ffn1_w_gate_up", "ffn1_w_down", "mix_norm", "ffn2_norm", "ffn2_w_gate_up",
            "ffn2_w_down", "attn_w_in", "attn_b_in", "attn_sinks", "attn_w_out", "attn_b_out", "gdn_w_in",
            "gdn_conv_w", "gdn_A_log", "gdn_dt_bias", "gdn_norm_w", "gdn_w_out", "final_norm")


def _pack_small(vals):
    flat = jnp.concatenate([v.reshape(-1).astype(F32) for v in vals])
    pad = (-flat.shape[0]) % (8 * LANES)
    return jnp.pad(flat, (0, pad)).reshape(-1, LANES)


def _unpack_small(packed, shapes):
    flat = packed.reshape(-1)
    out, off = [], 0
    for s in shapes:
        size = 1
        for d in s:
            size *= d
        out.append(flat[off:off + size].reshape(s))
        off += size
    return out


def kernel(x, positions, ffn1_norm, ffn1_w_gate_up, ffn1_w_down, mix_norm, ffn2_norm, ffn2_w_gate_up, ffn2_w_down, attn_w_in, attn_b_in, attn_sinks, attn_w_out, attn_b_out, gdn_w_in, gdn_conv_w, gdn_A_log, gdn_dt_bias, gdn_norm_w, gdn_w_out, final_norm, loss_target, m_ffn1_norm, m_ffn1_w_gate_up, m_ffn1_w_down, m_mix_norm, m_ffn2_norm, m_ffn2_w_gate_up, m_ffn2_w_down, m_attn_w_in, m_attn_b_in, m_attn_sinks, m_attn_w_out, m_attn_b_out, m_gdn_w_in, m_gdn_conv_w, m_gdn_A_log, m_gdn_dt_bias, m_gdn_norm_w, m_gdn_w_out, m_final_norm, v_ffn1_norm, v_ffn1_w_gate_up, v_ffn1_w_down, v_mix_norm, v_ffn2_norm, v_ffn2_w_gate_up, v_ffn2_w_down, v_attn_w_in, v_attn_b_in, v_attn_sinks, v_attn_w_out, v_attn_b_out, v_gdn_w_in, v_gdn_conv_w, v_gdn_A_log, v_gdn_dt_bias, v_gdn_norm_w, v_gdn_w_out, v_final_norm):
    args = dict(locals())
    wts = {n: args[n] for n in _WEIGHTS}
    moms = {n: args["m_" + n] for n in _WEIGHTS}
    vels = {n: args["v_" + n] for n in _WEIGHTS}

    shards = {n: wts[n] if n == "gdn_conv_w" else wts[n].astype(BF16) for n in _BIG}
    plan = _ShardedWeights(shards, wts)
    norms = dict(ffn1=ffn1_norm[:, None, :], mix=mix_norm[:, None, :], ffn2=ffn2_norm[:, None, :])
    loss, grad_x, dfinal, g_norm = _local_step(x[0], positions[0], loss_target[0], norms, final_norm[None], plan)
    plan.finish()

    layers = range(DEPTH)
    got = lambda kind, ls, i: jnp.stack([plan.received[(kind, l)][i] for l in ls], axis=1)
    received = {"attn_w_in": got("mix", layers[0::2], 0), "attn_w_out": got("mix", layers[0::2], 1),
                "gdn_w_in": got("mix", layers[1::2], 0), "gdn_w_out": got("mix", layers[1::2], 1),
                "gdn_conv_w": got("mix", layers[1::2], 2)}
    for kind in ("ffn1", "ffn2"):
        received[kind + "_w_gate_up"] = got(kind, layers, 0)
        received[kind + "_w_down"] = got(kind, layers, 1)
    g_attn = [plan.small_grads[l] for l in layers[0::2]]
    g_gdn = [plan.small_grads[l] for l in layers[1::2]]


    small_g = dict(
        ffn1_norm=jnp.concatenate(g_norm["ffn1"], axis=0), mix_norm=jnp.concatenate(g_norm["mix"], axis=0),
        ffn2_norm=jnp.concatenate(g_norm["ffn2"], axis=0),
        attn_b_in=jnp.concatenate([g["b_in"] for g in g_attn], axis=0),
        attn_sinks=jnp.stack([g["sinks"] for g in g_attn]),
        attn_b_out=jnp.concatenate([g["b_out"] for g in g_attn], axis=0),
        gdn_A_log=jnp.stack([g["A_log"] for g in g_gdn]), gdn_dt_bias=jnp.stack([g["dt_bias"] for g in g_gdn]),
        gdn_norm_w=jnp.stack([g["norm_w"] for g in g_gdn]), final_norm=dfinal[0])
    small_parts = _exchange([_pack_small([small_g[n] for n in _SMALL] + [loss[0, :1]])], False, "gather_small")[0]
    pad1 = jnp.zeros((1,), F32)
    sw = _pack_small([wts[n] for n in _SMALL] + [pad1])
    sm = _pack_small([moms[n] for n in _SMALL] + [pad1])
    sv = _pack_small([vels[n] for n in _SMALL] + [pad1])
    shapes = [wts[n].shape for n in _SMALL] + [(1,)]
    small_out = [_unpack_small(o, shapes) for o in _adamw(small_parts, sw, sm, sv, "adamw_small")]
    results = {n: tuple(o[i] for o in small_out) for i, n in enumerate(_SMALL)}
    loss_total = small_out[0][-1][0]

    for n in _BIG:
        shape = wts[n].shape
        two_d = lambda a: a.reshape(-1, shape[-1])
        p = received[n].reshape(N_DEV, -1, shape[-1])
        outs = _adamw(p, two_d(wts[n]), two_d(moms[n]), two_d(vels[n]), "adamw_" + n)
        results[n] = tuple(o.reshape(shape) for o in outs)

    return (loss_total, grad_x[None],
            *[results[n][0] for n in _WEIGHTS], *[results[n][1] for n in _WEIGHTS],
            *[results[n][2] for n in _WEIGHTS], *[results[n][3] for n in _WEIGHTS])
```

```python
import jax
import jax.numpy as jnp
from jax import lax
from jax.experimental import pallas as pl
from jax.experimental.pallas import tpu as pltpu

F32 = jnp.float32
BF16 = jnp.bfloat16

D_MODEL = 1024
DEPTH = 4
D_FF = 2816
NORM_EPS = 1e-6
N_DEV = 8

ATTN_Q_HEADS = 16
ATTN_KV_HEADS = 4
ATTN_HEAD_DIM = 64
ATTN_GROUP = 4
ATTN_BLOCK = 128
ROPE_DIM = 16
ROPE_THETA = 500000.0
ATTN_Q_W = 1024
ATTN_KV_W = 256
ATTN_IN = 1536
ATTN_SCALE = ATTN_HEAD_DIM ** -0.5

GDN_HEADS = 8
GDN_DK = 128
GDN_CONV = 4
GDN_CHUNK = 64
GDN_CONV_W = 3072
GDN_IN = 4112
GDN_QSCALE = GDN_DK ** -0.5
GDN_ROWS = 512
GDN_INTRA_HEADS = 2
GDN_SCAN_HEADS = 2
GDN_SUB_SHIFT = 4

ADAM_LR = 0.001
ADAM_B1 = 0.9
ADAM_B2 = 0.999
ADAM_EPS = 1e-08
ADAM_WD = 0.01
ADAM_STEP = 10

LANES = 128
NEG_BIG = -1e30
VMEM_LIMIT_BYTES = 56 * 1024 * 1024
MESH_ID = pl.DeviceIdType.MESH


def _params(n_axes, vmem_limit_bytes=VMEM_LIMIT_BYTES):
    return pltpu.CompilerParams(dimension_semantics=("arbitrary",) * n_axes,
                                vmem_limit_bytes=vmem_limit_bytes)


def _nn(a, b):
    return jnp.dot(a, b, preferred_element_type=F32)


def _nt(a, b):
    return lax.dot_general(a, b, (((1,), (1,)), ((), ())), preferred_element_type=F32)


def _tn(a, b):
    return lax.dot_general(a, b, (((0,), (0,)), ((), ())), preferred_element_type=F32)


def _bnn(a, b, precision=None):
    return lax.dot_general(a, b, (((2,), (1,)), ((0,), (0,))), precision=precision,
                           preferred_element_type=F32)


def _bnt(a, b, precision=None):
    return lax.dot_general(a, b, (((2,), (2,)), ((0,), (0,))), precision=precision,
                           preferred_element_type=F32)


def _sigmoid(x):
    return 1.0 / (1.0 + jnp.exp(-x))


def _rms_stats(x):
    r = lax.rsqrt(jnp.mean(x * x, axis=-1, keepdims=True) + NORM_EPS)
    return r, x * r


def _norm_bwd(x, nw, dhn):
    r, xhat = _rms_stats(x)
    dxh = dhn * nw
    dx = r * (dxh - xhat * jnp.mean(dxh * xhat, axis=-1, keepdims=True))
    dnw = jnp.sum(dhn * xhat, axis=0, keepdims=True)
    return dx, dnw


def _row_tile(t, pref):
    return min(t, pref)


FFN_TM = 512
FFN_BWD_TM = 512
FFN_BWD_TF = 1408
FFN_BWD_VMEM_LIMIT_BYTES = 61 * 1024 * 1024
FFN_TF = 1408


def _carried(body, n_in, n_out, carry, scatter, last_step):
    nc = len(carry)
    if not nc:
        return body

    def wrapped(*refs):
        ins, cin = refs[:n_in], refs[n_in:n_in + nc]
        outs = refs[n_in + nc:n_in + nc + n_out]
        cout = refs[n_in + nc + n_out:n_in + 2 * nc + n_out]
        scratch = refs[n_in + 2 * nc + n_out:]
        sems = scratch[len(scratch) - 3:]
        ids = [pl.program_id(ax) for ax in range(len(last_step))]
        first, last = ids[0] == 0, ids[0] == last_step[0]
        for ax in range(1, len(last_step)):
            first = jnp.logical_and(first, ids[ax] == 0)
            last = jnp.logical_and(last, ids[ax] == last_step[ax])

        @pl.when(first)
        def _():
            for cp in _exchange_copies(cin, cout, *sems, scatter):
                cp.start()

        body(*ins, *outs, *scratch[:len(scratch) - 3])

        @pl.when(last)
        def _():
            for cp in _exchange_copies(cin, cout, *sems, scatter):
                cp.wait()

    return wrapped


def _ffn_fwd(h, nw, wg, wu, wd, carry=(), scatter=False):
    t = h.shape[0]
    tm, tf = _row_tile(t, FFN_TM), FFN_TF
    nj = D_FF // tf
    nc = len(carry)

    def body(h_ref, nw_ref, wg_ref, wu_ref, wd_ref, out_ref, g_ref, u_ref, hn_ref, acc_ref):
        j = pl.program_id(1)

        @pl.when(j == 0)
        def _():
            _, xhat = _rms_stats(h_ref[...])
            hn_ref[...] = (xhat * nw_ref[...]).astype(BF16)
            acc_ref[...] = jnp.zeros_like(acc_ref)

        hn = hn_ref[...]
        g = _nn(hn, wg_ref[...])
        u = _nn(hn, wu_ref[...])
        g_ref[...] = g.astype(BF16)
        u_ref[...] = u.astype(BF16)
        a = (g * _sigmoid(g) * u).astype(BF16)
        acc_ref[...] += _nn(a, wd_ref[...])

        @pl.when(j == nj - 1)
        def _():
            out_ref[...] = h_ref[...] + 0.5 * acc_ref[...]

    outs = pl.pallas_call(
        _carried(body, 5, 4, carry, scatter, (t // tm - 1, nj - 1)),
        name="ffn_fwd_x" if nc else "ffn_fwd", grid=(t // tm, nj),
        in_specs=[pl.BlockSpec((tm, D_MODEL), lambda i, j: (i, 0)),
                  pl.BlockSpec((1, D_MODEL), lambda i, j: (0, 0)),
                  pl.BlockSpec((D_MODEL, tf), lambda i, j: (0, j)),
                  pl.BlockSpec((D_MODEL, tf), lambda i, j: (0, j)),
                  pl.BlockSpec((tf, D_MODEL), lambda i, j: (j, 0))] + [_ANY] * nc,
        out_specs=[pl.BlockSpec((tm, D_MODEL), lambda i, j: (i, 0)),
                   pl.BlockSpec((tm, tf), lambda i, j: (i, j)),
                   pl.BlockSpec((tm, tf), lambda i, j: (i, j)),
                   pl.BlockSpec((tm, D_MODEL), lambda i, j: (i, 0))] + [_ANY] * nc,
        out_shape=[jax.ShapeDtypeStruct((t, D_MODEL), F32),
                   jax.ShapeDtypeStruct((t, D_FF), BF16),
                   jax.ShapeDtypeStruct((t, D_FF), BF16),
                   jax.ShapeDtypeStruct((t, D_MODEL), BF16)] + _exchange_shapes(carry, scatter),
        scratch_shapes=[pltpu.VMEM((tm, D_MODEL), F32)] + (_exchange_sems(nc) if nc else []),
        compiler_params=_params(2),
    )(h, nw, wg, wu, wd, *carry)
    return outs[:4], outs[4:]


def _ffn_bwd(dy, h, nw, g, u, wg, wu, wd, carry=(), scatter=True):
    t = h.shape[0]
    tm, tf = _row_tile(t, FFN_BWD_TM), FFN_BWD_TF
    nj = D_FF // tf
    nc = len(carry)

    def body(dy_ref, h_ref, nw_ref, g_ref, u_ref, wg_ref, wu_ref, wd_ref,
             dh_ref, dg_ref, du_ref, a_ref, dz_ref, dnw_ref, acc_ref):
        i, j = pl.program_id(0), pl.program_id(1)

        @pl.when(j == 0)
        def _():
            dz_ref[...] = (0.5 * dy_ref[...]).astype(BF16)
            acc_ref[...] = jnp.zeros_like(acc_ref)

        @pl.when(jnp.logical_and(i == 0, j == 0))
        def _():
            dnw_ref[...] = jnp.zeros_like(dnw_ref)

        da = _nt(dz_ref[...], wd_ref[...])
        gv = g_ref[...].astype(F32)
        uv = u_ref[...].astype(F32)
        sig = _sigmoid(gv)
        silu = gv * sig
        dg = (da * uv * (sig * (1.0 + gv * (1.0 - sig)))).astype(BF16)
        du = (da * silu).astype(BF16)
        dg_ref[...] = dg
        du_ref[...] = du
        a_ref[...] = (silu * uv).astype(BF16)
        acc_ref[...] += _nt(dg, wg_ref[...]) + _nt(du, wu_ref[...])

        @pl.when(j == nj - 1)
        def _():
            dx, dnw = _norm_bwd(h_ref[...], nw_ref[...], acc_ref[...])
            dh_ref[...] = dy_ref[...] + dx
            dnw_ref[...] += dnw

    outs = pl.pallas_call(
        _carried(body, 8, 6, carry, scatter, (t // tm - 1, nj - 1)),
        name="ffn_bwd_x" if nc else "ffn_bwd", grid=(t // tm, nj),
        in_specs=[pl.BlockSpec((tm, D_MODEL), lambda i, j: (i, 0)),
                  pl.BlockSpec((tm, D_MODEL), lambda i, j: (i, 0)),
                  pl.BlockSpec((1, D_MODEL), lambda i, j: (0, 0)),
                  pl.BlockSpec((tm, tf), lambda i, j: (i, j)),
                  pl.BlockSpec((tm, tf), lambda i, j: (i, j)),
                  pl.BlockSpec((D_MODEL, tf), lambda i, j: (0, j)),
                  pl.BlockSpec((D_MODEL, tf), lambda i, j: (0, j)),
                  pl.BlockSpec((tf, D_MODEL), lambda i, j: (j, 0))] + [_ANY] * nc,
        out_specs=[pl.BlockSpec((tm, D_MODEL), lambda i, j: (i, 0)),
                   pl.BlockSpec((tm, tf), lambda i, j: (i, j)),
                   pl.BlockSpec((tm, tf), lambda i, j: (i, j)),
                   pl.BlockSpec((tm, tf), lambda i, j: (i, j)),
                   pl.BlockSpec((tm, D_MODEL), lambda i, j: (i, 0)),
                   pl.BlockSpec((1, D_MODEL), lambda i, j: (0, 0))] + [_ANY] * nc,
        out_shape=[jax.ShapeDtypeStruct((t, D_MODEL), F32),
                   jax.ShapeDtypeStruct((t, D_FF), BF16),
                   jax.ShapeDtypeStruct((t, D_FF), BF16),
                   jax.ShapeDtypeStruct((t, D_FF), BF16),
                   jax.ShapeDtypeStruct((t, D_MODEL), BF16),
                   jax.ShapeDtypeStruct((1, D_MODEL), F32)] + _exchange_shapes(carry, scatter),
        scratch_shapes=[pltpu.VMEM((tm, D_MODEL), F32)] + (_exchange_sems(nc) if nc else []),
        compiler_params=_params(2, FFN_BWD_VMEM_LIMIT_BYTES),
    )(dy, h, nw, g, u, wg, wu, wd, *carry)
    return outs[:6], outs[6:]


def _matmul_tn(a, b, tn, name, carry=()):
    k, m = a.shape
    n = b.shape[1]
    tm = min(m, 1408 if m % 1408 == 0 else 1024)
    tk = min(k, 2048)
    nc = len(carry)

    def body(a_ref, b_ref, o_ref):
        @pl.when(pl.program_id(2) == 0)
        def _():
            o_ref[...] = jnp.zeros_like(o_ref)

        o_ref[...] += _tn(a_ref[...], b_ref[...])

    grid = (m // tm, n // tn, k // tk)
    outs = pl.pallas_call(
        _carried(body, 2, 1, carry, True, tuple(g - 1 for g in grid)),
        name=name + "_x" if nc else name, grid=grid,
        in_specs=[pl.BlockSpec((tk, tm), lambda i, j, kk: (kk, i)),
                  pl.BlockSpec((tk, tn), lambda i, j, kk: (kk, j))] + [_ANY] * nc,
        out_specs=[pl.BlockSpec((tm, tn), lambda i, j, kk: (i, j))] + [_ANY] * nc,
        out_shape=[jax.ShapeDtypeStruct((m, n), F32)] + _exchange_shapes(carry, True),
        scratch_shapes=_exchange_sems(nc) if nc else [],
        compiler_params=_params(3),
    )(a, b, *carry)
    return (outs[0], outs[1:]) if nc else outs[0]


ATTN_TM = 512


def _rope(t, c, s1, s2):
    return t * c + pltpu.roll(t, LANES - ROPE_DIM // 2, 1) * s1 + pltpu.roll(t, ROPE_DIM // 2, 1) * s2


def _rope_bwd(d, c, s1, s2):
    return d * c + pltpu.roll(d * s1, ROPE_DIM // 2, 1) + pltpu.roll(d * s2, LANES - ROPE_DIM // 2, 1)


def _attn_qkv_fwd(h, nw, w_in, b_in, rc, rs1, rs2):
    t = h.shape[0]
    tm = _row_tile(t, ATTN_TM)

    def body(h_ref, nw_ref, w_ref, b_ref, c_ref, s1_ref, s2_ref, q_ref, k_ref, v_ref, hn_ref):
        _, xhat = _rms_stats(h_ref[...])
        hn = (xhat * nw_ref[...]).astype(BF16)
        hn_ref[...] = hn
        qkv = _nn(hn, w_ref[...]) + b_ref[...]
        c, s1, s2 = c_ref[...], s1_ref[...], s2_ref[...]
        for s in range(ATTN_Q_W // LANES):
            q_ref[:, s * LANES:(s + 1) * LANES] = _rope(qkv[:, s * LANES:(s + 1) * LANES], c, s1, s2).astype(BF16)
        for s in range(ATTN_KV_W // LANES):
            lo = ATTN_Q_W + s * LANES
            k_ref[:, s * LANES:(s + 1) * LANES] = _rope(qkv[:, lo:lo + LANES], c, s1, s2).astype(BF16)
        v_ref[...] = qkv[:, ATTN_Q_W + ATTN_KV_W:].astype(BF16)

    row = lambda w: pl.BlockSpec((tm, w), lambda i: (i, 0))
    full = lambda a: pl.BlockSpec(a.shape, lambda i: (0, 0))
    return pl.pallas_call(
        body, name="attn_qkv_fwd", grid=(t // tm,),
        in_specs=[row(D_MODEL), full(nw), full(w_in), full(b_in), row(LANES), row(LANES), row(LANES)],
        out_specs=[row(ATTN_Q_W), row(ATTN_KV_W), row(ATTN_KV_W), row(D_MODEL)],
        out_shape=[jax.ShapeDtypeStruct((t, ATTN_Q_W), BF16),
                   jax.ShapeDtypeStruct((t, ATTN_KV_W), BF16),
                   jax.ShapeDtypeStruct((t, ATTN_KV_W), BF16),
                   jax.ShapeDtypeStruct((t, D_MODEL), BF16)],
        compiler_params=_params(1),
    )(h, nw, w_in, b_in, rc, rs1, rs2)


def _attn_group(q_ref, kc_ref, kp_ref, vc_ref, vp_ref, sinks_ref, hk, n):
    hd = ATTN_HEAD_DIM
    cols = slice(hk * hd, (hk + 1) * hd)
    krow = lax.broadcasted_iota(jnp.int32, (2 * ATTN_BLOCK, hd), 0)
    kcat = jnp.concatenate([kp_ref[:, cols], kc_ref[:, cols]], axis=0)
    vcat = jnp.concatenate([vp_ref[:, cols], vc_ref[:, cols]], axis=0)
    kcat = jnp.where(krow == 0, jnp.zeros_like(kcat), kcat)
    vcat = jnp.where(krow == 0, jnp.zeros_like(vcat), vcat)
    heads = [hk * ATTN_GROUP + g for g in range(ATTN_GROUP)]
    qs = jnp.concatenate([q_ref[:, hq * hd:(hq + 1) * hd] for hq in heads], axis=0)
    s = _nt(qs, kcat) * ATTN_SCALE
    rows = ATTN_GROUP * ATTN_BLOCK
    ri = lax.broadcasted_iota(jnp.int32, (rows, 2 * ATTN_BLOCK), 0) & (ATTN_BLOCK - 1)
    cj = lax.broadcasted_iota(jnp.int32, (rows, 2 * ATTN_BLOCK), 1)
    first = jnp.where(n > 0, 0, 2 * ATTN_BLOCK)
    valid = jnp.logical_or(jnp.logical_and(cj < ATTN_BLOCK, cj > ri + first),
                           jnp.logical_and(cj >= ATTN_BLOCK, cj - ATTN_BLOCK <= ri))
    sink = jnp.concatenate([jnp.full((ATTN_BLOCK, 2 * ATTN_BLOCK), sinks_ref[hq], F32) for hq in heads], axis=0)
    s = jnp.where(valid, s, jnp.where(cj == 0, sink, NEG_BIG))
    p = jnp.exp(s - jnp.max(s, axis=-1, keepdims=True))
    probs = p / jnp.sum(p, axis=-1, keepdims=True)
    return heads, qs, kcat, vcat, probs, krow


def _attn_core_fwd(q, k, v, sinks):
    t = q.shape[0]
    nb = t // ATTN_BLOCK

    def body(q_ref, kc_ref, kp_ref, vc_ref, vp_ref, sinks_ref, o_ref):
        n = pl.program_id(0)
        for hk in range(ATTN_KV_HEADS):
            heads, _, _, vcat, probs, _ = _attn_group(q_ref, kc_ref, kp_ref, vc_ref, vp_ref, sinks_ref, hk, n)
            o = _nn(probs.astype(BF16), vcat)
            for g, hq in enumerate(heads):
                o_ref[:, hq * ATTN_HEAD_DIM:(hq + 1) * ATTN_HEAD_DIM] = (
                    o[g * ATTN_BLOCK:(g + 1) * ATTN_BLOCK].astype(BF16))

    cur = lambda w: pl.BlockSpec((ATTN_BLOCK, w), lambda n: (n, 0))
    prev = lambda w: pl.BlockSpec((ATTN_BLOCK, w), lambda n: (jnp.maximum(n - 1, 0), 0))
    return pl.pallas_call(
        body, name="attn_core_fwd", grid=(nb,),
        in_specs=[cur(ATTN_Q_W), cur(ATTN_KV_W), prev(ATTN_KV_W), cur(ATTN_KV_W), prev(ATTN_KV_W),
                  pl.BlockSpec(memory_space=pltpu.SMEM)],
        out_specs=cur(ATTN_Q_W),
        out_shape=jax.ShapeDtypeStruct((t, ATTN_Q_W), BF16),
        compiler_params=_params(1),
    )(q, k, k, v, v, sinks.reshape(-1))


def _attn_core_bwd(q, k, v, do, sinks):
    t = q.shape[0]
    nb = t // ATTN_BLOCK
    hd = ATTN_HEAD_DIM

    def body(q_ref, kc_ref, kp_ref, vc_ref, vp_ref, do_ref, sinks_ref,
             dq_ref, dk_ref, dv_ref, dsink_ref, ck_ref, cv_ref):
        n = pl.program_id(0)

        @pl.when(n == 0)
        def _():
            dsink_ref[...] = jnp.zeros_like(dsink_ref)
            ck_ref[...] = jnp.zeros_like(ck_ref)
            cv_ref[...] = jnp.zeros_like(cv_ref)

        @pl.when(n == nb)
        def _():
            dk_ref[...] = ck_ref[...]
            dv_ref[...] = cv_ref[...]

        @pl.when(n < nb)
        def _():
            for hk in range(ATTN_KV_HEADS):
                heads, qs, kcat, vcat, probs, krow = _attn_group(
                    q_ref, kc_ref, kp_ref, vc_ref, vp_ref, sinks_ref, hk, n)
                dos = jnp.concatenate([do_ref[:, hq * hd:(hq + 1) * hd] for hq in heads], axis=0)
                dp = _nt(dos, vcat)
                delta = jnp.sum(probs * dp, axis=-1, keepdims=True)
                dsf = probs * (dp - delta)
                ds = dsf.astype(BF16)
                dqs = _nn(ds, kcat) * ATTN_SCALE
                dkc = jnp.where(krow == 0, 0.0, _tn(ds, qs) * ATTN_SCALE)
                dvc = jnp.where(krow == 0, 0.0, _tn(probs.astype(BF16), dos))
                for g, hq in enumerate(heads):
                    blk = slice(g * ATTN_BLOCK, (g + 1) * ATTN_BLOCK)
                    dq_ref[:, hq * hd:(hq + 1) * hd] = dqs[blk]
                    dsink_ref[hq:hq + 1, :] += jnp.broadcast_to(
                        jnp.sum(dsf[blk, 0:1], axis=0, keepdims=True), (1, LANES))
                cols = slice(hk * hd, (hk + 1) * hd)
                dk_ref[:, cols] = ck_ref[:, cols] + dkc[:ATTN_BLOCK]
                dv_ref[:, cols] = cv_ref[:, cols] + dvc[:ATTN_BLOCK]
                ck_ref[:, cols] = dkc[ATTN_BLOCK:]
                cv_ref[:, cols] = dvc[ATTN_BLOCK:]

    cur = lambda w: pl.BlockSpec((ATTN_BLOCK, w), lambda n: (jnp.minimum(n, nb - 1), 0))
    prev = lambda w: pl.BlockSpec((ATTN_BLOCK, w), lambda n: (jnp.clip(n - 1, 0, nb - 1), 0))
    return pl.pallas_call(
        body, name="attn_core_bwd", grid=(nb + 1,),
        in_specs=[cur(ATTN_Q_W), cur(ATTN_KV_W), prev(ATTN_KV_W), cur(ATTN_KV_W), prev(ATTN_KV_W),
                  cur(ATTN_Q_W), pl.BlockSpec(memory_space=pltpu.SMEM)],
        out_specs=[cur(ATTN_Q_W), prev(ATTN_KV_W), prev(ATTN_KV_W),
                   pl.BlockSpec((ATTN_Q_HEADS, LANES), lambda n: (0, 0))],
        out_shape=[jax.ShapeDtypeStruct((t, ATTN_Q_W), F32),
                   jax.ShapeDtypeStruct((t, ATTN_KV_W), F32),
                   jax.ShapeDtypeStruct((t, ATTN_KV_W), F32),
                   jax.ShapeDtypeStruct((ATTN_Q_HEADS, LANES), F32)],
        scratch_shapes=[pltpu.VMEM((ATTN_BLOCK, ATTN_KV_W), F32),
                        pltpu.VMEM((ATTN_BLOCK, ATTN_KV_W), F32)],
        compiler_params=_params(1),
    )(q, k, k, v, v, do, sinks.reshape(-1))


def _proj_out_fwd(x, w, b, res):
    t = x.shape[0]
    tm = _row_tile(t, 512)

    def body(x_ref, w_ref, b_ref, r_ref, o_ref):
        o_ref[...] = r_ref[...] + _nn(x_ref[...], w_ref[...]) + b_ref[...]

    row = pl.BlockSpec((tm, D_MODEL), lambda i: (i, 0))
    return pl.pallas_call(
        body, name="proj_out_fwd", grid=(t // tm,),
        in_specs=[row, pl.BlockSpec(w.shape, lambda i: (0, 0)), pl.BlockSpec(b.shape, lambda i: (0, 0)), row],
        out_specs=row,
        out_shape=jax.ShapeDtypeStruct((t, D_MODEL), F32),
        compiler_params=_params(1),
    )(x, w, b, res)


def _proj_out_bwd(dy, w):
    t = dy.shape[0]
    tm = _row_tile(t, 512)

    def body(dy_ref, w_ref, dx_ref, dyb_ref, db_ref):
        @pl.when(pl.program_id(0) == 0)
        def _():
            db_ref[...] = jnp.zeros_like(db_ref)

        dy_v = dy_ref[...]
        dyb = dy_v.astype(BF16)
        dyb_ref[...] = dyb
        dx_ref[...] = _nt(dyb, w_ref[...]).astype(BF16)
        db_ref[...] += jnp.sum(dy_v, axis=0, keepdims=True)

    row = pl.BlockSpec((tm, D_MODEL), lambda i: (i, 0))
    return pl.pallas_call(
        body, name="proj_out_bwd", grid=(t // tm,),
        in_specs=[row, pl.BlockSpec(w.shape, lambda i: (0, 0))],
        out_specs=[row, row, pl.BlockSpec((1, D_MODEL), lambda i: (0, 0))],
        out_shape=[jax.ShapeDtypeStruct((t, D_MODEL), BF16),
                   jax.ShapeDtypeStruct((t, D_MODEL), BF16),
                   jax.ShapeDtypeStruct((1, D_MODEL), F32)],
        compiler_params=_params(1),
    )(dy, w)


def _attn_qkv_bwd(dy, h, nw, w_in, dq, dk, dv, rc, rs1, rs2):
    t = h.shape[0]
    tm = _row_tile(t, ATTN_TM)

    def body(dy_ref, h_ref, nw_ref, w_ref, dq_ref, dk_ref, dv_ref, c_ref, s1_ref, s2_ref,
             dh_ref, dqkv_ref, db_ref, dnw_ref, tmp_ref):
        @pl.when(pl.program_id(0) == 0)
        def _():
            db_ref[...] = jnp.zeros_like(db_ref)
            dnw_ref[...] = jnp.zeros_like(dnw_ref)

        c, s1, s2 = c_ref[...], s1_ref[...], s2_ref[...]
        for s in range(ATTN_Q_W // LANES):
            tmp_ref[:, s * LANES:(s + 1) * LANES] = _rope_bwd(dq_ref[:, s * LANES:(s + 1) * LANES], c, s1, s2)
        for s in range(ATTN_KV_W // LANES):
            lo = ATTN_Q_W + s * LANES
            tmp_ref[:, lo:lo + LANES] = _rope_bwd(dk_ref[:, s * LANES:(s + 1) * LANES], c, s1, s2)
        tmp_ref[:, ATTN_Q_W + ATTN_KV_W:] = dv_ref[...]
        dqkv = tmp_ref[...]
        db_ref[...] += jnp.sum(dqkv, axis=0, keepdims=True)
        dqkv_b = dqkv.astype(BF16)
        dqkv_ref[...] = dqkv_b
        dx, dnw = _norm_bwd(h_ref[...], nw_ref[...], _nt(dqkv_b, w_ref[...]))
        dh_ref[...] = dy_ref[...] + dx
        dnw_ref[...] += dnw

    row = lambda w: pl.BlockSpec((tm, w), lambda i: (i, 0))
    full = lambda a: pl.BlockSpec(a.shape, lambda i: (0, 0))
    return pl.pallas_call(
        body, name="attn_qkv_bwd", grid=(t // tm,),
        in_specs=[row(D_MODEL), row(D_MODEL), full(nw), full(w_in), row(ATTN_Q_W), row(ATTN_KV_W),
                  row(ATTN_KV_W), row(LANES), row(LANES), row(LANES)],
        out_specs=[row(D_MODEL), row(ATTN_IN), pl.BlockSpec((1, ATTN_IN), lambda i: (0, 0)),
                   pl.BlockSpec((1, D_MODEL), lambda i: (0, 0))],
        out_shape=[jax.ShapeDtypeStruct((t, D_MODEL), F32),
                   jax.ShapeDtypeStruct((t, ATTN_IN), BF16),
                   jax.ShapeDtypeStruct((1, ATTN_IN), F32),
                   jax.ShapeDtypeStruct((1, D_MODEL), F32)],
        scratch_shapes=[pltpu.VMEM((tm, ATTN_IN), F32)],
        compiler_params=_params(1),
    )(dy, h, nw, w_in, dq, dk, dv, rc, rs1, rs2)


GDN_TM = 512
GDN_CONV_TM = 256


def _gdn_proj_fwd(h, nw, w_qkv, w_z, w_ba):
    t = h.shape[0]
    tm = _row_tile(t, GDN_TM)

    def body(h_ref, nw_ref, wq_ref, wz_ref, wb_ref, x_ref, z_ref, ba_ref, hn_ref):
        _, xhat = _rms_stats(h_ref[...])
        hn = (xhat * nw_ref[...]).astype(BF16)
        hn_ref[...] = hn
        x_ref[...] = _nn(hn, wq_ref[...])
        z_ref[...] = _nn(hn, wz_ref[...])
        ba_ref[...] = _nn(hn, wb_ref[...])

    row = lambda w: pl.BlockSpec((tm, w), lambda i: (i, 0))
    full = lambda a: pl.BlockSpec(a.shape, lambda i: (0, 0))
    return pl.pallas_call(
        body, name="gdn_proj_fwd", grid=(t // tm,),
        in_specs=[row(D_MODEL), full(nw), full(w_qkv), full(w_z), full(w_ba)],
        out_specs=[row(GDN_CONV_W), row(D_MODEL), row(LANES), row(D_MODEL)],
        out_shape=[jax.ShapeDtypeStruct((t, GDN_CONV_W), F32),
                   jax.ShapeDtypeStruct((t, D_MODEL), F32),
                   jax.ShapeDtypeStruct((t, LANES), F32),
                   jax.ShapeDtypeStruct((t, D_MODEL), BF16)],
        compiler_params=_params(1),
    )(h, nw, w_qkv, w_z, w_ba)


def _softplus(x):
    return jnp.maximum(x, 0.0) + jnp.log(1.0 + jnp.exp(-jnp.abs(x)))


def _conv_taps(x, halo, cw):
    tm = x.shape[0]
    xx = jnp.concatenate([halo, x], axis=0)
    taps = [xx[8 - (GDN_CONV - 1) + i: 8 - (GDN_CONV - 1) + i + tm] for i in range(GDN_CONV)]
    c = taps[0] * cw[0:1]
    for i in range(1, GDN_CONV):
        c = c + taps[i] * cw[i:i + 1]
    return c, taps


def _gates(ba, gp):
    lane = lax.broadcasted_iota(jnp.int32, ba.shape, 1)
    beta = _sigmoid(ba)
    pre = ba + gp[1:2]
    g = -jnp.exp(gp[0:1]) * _softplus(pre)
    gates = jnp.where(lane < GDN_HEADS, beta, jnp.where(lane < 2 * GDN_HEADS, g, 0.0))
    return lane, beta, pre, g, gates


def _gdn_conv_fwd(x, cw, ba, gp):
    t = x.shape[0]
    tm = _row_tile(t, GDN_CONV_TM)

    def body(x_ref, halo_ref, cw_ref, ba_ref, gp_ref, q_ref, k_ref, v_ref, gates_ref):
        halo = jnp.where(pl.program_id(0) > 0, halo_ref[...], 0.0)
        c, _ = _conv_taps(x_ref[...], halo, cw_ref[...])
        s = c * _sigmoid(c)
        for hh in range(GDN_HEADS):
            sq = s[:, hh * LANES:(hh + 1) * LANES]
            q_ref[:, hh * LANES:(hh + 1) * LANES] = (
                sq * lax.rsqrt(jnp.sum(sq * sq, axis=-1, keepdims=True) + NORM_EPS) * GDN_QSCALE)
            sk = s[:, D_MODEL + hh * LANES:D_MODEL + (hh + 1) * LANES]
            k_ref[:, hh * LANES:(hh + 1) * LANES] = (
                sk * lax.rsqrt(jnp.sum(sk * sk, axis=-1, keepdims=True) + NORM_EPS))
        v_ref[...] = s[:, 2 * D_MODEL:]
        gates_ref[...] = _gates(ba_ref[...], gp_ref[...])[4]

    row = lambda w: pl.BlockSpec((tm, w), lambda i: (i, 0))
    full = lambda a: pl.BlockSpec(a.shape, lambda i: (0, 0))
    halo = pl.BlockSpec((8, GDN_CONV_W), lambda i: (jnp.maximum(i * (tm // 8) - 1, 0), 0))
    return pl.pallas_call(
        body, name="gdn_conv_fwd", grid=(t // tm,),
        in_specs=[row(GDN_CONV_W), halo, full(cw), row(LANES), full(gp)],
        out_specs=[row(D_MODEL), row(D_MODEL), row(D_MODEL), row(LANES)],
        out_shape=[jax.ShapeDtypeStruct((t, D_MODEL), F32)] * 3 + [jax.ShapeDtypeStruct((t, LANES), F32)],
        compiler_params=_params(1),
    )(x, x, cw, ba, gp)


def _split3(x):
    hi = x.astype(BF16).astype(F32)
    r1 = x - hi
    mid = r1.astype(BF16).astype(F32)
    lo = (r1 - mid).astype(BF16).astype(F32)
    return hi, mid, lo


def _chunk_cumsum(col, keep):
    nb, c, _ = col.shape
    l3 = lax.broadcasted_iota(jnp.int32, (nb, c, LANES), 2)
    hi, mid, lo = _split3(col)
    pieces = jnp.where(l3 == 0, hi, jnp.where(l3 == 1, mid, jnp.where(l3 == 2, lo, 0.0))).astype(BF16)
    s = _bnn(jnp.where(keep, 1.0, 0.0).astype(BF16), pieces)
    return s[..., 0:1] + s[..., 1:2] + s[..., 2:3]


def _unit_inverse(nmat):
    nb, c, _ = nmat.shape
    ri = lax.broadcasted_iota(jnp.int32, (nb, c, c), 1)
    ci = lax.broadcasted_iota(jnp.int32, (nb, c, c), 2)
    eye = jnp.where(ri == ci, 1.0, 0.0).astype(F32)
    same = (ri >> GDN_SUB_SHIFT) == (ci >> GDN_SUB_SHIFT)
    nd = jnp.where(same, nmat, 0.0)
    no = nmat - nd
    mm = lambda a, b: _bnn(a.astype(BF16), b.astype(BF16))
    n2 = mm(nd, nd)
    n4 = mm(n2, n2)
    n8 = mm(n4, n4)
    td = mm(mm(mm(eye - nd, eye + n2), eye + n4), eye + n8)
    bm = mm(td, no)
    b2 = mm(bm, bm)
    return mm(mm(eye - bm, eye + b2), td)


class _LaneWindow:
    def __init__(self, ref, j):
        self.ref, self.lanes, self.shape = ref, slice(j * LANES, (j + 1) * LANES), (ref.shape[0], LANES)

    def __getitem__(self, idx):
        return self.ref[:, self.lanes]

    def __setitem__(self, idx, val):
        self.ref[:, self.lanes] = val


def _chunk_terms(q_ref, k_ref, gates_ref, h, transposed):
    rows = k_ref.shape[0]
    nb = rows // GDN_CHUNK
    c = GDN_CHUNK
    lane = lax.broadcasted_iota(jnp.int32, (rows, LANES), 1)
    gt = gates_ref[...]
    beta = jnp.sum(jnp.where(lane == h, gt, 0.0), axis=-1, keepdims=True).reshape(nb, c, 1)
    g = jnp.sum(jnp.where(lane == GDN_HEADS + h, gt, 0.0), axis=-1, keepdims=True).reshape(nb, c, 1)
    ri = lax.broadcasted_iota(jnp.int32, (nb, c, c), 1)
    ci = lax.broadcasted_iota(jnp.int32, (nb, c, c), 2)
    gcol = _chunk_cumsum(g, ri >= ci)
    gamma = jnp.broadcast_to(gcol, (nb, c, LANES))
    l3 = lax.broadcasted_iota(jnp.int32, (nb, c, LANES), 2)
    gh, gm, gl = _split3(gcol)
    pmat = jnp.where(l3 == 0, gh, jnp.where(l3 == 1, gm, jnp.where(l3 == 2, gl, jnp.where(l3 < 6, 1.0, 0.0))))
    qmat = jnp.where(l3 < 3, 1.0, jnp.where(l3 == 3, -gh, jnp.where(l3 == 4, -gm, jnp.where(l3 == 5, -gl, 0.0))))
    pmat, qmat = pmat.astype(BF16), qmat.astype(BF16)
    k = k_ref[...].reshape(nb, c, LANES)
    q = q_ref[...].reshape(nb, c, LANES)
    kb = k * beta
    kbf, kbb, qb = k.astype(BF16), kb.astype(BF16), q.astype(BF16)
    out = dict(beta=beta, g=g, gamma=gamma, k=k, q=q, kb=kb, kbf=kbf, kbb=kbb, qb=qb, ri=ri, ci=ci)
    diff = _bnt(pmat, qmat)
    lmat = jnp.exp(jnp.where(ri >= ci, diff, NEG_BIG))
    out["L"] = lmat
    out["A"] = jnp.where(ri > ci, _bnt(kbb, kbf) * lmat, 0.0)
    out["Aqk"] = jnp.where(ri >= ci, _bnt(qb, kbf) * lmat, 0.0)
    if transposed:
        difft = _bnt(qmat, pmat)
        lt = jnp.exp(jnp.where(ci >= ri, difft, NEG_BIG))
        out["LT"] = lt
        out["AT"] = jnp.where(ci > ri, _bnt(kbf, kbb) * lt, 0.0)
        out["AqkT"] = jnp.where(ci >= ri, _bnt(kbf, qb) * lt, 0.0)
    return out


def _gdn_intra_fwd(q, k, v, gates):
    t = q.shape[0]
    rows = _row_tile(t, GDN_ROWS)
    nb = rows // GDN_CHUNK
    nchunks = t // GDN_CHUNK
    hs = GDN_INTRA_HEADS

    def head(h, hh, q_ref, k_ref, v_ref, gates_ref, u_ref, w_ref, qd_ref, kd_ref, aqk_ref, cd_ref):
        tm_ = _chunk_terms(q_ref, k_ref, gates_ref, h, False)
        gamma, beta = tm_["gamma"], tm_["beta"]
        eg = jnp.exp(gamma)
        tinv = _unit_inverse(tm_["A"])
        v3 = v_ref[...].reshape(nb, GDN_CHUNK, LANES)
        rhs = jnp.concatenate([v3 * beta, tm_["kb"] * eg], axis=-1)
        sol = _bnn(tinv.astype(BF16), rhs.astype(BF16))
        u_ref[...] = sol[..., :LANES].reshape(rows, LANES)
        w_ref[...] = sol[..., LANES:].reshape(rows, LANES).astype(BF16)
        gl = gamma[:, GDN_CHUNK - 1:GDN_CHUNK, :]
        qd_ref[...] = (tm_["q"] * eg).reshape(rows, LANES).astype(BF16)
        kd_ref[...] = (tm_["k"] * jnp.exp(gl - gamma)).reshape(rows, LANES).astype(BF16)
        aqk_ref[hh] = tm_["Aqk"].reshape(rows, GDN_CHUNK).astype(BF16)
        cd_ref[hh] = jnp.broadcast_to(jnp.exp(gl), (nb, 8, LANES)).reshape(nb * 8, LANES)

    def body(q_ref, k_ref, v_ref, gates_ref, u_ref, w_ref, qd_ref, kd_ref, aqk_ref, cd_ref):
        for hh in range(hs):
            win = lambda ref: _LaneWindow(ref, hh)
            head(pl.program_id(1) * hs + hh, hh, win(q_ref), win(k_ref), win(v_ref), gates_ref,
                 win(u_ref), win(w_ref), win(qd_ref), win(kd_ref), aqk_ref, cd_ref)

    blk = pl.BlockSpec((rows, hs * LANES), lambda i, h: (i, h))
    return pl.pallas_call(
        body, name="gdn_intra_fwd", grid=(t // rows, GDN_HEADS // hs),
        in_specs=[blk, blk, blk, pl.BlockSpec((rows, LANES), lambda i, h: (i, 0))],
        out_specs=[blk, blk, blk, blk,
                   pl.BlockSpec((hs, rows, GDN_CHUNK), lambda i, h: (h, i, 0)),
                   pl.BlockSpec((hs, nb * 8, LANES), lambda i, h: (h, i, 0))],
        out_shape=[jax.ShapeDtypeStruct((t, D_MODEL), F32),
                   jax.ShapeDtypeStruct((t, D_MODEL), BF16),
                   jax.ShapeDtypeStruct((t, D_MODEL), BF16),
                   jax.ShapeDtypeStruct((t, D_MODEL), BF16),
                   jax.ShapeDtypeStruct((GDN_HEADS, t, GDN_CHUNK), BF16),
                   jax.ShapeDtypeStruct((GDN_HEADS, nchunks * 8, LANES), F32)],
        compiler_params=_params(2),
    )(q, k, v, gates)


def _gdn_scan_fwd(u, w, qd, kd, aqk, cd):
    t = u.shape[0]
    rows = _row_tile(t, GDN_ROWS)
    nb = rows // GDN_CHUNK
    nchunks = t // GDN_CHUNK
    hs = GDN_SCAN_HEADS

    def body(u_ref, w_ref, qd_ref, kd_ref, aqk_ref, cd_ref, o_ref, st_ref, s_ref):
        @pl.when(pl.program_id(1) == 0)
        def _():
            s_ref[...] = jnp.zeros_like(s_ref)

        states = [s_ref[hh] for hh in range(hs)]
        for c in range(nb):
            r = slice(c * GDN_CHUNK, (c + 1) * GDN_CHUNK)
            for hh in range(hs):
                ln = slice(hh * LANES, (hh + 1) * LANES)
                s = states[hh]
                st_ref[hh, c * LANES:(c + 1) * LANES, :] = s
                sb = s.astype(BF16)
                vb = (u_ref[r, ln] - _nn(w_ref[r, ln], sb)).astype(BF16)
                o_ref[r, ln] = _nn(qd_ref[r, ln], sb) + _nn(aqk_ref[hh, r, :], vb)
                states[hh] = s * cd_ref[hh, c * 8:c * 8 + 1, :] + _tn(kd_ref[r, ln], vb)
        for hh in range(hs):
            s_ref[hh] = states[hh]

    blk = pl.BlockSpec((rows, hs * LANES), lambda h, i: (i, h))
    return pl.pallas_call(
        body, name="gdn_scan_fwd", grid=(GDN_HEADS // hs, t // rows),
        in_specs=[blk, blk, blk, blk,
                  pl.BlockSpec((hs, rows, GDN_CHUNK), lambda h, i: (h, i, 0)),
                  pl.BlockSpec((hs, nb * 8, LANES), lambda h, i: (h, i, 0))],
        out_specs=[blk, pl.BlockSpec((hs, nb * LANES, LANES), lambda h, i: (h, i, 0))],
        out_shape=[jax.ShapeDtypeStruct((t, D_MODEL), F32),
                   jax.ShapeDtypeStruct((GDN_HEADS, nchunks * LANES, LANES), F32)],
        scratch_shapes=[pltpu.VMEM((hs, LANES, LANES), F32)],
        compiler_params=_params(2),
    )(u, w, qd, kd, aqk, cd)


def _gdn_out_fwd(o, z, nw, w_out, res):
    t = o.shape[0]
    tm = _row_tile(t, GDN_TM)

    def body(o_ref, z_ref, nw_ref, w_ref, r_ref, out_ref, gated_ref):
        nwv = nw_ref[...]
        for hh in range(GDN_HEADS):
            sl = slice(hh * LANES, (hh + 1) * LANES)
            _, on = _rms_stats(o_ref[:, sl])
            zv = z_ref[:, sl]
            gated_ref[:, sl] = (on * nwv * (zv * _sigmoid(zv))).astype(BF16)
        out_ref[...] = r_ref[...] + _nn(gated_ref[...], w_ref[...])

    row = pl.BlockSpec((tm, D_MODEL), lambda i: (i, 0))
    full = lambda a: pl.BlockSpec(a.shape, lambda i: (0, 0))
    return pl.pallas_call(
        body, name="gdn_out_fwd", grid=(t // tm,),
        in_specs=[row, row, full(nw), full(w_out), row],
        out_specs=[row, row],
        out_shape=[jax.ShapeDtypeStruct((t, D_MODEL), F32), jax.ShapeDtypeStruct((t, D_MODEL), BF16)],
        compiler_params=_params(1),
    )(o, z, nw, w_out, res)


def _gdn_out_bwd(dy, w_out, o, z, nw):
    t = o.shape[0]
    tm = _row_tile(t, GDN_TM)

    def body(dy_ref, w_ref, o_ref, z_ref, nw_ref, do_ref, dz_ref, dyb_ref, dnw_ref, dgt_ref):
        @pl.when(pl.program_id(0) == 0)
        def _():
            dnw_ref[...] = jnp.zeros_like(dnw_ref)

        dyb = dy_ref[...].astype(BF16)
        dyb_ref[...] = dyb
        dgt_ref[...] = _nt(dyb, w_ref[...])
        nwv = nw_ref[...]
        for hh in range(GDN_HEADS):
            sl = slice(hh * LANES, (hh + 1) * LANES)
            r, on = _rms_stats(o_ref[:, sl])
            zv = z_ref[:, sl]
            sig = _sigmoid(zv)
            sz = zv * sig
            dgt = dgt_ref[:, sl]
            d_on = dgt * nwv * sz
            dz_ref[:, sl] = (dgt * on * nwv * (sig * (1.0 + zv * (1.0 - sig)))).astype(BF16)
            dnw_ref[...] += jnp.sum(dgt * on * sz, axis=0, keepdims=True)
            do_ref[:, sl] = (r * (d_on - on * jnp.mean(d_on * on, axis=-1, keepdims=True))).astype(BF16)

    row = pl.BlockSpec((tm, D_MODEL), lambda i: (i, 0))
    full = lambda a: pl.BlockSpec(a.shape, lambda i: (0, 0))
    return pl.pallas_call(
        body, name="gdn_out_bwd", grid=(t // tm,),
        in_specs=[row, full(w_out), row, row, full(nw)],
        out_specs=[row, row, row, pl.BlockSpec((1, LANES), lambda i: (0, 0))],
        out_shape=[jax.ShapeDtypeStruct((t, D_MODEL), BF16)] * 3 + [jax.ShapeDtypeStruct((1, LANES), F32)],
        scratch_shapes=[pltpu.VMEM((tm, D_MODEL), F32)],
        compiler_params=_params(1),
    )(dy, w_out, o, z, nw)


def _gdn_scan_bwd(u, w, qd, kd, aqk, cd, st, do):
    t = u.shape[0]
    rows = _row_tile(t, GDN_ROWS)
    nb = rows // GDN_CHUNK
    nchunks = t // GDN_CHUNK
    nsteps = t // rows
    cc = GDN_CHUNK
    hs = GDN_SCAN_HEADS

    def body(u_ref, w_ref, qd_ref, kd_ref, aqk_ref, cd_ref, st_ref, do_ref,
             du_ref, dw_ref, dqd_ref, dkd_ref, daqk_ref, daqkt_ref, dcd_ref, ds_ref):
        @pl.when(pl.program_id(1) == 0)
        def _():
            ds_ref[...] = jnp.zeros_like(ds_ref)

        ri = lax.broadcasted_iota(jnp.int32, (cc, cc), 0)
        ci = lax.broadcasted_iota(jnp.int32, (cc, cc), 1)
        dstates = [ds_ref[hh] for hh in range(hs)]
        for c in reversed(range(nb)):
            r = slice(c * cc, (c + 1) * cc)
            for hh in range(hs):
                ln = slice(hh * LANES, (hh + 1) * LANES)
                s = st_ref[hh, c * LANES:(c + 1) * LANES, :]
                sb = s.astype(BF16)
                dsn = dstates[hh]
                dsb = dsn.astype(BF16)
                wv, kdv, qdv, aq, dov = w_ref[r, ln], kd_ref[r, ln], qd_ref[r, ln], aqk_ref[hh, r, :], do_ref[r, ln]
                vb = (u_ref[r, ln] - _nn(wv, sb)).astype(BF16)
                dv = _tn(aq, dov) + _nn(kdv, dsb)
                dvb = dv.astype(BF16)
                daqk_ref[hh, r, :] = jnp.where(ri >= ci, _nt(dov, vb), 0.0)
                daqkt_ref[hh, r, :] = jnp.where(ci >= ri, _nt(vb, dov), 0.0)
                dqd_ref[r, ln] = _nt(dov, sb)
                dkd_ref[r, ln] = _nt(vb, dsb)
                dcd_ref[hh, c * 8:(c + 1) * 8, :] = jnp.broadcast_to(jnp.sum(s * dsn), (8, LANES))
                du_ref[r, ln] = dv
                dw_ref[r, ln] = -_nt(dvb, sb)
                dstates[hh] = _tn(qdv, dov) + dsn * cd_ref[hh, c * 8:c * 8 + 1, :] - _tn(wv, dvb)
        for hh in range(hs):
            ds_ref[hh] = dstates[hh]

    rev = lambda i: nsteps - 1 - i
    blk = pl.BlockSpec((rows, hs * LANES), lambda h, i: (rev(i), h))
    sq = pl.BlockSpec((hs, rows, cc), lambda h, i: (h, rev(i), 0))
    cdb = pl.BlockSpec((hs, nb * 8, LANES), lambda h, i: (h, rev(i), 0))
    return pl.pallas_call(
        body, name="gdn_scan_bwd", grid=(GDN_HEADS // hs, nsteps),
        in_specs=[blk, blk, blk, blk, sq, cdb,
                  pl.BlockSpec((hs, nb * LANES, LANES), lambda h, i: (h, rev(i), 0)), blk],
        out_specs=[blk, blk, blk, blk, sq, sq, cdb],
        out_shape=[jax.ShapeDtypeStruct((t, D_MODEL), F32)] * 4
        + [jax.ShapeDtypeStruct((GDN_HEADS, t, cc), F32)] * 2
        + [jax.ShapeDtypeStruct((GDN_HEADS, nchunks * 8, LANES), F32)],
        scratch_shapes=[pltpu.VMEM((hs, LANES, LANES), F32)],
        compiler_params=_params(2),
    )(u, w, qd, kd, aqk, cd, st, do)


def _gdn_intra_bwd(q, k, v, gates, u, w, du, dw, dqd, dkd, daqk, daqkt, dcd):
    t = q.shape[0]
    rows = _row_tile(t, GDN_ROWS)
    nb = rows // GDN_CHUNK
    cc = GDN_CHUNK
    hs = GDN_INTRA_HEADS

    def head(h, hh, q_ref, k_ref, v_ref, gates_ref, u_ref, w_ref, du_ref, dw_ref, dqd_ref, dkd_ref,
             daqk_ref, daqkt_ref, dcd_ref, dq_ref, dk_ref, dv_ref, dgates_ref):
        tm_ = _chunk_terms(q_ref, k_ref, gates_ref, h, True)
        gamma, beta, kk, qq, kb = tm_["gamma"], tm_["beta"], tm_["k"], tm_["q"], tm_["kb"]
        kbf, kbb, qb = tm_["kbf"], tm_["kbb"], tm_["qb"]
        lmat, lt = tm_["L"], tm_["LT"]
        ri, ci = tm_["ri"], tm_["ci"]
        r3 = lambda ref: ref[...].reshape(nb, cc, LANES)
        eg = jnp.exp(gamma)
        gl = gamma[:, cc - 1:cc, :]
        ekd = jnp.exp(gl - gamma)
        v3 = r3(v_ref)
        tt = _unit_inverse(tm_["AT"])
        dsol = jnp.concatenate([r3(du_ref), r3(dw_ref)], axis=-1)
        sol = jnp.concatenate([r3(u_ref), r3(w_ref).astype(F32)], axis=-1)
        dx = _bnn(tt.astype(BF16), dsol.astype(BF16))
        dxb, solb = dx.astype(BF16), sol.astype(BF16)
        da = jnp.where(ri > ci, -_bnt(dxb, solb), 0.0)
        dat = jnp.where(ci > ri, -_bnt(solb, dxb), 0.0)
        dxu, dxw = dx[..., :LANES], dx[..., LANES:]
        dv_ref[...] = (dxu * beta).reshape(rows, LANES)
        dbeta = jnp.sum(dxu * v3, axis=-1, keepdims=True)
        dkb = dxw * eg
        dgam = jnp.sum(dxw * kb * eg, axis=-1, keepdims=True)
        dkb = dkb + _bnn((da * lmat).astype(BF16), kbf)
        dk = _bnn((dat * lt).astype(BF16), kbb)
        dgam = dgam + jnp.sum(da * tm_["A"], axis=-1, keepdims=True) - jnp.sum(dat * tm_["AT"], axis=-1, keepdims=True)
        daq = daqk_ref[hh].reshape(nb, cc, cc)
        daqt = daqkt_ref[hh].reshape(nb, cc, cc)
        dq = _bnn((daq * lmat).astype(BF16), kbf)
        dk = dk + _bnn((daqt * lt).astype(BF16), qb)
        dgam = dgam + jnp.sum(daq * tm_["Aqk"], axis=-1, keepdims=True) - jnp.sum(daqt * tm_["AqkT"], axis=-1, keepdims=True)
        dqd3, dkd3 = r3(dqd_ref), r3(dkd_ref)
        dq = dq + dqd3 * eg
        dgam = dgam + jnp.sum(dqd3 * qq * eg, axis=-1, keepdims=True)
        dk = dk + dkd3 * ekd
        tk = jnp.sum(dkd3 * kk * ekd, axis=-1, keepdims=True)
        dgam = dgam - tk
        dcdv = dcd_ref[hh].reshape(nb, 8, LANES)[:, 0:1, 0:1]
        dglast = jnp.sum(tk, axis=1, keepdims=True) + dcdv * jnp.exp(gl[:, :, 0:1])
        rowi = lax.broadcasted_iota(jnp.int32, (nb, cc, 1), 1)
        dgam = dgam + jnp.where(rowi == cc - 1, dglast, 0.0)
        dk = dk + dkb * beta
        dbeta = dbeta + jnp.sum(dkb * kk, axis=-1, keepdims=True)
        dg = _chunk_cumsum(dgam, ci >= ri)
        dq_ref[...] = dq.reshape(rows, LANES)
        dk_ref[...] = dk.reshape(rows, LANES)
        lane = lax.broadcasted_iota(jnp.int32, (rows, LANES), 1)
        dgates_ref[...] += (jnp.where(lane == h, dbeta.reshape(rows, 1), 0.0)
                            + jnp.where(lane == GDN_HEADS + h, dg.reshape(rows, 1), 0.0))

    def body(q_ref, k_ref, v_ref, gates_ref, u_ref, w_ref, du_ref, dw_ref, dqd_ref, dkd_ref,
             daqk_ref, daqkt_ref, dcd_ref, dq_ref, dk_ref, dv_ref, dgates_ref):
        @pl.when(pl.program_id(1) == 0)
        def _():
            dgates_ref[...] = jnp.zeros_like(dgates_ref)

        for hh in range(hs):
            win = lambda ref: _LaneWindow(ref, hh)
            head(pl.program_id(1) * hs + hh, hh, win(q_ref), win(k_ref), win(v_ref), gates_ref, win(u_ref),
                 win(w_ref), win(du_ref), win(dw_ref), win(dqd_ref), win(dkd_ref), daqk_ref, daqkt_ref, dcd_ref,
                 win(dq_ref), win(dk_ref), win(dv_ref), dgates_ref)

    blk = pl.BlockSpec((rows, hs * LANES), lambda i, h: (i, h))
    shared = pl.BlockSpec((rows, LANES), lambda i, h: (i, 0))
    sq = pl.BlockSpec((hs, rows, cc), lambda i, h: (h, i, 0))
    return pl.pallas_call(
        body, name="gdn_intra_bwd", grid=(t // rows, GDN_HEADS // hs),
        in_specs=[blk, blk, blk, shared, blk, blk, blk, blk, blk, blk, sq, sq,
                  pl.BlockSpec((hs, nb * 8, LANES), lambda i, h: (h, i, 0))],
        out_specs=[blk, blk, blk, shared],
        out_shape=[jax.ShapeDtypeStruct((t, D_MODEL), F32)] * 3 + [jax.ShapeDtypeStruct((t, LANES), F32)],
        compiler_params=_params(2),
    )(q, k, v, gates, u, w, du, dw, dqd, dkd, daqk, daqkt, dcd)


def _gdn_conv_bwd_a(x, cw, ba, gp, dq, dk, dv, dgates):
    t = x.shape[0]
    tm = _row_tile(t, GDN_CONV_TM)

    def body(x_ref, halo_ref, cw_ref, ba_ref, gp_ref, dq_ref, dk_ref, dv_ref, dgates_ref,
             dc_ref, dba_ref, dcw_ref, dgp_ref, ds_ref):
        @pl.when(pl.program_id(0) == 0)
        def _():
            dcw_ref[...] = jnp.zeros_like(dcw_ref)
            dgp_ref[...] = jnp.zeros_like(dgp_ref)

        halo = jnp.where(pl.program_id(0) > 0, halo_ref[...], 0.0)
        c, taps = _conv_taps(x_ref[...], halo, cw_ref[...])
        sig = _sigmoid(c)
        s = c * sig
        for hh in range(GDN_HEADS):
            sl = slice(hh * LANES, (hh + 1) * LANES)
            sq = s[:, sl]
            rq = lax.rsqrt(jnp.sum(sq * sq, axis=-1, keepdims=True) + NORM_EPS)
            qh = sq * rq
            dqv = dq_ref[:, sl]
            ds_ref[:, sl] = GDN_QSCALE * rq * (dqv - qh * jnp.sum(dqv * qh, axis=-1, keepdims=True))
            sl2 = slice(D_MODEL + hh * LANES, D_MODEL + (hh + 1) * LANES)
            sk = s[:, sl2]
            rk = lax.rsqrt(jnp.sum(sk * sk, axis=-1, keepdims=True) + NORM_EPS)
            kh = sk * rk
            dkv = dk_ref[:, sl]
            ds_ref[:, sl2] = rk * (dkv - kh * jnp.sum(dkv * kh, axis=-1, keepdims=True))
        ds_ref[:, 2 * D_MODEL:] = dv_ref[...]
        dc = ds_ref[...] * (sig * (1.0 + c * (1.0 - sig)))
        dc_ref[...] = dc
        for i in range(GDN_CONV):
            dcw_ref[i:i + 1, :] += jnp.sum(dc * taps[i], axis=0, keepdims=True)
        lane, beta, pre, g, _ = _gates(ba_ref[...], gp_ref[...])
        dgt = dgates_ref[...]
        db = dgt * beta * (1.0 - beta)
        dpre = dgt * (-jnp.exp(gp_ref[0:1, :])) * _sigmoid(pre)
        isa = jnp.logical_and(lane >= GDN_HEADS, lane < 2 * GDN_HEADS)
        dba_ref[...] = jnp.where(lane < GDN_HEADS, db, jnp.where(isa, dpre, 0.0)).astype(BF16)
        dgp_ref[0:1, :] += jnp.sum(jnp.where(isa, dgt * g, 0.0), axis=0, keepdims=True)
        dgp_ref[1:2, :] += jnp.sum(jnp.where(isa, dpre, 0.0), axis=0, keepdims=True)

    row = lambda w: pl.BlockSpec((tm, w), lambda i: (i, 0))
    full = lambda a: pl.BlockSpec(a.shape, lambda i: (0, 0))
    halo = pl.BlockSpec((8, GDN_CONV_W), lambda i: (jnp.maximum(i * (tm // 8) - 1, 0), 0))
    return pl.pallas_call(
        body, name="gdn_conv_bwd_a", grid=(t // tm,),
        in_specs=[row(GDN_CONV_W), halo, full(cw), row(LANES), full(gp), row(D_MODEL), row(D_MODEL),
                  row(D_MODEL), row(LANES)],
        out_specs=[row(GDN_CONV_W), row(LANES), pl.BlockSpec((8, GDN_CONV_W), lambda i: (0, 0)),
                   pl.BlockSpec((8, LANES), lambda i: (0, 0))],
        out_shape=[jax.ShapeDtypeStruct((t, GDN_CONV_W), F32),
                   jax.ShapeDtypeStruct((t, LANES), BF16),
                   jax.ShapeDtypeStruct((8, GDN_CONV_W), F32),
                   jax.ShapeDtypeStruct((8, LANES), F32)],
        scratch_shapes=[pltpu.VMEM((tm, GDN_CONV_W), F32)],
        compiler_params=_params(1),
    )(x, x, cw, ba, gp, dq, dk, dv, dgates)


def _gdn_conv_bwd_b(dc, cw):
    t = dc.shape[0]
    tm = _row_tile(t, GDN_CONV_TM)
    nsteps = t // tm

    def body(dc_ref, halo_ref, cw_ref, dx_ref):
        halo = jnp.where(pl.program_id(0) < nsteps - 1, halo_ref[...], 0.0)
        dd = jnp.concatenate([dc_ref[...], halo], axis=0)
        cw_v = cw_ref[...]
        acc = dd[GDN_CONV - 1:GDN_CONV - 1 + tm] * cw_v[0:1]
        for i in range(1, GDN_CONV):
            acc = acc + dd[GDN_CONV - 1 - i:GDN_CONV - 1 - i + tm] * cw_v[i:i + 1]
        dx_ref[...] = acc.astype(BF16)

    row = pl.BlockSpec((tm, GDN_CONV_W), lambda i: (i, 0))
    halo = pl.BlockSpec((8, GDN_CONV_W), lambda i: (jnp.minimum((i + 1) * (tm // 8), t // 8 - 1), 0))
    return pl.pallas_call(
        body, name="gdn_conv_bwd_b", grid=(nsteps,),
        in_specs=[row, halo, pl.BlockSpec(cw.shape, lambda i: (0, 0))],
        out_specs=row,
        out_shape=jax.ShapeDtypeStruct((t, GDN_CONV_W), BF16),
        compiler_params=_params(1),
    )(dc, dc, cw)


def _gdn_proj_bwd(dy, h, nw, w_qkv, w_z, w_ba, dx, dz, dba):
    t = h.shape[0]
    tm = _row_tile(t, GDN_TM)

    def body(dy_ref, h_ref, nw_ref, wq_ref, wz_ref, wb_ref, dx_ref, dz_ref, dba_ref, dh_ref, dnw_ref):
        @pl.when(pl.program_id(0) == 0)
        def _():
            dnw_ref[...] = jnp.zeros_like(dnw_ref)

        dhn = _nt(dx_ref[...], wq_ref[...]) + _nt(dz_ref[...], wz_ref[...]) + _nt(dba_ref[...], wb_ref[...])
        dxx, dnw = _norm_bwd(h_ref[...], nw_ref[...], dhn)
        dh_ref[...] = dy_ref[...] + dxx
        dnw_ref[...] += dnw

    row = lambda w: pl.BlockSpec((tm, w), lambda i: (i, 0))
    full = lambda a: pl.BlockSpec(a.shape, lambda i: (0, 0))
    return pl.pallas_call(
        body, name="gdn_proj_bwd", grid=(t // tm,),
        in_specs=[row(D_MODEL), row(D_MODEL), full(nw), full(w_qkv), full(w_z), full(w_ba),
                  row(GDN_CONV_W), row(D_MODEL), row(LANES)],
        out_specs=[row(D_MODEL), pl.BlockSpec((1, D_MODEL), lambda i: (0, 0))],
        out_shape=[jax.ShapeDtypeStruct((t, D_MODEL), F32), jax.ShapeDtypeStruct((1, D_MODEL), F32)],
        compiler_params=_params(1),
    )(dy, h, nw, w_qkv, w_z, w_ba, dx, dz, dba)


def _loss_head(h, nw, target):
    t = h.shape[0]
    tm = _row_tile(t, 512)

    def body(h_ref, nw_ref, t_ref, loss_ref, dh_ref, dnw_ref):
        @pl.when(pl.program_id(0) == 0)
        def _():
            loss_ref[...] = jnp.zeros_like(loss_ref)
            dnw_ref[...] = jnp.zeros_like(dnw_ref)

        x = h_ref[...]
        nwv = nw_ref[...]
        _, xhat = _rms_stats(x)
        err = xhat * nwv - t_ref[...]
        loss_ref[...] += 0.5 * jnp.sum(jnp.mean(err * err, axis=-1, keepdims=True))
        dx, dnw = _norm_bwd(x, nwv, err * (1.0 / D_MODEL))
        dh_ref[...] = dx
        dnw_ref[...] += dnw

    row = pl.BlockSpec((tm, D_MODEL), lambda i: (i, 0))
    return pl.pallas_call(
        body, name="loss_head", grid=(t // tm,),
        in_specs=[row, pl.BlockSpec((1, D_MODEL), lambda i: (0, 0)), row],
        out_specs=[pl.BlockSpec((8, LANES), lambda i: (0, 0)), row, pl.BlockSpec((1, D_MODEL), lambda i: (0, 0))],
        out_shape=[jax.ShapeDtypeStruct((8, LANES), F32),
                   jax.ShapeDtypeStruct((t, D_MODEL), F32),
                   jax.ShapeDtypeStruct((1, D_MODEL), F32)],
        compiler_params=_params(1),
    )(h, nw, target)


_PEER_FLIPS = [(dx, dy, dc) for dx in (0, 1) for dy in (0, 1) for dc in (0, 1)][1:]


_ANY = pl.BlockSpec(memory_space=pl.ANY)


def _exchange_copies(ins, outs, send_sems, recv_sems, local_sems, scatter):
    x, y, c = lax.axis_index("x"), lax.axis_index("y"), lax.axis_index("c")
    me = 4 * x + 2 * y + c
    copies = []
    for a in range(len(ins)):
        src = ins[a].at[me] if scatter else ins[a]
        copies.append(pltpu.make_async_copy(src, outs[a].at[me], local_sems.at[a]))
    for k, (fx, fy, fc) in enumerate(_PEER_FLIPS):
        px, py, pc = lax.rem(x + fx, 2), lax.rem(y + fy, 2), lax.rem(c + fc, 2)
        peer = 4 * px + 2 * py + pc
        for a in range(len(ins)):
            copies.append(pltpu.make_async_remote_copy(
                src_ref=ins[a].at[peer] if scatter else ins[a],
                dst_ref=outs[a].at[me],
                send_sem=send_sems.at[a, k], recv_sem=recv_sems.at[a, k],
                device_id=(px, py, pc), device_id_type=MESH_ID))
    return copies


def _exchange_shapes(arrs, scatter):
    return [jax.ShapeDtypeStruct((N_DEV,) + (a.shape[1:] if scatter else a.shape), a.dtype) for a in arrs]


def _exchange_sems(n):
    npeer = len(_PEER_FLIPS)
    return [pltpu.SemaphoreType.DMA((n, npeer)), pltpu.SemaphoreType.DMA((n, npeer)),
            pltpu.SemaphoreType.DMA((n,))]


def _exchange(arrs, scatter, name):
    n = len(arrs)

    def body(*refs):
        copies = _exchange_copies(refs[:n], refs[n:2 * n], *refs[2 * n:], scatter)
        for cp in copies:
            cp.start()
        for cp in copies:
            cp.wait()

    return pl.pallas_call(
        body, name=name, in_specs=[_ANY] * n, out_specs=[_ANY] * n,
        out_shape=_exchange_shapes(arrs, scatter), scratch_shapes=_exchange_sems(n),
    )(*arrs)


ADAMW_PART_BLOCK_BYTES = 2 * 1024 * 1024


def _adamw(parts, w, m, v, name):
    nl, r, c = w.shape
    cap = max(8, ADAMW_PART_BLOCK_BYTES // (N_DEV * c * parts[0].dtype.itemsize))
    tr = r
    for cand in range(min(r, cap) // 8 * 8, 7, -8):
        if r % cand == 0:
            tr = cand
            break
    nr = r // tr
    c1 = 1.0 - ADAM_B1 ** ADAM_STEP
    c2 = 1.0 - ADAM_B2 ** ADAM_STEP

    def body(*refs):
        p_refs = refs[:nl]
        w_ref, m_ref, v_ref, g_ref, d_ref, nm_ref, nv_ref = refs[nl:]
        layer = pl.program_id(0)
        for k in range(nl):
            @pl.when(layer == k)
            def _():
                g = p_refs[k][0].astype(F32)
                for s in range(1, N_DEV):
                    g = g + p_refs[k][s].astype(F32)
                nm = ADAM_B1 * m_ref[0] + (1.0 - ADAM_B1) * g
                nv = ADAM_B2 * v_ref[0] + (1.0 - ADAM_B2) * (g * g)
                g_ref[0] = g
                nm_ref[0] = nm
                nv_ref[0] = nv
                d_ref[0] = -ADAM_LR * ((nm / c1) / (jnp.sqrt(nv / c2) + ADAM_EPS) + ADAM_WD * w_ref[0])

    def part_spec(k):
        return pl.BlockSpec((N_DEV, tr, c),
                            lambda l, i: (0, jnp.where(l == k, i, jnp.where(l < k, 0, nr - 1)), 0))

    blk = pl.BlockSpec((1, tr, c), lambda l, i: (l, i, 0))
    return pl.pallas_call(
        body, name=name, grid=(nl, nr),
        in_specs=[part_spec(k) for k in range(nl)] + [blk, blk, blk],
        out_specs=[blk] * 4,
        out_shape=[jax.ShapeDtypeStruct((nl, r, c), F32)] * 4,
        compiler_params=_params(2),
    )(*parts, w, m, v)


def _rope_tables(positions):
    half = ROPE_DIM // 2
    inv_freq = ROPE_THETA ** (-jnp.arange(0, ROPE_DIM, 2, dtype=F32) / ROPE_DIM)
    ang = positions.astype(F32)[:, None] * inv_freq
    cos, sin = jnp.cos(ang), jnp.sin(ang)
    t = positions.shape[0]
    zeros = lambda w: jnp.zeros((t, w), F32)
    c = jnp.concatenate([cos, cos, jnp.ones((t, ATTN_HEAD_DIM - ROPE_DIM), F32)], axis=1)
    s1 = jnp.concatenate([-sin, zeros(ATTN_HEAD_DIM - half)], axis=1)
    s2 = jnp.concatenate([zeros(half), sin, zeros(ATTN_HEAD_DIM - ROPE_DIM)], axis=1)
    return tuple(jnp.tile(a, (1, LANES // ATTN_HEAD_DIM)) for a in (c, s1, s2))


def _ffn_layer_fwd(h, nw, w, carry):
    (out, g, u, hn), landed = _ffn_fwd(h, nw, w["wg"], w["wu"], w["wd"], carry, False)
    return out, (h, g, u, hn), landed


def _ffn_layer_bwd(dy, saved, nw, w, carry, early=None):
    h, g, u, hn = saved
    (dh, dg, du, act, dz, dnw), landed = _ffn_bwd(dy, h, nw, g, u, w["wg"], w["wu"], w["wd"], carry, True)
    dwg = _matmul_tn(hn, dg, FFN_TF, "ffn_dwg")
    dwu = _matmul_tn(hn, du, FFN_TF, "ffn_dwu")
    if early is None:
        dwd, early_landed = _matmul_tn(act, dz, D_MODEL, "ffn_dwd"), ()
    else:
        dwd, early_landed = _matmul_tn(act, dz, D_MODEL, "ffn_dwd", early(dwg, dwu))
    return dh, dnw, (dwg, dwu, dwd), landed, early_landed


def _attn_layer_fwd(h, nw, w, ropes):
    q, k, v, hn = _attn_qkv_fwd(h, nw, w["w_in"], w["b_in"], *ropes)
    o = _attn_core_fwd(q, k, v, w["sinks"])
    out = _proj_out_fwd(o, w["w_out"], w["b_out"], h)
    return out, (h, hn, q, k, v, o)


def _attn_layer_bwd(dy, saved, nw, w, ropes):
    h, hn, q, k, v, o = saved
    do, dyb, db_out = _proj_out_bwd(dy, w["w_out"])
    dw_out = _matmul_tn(o, dyb, D_MODEL, "attn_dw_out")
    dq, dk, dv, dsink = _attn_core_bwd(q, k, v, do, w["sinks"])
    dh, dqkv, db_in, dnw = _attn_qkv_bwd(dy, h, nw, w["w_in"], dq, dk, dv, *ropes)
    dw_in = _matmul_tn(hn, dqkv, ATTN_IN, "attn_dw_in")
    return dh, dnw, dict(w_in=dw_in, b_in=db_in, sinks=dsink[:, 0], w_out=dw_out, b_out=db_out)


def _gdn_layer_fwd(h, nw, w):
    x, z, ba, hn = _gdn_proj_fwd(h, nw, w["w_qkv"], w["w_z"], w["w_ba"])
    q, k, v, gates = _gdn_conv_fwd(x, w["conv_w"], ba, w["gp"])
    u, ww, qd, kd, aqk, cd = _gdn_intra_fwd(q, k, v, gates)
    o, st = _gdn_scan_fwd(u, ww, qd, kd, aqk, cd)
    out, gated = _gdn_out_fwd(o, z, w["norm_w"], w["w_out"], h)
    return out, (h, hn, x, z, ba, q, k, v, gates, u, ww, qd, kd, aqk, cd, st, o, gated)


def _gdn_layer_bwd(dy, saved, nw, w):
    h, hn, x, z, ba, q, k, v, gates, u, ww, qd, kd, aqk, cd, st, o, gated = saved
    do, dz, dyb, dnorm_w = _gdn_out_bwd(dy, w["w_out"], o, z, w["norm_w"])
    dw_out = _matmul_tn(gated, dyb, D_MODEL, "gdn_dw_out")
    du, dw, dqd, dkd, daqk, daqkt, dcd = _gdn_scan_bwd(u, ww, qd, kd, aqk, cd, st, do)
    dq, dk, dv, dgates = _gdn_intra_bwd(q, k, v, gates, u, ww, du, dw, dqd, dkd, daqk, daqkt, dcd)
    dc, dba, dcw, dgp = _gdn_conv_bwd_a(x, w["conv_w"], ba, w["gp"], dq, dk, dv, dgates)
    dx = _gdn_conv_bwd_b(dc, w["conv_w"])
    dh, dnw = _gdn_proj_bwd(dy, h, nw, w["w_qkv"], w["w_z"], w["w_ba"], dx, dz, dba)
    dw_qkv = _matmul_tn(hn, dx, GDN_CONV_W // 2, "gdn_dw_qkv")
    dw_z = _matmul_tn(hn, dz, D_MODEL, "gdn_dw_z")
    dw_ba = _matmul_tn(hn, dba, LANES, "gdn_dw_ba")
    dw_in = jnp.concatenate([dw_qkv, dw_z, dw_ba[:, :2 * GDN_HEADS]], axis=1)
    lo, hi = GDN_HEADS, 2 * GDN_HEADS
    return dh, dnw, dict(w_in=dw_in, conv_w=dcw[:GDN_CONV], A_log=dgp[0, lo:hi], dt_bias=dgp[1, lo:hi],
                         norm_w=dnorm_w[0], w_out=dw_out)


def _local_step(x, positions, target, norms, final_norm, plan):
    ropes = _rope_tables(positions)
    h = x
    saved = []
    for layer in range(DEPTH):
        h, s1, landed = _ffn_layer_fwd(h, norms["ffn1"][layer], plan.weights("ffn1", layer),
                                       plan.fwd_carry("ffn1", layer))
        plan.fwd_landed(landed)
        if layer % 2 == 0:
            h, s2 = _attn_layer_fwd(h, norms["mix"][layer], plan.weights("mix", layer), ropes)
        else:
            h, s2 = _gdn_layer_fwd(h, norms["mix"][layer], plan.weights("mix", layer))
        h, s3, landed = _ffn_layer_fwd(h, norms["ffn2"][layer], plan.weights("ffn2", layer),
                                       plan.fwd_carry("ffn2", layer))
        plan.fwd_landed(landed)
        saved.append((s1, s2, s3))
    loss, dh, dfinal = _loss_head(h, final_norm, target)

    g_norm = {k: [None] * DEPTH for k in ("ffn1", "mix", "ffn2")}
    for layer in reversed(range(DEPTH)):
        s1, s2, s3 = saved[layer]
        dh, g_norm["ffn2"][layer], gw, landed, _ = _ffn_layer_bwd(
            dh, s3, norms["ffn2"][layer], plan.weights("ffn2", layer), plan.bwd_carry("ffn2", layer))
        plan.bwd_landed(landed)
        plan.grads("ffn2", layer, gw)
        if layer % 2 == 0:
            dh, g_norm["mix"][layer], gw = _attn_layer_bwd(dh, s2, norms["mix"][layer], plan.weights("mix", layer),
                                                           ropes)
        else:
            dh, g_norm["mix"][layer], gw = _gdn_layer_bwd(dh, s2, norms["mix"][layer], plan.weights("mix", layer))
        plan.grads("mix", layer, gw)
        dh, g_norm["ffn1"][layer], gw, landed, early_landed = _ffn_layer_bwd(
            dh, s1, norms["ffn1"][layer], plan.weights("ffn1", layer), plan.bwd_carry("ffn1", layer),
            plan.early_parts("ffn1", layer))
        plan.bwd_landed(landed)
        plan.early_landed("ffn1", layer, early_landed)
        plan.grads("ffn1", layer, gw)
    return loss, dh, dfinal, g_norm


def _cols_full(g):
    return jnp.transpose(g, (1, 0, 2)).reshape(g.shape[1], -1)


def _rows_full(g):
    return g.reshape(-1, g.shape[-1])


class _ShardedWeights:
    def __init__(self, shards, small):
        self.shards, self.small = shards, small
        self.whole, self.pending, self.received, self.small_grads, self.early = {}, {}, {}, {}, {}
        self.in_flight = []
        first = [("ffn1", 0)]
        self.in_flight = first
        self.fwd_landed(_exchange(self._shards_of(first), False, "gather_first"))

    def _group(self, key):
        kind, layer = key
        j = layer // 2
        if kind != "mix":
            return [self.shards[kind + "_w_gate_up"][layer], self.shards[kind + "_w_down"][layer]]
        if layer % 2 == 0:
            return [self.shards["attn_w_in"][j], self.shards["attn_w_out"][j]]
        return [self.shards["gdn_w_in"][j], self.shards["gdn_w_out"][j], self.shards["gdn_conv_w"][j]]

    def _shards_of(self, keys):
        return [a for key in keys for a in self._group(key)]

    def weights(self, kind, layer):
        return self.whole[(kind, layer)]

    def fwd_carry(self, kind, layer):
        if kind == "ffn1":
            keys = [("mix", 0), ("ffn2", 0)] if layer == 0 else [("ffn2", layer)]
        else:
            keys = [("ffn1", layer + 1), ("mix", layer + 1)] if layer + 1 < DEPTH else []
        self.in_flight = keys
        return self._shards_of(keys)

    def fwd_landed(self, landed):
        landed = list(landed)
        for key in self.in_flight:
            kind, layer = key
            j = layer // 2
            sm = self.small
            if kind != "mix":
                gu, dn = landed[:2]
                w = dict(wg=_cols_full(gu[:4]), wu=_cols_full(gu[4:]), wd=_rows_full(dn))
                landed = landed[2:]
            elif layer % 2 == 0:
                w = dict(w_in=_cols_full(landed[0]), w_out=_rows_full(landed[1]), b_in=sm["attn_b_in"][j][None],
                         sinks=sm["attn_sinks"][j][None], b_out=sm["attn_b_out"][j][None])
                landed = landed[2:]
            else:
                w_in = _cols_full(landed[0])
                w_ba = jnp.pad(w_in[:, GDN_CONV_W + D_MODEL:], ((0, 0), (0, LANES - 2 * GDN_HEADS)))
                gp = jnp.pad(jnp.stack([sm["gdn_A_log"][j], sm["gdn_dt_bias"][j]]),
                             ((0, 6), (GDN_HEADS, LANES - 2 * GDN_HEADS)))
                w = dict(w_qkv=w_in[:, :GDN_CONV_W], w_z=w_in[:, GDN_CONV_W:GDN_CONV_W + D_MODEL], w_ba=w_ba,
                         conv_w=_cols_full(landed[2]), gp=gp, norm_w=sm["gdn_norm_w"][j][None],
                         w_out=_rows_full(landed[1]))
                landed = landed[3:]
            self.whole[key] = w
        self.in_flight = []

    @staticmethod
    def _gate_up_part(dwg, dwu):
        return jnp.concatenate([_cols_shards(dwg, 4), _cols_shards(dwu, 4)], axis=0).astype(BF16)

    def early_parts(self, kind, layer):
        if (kind, layer) != ("ffn1", 0):
            return None
        return lambda dwg, dwu: [self._gate_up_part(dwg, dwu)]

    def early_landed(self, kind, layer, landed):
        if len(landed):
            self.early[(kind, layer)] = list(landed)

    def grads(self, kind, layer, g):
        if kind != "mix":
            dwg, dwu, dwd = g
            parts = [_rows_shards(dwd).astype(BF16)]
            if (kind, layer) not in self.early:
                parts = [self._gate_up_part(dwg, dwu)] + parts
        else:
            parts = [_cols_shards(g["w_in"]).astype(BF16), _rows_shards(g["w_out"]).astype(BF16)]
            if layer % 2 == 1:
                parts.append(_cols_shards(g["conv_w"]))
            self.small_grads[layer] = g
        self.pending[(kind, layer)] = parts

    def bwd_carry(self, kind, layer):
        if kind == "ffn1":
            keys = [("ffn2", layer), ("mix", layer)]
        else:
            keys = [("ffn1", layer + 1)] if layer + 1 < DEPTH else []
        self.in_flight = keys
        return [a for key in keys for a in self.pending[key]]

    def bwd_landed(self, landed):
        landed = list(landed)
        for key in self.in_flight:
            n = len(self.pending.pop(key))
            self.received[key], landed = landed[:n], landed[n:]
        self.in_flight = []

    def finish(self):
        self.in_flight = list(self.pending)
        self.bwd_landed(_exchange([a for key in self.in_flight for a in self.pending[key]], True, "scatter_last"))
        for key, landed in self.early.items():
            self.received[key] = landed + self.received[key]


def _cols_shards(full, n=N_DEV):
    r = full.shape[0]
    return jnp.transpose(full.reshape(r, n, -1), (1, 0, 2))


def _rows_shards(full):
    return full.reshape(N_DEV, -1, full.shape[-1])


_SMALL = ("ffn1_norm", "mix_norm", "ffn2_norm", "attn_b_in", "attn_sinks", "attn_b_out",
          "gdn_A_log", "gdn_dt_bias", "gdn_norm_w", "final_norm")
_BIG = ("ffn1_w_gate_up", "ffn1_w_down", "ffn2_w_gate_up", "ffn2_w_down", "attn_w_in", "attn_w_out",
        "gdn_w_in", "gdn_conv_w", "gdn_w_out")
_WEIGHTS = ("ffn1_norm", "ffn1_w_gate_up", "ffn1_w_down", "mix_norm", "ffn2_norm", "ffn2_w_gate_up",
            "ffn2_w_down", "attn_w_in", "attn_b_in", "attn_sinks", "attn_w_out", "attn_b_out", "gdn_w_in",
            "gdn_conv_w", "gdn_A_log", "gdn_dt_bias", "gdn_norm_w", "gdn_w_out", "final_norm")


def _pack_small(vals):
    flat = jnp.concatenate([v.reshape(-1).astype(F32) for v in vals])
    pad = (-flat.shape[0]) % (8 * LANES)
    return jnp.pad(flat, (0, pad)).reshape(-1, LANES)


def _unpack_small(packed, shapes):
    flat = packed.reshape(-1)
    out, off = [], 0
    for s in shapes:
        size = 1
        for d in s:
            size *= d
        out.append(flat[off:off + size].reshape(s))
        off += size
    return out


def kernel(x, positions, ffn1_norm, ffn1_w_gate_up, ffn1_w_down, mix_norm, ffn2_norm, ffn2_w_gate_up, ffn2_w_down, attn_w_in, attn_b_in, attn_sinks, attn_w_out, attn_b_out, gdn_w_in, gdn_conv_w, gdn_A_log, gdn_dt_bias, gdn_norm_w, gdn_w_out, final_norm, loss_target, m_ffn1_norm, m_ffn1_w_gate_up, m_ffn1_w_down, m_mix_norm, m_ffn2_norm, m_ffn2_w_gate_up, m_ffn2_w_down, m_attn_w_in, m_attn_b_in, m_attn_sinks, m_attn_w_out, m_attn_b_out, m_gdn_w_in, m_gdn_conv_w, m_gdn_A_log, m_gdn_dt_bias, m_gdn_norm_w, m_gdn_w_out, m_final_norm, v_ffn1_norm, v_ffn1_w_gate_up, v_ffn1_w_down, v_mix_norm, v_ffn2_norm, v_ffn2_w_gate_up, v_ffn2_w_down, v_attn_w_in, v_attn_b_in, v_attn_sinks, v_attn_w_out, v_attn_b_out, v_gdn_w_in, v_gdn_conv_w, v_gdn_A_log, v_gdn_dt_bias, v_gdn_norm_w, v_gdn_w_out, v_final_norm):
    args = dict(locals())
    wts = {n: args[n] for n in _WEIGHTS}
    moms = {n: args["m_" + n] for n in _WEIGHTS}
    vels = {n: args["v_" + n] for n in _WEIGHTS}

    shards = {n: wts[n] if n == "gdn_conv_w" else wts[n].astype(BF16) for n in _BIG}
    plan = _ShardedWeights(shards, wts)
    norms = dict(ffn1=ffn1_norm[:, None, :], mix=mix_norm[:, None, :], ffn2=ffn2_norm[:, None, :])
    loss, grad_x, dfinal, g_norm = _local_step(x[0], positions[0], loss_target[0], norms, final_norm[None], plan)
    plan.finish()

    layers = range(DEPTH)
    got = lambda kind, ls, i: [plan.received[(kind, l)][i] for l in ls]
    received = {"attn_w_in": got("mix", layers[0::2], 0), "attn_w_out": got("mix", layers[0::2], 1),
                "gdn_w_in": got("mix", layers[1::2], 0), "gdn_w_out": got("mix", layers[1::2], 1),
                "gdn_conv_w": got("mix", layers[1::2], 2)}
    for kind in ("ffn1", "ffn2"):
        received[kind + "_w_gate_up"] = got(kind, layers, 0)
        received[kind + "_w_down"] = got(kind, layers, 1)
    g_attn = [plan.small_grads[l] for l in layers[0::2]]
    g_gdn = [plan.small_grads[l] for l in layers[1::2]]


    small_g = dict(
        ffn1_norm=jnp.concatenate(g_norm["ffn1"], axis=0), mix_norm=jnp.concatenate(g_norm["mix"], axis=0),
        ffn2_norm=jnp.concatenate(g_norm["ffn2"], axis=0),
        attn_b_in=jnp.concatenate([g["b_in"] for g in g_attn], axis=0),
        attn_sinks=jnp.stack([g["sinks"] for g in g_attn]),
        attn_b_out=jnp.concatenate([g["b_out"] for g in g_attn], axis=0),
        gdn_A_log=jnp.stack([g["A_log"] for g in g_gdn]), gdn_dt_bias=jnp.stack([g["dt_bias"] for g in g_gdn]),
        gdn_norm_w=jnp.stack([g["norm_w"] for g in g_gdn]), final_norm=dfinal[0])
    small_parts = _exchange([_pack_small([small_g[n] for n in _SMALL] + [loss[0, :1]])], False, "gather_small")[0]
    pad1 = jnp.zeros((1,), F32)
    sw = _pack_small([wts[n] for n in _SMALL] + [pad1])
    sm = _pack_small([moms[n] for n in _SMALL] + [pad1])
    sv = _pack_small([vels[n] for n in _SMALL] + [pad1])
    shapes = [wts[n].shape for n in _SMALL] + [(1,)]
    small_out = [_unpack_small(o[0], shapes)
                 for o in _adamw([small_parts], sw[None], sm[None], sv[None], "adamw_small")]
    results = {n: tuple(o[i] for o in small_out) for i, n in enumerate(_SMALL)}
    loss_total = small_out[0][-1][0]

    for n in _BIG:
        results[n] = tuple(_adamw(received[n], wts[n], moms[n], vels[n], "adamw_" + n))

    return (loss_total, grad_x[None],
            *[results[n][0] for n in _WEIGHTS], *[results[n][1] for n in _WEIGHTS],
            *[results[n][2] for n in _WEIGHTS], *[results[n][3] for n in _WEIGHTS])
```

```python
import jax
import jax.numpy as jnp
from jax import lax
from jax.experimental import pallas as pl
from jax.experimental.pallas import tpu as pltpu

F32 = jnp.float32
BF16 = jnp.bfloat16

D_MODEL = 1024
DEPTH = 4
D_FF = 2816
NORM_EPS = 1e-6
N_DEV = 8

ATTN_Q_HEADS = 16
ATTN_KV_HEADS = 4
ATTN_HEAD_DIM = 64
ATTN_GROUP = 4
ATTN_BLOCK = 128
ROPE_DIM = 16
ROPE_THETA = 500000.0
ATTN_Q_W = 1024
ATTN_KV_W = 256
ATTN_IN = 1536
ATTN_SCALE = ATTN_HEAD_DIM ** -0.5

GDN_HEADS = 8
GDN_DK = 128
GDN_CONV = 4
GDN_CHUNK = 64
GDN_CONV_W = 3072
GDN_IN = 4112
GDN_QSCALE = GDN_DK ** -0.5
GDN_ROWS = 512
GDN_SCAN_ROWS = 1024
GDN_INTRA_HEADS = 2
GDN_SCAN_HEADS = 2
GDN_SUB_SHIFT = 4

ADAM_LR = 0.001
ADAM_B1 = 0.9
ADAM_B2 = 0.999
ADAM_EPS = 1e-08
ADAM_WD = 0.01
ADAM_STEP = 10

LANES = 128
NEG_BIG = -1e30
VMEM_LIMIT_BYTES = 56 * 1024 * 1024
MESH_ID = pl.DeviceIdType.MESH


def _params(n_axes, vmem_limit_bytes=VMEM_LIMIT_BYTES):
    return pltpu.CompilerParams(dimension_semantics=("arbitrary",) * n_axes,
                                vmem_limit_bytes=vmem_limit_bytes)


def _nn(a, b):
    return jnp.dot(a, b, preferred_element_type=F32)


def _nt(a, b):
    return lax.dot_general(a, b, (((1,), (1,)), ((), ())), preferred_element_type=F32)


def _tn(a, b):
    return lax.dot_general(a, b, (((0,), (0,)), ((), ())), preferred_element_type=F32)


def _bnn(a, b, precision=None):
    return lax.dot_general(a, b, (((2,), (1,)), ((0,), (0,))), precision=precision,
                           preferred_element_type=F32)


def _bnt(a, b, precision=None):
    return lax.dot_general(a, b, (((2,), (2,)), ((0,), (0,))), precision=precision,
                           preferred_element_type=F32)


def _sigmoid(x):
    return 1.0 / (1.0 + jnp.exp(-x))


def _rms_stats(x):
    r = lax.rsqrt(jnp.mean(x * x, axis=-1, keepdims=True) + NORM_EPS)
    return r, x * r


def _norm_bwd(x, nw, dhn):
    r, xhat = _rms_stats(x)
    dxh = dhn * nw
    dx = r * (dxh - xhat * jnp.mean(dxh * xhat, axis=-1, keepdims=True))
    dnw = jnp.sum(dhn * xhat, axis=0, keepdims=True)
    return dx, dnw


def _row_tile(t, pref):
    return min(t, pref)


FFN_TM = 512
FFN_BWD_TM = 512
FFN_BWD_TF = 1408
FFN_BWD_VMEM_LIMIT_BYTES = 61 * 1024 * 1024
FFN_TF = 1408


def _carried(body, n_in, n_out, carry, scatter, last_step):
    nc = len(carry)
    if not nc:
        return body

    def wrapped(*refs):
        ins, cin = refs[:n_in], refs[n_in:n_in + nc]
        outs = refs[n_in + nc:n_in + nc + n_out]
        cout = refs[n_in + nc + n_out:n_in + 2 * nc + n_out]
        scratch = refs[n_in + 2 * nc + n_out:]
        sems = scratch[len(scratch) - 3:]
        ids = [pl.program_id(ax) for ax in range(len(last_step))]
        first, last = ids[0] == 0, ids[0] == last_step[0]
        for ax in range(1, len(last_step)):
            first = jnp.logical_and(first, ids[ax] == 0)
            last = jnp.logical_and(last, ids[ax] == last_step[ax])

        @pl.when(first)
        def _():
            for cp in _exchange_copies(cin, cout, *sems, scatter):
                cp.start()

        body(*ins, *outs, *scratch[:len(scratch) - 3])

        @pl.when(last)
        def _():
            for cp in _exchange_copies(cin, cout, *sems, scatter):
                cp.wait()

    return wrapped


def _ffn_fwd(h, nw, wg, wu, wd, carry=(), scatter=False):
    t = h.shape[0]
    tm, tf = _row_tile(t, FFN_TM), FFN_TF
    nj = D_FF // tf
    nc = len(carry)

    def body(h_ref, nw_ref, wg_ref, wu_ref, wd_ref, out_ref, g_ref, u_ref, hn_ref, acc_ref):
        j = pl.program_id(1)

        @pl.when(j == 0)
        def _():
            _, xhat = _rms_stats(h_ref[...])
            hn_ref[...] = (xhat * nw_ref[...]).astype(BF16)
            acc_ref[...] = jnp.zeros_like(acc_ref)

        hn = hn_ref[...]
        g = _nn(hn, wg_ref[...])
        u = _nn(hn, wu_ref[...])
        g_ref[...] = g.astype(BF16)
        u_ref[...] = u.astype(BF16)
        a = (g * _sigmoid(g) * u).astype(BF16)
        acc_ref[...] += _nn(a, wd_ref[...])

        @pl.when(j == nj - 1)
        def _():
            out_ref[...] = h_ref[...] + 0.5 * acc_ref[...]

    outs = pl.pallas_call(
        _carried(body, 5, 4, carry, scatter, (t // tm - 1, nj - 1)),
        name="ffn_fwd_x" if nc else "ffn_fwd", grid=(t // tm, nj),
        in_specs=[pl.BlockSpec((tm, D_MODEL), lambda i, j: (i, 0)),
                  pl.BlockSpec((1, D_MODEL), lambda i, j: (0, 0)),
                  pl.BlockSpec((D_MODEL, tf), lambda i, j: (0, j)),
                  pl.BlockSpec((D_MODEL, tf), lambda i, j: (0, j)),
                  pl.BlockSpec((tf, D_MODEL), lambda i, j: (j, 0))] + [_ANY] * nc,
        out_specs=[pl.BlockSpec((tm, D_MODEL), lambda i, j: (i, 0)),
                   pl.BlockSpec((tm, tf), lambda i, j: (i, j)),
                   pl.BlockSpec((tm, tf), lambda i, j: (i, j)),
                   pl.BlockSpec((tm, D_MODEL), lambda i, j: (i, 0))] + [_ANY] * nc,
        out_shape=[jax.ShapeDtypeStruct((t, D_MODEL), F32),
                   jax.ShapeDtypeStruct((t, D_FF), BF16),
                   jax.ShapeDtypeStruct((t, D_FF), BF16),
                   jax.ShapeDtypeStruct((t, D_MODEL), BF16)] + _exchange_shapes(carry, scatter),
        scratch_shapes=[pltpu.VMEM((tm, D_MODEL), F32)] + (_exchange_sems(nc) if nc else []),
        compiler_params=_params(2),
    )(h, nw, wg, wu, wd, *carry)
    return outs[:4], outs[4:]


def _ffn_bwd(dy, h, nw, g, u, wg, wu, wd, carry=(), scatter=True):
    t = h.shape[0]
    tm, tf = _row_tile(t, FFN_BWD_TM), FFN_BWD_TF
    nj = D_FF // tf
    nc = len(carry)

    def body(dy_ref, h_ref, nw_ref, g_ref, u_ref, wg_ref, wu_ref, wd_ref,
             dh_ref, dg_ref, du_ref, a_ref, dz_ref, dnw_ref, acc_ref):
        i, j = pl.program_id(0), pl.program_id(1)

        @pl.when(j == 0)
        def _():
            dz_ref[...] = (0.5 * dy_ref[...]).astype(BF16)
            acc_ref[...] = jnp.zeros_like(acc_ref)

        @pl.when(jnp.logical_and(i == 0, j == 0))
        def _():
            dnw_ref[...] = jnp.zeros_like(dnw_ref)

        da = _nt(dz_ref[...], wd_ref[...])
        gv = g_ref[...].astype(F32)
        uv = u_ref[...].astype(F32)
        sig = _sigmoid(gv)
        silu = gv * sig
        dg = (da * uv * (sig * (1.0 + gv * (1.0 - sig)))).astype(BF16)
        du = (da * silu).astype(BF16)
        dg_ref[...] = dg
        du_ref[...] = du
        a_ref[...] = (silu * uv).astype(BF16)
        acc_ref[...] += _nt(dg, wg_ref[...]) + _nt(du, wu_ref[...])

        @pl.when(j == nj - 1)
        def _():
            dx, dnw = _norm_bwd(h_ref[...], nw_ref[...], acc_ref[...])
            dh_ref[...] = dy_ref[...] + dx
            dnw_ref[...] += dnw

    outs = pl.pallas_call(
        _carried(body, 8, 6, carry, scatter, (t // tm - 1, nj - 1)),
        name="ffn_bwd_x" if nc else "ffn_bwd", grid=(t // tm, nj),
        in_specs=[pl.BlockSpec((tm, D_MODEL), lambda i, j: (i, 0)),
                  pl.BlockSpec((tm, D_MODEL), lambda i, j: (i, 0)),
                  pl.BlockSpec((1, D_MODEL), lambda i, j: (0, 0)),
                  pl.BlockSpec((tm, tf), lambda i, j: (i, j)),
                  pl.BlockSpec((tm, tf), lambda i, j: (i, j)),
                  pl.BlockSpec((D_MODEL, tf), lambda i, j: (0, j)),
                  pl.BlockSpec((D_MODEL, tf), lambda i, j: (0, j)),
                  pl.BlockSpec((tf, D_MODEL), lambda i, j: (j, 0))] + [_ANY] * nc,
        out_specs=[pl.BlockSpec((tm, D_MODEL), lambda i, j: (i, 0)),
                   pl.BlockSpec((tm, tf), lambda i, j: (i, j)),
                   pl.BlockSpec((tm, tf), lambda i, j: (i, j)),
                   pl.BlockSpec((tm, tf), lambda i, j: (i, j)),
                   pl.BlockSpec((tm, D_MODEL), lambda i, j: (i, 0)),
                   pl.BlockSpec((1, D_MODEL), lambda i, j: (0, 0))] + [_ANY] * nc,
        out_shape=[jax.ShapeDtypeStruct((t, D_MODEL), F32),
                   jax.ShapeDtypeStruct((t, D_FF), BF16),
                   jax.ShapeDtypeStruct((t, D_FF), BF16),
                   jax.ShapeDtypeStruct((t, D_FF), BF16),
                   jax.ShapeDtypeStruct((t, D_MODEL), BF16),
                   jax.ShapeDtypeStruct((1, D_MODEL), F32)] + _exchange_shapes(carry, scatter),
        scratch_shapes=[pltpu.VMEM((tm, D_MODEL), F32)] + (_exchange_sems(nc) if nc else []),
        compiler_params=_params(2, FFN_BWD_VMEM_LIMIT_BYTES),
    )(dy, h, nw, g, u, wg, wu, wd, *carry)
    return outs[:6], outs[6:]


def _matmul_tn(a, b, tn, name, carry=(), out_dtype=None):
    out_dtype = BF16 if out_dtype is None else out_dtype
    k, m = a.shape
    n = b.shape[1]
    tm = min(m, 1408 if m % 1408 == 0 else 1024)
    tk = min(k, 2048)
    nc = len(carry)

    grid = (m // tm, n // tn, k // tk)

    def body(a_ref, b_ref, o_ref, acc_ref):
        kk = pl.program_id(2)

        @pl.when(kk == 0)
        def _():
            acc_ref[...] = jnp.zeros_like(acc_ref)

        acc_ref[...] += _tn(a_ref[...], b_ref[...])

        @pl.when(kk == grid[2] - 1)
        def _():
            o_ref[...] = acc_ref[...].astype(out_dtype)

    outs = pl.pallas_call(
        _carried(body, 2, 1, carry, True, tuple(g - 1 for g in grid)),
        name=name + "_x" if nc else name, grid=grid,
        in_specs=[pl.BlockSpec((tk, tm), lambda i, j, kk: (kk, i)),
                  pl.BlockSpec((tk, tn), lambda i, j, kk: (kk, j))] + [_ANY] * nc,
        out_specs=[pl.BlockSpec((tm, tn), lambda i, j, kk: (i, j))] + [_ANY] * nc,
        out_shape=[jax.ShapeDtypeStruct((m, n), out_dtype)] + _exchange_shapes(carry, True),
        scratch_shapes=[pltpu.VMEM((tm, tn), F32)] + (_exchange_sems(nc) if nc else []),
        compiler_params=_params(3),
    )(a, b, *carry)
    return (outs[0], outs[1:]) if nc else outs[0]


ATTN_TM = 512


def _rope(t, c, s1, s2):
    return t * c + pltpu.roll(t, LANES - ROPE_DIM // 2, 1) * s1 + pltpu.roll(t, ROPE_DIM // 2, 1) * s2


def _rope_bwd(d, c, s1, s2):
    return d * c + pltpu.roll(d * s1, ROPE_DIM // 2, 1) + pltpu.roll(d * s2, LANES - ROPE_DIM // 2, 1)


def _attn_qkv_fwd(h, nw, w_in, b_in, rc, rs1, rs2):
    t = h.shape[0]
    tm = _row_tile(t, ATTN_TM)

    def body(h_ref, nw_ref, w_ref, b_ref, c_ref, s1_ref, s2_ref, q_ref, k_ref, v_ref, hn_ref):
        _, xhat = _rms_stats(h_ref[...])
        hn = (xhat * nw_ref[...]).astype(BF16)
        hn_ref[...] = hn
        qkv = _nn(hn, w_ref[...]) + b_ref[...]
        c, s1, s2 = c_ref[...], s1_ref[...], s2_ref[...]
        for s in range(ATTN_Q_W // LANES):
            q_ref[:, s * LANES:(s + 1) * LANES] = _rope(qkv[:, s * LANES:(s + 1) * LANES], c, s1, s2).astype(BF16)
        for s in range(ATTN_KV_W // LANES):
            lo = ATTN_Q_W + s * LANES
            k_ref[:, s * LANES:(s + 1) * LANES] = _rope(qkv[:, lo:lo + LANES], c, s1, s2).astype(BF16)
        v_ref[...] = qkv[:, ATTN_Q_W + ATTN_KV_W:].astype(BF16)

    row = lambda w: pl.BlockSpec((tm, w), lambda i: (i, 0))
    full = lambda a: pl.BlockSpec(a.shape, lambda i: (0, 0))
    return pl.pallas_call(
        body, name="attn_qkv_fwd", grid=(t // tm,),
        in_specs=[row(D_MODEL), full(nw), full(w_in), full(b_in), row(LANES), row(LANES), row(LANES)],
        out_specs=[row(ATTN_Q_W), row(ATTN_KV_W), row(ATTN_KV_W), row(D_MODEL)],
        out_shape=[jax.ShapeDtypeStruct((t, ATTN_Q_W), BF16),
                   jax.ShapeDtypeStruct((t, ATTN_KV_W), BF16),
                   jax.ShapeDtypeStruct((t, ATTN_KV_W), BF16),
                   jax.ShapeDtypeStruct((t, D_MODEL), BF16)],
        compiler_params=_params(1),
    )(h, nw, w_in, b_in, rc, rs1, rs2)


def _attn_group(q_ref, kc_ref, kp_ref, vc_ref, vp_ref, sinks_ref, hk, n):
    hd = ATTN_HEAD_DIM
    cols = slice(hk * hd, (hk + 1) * hd)
    krow = lax.broadcasted_iota(jnp.int32, (2 * ATTN_BLOCK, hd), 0)
    kcat = jnp.concatenate([kp_ref[:, cols], kc_ref[:, cols]], axis=0)
    vcat = jnp.concatenate([vp_ref[:, cols], vc_ref[:, cols]], axis=0)
    kcat = jnp.where(krow == 0, jnp.zeros_like(kcat), kcat)
    vcat = jnp.where(krow == 0, jnp.zeros_like(vcat), vcat)
    heads = [hk * ATTN_GROUP + g for g in range(ATTN_GROUP)]
    qs = jnp.concatenate([q_ref[:, hq * hd:(hq + 1) * hd] for hq in heads], axis=0)
    s = _nt(qs, kcat) * ATTN_SCALE
    rows = ATTN_GROUP * ATTN_BLOCK
    ri = lax.broadcasted_iota(jnp.int32, (rows, 2 * ATTN_BLOCK), 0) & (ATTN_BLOCK - 1)
    cj = lax.broadcasted_iota(jnp.int32, (rows, 2 * ATTN_BLOCK), 1)
    first = jnp.where(n > 0, 0, 2 * ATTN_BLOCK)
    valid = jnp.logical_or(jnp.logical_and(cj < ATTN_BLOCK, cj > ri + first),
                           jnp.logical_and(cj >= ATTN_BLOCK, cj - ATTN_BLOCK <= ri))
    sink = jnp.concatenate([jnp.full((ATTN_BLOCK, 2 * ATTN_BLOCK), sinks_ref[hq], F32) for hq in heads], axis=0)
    s = jnp.where(valid, s, jnp.where(cj == 0, sink, NEG_BIG))
    p = jnp.exp(s - jnp.max(s, axis=-1, keepdims=True))
    probs = p / jnp.sum(p, axis=-1, keepdims=True)
    return heads, qs, kcat, vcat, probs, krow


def _attn_core_fwd(q, k, v, sinks):
    t = q.shape[0]
    nb = t // ATTN_BLOCK

    def body(q_ref, kc_ref, kp_ref, vc_ref, vp_ref, sinks_ref, o_ref):
        n = pl.program_id(0)
        for hk in range(ATTN_KV_HEADS):
            heads, _, _, vcat, probs, _ = _attn_group(q_ref, kc_ref, kp_ref, vc_ref, vp_ref, sinks_ref, hk, n)
            o = _nn(probs.astype(BF16), vcat)
            for g, hq in enumerate(heads):
                o_ref[:, hq * ATTN_HEAD_DIM:(hq + 1) * ATTN_HEAD_DIM] = (
                    o[g * ATTN_BLOCK:(g + 1) * ATTN_BLOCK].astype(BF16))

    cur = lambda w: pl.BlockSpec((ATTN_BLOCK, w), lambda n: (n, 0))
    prev = lambda w: pl.BlockSpec((ATTN_BLOCK, w), lambda n: (jnp.maximum(n - 1, 0), 0))
    return pl.pallas_call(
        body, name="attn_core_fwd", grid=(nb,),
        in_specs=[cur(ATTN_Q_W), cur(ATTN_KV_W), prev(ATTN_KV_W), cur(ATTN_KV_W), prev(ATTN_KV_W),
                  pl.BlockSpec(memory_space=pltpu.SMEM)],
        out_specs=cur(ATTN_Q_W),
        out_shape=jax.ShapeDtypeStruct((t, ATTN_Q_W), BF16),
        compiler_params=_params(1),
    )(q, k, k, v, v, sinks.reshape(-1))


def _attn_core_bwd(q, k, v, do, sinks):
    t = q.shape[0]
    nb = t // ATTN_BLOCK
    hd = ATTN_HEAD_DIM

    def body(q_ref, kc_ref, kp_ref, vc_ref, vp_ref, do_ref, sinks_ref,
             dq_ref, dk_ref, dv_ref, dsink_ref, ck_ref, cv_ref):
        n = pl.program_id(0)

        @pl.when(n == 0)
        def _():
            dsink_ref[...] = jnp.zeros_like(dsink_ref)
            ck_ref[...] = jnp.zeros_like(ck_ref)
            cv_ref[...] = jnp.zeros_like(cv_ref)

        @pl.when(n == nb)
        def _():
            dk_ref[...] = ck_ref[...]
            dv_ref[...] = cv_ref[...]

        @pl.when(n < nb)
        def _():
            for hk in range(ATTN_KV_HEADS):
                heads, qs, kcat, vcat, probs, krow = _attn_group(
                    q_ref, kc_ref, kp_ref, vc_ref, vp_ref, sinks_ref, hk, n)
                dos = jnp.concatenate([do_ref[:, hq * hd:(hq + 1) * hd] for hq in heads], axis=0)
                dp = _nt(dos, vcat)
                delta = jnp.sum(probs * dp, axis=-1, keepdims=True)
                dsf = probs * (dp - delta)
                ds = dsf.astype(BF16)
                dqs = _nn(ds, kcat) * ATTN_SCALE
                dkc = jnp.where(krow == 0, 0.0, _tn(ds, qs) * ATTN_SCALE)
                dvc = jnp.where(krow == 0, 0.0, _tn(probs.astype(BF16), dos))
                for g, hq in enumerate(heads):
                    blk = slice(g * ATTN_BLOCK, (g + 1) * ATTN_BLOCK)
                    dq_ref[:, hq * hd:(hq + 1) * hd] = dqs[blk]
                    dsink_ref[hq:hq + 1, :] += jnp.broadcast_to(
                        jnp.sum(dsf[blk, 0:1], axis=0, keepdims=True), (1, LANES))
                cols = slice(hk * hd, (hk + 1) * hd)
                dk_ref[:, cols] = ck_ref[:, cols] + dkc[:ATTN_BLOCK]
                dv_ref[:, cols] = cv_ref[:, cols] + dvc[:ATTN_BLOCK]
                ck_ref[:, cols] = dkc[ATTN_BLOCK:]
                cv_ref[:, cols] = dvc[ATTN_BLOCK:]

    cur = lambda w: pl.BlockSpec((ATTN_BLOCK, w), lambda n: (jnp.minimum(n, nb - 1), 0))
    prev = lambda w: pl.BlockSpec((ATTN_BLOCK, w), lambda n: (jnp.clip(n - 1, 0, nb - 1), 0))
    return pl.pallas_call(
        body, name="attn_core_bwd", grid=(nb + 1,),
        in_specs=[cur(ATTN_Q_W), cur(ATTN_KV_W), prev(ATTN_KV_W), cur(ATTN_KV_W), prev(ATTN_KV_W),
                  cur(ATTN_Q_W), pl.BlockSpec(memory_space=pltpu.SMEM)],
        out_specs=[cur(ATTN_Q_W), prev(ATTN_KV_W), prev(ATTN_KV_W),
                   pl.BlockSpec((ATTN_Q_HEADS, LANES), lambda n: (0, 0))],
        out_shape=[jax.ShapeDtypeStruct((t, ATTN_Q_W), F32),
                   jax.ShapeDtypeStruct((t, ATTN_KV_W), F32),
                   jax.ShapeDtypeStruct((t, ATTN_KV_W), F32),
                   jax.ShapeDtypeStruct((ATTN_Q_HEADS, LANES), F32)],
        scratch_shapes=[pltpu.VMEM((ATTN_BLOCK, ATTN_KV_W), F32),
                        pltpu.VMEM((ATTN_BLOCK, ATTN_KV_W), F32)],
        compiler_params=_params(1),
    )(q, k, k, v, v, do, sinks.reshape(-1))


def _proj_out_fwd(x, w, b, res):
    t = x.shape[0]
    tm = _row_tile(t, 512)

    def body(x_ref, w_ref, b_ref, r_ref, o_ref):
        o_ref[...] = r_ref[...] + _nn(x_ref[...], w_ref[...]) + b_ref[...]

    row = pl.BlockSpec((tm, D_MODEL), lambda i: (i, 0))
    return pl.pallas_call(
        body, name="proj_out_fwd", grid=(t // tm,),
        in_specs=[row, pl.BlockSpec(w.shape, lambda i: (0, 0)), pl.BlockSpec(b.shape, lambda i: (0, 0)), row],
        out_specs=row,
        out_shape=jax.ShapeDtypeStruct((t, D_MODEL), F32),
        compiler_params=_params(1),
    )(x, w, b, res)


def _proj_out_bwd(dy, w):
    t = dy.shape[0]
    tm = _row_tile(t, 512)

    def body(dy_ref, w_ref, dx_ref, dyb_ref, db_ref):
        @pl.when(pl.program_id(0) == 0)
        def _():
            db_ref[...] = jnp.zeros_like(db_ref)

        dy_v = dy_ref[...]
        dyb = dy_v.astype(BF16)
        dyb_ref[...] = dyb
        dx_ref[...] = _nt(dyb, w_ref[...]).astype(BF16)
        db_ref[...] += jnp.sum(dy_v, axis=0, keepdims=True)

    row = pl.BlockSpec((tm, D_MODEL), lambda i: (i, 0))
    return pl.pallas_call(
        body, name="proj_out_bwd", grid=(t // tm,),
        in_specs=[row, pl.BlockSpec(w.shape, lambda i: (0, 0))],
        out_specs=[row, row, pl.BlockSpec((1, D_MODEL), lambda i: (0, 0))],
        out_shape=[jax.ShapeDtypeStruct((t, D_MODEL), BF16),
                   jax.ShapeDtypeStruct((t, D_MODEL), BF16),
                   jax.ShapeDtypeStruct((1, D_MODEL), F32)],
        compiler_params=_params(1),
    )(dy, w)


def _attn_qkv_bwd(dy, h, nw, w_in, dq, dk, dv, rc, rs1, rs2):
    t = h.shape[0]
    tm = _row_tile(t, ATTN_TM)

    def body(dy_ref, h_ref, nw_ref, w_ref, dq_ref, dk_ref, dv_ref, c_ref, s1_ref, s2_ref,
             dh_ref, dqkv_ref, db_ref, dnw_ref, tmp_ref):
        @pl.when(pl.program_id(0) == 0)
        def _():
            db_ref[...] = jnp.zeros_like(db_ref)
            dnw_ref[...] = jnp.zeros_like(dnw_ref)

        c, s1, s2 = c_ref[...], s1_ref[...], s2_ref[...]
        for s in range(ATTN_Q_W // LANES):
            tmp_ref[:, s * LANES:(s + 1) * LANES] = _rope_bwd(dq_ref[:, s * LANES:(s + 1) * LANES], c, s1, s2)
        for s in range(ATTN_KV_W // LANES):
            lo = ATTN_Q_W + s * LANES
            tmp_ref[:, lo:lo + LANES] = _rope_bwd(dk_ref[:, s * LANES:(s + 1) * LANES], c, s1, s2)
        tmp_ref[:, ATTN_Q_W + ATTN_KV_W:] = dv_ref[...]
        dqkv = tmp_ref[...]
        db_ref[...] += jnp.sum(dqkv, axis=0, keepdims=True)
        dqkv_b = dqkv.astype(BF16)
        dqkv_ref[...] = dqkv_b
        dx, dnw = _norm_bwd(h_ref[...], nw_ref[...], _nt(dqkv_b, w_ref[...]))
        dh_ref[...] = dy_ref[...] + dx
        dnw_ref[...] += dnw

    row = lambda w: pl.BlockSpec((tm, w), lambda i: (i, 0))
    full = lambda a: pl.BlockSpec(a.shape, lambda i: (0, 0))
    return pl.pallas_call(
        body, name="attn_qkv_bwd", grid=(t // tm,),
        in_specs=[row(D_MODEL), row(D_MODEL), full(nw), full(w_in), row(ATTN_Q_W), row(ATTN_KV_W),
                  row(ATTN_KV_W), row(LANES), row(LANES), row(LANES)],
        out_specs=[row(D_MODEL), row(ATTN_IN), pl.BlockSpec((1, ATTN_IN), lambda i: (0, 0)),
                   pl.BlockSpec((1, D_MODEL), lambda i: (0, 0))],
        out_shape=[jax.ShapeDtypeStruct((t, D_MODEL), F32),
                   jax.ShapeDtypeStruct((t, ATTN_IN), BF16),
                   jax.ShapeDtypeStruct((1, ATTN_IN), F32),
                   jax.ShapeDtypeStruct((1, D_MODEL), F32)],
        scratch_shapes=[pltpu.VMEM((tm, ATTN_IN), F32)],
        compiler_params=_params(1),
    )(dy, h, nw, w_in, dq, dk, dv, rc, rs1, rs2)


GDN_TM = 512
GDN_CONV_TM = 256


def _gdn_proj_fwd(h, nw, w_qkv, w_z, w_ba):
    t = h.shape[0]
    tm = _row_tile(t, GDN_TM)

    def body(h_ref, nw_ref, wq_ref, wz_ref, wb_ref, x_ref, z_ref, ba_ref, hn_ref):
        _, xhat = _rms_stats(h_ref[...])
        hn = (xhat * nw_ref[...]).astype(BF16)
        hn_ref[...] = hn
        x_ref[...] = _nn(hn, wq_ref[...])
        z_ref[...] = _nn(hn, wz_ref[...])
        ba_ref[...] = _nn(hn, wb_ref[...])

    row = lambda w: pl.BlockSpec((tm, w), lambda i: (i, 0))
    full = lambda a: pl.BlockSpec(a.shape, lambda i: (0, 0))
    return pl.pallas_call(
        body, name="gdn_proj_fwd", grid=(t // tm,),
        in_specs=[row(D_MODEL), full(nw), full(w_qkv), full(w_z), full(w_ba)],
        out_specs=[row(GDN_CONV_W), row(D_MODEL), row(LANES), row(D_MODEL)],
        out_shape=[jax.ShapeDtypeStruct((t, GDN_CONV_W), F32),
                   jax.ShapeDtypeStruct((t, D_MODEL), F32),
                   jax.ShapeDtypeStruct((t, LANES), F32),
                   jax.ShapeDtypeStruct((t, D_MODEL), BF16)],
        compiler_params=_params(1),
    )(h, nw, w_qkv, w_z, w_ba)


def _softplus(x):
    return jnp.maximum(x, 0.0) + jnp.log(1.0 + jnp.exp(-jnp.abs(x)))


def _conv_taps(x, halo, cw):
    tm = x.shape[0]
    xx = jnp.concatenate([halo, x], axis=0)
    taps = [xx[8 - (GDN_CONV - 1) + i: 8 - (GDN_CONV - 1) + i + tm] for i in range(GDN_CONV)]
    c = taps[0] * cw[0:1]
    for i in range(1, GDN_CONV):
        c = c + taps[i] * cw[i:i + 1]
    return c, taps


def _gates(ba, gp):
    lane = lax.broadcasted_iota(jnp.int32, ba.shape, 1)
    beta = _sigmoid(ba)
    pre = ba + gp[1:2]
    g = -jnp.exp(gp[0:1]) * _softplus(pre)
    gates = jnp.where(lane < GDN_HEADS, beta, jnp.where(lane < 2 * GDN_HEADS, g, 0.0))
    return lane, beta, pre, g, gates


def _gdn_conv_fwd(x, cw, ba, gp):
    t = x.shape[0]
    tm = _row_tile(t, GDN_CONV_TM)

    def body(x_ref, halo_ref, cw_ref, ba_ref, gp_ref, q_ref, k_ref, v_ref, gates_ref):
        halo = jnp.where(pl.program_id(0) > 0, halo_ref[...], 0.0)
        c, _ = _conv_taps(x_ref[...], halo, cw_ref[...])
        s = c * _sigmoid(c)
        for hh in range(GDN_HEADS):
            sq = s[:, hh * LANES:(hh + 1) * LANES]
            q_ref[:, hh * LANES:(hh + 1) * LANES] = (
                sq * lax.rsqrt(jnp.sum(sq * sq, axis=-1, keepdims=True) + NORM_EPS) * GDN_QSCALE)
            sk = s[:, D_MODEL + hh * LANES:D_MODEL + (hh + 1) * LANES]
            k_ref[:, hh * LANES:(hh + 1) * LANES] = (
                sk * lax.rsqrt(jnp.sum(sk * sk, axis=-1, keepdims=True) + NORM_EPS))
        v_ref[...] = s[:, 2 * D_MODEL:]
        gates_ref[...] = _gates(ba_ref[...], gp_ref[...])[4]

    row = lambda w: pl.BlockSpec((tm, w), lambda i: (i, 0))
    full = lambda a: pl.BlockSpec(a.shape, lambda i: (0, 0))
    halo = pl.BlockSpec((8, GDN_CONV_W), lambda i: (jnp.maximum(i * (tm // 8) - 1, 0), 0))
    return pl.pallas_call(
        body, name="gdn_conv_fwd", grid=(t // tm,),
        in_specs=[row(GDN_CONV_W), halo, full(cw), row(LANES), full(gp)],
        out_specs=[row(D_MODEL), row(D_MODEL), row(D_MODEL), row(LANES)],
        out_shape=[jax.ShapeDtypeStruct((t, D_MODEL), F32)] * 3 + [jax.ShapeDtypeStruct((t, LANES), F32)],
        compiler_params=_params(1),
    )(x, x, cw, ba, gp)


def _split3(x):
    hi = x.astype(BF16).astype(F32)
    r1 = x - hi
    mid = r1.astype(BF16).astype(F32)
    lo = (r1 - mid).astype(BF16).astype(F32)
    return hi, mid, lo


def _chunk_cumsum(col, keep):
    nb, c, _ = col.shape
    l3 = lax.broadcasted_iota(jnp.int32, (nb, c, LANES), 2)
    hi, mid, lo = _split3(col)
    pieces = jnp.where(l3 == 0, hi, jnp.where(l3 == 1, mid, jnp.where(l3 == 2, lo, 0.0))).astype(BF16)
    s = _bnn(jnp.where(keep, 1.0, 0.0).astype(BF16), pieces)
    return s[..., 0:1] + s[..., 1:2] + s[..., 2:3]


def _unit_inverse(nmat):
    nb, c, _ = nmat.shape
    ri = lax.broadcasted_iota(jnp.int32, (nb, c, c), 1)
    ci = lax.broadcasted_iota(jnp.int32, (nb, c, c), 2)
    eye = jnp.where(ri == ci, 1.0, 0.0).astype(F32)
    same = (ri >> GDN_SUB_SHIFT) == (ci >> GDN_SUB_SHIFT)
    nd = jnp.where(same, nmat, 0.0)
    no = nmat - nd
    mm = lambda a, b: _bnn(a.astype(BF16), b.astype(BF16))
    n2 = mm(nd, nd)
    n4 = mm(n2, n2)
    n8 = mm(n4, n4)
    td = mm(mm(mm(eye - nd, eye + n2), eye + n4), eye + n8)
    bm = mm(td, no)
    b2 = mm(bm, bm)
    return mm(mm(eye - bm, eye + b2), td)


class _LaneWindow:
    def __init__(self, ref, j):
        self.ref, self.lanes, self.shape = ref, slice(j * LANES, (j + 1) * LANES), (ref.shape[0], LANES)

    def __getitem__(self, idx):
        return self.ref[:, self.lanes]

    def __setitem__(self, idx, val):
        self.ref[:, self.lanes] = val


def _chunk_terms(q_ref, k_ref, gates_ref, h, transposed):
    rows = k_ref.shape[0]
    nb = rows // GDN_CHUNK
    c = GDN_CHUNK
    lane = lax.broadcasted_iota(jnp.int32, (rows, LANES), 1)
    gt = gates_ref[...]
    beta = jnp.sum(jnp.where(lane == h, gt, 0.0), axis=-1, keepdims=True).reshape(nb, c, 1)
    g = jnp.sum(jnp.where(lane == GDN_HEADS + h, gt, 0.0), axis=-1, keepdims=True).reshape(nb, c, 1)
    ri = lax.broadcasted_iota(jnp.int32, (nb, c, c), 1)
    ci = lax.broadcasted_iota(jnp.int32, (nb, c, c), 2)
    gcol = _chunk_cumsum(g, ri >= ci)
    gamma = jnp.broadcast_to(gcol, (nb, c, LANES))
    l3 = lax.broadcasted_iota(jnp.int32, (nb, c, LANES), 2)
    gh, gm, gl = _split3(gcol)
    pmat = jnp.where(l3 == 0, gh, jnp.where(l3 == 1, gm, jnp.where(l3 == 2, gl, jnp.where(l3 < 6, 1.0, 0.0))))
    qmat = jnp.where(l3 < 3, 1.0, jnp.where(l3 == 3, -gh, jnp.where(l3 == 4, -gm, jnp.where(l3 == 5, -gl, 0.0))))
    pmat, qmat = pmat.astype(BF16), qmat.astype(BF16)
    k = k_ref[...].reshape(nb, c, LANES)
    q = q_ref[...].reshape(nb, c, LANES)
    kb = k * beta
    kbf, kbb, qb = k.astype(BF16), kb.astype(BF16), q.astype(BF16)
    out = dict(beta=beta, g=g, gamma=gamma, k=k, q=q, kb=kb, kbf=kbf, kbb=kbb, qb=qb, ri=ri, ci=ci)
    diff = _bnt(pmat, qmat)
    lmat = jnp.exp(jnp.where(ri >= ci, diff, NEG_BIG))
    out["L"] = lmat
    out["A"] = jnp.where(ri > ci, _bnt(kbb, kbf) * lmat, 0.0)
    out["Aqk"] = jnp.where(ri >= ci, _bnt(qb, kbf) * lmat, 0.0)
    if transposed:
        difft = _bnt(qmat, pmat)
        lt = jnp.exp(jnp.where(ci >= ri, difft, NEG_BIG))
        out["LT"] = lt
        out["AT"] = jnp.where(ci > ri, _bnt(kbf, kbb) * lt, 0.0)
        out["AqkT"] = jnp.where(ci >= ri, _bnt(kbf, qb) * lt, 0.0)
    return out


def _gdn_intra_fwd(q, k, v, gates):
    t = q.shape[0]
    rows = _row_tile(t, GDN_ROWS)
    nb = rows // GDN_CHUNK
    nchunks = t // GDN_CHUNK
    hs = GDN_INTRA_HEADS

    def head(h, hh, q_ref, k_ref, v_ref, gates_ref, u_ref, w_ref, qd_ref, kd_ref, aqk_ref, cd_ref):
        tm_ = _chunk_terms(q_ref, k_ref, gates_ref, h, False)
        gamma, beta = tm_["gamma"], tm_["beta"]
        eg = jnp.exp(gamma)
        tinv = _unit_inverse(tm_["A"])
        v3 = v_ref[...].reshape(nb, GDN_CHUNK, LANES)
        rhs = jnp.concatenate([v3 * beta, tm_["kb"] * eg], axis=-1)
        sol = _bnn(tinv.astype(BF16), rhs.astype(BF16))
        u_ref[...] = sol[..., :LANES].reshape(rows, LANES)
        w_ref[...] = sol[..., LANES:].reshape(rows, LANES).astype(BF16)
        gl = gamma[:, GDN_CHUNK - 1:GDN_CHUNK, :]
        qd_ref[...] = (tm_["q"] * eg).reshape(rows, LANES).astype(BF16)
        kd_ref[...] = (tm_["k"] * jnp.exp(gl - gamma)).reshape(rows, LANES).astype(BF16)
        aqk_ref[hh] = tm_["Aqk"].reshape(rows, GDN_CHUNK).astype(BF16)
        cd_ref[hh] = jnp.broadcast_to(jnp.exp(gl), (nb, 8, LANES)).reshape(nb * 8, LANES)

    def body(q_ref, k_ref, v_ref, gates_ref, u_ref, w_ref, qd_ref, kd_ref, aqk_ref, cd_ref):
        for hh in range(hs):
            win = lambda ref: _LaneWindow(ref, hh)
            head(pl.program_id(1) * hs + hh, hh, win(q_ref), win(k_ref), win(v_ref), gates_ref,
                 win(u_ref), win(w_ref), win(qd_ref), win(kd_ref), aqk_ref, cd_ref)

    blk = pl.BlockSpec((rows, hs * LANES), lambda i, h: (i, h))
    return pl.pallas_call(
        body, name="gdn_intra_fwd", grid=(t // rows, GDN_HEADS // hs),
        in_specs=[blk, blk, blk, pl.BlockSpec((rows, LANES), lambda i, h: (i, 0))],
        out_specs=[blk, blk, blk, blk,
                   pl.BlockSpec((hs, rows, GDN_CHUNK), lambda i, h: (h, i, 0)),
                   pl.BlockSpec((hs, nb * 8, LANES), lambda i, h: (h, i, 0))],
        out_shape=[jax.ShapeDtypeStruct((t, D_MODEL), F32),
                   jax.ShapeDtypeStruct((t, D_MODEL), BF16),
                   jax.ShapeDtypeStruct((t, D_MODEL), BF16),
                   jax.ShapeDtypeStruct((t, D_MODEL), BF16),
                   jax.ShapeDtypeStruct((GDN_HEADS, t, GDN_CHUNK), BF16),
                   jax.ShapeDtypeStruct((GDN_HEADS, nchunks * 8, LANES), F32)],
        compiler_params=_params(2),
    )(q, k, v, gates)


def _gdn_scan_fwd(u, w, qd, kd, aqk, cd):
    t = u.shape[0]
    rows = _row_tile(t, GDN_SCAN_ROWS)
    nb = rows // GDN_CHUNK
    nchunks = t // GDN_CHUNK
    hs = GDN_SCAN_HEADS

    def body(u_ref, w_ref, qd_ref, kd_ref, aqk_ref, cd_ref, o_ref, st_ref, s_ref):
        @pl.when(pl.program_id(1) == 0)
        def _():
            s_ref[...] = jnp.zeros_like(s_ref)

        states = [s_ref[hh] for hh in range(hs)]
        for c in range(nb):
            r = slice(c * GDN_CHUNK, (c + 1) * GDN_CHUNK)
            for hh in range(hs):
                ln = slice(hh * LANES, (hh + 1) * LANES)
                s = states[hh]
                st_ref[hh, c * LANES:(c + 1) * LANES, :] = s
                sb = s.astype(BF16)
                vb = (u_ref[r, ln] - _nn(w_ref[r, ln], sb)).astype(BF16)
                o_ref[r, ln] = _nn(qd_ref[r, ln], sb) + _nn(aqk_ref[hh, r, :], vb)
                states[hh] = s * cd_ref[hh, c * 8:c * 8 + 1, :] + _tn(kd_ref[r, ln], vb)
        for hh in range(hs):
            s_ref[hh] = states[hh]

    blk = pl.BlockSpec((rows, hs * LANES), lambda h, i: (i, h))
    return pl.pallas_call(
        body, name="gdn_scan_fwd", grid=(GDN_HEADS // hs, t // rows),
        in_specs=[blk, blk, blk, blk,
                  pl.BlockSpec((hs, rows, GDN_CHUNK), lambda h, i: (h, i, 0)),
                  pl.BlockSpec((hs, nb * 8, LANES), lambda h, i: (h, i, 0))],
        out_specs=[blk, pl.BlockSpec((hs, nb * LANES, LANES), lambda h, i: (h, i, 0))],
        out_shape=[jax.ShapeDtypeStruct((t, D_MODEL), F32),
                   jax.ShapeDtypeStruct((GDN_HEADS, nchunks * LANES, LANES), F32)],
        scratch_shapes=[pltpu.VMEM((hs, LANES, LANES), F32)],
        compiler_params=_params(2),
    )(u, w, qd, kd, aqk, cd)


def _gdn_out_fwd(o, z, nw, w_out, res):
    t = o.shape[0]
    tm = _row_tile(t, GDN_TM)

    def body(o_ref, z_ref, nw_ref, w_ref, r_ref, out_ref, gated_ref):
        nwv = nw_ref[...]
        for hh in range(GDN_HEADS):
            sl = slice(hh * LANES, (hh + 1) * LANES)
            _, on = _rms_stats(o_ref[:, sl])
            zv = z_ref[:, sl]
            gated_ref[:, sl] = (on * nwv * (zv * _sigmoid(zv))).astype(BF16)
        out_ref[...] = r_ref[...] + _nn(gated_ref[...], w_ref[...])

    row = pl.BlockSpec((tm, D_MODEL), lambda i: (i, 0))
    full = lambda a: pl.BlockSpec(a.shape, lambda i: (0, 0))
    return pl.pallas_call(
        body, name="gdn_out_fwd", grid=(t // tm,),
        in_specs=[row, row, full(nw), full(w_out), row],
        out_specs=[row, row],
        out_shape=[jax.ShapeDtypeStruct((t, D_MODEL), F32), jax.ShapeDtypeStruct((t, D_MODEL), BF16)],
        compiler_params=_params(1),
    )(o, z, nw, w_out, res)


def _gdn_out_bwd(dy, w_out, o, z, nw):
    t = o.shape[0]
    tm = _row_tile(t, GDN_TM)

    def body(dy_ref, w_ref, o_ref, z_ref, nw_ref, do_ref, dz_ref, dyb_ref, dnw_ref, dgt_ref):
        @pl.when(pl.program_id(0) == 0)
        def _():
            dnw_ref[...] = jnp.zeros_like(dnw_ref)

        dyb = dy_ref[...].astype(BF16)
        dyb_ref[...] = dyb
        dgt_ref[...] = _nt(dyb, w_ref[...])
        nwv = nw_ref[...]
        for hh in range(GDN_HEADS):
            sl = slice(hh * LANES, (hh + 1) * LANES)
            r, on = _rms_stats(o_ref[:, sl])
            zv = z_ref[:, sl]
            sig = _sigmoid(zv)
            sz = zv * sig
            dgt = dgt_ref[:, sl]
            d_on = dgt * nwv * sz
            dz_ref[:, sl] = (dgt * on * nwv * (sig * (1.0 + zv * (1.0 - sig)))).astype(BF16)
            dnw_ref[...] += jnp.sum(dgt * on * sz, axis=0, keepdims=True)
            do_ref[:, sl] = (r * (d_on - on * jnp.mean(d_on * on, axis=-1, keepdims=True))).astype(BF16)

    row = pl.BlockSpec((tm, D_MODEL), lambda i: (i, 0))
    full = lambda a: pl.BlockSpec(a.shape, lambda i: (0, 0))
    return pl.pallas_call(
        body, name="gdn_out_bwd", grid=(t // tm,),
        in_specs=[row, full(w_out), row, row, full(nw)],
        out_specs=[row, row, row, pl.BlockSpec((1, LANES), lambda i: (0, 0))],
        out_shape=[jax.ShapeDtypeStruct((t, D_MODEL), BF16)] * 3 + [jax.ShapeDtypeStruct((1, LANES), F32)],
        scratch_shapes=[pltpu.VMEM((tm, D_MODEL), F32)],
        compiler_params=_params(1),
    )(dy, w_out, o, z, nw)


def _gdn_scan_bwd(u, w, qd, kd, aqk, cd, st, do):
    t = u.shape[0]
    rows = _row_tile(t, GDN_SCAN_ROWS)
    nb = rows // GDN_CHUNK
    nchunks = t // GDN_CHUNK
    nsteps = t // rows
    cc = GDN_CHUNK
    hs = GDN_SCAN_HEADS

    def body(u_ref, w_ref, qd_ref, kd_ref, aqk_ref, cd_ref, st_ref, do_ref,
             du_ref, dw_ref, dqd_ref, dkd_ref, daqk_ref, daqkt_ref, dcd_ref, ds_ref):
        @pl.when(pl.program_id(1) == 0)
        def _():
            ds_ref[...] = jnp.zeros_like(ds_ref)

        ri = lax.broadcasted_iota(jnp.int32, (cc, cc), 0)
        ci = lax.broadcasted_iota(jnp.int32, (cc, cc), 1)
        dstates = [ds_ref[hh] for hh in range(hs)]
        for c in reversed(range(nb)):
            r = slice(c * cc, (c + 1) * cc)
            for hh in range(hs):
                ln = slice(hh * LANES, (hh + 1) * LANES)
                s = st_ref[hh, c * LANES:(c + 1) * LANES, :]
                sb = s.astype(BF16)
                dsn = dstates[hh]
                dsb = dsn.astype(BF16)
                wv, kdv, qdv, aq, dov = w_ref[r, ln], kd_ref[r, ln], qd_ref[r, ln], aqk_ref[hh, r, :], do_ref[r, ln]
                vb = (u_ref[r, ln] - _nn(wv, sb)).astype(BF16)
                dv = _tn(aq, dov) + _nn(kdv, dsb)
                dvb = dv.astype(BF16)
                daqk_ref[hh, r, :] = jnp.where(ri >= ci, _nt(dov, vb), 0.0)
                daqkt_ref[hh, r, :] = jnp.where(ci >= ri, _nt(vb, dov), 0.0)
                dqd_ref[r, ln] = _nt(dov, sb)
                dkd_ref[r, ln] = _nt(vb, dsb)
                dcd_ref[hh, c * 8:(c + 1) * 8, :] = jnp.broadcast_to(jnp.sum(s * dsn), (8, LANES))
                du_ref[r, ln] = dv
                dw_ref[r, ln] = -_nt(dvb, sb)
                dstates[hh] = _tn(qdv, dov) + dsn * cd_ref[hh, c * 8:c * 8 + 1, :] - _tn(wv, dvb)
        for hh in range(hs):
            ds_ref[hh] = dstates[hh]

    rev = lambda i: nsteps - 1 - i
    blk = pl.BlockSpec((rows, hs * LANES), lambda h, i: (rev(i), h))
    sq = pl.BlockSpec((hs, rows, cc), lambda h, i: (h, rev(i), 0))
    cdb = pl.BlockSpec((hs, nb * 8, LANES), lambda h, i: (h, rev(i), 0))
    return pl.pallas_call(
        body, name="gdn_scan_bwd", grid=(GDN_HEADS // hs, nsteps),
        in_specs=[blk, blk, blk, blk, sq, cdb,
                  pl.BlockSpec((hs, nb * LANES, LANES), lambda h, i: (h, rev(i), 0)), blk],
        out_specs=[blk, blk, blk, blk, sq, sq, cdb],
        out_shape=[jax.ShapeDtypeStruct((t, D_MODEL), F32)] * 4
        + [jax.ShapeDtypeStruct((GDN_HEADS, t, cc), F32)] * 2
        + [jax.ShapeDtypeStruct((GDN_HEADS, nchunks * 8, LANES), F32)],
        scratch_shapes=[pltpu.VMEM((hs, LANES, LANES), F32)],
        compiler_params=_params(2),
    )(u, w, qd, kd, aqk, cd, st, do)


def _gdn_intra_bwd(q, k, v, gates, u, w, du, dw, dqd, dkd, daqk, daqkt, dcd):
    t = q.shape[0]
    rows = _row_tile(t, GDN_ROWS)
    nb = rows // GDN_CHUNK
    cc = GDN_CHUNK
    hs = GDN_INTRA_HEADS

    def head(h, hh, q_ref, k_ref, v_ref, gates_ref, u_ref, w_ref, du_ref, dw_ref, dqd_ref, dkd_ref,
             daqk_ref, daqkt_ref, dcd_ref, dq_ref, dk_ref, dv_ref, dgates_ref):
        tm_ = _chunk_terms(q_ref, k_ref, gates_ref, h, True)
        gamma, beta, kk, qq, kb = tm_["gamma"], tm_["beta"], tm_["k"], tm_["q"], tm_["kb"]
        kbf, kbb, qb = tm_["kbf"], tm_["kbb"], tm_["qb"]
        lmat, lt = tm_["L"], tm_["LT"]
        ri, ci = tm_["ri"], tm_["ci"]
        r3 = lambda ref: ref[...].reshape(nb, cc, LANES)
        eg = jnp.exp(gamma)
        gl = gamma[:, cc - 1:cc, :]
        ekd = jnp.exp(gl - gamma)
        v3 = r3(v_ref)
        tt = _unit_inverse(tm_["AT"])
        dsol = jnp.concatenate([r3(du_ref), r3(dw_ref)], axis=-1)
        sol = jnp.concatenate([r3(u_ref), r3(w_ref).astype(F32)], axis=-1)
        dx = _bnn(tt.astype(BF16), dsol.astype(BF16))
        dxb, solb = dx.astype(BF16), sol.astype(BF16)
        da = jnp.where(ri > ci, -_bnt(dxb, solb), 0.0)
        dat = jnp.where(ci > ri, -_bnt(solb, dxb), 0.0)
        dxu, dxw = dx[..., :LANES], dx[..., LANES:]
        dv_ref[...] = (dxu * beta).reshape(rows, LANES)
        dbeta = jnp.sum(dxu * v3, axis=-1, keepdims=True)
        dkb = dxw * eg
        dgam = jnp.sum(dxw * kb * eg, axis=-1, keepdims=True)
        dkb = dkb + _bnn((da * lmat).astype(BF16), kbf)
        dk = _bnn((dat * lt).astype(BF16), kbb)
        dgam = dgam + jnp.sum(da * tm_["A"], axis=-1, keepdims=True) - jnp.sum(dat * tm_["AT"], axis=-1, keepdims=True)
        daq = daqk_ref[hh].reshape(nb, cc, cc)
        daqt = daqkt_ref[hh].reshape(nb, cc, cc)
        dq = _bnn((daq * lmat).astype(BF16), kbf)
        dk = dk + _bnn((daqt * lt).astype(BF16), qb)
        dgam = dgam + jnp.sum(daq * tm_["Aqk"], axis=-1, keepdims=True) - jnp.sum(daqt * tm_["AqkT"], axis=-1, keepdims=True)
        dqd3, dkd3 = r3(dqd_ref), r3(dkd_ref)
        dq = dq + dqd3 * eg
        dgam = dgam + jnp.sum(dqd3 * qq * eg, axis=-1, keepdims=True)
        dk = dk + dkd3 * ekd
        tk = jnp.sum(dkd3 * kk * ekd, axis=-1, keepdims=True)
        dgam = dgam - tk
        dcdv = dcd_ref[hh].reshape(nb, 8, LANES)[:, 0:1, 0:1]
        dglast = jnp.sum(tk, axis=1, keepdims=True) + dcdv * jnp.exp(gl[:, :, 0:1])
        rowi = lax.broadcasted_iota(jnp.int32, (nb, cc, 1), 1)
        dgam = dgam + jnp.where(rowi == cc - 1, dglast, 0.0)
        dk = dk + dkb * beta
        dbeta = dbeta + jnp.sum(dkb * kk, axis=-1, keepdims=True)
        dg = _chunk_cumsum(dgam, ci >= ri)
        dq_ref[...] = dq.reshape(rows, LANES)
        dk_ref[...] = dk.reshape(rows, LANES)
        lane = lax.broadcasted_iota(jnp.int32, (rows, LANES), 1)
        dgates_ref[...] += (jnp.where(lane == h, dbeta.reshape(rows, 1), 0.0)
                            + jnp.where(lane == GDN_HEADS + h, dg.reshape(rows, 1), 0.0))

    def body(q_ref, k_ref, v_ref, gates_ref, u_ref, w_ref, du_ref, dw_ref, dqd_ref, dkd_ref,
             daqk_ref, daqkt_ref, dcd_ref, dq_ref, dk_ref, dv_ref, dgates_ref):
        @pl.when(pl.program_id(1) == 0)
        def _():
            dgates_ref[...] = jnp.zeros_like(dgates_ref)

        for hh in range(hs):
            win = lambda ref: _LaneWindow(ref, hh)
            head(pl.program_id(1) * hs + hh, hh, win(q_ref), win(k_ref), win(v_ref), gates_ref, win(u_ref),
                 win(w_ref), win(du_ref), win(dw_ref), win(dqd_ref), win(dkd_ref), daqk_ref, daqkt_ref, dcd_ref,
                 win(dq_ref), win(dk_ref), win(dv_ref), dgates_ref)

    blk = pl.BlockSpec((rows, hs * LANES), lambda i, h: (i, h))
    shared = pl.BlockSpec((rows, LANES), lambda i, h: (i, 0))
    sq = pl.BlockSpec((hs, rows, cc), lambda i, h: (h, i, 0))
    return pl.pallas_call(
        body, name="gdn_intra_bwd", grid=(t // rows, GDN_HEADS // hs),
        in_specs=[blk, blk, blk, shared, blk, blk, blk, blk, blk, blk, sq, sq,
                  pl.BlockSpec((hs, nb * 8, LANES), lambda i, h: (h, i, 0))],
        out_specs=[blk, blk, blk, shared],
        out_shape=[jax.ShapeDtypeStruct((t, D_MODEL), F32)] * 3 + [jax.ShapeDtypeStruct((t, LANES), F32)],
        compiler_params=_params(2),
    )(q, k, v, gates, u, w, du, dw, dqd, dkd, daqk, daqkt, dcd)


def _gdn_conv_bwd_a(x, cw, ba, gp, dq, dk, dv, dgates):
    t = x.shape[0]
    tm = _row_tile(t, GDN_CONV_TM)

    def body(x_ref, halo_ref, cw_ref, ba_ref, gp_ref, dq_ref, dk_ref, dv_ref, dgates_ref,
             dc_ref, dba_ref, dcw_ref, dgp_ref, ds_ref):
        @pl.when(pl.program_id(0) == 0)
        def _():
            dcw_ref[...] = jnp.zeros_like(dcw_ref)
            dgp_ref[...] = jnp.zeros_like(dgp_ref)

        halo = jnp.where(pl.program_id(0) > 0, halo_ref[...], 0.0)
        c, taps = _conv_taps(x_ref[...], halo, cw_ref[...])
        sig = _sigmoid(c)
        s = c * sig
        for hh in range(GDN_HEADS):
            sl = slice(hh * LANES, (hh + 1) * LANES)
            sq = s[:, sl]
            rq = lax.rsqrt(jnp.sum(sq * sq, axis=-1, keepdims=True) + NORM_EPS)
            qh = sq * rq
            dqv = dq_ref[:, sl]
            ds_ref[:, sl] = GDN_QSCALE * rq * (dqv - qh * jnp.sum(dqv * qh, axis=-1, keepdims=True))
            sl2 = slice(D_MODEL + hh * LANES, D_MODEL + (hh + 1) * LANES)
            sk = s[:, sl2]
            rk = lax.rsqrt(jnp.sum(sk * sk, axis=-1, keepdims=True) + NORM_EPS)
            kh = sk * rk
            dkv = dk_ref[:, sl]
            ds_ref[:, sl2] = rk * (dkv - kh * jnp.sum(dkv * kh, axis=-1, keepdims=True))
        ds_ref[:, 2 * D_MODEL:] = dv_ref[...]
        dc = ds_ref[...] * (sig * (1.0 + c * (1.0 - sig)))
        dc_ref[...] = dc
        for i in range(GDN_CONV):
            dcw_ref[i:i + 1, :] += jnp.sum(dc * taps[i], axis=0, keepdims=True)
        lane, beta, pre, g, _ = _gates(ba_ref[...], gp_ref[...])
        dgt = dgates_ref[...]
        db = dgt * beta * (1.0 - beta)
        dpre = dgt * (-jnp.exp(gp_ref[0:1, :])) * _sigmoid(pre)
        isa = jnp.logical_and(lane >= GDN_HEADS, lane < 2 * GDN_HEADS)
        dba_ref[...] = jnp.where(lane < GDN_HEADS, db, jnp.where(isa, dpre, 0.0)).astype(BF16)
        dgp_ref[0:1, :] += jnp.sum(jnp.where(isa, dgt * g, 0.0), axis=0, keepdims=True)
        dgp_ref[1:2, :] += jnp.sum(jnp.where(isa, dpre, 0.0), axis=0, keepdims=True)

    row = lambda w: pl.BlockSpec((tm, w), lambda i: (i, 0))
    full = lambda a: pl.BlockSpec(a.shape, lambda i: (0, 0))
    halo = pl.BlockSpec((8, GDN_CONV_W), lambda i: (jnp.maximum(i * (tm // 8) - 1, 0), 0))
    return pl.pallas_call(
        body, name="gdn_conv_bwd_a", grid=(t // tm,),
        in_specs=[row(GDN_CONV_W), halo, full(cw), row(LANES), full(gp), row(D_MODEL), row(D_MODEL),
                  row(D_MODEL), row(LANES)],
        out_specs=[row(GDN_CONV_W), row(LANES), pl.BlockSpec((8, GDN_CONV_W), lambda i: (0, 0)),
                   pl.BlockSpec((8, LANES), lambda i: (0, 0))],
        out_shape=[jax.ShapeDtypeStruct((t, GDN_CONV_W), F32),
                   jax.ShapeDtypeStruct((t, LANES), BF16),
                   jax.ShapeDtypeStruct((8, GDN_CONV_W), F32),
                   jax.ShapeDtypeStruct((8, LANES), F32)],
        scratch_shapes=[pltpu.VMEM((tm, GDN_CONV_W), F32)],
        compiler_params=_params(1),
    )(x, x, cw, ba, gp, dq, dk, dv, dgates)


def _gdn_conv_bwd_b(dc, cw):
    t = dc.shape[0]
    tm = _row_tile(t, GDN_CONV_TM)
    nsteps = t // tm

    def body(dc_ref, halo_ref, cw_ref, dx_ref):
        halo = jnp.where(pl.program_id(0) < nsteps - 1, halo_ref[...], 0.0)
        dd = jnp.concatenate([dc_ref[...], halo], axis=0)
        cw_v = cw_ref[...]
        acc = dd[GDN_CONV - 1:GDN_CONV - 1 + tm] * cw_v[0:1]
        for i in range(1, GDN_CONV):
            acc = acc + dd[GDN_CONV - 1 - i:GDN_CONV - 1 - i + tm] * cw_v[i:i + 1]
        dx_ref[...] = acc.astype(BF16)

    row = pl.BlockSpec((tm, GDN_CONV_W), lambda i: (i, 0))
    halo = pl.BlockSpec((8, GDN_CONV_W), lambda i: (jnp.minimum((i + 1) * (tm // 8), t // 8 - 1), 0))
    return pl.pallas_call(
        body, name="gdn_conv_bwd_b", grid=(nsteps,),
        in_specs=[row, halo, pl.BlockSpec(cw.shape, lambda i: (0, 0))],
        out_specs=row,
        out_shape=jax.ShapeDtypeStruct((t, GDN_CONV_W), BF16),
        compiler_params=_params(1),
    )(dc, dc, cw)


def _gdn_proj_bwd(dy, h, nw, w_qkv, w_z, w_ba, dx, dz, dba):
    t = h.shape[0]
    tm = _row_tile(t, GDN_TM)

    def body(dy_ref, h_ref, nw_ref, wq_ref, wz_ref, wb_ref, dx_ref, dz_ref, dba_ref, dh_ref, dnw_ref):
        @pl.when(pl.program_id(0) == 0)
        def _():
            dnw_ref[...] = jnp.zeros_like(dnw_ref)

        dhn = _nt(dx_ref[...], wq_ref[...]) + _nt(dz_ref[...], wz_ref[...]) + _nt(dba_ref[...], wb_ref[...])
        dxx, dnw = _norm_bwd(h_ref[...], nw_ref[...], dhn)
        dh_ref[...] = dy_ref[...] + dxx
        dnw_ref[...] += dnw

    row = lambda w: pl.BlockSpec((tm, w), lambda i: (i, 0))
    full = lambda a: pl.BlockSpec(a.shape, lambda i: (0, 0))
    return pl.pallas_call(
        body, name="gdn_proj_bwd", grid=(t // tm,),
        in_specs=[row(D_MODEL), row(D_MODEL), full(nw), full(w_qkv), full(w_z), full(w_ba),
                  row(GDN_CONV_W), row(D_MODEL), row(LANES)],
        out_specs=[row(D_MODEL), pl.BlockSpec((1, D_MODEL), lambda i: (0, 0))],
        out_shape=[jax.ShapeDtypeStruct((t, D_MODEL), F32), jax.ShapeDtypeStruct((1, D_MODEL), F32)],
        compiler_params=_params(1),
    )(dy, h, nw, w_qkv, w_z, w_ba, dx, dz, dba)


def _loss_head(h, nw, target):
    t = h.shape[0]
    tm = _row_tile(t, 512)

    def body(h_ref, nw_ref, t_ref, loss_ref, dh_ref, dnw_ref):
        @pl.when(pl.program_id(0) == 0)
        def _():
            loss_ref[...] = jnp.zeros_like(loss_ref)
            dnw_ref[...] = jnp.zeros_like(dnw_ref)

        x = h_ref[...]
        nwv = nw_ref[...]
        _, xhat = _rms_stats(x)
        err = xhat * nwv - t_ref[...]
        loss_ref[...] += 0.5 * jnp.sum(jnp.mean(err * err, axis=-1, keepdims=True))
        dx, dnw = _norm_bwd(x, nwv, err * (1.0 / D_MODEL))
        dh_ref[...] = dx
        dnw_ref[...] += dnw

    row = pl.BlockSpec((tm, D_MODEL), lambda i: (i, 0))
    return pl.pallas_call(
        body, name="loss_head", grid=(t // tm,),
        in_specs=[row, pl.BlockSpec((1, D_MODEL), lambda i: (0, 0)), row],
        out_specs=[pl.BlockSpec((8, LANES), lambda i: (0, 0)), row, pl.BlockSpec((1, D_MODEL), lambda i: (0, 0))],
        out_shape=[jax.ShapeDtypeStruct((8, LANES), F32),
                   jax.ShapeDtypeStruct((t, D_MODEL), F32),
                   jax.ShapeDtypeStruct((1, D_MODEL), F32)],
        compiler_params=_params(1),
    )(h, nw, target)


_PEER_FLIPS = [(dx, dy, dc) for dx in (0, 1) for dy in (0, 1) for dc in (0, 1)][1:]


_ANY = pl.BlockSpec(memory_space=pl.ANY)


def _exchange_copies(ins, outs, send_sems, recv_sems, local_sems, scatter):
    x, y, c = lax.axis_index("x"), lax.axis_index("y"), lax.axis_index("c")
    me = 4 * x + 2 * y + c
    copies = []
    for a in range(len(ins)):
        src = ins[a].at[me] if scatter else ins[a]
        copies.append(pltpu.make_async_copy(src, outs[a].at[me], local_sems.at[a]))
    for k, (fx, fy, fc) in enumerate(_PEER_FLIPS):
        px, py, pc = lax.rem(x + fx, 2), lax.rem(y + fy, 2), lax.rem(c + fc, 2)
        peer = 4 * px + 2 * py + pc
        for a in range(len(ins)):
            copies.append(pltpu.make_async_remote_copy(
                src_ref=ins[a].at[peer] if scatter else ins[a],
                dst_ref=outs[a].at[me],
                send_sem=send_sems.at[a, k], recv_sem=recv_sems.at[a, k],
                device_id=(px, py, pc), device_id_type=MESH_ID))
    return copies


def _exchange_shapes(arrs, scatter):
    return [jax.ShapeDtypeStruct((N_DEV,) + (a.shape[1:] if scatter else a.shape), a.dtype) for a in arrs]


def _exchange_sems(n):
    npeer = len(_PEER_FLIPS)
    return [pltpu.SemaphoreType.DMA((n, npeer)), pltpu.SemaphoreType.DMA((n, npeer)),
            pltpu.SemaphoreType.DMA((n,))]


def _exchange(arrs, scatter, name):
    n = len(arrs)

    def body(*refs):
        copies = _exchange_copies(refs[:n], refs[n:2 * n], *refs[2 * n:], scatter)
        for cp in copies:
            cp.start()
        for cp in copies:
            cp.wait()

    return pl.pallas_call(
        body, name=name, in_specs=[_ANY] * n, out_specs=[_ANY] * n,
        out_shape=_exchange_shapes(arrs, scatter), scratch_shapes=_exchange_sems(n),
    )(*arrs)


ADAMW_PART_BLOCK_BYTES = 2 * 1024 * 1024


def _adamw(parts, w, m, v, name):
    nl, r, c = w.shape
    cap = max(8, ADAMW_PART_BLOCK_BYTES // (N_DEV * c * parts[0].dtype.itemsize))
    tr = r
    for cand in range(min(r, cap) // 8 * 8, 7, -8):
        if r % cand == 0:
            tr = cand
            break
    nr = r // tr
    c1 = 1.0 - ADAM_B1 ** ADAM_STEP
    c2 = 1.0 - ADAM_B2 ** ADAM_STEP

    def body(*refs):
        p_refs = refs[:nl]
        w_ref, m_ref, v_ref, g_ref, d_ref, nm_ref, nv_ref = refs[nl:]
        layer = pl.program_id(0)
        for k in range(nl):
            @pl.when(layer == k)
            def _():
                g = p_refs[k][0].astype(F32)
                for s in range(1, N_DEV):
                    g = g + p_refs[k][s].astype(F32)
                nm = ADAM_B1 * m_ref[0] + (1.0 - ADAM_B1) * g
                nv = ADAM_B2 * v_ref[0] + (1.0 - ADAM_B2) * (g * g)
                g_ref[0] = g
                nm_ref[0] = nm
                nv_ref[0] = nv
                d_ref[0] = -ADAM_LR * ((nm / c1) / (jnp.sqrt(nv / c2) + ADAM_EPS) + ADAM_WD * w_ref[0])

    def part_spec(k):
        return pl.BlockSpec((N_DEV, tr, c),
                            lambda l, i: (0, jnp.where(l == k, i, jnp.where(l < k, 0, nr - 1)), 0))

    blk = pl.BlockSpec((1, tr, c), lambda l, i: (l, i, 0))
    return pl.pallas_call(
        body, name=name, grid=(nl, nr),
        in_specs=[part_spec(k) for k in range(nl)] + [blk, blk, blk],
        out_specs=[blk] * 4,
        out_shape=[jax.ShapeDtypeStruct((nl, r, c), F32)] * 4,
        compiler_params=_params(2),
    )(*parts, w, m, v)


def _rope_tables(positions):
    half = ROPE_DIM // 2
    inv_freq = ROPE_THETA ** (-jnp.arange(0, ROPE_DIM, 2, dtype=F32) / ROPE_DIM)
    ang = positions.astype(F32)[:, None] * inv_freq
    cos, sin = jnp.cos(ang), jnp.sin(ang)
    t = positions.shape[0]
    zeros = lambda w: jnp.zeros((t, w), F32)
    c = jnp.concatenate([cos, cos, jnp.ones((t, ATTN_HEAD_DIM - ROPE_DIM), F32)], axis=1)
    s1 = jnp.concatenate([-sin, zeros(ATTN_HEAD_DIM - half)], axis=1)
    s2 = jnp.concatenate([zeros(half), sin, zeros(ATTN_HEAD_DIM - ROPE_DIM)], axis=1)
    return tuple(jnp.tile(a, (1, LANES // ATTN_HEAD_DIM)) for a in (c, s1, s2))


def _ffn_layer_fwd(h, nw, w, carry):
    (out, g, u, hn), landed = _ffn_fwd(h, nw, w["wg"], w["wu"], w["wd"], carry, False)
    return out, (h, g, u, hn), landed


def _ffn_layer_bwd(dy, saved, nw, w, carry, early=None):
    h, g, u, hn = saved
    (dh, dg, du, act, dz, dnw), landed = _ffn_bwd(dy, h, nw, g, u, w["wg"], w["wu"], w["wd"], carry, True)
    dwg = _matmul_tn(hn, dg, FFN_TF, "ffn_dwg")
    dwu = _matmul_tn(hn, du, FFN_TF, "ffn_dwu")
    if early is None:
        dwd, early_landed = _matmul_tn(act, dz, D_MODEL, "ffn_dwd"), ()
    else:
        dwd, early_landed = _matmul_tn(act, dz, D_MODEL, "ffn_dwd", early(dwg, dwu))
    return dh, dnw, (dwg, dwu, dwd), landed, early_landed


def _attn_layer_fwd(h, nw, w, ropes):
    q, k, v, hn = _attn_qkv_fwd(h, nw, w["w_in"], w["b_in"], *ropes)
    o = _attn_core_fwd(q, k, v, w["sinks"])
    out = _proj_out_fwd(o, w["w_out"], w["b_out"], h)
    return out, (h, hn, q, k, v, o)


def _attn_layer_bwd(dy, saved, nw, w, ropes):
    h, hn, q, k, v, o = saved
    do, dyb, db_out = _proj_out_bwd(dy, w["w_out"])
    dw_out = _matmul_tn(o, dyb, D_MODEL, "attn_dw_out")
    dq, dk, dv, dsink = _attn_core_bwd(q, k, v, do, w["sinks"])
    dh, dqkv, db_in, dnw = _attn_qkv_bwd(dy, h, nw, w["w_in"], dq, dk, dv, *ropes)
    dw_in = _matmul_tn(hn, dqkv, ATTN_IN, "attn_dw_in")
    return dh, dnw, dict(w_in=dw_in, b_in=db_in, sinks=dsink[:, 0], w_out=dw_out, b_out=db_out)


def _gdn_layer_fwd(h, nw, w):
    x, z, ba, hn = _gdn_proj_fwd(h, nw, w["w_qkv"], w["w_z"], w["w_ba"])
    q, k, v, gates = _gdn_conv_fwd(x, w["conv_w"], ba, w["gp"])
    u, ww, qd, kd, aqk, cd = _gdn_intra_fwd(q, k, v, gates)
    o, st = _gdn_scan_fwd(u, ww, qd, kd, aqk, cd)
    out, gated = _gdn_out_fwd(o, z, w["norm_w"], w["w_out"], h)
    return out, (h, hn, x, z, ba, q, k, v, gates, u, ww, qd, kd, aqk, cd, st, o, gated)


def _gdn_layer_bwd(dy, saved, nw, w):
    h, hn, x, z, ba, q, k, v, gates, u, ww, qd, kd, aqk, cd, st, o, gated = saved
    do, dz, dyb, dnorm_w = _gdn_out_bwd(dy, w["w_out"], o, z, w["norm_w"])
    dw_out = _matmul_tn(gated, dyb, D_MODEL, "gdn_dw_out")
    du, dw, dqd, dkd, daqk, daqkt, dcd = _gdn_scan_bwd(u, ww, qd, kd, aqk, cd, st, do)
    dq, dk, dv, dgates = _gdn_intra_bwd(q, k, v, gates, u, ww, du, dw, dqd, dkd, daqk, daqkt, dcd)
    dc, dba, dcw, dgp = _gdn_conv_bwd_a(x, w["conv_w"], ba, w["gp"], dq, dk, dv, dgates)
    dx = _gdn_conv_bwd_b(dc, w["conv_w"])
    dh, dnw = _gdn_proj_bwd(dy, h, nw, w["w_qkv"], w["w_z"], w["w_ba"], dx, dz, dba)
    dw_qkv = _matmul_tn(hn, dx, GDN_CONV_W // 2, "gdn_dw_qkv")
    dw_z = _matmul_tn(hn, dz, D_MODEL, "gdn_dw_z")
    dw_ba = _matmul_tn(hn, dba, LANES, "gdn_dw_ba")
    dw_in = jnp.concatenate([dw_qkv, dw_z, dw_ba[:, :2 * GDN_HEADS]], axis=1)
    lo, hi = GDN_HEADS, 2 * GDN_HEADS
    return dh, dnw, dict(w_in=dw_in, conv_w=dcw[:GDN_CONV], A_log=dgp[0, lo:hi], dt_bias=dgp[1, lo:hi],
                         norm_w=dnorm_w[0], w_out=dw_out)


def _local_step(x, positions, target, norms, final_norm, plan):
    ropes = _rope_tables(positions)
    h = x
    saved = []
    for layer in range(DEPTH):
        h, s1, landed = _ffn_layer_fwd(h, norms["ffn1"][layer], plan.weights("ffn1", layer),
                                       plan.fwd_carry("ffn1", layer))
        plan.fwd_landed(landed)
        if layer % 2 == 0:
            h, s2 = _attn_layer_fwd(h, norms["mix"][layer], plan.weights("mix", layer), ropes)
        else:
            h, s2 = _gdn_layer_fwd(h, norms["mix"][layer], plan.weights("mix", layer))
        h, s3, landed = _ffn_layer_fwd(h, norms["ffn2"][layer], plan.weights("ffn2", layer),
                                       plan.fwd_carry("ffn2", layer))
        plan.fwd_landed(landed)
        saved.append((s1, s2, s3))
    loss, dh, dfinal = _loss_head(h, final_norm, target)

    g_norm = {k: [None] * DEPTH for k in ("ffn1", "mix", "ffn2")}
    for layer in reversed(range(DEPTH)):
        s1, s2, s3 = saved[layer]
        dh, g_norm["ffn2"][layer], gw, landed, _ = _ffn_layer_bwd(
            dh, s3, norms["ffn2"][layer], plan.weights("ffn2", layer), plan.bwd_carry("ffn2", layer))
        plan.bwd_landed(landed)
        plan.grads("ffn2", layer, gw)
        if layer % 2 == 0:
            dh, g_norm["mix"][layer], gw = _attn_layer_bwd(dh, s2, norms["mix"][layer], plan.weights("mix", layer),
                                                           ropes)
        else:
            dh, g_norm["mix"][layer], gw = _gdn_layer_bwd(dh, s2, norms["mix"][layer], plan.weights("mix", layer))
        plan.grads("mix", layer, gw)
        dh, g_norm["ffn1"][layer], gw, landed, early_landed = _ffn_layer_bwd(
            dh, s1, norms["ffn1"][layer], plan.weights("ffn1", layer), plan.bwd_carry("ffn1", layer),
            plan.early_parts("ffn1", layer))
        plan.bwd_landed(landed)
        plan.early_landed("ffn1", layer, early_landed)
        plan.grads("ffn1", layer, gw)
    return loss, dh, dfinal, g_norm


def _cols_full(g):
    return jnp.transpose(g, (1, 0, 2)).reshape(g.shape[1], -1)


def _rows_full(g):
    return g.reshape(-1, g.shape[-1])


class _ShardedWeights:
    def __init__(self, shards, small):
        self.shards, self.small = shards, small
        self.whole, self.pending, self.received, self.small_grads, self.early = {}, {}, {}, {}, {}
        self.in_flight = []
        first = [("ffn1", 0)]
        self.in_flight = first
        self.fwd_landed(_exchange(self._shards_of(first), False, "gather_first"))

    def _group(self, key):
        kind, layer = key
        j = layer // 2
        if kind != "mix":
            return [self.shards[kind + "_w_gate_up"][layer], self.shards[kind + "_w_down"][layer]]
        if layer % 2 == 0:
            return [self.shards["attn_w_in"][j], self.shards["attn_w_out"][j]]
        return [self.shards["gdn_w_in"][j], self.shards["gdn_w_out"][j], self.shards["gdn_conv_w"][j]]

    def _shards_of(self, keys):
        return [a for key in keys for a in self._group(key)]

    def weights(self, kind, layer):
        return self.whole[(kind, layer)]

    def fwd_carry(self, kind, layer):
        if kind == "ffn1":
            keys = [("mix", 0), ("ffn2", 0)] if layer == 0 else [("ffn2", layer)]
        else:
            keys = [("ffn1", layer + 1), ("mix", layer + 1)] if layer + 1 < DEPTH else []
        self.in_flight = keys
        return self._shards_of(keys)

    def fwd_landed(self, landed):
        landed = list(landed)
        for key in self.in_flight:
            kind, layer = key
            j = layer // 2
            sm = self.small
            if kind != "mix":
                gu, dn = landed[:2]
                w = dict(wg=_cols_full(gu[:4]), wu=_cols_full(gu[4:]), wd=_rows_full(dn))
                landed = landed[2:]
            elif layer % 2 == 0:
                w = dict(w_in=_cols_full(landed[0]), w_out=_rows_full(landed[1]), b_in=sm["attn_b_in"][j][None],
                         sinks=sm["attn_sinks"][j][None], b_out=sm["attn_b_out"][j][None])
                landed = landed[2:]
            else:
                w_in = _cols_full(landed[0])
                w_ba = jnp.pad(w_in[:, GDN_CONV_W + D_MODEL:], ((0, 0), (0, LANES - 2 * GDN_HEADS)))
                gp = jnp.pad(jnp.stack([sm["gdn_A_log"][j], sm["gdn_dt_bias"][j]]),
                             ((0, 6), (GDN_HEADS, LANES - 2 * GDN_HEADS)))
                w = dict(w_qkv=w_in[:, :GDN_CONV_W], w_z=w_in[:, GDN_CONV_W:GDN_CONV_W + D_MODEL], w_ba=w_ba,
                         conv_w=_cols_full(landed[2]), gp=gp, norm_w=sm["gdn_norm_w"][j][None],
                         w_out=_rows_full(landed[1]))
                landed = landed[3:]
            self.whole[key] = w
        self.in_flight = []

    @staticmethod
    def _gate_up_part(dwg, dwu):
        return jnp.concatenate([_cols_shards(dwg, 4), _cols_shards(dwu, 4)], axis=0).astype(BF16)

    def early_parts(self, kind, layer):
        if (kind, layer) != ("ffn1", 0):
            return None
        return lambda dwg, dwu: [self._gate_up_part(dwg, dwu)]

    def early_landed(self, kind, layer, landed):
        if len(landed):
            self.early[(kind, layer)] = list(landed)

    def grads(self, kind, layer, g):
        if kind != "mix":
            dwg, dwu, dwd = g
            parts = [_rows_shards(dwd).astype(BF16)]
            if (kind, layer) not in self.early:
                parts = [self._gate_up_part(dwg, dwu)] + parts
        else:
            parts = [_cols_shards(g["w_in"]).astype(BF16), _rows_shards(g["w_out"]).astype(BF16)]
            if layer % 2 == 1:
                parts.append(_cols_shards(g["conv_w"]))
            self.small_grads[layer] = g
        self.pending[(kind, layer)] = parts

    def bwd_carry(self, kind, layer):
        if kind == "ffn1":
            keys = [("ffn2", layer), ("mix", layer)]
        else:
            keys = [("ffn1", layer + 1)] if layer + 1 < DEPTH else []
        self.in_flight = keys
        return [a for key in keys for a in self.pending[key]]

    def bwd_landed(self, landed):
        landed = list(landed)
        for key in self.in_flight:
            n = len(self.pending.pop(key))
            self.received[key], landed = landed[:n], landed[n:]
        self.in_flight = []

    def finish(self):
        self.in_flight = list(self.pending)
        self.bwd_landed(_exchange([a for key in self.in_flight for a in self.pending[key]], True, "scatter_last"))
        for key, landed in self.early.items():
            self.received[key] = landed + self.received[key]


def _cols_shards(full, n=N_DEV):
    r = full.shape[0]
    return jnp.transpose(full.reshape(r, n, -1), (1, 0, 2))


def _rows_shards(full):
    return full.reshape(N_DEV, -1, full.shape[-1])


_SMALL = ("ffn1_norm", "mix_norm", "ffn2_norm", "attn_b_in", "attn_sinks", "attn_b_out",
          "gdn_A_log", "gdn_dt_bias", "gdn_norm_w", "final_norm")
_BIG = ("ffn1_w_gate_up", "ffn1_w_down", "ffn2_w_gate_up", "ffn2_w_down", "attn_w_in", "attn_w_out",
        "gdn_w_in", "gdn_conv_w", "gdn_w_out")
_WEIGHTS = ("ffn1_norm", "ffn1_w_gate_up", "ffn1_w_down", "mix_norm", "ffn2_norm", "ffn2_w_gate_up",
            "ffn2_w_down", "attn_w_in", "attn_b_in", "attn_sinks", "attn_w_out", "attn_b_out", "gdn_w_in",
            "gdn_conv_w", "gdn_A_log", "gdn_dt_bias", "gdn_norm_w", "gdn_w_out", "final_norm")


def _pack_small(vals):
    flat = jnp.concatenate([v.reshape(-1).astype(F32) for v in vals])
    pad = (-flat.shape[0]) % (8 * LANES)
    return jnp.pad(flat, (0, pad)).reshape(-1, LANES)


def _unpack_small(packed, shapes):
    flat = packed.reshape(-1)
    out, off = [], 0
    for s in shapes:
        size = 1
        for d in s:
            size *= d
        out.append(flat[off:off + size].reshape(s))
        off += size
    return out


def kernel(x, positions, ffn1_norm, ffn1_w_gate_up, ffn1_w_down, mix_norm, ffn2_norm, ffn2_w_gate_up, ffn2_w_down, attn_w_in, attn_b_in, attn_sinks, attn_w_out, attn_b_out, gdn_w_in, gdn_conv_w, gdn_A_log, gdn_dt_bias, gdn_norm_w, gdn_w_out, final_norm, loss_target, m_ffn1_norm, m_ffn1_w_gate_up, m_ffn1_w_down, m_mix_norm, m_ffn2_norm, m_ffn2_w_gate_up, m_ffn2_w_down, m_attn_w_in, m_attn_b_in, m_attn_sinks, m_attn_w_out, m_attn_b_out, m_gdn_w_in, m_gdn_conv_w, m_gdn_A_log, m_gdn_dt_bias, m_gdn_norm_w, m_gdn_w_out, m_final_norm, v_ffn1_norm, v_ffn1_w_gate_up, v_ffn1_w_down, v_mix_norm, v_ffn2_norm, v_ffn2_w_gate_up, v_ffn2_w_down, v_attn_w_in, v_attn_b_in, v_attn_sinks, v_attn_w_out, v_attn_b_out, v_gdn_w_in, v_gdn_conv_w, v_gdn_A_log, v_gdn_dt_bias, v_gdn_norm_w, v_gdn_w_out, v_final_norm):
    args = dict(locals())
    wts = {n: args[n] for n in _WEIGHTS}
    moms = {n: args["m_" + n] for n in _WEIGHTS}
    vels = {n: args["v_" + n] for n in _WEIGHTS}

    shards = {n: wts[n] if n == "gdn_conv_w" else wts[n].astype(BF16) for n in _BIG}
    plan = _ShardedWeights(shards, wts)
    norms = dict(ffn1=ffn1_norm[:, None, :], mix=mix_norm[:, None, :], ffn2=ffn2_norm[:, None, :])
    loss, grad_x, dfinal, g_norm = _local_step(x[0], positions[0], loss_target[0], norms, final_norm[None], plan)
    plan.finish()

    layers = range(DEPTH)
    got = lambda kind, ls, i: [plan.received[(kind, l)][i] for l in ls]
    received = {"attn_w_in": got("mix", layers[0::2], 0), "attn_w_out": got("mix", layers[0::2], 1),
                "gdn_w_in": got("mix", layers[1::2], 0), "gdn_w_out": got("mix", layers[1::2], 1),
                "gdn_conv_w": got("mix", layers[1::2], 2)}
    for kind in ("ffn1", "ffn2"):
        received[kind + "_w_gate_up"] = got(kind, layers, 0)
        received[kind + "_w_down"] = got(kind, layers, 1)
    g_attn = [plan.small_grads[l] for l in layers[0::2]]
    g_gdn = [plan.small_grads[l] for l in layers[1::2]]


    small_g = dict(
        ffn1_norm=jnp.concatenate(g_norm["ffn1"], axis=0), mix_norm=jnp.concatenate(g_norm["mix"], axis=0),
        ffn2_norm=jnp.concatenate(g_norm["ffn2"], axis=0),
        attn_b_in=jnp.concatenate([g["b_in"] for g in g_attn], axis=0),
        attn_sinks=jnp.stack([g["sinks"] for g in g_attn]),
        attn_b_out=jnp.concatenate([g["b_out"] for g in g_attn], axis=0),
        gdn_A_log=jnp.stack([g["A_log"] for g in g_gdn]), gdn_dt_bias=jnp.stack([g["dt_bias"] for g in g_gdn]),
        gdn_norm_w=jnp.stack([g["norm_w"] for g in g_gdn]), final_norm=dfinal[0])
    small_parts = _exchange([_pack_small([small_g[n] for n in _SMALL] + [loss[0, :1]])], False, "gather_small")[0]
    pad1 = jnp.zeros((1,), F32)
    sw = _pack_small([wts[n] for n in _SMALL] + [pad1])
    sm = _pack_small([moms[n] for n in _SMALL] + [pad1])
    sv = _pack_small([vels[n] for n in _SMALL] + [pad1])
    shapes = [wts[n].shape for n in _SMALL] + [(1,)]
    small_out = [_unpack_small(o[0], shapes)
                 for o in _adamw([small_parts], sw[None], sm[None], sv[None], "adamw_small")]
    results = {n: tuple(o[i] for o in small_out) for i, n in enumerate(_SMALL)}
    loss_total = small_out[0][-1][0]

    for n in _BIG:
        results[n] = tuple(_adamw(received[n], wts[n], moms[n], vels[n], "adamw_" + n))

    return (loss_total, grad_x[None],
            *[results[n][0] for n in _WEIGHTS], *[results[n][1] for n in _WEIGHTS],
            *[results[n][2] for n in _WEIGHTS], *[results[n][3] for n in _WEIGHTS])
```

```python
import jax
import jax.numpy as jnp
from jax import lax
from jax.experimental import pallas as pl
from jax.experimental.pallas import tpu as pltpu

F32 = jnp.float32
BF16 = jnp.bfloat16

D_MODEL = 1024
DEPTH = 4
D_FF = 2816
NORM_EPS = 1e-6
N_DEV = 8

ATTN_Q_HEADS = 16
ATTN_KV_HEADS = 4
ATTN_HEAD_DIM = 64
ATTN_GROUP = 4
ATTN_BLOCK = 128
ROPE_DIM = 16
ROPE_THETA = 500000.0
ATTN_Q_W = 1024
ATTN_KV_W = 256
ATTN_IN = 1536
ATTN_SCALE = ATTN_HEAD_DIM ** -0.5

GDN_HEADS = 8
GDN_DK = 128
GDN_CONV = 4
GDN_CHUNK = 64
GDN_CONV_W = 3072
GDN_IN = 4112
GDN_QSCALE = GDN_DK ** -0.5
GDN_ROWS = 1024
GDN_SCAN_ROWS = 1024
GDN_INTRA_HEADS = 2
GDN_SCAN_HEADS = 2
GDN_SUB_SHIFT = 4

ADAM_LR = 0.001
ADAM_B1 = 0.9
ADAM_B2 = 0.999
ADAM_EPS = 1e-08
ADAM_WD = 0.01
ADAM_STEP = 10

LANES = 128
NEG_BIG = -1e30
VMEM_LIMIT_BYTES = 56 * 1024 * 1024
MESH_ID = pl.DeviceIdType.MESH


def _params(n_axes, vmem_limit_bytes=VMEM_LIMIT_BYTES):
    return pltpu.CompilerParams(dimension_semantics=("arbitrary",) * n_axes,
                                vmem_limit_bytes=vmem_limit_bytes)


def _nn(a, b):
    return jnp.dot(a, b, preferred_element_type=F32)


def _nt(a, b):
    return lax.dot_general(a, b, (((1,), (1,)), ((), ())), preferred_element_type=F32)


def _tn(a, b):
    return lax.dot_general(a, b, (((0,), (0,)), ((), ())), preferred_element_type=F32)


def _bnn(a, b, precision=None):
    return lax.dot_general(a, b, (((2,), (1,)), ((0,), (0,))), precision=precision,
                           preferred_element_type=F32)


def _bnt(a, b, precision=None):
    return lax.dot_general(a, b, (((2,), (2,)), ((0,), (0,))), precision=precision,
                           preferred_element_type=F32)


def _sigmoid(x):
    return 1.0 / (1.0 + jnp.exp(-x))


def _rms_stats(x):
    r = lax.rsqrt(jnp.mean(x * x, axis=-1, keepdims=True) + NORM_EPS)
    return r, x * r


def _norm_bwd(x, nw, dhn):
    r, xhat = _rms_stats(x)
    dxh = dhn * nw
    dx = r * (dxh - xhat * jnp.mean(dxh * xhat, axis=-1, keepdims=True))
    dnw = jnp.sum(dhn * xhat, axis=0, keepdims=True)
    return dx, dnw


def _row_tile(t, pref):
    return min(t, pref)


FFN_TM = 512
FFN_BWD_TM = 512
FFN_BWD_TF = 1408
FFN_BWD_VMEM_LIMIT_BYTES = 61 * 1024 * 1024
FFN_TF = 1408


def _carried(body, n_in, n_out, carry, scatter, last_step):
    nc = len(carry)
    if not nc:
        return body

    def wrapped(*refs):
        ins, cin = refs[:n_in], refs[n_in:n_in + nc]
        outs = refs[n_in + nc:n_in + nc + n_out]
        cout = refs[n_in + nc + n_out:n_in + 2 * nc + n_out]
        scratch = refs[n_in + 2 * nc + n_out:]
        sems = scratch[len(scratch) - 3:]
        ids = [pl.program_id(ax) for ax in range(len(last_step))]
        first, last = ids[0] == 0, ids[0] == last_step[0]
        for ax in range(1, len(last_step)):
            first = jnp.logical_and(first, ids[ax] == 0)
            last = jnp.logical_and(last, ids[ax] == last_step[ax])

        @pl.when(first)
        def _():
            for cp in _exchange_copies(cin, cout, *sems, scatter):
                cp.start()

        body(*ins, *outs, *scratch[:len(scratch) - 3])

        @pl.when(last)
        def _():
            for cp in _exchange_copies(cin, cout, *sems, scatter):
                cp.wait()

    return wrapped


def _ffn_fwd(h, nw, wg, wu, wd, carry=(), scatter=False):
    t = h.shape[0]
    tm, tf = _row_tile(t, FFN_TM), FFN_TF
    nj = D_FF // tf
    nc = len(carry)

    def body(h_ref, nw_ref, wg_ref, wu_ref, wd_ref, out_ref, g_ref, u_ref, hn_ref, acc_ref):
        j = pl.program_id(1)

        @pl.when(j == 0)
        def _():
            _, xhat = _rms_stats(h_ref[...])
            hn_ref[...] = (xhat * nw_ref[...]).astype(BF16)
            acc_ref[...] = jnp.zeros_like(acc_ref)

        hn = hn_ref[...]
        g = _nn(hn, wg_ref[...])
        u = _nn(hn, wu_ref[...])
        g_ref[...] = g.astype(BF16)
        u_ref[...] = u.astype(BF16)
        a = (g * _sigmoid(g) * u).astype(BF16)
        acc_ref[...] += _nn(a, wd_ref[...])

        @pl.when(j == nj - 1)
        def _():
            out_ref[...] = h_ref[...] + 0.5 * acc_ref[...]

    outs = pl.pallas_call(
        _carried(body, 5, 4, carry, scatter, (t // tm - 1, nj - 1)),
        name="ffn_fwd_x" if nc else "ffn_fwd", grid=(t // tm, nj),
        in_specs=[pl.BlockSpec((tm, D_MODEL), lambda i, j: (i, 0)),
                  pl.BlockSpec((1, D_MODEL), lambda i, j: (0, 0)),
                  pl.BlockSpec((D_MODEL, tf), lambda i, j: (0, j)),
                  pl.BlockSpec((D_MODEL, tf), lambda i, j: (0, j)),
                  pl.BlockSpec((tf, D_MODEL), lambda i, j: (j, 0))] + [_ANY] * nc,
        out_specs=[pl.BlockSpec((tm, D_MODEL), lambda i, j: (i, 0)),
                   pl.BlockSpec((tm, tf), lambda i, j: (i, j)),
                   pl.BlockSpec((tm, tf), lambda i, j: (i, j)),
                   pl.BlockSpec((tm, D_MODEL), lambda i, j: (i, 0))] + [_ANY] * nc,
        out_shape=[jax.ShapeDtypeStruct((t, D_MODEL), F32),
                   jax.ShapeDtypeStruct((t, D_FF), BF16),
                   jax.ShapeDtypeStruct((t, D_FF), BF16),
                   jax.ShapeDtypeStruct((t, D_MODEL), BF16)] + _exchange_shapes(carry, scatter),
        scratch_shapes=[pltpu.VMEM((tm, D_MODEL), F32)] + (_exchange_sems(nc) if nc else []),
        compiler_params=_params(2),
    )(h, nw, wg, wu, wd, *carry)
    return outs[:4], outs[4:]


def _ffn_bwd(dy, h, nw, g, u, wg, wu, wd, carry=(), scatter=True):
    t = h.shape[0]
    tm, tf = _row_tile(t, FFN_BWD_TM), FFN_BWD_TF
    nj = D_FF // tf
    nc = len(carry)

    def body(dy_ref, h_ref, nw_ref, g_ref, u_ref, wg_ref, wu_ref, wd_ref,
             dh_ref, dg_ref, du_ref, a_ref, dz_ref, dnw_ref, acc_ref):
        i, j = pl.program_id(0), pl.program_id(1)

        @pl.when(j == 0)
        def _():
            dz_ref[...] = (0.5 * dy_ref[...]).astype(BF16)
            acc_ref[...] = jnp.zeros_like(acc_ref)

        @pl.when(jnp.logical_and(i == 0, j == 0))
        def _():
            dnw_ref[...] = jnp.zeros_like(dnw_ref)

        da = _nt(dz_ref[...], wd_ref[...])
        gv = g_ref[...].astype(F32)
        uv = u_ref[...].astype(F32)
        sig = _sigmoid(gv)
        silu = gv * sig
        dg = (da * uv * (sig * (1.0 + gv * (1.0 - sig)))).astype(BF16)
        du = (da * silu).astype(BF16)
        dg_ref[...] = dg
        du_ref[...] = du
        a_ref[...] = (silu * uv).astype(BF16)
        acc_ref[...] += _nt(dg, wg_ref[...]) + _nt(du, wu_ref[...])

        @pl.when(j == nj - 1)
        def _():
            dx, dnw = _norm_bwd(h_ref[...], nw_ref[...], acc_ref[...])
            dh_ref[...] = dy_ref[...] + dx
            dnw_ref[...] += dnw

    outs = pl.pallas_call(
        _carried(body, 8, 6, carry, scatter, (t // tm - 1, nj - 1)),
        name="ffn_bwd_x" if nc else "ffn_bwd", grid=(t // tm, nj),
        in_specs=[pl.BlockSpec((tm, D_MODEL), lambda i, j: (i, 0)),
                  pl.BlockSpec((tm, D_MODEL), lambda i, j: (i, 0)),
                  pl.BlockSpec((1, D_MODEL), lambda i, j: (0, 0)),
                  pl.BlockSpec((tm, tf), lambda i, j: (i, j)),
                  pl.BlockSpec((tm, tf), lambda i, j: (i, j)),
                  pl.BlockSpec((D_MODEL, tf), lambda i, j: (0, j)),
                  pl.BlockSpec((D_MODEL, tf), lambda i, j: (0, j)),
                  pl.BlockSpec((tf, D_MODEL), lambda i, j: (j, 0))] + [_ANY] * nc,
        out_specs=[pl.BlockSpec((tm, D_MODEL), lambda i, j: (i, 0)),
                   pl.BlockSpec((tm, tf), lambda i, j: (i, j)),
                   pl.BlockSpec((tm, tf), lambda i, j: (i, j)),
                   pl.BlockSpec((tm, tf), lambda i, j: (i, j)),
                   pl.BlockSpec((tm, D_MODEL), lambda i, j: (i, 0)),
                   pl.BlockSpec((1, D_MODEL), lambda i, j: (0, 0))] + [_ANY] * nc,
        out_shape=[jax.ShapeDtypeStruct((t, D_MODEL), F32),
                   jax.ShapeDtypeStruct((t, D_FF), BF16),
                   jax.ShapeDtypeStruct((t, D_FF), BF16),
                   jax.ShapeDtypeStruct((t, D_FF), BF16),
                   jax.ShapeDtypeStruct((t, D_MODEL), BF16),
                   jax.ShapeDtypeStruct((1, D_MODEL), F32)] + _exchange_shapes(carry, scatter),
        scratch_shapes=[pltpu.VMEM((tm, D_MODEL), F32)] + (_exchange_sems(nc) if nc else []),
        compiler_params=_params(2, FFN_BWD_VMEM_LIMIT_BYTES),
    )(dy, h, nw, g, u, wg, wu, wd, *carry)
    return outs[:6], outs[6:]


def _matmul_tn(a, b, tn, name, carry=(), out_dtype=None):
    out_dtype = BF16 if out_dtype is None else out_dtype
    k, m = a.shape
    n = b.shape[1]
    tm = min(m, 1408 if m % 1408 == 0 else 1024)
    tk = min(k, 2048)
    nc = len(carry)

    grid = (m // tm, n // tn, k // tk)

    def body(a_ref, b_ref, o_ref, acc_ref):
        kk = pl.program_id(2)

        @pl.when(kk == 0)
        def _():
            acc_ref[...] = jnp.zeros_like(acc_ref)

        acc_ref[...] += _tn(a_ref[...], b_ref[...])

        @pl.when(kk == grid[2] - 1)
        def _():
            o_ref[...] = acc_ref[...].astype(out_dtype)

    outs = pl.pallas_call(
        _carried(body, 2, 1, carry, True, tuple(g - 1 for g in grid)),
        name=name + "_x" if nc else name, grid=grid,
        in_specs=[pl.BlockSpec((tk, tm), lambda i, j, kk: (kk, i)),
                  pl.BlockSpec((tk, tn), lambda i, j, kk: (kk, j))] + [_ANY] * nc,
        out_specs=[pl.BlockSpec((tm, tn), lambda i, j, kk: (i, j))] + [_ANY] * nc,
        out_shape=[jax.ShapeDtypeStruct((m, n), out_dtype)] + _exchange_shapes(carry, True),
        scratch_shapes=[pltpu.VMEM((tm, tn), F32)] + (_exchange_sems(nc) if nc else []),
        compiler_params=_params(3),
    )(a, b, *carry)
    return (outs[0], outs[1:]) if nc else outs[0]


ATTN_TM = 512


def _rope(t, c, s1, s2):
    return t * c + pltpu.roll(t, LANES - ROPE_DIM // 2, 1) * s1 + pltpu.roll(t, ROPE_DIM // 2, 1) * s2


def _rope_bwd(d, c, s1, s2):
    return d * c + pltpu.roll(d * s1, ROPE_DIM // 2, 1) + pltpu.roll(d * s2, LANES - ROPE_DIM // 2, 1)


def _attn_qkv_fwd(h, nw, w_in, b_in, rc, rs1, rs2):
    t = h.shape[0]
    tm = _row_tile(t, ATTN_TM)

    def body(h_ref, nw_ref, w_ref, b_ref, c_ref, s1_ref, s2_ref, q_ref, k_ref, v_ref, hn_ref):
        _, xhat = _rms_stats(h_ref[...])
        hn = (xhat * nw_ref[...]).astype(BF16)
        hn_ref[...] = hn
        qkv = _nn(hn, w_ref[...]) + b_ref[...]
        c, s1, s2 = c_ref[...], s1_ref[...], s2_ref[...]
        for s in range(ATTN_Q_W // LANES):
            q_ref[:, s * LANES:(s + 1) * LANES] = _rope(qkv[:, s * LANES:(s + 1) * LANES], c, s1, s2).astype(BF16)
        for s in range(ATTN_KV_W // LANES):
            lo = ATTN_Q_W + s * LANES
            k_ref[:, s * LANES:(s + 1) * LANES] = _rope(qkv[:, lo:lo + LANES], c, s1, s2).astype(BF16)
        v_ref[...] = qkv[:, ATTN_Q_W + ATTN_KV_W:].astype(BF16)

    row = lambda w: pl.BlockSpec((tm, w), lambda i: (i, 0))
    full = lambda a: pl.BlockSpec(a.shape, lambda i: (0, 0))
    return pl.pallas_call(
        body, name="attn_qkv_fwd", grid=(t // tm,),
        in_specs=[row(D_MODEL), full(nw), full(w_in), full(b_in), row(LANES), row(LANES), row(LANES)],
        out_specs=[row(ATTN_Q_W), row(ATTN_KV_W), row(ATTN_KV_W), row(D_MODEL)],
        out_shape=[jax.ShapeDtypeStruct((t, ATTN_Q_W), BF16),
                   jax.ShapeDtypeStruct((t, ATTN_KV_W), BF16),
                   jax.ShapeDtypeStruct((t, ATTN_KV_W), BF16),
                   jax.ShapeDtypeStruct((t, D_MODEL), BF16)],
        compiler_params=_params(1),
    )(h, nw, w_in, b_in, rc, rs1, rs2)


def _attn_group(q_ref, kc_ref, kp_ref, vc_ref, vp_ref, sinks_ref, hk, n):
    hd = ATTN_HEAD_DIM
    cols = slice(hk * hd, (hk + 1) * hd)
    krow = lax.broadcasted_iota(jnp.int32, (2 * ATTN_BLOCK, hd), 0)
    kcat = jnp.concatenate([kp_ref[:, cols], kc_ref[:, cols]], axis=0)
    vcat = jnp.concatenate([vp_ref[:, cols], vc_ref[:, cols]], axis=0)
    kcat = jnp.where(krow == 0, jnp.zeros_like(kcat), kcat)
    vcat = jnp.where(krow == 0, jnp.zeros_like(vcat), vcat)
    heads = [hk * ATTN_GROUP + g for g in range(ATTN_GROUP)]
    qs = jnp.concatenate([q_ref[:, hq * hd:(hq + 1) * hd] for hq in heads], axis=0)
    s = _nt(qs, kcat) * ATTN_SCALE
    rows = ATTN_GROUP * ATTN_BLOCK
    ri = lax.broadcasted_iota(jnp.int32, (rows, 2 * ATTN_BLOCK), 0) & (ATTN_BLOCK - 1)
    cj = lax.broadcasted_iota(jnp.int32, (rows, 2 * ATTN_BLOCK), 1)
    first = jnp.where(n > 0, 0, 2 * ATTN_BLOCK)
    valid = jnp.logical_or(jnp.logical_and(cj < ATTN_BLOCK, cj > ri + first),
                           jnp.logical_and(cj >= ATTN_BLOCK, cj - ATTN_BLOCK <= ri))
    sink = jnp.concatenate([jnp.full((ATTN_BLOCK, 2 * ATTN_BLOCK), sinks_ref[hq], F32) for hq in heads], axis=0)
    s = jnp.where(valid, s, jnp.where(cj == 0, sink, NEG_BIG))
    p = jnp.exp(s - jnp.max(s, axis=-1, keepdims=True))
    probs = p / jnp.sum(p, axis=-1, keepdims=True)
    return heads, qs, kcat, vcat, probs, krow


def _attn_core_fwd(q, k, v, sinks):
    t = q.shape[0]
    nb = t // ATTN_BLOCK

    def body(q_ref, kc_ref, kp_ref, vc_ref, vp_ref, sinks_ref, o_ref):
        n = pl.program_id(0)
        for hk in range(ATTN_KV_HEADS):
            heads, _, _, vcat, probs, _ = _attn_group(q_ref, kc_ref, kp_ref, vc_ref, vp_ref, sinks_ref, hk, n)
            o = _nn(probs.astype(BF16), vcat)
            for g, hq in enumerate(heads):
                o_ref[:, hq * ATTN_HEAD_DIM:(hq + 1) * ATTN_HEAD_DIM] = (
                    o[g * ATTN_BLOCK:(g + 1) * ATTN_BLOCK].astype(BF16))

    cur = lambda w: pl.BlockSpec((ATTN_BLOCK, w), lambda n: (n, 0))
    prev = lambda w: pl.BlockSpec((ATTN_BLOCK, w), lambda n: (jnp.maximum(n - 1, 0), 0))
    return pl.pallas_call(
        body, name="attn_core_fwd", grid=(nb,),
        in_specs=[cur(ATTN_Q_W), cur(ATTN_KV_W), prev(ATTN_KV_W), cur(ATTN_KV_W), prev(ATTN_KV_W),
                  pl.BlockSpec(memory_space=pltpu.SMEM)],
        out_specs=cur(ATTN_Q_W),
        out_shape=jax.ShapeDtypeStruct((t, ATTN_Q_W), BF16),
        compiler_params=_params(1),
    )(q, k, k, v, v, sinks.reshape(-1))


def _attn_core_bwd(q, k, v, do, sinks):
    t = q.shape[0]
    nb = t // ATTN_BLOCK
    hd = ATTN_HEAD_DIM

    def body(q_ref, kc_ref, kp_ref, vc_ref, vp_ref, do_ref, sinks_ref,
             dq_ref, dk_ref, dv_ref, dsink_ref, ck_ref, cv_ref):
        n = pl.program_id(0)

        @pl.when(n == 0)
        def _():
            dsink_ref[...] = jnp.zeros_like(dsink_ref)
            ck_ref[...] = jnp.zeros_like(ck_ref)
            cv_ref[...] = jnp.zeros_like(cv_ref)

        @pl.when(n == nb)
        def _():
            dk_ref[...] = ck_ref[...]
            dv_ref[...] = cv_ref[...]

        @pl.when(n < nb)
        def _():
            for hk in range(ATTN_KV_HEADS):
                heads, qs, kcat, vcat, probs, krow = _attn_group(
                    q_ref, kc_ref, kp_ref, vc_ref, vp_ref, sinks_ref, hk, n)
                dos = jnp.concatenate([do_ref[:, hq * hd:(hq + 1) * hd] for hq in heads], axis=0)
                dp = _nt(dos, vcat)
                delta = jnp.sum(probs * dp, axis=-1, keepdims=True)
                dsf = probs * (dp - delta)
                ds = dsf.astype(BF16)
                dqs = _nn(ds, kcat) * ATTN_SCALE
                dkc = jnp.where(krow == 0, 0.0, _tn(ds, qs) * ATTN_SCALE)
                dvc = jnp.where(krow == 0, 0.0, _tn(probs.astype(BF16), dos))
                for g, hq in enumerate(heads):
                    blk = slice(g * ATTN_BLOCK, (g + 1) * ATTN_BLOCK)
                    dq_ref[:, hq * hd:(hq + 1) * hd] = dqs[blk]
                    dsink_ref[hq:hq + 1, :] += jnp.broadcast_to(
                        jnp.sum(dsf[blk, 0:1], axis=0, keepdims=True), (1, LANES))
                cols = slice(hk * hd, (hk + 1) * hd)
                dk_ref[:, cols] = ck_ref[:, cols] + dkc[:ATTN_BLOCK]
                dv_ref[:, cols] = cv_ref[:, cols] + dvc[:ATTN_BLOCK]
                ck_ref[:, cols] = dkc[ATTN_BLOCK:]
                cv_ref[:, cols] = dvc[ATTN_BLOCK:]

    cur = lambda w: pl.BlockSpec((ATTN_BLOCK, w), lambda n: (jnp.minimum(n, nb - 1), 0))
    prev = lambda w: pl.BlockSpec((ATTN_BLOCK, w), lambda n: (jnp.clip(n - 1, 0, nb - 1), 0))
    return pl.pallas_call(
        body, name="attn_core_bwd", grid=(nb + 1,),
        in_specs=[cur(ATTN_Q_W), cur(ATTN_KV_W), prev(ATTN_KV_W), cur(ATTN_KV_W), prev(ATTN_KV_W),
                  cur(ATTN_Q_W), pl.BlockSpec(memory_space=pltpu.SMEM)],
        out_specs=[cur(ATTN_Q_W), prev(ATTN_KV_W), prev(ATTN_KV_W),
                   pl.BlockSpec((ATTN_Q_HEADS, LANES), lambda n: (0, 0))],
        out_shape=[jax.ShapeDtypeStruct((t, ATTN_Q_W), F32),
                   jax.ShapeDtypeStruct((t, ATTN_KV_W), F32),
                   jax.ShapeDtypeStruct((t, ATTN_KV_W), F32),
                   jax.ShapeDtypeStruct((ATTN_Q_HEADS, LANES), F32)],
        scratch_shapes=[pltpu.VMEM((ATTN_BLOCK, ATTN_KV_W), F32),
                        pltpu.VMEM((ATTN_BLOCK, ATTN_KV_W), F32)],
        compiler_params=_params(1),
    )(q, k, k, v, v, do, sinks.reshape(-1))


def _proj_out_fwd(x, w, b, res):
    t = x.shape[0]
    tm = _row_tile(t, 512)

    def body(x_ref, w_ref, b_ref, r_ref, o_ref):
        o_ref[...] = r_ref[...] + _nn(x_ref[...], w_ref[...]) + b_ref[...]

    row = pl.BlockSpec((tm, D_MODEL), lambda i: (i, 0))
    return pl.pallas_call(
        body, name="proj_out_fwd", grid=(t // tm,),
        in_specs=[row, pl.BlockSpec(w.shape, lambda i: (0, 0)), pl.BlockSpec(b.shape, lambda i: (0, 0)), row],
        out_specs=row,
        out_shape=jax.ShapeDtypeStruct((t, D_MODEL), F32),
        compiler_params=_params(1),
    )(x, w, b, res)


def _proj_out_bwd(dy, w):
    t = dy.shape[0]
    tm = _row_tile(t, 512)

    def body(dy_ref, w_ref, dx_ref, dyb_ref, db_ref):
        @pl.when(pl.program_id(0) == 0)
        def _():
            db_ref[...] = jnp.zeros_like(db_ref)

        dy_v = dy_ref[...]
        dyb = dy_v.astype(BF16)
        dyb_ref[...] = dyb
        dx_ref[...] = _nt(dyb, w_ref[...]).astype(BF16)
        db_ref[...] += jnp.sum(dy_v, axis=0, keepdims=True)

    row = pl.BlockSpec((tm, D_MODEL), lambda i: (i, 0))
    return pl.pallas_call(
        body, name="proj_out_bwd", grid=(t // tm,),
        in_specs=[row, pl.BlockSpec(w.shape, lambda i: (0, 0))],
        out_specs=[row, row, pl.BlockSpec((1, D_MODEL), lambda i: (0, 0))],
        out_shape=[jax.ShapeDtypeStruct((t, D_MODEL), BF16),
                   jax.ShapeDtypeStruct((t, D_MODEL), BF16),
                   jax.ShapeDtypeStruct((1, D_MODEL), F32)],
        compiler_params=_params(1),
    )(dy, w)


def _attn_qkv_bwd(dy, h, nw, w_in, dq, dk, dv, rc, rs1, rs2):
    t = h.shape[0]
    tm = _row_tile(t, ATTN_TM)

    def body(dy_ref, h_ref, nw_ref, w_ref, dq_ref, dk_ref, dv_ref, c_ref, s1_ref, s2_ref,
             dh_ref, dqkv_ref, db_ref, dnw_ref, tmp_ref):
        @pl.when(pl.program_id(0) == 0)
        def _():
            db_ref[...] = jnp.zeros_like(db_ref)
            dnw_ref[...] = jnp.zeros_like(dnw_ref)

        c, s1, s2 = c_ref[...], s1_ref[...], s2_ref[...]
        for s in range(ATTN_Q_W // LANES):
            tmp_ref[:, s * LANES:(s + 1) * LANES] = _rope_bwd(dq_ref[:, s * LANES:(s + 1) * LANES], c, s1, s2)
        for s in range(ATTN_KV_W // LANES):
            lo = ATTN_Q_W + s * LANES
            tmp_ref[:, lo:lo + LANES] = _rope_bwd(dk_ref[:, s * LANES:(s + 1) * LANES], c, s1, s2)
        tmp_ref[:, ATTN_Q_W + ATTN_KV_W:] = dv_ref[...]
        dqkv = tmp_ref[...]
        db_ref[...] += jnp.sum(dqkv, axis=0, keepdims=True)
        dqkv_b = dqkv.astype(BF16)
        dqkv_ref[...] = dqkv_b
        dx, dnw = _norm_bwd(h_ref[...], nw_ref[...], _nt(dqkv_b, w_ref[...]))
        dh_ref[...] = dy_ref[...] + dx
        dnw_ref[...] += dnw

    row = lambda w: pl.BlockSpec((tm, w), lambda i: (i, 0))
    full = lambda a: pl.BlockSpec(a.shape, lambda i: (0, 0))
    return pl.pallas_call(
        body, name="attn_qkv_bwd", grid=(t // tm,),
        in_specs=[row(D_MODEL), row(D_MODEL), full(nw), full(w_in), row(ATTN_Q_W), row(ATTN_KV_W),
                  row(ATTN_KV_W), row(LANES), row(LANES), row(LANES)],
        out_specs=[row(D_MODEL), row(ATTN_IN), pl.BlockSpec((1, ATTN_IN), lambda i: (0, 0)),
                   pl.BlockSpec((1, D_MODEL), lambda i: (0, 0))],
        out_shape=[jax.ShapeDtypeStruct((t, D_MODEL), F32),
                   jax.ShapeDtypeStruct((t, ATTN_IN), BF16),
                   jax.ShapeDtypeStruct((1, ATTN_IN), F32),
                   jax.ShapeDtypeStruct((1, D_MODEL), F32)],
        scratch_shapes=[pltpu.VMEM((tm, ATTN_IN), F32)],
        compiler_params=_params(1),
    )(dy, h, nw, w_in, dq, dk, dv, rc, rs1, rs2)


GDN_TM = 512
GDN_CONV_TM = 256


def _gdn_proj_fwd(h, nw, w_qkv, w_z, w_ba):
    t = h.shape[0]
    tm = _row_tile(t, GDN_TM)

    def body(h_ref, nw_ref, wq_ref, wz_ref, wb_ref, x_ref, z_ref, ba_ref, hn_ref):
        _, xhat = _rms_stats(h_ref[...])
        hn = (xhat * nw_ref[...]).astype(BF16)
        hn_ref[...] = hn
        x_ref[...] = _nn(hn, wq_ref[...])
        z_ref[...] = _nn(hn, wz_ref[...])
        ba_ref[...] = _nn(hn, wb_ref[...])

    row = lambda w: pl.BlockSpec((tm, w), lambda i: (i, 0))
    full = lambda a: pl.BlockSpec(a.shape, lambda i: (0, 0))
    return pl.pallas_call(
        body, name="gdn_proj_fwd", grid=(t // tm,),
        in_specs=[row(D_MODEL), full(nw), full(w_qkv), full(w_z), full(w_ba)],
        out_specs=[row(GDN_CONV_W), row(D_MODEL), row(LANES), row(D_MODEL)],
        out_shape=[jax.ShapeDtypeStruct((t, GDN_CONV_W), F32),
                   jax.ShapeDtypeStruct((t, D_MODEL), F32),
                   jax.ShapeDtypeStruct((t, LANES), F32),
                   jax.ShapeDtypeStruct((t, D_MODEL), BF16)],
        compiler_params=_params(1),
    )(h, nw, w_qkv, w_z, w_ba)


def _softplus(x):
    return jnp.maximum(x, 0.0) + jnp.log(1.0 + jnp.exp(-jnp.abs(x)))


def _conv_taps(x, halo, cw):
    tm = x.shape[0]
    xx = jnp.concatenate([halo, x], axis=0)
    taps = [xx[8 - (GDN_CONV - 1) + i: 8 - (GDN_CONV - 1) + i + tm] for i in range(GDN_CONV)]
    c = taps[0] * cw[0:1]
    for i in range(1, GDN_CONV):
        c = c + taps[i] * cw[i:i + 1]
    return c, taps


def _gates(ba, gp):
    lane = lax.broadcasted_iota(jnp.int32, ba.shape, 1)
    beta = _sigmoid(ba)
    pre = ba + gp[1:2]
    g = -jnp.exp(gp[0:1]) * _softplus(pre)
    gates = jnp.where(lane < GDN_HEADS, beta, jnp.where(lane < 2 * GDN_HEADS, g, 0.0))
    return lane, beta, pre, g, gates


def _gdn_conv_fwd(x, cw, ba, gp):
    t = x.shape[0]
    tm = _row_tile(t, GDN_CONV_TM)

    def body(x_ref, halo_ref, cw_ref, ba_ref, gp_ref, q_ref, k_ref, v_ref, gates_ref):
        halo = jnp.where(pl.program_id(0) > 0, halo_ref[...], 0.0)
        c, _ = _conv_taps(x_ref[...], halo, cw_ref[...])
        s = c * _sigmoid(c)
        for hh in range(GDN_HEADS):
            sq = s[:, hh * LANES:(hh + 1) * LANES]
            q_ref[:, hh * LANES:(hh + 1) * LANES] = (
                sq * lax.rsqrt(jnp.sum(sq * sq, axis=-1, keepdims=True) + NORM_EPS) * GDN_QSCALE)
            sk = s[:, D_MODEL + hh * LANES:D_MODEL + (hh + 1) * LANES]
            k_ref[:, hh * LANES:(hh + 1) * LANES] = (
                sk * lax.rsqrt(jnp.sum(sk * sk, axis=-1, keepdims=True) + NORM_EPS))
        v_ref[...] = s[:, 2 * D_MODEL:]
        gates_ref[...] = _gates(ba_ref[...], gp_ref[...])[4]

    row = lambda w: pl.BlockSpec((tm, w), lambda i: (i, 0))
    full = lambda a: pl.BlockSpec(a.shape, lambda i: (0, 0))
    halo = pl.BlockSpec((8, GDN_CONV_W), lambda i: (jnp.maximum(i * (tm // 8) - 1, 0), 0))
    return pl.pallas_call(
        body, name="gdn_conv_fwd", grid=(t // tm,),
        in_specs=[row(GDN_CONV_W), halo, full(cw), row(LANES), full(gp)],
        out_specs=[row(D_MODEL), row(D_MODEL), row(D_MODEL), row(LANES)],
        out_shape=[jax.ShapeDtypeStruct((t, D_MODEL), F32)] * 3 + [jax.ShapeDtypeStruct((t, LANES), F32)],
        compiler_params=_params(1),
    )(x, x, cw, ba, gp)


def _split3(x):
    hi = x.astype(BF16).astype(F32)
    r1 = x - hi
    mid = r1.astype(BF16).astype(F32)
    lo = (r1 - mid).astype(BF16).astype(F32)
    return hi, mid, lo


def _chunk_cumsum(col, keep):
    nb, c, _ = col.shape
    l3 = lax.broadcasted_iota(jnp.int32, (nb, c, LANES), 2)
    hi, mid, lo = _split3(col)
    pieces = jnp.where(l3 == 0, hi, jnp.where(l3 == 1, mid, jnp.where(l3 == 2, lo, 0.0))).astype(BF16)
    s = _bnn(jnp.where(keep, 1.0, 0.0).astype(BF16), pieces)
    return s[..., 0:1] + s[..., 1:2] + s[..., 2:3]


def _unit_inverse(nmat):
    nb, c, _ = nmat.shape
    ri = lax.broadcasted_iota(jnp.int32, (nb, c, c), 1)
    ci = lax.broadcasted_iota(jnp.int32, (nb, c, c), 2)
    eye = jnp.where(ri == ci, 1.0, 0.0).astype(F32)
    same = (ri >> GDN_SUB_SHIFT) == (ci >> GDN_SUB_SHIFT)
    nd = jnp.where(same, nmat, 0.0)
    no = nmat - nd
    mm = lambda a, b: _bnn(a.astype(BF16), b.astype(BF16))
    n2 = mm(nd, nd)
    n4 = mm(n2, n2)
    n8 = mm(n4, n4)
    td = mm(mm(mm(eye - nd, eye + n2), eye + n4), eye + n8)
    bm = mm(td, no)
    b2 = mm(bm, bm)
    return mm(mm(eye - bm, eye + b2), td)


class _LaneWindow:
    def __init__(self, ref, j):
        self.ref, self.lanes, self.shape = ref, slice(j * LANES, (j + 1) * LANES), (ref.shape[0], LANES)

    def __getitem__(self, idx):
        return self.ref[:, self.lanes]

    def __setitem__(self, idx, val):
        self.ref[:, self.lanes] = val


def _chunk_terms(q_ref, k_ref, gates_ref, h, transposed):
    rows = k_ref.shape[0]
    nb = rows // GDN_CHUNK
    c = GDN_CHUNK
    lane = lax.broadcasted_iota(jnp.int32, (rows, LANES), 1)
    gt = gates_ref[...]
    beta = jnp.sum(jnp.where(lane == h, gt, 0.0), axis=-1, keepdims=True).reshape(nb, c, 1)
    g = jnp.sum(jnp.where(lane == GDN_HEADS + h, gt, 0.0), axis=-1, keepdims=True).reshape(nb, c, 1)
    ri = lax.broadcasted_iota(jnp.int32, (nb, c, c), 1)
    ci = lax.broadcasted_iota(jnp.int32, (nb, c, c), 2)
    gcol = _chunk_cumsum(g, ri >= ci)
    gamma = jnp.broadcast_to(gcol, (nb, c, LANES))
    l3 = lax.broadcasted_iota(jnp.int32, (nb, c, LANES), 2)
    gh, gm, gl = _split3(gcol)
    pmat = jnp.where(l3 == 0, gh, jnp.where(l3 == 1, gm, jnp.where(l3 == 2, gl, jnp.where(l3 < 6, 1.0, 0.0))))
    qmat = jnp.where(l3 < 3, 1.0, jnp.where(l3 == 3, -gh, jnp.where(l3 == 4, -gm, jnp.where(l3 == 5, -gl, 0.0))))
    pmat, qmat = pmat.astype(BF16), qmat.astype(BF16)
    k = k_ref[...].reshape(nb, c, LANES)
    q = q_ref[...].reshape(nb, c, LANES)
    kb = k * beta
    kbf, kbb, qb = k.astype(BF16), kb.astype(BF16), q.astype(BF16)
    out = dict(beta=beta, g=g, gamma=gamma, k=k, q=q, kb=kb, kbf=kbf, kbb=kbb, qb=qb, ri=ri, ci=ci)
    diff = _bnt(pmat, qmat)
    lmat = jnp.exp(jnp.where(ri >= ci, diff, NEG_BIG))
    out["L"] = lmat
    out["A"] = jnp.where(ri > ci, _bnt(kbb, kbf) * lmat, 0.0)
    out["Aqk"] = jnp.where(ri >= ci, _bnt(qb, kbf) * lmat, 0.0)
    if transposed:
        difft = _bnt(qmat, pmat)
        lt = jnp.exp(jnp.where(ci >= ri, difft, NEG_BIG))
        out["LT"] = lt
        out["AT"] = jnp.where(ci > ri, _bnt(kbf, kbb) * lt, 0.0)
        out["AqkT"] = jnp.where(ci >= ri, _bnt(kbf, qb) * lt, 0.0)
    return out


def _gdn_intra_fwd(q, k, v, gates):
    t = q.shape[0]
    rows = _row_tile(t, GDN_ROWS)
    nb = rows // GDN_CHUNK
    nchunks = t // GDN_CHUNK
    hs = GDN_INTRA_HEADS

    def head(h, hh, q_ref, k_ref, v_ref, gates_ref, u_ref, w_ref, qd_ref, kd_ref, aqk_ref, cd_ref):
        tm_ = _chunk_terms(q_ref, k_ref, gates_ref, h, False)
        gamma, beta = tm_["gamma"], tm_["beta"]
        eg = jnp.exp(gamma)
        tinv = _unit_inverse(tm_["A"])
        v3 = v_ref[...].reshape(nb, GDN_CHUNK, LANES)
        rhs = jnp.concatenate([v3 * beta, tm_["kb"] * eg], axis=-1)
        sol = _bnn(tinv.astype(BF16), rhs.astype(BF16))
        u_ref[...] = sol[..., :LANES].reshape(rows, LANES)
        w_ref[...] = sol[..., LANES:].reshape(rows, LANES).astype(BF16)
        gl = gamma[:, GDN_CHUNK - 1:GDN_CHUNK, :]
        qd_ref[...] = (tm_["q"] * eg).reshape(rows, LANES).astype(BF16)
        kd_ref[...] = (tm_["k"] * jnp.exp(gl - gamma)).reshape(rows, LANES).astype(BF16)
        aqk_ref[hh] = tm_["Aqk"].reshape(rows, GDN_CHUNK).astype(BF16)
        cd_ref[hh] = jnp.broadcast_to(jnp.exp(gl), (nb, 8, LANES)).reshape(nb * 8, LANES)

    def body(q_ref, k_ref, v_ref, gates_ref, u_ref, w_ref, qd_ref, kd_ref, aqk_ref, cd_ref):
        for hh in range(hs):
            win = lambda ref: _LaneWindow(ref, hh)
            head(pl.program_id(1) * hs + hh, hh, win(q_ref), win(k_ref), win(v_ref), gates_ref,
                 win(u_ref), win(w_ref), win(qd_ref), win(kd_ref), aqk_ref, cd_ref)

    blk = pl.BlockSpec((rows, hs * LANES), lambda i, h: (i, h))
    return pl.pallas_call(
        body, name="gdn_intra_fwd", grid=(t // rows, GDN_HEADS // hs),
        in_specs=[blk, blk, blk, pl.BlockSpec((rows, LANES), lambda i, h: (i, 0))],
        out_specs=[blk, blk, blk, blk,
                   pl.BlockSpec((hs, rows, GDN_CHUNK), lambda i, h: (h, i, 0)),
                   pl.BlockSpec((hs, nb * 8, LANES), lambda i, h: (h, i, 0))],
        out_shape=[jax.ShapeDtypeStruct((t, D_MODEL), F32),
                   jax.ShapeDtypeStruct((t, D_MODEL), BF16),
                   jax.ShapeDtypeStruct((t, D_MODEL), BF16),
                   jax.ShapeDtypeStruct((t, D_MODEL), BF16),
                   jax.ShapeDtypeStruct((GDN_HEADS, t, GDN_CHUNK), BF16),
                   jax.ShapeDtypeStruct((GDN_HEADS, nchunks * 8, LANES), F32)],
        compiler_params=_params(2),
    )(q, k, v, gates)


def _gdn_scan_fwd(u, w, qd, kd, aqk, cd):
    t = u.shape[0]
    rows = _row_tile(t, GDN_SCAN_ROWS)
    nb = rows // GDN_CHUNK
    nchunks = t // GDN_CHUNK
    hs = GDN_SCAN_HEADS

    def body(u_ref, w_ref, qd_ref, kd_ref, aqk_ref, cd_ref, o_ref, st_ref, s_ref):
        @pl.when(pl.program_id(1) == 0)
        def _():
            s_ref[...] = jnp.zeros_like(s_ref)

        states = [s_ref[hh] for hh in range(hs)]
        for c in range(nb):
            r = slice(c * GDN_CHUNK, (c + 1) * GDN_CHUNK)
            for hh in range(hs):
                ln = slice(hh * LANES, (hh + 1) * LANES)
                s = states[hh]
                st_ref[hh, c * LANES:(c + 1) * LANES, :] = s
                sb = s.astype(BF16)
                vb = (u_ref[r, ln] - _nn(w_ref[r, ln], sb)).astype(BF16)
                o_ref[r, ln] = _nn(qd_ref[r, ln], sb) + _nn(aqk_ref[hh, r, :], vb)
                states[hh] = s * cd_ref[hh, c * 8:c * 8 + 1, :] + _tn(kd_ref[r, ln], vb)
        for hh in range(hs):
            s_ref[hh] = states[hh]

    blk = pl.BlockSpec((rows, hs * LANES), lambda h, i: (i, h))
    return pl.pallas_call(
        body, name="gdn_scan_fwd", grid=(GDN_HEADS // hs, t // rows),
        in_specs=[blk, blk, blk, blk,
                  pl.BlockSpec((hs, rows, GDN_CHUNK), lambda h, i: (h, i, 0)),
                  pl.BlockSpec((hs, nb * 8, LANES), lambda h, i: (h, i, 0))],
        out_specs=[blk, pl.BlockSpec((hs, nb * LANES, LANES), lambda h, i: (h, i, 0))],
        out_shape=[jax.ShapeDtypeStruct((t, D_MODEL), F32),
                   jax.ShapeDtypeStruct((GDN_HEADS, nchunks * LANES, LANES), F32)],
        scratch_shapes=[pltpu.VMEM((hs, LANES, LANES), F32)],
        compiler_params=_params(2),
    )(u, w, qd, kd, aqk, cd)


def _gdn_out_fwd(o, z, nw, w_out, res):
    t = o.shape[0]
    tm = _row_tile(t, GDN_TM)

    def body(o_ref, z_ref, nw_ref, w_ref, r_ref, out_ref, gated_ref):
        nwv = nw_ref[...]
        for hh in range(GDN_HEADS):
            sl = slice(hh * LANES, (hh + 1) * LANES)
            _, on = _rms_stats(o_ref[:, sl])
            zv = z_ref[:, sl]
            gated_ref[:, sl] = (on * nwv * (zv * _sigmoid(zv))).astype(BF16)
        out_ref[...] = r_ref[...] + _nn(gated_ref[...], w_ref[...])

    row = pl.BlockSpec((tm, D_MODEL), lambda i: (i, 0))
    full = lambda a: pl.BlockSpec(a.shape, lambda i: (0, 0))
    return pl.pallas_call(
        body, name="gdn_out_fwd", grid=(t // tm,),
        in_specs=[row, row, full(nw), full(w_out), row],
        out_specs=[row, row],
        out_shape=[jax.ShapeDtypeStruct((t, D_MODEL), F32), jax.ShapeDtypeStruct((t, D_MODEL), BF16)],
        compiler_params=_params(1),
    )(o, z, nw, w_out, res)


def _gdn_out_bwd(dy, w_out, o, z, nw):
    t = o.shape[0]
    tm = _row_tile(t, GDN_TM)

    def body(dy_ref, w_ref, o_ref, z_ref, nw_ref, do_ref, dz_ref, dyb_ref, dnw_ref, dgt_ref):
        @pl.when(pl.program_id(0) == 0)
        def _():
            dnw_ref[...] = jnp.zeros_like(dnw_ref)

        dyb = dy_ref[...].astype(BF16)
        dyb_ref[...] = dyb
        dgt_ref[...] = _nt(dyb, w_ref[...])
        nwv = nw_ref[...]
        for hh in range(GDN_HEADS):
            sl = slice(hh * LANES, (hh + 1) * LANES)
            r, on = _rms_stats(o_ref[:, sl])
            zv = z_ref[:, sl]
            sig = _sigmoid(zv)
            sz = zv * sig
            dgt = dgt_ref[:, sl]
            d_on = dgt * nwv * sz
            dz_ref[:, sl] = (dgt * on * nwv * (sig * (1.0 + zv * (1.0 - sig)))).astype(BF16)
            dnw_ref[...] += jnp.sum(dgt * on * sz, axis=0, keepdims=True)
            do_ref[:, sl] = (r * (d_on - on * jnp.mean(d_on * on, axis=-1, keepdims=True))).astype(BF16)

    row = pl.BlockSpec((tm, D_MODEL), lambda i: (i, 0))
    full = lambda a: pl.BlockSpec(a.shape, lambda i: (0, 0))
    return pl.pallas_call(
        body, name="gdn_out_bwd", grid=(t // tm,),
        in_specs=[row, full(w_out), row, row, full(nw)],
        out_specs=[row, row, row, pl.BlockSpec((1, LANES), lambda i: (0, 0))],
        out_shape=[jax.ShapeDtypeStruct((t, D_MODEL), BF16)] * 3 + [jax.ShapeDtypeStruct((1, LANES), F32)],
        scratch_shapes=[pltpu.VMEM((tm, D_MODEL), F32)],
        compiler_params=_params(1),
    )(dy, w_out, o, z, nw)


def _gdn_scan_bwd(u, w, qd, kd, aqk, cd, st, do):
    t = u.shape[0]
    rows = _row_tile(t, GDN_SCAN_ROWS)
    nb = rows // GDN_CHUNK
    nchunks = t // GDN_CHUNK
    nsteps = t // rows
    cc = GDN_CHUNK
    hs = GDN_SCAN_HEADS

    def body(u_ref, w_ref, qd_ref, kd_ref, aqk_ref, cd_ref, st_ref, do_ref,
             du_ref, dw_ref, dqd_ref, dkd_ref, daqk_ref, daqkt_ref, dcd_ref, ds_ref):
        @pl.when(pl.program_id(1) == 0)
        def _():
            ds_ref[...] = jnp.zeros_like(ds_ref)

        ri = lax.broadcasted_iota(jnp.int32, (cc, cc), 0)
        ci = lax.broadcasted_iota(jnp.int32, (cc, cc), 1)
        dstates = [ds_ref[hh] for hh in range(hs)]
        for c in reversed(range(nb)):
            r = slice(c * cc, (c + 1) * cc)
            for hh in range(hs):
                ln = slice(hh * LANES, (hh + 1) * LANES)
                s = st_ref[hh, c * LANES:(c + 1) * LANES, :]
                sb = s.astype(BF16)
                dsn = dstates[hh]
                dsb = dsn.astype(BF16)
                wv, kdv, qdv, aq, dov = w_ref[r, ln], kd_ref[r, ln], qd_ref[r, ln], aqk_ref[hh, r, :], do_ref[r, ln]
                vb = (u_ref[r, ln] - _nn(wv, sb)).astype(BF16)
                dv = _tn(aq, dov) + _nn(kdv, dsb)
                dvb = dv.astype(BF16)
                daqk_ref[hh, r, :] = jnp.where(ri >= ci, _nt(dov, vb), 0.0)
                daqkt_ref[hh, r, :] = jnp.where(ci >= ri, _nt(vb, dov), 0.0)
                dqd_ref[r, ln] = _nt(dov, sb)
                dkd_ref[r, ln] = _nt(vb, dsb)
                dcd_ref[hh, c * 8:(c + 1) * 8, :] = jnp.broadcast_to(jnp.sum(s * dsn), (8, LANES))
                du_ref[r, ln] = dv
                dw_ref[r, ln] = -_nt(dvb, sb)
                dstates[hh] = _tn(qdv, dov) + dsn * cd_ref[hh, c * 8:c * 8 + 1, :] - _tn(wv, dvb)
        for hh in range(hs):
            ds_ref[hh] = dstates[hh]

    rev = lambda i: nsteps - 1 - i
    blk = pl.BlockSpec((rows, hs * LANES), lambda h, i: (rev(i), h))
    sq = pl.BlockSpec((hs, rows, cc), lambda h, i: (h, rev(i), 0))
    cdb = pl.BlockSpec((hs, nb * 8, LANES), lambda h, i: (h, rev(i), 0))
    return pl.pallas_call(
        body, name="gdn_scan_bwd", grid=(GDN_HEADS // hs, nsteps),
        in_specs=[blk, blk, blk, blk, sq, cdb,
                  pl.BlockSpec((hs, nb * LANES, LANES), lambda h, i: (h, rev(i), 0)), blk],
        out_specs=[blk, blk, blk, blk, sq, sq, cdb],
        out_shape=[jax.ShapeDtypeStruct((t, D_MODEL), F32)] * 4
        + [jax.ShapeDtypeStruct((GDN_HEADS, t, cc), F32)] * 2
        + [jax.ShapeDtypeStruct((GDN_HEADS, nchunks * 8, LANES), F32)],
        scratch_shapes=[pltpu.VMEM((hs, LANES, LANES), F32)],
        compiler_params=_params(2),
    )(u, w, qd, kd, aqk, cd, st, do)


def _gdn_intra_bwd(q, k, v, gates, u, w, du, dw, dqd, dkd, daqk, daqkt, dcd):
    t = q.shape[0]
    rows = _row_tile(t, GDN_ROWS)
    nb = rows // GDN_CHUNK
    cc = GDN_CHUNK
    hs = GDN_INTRA_HEADS

    def head(h, hh, q_ref, k_ref, v_ref, gates_ref, u_ref, w_ref, du_ref, dw_ref, dqd_ref, dkd_ref,
             daqk_ref, daqkt_ref, dcd_ref, dq_ref, dk_ref, dv_ref, dgates_ref):
        tm_ = _chunk_terms(q_ref, k_ref, gates_ref, h, True)
        gamma, beta, kk, qq, kb = tm_["gamma"], tm_["beta"], tm_["k"], tm_["q"], tm_["kb"]
        kbf, kbb, qb = tm_["kbf"], tm_["kbb"], tm_["qb"]
        lmat, lt = tm_["L"], tm_["LT"]
        ri, ci = tm_["ri"], tm_["ci"]
        r3 = lambda ref: ref[...].reshape(nb, cc, LANES)
        eg = jnp.exp(gamma)
        gl = gamma[:, cc - 1:cc, :]
        ekd = jnp.exp(gl - gamma)
        v3 = r3(v_ref)
        tt = _unit_inverse(tm_["AT"])
        dsol = jnp.concatenate([r3(du_ref), r3(dw_ref)], axis=-1)
        sol = jnp.concatenate([r3(u_ref), r3(w_ref).astype(F32)], axis=-1)
        dx = _bnn(tt.astype(BF16), dsol.astype(BF16))
        dxb, solb = dx.astype(BF16), sol.astype(BF16)
        da = jnp.where(ri > ci, -_bnt(dxb, solb), 0.0)
        dat = jnp.where(ci > ri, -_bnt(solb, dxb), 0.0)
        dxu, dxw = dx[..., :LANES], dx[..., LANES:]
        dv_ref[...] = (dxu * beta).reshape(rows, LANES)
        dbeta = jnp.sum(dxu * v3, axis=-1, keepdims=True)
        dkb = dxw * eg
        dgam = jnp.sum(dxw * kb * eg, axis=-1, keepdims=True)
        dkb = dkb + _bnn((da * lmat).astype(BF16), kbf)
        dk = _bnn((dat * lt).astype(BF16), kbb)
        dgam = dgam + jnp.sum(da * tm_["A"], axis=-1, keepdims=True) - jnp.sum(dat * tm_["AT"], axis=-1, keepdims=True)
        daq = daqk_ref[hh].reshape(nb, cc, cc)
        daqt = daqkt_ref[hh].reshape(nb, cc, cc)
        dq = _bnn((daq * lmat).astype(BF16), kbf)
        dk = dk + _bnn((daqt * lt).astype(BF16), qb)
        dgam = dgam + jnp.sum(daq * tm_["Aqk"], axis=-1, keepdims=True) - jnp.sum(daqt * tm_["AqkT"], axis=-1, keepdims=True)
        dqd3, dkd3 = r3(dqd_ref), r3(dkd_ref)
        dq = dq + dqd3 * eg
        dgam = dgam + jnp.sum(dqd3 * qq * eg, axis=-1, keepdims=True)
        dk = dk + dkd3 * ekd
        tk = jnp.sum(dkd3 * kk * ekd, axis=-1, keepdims=True)
        dgam = dgam - tk
        dcdv = dcd_ref[hh].reshape(nb, 8, LANES)[:, 0:1, 0:1]
        dglast = jnp.sum(tk, axis=1, keepdims=True) + dcdv * jnp.exp(gl[:, :, 0:1])
        rowi = lax.broadcasted_iota(jnp.int32, (nb, cc, 1), 1)
        dgam = dgam + jnp.where(rowi == cc - 1, dglast, 0.0)
        dk = dk + dkb * beta
        dbeta = dbeta + jnp.sum(dkb * kk, axis=-1, keepdims=True)
        dg = _chunk_cumsum(dgam, ci >= ri)
        dq_ref[...] = dq.reshape(rows, LANES)
        dk_ref[...] = dk.reshape(rows, LANES)
        lane = lax.broadcasted_iota(jnp.int32, (rows, LANES), 1)
        dgates_ref[...] += (jnp.where(lane == h, dbeta.reshape(rows, 1), 0.0)
                            + jnp.where(lane == GDN_HEADS + h, dg.reshape(rows, 1), 0.0))

    def body(q_ref, k_ref, v_ref, gates_ref, u_ref, w_ref, du_ref, dw_ref, dqd_ref, dkd_ref,
             daqk_ref, daqkt_ref, dcd_ref, dq_ref, dk_ref, dv_ref, dgates_ref):
        @pl.when(pl.program_id(1) == 0)
        def _():
            dgates_ref[...] = jnp.zeros_like(dgates_ref)

        for hh in range(hs):
            win = lambda ref: _LaneWindow(ref, hh)
            head(pl.program_id(1) * hs + hh, hh, win(q_ref), win(k_ref), win(v_ref), gates_ref, win(u_ref),
                 win(w_ref), win(du_ref), win(dw_ref), win(dqd_ref), win(dkd_ref), daqk_ref, daqkt_ref, dcd_ref,
                 win(dq_ref), win(dk_ref), win(dv_ref), dgates_ref)

    blk = pl.BlockSpec((rows, hs * LANES), lambda i, h: (i, h))
    shared = pl.BlockSpec((rows, LANES), lambda i, h: (i, 0))
    sq = pl.BlockSpec((hs, rows, cc), lambda i, h: (h, i, 0))
    return pl.pallas_call(
        body, name="gdn_intra_bwd", grid=(t // rows, GDN_HEADS // hs),
        in_specs=[blk, blk, blk, shared, blk, blk, blk, blk, blk, blk, sq, sq,
                  pl.BlockSpec((hs, nb * 8, LANES), lambda i, h: (h, i, 0))],
        out_specs=[blk, blk, blk, shared],
        out_shape=[jax.ShapeDtypeStruct((t, D_MODEL), F32)] * 3 + [jax.ShapeDtypeStruct((t, LANES), F32)],
        compiler_params=_params(2),
    )(q, k, v, gates, u, w, du, dw, dqd, dkd, daqk, daqkt, dcd)


def _gdn_conv_bwd_a(x, cw, ba, gp, dq, dk, dv, dgates):
    t = x.shape[0]
    tm = _row_tile(t, GDN_CONV_TM)

    def body(x_ref, halo_ref, cw_ref, ba_ref, gp_ref, dq_ref, dk_ref, dv_ref, dgates_ref,
             dc_ref, dba_ref, dcw_ref, dgp_ref, ds_ref):
        @pl.when(pl.program_id(0) == 0)
        def _():
            dcw_ref[...] = jnp.zeros_like(dcw_ref)
            dgp_ref[...] = jnp.zeros_like(dgp_ref)

        halo = jnp.where(pl.program_id(0) > 0, halo_ref[...], 0.0)
        c, taps = _conv_taps(x_ref[...], halo, cw_ref[...])
        sig = _sigmoid(c)
        s = c * sig
        for hh in range(GDN_HEADS):
            sl = slice(hh * LANES, (hh + 1) * LANES)
            sq = s[:, sl]
            rq = lax.rsqrt(jnp.sum(sq * sq, axis=-1, keepdims=True) + NORM_EPS)
            qh = sq * rq
            dqv = dq_ref[:, sl]
            ds_ref[:, sl] = GDN_QSCALE * rq * (dqv - qh * jnp.sum(dqv * qh, axis=-1, keepdims=True))
            sl2 = slice(D_MODEL + hh * LANES, D_MODEL + (hh + 1) * LANES)
            sk = s[:, sl2]
            rk = lax.rsqrt(jnp.sum(sk * sk, axis=-1, keepdims=True) + NORM_EPS)
            kh = sk * rk
            dkv = dk_ref[:, sl]
            ds_ref[:, sl2] = rk * (dkv - kh * jnp.sum(dkv * kh, axis=-1, keepdims=True))
        ds_ref[:, 2 * D_MODEL:] = dv_ref[...]
        dc = ds_ref[...] * (sig * (1.0 + c * (1.0 - sig)))
        dc_ref[...] = dc
        for i in range(GDN_CONV):
            dcw_ref[i:i + 1, :] += jnp.sum(dc * taps[i], axis=0, keepdims=True)
        lane, beta, pre, g, _ = _gates(ba_ref[...], gp_ref[...])
        dgt = dgates_ref[...]
        db = dgt * beta * (1.0 - beta)
        dpre = dgt * (-jnp.exp(gp_ref[0:1, :])) * _sigmoid(pre)
        isa = jnp.logical_and(lane >= GDN_HEADS, lane < 2 * GDN_HEADS)
        dba_ref[...] = jnp.where(lane < GDN_HEADS, db, jnp.where(isa, dpre, 0.0)).astype(BF16)
        dgp_ref[0:1, :] += jnp.sum(jnp.where(isa, dgt * g, 0.0), axis=0, keepdims=True)
        dgp_ref[1:2, :] += jnp.sum(jnp.where(isa, dpre, 0.0), axis=0, keepdims=True)

    row = lambda w: pl.BlockSpec((tm, w), lambda i: (i, 0))
    full = lambda a: pl.BlockSpec(a.shape, lambda i: (0, 0))
    halo = pl.BlockSpec((8, GDN_CONV_W), lambda i: (jnp.maximum(i * (tm // 8) - 1, 0), 0))
    return pl.pallas_call(
        body, name="gdn_conv_bwd_a", grid=(t // tm,),
        in_specs=[row(GDN_CONV_W), halo, full(cw), row(LANES), full(gp), row(D_MODEL), row(D_MODEL),
                  row(D_MODEL), row(LANES)],
        out_specs=[row(GDN_CONV_W), row(LANES), pl.BlockSpec((8, GDN_CONV_W), lambda i: (0, 0)),
                   pl.BlockSpec((8, LANES), lambda i: (0, 0))],
        out_shape=[jax.ShapeDtypeStruct((t, GDN_CONV_W), F32),
                   jax.ShapeDtypeStruct((t, LANES), BF16),
                   jax.ShapeDtypeStruct((8, GDN_CONV_W), F32),
                   jax.ShapeDtypeStruct((8, LANES), F32)],
        scratch_shapes=[pltpu.VMEM((tm, GDN_CONV_W), F32)],
        compiler_params=_params(1),
    )(x, x, cw, ba, gp, dq, dk, dv, dgates)


def _gdn_conv_bwd_b(dc, cw):
    t = dc.shape[0]
    tm = _row_tile(t, GDN_CONV_TM)
    nsteps = t // tm

    def body(dc_ref, halo_ref, cw_ref, dx_ref):
        halo = jnp.where(pl.program_id(0) < nsteps - 1, halo_ref[...], 0.0)
        dd = jnp.concatenate([dc_ref[...], halo], axis=0)
        cw_v = cw_ref[...]
        acc = dd[GDN_CONV - 1:GDN_CONV - 1 + tm] * cw_v[0:1]
        for i in range(1, GDN_CONV):
            acc = acc + dd[GDN_CONV - 1 - i:GDN_CONV - 1 - i + tm] * cw_v[i:i + 1]
        dx_ref[...] = acc.astype(BF16)

    row = pl.BlockSpec((tm, GDN_CONV_W), lambda i: (i, 0))
    halo = pl.BlockSpec((8, GDN_CONV_W), lambda i: (jnp.minimum((i + 1) * (tm // 8), t // 8 - 1), 0))
    return pl.pallas_call(
        body, name="gdn_conv_bwd_b", grid=(nsteps,),
        in_specs=[row, halo, pl.BlockSpec(cw.shape, lambda i: (0, 0))],
        out_specs=row,
        out_shape=jax.ShapeDtypeStruct((t, GDN_CONV_W), BF16),
        compiler_params=_params(1),
    )(dc, dc, cw)


def _gdn_proj_bwd(dy, h, nw, w_qkv, w_z, w_ba, dx, dz, dba):
    t = h.shape[0]
    tm = _row_tile(t, GDN_TM)

    def body(dy_ref, h_ref, nw_ref, wq_ref, wz_ref, wb_ref, dx_ref, dz_ref, dba_ref, dh_ref, dnw_ref):
        @pl.when(pl.program_id(0) == 0)
        def _():
            dnw_ref[...] = jnp.zeros_like(dnw_ref)

        dhn = _nt(dx_ref[...], wq_ref[...]) + _nt(dz_ref[...], wz_ref[...]) + _nt(dba_ref[...], wb_ref[...])
        dxx, dnw = _norm_bwd(h_ref[...], nw_ref[...], dhn)
        dh_ref[...] = dy_ref[...] + dxx
        dnw_ref[...] += dnw

    row = lambda w: pl.BlockSpec((tm, w), lambda i: (i, 0))
    full = lambda a: pl.BlockSpec(a.shape, lambda i: (0, 0))
    return pl.pallas_call(
        body, name="gdn_proj_bwd", grid=(t // tm,),
        in_specs=[row(D_MODEL), row(D_MODEL), full(nw), full(w_qkv), full(w_z), full(w_ba),
                  row(GDN_CONV_W), row(D_MODEL), row(LANES)],
        out_specs=[row(D_MODEL), pl.BlockSpec((1, D_MODEL), lambda i: (0, 0))],
        out_shape=[jax.ShapeDtypeStruct((t, D_MODEL), F32), jax.ShapeDtypeStruct((1, D_MODEL), F32)],
        compiler_params=_params(1),
    )(dy, h, nw, w_qkv, w_z, w_ba, dx, dz, dba)


def _loss_head(h, nw, target):
    t = h.shape[0]
    tm = _row_tile(t, 512)

    def body(h_ref, nw_ref, t_ref, loss_ref, dh_ref, dnw_ref):
        @pl.when(pl.program_id(0) == 0)
        def _():
            loss_ref[...] = jnp.zeros_like(loss_ref)
            dnw_ref[...] = jnp.zeros_like(dnw_ref)

        x = h_ref[...]
        nwv = nw_ref[...]
        _, xhat = _rms_stats(x)
        err = xhat * nwv - t_ref[...]
        loss_ref[...] += 0.5 * jnp.sum(jnp.mean(err * err, axis=-1, keepdims=True))
        dx, dnw = _norm_bwd(x, nwv, err * (1.0 / D_MODEL))
        dh_ref[...] = dx
        dnw_ref[...] += dnw

    row = pl.BlockSpec((tm, D_MODEL), lambda i: (i, 0))
    return pl.pallas_call(
        body, name="loss_head", grid=(t // tm,),
        in_specs=[row, pl.BlockSpec((1, D_MODEL), lambda i: (0, 0)), row],
        out_specs=[pl.BlockSpec((8, LANES), lambda i: (0, 0)), row, pl.BlockSpec((1, D_MODEL), lambda i: (0, 0))],
        out_shape=[jax.ShapeDtypeStruct((8, LANES), F32),
                   jax.ShapeDtypeStruct((t, D_MODEL), F32),
                   jax.ShapeDtypeStruct((1, D_MODEL), F32)],
        compiler_params=_params(1),
    )(h, nw, target)


_PEER_FLIPS = [(dx, dy, dc) for dx in (0, 1) for dy in (0, 1) for dc in (0, 1)][1:]


_ANY = pl.BlockSpec(memory_space=pl.ANY)


def _exchange_copies(ins, outs, send_sems, recv_sems, local_sems, scatter):
    x, y, c = lax.axis_index("x"), lax.axis_index("y"), lax.axis_index("c")
    me = 4 * x + 2 * y + c
    copies = []
    for a in range(len(ins)):
        src = ins[a].at[me] if scatter else ins[a]
        copies.append(pltpu.make_async_copy(src, outs[a].at[me], local_sems.at[a]))
    for k, (fx, fy, fc) in enumerate(_PEER_FLIPS):
        px, py, pc = lax.rem(x + fx, 2), lax.rem(y + fy, 2), lax.rem(c + fc, 2)
        peer = 4 * px + 2 * py + pc
        for a in range(len(ins)):
            copies.append(pltpu.make_async_remote_copy(
                src_ref=ins[a].at[peer] if scatter else ins[a],
                dst_ref=outs[a].at[me],
                send_sem=send_sems.at[a, k], recv_sem=recv_sems.at[a, k],
                device_id=(px, py, pc), device_id_type=MESH_ID))
    return copies


def _exchange_shapes(arrs, scatter):
    return [jax.ShapeDtypeStruct((N_DEV,) + (a.shape[1:] if scatter else a.shape), a.dtype) for a in arrs]


def _exchange_sems(n):
    npeer = len(_PEER_FLIPS)
    return [pltpu.SemaphoreType.DMA((n, npeer)), pltpu.SemaphoreType.DMA((n, npeer)),
            pltpu.SemaphoreType.DMA((n,))]


def _exchange(arrs, scatter, name):
    n = len(arrs)

    def body(*refs):
        copies = _exchange_copies(refs[:n], refs[n:2 * n], *refs[2 * n:], scatter)
        for cp in copies:
            cp.start()
        for cp in copies:
            cp.wait()

    return pl.pallas_call(
        body, name=name, in_specs=[_ANY] * n, out_specs=[_ANY] * n,
        out_shape=_exchange_shapes(arrs, scatter), scratch_shapes=_exchange_sems(n),
    )(*arrs)


ADAMW_PART_BLOCK_BYTES = 2 * 1024 * 1024


def _adamw(parts, w, m, v, name):
    nl, r, c = w.shape
    cap = max(8, ADAMW_PART_BLOCK_BYTES // (N_DEV * c * parts[0].dtype.itemsize))
    tr = r
    for cand in range(min(r, cap) // 8 * 8, 7, -8):
        if r % cand == 0:
            tr = cand
            break
    nr = r // tr
    c1 = 1.0 - ADAM_B1 ** ADAM_STEP
    c2 = 1.0 - ADAM_B2 ** ADAM_STEP

    def body(*refs):
        p_refs = refs[:nl]
        w_ref, m_ref, v_ref, g_ref, d_ref, nm_ref, nv_ref = refs[nl:]
        layer = pl.program_id(0)
        for k in range(nl):
            @pl.when(layer == k)
            def _():
                g = p_refs[k][0].astype(F32)
                for s in range(1, N_DEV):
                    g = g + p_refs[k][s].astype(F32)
                nm = ADAM_B1 * m_ref[0] + (1.0 - ADAM_B1) * g
                nv = ADAM_B2 * v_ref[0] + (1.0 - ADAM_B2) * (g * g)
                g_ref[0] = g
                nm_ref[0] = nm
                nv_ref[0] = nv
                d_ref[0] = -ADAM_LR * ((nm / c1) / (jnp.sqrt(nv / c2) + ADAM_EPS) + ADAM_WD * w_ref[0])

    def part_spec(k):
        return pl.BlockSpec((N_DEV, tr, c),
                            lambda l, i: (0, jnp.where(l == k, i, jnp.where(l < k, 0, nr - 1)), 0))

    blk = pl.BlockSpec((1, tr, c), lambda l, i: (l, i, 0))
    return pl.pallas_call(
        body, name=name, grid=(nl, nr),
        in_specs=[part_spec(k) for k in range(nl)] + [blk, blk, blk],
        out_specs=[blk] * 4,
        out_shape=[jax.ShapeDtypeStruct((nl, r, c), F32)] * 4,
        compiler_params=_params(2),
    )(*parts, w, m, v)


def _rope_tables(positions):
    half = ROPE_DIM // 2
    inv_freq = ROPE_THETA ** (-jnp.arange(0, ROPE_DIM, 2, dtype=F32) / ROPE_DIM)
    ang = positions.astype(F32)[:, None] * inv_freq
    cos, sin = jnp.cos(ang), jnp.sin(ang)
    t = positions.shape[0]
    zeros = lambda w: jnp.zeros((t, w), F32)
    c = jnp.concatenate([cos, cos, jnp.ones((t, ATTN_HEAD_DIM - ROPE_DIM), F32)], axis=1)
    s1 = jnp.concatenate([-sin, zeros(ATTN_HEAD_DIM - half)], axis=1)
    s2 = jnp.concatenate([zeros(half), sin, zeros(ATTN_HEAD_DIM - ROPE_DIM)], axis=1)
    return tuple(jnp.tile(a, (1, LANES // ATTN_HEAD_DIM)) for a in (c, s1, s2))


def _ffn_layer_fwd(h, nw, w, carry):
    (out, g, u, hn), landed = _ffn_fwd(h, nw, w["wg"], w["wu"], w["wd"], carry, False)
    return out, (h, g, u, hn), landed


def _ffn_layer_bwd(dy, saved, nw, w, carry, early=None):
    h, g, u, hn = saved
    (dh, dg, du, act, dz, dnw), landed = _ffn_bwd(dy, h, nw, g, u, w["wg"], w["wu"], w["wd"], carry, True)
    dwg = _matmul_tn(hn, dg, FFN_TF, "ffn_dwg")
    dwu = _matmul_tn(hn, du, FFN_TF, "ffn_dwu")
    if early is None:
        dwd, early_landed = _matmul_tn(act, dz, D_MODEL, "ffn_dwd"), ()
    else:
        dwd, early_landed = _matmul_tn(act, dz, D_MODEL, "ffn_dwd", early(dwg, dwu))
    return dh, dnw, (dwg, dwu, dwd), landed, early_landed


def _attn_layer_fwd(h, nw, w, ropes):
    q, k, v, hn = _attn_qkv_fwd(h, nw, w["w_in"], w["b_in"], *ropes)
    o = _attn_core_fwd(q, k, v, w["sinks"])
    out = _proj_out_fwd(o, w["w_out"], w["b_out"], h)
    return out, (h, hn, q, k, v, o)


def _attn_layer_bwd(dy, saved, nw, w, ropes):
    h, hn, q, k, v, o = saved
    do, dyb, db_out = _proj_out_bwd(dy, w["w_out"])
    dw_out = _matmul_tn(o, dyb, D_MODEL, "attn_dw_out")
    dq, dk, dv, dsink = _attn_core_bwd(q, k, v, do, w["sinks"])
    dh, dqkv, db_in, dnw = _attn_qkv_bwd(dy, h, nw, w["w_in"], dq, dk, dv, *ropes)
    dw_in = _matmul_tn(hn, dqkv, ATTN_IN, "attn_dw_in")
    return dh, dnw, dict(w_in=dw_in, b_in=db_in, sinks=dsink[:, 0], w_out=dw_out, b_out=db_out)


def _gdn_layer_fwd(h, nw, w):
    x, z, ba, hn = _gdn_proj_fwd(h, nw, w["w_qkv"], w["w_z"], w["w_ba"])
    q, k, v, gates = _gdn_conv_fwd(x, w["conv_w"], ba, w["gp"])
    u, ww, qd, kd, aqk, cd = _gdn_intra_fwd(q, k, v, gates)
    o, st = _gdn_scan_fwd(u, ww, qd, kd, aqk, cd)
    out, gated = _gdn_out_fwd(o, z, w["norm_w"], w["w_out"], h)
    return out, (h, hn, x, z, ba, q, k, v, gates, u, ww, qd, kd, aqk, cd, st, o, gated)


def _gdn_layer_bwd(dy, saved, nw, w):
    h, hn, x, z, ba, q, k, v, gates, u, ww, qd, kd, aqk, cd, st, o, gated = saved
    do, dz, dyb, dnorm_w = _gdn_out_bwd(dy, w["w_out"], o, z, w["norm_w"])
    dw_out = _matmul_tn(gated, dyb, D_MODEL, "gdn_dw_out")
    du, dw, dqd, dkd, daqk, daqkt, dcd = _gdn_scan_bwd(u, ww, qd, kd, aqk, cd, st, do)
    dq, dk, dv, dgates = _gdn_intra_bwd(q, k, v, gates, u, ww, du, dw, dqd, dkd, daqk, daqkt, dcd)
    dc, dba, dcw, dgp = _gdn_conv_bwd_a(x, w["conv_w"], ba, w["gp"], dq, dk, dv, dgates)
    dx = _gdn_conv_bwd_b(dc, w["conv_w"])
    dh, dnw = _gdn_proj_bwd(dy, h, nw, w["w_qkv"], w["w_z"], w["w_ba"], dx, dz, dba)
    dw_qkv = _matmul_tn(hn, dx, GDN_CONV_W // 2, "gdn_dw_qkv")
    dw_z = _matmul_tn(hn, dz, D_MODEL, "gdn_dw_z")
    dw_ba = _matmul_tn(hn, dba, LANES, "gdn_dw_ba")
    dw_in = jnp.concatenate([dw_qkv, dw_z, dw_ba[:, :2 * GDN_HEADS]], axis=1)
    lo, hi = GDN_HEADS, 2 * GDN_HEADS
    return dh, dnw, dict(w_in=dw_in, conv_w=dcw[:GDN_CONV], A_log=dgp[0, lo:hi], dt_bias=dgp[1, lo:hi],
                         norm_w=dnorm_w[0], w_out=dw_out)


def _local_step(x, positions, target, norms, final_norm, plan):
    ropes = _rope_tables(positions)
    h = x
    saved = []
    for layer in range(DEPTH):
        h, s1, landed = _ffn_layer_fwd(h, norms["ffn1"][layer], plan.weights("ffn1", layer),
                                       plan.fwd_carry("ffn1", layer))
        plan.fwd_landed(landed)
        if layer % 2 == 0:
            h, s2 = _attn_layer_fwd(h, norms["mix"][layer], plan.weights("mix", layer), ropes)
        else:
            h, s2 = _gdn_layer_fwd(h, norms["mix"][layer], plan.weights("mix", layer))
        h, s3, landed = _ffn_layer_fwd(h, norms["ffn2"][layer], plan.weights("ffn2", layer),
                                       plan.fwd_carry("ffn2", layer))
        plan.fwd_landed(landed)
        saved.append((s1, s2, s3))
    loss, dh, dfinal = _loss_head(h, final_norm, target)

    g_norm = {k: [None] * DEPTH for k in ("ffn1", "mix", "ffn2")}
    for layer in reversed(range(DEPTH)):
        s1, s2, s3 = saved[layer]
        dh, g_norm["ffn2"][layer], gw, landed, _ = _ffn_layer_bwd(
            dh, s3, norms["ffn2"][layer], plan.weights("ffn2", layer), plan.bwd_carry("ffn2", layer))
        plan.bwd_landed(landed)
        plan.grads("ffn2", layer, gw)
        if layer % 2 == 0:
            dh, g_norm["mix"][layer], gw = _attn_layer_bwd(dh, s2, norms["mix"][layer], plan.weights("mix", layer),
                                                           ropes)
        else:
            dh, g_norm["mix"][layer], gw = _gdn_layer_bwd(dh, s2, norms["mix"][layer], plan.weights("mix", layer))
        plan.grads("mix", layer, gw)
        dh, g_norm["ffn1"][layer], gw, landed, early_landed = _ffn_layer_bwd(
            dh, s1, norms["ffn1"][layer], plan.weights("ffn1", layer), plan.bwd_carry("ffn1", layer),
            plan.early_parts("ffn1", layer))
        plan.bwd_landed(landed)
        plan.early_landed("ffn1", layer, early_landed)
        plan.grads("ffn1", layer, gw)
    return loss, dh, dfinal, g_norm


def _cols_full(g):
    return jnp.transpose(g, (1, 0, 2)).reshape(g.shape[1], -1)


def _rows_full(g):
    return g.reshape(-1, g.shape[-1])


class _ShardedWeights:
    def __init__(self, shards, small):
        self.shards, self.small = shards, small
        self.whole, self.pending, self.received, self.small_grads, self.early = {}, {}, {}, {}, {}
        self.in_flight = []
        first = [("ffn1", 0)]
        self.in_flight = first
        self.fwd_landed(_exchange(self._shards_of(first), False, "gather_first"))

    def _group(self, key):
        kind, layer = key
        j = layer // 2
        if kind != "mix":
            return [self.shards[kind + "_w_gate_up"][layer], self.shards[kind + "_w_down"][layer]]
        if layer % 2 == 0:
            return [self.shards["attn_w_in"][j], self.shards["attn_w_out"][j]]
        return [self.shards["gdn_w_in"][j], self.shards["gdn_w_out"][j], self.shards["gdn_conv_w"][j]]

    def _shards_of(self, keys):
        return [a for key in keys for a in self._group(key)]

    def weights(self, kind, layer):
        return self.whole[(kind, layer)]

    def fwd_carry(self, kind, layer):
        if kind == "ffn1":
            keys = [("mix", 0), ("ffn2", 0)] if layer == 0 else [("ffn2", layer)]
        else:
            keys = [("ffn1", layer + 1), ("mix", layer + 1)] if layer + 1 < DEPTH else []
        self.in_flight = keys
        return self._shards_of(keys)

    def fwd_landed(self, landed):
        landed = list(landed)
        for key in self.in_flight:
            kind, layer = key
            j = layer // 2
            sm = self.small
            if kind != "mix":
                gu, dn = landed[:2]
                w = dict(wg=_cols_full(gu[:4]), wu=_cols_full(gu[4:]), wd=_rows_full(dn))
                landed = landed[2:]
            elif layer % 2 == 0:
                w = dict(w_in=_cols_full(landed[0]), w_out=_rows_full(landed[1]), b_in=sm["attn_b_in"][j][None],
                         sinks=sm["attn_sinks"][j][None], b_out=sm["attn_b_out"][j][None])
                landed = landed[2:]
            else:
                w_in = _cols_full(landed[0])
                w_ba = jnp.pad(w_in[:, GDN_CONV_W + D_MODEL:], ((0, 0), (0, LANES - 2 * GDN_HEADS)))
                gp = jnp.pad(jnp.stack([sm["gdn_A_log"][j], sm["gdn_dt_bias"][j]]),
                             ((0, 6), (GDN_HEADS, LANES - 2 * GDN_HEADS)))
                w = dict(w_qkv=w_in[:, :GDN_CONV_W], w_z=w_in[:, GDN_CONV_W:GDN_CONV_W + D_MODEL], w_ba=w_ba,
                         conv_w=_cols_full(landed[2]), gp=gp, norm_w=sm["gdn_norm_w"][j][None],
                         w_out=_rows_full(landed[1]))
                landed = landed[3:]
            self.whole[key] = w
        self.in_flight = []

    @staticmethod
    def _gate_up_part(dwg, dwu):
        return jnp.concatenate([_cols_shards(dwg, 4), _cols_shards(dwu, 4)], axis=0).astype(BF16)

    def early_parts(self, kind, layer):
        if (kind, layer) != ("ffn1", 0):
            return None
        return lambda dwg, dwu: [self._gate_up_part(dwg, dwu)]

    def early_landed(self, kind, layer, landed):
        if len(landed):
            self.early[(kind, layer)] = list(landed)

    def grads(self, kind, layer, g):
        if kind != "mix":
            dwg, dwu, dwd = g
            parts = [_rows_shards(dwd).astype(BF16)]
            if (kind, layer) not in self.early:
                parts = [self._gate_up_part(dwg, dwu)] + parts
        else:
            parts = [_cols_shards(g["w_in"]).astype(BF16), _rows_shards(g["w_out"]).astype(BF16)]
            if layer % 2 == 1:
                parts.append(_cols_shards(g["conv_w"]))
            self.small_grads[layer] = g
        self.pending[(kind, layer)] = parts

    def bwd_carry(self, kind, layer):
        if kind == "ffn1":
            keys = [("ffn2", layer), ("mix", layer)]
        else:
            keys = [("ffn1", layer + 1)] if layer + 1 < DEPTH else []
        self.in_flight = keys
        return [a for key in keys for a in self.pending[key]]

    def bwd_landed(self, landed):
        landed = list(landed)
        for key in self.in_flight:
            n = len(self.pending.pop(key))
            self.received[key], landed = landed[:n], landed[n:]
        self.in_flight = []

    def finish(self):
        self.in_flight = list(self.pending)
        self.bwd_landed(_exchange([a for key in self.in_flight for a in self.pending[key]], True, "scatter_last"))
        for key, landed in self.early.items():
            self.received[key] = landed + self.received[key]


def _cols_shards(full, n=N_DEV):
    r = full.shape[0]
    return jnp.transpose(full.reshape(r, n, -1), (1, 0, 2))


def _rows_shards(full):
    return full.reshape(N_DEV, -1, full.shape[-1])


_SMALL = ("ffn1_norm", "mix_norm", "ffn2_norm", "attn_b_in", "attn_sinks", "attn_b_out",
          "gdn_A_log", "gdn_dt_bias", "gdn_norm_w", "final_norm")
_BIG = ("ffn1_w_gate_up", "ffn1_w_down", "ffn2_w_gate_up", "ffn2_w_down", "attn_w_in", "attn_w_out",
        "gdn_w_in", "gdn_conv_w", "gdn_w_out")
_WEIGHTS = ("ffn1_norm", "ffn1_w_gate_up", "ffn1_w_down", "mix_norm", "ffn2_norm", "ffn2_w_gate_up",
            "ffn2_w_down", "attn_w_in", "attn_b_in", "attn_sinks", "attn_w_out", "attn_b_out", "gdn_w_in",
            "gdn_conv_w", "gdn_A_log", "gdn_dt_bias", "gdn_norm_w", "gdn_w_out", "final_norm")


def _pack_small(vals):
    flat = jnp.concatenate([v.reshape(-1).astype(F32) for v in vals])
    pad = (-flat.shape[0]) % (8 * LANES)
    return jnp.pad(flat, (0, pad)).reshape(-1, LANES)


def _unpack_small(packed, shapes):
    flat = packed.reshape(-1)
    out, off = [], 0
    for s in shapes:
        size = 1
        for d in s:
            size *= d
        out.append(flat[off:off + size].reshape(s))
        off += size
    return out


def kernel(x, positions, ffn1_norm, ffn1_w_gate_up, ffn1_w_down, mix_norm, ffn2_norm, ffn2_w_gate_up, ffn2_w_down, attn_w_in, attn_b_in, attn_sinks, attn_w_out, attn_b_out, gdn_w_in, gdn_conv_w, gdn_A_log, gdn_dt_bias, gdn_norm_w, gdn_w_out, final_norm, loss_target, m_ffn1_norm, m_ffn1_w_gate_up, m_ffn1_w_down, m_mix_norm, m_ffn2_norm, m_ffn2_w_gate_up, m_ffn2_w_down, m_attn_w_in, m_attn_b_in, m_attn_sinks, m_attn_w_out, m_attn_b_out, m_gdn_w_in, m_gdn_conv_w, m_gdn_A_log, m_gdn_dt_bias, m_gdn_norm_w, m_gdn_w_out, m_final_norm, v_ffn1_norm, v_ffn1_w_gate_up, v_ffn1_w_down, v_mix_norm, v_ffn2_norm, v_ffn2_w_gate_up, v_ffn2_w_down, v_attn_w_in, v_attn_b_in, v_attn_sinks, v_attn_w_out, v_attn_b_out, v_gdn_w_in, v_gdn_conv_w, v_gdn_A_log, v_gdn_dt_bias, v_gdn_norm_w, v_gdn_w_out, v_final_norm):
    args = dict(locals())
    wts = {n: args[n] for n in _WEIGHTS}
    moms = {n: args["m_" + n] for n in _WEIGHTS}
    vels = {n: args["v_" + n] for n in _WEIGHTS}

    shards = {n: wts[n] if n == "gdn_conv_w" else wts[n].astype(BF16) for n in _BIG}
    plan = _ShardedWeights(shards, wts)
    norms = dict(ffn1=ffn1_norm[:, None, :], mix=mix_norm[:, None, :], ffn2=ffn2_norm[:, None, :])
    loss, grad_x, dfinal, g_norm = _local_step(x[0], positions[0], loss_target[0], norms, final_norm[None], plan)
    plan.finish()

    layers = range(DEPTH)
    got = lambda kind, ls, i: [plan.received[(kind, l)][i] for l in ls]
    received = {"attn_w_in": got("mix", layers[0::2], 0), "attn_w_out": got("mix", layers[0::2], 1),
                "gdn_w_in": got("mix", layers[1::2], 0), "gdn_w_out": got("mix", layers[1::2], 1),
                "gdn_conv_w": got("mix", layers[1::2], 2)}
    for kind in ("ffn1", "ffn2"):
        received[kind + "_w_gate_up"] = got(kind, layers, 0)
        received[kind + "_w_down"] = got(kind, layers, 1)
    g_attn = [plan.small_grads[l] for l in layers[0::2]]
    g_gdn = [plan.small_grads[l] for l in layers[1::2]]


    small_g = dict(
        ffn1_norm=jnp.concatenate(g_norm["ffn1"], axis=0), mix_norm=jnp.concatenate(g_norm["mix"], axis=0),
        ffn2_norm=jnp.concatenate(g_norm["ffn2"], axis=0),
        attn_b_in=jnp.concatenate([g["b_in"] for g in g_attn], axis=0),
        attn_sinks=jnp.stack([g["sinks"] for g in g_attn]),
        attn_b_out=jnp.concatenate([g["b_out"] for g in g_attn], axis=0),
        gdn_A_log=jnp.stack([g["A_log"] for g in g_gdn]), gdn_dt_bias=jnp.stack([g["dt_bias"] for g in g_gdn]),
        gdn_norm_w=jnp.stack([g["norm_w"] for g in g_gdn]), final_norm=dfinal[0])
    small_parts = _exchange([_pack_small([small_g[n] for n in _SMALL] + [loss[0, :1]])], False, "gather_small")[0]
    pad1 = jnp.zeros((1,), F32)
    sw = _pack_small([wts[n] for n in _SMALL] + [pad1])
    sm = _pack_small([moms[n] for n in _SMALL] + [pad1])
    sv = _pack_small([vels[n] for n in _SMALL] + [pad1])
    shapes = [wts[n].shape for n in _SMALL] + [(1,)]
    small_out = [_unpack_small(o[0], shapes)
                 for o in _adamw([small_parts], sw[None], sm[None], sv[None], "adamw_small")]
    results = {n: tuple(o[i] for o in small_out) for i, n in enumerate(_SMALL)}
    loss_total = small_out[0][-1][0]

    for n in _BIG:
        results[n] = tuple(_adamw(received[n], wts[n], moms[n], vels[n], "adamw_" + n))

    return (loss_total, grad_x[None],
            *[results[n][0] for n in _WEIGHTS], *[results[n][1] for n in _WEIGHTS],
            *[results[n][2] for n in _WEIGHTS], *[results[n][3] for n in _WEIGHTS])
```

```python
import jax
import jax.numpy as jnp
from jax import lax
from jax.experimental import pallas as pl
from jax.experimental.pallas import tpu as pltpu

F32 = jnp.float32
BF16 = jnp.bfloat16

D_MODEL = 1024
DEPTH = 4
D_FF = 2816
NORM_EPS = 1e-6
N_DEV = 8

ATTN_Q_HEADS = 16
ATTN_KV_HEADS = 4
ATTN_HEAD_DIM = 64
ATTN_GROUP = 4
ATTN_BLOCK = 128
ROPE_DIM = 16
ROPE_THETA = 500000.0
ATTN_Q_W = 1024
ATTN_KV_W = 256
ATTN_IN = 1536
ATTN_SCALE = ATTN_HEAD_DIM ** -0.5

GDN_HEADS = 8
GDN_DK = 128
GDN_CONV = 4
GDN_CHUNK = 64
GDN_CONV_W = 3072
GDN_IN = 4112
GDN_QSCALE = GDN_DK ** -0.5
GDN_ROWS = 1024
GDN_SCAN_ROWS = 1024
GDN_INTRA_HEADS = 2
GDN_SCAN_HEADS = 2
GDN_SUB_SHIFT = 4

ADAM_LR = 0.001
ADAM_B1 = 0.9
ADAM_B2 = 0.999
ADAM_EPS = 1e-08
ADAM_WD = 0.01
ADAM_STEP = 10

LANES = 128
NEG_BIG = -1e30
VMEM_LIMIT_BYTES = 56 * 1024 * 1024
MESH_ID = pl.DeviceIdType.MESH


def _params(n_axes, vmem_limit_bytes=VMEM_LIMIT_BYTES):
    return pltpu.CompilerParams(dimension_semantics=("arbitrary",) * n_axes,
                                vmem_limit_bytes=vmem_limit_bytes)


def _nn(a, b):
    return jnp.dot(a, b, preferred_element_type=F32)


def _nt(a, b):
    return lax.dot_general(a, b, (((1,), (1,)), ((), ())), preferred_element_type=F32)


def _tn(a, b):
    return lax.dot_general(a, b, (((0,), (0,)), ((), ())), preferred_element_type=F32)


def _bnn(a, b, precision=None):
    return lax.dot_general(a, b, (((2,), (1,)), ((0,), (0,))), precision=precision,
                           preferred_element_type=F32)


def _bnt(a, b, precision=None):
    return lax.dot_general(a, b, (((2,), (2,)), ((0,), (0,))), precision=precision,
                           preferred_element_type=F32)


def _sigmoid(x):
    return 1.0 / (1.0 + jnp.exp(-x))


def _rms_stats(x):
    r = lax.rsqrt(jnp.mean(x * x, axis=-1, keepdims=True) + NORM_EPS)
    return r, x * r


def _norm_bwd(x, nw, dhn):
    r, xhat = _rms_stats(x)
    dxh = dhn * nw
    dx = r * (dxh - xhat * jnp.mean(dxh * xhat, axis=-1, keepdims=True))
    dnw = jnp.sum(dhn * xhat, axis=0, keepdims=True)
    return dx, dnw


def _row_tile(t, pref):
    return min(t, pref)


FFN_TM = 512
FFN_BWD_TM = 512
FFN_BWD_TF = 1408
FFN_BWD_VMEM_LIMIT_BYTES = 61 * 1024 * 1024
FFN_TF = 1408


def _carried(body, n_in, n_out, carry, scatter, last_step):
    nc = len(carry)
    if not nc:
        return body

    def wrapped(*refs):
        ins, cin = refs[:n_in], refs[n_in:n_in + nc]
        outs = refs[n_in + nc:n_in + nc + n_out]
        cout = refs[n_in + nc + n_out:n_in + 2 * nc + n_out]
        scratch = refs[n_in + 2 * nc + n_out:]
        sems = scratch[len(scratch) - 3:]
        ids = [pl.program_id(ax) for ax in range(len(last_step))]
        first, last = ids[0] == 0, ids[0] == last_step[0]
        for ax in range(1, len(last_step)):
            first = jnp.logical_and(first, ids[ax] == 0)
            last = jnp.logical_and(last, ids[ax] == last_step[ax])

        @pl.when(first)
        def _():
            for cp in _exchange_copies(cin, cout, *sems, scatter):
                cp.start()

        body(*ins, *outs, *scratch[:len(scratch) - 3])

        @pl.when(last)
        def _():
            for cp in _exchange_copies(cin, cout, *sems, scatter):
                cp.wait()

    return wrapped


def _ffn_fwd(h, nw, wg, wu, wd, carry=(), scatter=False):
    t = h.shape[0]
    tm, tf = _row_tile(t, FFN_TM), FFN_TF
    nj = D_FF // tf
    nc = len(carry)

    def body(h_ref, nw_ref, wg_ref, wu_ref, wd_ref, out_ref, g_ref, u_ref, hn_ref, acc_ref):
        j = pl.program_id(1)

        @pl.when(j == 0)
        def _():
            _, xhat = _rms_stats(h_ref[...])
            hn_ref[...] = (xhat * nw_ref[...]).astype(BF16)
            acc_ref[...] = jnp.zeros_like(acc_ref)

        hn = hn_ref[...]
        g = _nn(hn, wg_ref[...])
        u = _nn(hn, wu_ref[...])
        g_ref[...] = g.astype(BF16)
        u_ref[...] = u.astype(BF16)
        a = (g * _sigmoid(g) * u).astype(BF16)
        acc_ref[...] += _nn(a, wd_ref[...])

        @pl.when(j == nj - 1)
        def _():
            out_ref[...] = h_ref[...] + 0.5 * acc_ref[...]

    outs = pl.pallas_call(
        _carried(body, 5, 4, carry, scatter, (t // tm - 1, nj - 1)),
        name="ffn_fwd_x" if nc else "ffn_fwd", grid=(t // tm, nj),
        in_specs=[pl.BlockSpec((tm, D_MODEL), lambda i, j: (i, 0)),
                  pl.BlockSpec((1, D_MODEL), lambda i, j: (0, 0)),
                  pl.BlockSpec((D_MODEL, tf), lambda i, j: (0, j)),
                  pl.BlockSpec((D_MODEL, tf), lambda i, j: (0, j)),
                  pl.BlockSpec((tf, D_MODEL), lambda i, j: (j, 0))] + [_ANY] * nc,
        out_specs=[pl.BlockSpec((tm, D_MODEL), lambda i, j: (i, 0)),
                   pl.BlockSpec((tm, tf), lambda i, j: (i, j)),
                   pl.BlockSpec((tm, tf), lambda i, j: (i, j)),
                   pl.BlockSpec((tm, D_MODEL), lambda i, j: (i, 0))] + [_ANY] * nc,
        out_shape=[jax.ShapeDtypeStruct((t, D_MODEL), F32),
                   jax.ShapeDtypeStruct((t, D_FF), BF16),
                   jax.ShapeDtypeStruct((t, D_FF), BF16),
                   jax.ShapeDtypeStruct((t, D_MODEL), BF16)] + _exchange_shapes(carry, scatter),
        scratch_shapes=[pltpu.VMEM((tm, D_MODEL), F32)] + (_exchange_sems(nc) if nc else []),
        compiler_params=_params(2),
    )(h, nw, wg, wu, wd, *carry)
    return outs[:4], outs[4:]


def _ffn_bwd(dy, h, nw, g, u, wg, wu, wd, carry=(), scatter=True):
    t = h.shape[0]
    tm, tf = _row_tile(t, FFN_BWD_TM), FFN_BWD_TF
    nj = D_FF // tf
    nc = len(carry)

    def body(dy_ref, h_ref, nw_ref, g_ref, u_ref, wg_ref, wu_ref, wd_ref,
             dh_ref, dg_ref, du_ref, a_ref, dz_ref, dnw_ref, acc_ref):
        i, j = pl.program_id(0), pl.program_id(1)

        @pl.when(j == 0)
        def _():
            dz_ref[...] = (0.5 * dy_ref[...]).astype(BF16)
            acc_ref[...] = jnp.zeros_like(acc_ref)

        @pl.when(jnp.logical_and(i == 0, j == 0))
        def _():
            dnw_ref[...] = jnp.zeros_like(dnw_ref)

        da = _nt(dz_ref[...], wd_ref[...])
        gv = g_ref[...].astype(F32)
        uv = u_ref[...].astype(F32)
        sig = _sigmoid(gv)
        silu = gv * sig
        dg = (da * uv * (sig * (1.0 + gv * (1.0 - sig)))).astype(BF16)
        du = (da * silu).astype(BF16)
        dg_ref[...] = dg
        du_ref[...] = du
        a_ref[...] = (silu * uv).astype(BF16)
        acc_ref[...] += _nt(dg, wg_ref[...]) + _nt(du, wu_ref[...])

        @pl.when(j == nj - 1)
        def _():
            dx, dnw = _norm_bwd(h_ref[...], nw_ref[...], acc_ref[...])
            dh_ref[...] = dy_ref[...] + dx
            dnw_ref[...] += dnw

    outs = pl.pallas_call(
        _carried(body, 8, 6, carry, scatter, (t // tm - 1, nj - 1)),
        name="ffn_bwd_x" if nc else "ffn_bwd", grid=(t // tm, nj),
        in_specs=[pl.BlockSpec((tm, D_MODEL), lambda i, j: (i, 0)),
                  pl.BlockSpec((tm, D_MODEL), lambda i, j: (i, 0)),
                  pl.BlockSpec((1, D_MODEL), lambda i, j: (0, 0)),
                  pl.BlockSpec((tm, tf), lambda i, j: (i, j)),
                  pl.BlockSpec((tm, tf), lambda i, j: (i, j)),
                  pl.BlockSpec((D_MODEL, tf), lambda i, j: (0, j)),
                  pl.BlockSpec((D_MODEL, tf), lambda i, j: (0, j)),
                  pl.BlockSpec((tf, D_MODEL), lambda i, j: (j, 0))] + [_ANY] * nc,
        out_specs=[pl.BlockSpec((tm, D_MODEL), lambda i, j: (i, 0)),
                   pl.BlockSpec((tm, tf), lambda i, j: (i, j)),
                   pl.BlockSpec((tm, tf), lambda i, j: (i, j)),
                   pl.BlockSpec((tm, tf), lambda i, j: (i, j)),
                   pl.BlockSpec((tm, D_MODEL), lambda i, j: (i, 0)),
                   pl.BlockSpec((1, D_MODEL), lambda i, j: (0, 0))] + [_ANY] * nc,
        out_shape=[jax.ShapeDtypeStruct((t, D_MODEL), F32),
                   jax.ShapeDtypeStruct((t, D_FF), BF16),
                   jax.ShapeDtypeStruct((t, D_FF), BF16),
                   jax.ShapeDtypeStruct((t, D_FF), BF16),
                   jax.ShapeDtypeStruct((t, D_MODEL), BF16),
                   jax.ShapeDtypeStruct((1, D_MODEL), F32)] + _exchange_shapes(carry, scatter),
        scratch_shapes=[pltpu.VMEM((tm, D_MODEL), F32)] + (_exchange_sems(nc) if nc else []),
        compiler_params=_params(2, FFN_BWD_VMEM_LIMIT_BYTES),
    )(dy, h, nw, g, u, wg, wu, wd, *carry)
    return outs[:6], outs[6:]


def _matmul_tn(a, b, tn, name, carry=(), out_dtype=None):
    out_dtype = BF16 if out_dtype is None else out_dtype
    k, m = a.shape
    n = b.shape[1]
    tm = min(m, 1408 if m % 1408 == 0 else 1024)
    tk = min(k, 2048)
    nc = len(carry)

    grid = (m // tm, n // tn, k // tk)

    def body(a_ref, b_ref, o_ref, acc_ref):
        kk = pl.program_id(2)

        @pl.when(kk == 0)
        def _():
            acc_ref[...] = jnp.zeros_like(acc_ref)

        acc_ref[...] += _tn(a_ref[...], b_ref[...])

        @pl.when(kk == grid[2] - 1)
        def _():
            o_ref[...] = acc_ref[...].astype(out_dtype)

    outs = pl.pallas_call(
        _carried(body, 2, 1, carry, True, tuple(g - 1 for g in grid)),
        name=name + "_x" if nc else name, grid=grid,
        in_specs=[pl.BlockSpec((tk, tm), lambda i, j, kk: (kk, i)),
                  pl.BlockSpec((tk, tn), lambda i, j, kk: (kk, j))] + [_ANY] * nc,
        out_specs=[pl.BlockSpec((tm, tn), lambda i, j, kk: (i, j))] + [_ANY] * nc,
        out_shape=[jax.ShapeDtypeStruct((m, n), out_dtype)] + _exchange_shapes(carry, True),
        scratch_shapes=[pltpu.VMEM((tm, tn), F32)] + (_exchange_sems(nc) if nc else []),
        compiler_params=_params(3),
    )(a, b, *carry)
    return (outs[0], outs[1:]) if nc else outs[0]


ATTN_TM = 512


def _rope(t, c, s1, s2):
    return t * c + pltpu.roll(t, LANES - ROPE_DIM // 2, 1) * s1 + pltpu.roll(t, ROPE_DIM // 2, 1) * s2


def _rope_bwd(d, c, s1, s2):
    return d * c + pltpu.roll(d * s1, ROPE_DIM // 2, 1) + pltpu.roll(d * s2, LANES - ROPE_DIM // 2, 1)


def _attn_qkv_fwd(h, nw, w_in, b_in, rc, rs1, rs2):
    t = h.shape[0]
    tm = _row_tile(t, ATTN_TM)

    def body(h_ref, nw_ref, w_ref, b_ref, c_ref, s1_ref, s2_ref, q_ref, k_ref, v_ref, hn_ref):
        _, xhat = _rms_stats(h_ref[...])
        hn = (xhat * nw_ref[...]).astype(BF16)
        hn_ref[...] = hn
        qkv = _nn(hn, w_ref[...]) + b_ref[...]
        c, s1, s2 = c_ref[...], s1_ref[...], s2_ref[...]
        for s in range(ATTN_Q_W // LANES):
            q_ref[:, s * LANES:(s + 1) * LANES] = _rope(qkv[:, s * LANES:(s + 1) * LANES], c, s1, s2).astype(BF16)
        for s in range(ATTN_KV_W // LANES):
            lo = ATTN_Q_W + s * LANES
            k_ref[:, s * LANES:(s + 1) * LANES] = _rope(qkv[:, lo:lo + LANES], c, s1, s2).astype(BF16)
        v_ref[...] = qkv[:, ATTN_Q_W + ATTN_KV_W:].astype(BF16)

    row = lambda w: pl.BlockSpec((tm, w), lambda i: (i, 0))
    full = lambda a: pl.BlockSpec(a.shape, lambda i: (0, 0))
    return pl.pallas_call(
        body, name="attn_qkv_fwd", grid=(t // tm,),
        in_specs=[row(D_MODEL), full(nw), full(w_in), full(b_in), row(LANES), row(LANES), row(LANES)],
        out_specs=[row(ATTN_Q_W), row(ATTN_KV_W), row(ATTN_KV_W), row(D_MODEL)],
        out_shape=[jax.ShapeDtypeStruct((t, ATTN_Q_W), BF16),
                   jax.ShapeDtypeStruct((t, ATTN_KV_W), BF16),
                   jax.ShapeDtypeStruct((t, ATTN_KV_W), BF16),
                   jax.ShapeDtypeStruct((t, D_MODEL), BF16)],
        compiler_params=_params(1),
    )(h, nw, w_in, b_in, rc, rs1, rs2)


def _attn_group(q_ref, kc_ref, kp_ref, vc_ref, vp_ref, sinks_ref, hk, n):
    hd = ATTN_HEAD_DIM
    cols = slice(hk * hd, (hk + 1) * hd)
    krow = lax.broadcasted_iota(jnp.int32, (2 * ATTN_BLOCK, hd), 0)
    kcat = jnp.concatenate([kp_ref[:, cols], kc_ref[:, cols]], axis=0)
    vcat = jnp.concatenate([vp_ref[:, cols], vc_ref[:, cols]], axis=0)
    kcat = jnp.where(krow == 0, jnp.zeros_like(kcat), kcat)
    vcat = jnp.where(krow == 0, jnp.zeros_like(vcat), vcat)
    heads = [hk * ATTN_GROUP + g for g in range(ATTN_GROUP)]
    qs = jnp.concatenate([q_ref[:, hq * hd:(hq + 1) * hd] for hq in heads], axis=0)
    s = _nt(qs, kcat) * ATTN_SCALE
    rows = ATTN_GROUP * ATTN_BLOCK
    ri = lax.broadcasted_iota(jnp.int32, (rows, 2 * ATTN_BLOCK), 0) & (ATTN_BLOCK - 1)
    cj = lax.broadcasted_iota(jnp.int32, (rows, 2 * ATTN_BLOCK), 1)
    first = jnp.where(n > 0, 0, 2 * ATTN_BLOCK)
    valid = jnp.logical_or(jnp.logical_and(cj < ATTN_BLOCK, cj > ri + first),
                           jnp.logical_and(cj >= ATTN_BLOCK, cj - ATTN_BLOCK <= ri))
    sink = jnp.concatenate([jnp.full((ATTN_BLOCK, 2 * ATTN_BLOCK), sinks_ref[hq], F32) for hq in heads], axis=0)
    s = jnp.where(valid, s, jnp.where(cj == 0, sink, NEG_BIG))
    p = jnp.exp(s - jnp.max(s, axis=-1, keepdims=True))
    probs = p / jnp.sum(p, axis=-1, keepdims=True)
    return heads, qs, kcat, vcat, probs, krow


def _attn_core_fwd(q, k, v, sinks):
    t = q.shape[0]
    nb = t // ATTN_BLOCK

    def body(q_ref, kc_ref, kp_ref, vc_ref, vp_ref, sinks_ref, o_ref):
        n = pl.program_id(0)
        for hk in range(ATTN_KV_HEADS):
            heads, _, _, vcat, probs, _ = _attn_group(q_ref, kc_ref, kp_ref, vc_ref, vp_ref, sinks_ref, hk, n)
            o = _nn(probs.astype(BF16), vcat)
            for g, hq in enumerate(heads):
                o_ref[:, hq * ATTN_HEAD_DIM:(hq + 1) * ATTN_HEAD_DIM] = (
                    o[g * ATTN_BLOCK:(g + 1) * ATTN_BLOCK].astype(BF16))

    cur = lambda w: pl.BlockSpec((ATTN_BLOCK, w), lambda n: (n, 0))
    prev = lambda w: pl.BlockSpec((ATTN_BLOCK, w), lambda n: (jnp.maximum(n - 1, 0), 0))
    return pl.pallas_call(
        body, name="attn_core_fwd", grid=(nb,),
        in_specs=[cur(ATTN_Q_W), cur(ATTN_KV_W), prev(ATTN_KV_W), cur(ATTN_KV_W), prev(ATTN_KV_W),
                  pl.BlockSpec(memory_space=pltpu.SMEM)],
        out_specs=cur(ATTN_Q_W),
        out_shape=jax.ShapeDtypeStruct((t, ATTN_Q_W), BF16),
        compiler_params=_params(1),
    )(q, k, k, v, v, sinks.reshape(-1))


def _attn_core_bwd(q, k, v, do, sinks):
    t = q.shape[0]
    nb = t // ATTN_BLOCK
    hd = ATTN_HEAD_DIM

    def body(q_ref, kc_ref, kp_ref, vc_ref, vp_ref, do_ref, sinks_ref,
             dq_ref, dk_ref, dv_ref, dsink_ref, ck_ref, cv_ref):
        n = pl.program_id(0)

        @pl.when(n == 0)
        def _():
            dsink_ref[...] = jnp.zeros_like(dsink_ref)
            ck_ref[...] = jnp.zeros_like(ck_ref)
            cv_ref[...] = jnp.zeros_like(cv_ref)

        @pl.when(n == nb)
        def _():
            dk_ref[...] = ck_ref[...]
            dv_ref[...] = cv_ref[...]

        @pl.when(n < nb)
        def _():
            for hk in range(ATTN_KV_HEADS):
                heads, qs, kcat, vcat, probs, krow = _attn_group(
                    q_ref, kc_ref, kp_ref, vc_ref, vp_ref, sinks_ref, hk, n)
                dos = jnp.concatenate([do_ref[:, hq * hd:(hq + 1) * hd] for hq in heads], axis=0)
                dp = _nt(dos, vcat)
                delta = jnp.sum(probs * dp, axis=-1, keepdims=True)
                dsf = probs * (dp - delta)
                ds = dsf.astype(BF16)
                dqs = _nn(ds, kcat) * ATTN_SCALE
                dkc = jnp.where(krow == 0, 0.0, _tn(ds, qs) * ATTN_SCALE)
                dvc = jnp.where(krow == 0, 0.0, _tn(probs.astype(BF16), dos))
                for g, hq in enumerate(heads):
                    blk = slice(g * ATTN_BLOCK, (g + 1) * ATTN_BLOCK)
                    dq_ref[:, hq * hd:(hq + 1) * hd] = dqs[blk]
                    dsink_ref[hq:hq + 1, :] += jnp.broadcast_to(
                        jnp.sum(dsf[blk, 0:1], axis=0, keepdims=True), (1, LANES))
                cols = slice(hk * hd, (hk + 1) * hd)
                dk_ref[:, cols] = ck_ref[:, cols] + dkc[:ATTN_BLOCK]
                dv_ref[:, cols] = cv_ref[:, cols] + dvc[:ATTN_BLOCK]
                ck_ref[:, cols] = dkc[ATTN_BLOCK:]
                cv_ref[:, cols] = dvc[ATTN_BLOCK:]

    cur = lambda w: pl.BlockSpec((ATTN_BLOCK, w), lambda n: (jnp.minimum(n, nb - 1), 0))
    prev = lambda w: pl.BlockSpec((ATTN_BLOCK, w), lambda n: (jnp.clip(n - 1, 0, nb - 1), 0))
    return pl.pallas_call(
        body, name="attn_core_bwd", grid=(nb + 1,),
        in_specs=[cur(ATTN_Q_W), cur(ATTN_KV_W), prev(ATTN_KV_W), cur(ATTN_KV_W), prev(ATTN_KV_W),
                  cur(ATTN_Q_W), pl.BlockSpec(memory_space=pltpu.SMEM)],
        out_specs=[cur(ATTN_Q_W), prev(ATTN_KV_W), prev(ATTN_KV_W),
                   pl.BlockSpec((ATTN_Q_HEADS, LANES), lambda n: (0, 0))],
        out_shape=[jax.ShapeDtypeStruct((t, ATTN_Q_W), F32),
                   jax.ShapeDtypeStruct((t, ATTN_KV_W), F32),
                   jax.ShapeDtypeStruct((t, ATTN_KV_W), F32),
                   jax.ShapeDtypeStruct((ATTN_Q_HEADS, LANES), F32)],
        scratch_shapes=[pltpu.VMEM((ATTN_BLOCK, ATTN_KV_W), F32),
                        pltpu.VMEM((ATTN_BLOCK, ATTN_KV_W), F32)],
        compiler_params=_params(1),
    )(q, k, k, v, v, do, sinks.reshape(-1))


def _proj_out_fwd(x, w, b, res):
    t = x.shape[0]
    tm = _row_tile(t, 512)

    def body(x_ref, w_ref, b_ref, r_ref, o_ref):
        o_ref[...] = r_ref[...] + _nn(x_ref[...], w_ref[...]) + b_ref[...]

    row = pl.BlockSpec((tm, D_MODEL), lambda i: (i, 0))
    return pl.pallas_call(
        body, name="proj_out_fwd", grid=(t // tm,),
        in_specs=[row, pl.BlockSpec(w.shape, lambda i: (0, 0)), pl.BlockSpec(b.shape, lambda i: (0, 0)), row],
        out_specs=row,
        out_shape=jax.ShapeDtypeStruct((t, D_MODEL), F32),
        compiler_params=_params(1),
    )(x, w, b, res)


def _proj_out_bwd(dy, w):
    t = dy.shape[0]
    tm = _row_tile(t, 512)

    def body(dy_ref, w_ref, dx_ref, dyb_ref, db_ref):
        @pl.when(pl.program_id(0) == 0)
        def _():
            db_ref[...] = jnp.zeros_like(db_ref)

        dy_v = dy_ref[...]
        dyb = dy_v.astype(BF16)
        dyb_ref[...] = dyb
        dx_ref[...] = _nt(dyb, w_ref[...]).astype(BF16)
        db_ref[...] += jnp.sum(dy_v, axis=0, keepdims=True)

    row = pl.BlockSpec((tm, D_MODEL), lambda i: (i, 0))
    return pl.pallas_call(
        body, name="proj_out_bwd", grid=(t // tm,),
        in_specs=[row, pl.BlockSpec(w.shape, lambda i: (0, 0))],
        out_specs=[row, row, pl.BlockSpec((1, D_MODEL), lambda i: (0, 0))],
        out_shape=[jax.ShapeDtypeStruct((t, D_MODEL), BF16),
                   jax.ShapeDtypeStruct((t, D_MODEL), BF16),
                   jax.ShapeDtypeStruct((1, D_MODEL), F32)],
        compiler_params=_params(1),
    )(dy, w)


def _attn_qkv_bwd(dy, h, nw, w_in, dq, dk, dv, rc, rs1, rs2):
    t = h.shape[0]
    tm = _row_tile(t, ATTN_TM)

    def body(dy_ref, h_ref, nw_ref, w_ref, dq_ref, dk_ref, dv_ref, c_ref, s1_ref, s2_ref,
             dh_ref, dqkv_ref, db_ref, dnw_ref, tmp_ref):
        @pl.when(pl.program_id(0) == 0)
        def _():
            db_ref[...] = jnp.zeros_like(db_ref)
            dnw_ref[...] = jnp.zeros_like(dnw_ref)

        c, s1, s2 = c_ref[...], s1_ref[...], s2_ref[...]
        for s in range(ATTN_Q_W // LANES):
            tmp_ref[:, s * LANES:(s + 1) * LANES] = _rope_bwd(dq_ref[:, s * LANES:(s + 1) * LANES], c, s1, s2)
        for s in range(ATTN_KV_W // LANES):
            lo = ATTN_Q_W + s * LANES
            tmp_ref[:, lo:lo + LANES] = _rope_bwd(dk_ref[:, s * LANES:(s + 1) * LANES], c, s1, s2)
        tmp_ref[:, ATTN_Q_W + ATTN_KV_W:] = dv_ref[...]
        dqkv = tmp_ref[...]
        db_ref[...] += jnp.sum(dqkv, axis=0, keepdims=True)
        dqkv_b = dqkv.astype(BF16)
        dqkv_ref[...] = dqkv_b
        dx, dnw = _norm_bwd(h_ref[...], nw_ref[...], _nt(dqkv_b, w_ref[...]))
        dh_ref[...] = dy_ref[...] + dx
        dnw_ref[...] += dnw

    row = lambda w: pl.BlockSpec((tm, w), lambda i: (i, 0))
    full = lambda a: pl.BlockSpec(a.shape, lambda i: (0, 0))
    return pl.pallas_call(
        body, name="attn_qkv_bwd", grid=(t // tm,),
        in_specs=[row(D_MODEL), row(D_MODEL), full(nw), full(w_in), row(ATTN_Q_W), row(ATTN_KV_W),
                  row(ATTN_KV_W), row(LANES), row(LANES), row(LANES)],
        out_specs=[row(D_MODEL), row(ATTN_IN), pl.BlockSpec((1, ATTN_IN), lambda i: (0, 0)),
                   pl.BlockSpec((1, D_MODEL), lambda i: (0, 0))],
        out_shape=[jax.ShapeDtypeStruct((t, D_MODEL), F32),
                   jax.ShapeDtypeStruct((t, ATTN_IN), BF16),
                   jax.ShapeDtypeStruct((1, ATTN_IN), F32),
                   jax.ShapeDtypeStruct((1, D_MODEL), F32)],
        scratch_shapes=[pltpu.VMEM((tm, ATTN_IN), F32)],
        compiler_params=_params(1),
    )(dy, h, nw, w_in, dq, dk, dv, rc, rs1, rs2)


GDN_TM = 512
GDN_CONV_TM = 256


def _gdn_proj_fwd(h, nw, w_qkv, w_z, w_ba):
    t = h.shape[0]
    tm = _row_tile(t, GDN_TM)

    def body(h_ref, nw_ref, wq_ref, wz_ref, wb_ref, x_ref, z_ref, ba_ref, hn_ref):
        _, xhat = _rms_stats(h_ref[...])
        hn = (xhat * nw_ref[...]).astype(BF16)
        hn_ref[...] = hn
        x_ref[...] = _nn(hn, wq_ref[...])
        z_ref[...] = _nn(hn, wz_ref[...])
        ba_ref[...] = _nn(hn, wb_ref[...])

    row = lambda w: pl.BlockSpec((tm, w), lambda i: (i, 0))
    full = lambda a: pl.BlockSpec(a.shape, lambda i: (0, 0))
    return pl.pallas_call(
        body, name="gdn_proj_fwd", grid=(t // tm,),
        in_specs=[row(D_MODEL), full(nw), full(w_qkv), full(w_z), full(w_ba)],
        out_specs=[row(GDN_CONV_W), row(D_MODEL), row(LANES), row(D_MODEL)],
        out_shape=[jax.ShapeDtypeStruct((t, GDN_CONV_W), F32),
                   jax.ShapeDtypeStruct((t, D_MODEL), F32),
                   jax.ShapeDtypeStruct((t, LANES), F32),
                   jax.ShapeDtypeStruct((t, D_MODEL), BF16)],
        compiler_params=_params(1),
    )(h, nw, w_qkv, w_z, w_ba)


def _softplus(x):
    return jnp.maximum(x, 0.0) + jnp.log(1.0 + jnp.exp(-jnp.abs(x)))


def _conv_taps(x, halo, cw):
    tm = x.shape[0]
    xx = jnp.concatenate([halo, x], axis=0)
    taps = [xx[8 - (GDN_CONV - 1) + i: 8 - (GDN_CONV - 1) + i + tm] for i in range(GDN_CONV)]
    c = taps[0] * cw[0:1]
    for i in range(1, GDN_CONV):
        c = c + taps[i] * cw[i:i + 1]
    return c, taps


def _gates(ba, gp):
    lane = lax.broadcasted_iota(jnp.int32, ba.shape, 1)
    beta = _sigmoid(ba)
    pre = ba + gp[1:2]
    g = -jnp.exp(gp[0:1]) * _softplus(pre)
    gates = jnp.where(lane < GDN_HEADS, beta, jnp.where(lane < 2 * GDN_HEADS, g, 0.0))
    return lane, beta, pre, g, gates


def _gdn_conv_fwd(x, cw, ba, gp):
    t = x.shape[0]
    tm = _row_tile(t, GDN_CONV_TM)

    def body(x_ref, halo_ref, cw_ref, ba_ref, gp_ref, q_ref, k_ref, v_ref, gates_ref):
        halo = jnp.where(pl.program_id(0) > 0, halo_ref[...], 0.0)
        c, _ = _conv_taps(x_ref[...], halo, cw_ref[...])
        s = c * _sigmoid(c)
        for hh in range(GDN_HEADS):
            sq = s[:, hh * LANES:(hh + 1) * LANES]
            q_ref[:, hh * LANES:(hh + 1) * LANES] = (
                sq * lax.rsqrt(jnp.sum(sq * sq, axis=-1, keepdims=True) + NORM_EPS) * GDN_QSCALE)
            sk = s[:, D_MODEL + hh * LANES:D_MODEL + (hh + 1) * LANES]
            k_ref[:, hh * LANES:(hh + 1) * LANES] = (
                sk * lax.rsqrt(jnp.sum(sk * sk, axis=-1, keepdims=True) + NORM_EPS))
        v_ref[...] = s[:, 2 * D_MODEL:]
        gates_ref[...] = _gates(ba_ref[...], gp_ref[...])[4]

    row = lambda w: pl.BlockSpec((tm, w), lambda i: (i, 0))
    full = lambda a: pl.BlockSpec(a.shape, lambda i: (0, 0))
    halo = pl.BlockSpec((8, GDN_CONV_W), lambda i: (jnp.maximum(i * (tm // 8) - 1, 0), 0))
    return pl.pallas_call(
        body, name="gdn_conv_fwd", grid=(t // tm,),
        in_specs=[row(GDN_CONV_W), halo, full(cw), row(LANES), full(gp)],
        out_specs=[row(D_MODEL), row(D_MODEL), row(D_MODEL), row(LANES)],
        out_shape=[jax.ShapeDtypeStruct((t, D_MODEL), F32)] * 3 + [jax.ShapeDtypeStruct((t, LANES), F32)],
        compiler_params=_params(1),
    )(x, x, cw, ba, gp)


def _split3(x):
    hi = x.astype(BF16).astype(F32)
    r1 = x - hi
    mid = r1.astype(BF16).astype(F32)
    lo = (r1 - mid).astype(BF16).astype(F32)
    return hi, mid, lo


def _chunk_cumsum(col, keep):
    nb, c, _ = col.shape
    l3 = lax.broadcasted_iota(jnp.int32, (nb, c, LANES), 2)
    hi, mid, lo = _split3(col)
    pieces = jnp.where(l3 == 0, hi, jnp.where(l3 == 1, mid, jnp.where(l3 == 2, lo, 0.0))).astype(BF16)
    s = _bnn(jnp.where(keep, 1.0, 0.0).astype(BF16), pieces)
    return s[..., 0:1] + s[..., 1:2] + s[..., 2:3]


def _unit_inverse(nmat):
    nb, c, _ = nmat.shape
    ri = lax.broadcasted_iota(jnp.int32, (nb, c, c), 1)
    ci = lax.broadcasted_iota(jnp.int32, (nb, c, c), 2)
    eye = jnp.where(ri == ci, 1.0, 0.0).astype(F32)
    same = (ri >> GDN_SUB_SHIFT) == (ci >> GDN_SUB_SHIFT)
    nd = jnp.where(same, nmat, 0.0)
    no = nmat - nd
    mm = lambda a, b: _bnn(a.astype(BF16), b.astype(BF16))
    n2 = mm(nd, nd)
    n4 = mm(n2, n2)
    n8 = mm(n4, n4)
    td = mm(mm(mm(eye - nd, eye + n2), eye + n4), eye + n8)
    bm = mm(td, no)
    b2 = mm(bm, bm)
    return mm(mm(eye - bm, eye + b2), td)


class _LaneWindow:
    def __init__(self, ref, j):
        self.ref, self.lanes, self.shape = ref, slice(j * LANES, (j + 1) * LANES), (ref.shape[0], LANES)

    def __getitem__(self, idx):
        return self.ref[:, self.lanes]

    def __setitem__(self, idx, val):
        self.ref[:, self.lanes] = val


def _chunk_terms(q_ref, k_ref, gates_ref, h, transposed):
    rows = k_ref.shape[0]
    nb = rows // GDN_CHUNK
    c = GDN_CHUNK
    lane = lax.broadcasted_iota(jnp.int32, (rows, LANES), 1)
    gt = gates_ref[...]
    beta = jnp.sum(jnp.where(lane == h, gt, 0.0), axis=-1, keepdims=True).reshape(nb, c, 1)
    g = jnp.sum(jnp.where(lane == GDN_HEADS + h, gt, 0.0), axis=-1, keepdims=True).reshape(nb, c, 1)
    ri = lax.broadcasted_iota(jnp.int32, (nb, c, c), 1)
    ci = lax.broadcasted_iota(jnp.int32, (nb, c, c), 2)
    gcol = _chunk_cumsum(g, ri >= ci)
    gamma = jnp.broadcast_to(gcol, (nb, c, LANES))
    l3 = lax.broadcasted_iota(jnp.int32, (nb, c, LANES), 2)
    gh, gm, gl = _split3(gcol)
    pmat = jnp.where(l3 == 0, gh, jnp.where(l3 == 1, gm, jnp.where(l3 == 2, gl, jnp.where(l3 < 6, 1.0, 0.0))))
    qmat = jnp.where(l3 < 3, 1.0, jnp.where(l3 == 3, -gh, jnp.where(l3 == 4, -gm, jnp.where(l3 == 5, -gl, 0.0))))
    pmat, qmat = pmat.astype(BF16), qmat.astype(BF16)
    k = k_ref[...].reshape(nb, c, LANES)
    q = q_ref[...].reshape(nb, c, LANES)
    kb = k * beta
    kbf, kbb, qb = k.astype(BF16), kb.astype(BF16), q.astype(BF16)
    out = dict(beta=beta, g=g, gamma=gamma, k=k, q=q, kb=kb, kbf=kbf, kbb=kbb, qb=qb, ri=ri, ci=ci)
    diff = _bnt(pmat, qmat)
    lmat = jnp.exp(jnp.where(ri >= ci, diff, NEG_BIG))
    out["L"] = lmat
    out["A"] = jnp.where(ri > ci, _bnt(kbb, kbf) * lmat, 0.0)
    out["Aqk"] = jnp.where(ri >= ci, _bnt(qb, kbf) * lmat, 0.0)
    if transposed:
        difft = _bnt(qmat, pmat)
        lt = jnp.exp(jnp.where(ci >= ri, difft, NEG_BIG))
        out["LT"] = lt
        out["AT"] = jnp.where(ci > ri, _bnt(kbf, kbb) * lt, 0.0)
        out["AqkT"] = jnp.where(ci >= ri, _bnt(kbf, qb) * lt, 0.0)
    return out


def _gdn_intra_fwd(q, k, v, gates):
    t = q.shape[0]
    rows = _row_tile(t, GDN_ROWS)
    nb = rows // GDN_CHUNK
    nchunks = t // GDN_CHUNK
    hs = GDN_INTRA_HEADS

    def head(h, hh, q_ref, k_ref, v_ref, gates_ref, u_ref, w_ref, qd_ref, kd_ref, aqk_ref, cd_ref):
        tm_ = _chunk_terms(q_ref, k_ref, gates_ref, h, False)
        gamma, beta = tm_["gamma"], tm_["beta"]
        eg = jnp.exp(gamma)
        tinv = _unit_inverse(tm_["A"])
        v3 = v_ref[...].reshape(nb, GDN_CHUNK, LANES)
        rhs = jnp.concatenate([v3 * beta, tm_["kb"] * eg], axis=-1)
        sol = _bnn(tinv.astype(BF16), rhs.astype(BF16))
        u_ref[...] = sol[..., :LANES].reshape(rows, LANES)
        w_ref[...] = sol[..., LANES:].reshape(rows, LANES).astype(BF16)
        gl = gamma[:, GDN_CHUNK - 1:GDN_CHUNK, :]
        qd_ref[...] = (tm_["q"] * eg).reshape(rows, LANES).astype(BF16)
        kd_ref[...] = (tm_["k"] * jnp.exp(gl - gamma)).reshape(rows, LANES).astype(BF16)
        aqk_ref[hh] = tm_["Aqk"].reshape(rows, GDN_CHUNK).astype(BF16)
        cd_ref[hh] = jnp.broadcast_to(jnp.exp(gl), (nb, 8, LANES)).reshape(nb * 8, LANES)

    def body(q_ref, k_ref, v_ref, gates_ref, u_ref, w_ref, qd_ref, kd_ref, aqk_ref, cd_ref):
        for hh in range(hs):
            win = lambda ref: _LaneWindow(ref, hh)
            head(pl.program_id(1) * hs + hh, hh, win(q_ref), win(k_ref), win(v_ref), gates_ref,
                 win(u_ref), win(w_ref), win(qd_ref), win(kd_ref), aqk_ref, cd_ref)

    blk = pl.BlockSpec((rows, hs * LANES), lambda i, h: (i, h))
    return pl.pallas_call(
        body, name="gdn_intra_fwd", grid=(t // rows, GDN_HEADS // hs),
        in_specs=[blk, blk, blk, pl.BlockSpec((rows, LANES), lambda i, h: (i, 0))],
        out_specs=[blk, blk, blk, blk,
                   pl.BlockSpec((hs, rows, GDN_CHUNK), lambda i, h: (h, i, 0)),
                   pl.BlockSpec((hs, nb * 8, LANES), lambda i, h: (h, i, 0))],
        out_shape=[jax.ShapeDtypeStruct((t, D_MODEL), F32),
                   jax.ShapeDtypeStruct((t, D_MODEL), BF16),
                   jax.ShapeDtypeStruct((t, D_MODEL), BF16),
                   jax.ShapeDtypeStruct((t, D_MODEL), BF16),
                   jax.ShapeDtypeStruct((GDN_HEADS, t, GDN_CHUNK), BF16),
                   jax.ShapeDtypeStruct((GDN_HEADS, nchunks * 8, LANES), F32)],
        compiler_params=_params(2),
    )(q, k, v, gates)


def _gdn_scan_fwd(u, w, qd, kd, aqk, cd):
    t = u.shape[0]
    rows = _row_tile(t, GDN_SCAN_ROWS)
    nb = rows // GDN_CHUNK
    nchunks = t // GDN_CHUNK
    hs = GDN_SCAN_HEADS

    def body(u_ref, w_ref, qd_ref, kd_ref, aqk_ref, cd_ref, o_ref, st_ref, s_ref):
        @pl.when(pl.program_id(1) == 0)
        def _():
            s_ref[...] = jnp.zeros_like(s_ref)

        states = [s_ref[hh] for hh in range(hs)]
        for c in range(nb):
            r = slice(c * GDN_CHUNK, (c + 1) * GDN_CHUNK)
            for hh in range(hs):
                ln = slice(hh * LANES, (hh + 1) * LANES)
                s = states[hh]
                st_ref[hh, c * LANES:(c + 1) * LANES, :] = s
                sb = s.astype(BF16)
                vb = (u_ref[r, ln] - _nn(w_ref[r, ln], sb)).astype(BF16)
                o_ref[r, ln] = _nn(qd_ref[r, ln], sb) + _nn(aqk_ref[hh, r, :], vb)
                states[hh] = s * cd_ref[hh, c * 8:c * 8 + 1, :] + _tn(kd_ref[r, ln], vb)
        for hh in range(hs):
            s_ref[hh] = states[hh]

    blk = pl.BlockSpec((rows, hs * LANES), lambda h, i: (i, h))
    return pl.pallas_call(
        body, name="gdn_scan_fwd", grid=(GDN_HEADS // hs, t // rows),
        in_specs=[blk, blk, blk, blk,
                  pl.BlockSpec((hs, rows, GDN_CHUNK), lambda h, i: (h, i, 0)),
                  pl.BlockSpec((hs, nb * 8, LANES), lambda h, i: (h, i, 0))],
        out_specs=[blk, pl.BlockSpec((hs, nb * LANES, LANES), lambda h, i: (h, i, 0))],
        out_shape=[jax.ShapeDtypeStruct((t, D_MODEL), F32),
                   jax.ShapeDtypeStruct((GDN_HEADS, nchunks * LANES, LANES), F32)],
        scratch_shapes=[pltpu.VMEM((hs, LANES, LANES), F32)],
        compiler_params=_params(2),
    )(u, w, qd, kd, aqk, cd)


def _gdn_out_fwd(o, z, nw, w_out, res):
    t = o.shape[0]
    tm = _row_tile(t, GDN_TM)

    def body(o_ref, z_ref, nw_ref, w_ref, r_ref, out_ref, gated_ref):
        nwv = nw_ref[...]
        for hh in range(GDN_HEADS):
            sl = slice(hh * LANES, (hh + 1) * LANES)
            _, on = _rms_stats(o_ref[:, sl])
            zv = z_ref[:, sl]
            gated_ref[:, sl] = (on * nwv * (zv * _sigmoid(zv))).astype(BF16)
        out_ref[...] = r_ref[...] + _nn(gated_ref[...], w_ref[...])

    row = pl.BlockSpec((tm, D_MODEL), lambda i: (i, 0))
    full = lambda a: pl.BlockSpec(a.shape, lambda i: (0, 0))
    return pl.pallas_call(
        body, name="gdn_out_fwd", grid=(t // tm,),
        in_specs=[row, row, full(nw), full(w_out), row],
        out_specs=[row, row],
        out_shape=[jax.ShapeDtypeStruct((t, D_MODEL), F32), jax.ShapeDtypeStruct((t, D_MODEL), BF16)],
        compiler_params=_params(1),
    )(o, z, nw, w_out, res)


def _gdn_out_bwd(dy, w_out, o, z, nw):
    t = o.shape[0]
    tm = _row_tile(t, GDN_TM)

    def body(dy_ref, w_ref, o_ref, z_ref, nw_ref, do_ref, dz_ref, dyb_ref, dnw_ref, dgt_ref):
        @pl.when(pl.program_id(0) == 0)
        def _():
            dnw_ref[...] = jnp.zeros_like(dnw_ref)

        dyb = dy_ref[...].astype(BF16)
        dyb_ref[...] = dyb
        dgt_ref[...] = _nt(dyb, w_ref[...])
        nwv = nw_ref[...]
        for hh in range(GDN_HEADS):
            sl = slice(hh * LANES, (hh + 1) * LANES)
            r, on = _rms_stats(o_ref[:, sl])
            zv = z_ref[:, sl]
            sig = _sigmoid(zv)
            sz = zv * sig
            dgt = dgt_ref[:, sl]
            d_on = dgt * nwv * sz
            dz_ref[:, sl] = (dgt * on * nwv * (sig * (1.0 + zv * (1.0 - sig)))).astype(BF16)
            dnw_ref[...] += jnp.sum(dgt * on * sz, axis=0, keepdims=True)
            do_ref[:, sl] = (r * (d_on - on * jnp.mean(d_on * on, axis=-1, keepdims=True))).astype(BF16)

    row = pl.BlockSpec((tm, D_MODEL), lambda i: (i, 0))
    full = lambda a: pl.BlockSpec(a.shape, lambda i: (0, 0))
    return pl.pallas_call(
        body, name="gdn_out_bwd", grid=(t // tm,),
        in_specs=[row, full(w_out), row, row, full(nw)],
        out_specs=[row, row, row, pl.BlockSpec((1, LANES), lambda i: (0, 0))],
        out_shape=[jax.ShapeDtypeStruct((t, D_MODEL), BF16)] * 3 + [jax.ShapeDtypeStruct((1, LANES), F32)],
        scratch_shapes=[pltpu.VMEM((tm, D_MODEL), F32)],
        compiler_params=_params(1),
    )(dy, w_out, o, z, nw)


def _gdn_scan_bwd(u, w, qd, kd, aqk, cd, st, do):
    t = u.shape[0]
    rows = _row_tile(t, GDN_SCAN_ROWS)
    nb = rows // GDN_CHUNK
    nchunks = t // GDN_CHUNK
    nsteps = t // rows
    cc = GDN_CHUNK
    hs = GDN_SCAN_HEADS

    def body(u_ref, w_ref, qd_ref, kd_ref, aqk_ref, cd_ref, st_ref, do_ref,
             du_ref, dw_ref, dqd_ref, dkd_ref, daqk_ref, daqkt_ref, dcd_ref, ds_ref):
        @pl.when(pl.program_id(1) == 0)
        def _():
            ds_ref[...] = jnp.zeros_like(ds_ref)

        ri = lax.broadcasted_iota(jnp.int32, (cc, cc), 0)
        ci = lax.broadcasted_iota(jnp.int32, (cc, cc), 1)
        dstates = [ds_ref[hh] for hh in range(hs)]
        for c in reversed(range(nb)):
            r = slice(c * cc, (c + 1) * cc)
            for hh in range(hs):
                ln = slice(hh * LANES, (hh + 1) * LANES)
                s = st_ref[hh, c * LANES:(c + 1) * LANES, :]
                sb = s.astype(BF16)
                dsn = dstates[hh]
                dsb = dsn.astype(BF16)
                wv, kdv, qdv, aq, dov = w_ref[r, ln], kd_ref[r, ln], qd_ref[r, ln], aqk_ref[hh, r, :], do_ref[r, ln]
                vb = (u_ref[r, ln] - _nn(wv, sb)).astype(BF16)
                dv = _tn(aq, dov) + _nn(kdv, dsb)
                dvb = dv.astype(BF16)
                daqk_ref[hh, r, :] = jnp.where(ri >= ci, _nt(dov, vb), 0.0)
                daqkt_ref[hh, r, :] = jnp.where(ci >= ri, _nt(vb, dov), 0.0)
                dqd_ref[r, ln] = _nt(dov, sb)
                dkd_ref[r, ln] = _nt(vb, dsb)
                dcd_ref[hh, c * 8:(c + 1) * 8, :] = jnp.broadcast_to(jnp.sum(s * dsn), (8, LANES))
                du_ref[r, ln] = dv
                dw_ref[r, ln] = -_nt(dvb, sb)
                dstates[hh] = _tn(qdv, dov) + dsn * cd_ref[hh, c * 8:c * 8 + 1, :] - _tn(wv, dvb)
        for hh in range(hs):
            ds_ref[hh] = dstates[hh]

    rev = lambda i: nsteps - 1 - i
    blk = pl.BlockSpec((rows, hs * LANES), lambda h, i: (rev(i), h))
    sq = pl.BlockSpec((hs, rows, cc), lambda h, i: (h, rev(i), 0))
    cdb = pl.BlockSpec((hs, nb * 8, LANES), lambda h, i: (h, rev(i), 0))
    return pl.pallas_call(
        body, name="gdn_scan_bwd", grid=(GDN_HEADS // hs, nsteps),
        in_specs=[blk, blk, blk, blk, sq, cdb,
                  pl.BlockSpec((hs, nb * LANES, LANES), lambda h, i: (h, rev(i), 0)), blk],
        out_specs=[blk, blk, blk, blk, sq, sq, cdb],
        out_shape=[jax.ShapeDtypeStruct((t, D_MODEL), F32)] * 4
        + [jax.ShapeDtypeStruct((GDN_HEADS, t, cc), F32)] * 2
        + [jax.ShapeDtypeStruct((GDN_HEADS, nchunks * 8, LANES), F32)],
        scratch_shapes=[pltpu.VMEM((hs, LANES, LANES), F32)],
        compiler_params=_params(2),
    )(u, w, qd, kd, aqk, cd, st, do)


def _gdn_intra_bwd(q, k, v, gates, u, w, du, dw, dqd, dkd, daqk, daqkt, dcd):
    t = q.shape[0]
    rows = _row_tile(t, GDN_ROWS)
    nb = rows // GDN_CHUNK
    cc = GDN_CHUNK
    hs = GDN_INTRA_HEADS

    def head(h, hh, q_ref, k_ref, v_ref, gates_ref, u_ref, w_ref, du_ref, dw_ref, dqd_ref, dkd_ref,
             daqk_ref, daqkt_ref, dcd_ref, dq_ref, dk_ref, dv_ref, dgates_ref):
        tm_ = _chunk_terms(q_ref, k_ref, gates_ref, h, True)
        gamma, beta, kk, qq, kb = tm_["gamma"], tm_["beta"], tm_["k"], tm_["q"], tm_["kb"]
        kbf, kbb, qb = tm_["kbf"], tm_["kbb"], tm_["qb"]
        lmat, lt = tm_["L"], tm_["LT"]
        ri, ci = tm_["ri"], tm_["ci"]
        r3 = lambda ref: ref[...].reshape(nb, cc, LANES)
        eg = jnp.exp(gamma)
        gl = gamma[:, cc - 1:cc, :]
        ekd = jnp.exp(gl - gamma)
        v3 = r3(v_ref)
        tt = _unit_inverse(tm_["AT"])
        dsol = jnp.concatenate([r3(du_ref), r3(dw_ref)], axis=-1)
        sol = jnp.concatenate([r3(u_ref), r3(w_ref).astype(F32)], axis=-1)
        dx = _bnn(tt.astype(BF16), dsol.astype(BF16))
        dxb, solb = dx.astype(BF16), sol.astype(BF16)
        da = jnp.where(ri > ci, -_bnt(dxb, solb), 0.0)
        dat = jnp.where(ci > ri, -_bnt(solb, dxb), 0.0)
        dxu, dxw = dx[..., :LANES], dx[..., LANES:]
        dv_ref[...] = (dxu * beta).reshape(rows, LANES)
        dbeta = jnp.sum(dxu * v3, axis=-1, keepdims=True)
        dkb = dxw * eg
        dgam = jnp.sum(dxw * kb * eg, axis=-1, keepdims=True)
        dkb = dkb + _bnn((da * lmat).astype(BF16), kbf)
        dk = _bnn((dat * lt).astype(BF16), kbb)
        dgam = dgam + jnp.sum(da * tm_["A"], axis=-1, keepdims=True) - jnp.sum(dat * tm_["AT"], axis=-1, keepdims=True)
        daq = daqk_ref[hh].reshape(nb, cc, cc)
        daqt = daqkt_ref[hh].reshape(nb, cc, cc)
        dq = _bnn((daq * lmat).astype(BF16), kbf)
        dk = dk + _bnn((daqt * lt).astype(BF16), qb)
        dgam = dgam + jnp.sum(daq * tm_["Aqk"], axis=-1, keepdims=True) - jnp.sum(daqt * tm_["AqkT"], axis=-1, keepdims=True)
        dqd3, dkd3 = r3(dqd_ref), r3(dkd_ref)
        dq = dq + dqd3 * eg
        dgam = dgam + jnp.sum(dqd3 * qq * eg, axis=-1, keepdims=True)
        dk = dk + dkd3 * ekd
        tk = jnp.sum(dkd3 * kk * ekd, axis=-1, keepdims=True)
        dgam = dgam - tk
        dcdv = dcd_ref[hh].reshape(nb, 8, LANES)[:, 0:1, 0:1]
        dglast = jnp.sum(tk, axis=1, keepdims=True) + dcdv * jnp.exp(gl[:, :, 0:1])
        rowi = lax.broadcasted_iota(jnp.int32, (nb, cc, 1), 1)
        dgam = dgam + jnp.where(rowi == cc - 1, dglast, 0.0)
        dk = dk + dkb * beta
        dbeta = dbeta + jnp.sum(dkb * kk, axis=-1, keepdims=True)
        dg = _chunk_cumsum(dgam, ci >= ri)
        dq_ref[...] = dq.reshape(rows, LANES)
        dk_ref[...] = dk.reshape(rows, LANES)
        lane = lax.broadcasted_iota(jnp.int32, (rows, LANES), 1)
        dgates_ref[...] += (jnp.where(lane == h, dbeta.reshape(rows, 1), 0.0)
                            + jnp.where(lane == GDN_HEADS + h, dg.reshape(rows, 1), 0.0))

    def body(q_ref, k_ref, v_ref, gates_ref, u_ref, w_ref, du_ref, dw_ref, dqd_ref, dkd_ref,
             daqk_ref, daqkt_ref, dcd_ref, dq_ref, dk_ref, dv_ref, dgates_ref):
        @pl.when(pl.program_id(1) == 0)
        def _():
            dgates_ref[...] = jnp.zeros_like(dgates_ref)

        for hh in range(hs):
            win = lambda ref: _LaneWindow(ref, hh)
            head(pl.program_id(1) * hs + hh, hh, win(q_ref), win(k_ref), win(v_ref), gates_ref, win(u_ref),
                 win(w_ref), win(du_ref), win(dw_ref), win(dqd_ref), win(dkd_ref), daqk_ref, daqkt_ref, dcd_ref,
                 win(dq_ref), win(dk_ref), win(dv_ref), dgates_ref)

    blk = pl.BlockSpec((rows, hs * LANES), lambda i, h: (i, h))
    shared = pl.BlockSpec((rows, LANES), lambda i, h: (i, 0))
    sq = pl.BlockSpec((hs, rows, cc), lambda i, h: (h, i, 0))
    return pl.pallas_call(
        body, name="gdn_intra_bwd", grid=(t // rows, GDN_HEADS // hs),
        in_specs=[blk, blk, blk, shared, blk, blk, blk, blk, blk, blk, sq, sq,
                  pl.BlockSpec((hs, nb * 8, LANES), lambda i, h: (h, i, 0))],
        out_specs=[blk, blk, blk, shared],
        out_shape=[jax.ShapeDtypeStruct((t, D_MODEL), F32)] * 3 + [jax.ShapeDtypeStruct((t, LANES), F32)],
        compiler_params=_params(2),
    )(q, k, v, gates, u, w, du, dw, dqd, dkd, daqk, daqkt, dcd)


def _gdn_conv_bwd_a(x, cw, ba, gp, dq, dk, dv, dgates):
    t = x.shape[0]
    tm = _row_tile(t, GDN_CONV_TM)

    def body(x_ref, halo_ref, cw_ref, ba_ref, gp_ref, dq_ref, dk_ref, dv_ref, dgates_ref,
             dc_ref, dba_ref, dcw_ref, dgp_ref, ds_ref):
        @pl.when(pl.program_id(0) == 0)
        def _():
            dcw_ref[...] = jnp.zeros_like(dcw_ref)
            dgp_ref[...] = jnp.zeros_like(dgp_ref)

        halo = jnp.where(pl.program_id(0) > 0, halo_ref[...], 0.0)
        c, taps = _conv_taps(x_ref[...], halo, cw_ref[...])
        sig = _sigmoid(c)
        s = c * sig
        for hh in range(GDN_HEADS):
            sl = slice(hh * LANES, (hh + 1) * LANES)
            sq = s[:, sl]
            rq = lax.rsqrt(jnp.sum(sq * sq, axis=-1, keepdims=True) + NORM_EPS)
            qh = sq * rq
            dqv = dq_ref[:, sl]
            ds_ref[:, sl] = GDN_QSCALE * rq * (dqv - qh * jnp.sum(dqv * qh, axis=-1, keepdims=True))
            sl2 = slice(D_MODEL + hh * LANES, D_MODEL + (hh + 1) * LANES)
            sk = s[:, sl2]
            rk = lax.rsqrt(jnp.sum(sk * sk, axis=-1, keepdims=True) + NORM_EPS)
            kh = sk * rk
            dkv = dk_ref[:, sl]
            ds_ref[:, sl2] = rk * (dkv - kh * jnp.sum(dkv * kh, axis=-1, keepdims=True))
        ds_ref[:, 2 * D_MODEL:] = dv_ref[...]
        dc = ds_ref[...] * (sig * (1.0 + c * (1.0 - sig)))
        dc_ref[...] = dc
        for i in range(GDN_CONV):
            dcw_ref[i:i + 1, :] += jnp.sum(dc * taps[i], axis=0, keepdims=True)
        lane, beta, pre, g, _ = _gates(ba_ref[...], gp_ref[...])
        dgt = dgates_ref[...]
        db = dgt * beta * (1.0 - beta)
        dpre = dgt * (-jnp.exp(gp_ref[0:1, :])) * _sigmoid(pre)
        isa = jnp.logical_and(lane >= GDN_HEADS, lane < 2 * GDN_HEADS)
        dba_ref[...] = jnp.where(lane < GDN_HEADS, db, jnp.where(isa, dpre, 0.0)).astype(BF16)
        dgp_ref[0:1, :] += jnp.sum(jnp.where(isa, dgt * g, 0.0), axis=0, keepdims=True)
        dgp_ref[1:2, :] += jnp.sum(jnp.where(isa, dpre, 0.0), axis=0, keepdims=True)

    row = lambda w: pl.BlockSpec((tm, w), lambda i: (i, 0))
    full = lambda a: pl.BlockSpec(a.shape, lambda i: (0, 0))
    halo = pl.BlockSpec((8, GDN_CONV_W), lambda i: (jnp.maximum(i * (tm // 8) - 1, 0), 0))
    return pl.pallas_call(
        body, name="gdn_conv_bwd_a", grid=(t // tm,),
        in_specs=[row(GDN_CONV_W), halo, full(cw), row(LANES), full(gp), row(D_MODEL), row(D_MODEL),
                  row(D_MODEL), row(LANES)],
        out_specs=[row(GDN_CONV_W), row(LANES), pl.BlockSpec((8, GDN_CONV_W), lambda i: (0, 0)),
                   pl.BlockSpec((8, LANES), lambda i: (0, 0))],
        out_shape=[jax.ShapeDtypeStruct((t, GDN_CONV_W), F32),
                   jax.ShapeDtypeStruct((t, LANES), BF16),
                   jax.ShapeDtypeStruct((8, GDN_CONV_W), F32),
                   jax.ShapeDtypeStruct((8, LANES), F32)],
        scratch_shapes=[pltpu.VMEM((tm, GDN_CONV_W), F32)],
        compiler_params=_params(1),
    )(x, x, cw, ba, gp, dq, dk, dv, dgates)


def _gdn_conv_bwd_b(dc, cw):
    t = dc.shape[0]
    tm = _row_tile(t, GDN_CONV_TM)
    nsteps = t // tm

    def body(dc_ref, halo_ref, cw_ref, dx_ref):
        halo = jnp.where(pl.program_id(0) < nsteps - 1, halo_ref[...], 0.0)
        dd = jnp.concatenate([dc_ref[...], halo], axis=0)
        cw_v = cw_ref[...]
        acc = dd[GDN_CONV - 1:GDN_CONV - 1 + tm] * cw_v[0:1]
        for i in range(1, GDN_CONV):
            acc = acc + dd[GDN_CONV - 1 - i:GDN_CONV - 1 - i + tm] * cw_v[i:i + 1]
        dx_ref[...] = acc.astype(BF16)

    row = pl.BlockSpec((tm, GDN_CONV_W), lambda i: (i, 0))
    halo = pl.BlockSpec((8, GDN_CONV_W), lambda i: (jnp.minimum((i + 1) * (tm // 8), t // 8 - 1), 0))
    return pl.pallas_call(
        body, name="gdn_conv_bwd_b", grid=(nsteps,),
        in_specs=[row, halo, pl.BlockSpec(cw.shape, lambda i: (0, 0))],
        out_specs=row,
        out_shape=jax.ShapeDtypeStruct((t, GDN_CONV_W), BF16),
        compiler_params=_params(1),
    )(dc, dc, cw)


def _gdn_proj_bwd(dy, h, nw, w_qkv, w_z, w_ba, dx, dz, dba):
    t = h.shape[0]
    tm = _row_tile(t, GDN_TM)

    def body(dy_ref, h_ref, nw_ref, wq_ref, wz_ref, wb_ref, dx_ref, dz_ref, dba_ref, dh_ref, dnw_ref):
        @pl.when(pl.program_id(0) == 0)
        def _():
            dnw_ref[...] = jnp.zeros_like(dnw_ref)

        dhn = _nt(dx_ref[...], wq_ref[...]) + _nt(dz_ref[...], wz_ref[...]) + _nt(dba_ref[...], wb_ref[...])
        dxx, dnw = _norm_bwd(h_ref[...], nw_ref[...], dhn)
        dh_ref[...] = dy_ref[...] + dxx
        dnw_ref[...] += dnw

    row = lambda w: pl.BlockSpec((tm, w), lambda i: (i, 0))
    full = lambda a: pl.BlockSpec(a.shape, lambda i: (0, 0))
    return pl.pallas_call(
        body, name="gdn_proj_bwd", grid=(t // tm,),
        in_specs=[row(D_MODEL), row(D_MODEL), full(nw), full(w_qkv), full(w_z), full(w_ba),
                  row(GDN_CONV_W), row(D_MODEL), row(LANES)],
        out_specs=[row(D_MODEL), pl.BlockSpec((1, D_MODEL), lambda i: (0, 0))],
        out_shape=[jax.ShapeDtypeStruct((t, D_MODEL), F32), jax.ShapeDtypeStruct((1, D_MODEL), F32)],
        compiler_params=_params(1),
    )(dy, h, nw, w_qkv, w_z, w_ba, dx, dz, dba)


def _loss_head(h, nw, target):
    t = h.shape[0]
    tm = _row_tile(t, 512)

    def body(h_ref, nw_ref, t_ref, loss_ref, dh_ref, dnw_ref):
        @pl.when(pl.program_id(0) == 0)
        def _():
            loss_ref[...] = jnp.zeros_like(loss_ref)
            dnw_ref[...] = jnp.zeros_like(dnw_ref)

        x = h_ref[...]
        nwv = nw_ref[...]
        _, xhat = _rms_stats(x)
        err = xhat * nwv - t_ref[...]
        loss_ref[...] += 0.5 * jnp.sum(jnp.mean(err * err, axis=-1, keepdims=True))
        dx, dnw = _norm_bwd(x, nwv, err * (1.0 / D_MODEL))
        dh_ref[...] = dx
        dnw_ref[...] += dnw

    row = pl.BlockSpec((tm, D_MODEL), lambda i: (i, 0))
    return pl.pallas_call(
        body, name="loss_head", grid=(t // tm,),
        in_specs=[row, pl.BlockSpec((1, D_MODEL), lambda i: (0, 0)), row],
        out_specs=[pl.BlockSpec((8, LANES), lambda i: (0, 0)), row, pl.BlockSpec((1, D_MODEL), lambda i: (0, 0))],
        out_shape=[jax.ShapeDtypeStruct((8, LANES), F32),
                   jax.ShapeDtypeStruct((t, D_MODEL), F32),
                   jax.ShapeDtypeStruct((1, D_MODEL), F32)],
        compiler_params=_params(1),
    )(h, nw, target)


_PEER_FLIPS = [(dx, dy, dc) for dx in (0, 1) for dy in (0, 1) for dc in (0, 1)][1:]


_ANY = pl.BlockSpec(memory_space=pl.ANY)


def _exchange_copies(ins, outs, send_sems, recv_sems, local_sems, scatter):
    x, y, c = lax.axis_index("x"), lax.axis_index("y"), lax.axis_index("c")
    me = 4 * x + 2 * y + c
    copies = []
    for a in range(len(ins)):
        src = ins[a].at[me] if scatter else ins[a]
        copies.append(pltpu.make_async_copy(src, outs[a].at[me], local_sems.at[a]))
    for k, (fx, fy, fc) in enumerate(_PEER_FLIPS):
        px, py, pc = lax.rem(x + fx, 2), lax.rem(y + fy, 2), lax.rem(c + fc, 2)
        peer = 4 * px + 2 * py + pc
        for a in range(len(ins)):
            copies.append(pltpu.make_async_remote_copy(
                src_ref=ins[a].at[peer] if scatter else ins[a],
                dst_ref=outs[a].at[me],
                send_sem=send_sems.at[a, k], recv_sem=recv_sems.at[a, k],
                device_id=(px, py, pc), device_id_type=MESH_ID))
    return copies


def _exchange_shapes(arrs, scatter):
    return [jax.ShapeDtypeStruct((N_DEV,) + (a.shape[1:] if scatter else a.shape), a.dtype) for a in arrs]


def _exchange_sems(n):
    npeer = len(_PEER_FLIPS)
    return [pltpu.SemaphoreType.DMA((n, npeer)), pltpu.SemaphoreType.DMA((n, npeer)),
            pltpu.SemaphoreType.DMA((n,))]


def _gather_two_level(arrs, name):
    n = len(arrs)

    def body(*refs):
        ins, outs = refs[:n], refs[n:2 * n]
        send_sems, recv_sems, local_sems = refs[2 * n:]
        x, y, c = lax.axis_index("x"), lax.axis_index("y"), lax.axis_index("c")
        slot = lambda px, py, pc: 4 * px + 2 * py + pc
        me, sib = slot(x, y, c), (x, y, 1 - c)
        chips = [(1 - x, y), (x, 1 - y), (1 - x, 1 - y)]

        def copy(a, k, block, to, src=None):
            dst = outs[a].at[block]
            return pltpu.make_async_remote_copy(
                src_ref=dst if src is None else src, dst_ref=dst, send_sem=send_sems.at[a, k],
                recv_sem=recv_sems.at[a, k], device_id=to, device_id_type=MESH_ID)

        sends, own = [], []
        for a in range(n):
            local = pltpu.make_async_copy(ins[a], outs[a].at[me], local_sems.at[a])
            local.start()
            own.append(local)
            first = [copy(a, 0, me, sib, ins[a])]
            first += [copy(a, 1 + j, me, (*chip, c), ins[a]) for j, chip in enumerate(chips)]
            for cp in first:
                cp.start()
            sends += first
        for j, chip in enumerate(chips):
            for a in range(n):
                copy(a, 1 + j, slot(*chip, c), (x, y, c)).wait_recv()
                passed = copy(a, 4 + j, slot(*chip, c), sib)
                passed.start()
                sends.append(passed)
        for a in range(n):
            copy(a, 0, slot(x, y, 1 - c), (x, y, c)).wait_recv()
            for j, chip in enumerate(chips):
                copy(a, 4 + j, slot(*chip, 1 - c), (x, y, c)).wait_recv()
        for cp in sends:
            cp.wait_send()
        for cp in own:
            cp.wait()

    return pl.pallas_call(
        body, name=name, in_specs=[_ANY] * n, out_specs=[_ANY] * n,
        out_shape=_exchange_shapes(arrs, False), scratch_shapes=_exchange_sems(n),
    )(*arrs)


def _exchange(arrs, scatter, name):
    n = len(arrs)

    def body(*refs):
        copies = _exchange_copies(refs[:n], refs[n:2 * n], *refs[2 * n:], scatter)
        for cp in copies:
            cp.start()
        for cp in copies:
            cp.wait()

    return pl.pallas_call(
        body, name=name, in_specs=[_ANY] * n, out_specs=[_ANY] * n,
        out_shape=_exchange_shapes(arrs, scatter), scratch_shapes=_exchange_sems(n),
    )(*arrs)


ADAMW_PART_BLOCK_BYTES = 2 * 1024 * 1024


def _adamw(parts, w, m, v, name):
    nl, r, c = w.shape
    cap = max(8, ADAMW_PART_BLOCK_BYTES // (N_DEV * c * parts[0].dtype.itemsize))
    tr = r
    for cand in range(min(r, cap) // 8 * 8, 7, -8):
        if r % cand == 0:
            tr = cand
            break
    nr = r // tr
    c1 = 1.0 - ADAM_B1 ** ADAM_STEP
    c2 = 1.0 - ADAM_B2 ** ADAM_STEP

    def body(*refs):
        p_refs = refs[:nl]
        w_ref, m_ref, v_ref, g_ref, d_ref, nm_ref, nv_ref = refs[nl:]
        layer = pl.program_id(0)
        for k in range(nl):
            @pl.when(layer == k)
            def _():
                g = p_refs[k][0].astype(F32)
                for s in range(1, N_DEV):
                    g = g + p_refs[k][s].astype(F32)
                nm = ADAM_B1 * m_ref[0] + (1.0 - ADAM_B1) * g
                nv = ADAM_B2 * v_ref[0] + (1.0 - ADAM_B2) * (g * g)
                g_ref[0] = g
                nm_ref[0] = nm
                nv_ref[0] = nv
                d_ref[0] = -ADAM_LR * ((nm / c1) / (jnp.sqrt(nv / c2) + ADAM_EPS) + ADAM_WD * w_ref[0])

    def part_spec(k):
        return pl.BlockSpec((N_DEV, tr, c),
                            lambda l, i: (0, jnp.where(l == k, i, jnp.where(l < k, 0, nr - 1)), 0))

    blk = pl.BlockSpec((1, tr, c), lambda l, i: (l, i, 0))
    return pl.pallas_call(
        body, name=name, grid=(nl, nr),
        in_specs=[part_spec(k) for k in range(nl)] + [blk, blk, blk],
        out_specs=[blk] * 4,
        out_shape=[jax.ShapeDtypeStruct((nl, r, c), F32)] * 4,
        compiler_params=_params(2),
    )(*parts, w, m, v)


def _rope_tables(positions):
    half = ROPE_DIM // 2
    inv_freq = ROPE_THETA ** (-jnp.arange(0, ROPE_DIM, 2, dtype=F32) / ROPE_DIM)
    ang = positions.astype(F32)[:, None] * inv_freq
    cos, sin = jnp.cos(ang), jnp.sin(ang)
    t = positions.shape[0]
    zeros = lambda w: jnp.zeros((t, w), F32)
    c = jnp.concatenate([cos, cos, jnp.ones((t, ATTN_HEAD_DIM - ROPE_DIM), F32)], axis=1)
    s1 = jnp.concatenate([-sin, zeros(ATTN_HEAD_DIM - half)], axis=1)
    s2 = jnp.concatenate([zeros(half), sin, zeros(ATTN_HEAD_DIM - ROPE_DIM)], axis=1)
    return tuple(jnp.tile(a, (1, LANES // ATTN_HEAD_DIM)) for a in (c, s1, s2))


def _ffn_layer_fwd(h, nw, w, carry):
    (out, g, u, hn), landed = _ffn_fwd(h, nw, w["wg"], w["wu"], w["wd"], carry, False)
    return out, (h, g, u, hn), landed


def _ffn_layer_bwd(dy, saved, nw, w, carry, early=None):
    h, g, u, hn = saved
    (dh, dg, du, act, dz, dnw), landed = _ffn_bwd(dy, h, nw, g, u, w["wg"], w["wu"], w["wd"], carry, True)
    dwg = _matmul_tn(hn, dg, FFN_TF, "ffn_dwg")
    dwu = _matmul_tn(hn, du, FFN_TF, "ffn_dwu")
    if early is None:
        dwd, early_landed = _matmul_tn(act, dz, D_MODEL, "ffn_dwd"), ()
    else:
        dwd, early_landed = _matmul_tn(act, dz, D_MODEL, "ffn_dwd", early(dwg, dwu))
    return dh, dnw, (dwg, dwu, dwd), landed, early_landed


def _attn_layer_fwd(h, nw, w, ropes):
    q, k, v, hn = _attn_qkv_fwd(h, nw, w["w_in"], w["b_in"], *ropes)
    o = _attn_core_fwd(q, k, v, w["sinks"])
    out = _proj_out_fwd(o, w["w_out"], w["b_out"], h)
    return out, (h, hn, q, k, v, o)


def _attn_layer_bwd(dy, saved, nw, w, ropes):
    h, hn, q, k, v, o = saved
    do, dyb, db_out = _proj_out_bwd(dy, w["w_out"])
    dw_out = _matmul_tn(o, dyb, D_MODEL, "attn_dw_out")
    dq, dk, dv, dsink = _attn_core_bwd(q, k, v, do, w["sinks"])
    dh, dqkv, db_in, dnw = _attn_qkv_bwd(dy, h, nw, w["w_in"], dq, dk, dv, *ropes)
    dw_in = _matmul_tn(hn, dqkv, ATTN_IN, "attn_dw_in")
    return dh, dnw, dict(w_in=dw_in, b_in=db_in, sinks=dsink[:, 0], w_out=dw_out, b_out=db_out)


def _gdn_layer_fwd(h, nw, w):
    x, z, ba, hn = _gdn_proj_fwd(h, nw, w["w_qkv"], w["w_z"], w["w_ba"])
    q, k, v, gates = _gdn_conv_fwd(x, w["conv_w"], ba, w["gp"])
    u, ww, qd, kd, aqk, cd = _gdn_intra_fwd(q, k, v, gates)
    o, st = _gdn_scan_fwd(u, ww, qd, kd, aqk, cd)
    out, gated = _gdn_out_fwd(o, z, w["norm_w"], w["w_out"], h)
    return out, (h, hn, x, z, ba, q, k, v, gates, u, ww, qd, kd, aqk, cd, st, o, gated)


def _gdn_layer_bwd(dy, saved, nw, w):
    h, hn, x, z, ba, q, k, v, gates, u, ww, qd, kd, aqk, cd, st, o, gated = saved
    do, dz, dyb, dnorm_w = _gdn_out_bwd(dy, w["w_out"], o, z, w["norm_w"])
    dw_out = _matmul_tn(gated, dyb, D_MODEL, "gdn_dw_out")
    du, dw, dqd, dkd, daqk, daqkt, dcd = _gdn_scan_bwd(u, ww, qd, kd, aqk, cd, st, do)
    dq, dk, dv, dgates = _gdn_intra_bwd(q, k, v, gates, u, ww, du, dw, dqd, dkd, daqk, daqkt, dcd)
    dc, dba, dcw, dgp = _gdn_conv_bwd_a(x, w["conv_w"], ba, w["gp"], dq, dk, dv, dgates)
    dx = _gdn_conv_bwd_b(dc, w["conv_w"])
    dh, dnw = _gdn_proj_bwd(dy, h, nw, w["w_qkv"], w["w_z"], w["w_ba"], dx, dz, dba)
    dw_qkv = _matmul_tn(hn, dx, GDN_CONV_W // 2, "gdn_dw_qkv")
    dw_z = _matmul_tn(hn, dz, D_MODEL, "gdn_dw_z")
    dw_ba = _matmul_tn(hn, dba, LANES, "gdn_dw_ba")
    dw_in = jnp.concatenate([dw_qkv, dw_z, dw_ba[:, :2 * GDN_HEADS]], axis=1)
    lo, hi = GDN_HEADS, 2 * GDN_HEADS
    return dh, dnw, dict(w_in=dw_in, conv_w=dcw[:GDN_CONV], A_log=dgp[0, lo:hi], dt_bias=dgp[1, lo:hi],
                         norm_w=dnorm_w[0], w_out=dw_out)


def _local_step(x, positions, target, norms, final_norm, plan):
    ropes = _rope_tables(positions)
    h = x
    saved = []
    for layer in range(DEPTH):
        h, s1, landed = _ffn_layer_fwd(h, norms["ffn1"][layer], plan.weights("ffn1", layer),
                                       plan.fwd_carry("ffn1", layer))
        plan.fwd_landed(landed)
        if layer % 2 == 0:
            h, s2 = _attn_layer_fwd(h, norms["mix"][layer], plan.weights("mix", layer), ropes)
        else:
            h, s2 = _gdn_layer_fwd(h, norms["mix"][layer], plan.weights("mix", layer))
        h, s3, landed = _ffn_layer_fwd(h, norms["ffn2"][layer], plan.weights("ffn2", layer),
                                       plan.fwd_carry("ffn2", layer))
        plan.fwd_landed(landed)
        saved.append((s1, s2, s3))
    loss, dh, dfinal = _loss_head(h, final_norm, target)

    g_norm = {k: [None] * DEPTH for k in ("ffn1", "mix", "ffn2")}
    for layer in reversed(range(DEPTH)):
        s1, s2, s3 = saved[layer]
        dh, g_norm["ffn2"][layer], gw, landed, _ = _ffn_layer_bwd(
            dh, s3, norms["ffn2"][layer], plan.weights("ffn2", layer), plan.bwd_carry("ffn2", layer))
        plan.bwd_landed(landed)
        plan.grads("ffn2", layer, gw)
        if layer % 2 == 0:
            dh, g_norm["mix"][layer], gw = _attn_layer_bwd(dh, s2, norms["mix"][layer], plan.weights("mix", layer),
                                                           ropes)
        else:
            dh, g_norm["mix"][layer], gw = _gdn_layer_bwd(dh, s2, norms["mix"][layer], plan.weights("mix", layer))
        plan.grads("mix", layer, gw)
        dh, g_norm["ffn1"][layer], gw, landed, early_landed = _ffn_layer_bwd(
            dh, s1, norms["ffn1"][layer], plan.weights("ffn1", layer), plan.bwd_carry("ffn1", layer),
            plan.early_parts("ffn1", layer))
        plan.bwd_landed(landed)
        plan.early_landed("ffn1", layer, early_landed)
        plan.grads("ffn1", layer, gw)
    return loss, dh, dfinal, g_norm


def _cols_full(g):
    return jnp.transpose(g, (1, 0, 2)).reshape(g.shape[1], -1)


def _rows_full(g):
    return g.reshape(-1, g.shape[-1])


class _ShardedWeights:
    def __init__(self, shards, small):
        self.shards, self.small = shards, small
        self.whole, self.pending, self.received, self.small_grads, self.early = {}, {}, {}, {}, {}
        self.in_flight = []
        first = [("ffn1", 0)]
        self.in_flight = first
        self.fwd_landed(_gather_two_level(self._shards_of(first), "gather_first"))

    def _group(self, key):
        kind, layer = key
        j = layer // 2
        if kind != "mix":
            return [self.shards[kind + "_w_gate_up"][layer], self.shards[kind + "_w_down"][layer]]
        if layer % 2 == 0:
            return [self.shards["attn_w_in"][j], self.shards["attn_w_out"][j]]
        return [self.shards["gdn_w_in"][j], self.shards["gdn_w_out"][j], self.shards["gdn_conv_w"][j]]

    def _shards_of(self, keys):
        return [a for key in keys for a in self._group(key)]

    def weights(self, kind, layer):
        return self.whole[(kind, layer)]

    def fwd_carry(self, kind, layer):
        if kind == "ffn1":
            keys = [("mix", 0), ("ffn2", 0)] if layer == 0 else [("ffn2", layer)]
        else:
            keys = [("ffn1", layer + 1), ("mix", layer + 1)] if layer + 1 < DEPTH else []
        self.in_flight = keys
        return self._shards_of(keys)

    def fwd_landed(self, landed):
        landed = list(landed)
        for key in self.in_flight:
            kind, layer = key
            j = layer // 2
            sm = self.small
            if kind != "mix":
                gu, dn = landed[:2]
                w = dict(wg=_cols_full(gu[:4]), wu=_cols_full(gu[4:]), wd=_rows_full(dn))
                landed = landed[2:]
            elif layer % 2 == 0:
                w = dict(w_in=_cols_full(landed[0]), w_out=_rows_full(landed[1]), b_in=sm["attn_b_in"][j][None],
                         sinks=sm["attn_sinks"][j][None], b_out=sm["attn_b_out"][j][None])
                landed = landed[2:]
            else:
                w_in = _cols_full(landed[0])
                w_ba = jnp.pad(w_in[:, GDN_CONV_W + D_MODEL:], ((0, 0), (0, LANES - 2 * GDN_HEADS)))
                gp = jnp.pad(jnp.stack([sm["gdn_A_log"][j], sm["gdn_dt_bias"][j]]),
                             ((0, 6), (GDN_HEADS, LANES - 2 * GDN_HEADS)))
                w = dict(w_qkv=w_in[:, :GDN_CONV_W], w_z=w_in[:, GDN_CONV_W:GDN_CONV_W + D_MODEL], w_ba=w_ba,
                         conv_w=_cols_full(landed[2]), gp=gp, norm_w=sm["gdn_norm_w"][j][None],
                         w_out=_rows_full(landed[1]))
                landed = landed[3:]
            self.whole[key] = w
        self.in_flight = []

    @staticmethod
    def _gate_up_part(dwg, dwu):
        return jnp.concatenate([_cols_shards(dwg, 4), _cols_shards(dwu, 4)], axis=0).astype(BF16)

    def early_parts(self, kind, layer):
        if (kind, layer) != ("ffn1", 0):
            return None
        return lambda dwg, dwu: [self._gate_up_part(dwg, dwu)]

    def early_landed(self, kind, layer, landed):
        if len(landed):
            self.early[(kind, layer)] = list(landed)

    def grads(self, kind, layer, g):
        if kind != "mix":
            dwg, dwu, dwd = g
            parts = [_rows_shards(dwd).astype(BF16)]
            if (kind, layer) not in self.early:
                parts = [self._gate_up_part(dwg, dwu)] + parts
        else:
            parts = [_cols_shards(g["w_in"]).astype(BF16), _rows_shards(g["w_out"]).astype(BF16)]
            if layer % 2 == 1:
                parts.append(_cols_shards(g["conv_w"]))
            self.small_grads[layer] = g
        self.pending[(kind, layer)] = parts

    def bwd_carry(self, kind, layer):
        if kind == "ffn1":
            keys = [("ffn2", layer), ("mix", layer)]
        else:
            keys = [("ffn1", layer + 1)] if layer + 1 < DEPTH else []
        self.in_flight = keys
        return [a for key in keys for a in self.pending[key]]

    def bwd_landed(self, landed):
        landed = list(landed)
        for key in self.in_flight:
            n = len(self.pending.pop(key))
            self.received[key], landed = landed[:n], landed[n:]
        self.in_flight = []

    def finish(self):
        self.in_flight = list(self.pending)
        self.bwd_landed(_exchange([a for key in self.in_flight for a in self.pending[key]], True, "scatter_last"))
        for key, landed in self.early.items():
            self.received[key] = landed + self.received[key]


def _cols_shards(full, n=N_DEV):
    r = full.shape[0]
    return jnp.transpose(full.reshape(r, n, -1), (1, 0, 2))


def _rows_shards(full):
    return full.reshape(N_DEV, -1, full.shape[-1])


_SMALL = ("ffn1_norm", "mix_norm", "ffn2_norm", "attn_b_in", "attn_sinks", "attn_b_out",
          "gdn_A_log", "gdn_dt_bias", "gdn_norm_w", "final_norm")
_BIG = ("ffn1_w_gate_up", "ffn1_w_down", "ffn2_w_gate_up", "ffn2_w_down", "attn_w_in", "attn_w_out",
        "gdn_w_in", "gdn_conv_w", "gdn_w_out")
_WEIGHTS = ("ffn1_norm", "ffn1_w_gate_up", "ffn1_w_down", "mix_norm", "ffn2_norm", "ffn2_w_gate_up",
            "ffn2_w_down", "attn_w_in", "attn_b_in", "attn_sinks", "attn_w_out", "attn_b_out", "gdn_w_in",
            "gdn_conv_w", "gdn_A_log", "gdn_dt_bias", "gdn_norm_w", "gdn_w_out", "final_norm")


def _pack_small(vals):
    flat = jnp.concatenate([v.reshape(-1).astype(F32) for v in vals])
    pad = (-flat.shape[0]) % (8 * LANES)
    return jnp.pad(flat, (0, pad)).reshape(-1, LANES)


def _unpack_small(packed, shapes):
    flat = packed.reshape(-1)
    out, off = [], 0
    for s in shapes:
        size = 1
        for d in s:
            size *= d
        out.append(flat[off:off + size].reshape(s))
        off += size
    return out


def kernel(x, positions, ffn1_norm, ffn1_w_gate_up, ffn1_w_down, mix_norm, ffn2_norm, ffn2_w_gate_up, ffn2_w_down, attn_w_in, attn_b_in, attn_sinks, attn_w_out, attn_b_out, gdn_w_in, gdn_conv_w, gdn_A_log, gdn_dt_bias, gdn_norm_w, gdn_w_out, final_norm, loss_target, m_ffn1_norm, m_ffn1_w_gate_up, m_ffn1_w_down, m_mix_norm, m_ffn2_norm, m_ffn2_w_gate_up, m_ffn2_w_down, m_attn_w_in, m_attn_b_in, m_attn_sinks, m_attn_w_out, m_attn_b_out, m_gdn_w_in, m_gdn_conv_w, m_gdn_A_log, m_gdn_dt_bias, m_gdn_norm_w, m_gdn_w_out, m_final_norm, v_ffn1_norm, v_ffn1_w_gate_up, v_ffn1_w_down, v_mix_norm, v_ffn2_norm, v_ffn2_w_gate_up, v_ffn2_w_down, v_attn_w_in, v_attn_b_in, v_attn_sinks, v_attn_w_out, v_attn_b_out, v_gdn_w_in, v_gdn_conv_w, v_gdn_A_log, v_gdn_dt_bias, v_gdn_norm_w, v_gdn_w_out, v_final_norm):
    args = dict(locals())
    wts = {n: args[n] for n in _WEIGHTS}
    moms = {n: args["m_" + n] for n in _WEIGHTS}
    vels = {n: args["v_" + n] for n in _WEIGHTS}

    shards = {n: wts[n] if n == "gdn_conv_w" else wts[n].astype(BF16) for n in _BIG}
    plan = _ShardedWeights(shards, wts)
    norms = dict(ffn1=ffn1_norm[:, None, :], mix=mix_norm[:, None, :], ffn2=ffn2_norm[:, None, :])
    loss, grad_x, dfinal, g_norm = _local_step(x[0], positions[0], loss_target[0], norms, final_norm[None], plan)
    plan.finish()

    layers = range(DEPTH)
    got = lambda kind, ls, i: [plan.received[(kind, l)][i] for l in ls]
    received = {"attn_w_in": got("mix", layers[0::2], 0), "attn_w_out": got("mix", layers[0::2], 1),
                "gdn_w_in": got("mix", layers[1::2], 0), "gdn_w_out": got("mix", layers[1::2], 1),
                "gdn_conv_w": got("mix", layers[1::2], 2)}
    for kind in ("ffn1", "ffn2"):
        received[kind + "_w_gate_up"] = got(kind, layers, 0)
        received[kind + "_w_down"] = got(kind, layers, 1)
    g_attn = [plan.small_grads[l] for l in layers[0::2]]
    g_gdn = [plan.small_grads[l] for l in layers[1::2]]


    small_g = dict(
        ffn1_norm=jnp.concatenate(g_norm["ffn1"], axis=0), mix_norm=jnp.concatenate(g_norm["mix"], axis=0),
        ffn2_norm=jnp.concatenate(g_norm["ffn2"], axis=0),
        attn_b_in=jnp.concatenate([g["b_in"] for g in g_attn], axis=0),
        attn_sinks=jnp.stack([g["sinks"] for g in g_attn]),
        attn_b_out=jnp.concatenate([g["b_out"] for g in g_attn], axis=0),
        gdn_A_log=jnp.stack([g["A_log"] for g in g_gdn]), gdn_dt_bias=jnp.stack([g["dt_bias"] for g in g_gdn]),
        gdn_norm_w=jnp.stack([g["norm_w"] for g in g_gdn]), final_norm=dfinal[0])
    small_parts = _exchange([_pack_small([small_g[n] for n in _SMALL] + [loss[0, :1]])], False, "gather_small")[0]
    pad1 = jnp.zeros((1,), F32)
    sw = _pack_small([wts[n] for n in _SMALL] + [pad1])
    sm = _pack_small([moms[n] for n in _SMALL] + [pad1])
    sv = _pack_small([vels[n] for n in _SMALL] + [pad1])
    shapes = [wts[n].shape for n in _SMALL] + [(1,)]
    small_out = [_unpack_small(o[0], shapes)
                 for o in _adamw([small_parts], sw[None], sm[None], sv[None], "adamw_small")]
    results = {n: tuple(o[i] for o in small_out) for i, n in enumerate(_SMALL)}
    loss_total = small_out[0][-1][0]

    for n in _BIG:
        results[n] = tuple(_adamw(received[n], wts[n], moms[n], vels[n], "adamw_" + n))

    return (loss_total, grad_x[None],
            *[results[n][0] for n in _WEIGHTS], *[results[n][1] for n in _WEIGHTS],
            *[results[n][2] for n in _WEIGHTS], *[results[n][3] for n in _WEIGHTS])
```
